```python
import math
import jax
import jax.numpy as jnp
from jax import lax
import numpy as np

D_MODEL = 1024
BATCH = 8
SEQ = 4096
DEPTH = 1
DEC_BATCH = 128
DEC_SEQ = 1
PAST_LEN = 16384
PAGE_SIZE = 128

N_META = 16
M_HEADS = 4
M_DK = D_MODEL // (2 * M_HEADS)
M_DV = D_MODEL // (2 * M_HEADS)
M_WIDTH = M_HEADS * M_DV
M_CHUNK = 64
A_HD = 64
A_HEADS = D_MODEL // (2 * A_HD)
A_KV_HEADS = 2
A_GROUP = A_HEADS // A_KV_HEADS
A_WIDTH = A_HEADS * A_HD
WINDOW = 128
A_BLOCK = 128
MIX_WIDTH = M_WIDTH + A_WIDTH
REL_BUCKETS = 32
REL_MAX_DIST = 128
N_EXPERTS = 32
TOP_K = 4
D_FF = D_MODEL
SWIGLU_LIMIT = 7.0
SWIGLU_ALPHA = 1.702
MOE_BLOCK = 256
DN_ALPHA = (2.0 * DEPTH) ** 0.25
DN_BETA = (8.0 * DEPTH) ** -0.25
LN_EPS = 1e-5
IN_WIDTHS = (M_HEADS * M_DK, M_HEADS * M_DK, M_WIDTH, M_WIDTH, M_HEADS, M_HEADS,
             A_WIDTH, A_KV_HEADS * A_HD, A_KV_HEADS * A_HD)
N_IN = sum(IN_WIDTHS)
F32 = jnp.float32

kernel_name = 'hymba_mlstm_swa_moe_decode_step'


def layer_norm(x, g, b):
    xf = x.astype(F32)
    mu = jnp.mean(xf, -1, keepdims=True)
    var = jnp.mean(jnp.square(xf - mu), -1, keepdims=True)
    return ((xf - mu) * lax.rsqrt(var + LN_EPS) * g.astype(F32) + b.astype(F32)).astype(x.dtype)


def rms_norm(x, g):
    xf = x.astype(F32)
    r = lax.rsqrt(jnp.mean(jnp.square(xf), -1, keepdims=True) + LN_EPS)
    return (xf * r * g.astype(F32)).astype(x.dtype)


def rel_bucket(dist):
    exact = REL_BUCKETS // 2
    d = jnp.maximum(dist, 0)
    log_b = exact + (jnp.log(jnp.maximum(d, 1).astype(F32) / exact)
                     / math.log(REL_MAX_DIST / exact) * (REL_BUCKETS - exact)).astype(jnp.int32)
    return jnp.where(d < exact, d, jnp.minimum(log_b, REL_BUCKETS - 1))


def rel_bias_heads(rel_table, dist):
    b = rel_table.astype(F32)[rel_bucket(dist)]
    return jnp.moveaxis(b, -1, 0).reshape(A_KV_HEADS, A_GROUP, *dist.shape)


def sink_softmax(s, sinks):
    sk = sinks.astype(F32).reshape(A_KV_HEADS, A_GROUP, 1)
    m = jnp.maximum(jnp.max(s, -1), sk)
    p = jnp.exp(s - m[..., None])
    return p / (jnp.sum(p, -1) + jnp.exp(sk - m))[..., None]


def split_projections(x, w_in, b_i, b_f):
    B, L, _ = x.shape
    proj = x @ w_in
    pts = [int(p) for p in np.cumsum(IN_WIDTHS)[:-1]]
    qm, km, vm, om, im, fm, qa, ka, va = jnp.split(proj, pts, axis=-1)
    heads = lambda a, h: a.reshape(B, L, h, -1)
    t = lambda a: jnp.swapaxes(a, 1, 2).astype(F32)
    q_m = t(heads(qm, M_HEADS))
    k_m = t(heads(km, M_HEADS)) * (M_DK ** -0.5)
    v_m = t(heads(vm, M_HEADS))
    ig = t(im + b_i)
    lf = jax.nn.log_sigmoid(t(fm + b_f))
    att_in = (heads(qa, A_HEADS), heads(ka, A_KV_HEADS), heads(va, A_KV_HEADS))
    return (q_m, k_m, v_m, ig, lf), om, att_in


def mlstm_chunk(carry, inp):
    C, n, m = carry
    q, k, v, ig, lf = inp
    L = q.shape[2]
    b = jnp.cumsum(lf, axis=-1)
    causal = jnp.tril(jnp.ones((L, L), bool))
    D = jnp.where(causal, b[..., :, None] - b[..., None, :] + ig[..., None, :], -jnp.inf)
    inter = b + m[..., None]
    m_t = jnp.maximum(inter, jnp.max(D, -1))
    w_inter = jnp.exp(inter - m_t)
    S = jnp.einsum('bhtd,bhsd->bhts', q, k) * jnp.exp(D - m_t[..., None])
    num = w_inter[..., None] * jnp.einsum('bhvd,bhtd->bhtv', C, q) + jnp.einsum('bhts,bhsv->bhtv', S, v)
    den = w_inter * jnp.einsum('bhd,bhtd->bht', n, q) + jnp.sum(S, -1)
    h = num / jnp.maximum(jnp.abs(den), jnp.exp(-m_t))[..., None]
    bL = b[..., -1]
    g = ig + bL[..., None] - b
    m_new = jnp.maximum(bL + m, jnp.max(g, -1))
    a = jnp.exp(bL + m - m_new)
    wg = jnp.exp(g - m_new[..., None])
    C_new = a[..., None, None] * C + jnp.einsum('bhs,bhsv,bhsd->bhvd', wg, v, k)
    n_new = a[..., None] * n + jnp.einsum('bhs,bhsd->bhd', wg, k)
    return (C_new, n_new, m_new), h


def mlstm_prompt(q, k, v, ig, lf):
    B, H, L, _ = q.shape
    init = (jnp.zeros((B, H, M_DV, M_DK), F32), jnp.zeros((B, H, M_DK), F32), jnp.zeros((B, H), F32))
    st, h_meta = mlstm_chunk(init, tuple(a[:, :, :N_META] for a in (q, k, v, ig, lf)))
    nc = (L - N_META) // M_CHUNK

    def to_chunks(a):
        a = a[:, :, N_META:]
        return jnp.moveaxis(a.reshape(B, H, nc, M_CHUNK, *a.shape[3:]), 2, 0)

    st, h_rest = lax.scan(mlstm_chunk, st, tuple(to_chunks(a) for a in (q, k, v, ig, lf)))
    h_rest = jnp.moveaxis(h_rest, 0, 2).reshape(B, H, nc * M_CHUNK, M_DV)
    return jnp.concatenate([h_meta, h_rest], axis=2), st


def swa_prompt(q, k, v, sinks, rel_table):
    B, L = q.shape[:2]
    pad = (-N_META) % A_BLOCK
    nq = (L + pad) // A_BLOCK
    qb = jnp.pad(q.astype(F32), ((0, 0), (pad, 0), (0, 0), (0, 0))).reshape(
        B, nq, A_BLOCK, A_KV_HEADS, A_GROUP, A_HD)

    def key_blocks(a):
        a = jnp.pad(a.astype(F32), ((0, 0), (pad + A_BLOCK, 0), (0, 0), (0, 0))).reshape(
            B, nq + 1, A_BLOCK, A_KV_HEADS, A_HD)
        return jnp.concatenate([a[:, :-1], a[:, 1:]], axis=2)

    kb, vb = key_blocks(k), key_blocks(v)
    r = jnp.arange(A_BLOCK)
    c = jnp.arange(2 * A_BLOCK)
    dist = r[:, None] + A_BLOCK - c[None, :]
    kpos = (jnp.arange(nq)[:, None] - 1) * A_BLOCK - pad + c[None, :]
    valid = ((dist >= 0) & (dist < WINDOW))[None] & (kpos >= 0)[:, None, :]
    s = jnp.einsum('bnqhgd,bnkhd->bnhgqk', qb, kb) * (A_HD ** -0.5) + rel_bias_heads(rel_table, dist)
    s = jnp.where(valid[None, :, None, None], s, -jnp.inf)
    p = sink_softmax(s, sinks)
    o = jnp.einsum('bnhgqk,bnkhd->bnqhgd', p, vb).reshape(B, nq * A_BLOCK, A_WIDTH)
    return o[:, pad:].astype(q.dtype)


def swa_sample(q, k, v, cache_k, cache_v, sinks, rel_table):
    B, S = q.shape[:2]
    kk = jnp.concatenate([cache_k, k.astype(cache_k.dtype)], axis=1)
    vv = jnp.concatenate([cache_v, v.astype(cache_v.dtype)], axis=1)
    qpos = PAST_LEN + jnp.arange(S)
    kpos = jnp.concatenate([PAST_LEN - WINDOW + jnp.arange(WINDOW), PAST_LEN + jnp.arange(S)])
    dist = qpos[:, None] - kpos[None, :]
    valid = (dist >= 0) & (dist < WINDOW)
    qg = q.reshape(B, S, A_KV_HEADS, A_GROUP, A_HD).astype(F32)
    s = jnp.einsum('bqhgd,bkhd->bhgqk', qg, kk.astype(F32)) * (A_HD ** -0.5) + rel_bias_heads(rel_table, dist)
    s = jnp.where(valid, s, -jnp.inf)
    p = sink_softmax(s, sinks)
    o = jnp.einsum('bhgqk,bkhd->bqhgd', p, vv.astype(F32)).reshape(B, S, A_WIDTH)
    return o.astype(q.dtype), kk[:, -WINDOW:], vv[:, -WINDOW:]


def merge_groups(h_m, o_m, att, g_m, g_a, w_out):
    B, H, L, _ = h_m.shape
    hm = jnp.swapaxes(h_m, 1, 2).reshape(B, L, M_WIDTH).astype(o_m.dtype) * jax.nn.sigmoid(o_m)
    y = jnp.concatenate([rms_norm(hm, g_m), rms_norm(att, g_a)], axis=-1)
    return y @ w_out


def moe_ffn(x, w_router, b_router, w1, b1, w2, b2):
    xf = x.reshape(-1, D_MODEL)
    T = xf.shape[0]
    logits = (xf @ w_router + b_router).astype(F32)
    top_v, top_i = lax.top_k(logits, TOP_K)
    gates = jax.nn.softmax(top_v, axis=-1)
    A = T * TOP_K
    blk = max(8, min(MOE_BLOCK, A // N_EXPERTS))
    nb = -(-A // blk) + N_EXPERTS
    e_flat = top_i.reshape(-1)
    tok_flat = jnp.repeat(jnp.arange(T, dtype=jnp.int32), TOP_K)
    g_flat = gates.reshape(-1)
    order = jnp.argsort(e_flat, stable=True)
    e_s, tok_s, g_s = e_flat[order], tok_flat[order], g_flat[order]
    counts = jnp.bincount(e_flat, length=N_EXPERTS)
    starts = jnp.cumsum(counts) - counts
    padded = (counts + blk - 1) // blk * blk
    pends = jnp.cumsum(padded)
    dest = (pends - padded)[e_s] + jnp.arange(A) - starts[e_s]
    rows = nb * blk
    row_tok = jnp.full((rows,), T, jnp.int32).at[dest].set(tok_s)
    row_gate = jnp.zeros((rows,), F32).at[dest].set(g_s)
    blk_exp = jnp.minimum(jnp.searchsorted(pends, jnp.arange(nb) * blk, side='right'), N_EXPERTS - 1)

    def run_block(args):
        toks, gate, e = args
        xb = xf[jnp.minimum(toks, T - 1)]
        h = xb @ w1[e] + b1[e]
        x_glu = jnp.minimum(h[:, 0::2], SWIGLU_LIMIT)
        x_lin = jnp.clip(h[:, 1::2], -SWIGLU_LIMIT, SWIGLU_LIMIT)
        a = x_glu * jax.nn.sigmoid(SWIGLU_ALPHA * x_glu) * (x_lin + 1.0)
        y = a @ w2[e] + b2[e]
        return y.astype(F32) * gate[:, None]

    ys = lax.map(run_block, (row_tok.reshape(nb, blk), row_gate.reshape(nb, blk), blk_exp))
    out = jax.ops.segment_sum(ys.reshape(rows, D_MODEL), row_tok, num_segments=T + 1)[:T]
    return out.reshape(x.shape).astype(x.dtype)


def setup_inputs(seed: int = 0) -> dict:
    key = jax.random.key(seed)
    ks = jax.random.split(key, 26)
    nrm = lambda k, shape, scale: jax.random.normal(k, shape, F32) * scale
    return {
        'x_prompt': nrm(ks[0], (BATCH, SEQ, D_MODEL), 1.0),
        'x_sample': nrm(ks[1], (DEC_BATCH, DEC_SEQ, D_MODEL), 1.0),
        'cache_swa_k': nrm(ks[2], (DEPTH, DEC_BATCH, WINDOW, A_KV_HEADS, A_HD), 1.0),
        'cache_swa_v': nrm(ks[3], (DEPTH, DEC_BATCH, WINDOW, A_KV_HEADS, A_HD), 1.0),
        'state_mlstm_C': nrm(ks[4], (DEPTH, DEC_BATCH, M_HEADS, M_DV, M_DK), 0.5),
        'state_mlstm_n': nrm(ks[5], (DEPTH, DEC_BATCH, M_HEADS, M_DK), 0.5),
        'state_mlstm_m': nrm(ks[6], (DEPTH, DEC_BATCH, M_HEADS), 1.0),
        'meta_tokens': nrm(ks[7], (N_META, D_MODEL), 1.0),
        'rel_bias': nrm(ks[8], (REL_BUCKETS, A_HEADS), 0.5),
        'w_in': nrm(ks[9], (DEPTH, D_MODEL, N_IN), D_MODEL ** -0.5),
        'b_igate': nrm(ks[10], (DEPTH, M_HEADS), 0.1),
        'b_fgate': 3.0 + 3.0 * jax.random.uniform(ks[11], (DEPTH, M_HEADS), F32),
        'attn_sinks': nrm(ks[12], (DEPTH, A_HEADS), 0.5),
        'g_mlstm_out': 1.0 + nrm(ks[13], (DEPTH, M_WIDTH), 0.02),
        'g_attn_out': 1.0 + nrm(ks[14], (DEPTH, A_WIDTH), 0.02),
        'w_out': nrm(ks[15], (DEPTH, MIX_WIDTH, D_MODEL), DN_BETA * MIX_WIDTH ** -0.5),
        'ln1_g': 1.0 + nrm(ks[16], (DEPTH, D_MODEL), 0.02),
        'ln1_b': nrm(ks[17], (DEPTH, D_MODEL), 0.02),
        'w_router': nrm(ks[18], (DEPTH, D_MODEL, N_EXPERTS), D_MODEL ** -0.5),
        'b_router': nrm(ks[19], (DEPTH, N_EXPERTS), 0.01),
        'w_moe1': nrm(ks[20], (DEPTH, N_EXPERTS, D_MODEL, 2 * D_FF), D_MODEL ** -0.5),
        'b_moe1': nrm(ks[21], (DEPTH, N_EXPERTS, 2 * D_FF), 0.01),
        'w_moe2': nrm(ks[22], (DEPTH, N_EXPERTS, D_FF, D_MODEL), DN_BETA * D_FF ** -0.5),
        'b_moe2': nrm(ks[23], (DEPTH, N_EXPERTS, D_MODEL), 0.01),
        'ln2_g': 1.0 + nrm(ks[24], (DEPTH, D_MODEL), 0.02),
        'ln2_b': nrm(ks[25], (DEPTH, D_MODEL), 0.02),
    }


def reference(x_prompt, x_sample, cache_swa_k, cache_swa_v, state_mlstm_C, state_mlstm_n, state_mlstm_m,
              meta_tokens, rel_bias, w_in, b_igate, b_fgate, attn_sinks, g_mlstm_out, g_attn_out, w_out,
              ln1_g, ln1_b, w_router, b_router, w_moe1, b_moe1, w_moe2, b_moe2, ln2_g, ln2_b):
    B = x_prompt.shape[0]
    meta = jnp.broadcast_to(meta_tokens.astype(x_prompt.dtype)[None], (B, N_META, D_MODEL))
    xp = jnp.concatenate([meta, x_prompt], axis=1)
    xs = x_sample
    kp_l, vp_l, Cp_l, np_l, mp_l = [], [], [], [], []
    ks_l, vs_l, Cs_l, ns_l, ms_l = [], [], [], [], []
    for l in range(DEPTH):
        (qm, km, vm, ig, lf), om, (qa, ka, va) = split_projections(xp, w_in[l], b_igate[l], b_fgate[l])
        h, (C, n, m) = mlstm_prompt(qm, km, vm, ig, lf)
        att = swa_prompt(qa, ka, va, attn_sinks[l], rel_bias)
        mix = merge_groups(h, om, att, g_mlstm_out[l], g_attn_out[l], w_out[l])
        xp = layer_norm(DN_ALPHA * xp + mix, ln1_g[l], ln1_b[l])
        ff = moe_ffn(xp, w_router[l], b_router[l], w_moe1[l], b_moe1[l], w_moe2[l], b_moe2[l])
        xp = layer_norm(DN_ALPHA * xp + ff, ln2_g[l], ln2_b[l])
        kp_l.append(ka[:, -WINDOW:].astype(cache_swa_k.dtype))
        vp_l.append(va[:, -WINDOW:].astype(cache_swa_v.dtype))
        Cp_l.append(C.astype(state_mlstm_C.dtype))
        np_l.append(n.astype(state_mlstm_n.dtype))
        mp_l.append(m.astype(state_mlstm_m.dtype))
        (qm, km, vm, ig, lf), om, (qa, ka, va) = split_projections(xs, w_in[l], b_igate[l], b_fgate[l])
        st = (state_mlstm_C[l].astype(F32), state_mlstm_n[l].astype(F32), state_mlstm_m[l].astype(F32))
        (C, n, m), h = mlstm_chunk(st, (qm, km, vm, ig, lf))
        att, kbuf, vbuf = swa_sample(qa, ka, va, cache_swa_k[l], cache_swa_v[l], attn_sinks[l], rel_bias)
        mix = merge_groups(h, om, att, g_mlstm_out[l], g_attn_out[l], w_out[l])
        xs = layer_norm(DN_ALPHA * xs + mix, ln1_g[l], ln1_b[l])
        ff = moe_ffn(xs, w_router[l], b_router[l], w_moe1[l], b_moe1[l], w_moe2[l], b_moe2[l])
        xs = layer_norm(DN_ALPHA * xs + ff, ln2_g[l], ln2_b[l])
        ks_l.append(kbuf)
        vs_l.append(vbuf)
        Cs_l.append(C.astype(state_mlstm_C.dtype))
        ns_l.append(n.astype(state_mlstm_n.dtype))
        ms_l.append(m.astype(state_mlstm_m.dtype))
    y_prompt = xp[:, N_META:]
    y_sample = xs
    return (y_prompt, y_sample,
            jnp.stack(kp_l), jnp.stack(vp_l), jnp.stack(Cp_l), jnp.stack(np_l), jnp.stack(mp_l),
            jnp.stack(ks_l), jnp.stack(vs_l), jnp.stack(Cs_l), jnp.stack(ns_l), jnp.stack(ms_l))
```

```python
import functools
import math

import numpy as np
import jax
import jax.numpy as jnp
from jax import lax
from jax.experimental import pallas as pl
from jax.experimental.pallas import tpu as pltpu

F32 = jnp.float32
BF16 = jnp.bfloat16
I32 = jnp.int32
U32 = jnp.uint32

D_MODEL = 1024
N_META = 16
M_HEADS = 4
M_DK = 128
M_DV = 128
M_WIDTH = M_HEADS * M_DV
A_HD = 64
A_HEADS = 8
A_KV_HEADS = 2
A_GROUP = A_HEADS // A_KV_HEADS
A_WIDTH = A_HEADS * A_HD
WINDOW = 128
REL_BUCKETS = 32
REL_MAX_DIST = 128
N_EXPERTS = 32
TOP_K = 4
D_FF = D_MODEL
SWIGLU_LIMIT = 7.0
SWIGLU_ALPHA = 1.702
DEPTH = 1
DN_ALPHA = (2.0 * DEPTH) ** 0.25
LN_EPS = 1e-5
IN_WIDTHS = (M_WIDTH, M_WIDTH, M_WIDTH, M_WIDTH, M_HEADS, M_HEADS, A_WIDTH, A_KV_HEADS * A_HD, A_KV_HEADS * A_HD)

LANES = 128
NEG = -1e30
VMEM_LIMIT = 56 * 1024 * 1024

M_CHUNK = 128
PROJ_TILE = 512
MERGE_TILE = 512
RANK_TILE = 512
ROW_TILE = 256
EXPERT_TILE = 512
SAMPLE_GROUP = 8
HEAD_ORDER = (0, 4, 1, 5, 2, 6, 3, 7)


def _cparams(*sem):
    return pltpu.CompilerParams(dimension_semantics=sem, vmem_limit_bytes=VMEM_LIMIT)


def _log_sigmoid(x):
    return jnp.minimum(x, 0.0) - jnp.log1p(jnp.exp(-jnp.abs(x)))


def _sigmoid(x):
    return 1.0 / (1.0 + jnp.exp(-x))


def _proj_kernel(x_ref, wr_ref, wt_ref, brow_ref, bcol_ref, *outs, row_plan, t_plan, tail_cols):
    xb = x_ref[...].astype(BF16)
    tm = xb.shape[0]
    o = 0
    for (c0, width, kind, _) in row_plan:
        r = jnp.dot(xb, wr_ref[:, c0:c0 + width], preferred_element_type=F32)
        if kind == "gate":
            r = r + brow_ref[...]
            lane = lax.broadcasted_iota(I32, r.shape, 1)
            r = jnp.where(lane < M_HEADS, r, _log_sigmoid(r))
        outs[o][...] = r.astype(outs[o].dtype)
        o += 1
    for (r0, nrows, kind, _) in t_plan:
        r = lax.dot_general(wt_ref[r0:r0 + nrows, :], xb, (((1,), (1,)), ((), ())), preferred_element_type=F32)
        if kind == "gate":
            r = r + bcol_ref[...]
            row = lax.broadcasted_iota(I32, r.shape, 0)
            r = jnp.where(row < M_HEADS, r, _log_sigmoid(r))
        outs[o][...] = r.astype(outs[o].dtype)
        o += 1
    if tail_cols is not None:
        c0, width = tail_cols
        outs[o][...] = jnp.dot(xb[tm - WINDOW:, :], wr_ref[:, c0:c0 + width], preferred_element_type=F32)


def _proj(x, wr, wt, brow, bcol, row_plan, t_plan, tail_cols, tile, rows_per_group, name):
    t = x.shape[0]
    nt = t // tile
    out_shape, out_specs = [], []
    for (_, width, _, dt) in row_plan:
        out_shape.append(jax.ShapeDtypeStruct((t, width), dt))
        out_specs.append(pl.BlockSpec((tile, width), lambda i: (i, 0)))
    for (_, nrows, _, dt) in t_plan:
        out_shape.append(jax.ShapeDtypeStruct((nrows, t), dt))
        out_specs.append(pl.BlockSpec((nrows, tile), lambda i: (0, i)))
    if tail_cols is not None:
        tiles_per_group = rows_per_group // tile
        out_shape.append(jax.ShapeDtypeStruct((t // rows_per_group * WINDOW, tail_cols[1]), F32))
        out_specs.append(pl.BlockSpec((WINDOW, tail_cols[1]), lambda i: (i // tiles_per_group, 0)))
    kern = functools.partial(_proj_kernel, row_plan=row_plan, t_plan=t_plan, tail_cols=tail_cols)
    return pl.pallas_call(
        kern, out_shape=out_shape, grid=(nt,),
        in_specs=[pl.BlockSpec((tile, D_MODEL), lambda i: (i, 0)),
                  pl.BlockSpec(wr.shape, lambda i: (0, 0)),
                  pl.BlockSpec(wt.shape, lambda i: (0, 0)),
                  pl.BlockSpec(brow.shape, lambda i: (0, 0)),
                  pl.BlockSpec(bcol.shape, lambda i: (0, 0))],
        out_specs=out_specs, compiler_params=_cparams("arbitrary"), name=name,
    )(x, wr, wt, brow, bcol)


def _split3(a):
    hi = a.astype(BF16)
    r1 = a - hi.astype(F32)
    mid = r1.astype(BF16)
    lo = (r1 - mid.astype(F32)).astype(BF16)
    return hi, mid, lo


def _mlstm_kernel(q_ref, v_ref, kt_ref, gc_ref, gr_ref, c0_ref, m0_ref, h_ref, ct_out_ref, m_out_ref,
                  ct_scr, m_scr, *, n_valid):
    c = pl.program_id(1)
    nc = pl.num_programs(1)
    L = q_ref.shape[0]

    @pl.when(c == 0)
    def _():
        ct_scr[...] = c0_ref[...]
        m_scr[...] = m0_ref[...]

    gc = gc_ref[...]
    gr = gr_ref[...]
    if n_valid < L:
        rowc = lax.broadcasted_iota(I32, gc.shape, 0)
        lanec = lax.broadcasted_iota(I32, gc.shape, 1)
        gc = jnp.where(rowc < n_valid, gc, jnp.where(lanec < M_HEADS, NEG, 0.0))
        rowr = lax.broadcasted_iota(I32, gr.shape, 0)
        colr = lax.broadcasted_iota(I32, gr.shape, 1)
        gr = jnp.where(colr < n_valid, gr, jnp.where(rowr < M_HEADS, NEG, 0.0))
    r_i = lax.broadcasted_iota(I32, (L, L), 0)
    c_i = lax.broadcasted_iota(I32, (L, L), 1)
    causal = c_i <= r_i
    tril = jnp.where(causal, 1.0, 0.0).astype(BF16)
    triu = jnp.where(r_i <= c_i, 1.0, 0.0).astype(BF16)
    b_cols = sum(jnp.dot(tril, part, preferred_element_type=F32) for part in _split3(gc))
    b_rows = sum(jnp.dot(part, triu, preferred_element_type=F32) for part in _split3(gr))
    lane_l = lax.broadcasted_iota(I32, (L, LANES), 1)
    e0 = jnp.where(lane_l == 0, 1.0, 0.0)
    scale = M_DK ** -0.5

    for h in range(M_HEADS):
        sl = slice(h * M_DK, (h + 1) * M_DK)
        q = q_ref[:, sl]
        v = v_ref[:, sl]
        kt = kt_ref[sl, :]
        ig_c = gc[:, h:h + 1]
        b_c = b_cols[:, M_HEADS + h:M_HEADS + h + 1]
        ig_r = gr[h:h + 1, :]
        b_r = b_rows[M_HEADS + h:M_HEADS + h + 1, :]
        m_prev = m_scr[h:h + 1, 0:1]
        ct = ct_scr[h]

        d = jnp.where(causal, b_c + (ig_r - b_r), NEG)
        m_t = jnp.maximum(b_c + m_prev, jnp.max(d, axis=1, keepdims=True))
        qk = jnp.dot(q, kt, preferred_element_type=F32) * scale
        s = qk * jnp.exp(d - m_t)
        inter = jnp.dot(q, ct.astype(BF16), preferred_element_type=F32)
        v_aug = jnp.concatenate([v, e0.astype(BF16)], axis=1)
        intra = jnp.dot(s.astype(BF16), v_aug, preferred_element_type=F32)
        nd = jnp.exp(b_c + m_prev - m_t) * inter + intra
        den = nd[:, M_DV:M_DV + 1]
        h_ref[:, sl] = nd[:, :M_DV] / jnp.maximum(jnp.abs(den), jnp.exp(-m_t))

        b_last = b_c[L - 1:L, :]
        g = ig_c + b_last - b_c
        m_new = jnp.maximum(b_last + m_prev, jnp.max(g, axis=0, keepdims=True))
        a = jnp.exp(b_last + m_prev - m_new)
        wg = jnp.exp(g - m_new)
        wv = jnp.concatenate([(v.astype(F32) * wg).astype(BF16), (e0 * wg).astype(BF16)], axis=1)
        upd = jnp.dot(kt, wv, preferred_element_type=F32)
        ct_scr[h] = a * ct + upd * scale
        m_scr[h:h + 1, :] = jnp.broadcast_to(m_new, (1, LANES))

    @pl.when(c == nc - 1)
    def _():
        for h in range(M_HEADS):
            ct_out_ref[0, h] = ct_scr[h].T
        m_out_ref[0] = m_scr[...]


def _mlstm(qm, vm, kt, gc, gr, c0, m0, batch, n_valid, name):
    L = M_CHUNK
    nc = qm.shape[0] // (batch * L)
    kern = functools.partial(_mlstm_kernel, n_valid=n_valid)
    return pl.pallas_call(
        kern,
        out_shape=[jax.ShapeDtypeStruct((batch * nc * L, M_WIDTH), F32),
                   jax.ShapeDtypeStruct((batch, M_HEADS, 2 * M_DV, M_DK), F32),
                   jax.ShapeDtypeStruct((batch, 8, LANES), F32)],
        grid=(batch, nc),
        in_specs=[pl.BlockSpec((L, M_WIDTH), lambda b, c: (b * nc + c, 0)),
                  pl.BlockSpec((L, M_WIDTH), lambda b, c: (b * nc + c, 0)),
                  pl.BlockSpec((M_WIDTH, L), lambda b, c: (0, b * nc + c)),
                  pl.BlockSpec((L, LANES), lambda b, c: (b * nc + c, 0)),
                  pl.BlockSpec((8, L), lambda b, c: (0, b * nc + c)),
                  pl.BlockSpec((M_HEADS, M_DK, 2 * M_DV), lambda b, c: (0, 0, 0)),
                  pl.BlockSpec((8, LANES), lambda b, c: (0, 0))],
        out_specs=[pl.BlockSpec((L, M_WIDTH), lambda b, c: (b * nc + c, 0)),
                   pl.BlockSpec((1, M_HEADS, 2 * M_DV, M_DK), lambda b, c: (b, 0, 0, 0)),
                   pl.BlockSpec((1, 8, LANES), lambda b, c: (b, 0, 0))],
        scratch_shapes=[pltpu.VMEM((M_HEADS, M_DK, 2 * M_DV), F32), pltpu.VMEM((8, LANES), F32)],
        compiler_params=_cparams("arbitrary", "arbitrary"), name=name,
    )(qm, vm, kt, gc, gr, c0, m0)


def _mlstm_step_kernel(c_ref, n_ref, m_ref, gc_ref, q_ref, k_ref, v_ref, vt_ref,
                       c_out_ref, n_out_ref, m_out_ref, h_ref):
    i = pl.program_id(0)
    g = c_ref.shape[0]
    scale = M_DK ** -0.5
    lane_b = lax.broadcasted_iota(I32, (M_DV, vt_ref.shape[1]), 1)
    lane_m = lax.broadcasted_iota(I32, (1, LANES), 1)
    for j in range(g):
        m_row = jnp.zeros((1, LANES), F32)
        for h in range(M_HEADS):
            sl = slice(h * M_DK, (h + 1) * M_DK)
            q = q_ref[j:j + 1, sl]
            k = k_ref[j:j + 1, sl] * scale
            v = v_ref[j:j + 1, sl]
            ig = gc_ref[j:j + 1, h:h + 1]
            lf = gc_ref[j:j + 1, M_HEADS + h:M_HEADS + h + 1]
            m = m_ref[j:j + 1, h:h + 1]
            c = c_ref[j, h]
            n = n_ref[j, h:h + 1, :]
            m_t = jnp.maximum(lf + m, ig)
            w = jnp.exp(lf + m - m_t)
            wg = jnp.exp(ig - m_t)
            s = jnp.sum(q * k, axis=1, keepdims=True) * wg
            q8 = jnp.broadcast_to(q, (8, M_DK)).astype(BF16)
            cq = lax.dot_general(q8, c.astype(BF16), (((1,), (1,)), ((), ())), preferred_element_type=F32)[0:1, :]
            den = w * jnp.sum(n * q, axis=1, keepdims=True) + s
            h_ref[j:j + 1, sl] = (w * cq + s * v) / jnp.maximum(jnp.abs(den), jnp.exp(-m_t))
            v_col = jnp.sum(jnp.where(lane_b == i * g + j, vt_ref[sl, :], 0.0), axis=1, keepdims=True)
            c_out_ref[j, h] = w * c + (wg * v_col) * k
            n_out_ref[j, h:h + 1, :] = w * n + wg * k
            m_row = jnp.where(lane_m == h, m_t, m_row)
        m_out_ref[j:j + 1, :] = m_row


def _mlstm_step(c, n, m_pad, gc, q, k, v, vt):
    nb = c.shape[0]
    g = SAMPLE_GROUP
    row = lambda w: pl.BlockSpec((g, w), lambda i: (i, 0))
    return pl.pallas_call(
        _mlstm_step_kernel,
        out_shape=[jax.ShapeDtypeStruct(c.shape, F32), jax.ShapeDtypeStruct(n.shape, F32),
                   jax.ShapeDtypeStruct((nb, LANES), F32), jax.ShapeDtypeStruct((nb, M_WIDTH), F32)],
        grid=(nb // g,),
        in_specs=[pl.BlockSpec((g, M_HEADS, M_DV, M_DK), lambda i: (i, 0, 0, 0)),
                  pl.BlockSpec((g, M_HEADS, M_DK), lambda i: (i, 0, 0)),
                  row(LANES), row(LANES), row(M_WIDTH), row(M_WIDTH), row(M_WIDTH),
                  pl.BlockSpec(vt.shape, lambda i: (0, 0))],
        out_specs=[pl.BlockSpec((g, M_HEADS, M_DV, M_DK), lambda i: (i, 0, 0, 0)),
                   pl.BlockSpec((g, M_HEADS, M_DK), lambda i: (i, 0, 0)),
                   row(LANES), row(M_WIDTH)],
        compiler_params=_cparams("arbitrary"), name="mlstm_step",
    )(c, n, m_pad, gc, q, k, v, vt)


def _swa_kernel(q_ref, kc_ref, kp_ref, vc_ref, vp_ref, km_ref, vm_ref, bias_ref, sink_ref, o_ref):
    j = pl.program_id(1)
    first = j == 0
    kp = jnp.where(first, km_ref[...], kp_ref[...])
    vp = jnp.where(first, vm_ref[...], vp_ref[...])
    k = jnp.concatenate([kp, kc_ref[...]], axis=0)
    v = jnp.concatenate([vp, vc_ref[...]], axis=0)
    lane = lax.broadcasted_iota(I32, k.shape, 1)
    zero = jnp.zeros_like(k)
    k_half = (jnp.where(lane < A_HD, k, zero), jnp.where(lane >= A_HD, k, zero))
    v_stack = jnp.concatenate([jnp.where(lane < A_HD, v, zero), jnp.where(lane >= A_HD, v, zero)], axis=0)
    lane_q = lax.broadcasted_iota(I32, (q_ref.shape[0], LANES), 1)
    for p in range(A_GROUP):
        qs = q_ref[:, p * LANES:(p + 1) * LANES]
        probs, inv = [], []
        for half in range(2):
            hd = HEAD_ORDER[2 * p + half]
            s = lax.dot_general(qs, k_half[half], (((1,), (1,)), ((), ())), preferred_element_type=F32)
            s = s * (A_HD ** -0.5) + bias_ref[0, hd]
            sk = sink_ref[hd:hd + 1, 0:1]
            m = jnp.maximum(jnp.max(s, axis=1, keepdims=True), sk)
            e = jnp.exp(s - m)
            probs.append(e.astype(BF16))
            inv.append(1.0 / (jnp.sum(e, axis=1, keepdims=True) + jnp.exp(sk - m)))
        o = jnp.dot(jnp.concatenate(probs, axis=1), v_stack, preferred_element_type=F32)
        o_ref[:, p * LANES:(p + 1) * LANES] = o * jnp.where(lane_q < A_HD, inv[0], inv[1])


def _swa(qa, ka, va, kmeta, vmeta, bias, sinks, batch):
    blk = WINDOW
    nq = qa.shape[0] // (batch * blk)
    kv_cur = pl.BlockSpec((blk, LANES), lambda b, j: (b * nq + j, 0))
    kv_prev = pl.BlockSpec((blk, LANES), lambda b, j: (b * nq + jnp.maximum(j - 1, 0), 0))
    const2 = lambda shape: pl.BlockSpec(shape, lambda b, j: (0, 0))
    return pl.pallas_call(
        _swa_kernel, out_shape=jax.ShapeDtypeStruct((qa.shape[0], A_WIDTH), F32), grid=(batch, nq),
        in_specs=[pl.BlockSpec((blk, A_WIDTH), lambda b, j: (b * nq + j, 0)),
                  kv_cur, kv_prev, kv_cur, kv_prev, const2((blk, LANES)), const2((blk, LANES)),
                  pl.BlockSpec((1, A_HEADS, blk, 2 * blk), lambda b, j: (jnp.minimum(j, 1), 0, 0, 0)),
                  const2((8, LANES))],
        out_specs=pl.BlockSpec((blk, A_WIDTH), lambda b, j: (b * nq + j, 0)),
        compiler_params=_cparams("arbitrary", "arbitrary"), name="swa_prompt",
    )(qa, ka, ka, va, va, kmeta, vmeta, bias, sinks)


def _swa_step_kernel(ck_ref, cv_ref, q_ref, k_ref, v_ref, bias_ref, sink_ref, ko_ref, vo_ref, o_ref):
    g = ck_ref.shape[0]
    lane = lax.broadcasted_iota(I32, (WINDOW, LANES), 1)
    lane_q = lax.broadcasted_iota(I32, (1, LANES), 1)
    lo = lane < A_HD
    for j in range(g):
        ko_ref[j, 0:WINDOW - 1, :] = ck_ref[j, 1:WINDOW, :]
        ko_ref[j, WINDOW - 1:WINDOW, :] = k_ref[j:j + 1, :]
        vo_ref[j, 0:WINDOW - 1, :] = cv_ref[j, 1:WINDOW, :]
        vo_ref[j, WINDOW - 1:WINDOW, :] = v_ref[j:j + 1, :]
        kk = ko_ref[j]
        vv = vo_ref[j]
        for p in range(A_GROUP):
            prod = kk * q_ref[j:j + 1, p * LANES:(p + 1) * LANES]
            e2, inv2 = [], []
            for half in range(2):
                hd = HEAD_ORDER[2 * p + half]
                sel = lo if half == 0 else jnp.logical_not(lo)
                s = jnp.sum(jnp.where(sel, prod, 0.0), axis=1, keepdims=True) * (A_HD ** -0.5)
                s = s + bias_ref[:, hd:hd + 1]
                sk = sink_ref[hd:hd + 1, 0:1]
                m = jnp.maximum(jnp.max(s, axis=0, keepdims=True), sk)
                e = jnp.exp(s - m)
                e2.append(e)
                inv2.append(1.0 / (jnp.sum(e, axis=0, keepdims=True) + jnp.exp(sk - m)))
            o = jnp.sum(jnp.where(lo, e2[0], e2[1]) * vv, axis=0, keepdims=True)
            o_ref[j:j + 1, p * LANES:(p + 1) * LANES] = o * jnp.where(lane_q < A_HD, inv2[0], inv2[1])


def _swa_step(ck, cv, q, k, v, bias_cols, sinks):
    nb = ck.shape[0]
    g = SAMPLE_GROUP
    cache = pl.BlockSpec((g, WINDOW, LANES), lambda i: (i, 0, 0))
    row = lambda w: pl.BlockSpec((g, w), lambda i: (i, 0))
    const = lambda a: pl.BlockSpec(a.shape, lambda i: (0, 0))
    return pl.pallas_call(
        _swa_step_kernel,
        out_shape=[jax.ShapeDtypeStruct(ck.shape, F32), jax.ShapeDtypeStruct(cv.shape, F32),
                   jax.ShapeDtypeStruct((nb, A_WIDTH), F32)],
        grid=(nb // g,),
        in_specs=[cache, cache, row(A_WIDTH), row(LANES), row(LANES), const(bias_cols), const(sinks)],
        out_specs=[cache, cache, row(A_WIDTH)],
        compiler_params=_cparams("arbitrary"), name="swa_step",
    )(ck, cv, q, k, v, bias_cols, sinks)


def _layer_norm(z, g, b):
    mu = jnp.mean(z, axis=1, keepdims=True)
    zc = z - mu
    var = jnp.mean(zc * zc, axis=1, keepdims=True)
    return zc * lax.rsqrt(var + LN_EPS) * g + b


def _pack_halves(x):
    w = x.shape[1] // 2
    lo = pltpu.bitcast(x[:, :w].astype(BF16).astype(F32), U32)
    hi = pltpu.bitcast(x[:, w:].astype(BF16).astype(F32), U32)
    return (lo >> 16) | (hi & jnp.uint32(0xFFFF0000))


def _unpack_halves(words):
    lo = pltpu.bitcast(words << 16, F32).astype(BF16)
    hi = pltpu.bitcast(words & jnp.uint32(0xFFFF0000), F32).astype(BF16)
    return lo, hi


def _merge_kernel(h_ref, om_ref, att_ref, x_ref, gm_ref, ga_ref, wo_ref, g1_ref, b1_ref, wr_ref, br_ref,
                  x1_ref, xp_ref, tk_ref):
    hm = h_ref[...] * _sigmoid(om_ref[...])
    ym = hm * lax.rsqrt(jnp.mean(hm * hm, axis=1, keepdims=True) + LN_EPS) * gm_ref[...]
    att = att_ref[...]
    ya = att * lax.rsqrt(jnp.mean(att * att, axis=1, keepdims=True) + LN_EPS) * ga_ref[...]
    mix = (jnp.dot(ym.astype(BF16), wo_ref[0:M_WIDTH, :], preferred_element_type=F32)
           + jnp.dot(ya.astype(BF16), wo_ref[M_WIDTH:, :], preferred_element_type=F32))
    x1 = _layer_norm(DN_ALPHA * x_ref[...] + mix, g1_ref[...], b1_ref[...])
    x1_ref[...] = x1
    xp_ref[...] = _pack_halves(x1)
    logits = jnp.dot(x1.astype(BF16), wr_ref[...], preferred_element_type=F32) + br_ref[...]
    lane = lax.broadcasted_iota(I32, logits.shape, 1).astype(F32)
    vals, idxs = [], []
    for _ in range(TOP_K):
        mx = jnp.max(logits, axis=1, keepdims=True)
        idx = jnp.min(jnp.where(logits == mx, lane, float(LANES)), axis=1, keepdims=True)
        vals.append(mx)
        idxs.append(idx)
        logits = jnp.where(lane == idx, 2.0 * NEG, logits)
    es = [jnp.exp(vk - vals[0]) for vk in vals]
    tot = es[0] + es[1] + es[2] + es[3]
    tk = jnp.zeros(logits.shape, F32)
    for k in range(TOP_K):
        tk = jnp.where(lane == float(k), es[k] / tot, tk)
        tk = jnp.where(lane == float(TOP_K + k), idxs[k], tk)
    tk_ref[...] = tk


def _merge(h, om, att, x, gm, ga, wo, g1, b1, wr, br, tile, name):
    t = x.shape[0]
    rows = lambda w: pl.BlockSpec((tile, w), lambda i: (i, 0))
    const = lambda a: pl.BlockSpec(a.shape, lambda i: (0, 0))
    return pl.pallas_call(
        _merge_kernel,
        out_shape=[jax.ShapeDtypeStruct((t, D_MODEL), F32), jax.ShapeDtypeStruct((t, D_MODEL // 2), U32),
                   jax.ShapeDtypeStruct((t, LANES), F32)],
        grid=(t // tile,),
        in_specs=[rows(M_WIDTH), rows(M_WIDTH), rows(A_WIDTH), rows(D_MODEL), const(gm), const(ga), const(wo),
                  const(g1), const(b1), const(wr), const(br)],
        out_specs=[rows(D_MODEL), rows(D_MODEL // 2), rows(LANES)],
        compiler_params=_cparams("arbitrary"), name=name,
    )(h, om, att, x, gm, ga, wo, g1, b1, wr, br)


def _rank_kernel(tk_ref, base_ref, rank_ref, cnt_ref, base_scr):
    i = pl.program_id(0)

    @pl.when(i == 0)
    def _():
        base_scr[...] = base_ref[...]

    tk = tk_ref[...]
    t = tk.shape[0]
    lane = lax.broadcasted_iota(I32, tk.shape, 1).astype(F32)
    onehots = [jnp.where(lane == tk[:, TOP_K + k:TOP_K + k + 1], 1.0, 0.0) for k in range(TOP_K)]
    tot = onehots[0] + onehots[1] + onehots[2] + onehots[3]
    r_i = lax.broadcasted_iota(I32, (t, t), 0)
    c_i = lax.broadcasted_iota(I32, (t, t), 1)
    strict = jnp.where(c_i < r_i, 1.0, 0.0).astype(BF16)
    before = jnp.dot(strict, tot.astype(BF16), preferred_element_type=F32) + base_scr[0:1, :]
    out = jnp.zeros(tk.shape, F32)
    for k in range(TOP_K):
        out = jnp.where(lane == float(k), jnp.sum(onehots[k] * before, axis=1, keepdims=True), out)
    rank_ref[...] = out
    base_scr[...] = base_scr[...] + jnp.sum(tot, axis=0, keepdims=True)
    cnt_ref[...] = base_scr[...]


def _rank(tk, base):
    t = tk.shape[0]
    tile = min(RANK_TILE, t)
    return pl.pallas_call(
        _rank_kernel,
        out_shape=[jax.ShapeDtypeStruct((t, LANES), F32), jax.ShapeDtypeStruct((8, LANES), F32)],
        grid=(t // tile,),
        in_specs=[pl.BlockSpec((tile, LANES), lambda i: (i, 0)), pl.BlockSpec((8, LANES), lambda i: (0, 0))],
        out_specs=[pl.BlockSpec((tile, LANES), lambda i: (i, 0)), pl.BlockSpec((8, LANES), lambda i: (0, 0))],
        scratch_shapes=[pltpu.VMEM((8, LANES), F32)],
        compiler_params=_cparams("arbitrary"), name="moe_rank",
    )(tk, base)


def _offsets_kernel(cnt_ref, off_ref, be_ref, nu_ref, *, tile):
    cnt = cnt_ref[...]
    nblk = jnp.floor((cnt + float(tile - 1)) * (1.0 / tile))
    r_i = lax.broadcasted_iota(I32, (LANES, LANES), 0)
    c_i = lax.broadcasted_iota(I32, (LANES, LANES), 1)
    incl = jnp.where(r_i <= c_i, 1.0, 0.0).astype(BF16)
    cum = jnp.dot(nblk.astype(BF16), incl, preferred_element_type=F32)
    off_ref[...] = (cum - nblk) * float(tile)
    rows = be_ref.shape[0]
    jb = (lax.broadcasted_iota(I32, (rows, LANES), 0) * LANES + lax.broadcasted_iota(I32, (rows, LANES), 1)).astype(F32)
    acc = jnp.zeros((rows, LANES), F32)
    for e in range(N_EXPERTS):
        acc = acc + jnp.where(jb >= cum[0:1, e:e + 1], 1.0, 0.0)
    be_ref[...] = jnp.minimum(acc, float(N_EXPERTS - 1)).astype(I32)
    nu_ref[...] = jnp.broadcast_to(cum[0:1, N_EXPERTS - 1:N_EXPERTS], nu_ref.shape).astype(I32)


def _offsets(cnt, n_blocks, tile):
    rows = -(-n_blocks // LANES)
    rows = -(-rows // 8) * 8
    return pl.pallas_call(
        functools.partial(_offsets_kernel, tile=tile),
        out_shape=[jax.ShapeDtypeStruct((8, LANES), F32), jax.ShapeDtypeStruct((rows, LANES), I32),
                   jax.ShapeDtypeStruct((8, LANES), I32)],
        name="moe_offsets",
    )(cnt)


def _dest_kernel(tk_ref, rank_ref, off_ref, dest_ref):
    tk = tk_ref[...]
    rank = rank_ref[...]
    lane = lax.broadcasted_iota(I32, tk.shape, 1).astype(F32)
    off = off_ref[0:1, :]
    out = jnp.zeros(tk.shape, F32)
    for k in range(TOP_K):
        first = jnp.sum(jnp.where(lane == tk[:, TOP_K + k:TOP_K + k + 1], off, 0.0), axis=1, keepdims=True)
        out = jnp.where(lane == float(k), first + rank[:, k:k + 1], out)
    dest_ref[...] = out.astype(I32)


def _dest(tk, rank, off):
    t = tk.shape[0]
    tile = min(RANK_TILE, t)
    blk = pl.BlockSpec((tile, LANES), lambda i: (i, 0))
    return pl.pallas_call(
        _dest_kernel, out_shape=jax.ShapeDtypeStruct((t, LANES), I32), grid=(t // tile,),
        in_specs=[blk, blk, pl.BlockSpec((8, LANES), lambda i: (0, 0))], out_specs=blk,
        compiler_params=_cparams("arbitrary"), name="moe_dest",
    )(tk, rank, off)


def _dispatch_kernel(dest_ref, xp_ref, xs_in_ref, xs_ref, sem):
    del xs_in_ref
    t = xp_ref.shape[0]

    def row_copy(tok, dst):
        return pltpu.make_async_copy(xp_ref.at[pl.ds(tok, 1), :], xs_ref.at[pl.ds(dst, 1), :], sem)

    def issue(tok, carry):
        for k in range(TOP_K):
            row_copy(tok, dest_ref[tok * TOP_K + k]).start()
        return carry

    lax.fori_loop(0, t, issue, 0)
    for k in range(TOP_K):
        pltpu.make_async_copy(xp_ref, xs_ref.at[pl.ds(0, t), :], sem).wait()


def _dispatch(dest_flat, xp, xs):
    t = xp.shape[0]
    tile = min(ROW_TILE, t)
    return pl.pallas_call(
        _dispatch_kernel, out_shape=jax.ShapeDtypeStruct(xs.shape, xs.dtype), grid=(t // tile,),
        in_specs=[pl.BlockSpec((tile * TOP_K,), lambda i: (i,), memory_space=pltpu.SMEM),
                  pl.BlockSpec((tile, xp.shape[1]), lambda i: (i, 0)),
                  pl.BlockSpec(memory_space=pl.ANY)],
        out_specs=pl.BlockSpec(memory_space=pl.ANY),
        scratch_shapes=[pltpu.SemaphoreType.DMA(())],
        input_output_aliases={2: 0},
        compiler_params=_cparams("arbitrary"), name="moe_dispatch",
    )(dest_flat, xp, xs)


def _expert_kernel(be_ref, nu_ref, xs_ref, w1_ref, b1g_ref, b1l_ref, w2_ref, b2_ref, perm_ref, ys_ref,
                   w1g_scr, w1l_scr, w2_scr):
    j = pl.program_id(0)
    active = j < nu_ref[0]
    changed = jnp.logical_or(j == 0, be_ref[j] != be_ref[jnp.maximum(j - 1, 0)])
    half = D_MODEL // 2

    @pl.when(jnp.logical_and(active, changed))
    def _():
        for c in range(2 * D_FF // 256):
            wc = w1_ref[0, :, c * 256:(c + 1) * 256].astype(BF16)
            d = jnp.dot(wc, perm_ref[...], preferred_element_type=F32).astype(BF16)
            w1g_scr[:, c * 128:(c + 1) * 128] = d[:, :128]
            w1l_scr[:, c * 128:(c + 1) * 128] = d[:, 128:]
        for c in range(D_FF // 256):
            w2_scr[c * 256:(c + 1) * 256, :] = w2_ref[0, c * 256:(c + 1) * 256, :].astype(BF16)

    @pl.when(active)
    def _():
        lo, hi = _unpack_halves(xs_ref[...])
        y = jnp.zeros(ys_ref.shape, F32)
        nchunk = 512
        for c in range(D_FF // nchunk):
            cs = slice(c * nchunk, (c + 1) * nchunk)
            hg = (jnp.dot(lo, w1g_scr[0:half, cs], preferred_element_type=F32)
                  + jnp.dot(hi, w1g_scr[half:, cs], preferred_element_type=F32) + b1g_ref[0, :, cs])
            hl = (jnp.dot(lo, w1l_scr[0:half, cs], preferred_element_type=F32)
                  + jnp.dot(hi, w1l_scr[half:, cs], preferred_element_type=F32) + b1l_ref[0, :, cs])
            x_glu = jnp.minimum(hg, SWIGLU_LIMIT)
            x_lin = jnp.clip(hl, -SWIGLU_LIMIT, SWIGLU_LIMIT)
            a = x_glu * _sigmoid(SWIGLU_ALPHA * x_glu) * (x_lin + 1.0)
            y = y + jnp.dot(a.astype(BF16), w2_scr[cs, :], preferred_element_type=F32)
        ys_ref[...] = y + b2_ref[0]

    @pl.when(jnp.logical_not(active))
    def _():
        ys_ref[...] = jnp.zeros_like(ys_ref)


def _experts(be, nu, xs, w1, b1g, b1l, w2, b2, perm, tile):
    n_blocks = xs.shape[0] // tile
    last = lambda j, be, nu: jnp.minimum(j, nu[0] - 1)
    grid_spec = pltpu.PrefetchScalarGridSpec(
        num_scalar_prefetch=2, grid=(n_blocks,),
        in_specs=[pl.BlockSpec((tile, xs.shape[1]), lambda j, be, nu: (last(j, be, nu), 0)),
                  pl.BlockSpec((1, D_MODEL, 2 * D_FF), lambda j, be, nu: (be[j], 0, 0)),
                  pl.BlockSpec((1, 1, D_FF), lambda j, be, nu: (be[j], 0, 0)),
                  pl.BlockSpec((1, 1, D_FF), lambda j, be, nu: (be[j], 0, 0)),
                  pl.BlockSpec((1, D_FF, D_MODEL), lambda j, be, nu: (be[j], 0, 0)),
                  pl.BlockSpec((1, 1, D_MODEL), lambda j, be, nu: (be[j], 0, 0)),
                  pl.BlockSpec((256, 256), lambda j, be, nu: (0, 0))],
        out_specs=pl.BlockSpec((tile, D_MODEL), lambda j, be, nu: (j, 0)),
        scratch_shapes=[pltpu.VMEM((D_MODEL, D_FF), BF16), pltpu.VMEM((D_MODEL, D_FF), BF16),
                        pltpu.VMEM((D_FF, D_MODEL), BF16)])
    return pl.pallas_call(
        _expert_kernel, out_shape=jax.ShapeDtypeStruct((xs.shape[0], D_MODEL), F32), grid_spec=grid_spec,
        compiler_params=_cparams("arbitrary"), name="moe_experts",
    )(be, nu, xs, w1, b1g, b1l, w2, b2, perm)


def _combine_kernel(dest_ref, ys_ref, tk_ref, x1_ref, g2_ref, b2_ref, out_ref, buf, sem):
    t = x1_ref.shape[0]

    def issue(tok, carry):
        for k in range(TOP_K):
            pltpu.make_async_copy(ys_ref.at[pl.ds(dest_ref[tok * TOP_K + k], 1), :],
                                  buf.at[k, pl.ds(tok, 1), :], sem).start()
        return carry

    lax.fori_loop(0, t, issue, 0)
    for k in range(TOP_K):
        pltpu.make_async_copy(ys_ref.at[pl.ds(0, t), :], buf.at[k], sem).wait()
    tk = tk_ref[...]
    ff = tk[:, 0:1] * buf[0]
    for k in range(1, TOP_K):
        ff = ff + tk[:, k:k + 1] * buf[k]
    out_ref[...] = _layer_norm(DN_ALPHA * x1_ref[...] + ff, g2_ref[...], b2_ref[...])


def _combine(dest_flat, ys, tk, x1, g2, b2):
    t = x1.shape[0]
    tile = min(ROW_TILE, t)
    return pl.pallas_call(
        _combine_kernel, out_shape=jax.ShapeDtypeStruct((t, D_MODEL), F32), grid=(t // tile,),
        in_specs=[pl.BlockSpec((tile * TOP_K,), lambda i: (i,), memory_space=pltpu.SMEM),
                  pl.BlockSpec(memory_space=pl.ANY),
                  pl.BlockSpec((tile, LANES), lambda i: (i, 0)),
                  pl.BlockSpec((tile, D_MODEL), lambda i: (i, 0)),
                  pl.BlockSpec((1, D_MODEL), lambda i: (0, 0)),
                  pl.BlockSpec((1, D_MODEL), lambda i: (0, 0))],
        out_specs=pl.BlockSpec((tile, D_MODEL), lambda i: (i, 0)),
        scratch_shapes=[pltpu.VMEM((TOP_K, tile, D_MODEL), F32), pltpu.SemaphoreType.DMA(())],
        compiler_params=_cparams("arbitrary"), name="moe_combine",
    )(dest_flat, ys, tk, x1, g2, b2)


def _rel_bucket(dist):
    exact = REL_BUCKETS // 2
    d = jnp.maximum(dist, 0)
    log_b = exact + (jnp.log(jnp.maximum(d, 1).astype(F32) / exact)
                     / math.log(REL_MAX_DIST / exact) * (REL_BUCKETS - exact)).astype(I32)
    return jnp.where(d < exact, d, jnp.minimum(log_b, REL_BUCKETS - 1))


def _bias_tables(rel_bias):
    table = rel_bias.astype(F32)
    r = jnp.arange(WINDOW)[:, None]
    c = jnp.arange(2 * WINDOW)[None, :]
    dist = r + WINDOW - c
    general = jnp.where(((dist >= 0) & (dist < WINDOW))[..., None], table[_rel_bucket(dist)], NEG)
    dist0 = jnp.where(c < N_META, N_META + r - c, dist)
    valid0 = jnp.where(c < N_META, dist0 < WINDOW, (c >= WINDOW) & (dist0 >= 0) & (dist0 < WINDOW))
    first = jnp.where(valid0[..., None], table[_rel_bucket(dist0)], NEG)
    both = jnp.moveaxis(jnp.stack([first, general]), -1, 1)
    dist_s = WINDOW - 1 - jnp.arange(WINDOW)
    cols = jnp.pad(table[_rel_bucket(dist_s)], ((0, 0), (0, LANES - A_HEADS)))
    return both, cols


def _perm_heads(a, axis):
    parts = [lax.slice_in_dim(a, h * A_HD, (h + 1) * A_HD, axis=axis) for h in HEAD_ORDER]
    return jnp.concatenate(parts, axis=axis)


def _rep_rows(vec, rows=8):
    out = jnp.zeros((rows, LANES), F32)
    return out.at[:vec.shape[0], :].set(jnp.broadcast_to(vec.astype(F32)[:, None], (vec.shape[0], LANES)))


def kernel(x_prompt, x_sample, cache_swa_k, cache_swa_v, state_mlstm_C, state_mlstm_n, state_mlstm_m, meta_tokens, rel_bias, w_in, b_igate, b_fgate, attn_sinks, g_mlstm_out, g_attn_out, w_out, ln1_g, ln1_b, w_router, b_router, w_moe1, b_moe1, w_moe2, b_moe2, ln2_g, ln2_b):
    B, S, _ = x_prompt.shape
    NB = x_sample.shape[0]
    assert x_sample.shape[1] == 1 and w_in.shape[0] == 1
    assert S % PROJ_TILE == 0 and S % M_CHUNK == 0 and S % WINDOW == 0 and NB % SAMPLE_GROUP == 0
    l = 0

    pts = np.cumsum(IN_WIDTHS)[:-1].tolist()
    w_qm, w_km, w_vm, w_om, w_ig, w_fg, w_qa, w_ka, w_va = jnp.split(w_in[l], pts, axis=1)
    w_gate = jnp.pad(jnp.concatenate([w_ig, w_fg], axis=1), ((0, 0), (0, LANES - 2 * M_HEADS)))
    w_qa = _perm_heads(w_qa, 1)
    b_gate = jnp.concatenate([b_igate[l], b_fgate[l]]).astype(F32)
    brow = jnp.pad(b_gate, (0, LANES - 2 * M_HEADS))[None, :]
    bcol = b_gate[:, None]
    bf = lambda a: a.astype(BF16)
    wr_p = bf(jnp.concatenate([w_qm, w_vm, w_om, w_qa, w_ka, w_va, w_gate], axis=1))
    wt_p = bf(jnp.concatenate([w_km.T, w_ig.T, w_fg.T], axis=0))
    plan_p = ((0, 512, "plain", BF16), (512, 512, "plain", BF16), (1024, 512, "plain", F32),
              (1536, 512, "plain", BF16), (2048, 128, "plain", BF16), (2176, 128, "plain", BF16),
              (2304, 128, "gate", F32))
    tplan_p = ((0, 512, "plain", BF16), (512, 8, "gate", F32))
    wr_s = bf(jnp.concatenate([w_qm, w_km, w_vm, w_om, w_qa, w_ka, w_va, w_gate], axis=1))
    wt_s = bf(w_vm.T)
    plan_s = ((0, 512, "plain", F32), (512, 512, "plain", F32), (1024, 512, "plain", F32), (1536, 512, "plain", F32),
              (2048, 512, "plain", F32), (2560, 128, "plain", F32), (2688, 128, "plain", F32), (2816, 128, "gate", F32))
    tplan_s = ((0, 512, "plain", F32),)

    bias_tab, bias_cols = _bias_tables(rel_bias)
    sinks = _rep_rows(attn_sinks[l])
    g_m = g_mlstm_out[l].astype(F32)[None, :]
    g_a = _perm_heads(g_attn_out[l].astype(F32), 0)[None, :]
    wo = bf(jnp.concatenate([w_out[l][:M_WIDTH], _perm_heads(w_out[l][M_WIDTH:], 0)], axis=0))
    g1, b1 = ln1_g[l].astype(F32)[None, :], ln1_b[l].astype(F32)[None, :]
    g2, b2 = ln2_g[l].astype(F32)[None, :], ln2_b[l].astype(F32)[None, :]
    w_r = bf(jnp.pad(w_router[l], ((0, 0), (0, LANES - N_EXPERTS))))
    b_r = jnp.pad(b_router[l].astype(F32), (0, LANES - N_EXPERTS), constant_values=NEG)[None, :]
    b1g = b_moe1[l][:, 0::2].astype(F32)[:, None, :]
    b1l = b_moe1[l][:, 1::2].astype(F32)[:, None, :]
    b2e = b_moe2[l].astype(F32)[:, None, :]
    pj = np.zeros((256, 256), np.float32)
    pj[2 * np.arange(128), np.arange(128)] = 1.0
    pj[2 * np.arange(128) + 1, 128 + np.arange(128)] = 1.0
    perm = jnp.asarray(pj, BF16)

    xp2 = x_prompt.reshape(B * S, D_MODEL)
    qm, vm, om, qa, ka, va, gc, kt, gr, kv_tail = _proj(
        xp2, wr_p, wt_p, brow, bcol, plan_p, tplan_p, (2048, 256), PROJ_TILE, S, "proj_prompt")
    x_meta = jnp.pad(meta_tokens.astype(F32), ((0, M_CHUNK - N_META), (0, 0)))
    qm0, vm0, _, _, ka0, va0, gc0, kt0, gr0 = _proj(
        x_meta, wr_p, wt_p, brow, bcol, plan_p, tplan_p, None, M_CHUNK, M_CHUNK, "proj_meta")
    xs2 = x_sample.reshape(NB, D_MODEL)
    qm_s, km_s, vm_s, om_s, qa_s, ka_s, va_s, gc_s, vt_s = _proj(
        xs2, wr_s, wt_s, brow, bcol, plan_s, tplan_s, None, NB, NB, "proj_sample")

    zero_c = jnp.zeros((M_HEADS, M_DK, 2 * M_DV), F32)
    zero_m = jnp.zeros((8, LANES), F32)
    _, ct_meta, m_meta = _mlstm(qm0, vm0, kt0, gc0, gr0, zero_c, zero_m, 1, N_META, "mlstm_meta")
    c0 = jnp.swapaxes(ct_meta[0], 1, 2)
    h_p, ct_p, m_p = _mlstm(qm, vm, kt, gc, gr, c0, m_meta[0], B, M_CHUNK, "mlstm_prompt")
    C_p = ct_p[:, :, :M_DV, :]
    n_p = ct_p[:, :, M_DV, :]
    m_prompt = m_p[:, :M_HEADS, 0]
    m_pad = jnp.pad(state_mlstm_m[l].astype(F32), ((0, 0), (0, LANES - M_HEADS)))
    C_s, n_s, m_s, h_s = _mlstm_step(state_mlstm_C[l].astype(F32), state_mlstm_n[l].astype(F32), m_pad,
                                     gc_s, qm_s, km_s, vm_s, vt_s)

    att_p = _swa(qa, ka, va, ka0, va0, bias_tab, sinks, B)
    ck = cache_swa_k[l].reshape(NB, WINDOW, LANES)
    cv = cache_swa_v[l].reshape(NB, WINDOW, LANES)
    k_new, v_new, att_s = _swa_step(ck, cv, qa_s, ka_s, va_s, bias_cols, sinks)

    x1_p, xpk_p, tk_p = _merge(h_p, om, att_p, xp2, g_m, g_a, wo, g1, b1, w_r, b_r, MERGE_TILE, "merge_prompt")
    x1_s, xpk_s, tk_s = _merge(h_s, om_s, att_s, xs2, g_m, g_a, wo, g1, b1, w_r, b_r, NB, "merge_sample")

    T_p = B * S
    assert T_p % RANK_TILE == 0 and T_p % ROW_TILE == 0
    n_blocks = -(-((T_p + NB) * TOP_K) // EXPERT_TILE) + N_EXPERTS
    rank_p, cnt_p = _rank(tk_p, jnp.zeros((8, LANES), F32))
    rank_s, cnt = _rank(tk_s, cnt_p)
    off, be2, nu2 = _offsets(cnt, n_blocks, EXPERT_TILE)
    dest_p = _dest(tk_p, rank_p, off)[:, :TOP_K].reshape(-1)
    dest_s = _dest(tk_s, rank_s, off)[:, :TOP_K].reshape(-1)
    be = be2.reshape(-1)[:n_blocks]
    nu = nu2[0, :1]
    xs = jnp.zeros((n_blocks * EXPERT_TILE, D_MODEL // 2), U32)
    xs = _dispatch(dest_p, xpk_p, xs)
    xs = _dispatch(dest_s, xpk_s, xs)
    ys = _experts(be, nu, xs, w_moe1[l], b1g, b1l, w_moe2[l], b2e, perm, EXPERT_TILE)
    y_p = _combine(dest_p, ys, tk_p, x1_p, g2, b2)
    y_s = _combine(dest_s, ys, tk_s, x1_s, g2, b2)

    kv_tail = kv_tail.reshape(B, WINDOW, 2, A_KV_HEADS, A_HD)
    dt_k, dt_v = cache_swa_k.dtype, cache_swa_v.dtype
    return (y_p.reshape(B, S, D_MODEL).astype(x_prompt.dtype), y_s.reshape(NB, 1, D_MODEL).astype(x_sample.dtype),
            kv_tail[:, :, 0][None].astype(dt_k), kv_tail[:, :, 1][None].astype(dt_v),
            C_p[None].astype(state_mlstm_C.dtype), n_p[None].astype(state_mlstm_n.dtype),
            m_prompt[None].astype(state_mlstm_m.dtype),
            k_new.reshape(1, NB, WINDOW, A_KV_HEADS, A_HD).astype(dt_k),
            v_new.reshape(1, NB, WINDOW, A_KV_HEADS, A_HD).astype(dt_v),
            C_s[None].astype(state_mlstm_C.dtype), n_s[None].astype(state_mlstm_n.dtype),
            m_s[:, :M_HEADS][None].astype(state_mlstm_m.dtype))
```

```python
import functools
import math

import numpy as np
import jax
import jax.numpy as jnp
from jax import lax
from jax.experimental import pallas as pl
from jax.experimental.pallas import tpu as pltpu

F32 = jnp.float32
BF16 = jnp.bfloat16
I32 = jnp.int32
U32 = jnp.uint32

D_MODEL = 1024
N_META = 16
M_HEADS = 4
M_DK = 128
M_DV = 128
M_WIDTH = M_HEADS * M_DV
A_HD = 64
A_HEADS = 8
A_KV_HEADS = 2
A_GROUP = A_HEADS // A_KV_HEADS
A_WIDTH = A_HEADS * A_HD
WINDOW = 128
REL_BUCKETS = 32
REL_MAX_DIST = 128
N_EXPERTS = 32
TOP_K = 4
D_FF = D_MODEL
SWIGLU_LIMIT = 7.0
SWIGLU_ALPHA = 1.702
DEPTH = 1
DN_ALPHA = (2.0 * DEPTH) ** 0.25
LN_EPS = 1e-5
IN_WIDTHS = (M_WIDTH, M_WIDTH, M_WIDTH, M_WIDTH, M_HEADS, M_HEADS, A_WIDTH, A_KV_HEADS * A_HD, A_KV_HEADS * A_HD)

LANES = 128
NEG = -1e30
VMEM_LIMIT = 56 * 1024 * 1024

M_CHUNK = 128
PROJ_TILE = 512
MERGE_TILE = 512
RANK_TILE = 512
ROW_TILE = 256
EXPERT_TILE = 512
SAMPLE_GROUP = 8
ISSUE_GROUP = 8
HEAD_ORDER = (0, 4, 1, 5, 2, 6, 3, 7)


def _cparams(*sem):
    return pltpu.CompilerParams(dimension_semantics=sem, vmem_limit_bytes=VMEM_LIMIT)


def _log_sigmoid(x):
    return jnp.minimum(x, 0.0) - jnp.log1p(jnp.exp(-jnp.abs(x)))


def _sigmoid(x):
    return 1.0 / (1.0 + jnp.exp(-x))


def _proj_kernel(x_ref, wr_ref, wt_ref, brow_ref, bcol_ref, *outs, row_plan, t_plan, tail_cols):
    xb = x_ref[...].astype(BF16)
    tm = xb.shape[0]
    o = 0
    for (c0, width, kind, _) in row_plan:
        r = jnp.dot(xb, wr_ref[:, c0:c0 + width], preferred_element_type=F32)
        if kind == "gate":
            r = r + brow_ref[...]
            lane = lax.broadcasted_iota(I32, r.shape, 1)
            r = jnp.where(lane < M_HEADS, r, _log_sigmoid(r))
        outs[o][...] = r.astype(outs[o].dtype)
        o += 1
    for (r0, nrows, kind, _) in t_plan:
        r = lax.dot_general(wt_ref[r0:r0 + nrows, :], xb, (((1,), (1,)), ((), ())), preferred_element_type=F32)
        if kind == "gate":
            r = r + bcol_ref[...]
            row = lax.broadcasted_iota(I32, r.shape, 0)
            r = jnp.where(row < M_HEADS, r, _log_sigmoid(r))
        outs[o][...] = r.astype(outs[o].dtype)
        o += 1
    if tail_cols is not None:
        c0, width = tail_cols
        outs[o][...] = jnp.dot(xb[tm - WINDOW:, :], wr_ref[:, c0:c0 + width], preferred_element_type=F32)


def _proj(x, wr, wt, brow, bcol, row_plan, t_plan, tail_cols, tile, rows_per_group, name):
    t = x.shape[0]
    nt = t // tile
    out_shape, out_specs = [], []
    for (_, width, _, dt) in row_plan:
        out_shape.append(jax.ShapeDtypeStruct((t, width), dt))
        out_specs.append(pl.BlockSpec((tile, width), lambda i: (i, 0)))
    for (_, nrows, _, dt) in t_plan:
        out_shape.append(jax.ShapeDtypeStruct((nrows, t), dt))
        out_specs.append(pl.BlockSpec((nrows, tile), lambda i: (0, i)))
    if tail_cols is not None:
        tiles_per_group = rows_per_group // tile
        out_shape.append(jax.ShapeDtypeStruct((t // rows_per_group * WINDOW, tail_cols[1]), F32))
        out_specs.append(pl.BlockSpec((WINDOW, tail_cols[1]), lambda i: (i // tiles_per_group, 0)))
    kern = functools.partial(_proj_kernel, row_plan=row_plan, t_plan=t_plan, tail_cols=tail_cols)
    return pl.pallas_call(
        kern, out_shape=out_shape, grid=(nt,),
        in_specs=[pl.BlockSpec((tile, D_MODEL), lambda i: (i, 0)),
                  pl.BlockSpec(wr.shape, lambda i: (0, 0)),
                  pl.BlockSpec(wt.shape, lambda i: (0, 0)),
                  pl.BlockSpec(brow.shape, lambda i: (0, 0)),
                  pl.BlockSpec(bcol.shape, lambda i: (0, 0))],
        out_specs=out_specs, compiler_params=_cparams("arbitrary"), name=name,
    )(x, wr, wt, brow, bcol)


def _split3(a):
    hi = a.astype(BF16)
    r1 = a - hi.astype(F32)
    mid = r1.astype(BF16)
    lo = (r1 - mid.astype(F32)).astype(BF16)
    return hi, mid, lo


def _mlstm_kernel(q_ref, v_ref, kt_ref, gc_ref, gr_ref, c0_ref, m0_ref, h_ref, ct_out_ref, m_out_ref,
                  ct_scr, m_scr, *, n_valid):
    c = pl.program_id(1)
    nc = pl.num_programs(1)
    L = q_ref.shape[0]

    @pl.when(c == 0)
    def _():
        ct_scr[...] = c0_ref[...]
        m_scr[...] = m0_ref[...]

    gc = gc_ref[...]
    gr = gr_ref[...]
    if n_valid < L:
        rowc = lax.broadcasted_iota(I32, gc.shape, 0)
        lanec = lax.broadcasted_iota(I32, gc.shape, 1)
        gc = jnp.where(rowc < n_valid, gc, jnp.where(lanec < M_HEADS, NEG, 0.0))
        rowr = lax.broadcasted_iota(I32, gr.shape, 0)
        colr = lax.broadcasted_iota(I32, gr.shape, 1)
        gr = jnp.where(colr < n_valid, gr, jnp.where(rowr < M_HEADS, NEG, 0.0))
    r_i = lax.broadcasted_iota(I32, (L, L), 0)
    c_i = lax.broadcasted_iota(I32, (L, L), 1)
    causal = c_i <= r_i
    tril = jnp.where(causal, 1.0, 0.0).astype(BF16)
    triu = jnp.where(r_i <= c_i, 1.0, 0.0).astype(BF16)
    b_cols = sum(jnp.dot(tril, part, preferred_element_type=F32) for part in _split3(gc))
    b_rows = sum(jnp.dot(part, triu, preferred_element_type=F32) for part in _split3(gr))
    lane_l = lax.broadcasted_iota(I32, (L, LANES), 1)
    e0 = jnp.where(lane_l == 0, 1.0, 0.0)
    scale = M_DK ** -0.5

    for h in range(M_HEADS):
        sl = slice(h * M_DK, (h + 1) * M_DK)
        q = q_ref[:, sl]
        v = v_ref[:, sl]
        kt = kt_ref[sl, :]
        ig_c = gc[:, h:h + 1]
        b_c = b_cols[:, M_HEADS + h:M_HEADS + h + 1]
        ig_r = gr[h:h + 1, :]
        b_r = b_rows[M_HEADS + h:M_HEADS + h + 1, :]
        m_prev = m_scr[h:h + 1, 0:1]
        ct = ct_scr[h]

        d = jnp.where(causal, b_c + (ig_r - b_r), NEG)
        m_t = jnp.maximum(b_c + m_prev, jnp.max(d, axis=1, keepdims=True))
        qk = jnp.dot(q, kt, preferred_element_type=F32) * scale
        s = qk * jnp.exp(d - m_t)
        inter = jnp.dot(q, ct.astype(BF16), preferred_element_type=F32)
        v_aug = jnp.concatenate([v, e0.astype(BF16)], axis=1)
        intra = jnp.dot(s.astype(BF16), v_aug, preferred_element_type=F32)
        nd = jnp.exp(b_c + m_prev - m_t) * inter + intra
        den = nd[:, M_DV:M_DV + 1]
        h_ref[:, sl] = nd[:, :M_DV] / jnp.maximum(jnp.abs(den), jnp.exp(-m_t))

        b_last = b_c[L - 1:L, :]
        g = ig_c + b_last - b_c
        m_new = jnp.maximum(b_last + m_prev, jnp.max(g, axis=0, keepdims=True))
        a = jnp.exp(b_last + m_prev - m_new)
        wg = jnp.exp(g - m_new)
        wv = jnp.concatenate([(v.astype(F32) * wg).astype(BF16), (e0 * wg).astype(BF16)], axis=1)
        upd = jnp.dot(kt, wv, preferred_element_type=F32)
        ct_scr[h] = a * ct + upd * scale
        m_scr[h:h + 1, :] = jnp.broadcast_to(m_new, (1, LANES))

    @pl.when(c == nc - 1)
    def _():
        for h in range(M_HEADS):
            ct_out_ref[0, h] = ct_scr[h].T
        m_out_ref[0] = m_scr[...]


def _mlstm(qm, vm, kt, gc, gr, c0, m0, batch, n_valid, name):
    L = M_CHUNK
    nc = qm.shape[0] // (batch * L)
    kern = functools.partial(_mlstm_kernel, n_valid=n_valid)
    return pl.pallas_call(
        kern,
        out_shape=[jax.ShapeDtypeStruct((batch * nc * L, M_WIDTH), F32),
                   jax.ShapeDtypeStruct((batch, M_HEADS, 2 * M_DV, M_DK), F32),
                   jax.ShapeDtypeStruct((batch, 8, LANES), F32)],
        grid=(batch, nc),
        in_specs=[pl.BlockSpec((L, M_WIDTH), lambda b, c: (b * nc + c, 0)),
                  pl.BlockSpec((L, M_WIDTH), lambda b, c: (b * nc + c, 0)),
                  pl.BlockSpec((M_WIDTH, L), lambda b, c: (0, b * nc + c)),
                  pl.BlockSpec((L, LANES), lambda b, c: (b * nc + c, 0)),
                  pl.BlockSpec((8, L), lambda b, c: (0, b * nc + c)),
                  pl.BlockSpec((M_HEADS, M_DK, 2 * M_DV), lambda b, c: (0, 0, 0)),
                  pl.BlockSpec((8, LANES), lambda b, c: (0, 0))],
        out_specs=[pl.BlockSpec((L, M_WIDTH), lambda b, c: (b * nc + c, 0)),
                   pl.BlockSpec((1, M_HEADS, 2 * M_DV, M_DK), lambda b, c: (b, 0, 0, 0)),
                   pl.BlockSpec((1, 8, LANES), lambda b, c: (b, 0, 0))],
        scratch_shapes=[pltpu.VMEM((M_HEADS, M_DK, 2 * M_DV), F32), pltpu.VMEM((8, LANES), F32)],
        compiler_params=_cparams("arbitrary", "arbitrary"), name=name,
    )(qm, vm, kt, gc, gr, c0, m0)


def _mlstm_step_kernel(c_ref, n_ref, m_ref, gc_ref, q_ref, k_ref, v_ref, vt_ref,
                       c_out_ref, n_out_ref, m_out_ref, h_ref):
    i = pl.program_id(0)
    g = c_ref.shape[0]
    scale = M_DK ** -0.5
    lane_b = lax.broadcasted_iota(I32, (M_DV, vt_ref.shape[1]), 1)
    lane_m = lax.broadcasted_iota(I32, (1, LANES), 1)
    for j in range(g):
        m_row = jnp.zeros((1, LANES), F32)
        for h in range(M_HEADS):
            sl = slice(h * M_DK, (h + 1) * M_DK)
            q = q_ref[j:j + 1, sl]
            k = k_ref[j:j + 1, sl] * scale
            v = v_ref[j:j + 1, sl]
            ig = gc_ref[j:j + 1, h:h + 1]
            lf = gc_ref[j:j + 1, M_HEADS + h:M_HEADS + h + 1]
            m = m_ref[j:j + 1, h:h + 1]
            c = c_ref[j, h]
            n = n_ref[j, h:h + 1, :]
            m_t = jnp.maximum(lf + m, ig)
            w = jnp.exp(lf + m - m_t)
            wg = jnp.exp(ig - m_t)
            s = jnp.sum(q * k, axis=1, keepdims=True) * wg
            q8 = jnp.broadcast_to(q, (8, M_DK)).astype(BF16)
            cq = lax.dot_general(q8, c.astype(BF16), (((1,), (1,)), ((), ())), preferred_element_type=F32)[0:1, :]
            den = w * jnp.sum(n * q, axis=1, keepdims=True) + s
            h_ref[j:j + 1, sl] = (w * cq + s * v) / jnp.maximum(jnp.abs(den), jnp.exp(-m_t))
            v_col = jnp.sum(jnp.where(lane_b == i * g + j, vt_ref[sl, :], 0.0), axis=1, keepdims=True)
            c_out_ref[j, h] = w * c + (wg * v_col) * k
            n_out_ref[j, h:h + 1, :] = w * n + wg * k
            m_row = jnp.where(lane_m == h, m_t, m_row)
        m_out_ref[j:j + 1, :] = m_row


def _mlstm_step(c, n, m_pad, gc, q, k, v, vt):
    nb = c.shape[0]
    g = SAMPLE_GROUP
    row = lambda w: pl.BlockSpec((g, w), lambda i: (i, 0))
    return pl.pallas_call(
        _mlstm_step_kernel,
        out_shape=[jax.ShapeDtypeStruct(c.shape, F32), jax.ShapeDtypeStruct(n.shape, F32),
                   jax.ShapeDtypeStruct((nb, LANES), F32), jax.ShapeDtypeStruct((nb, M_WIDTH), F32)],
        grid=(nb // g,),
        in_specs=[pl.BlockSpec((g, M_HEADS, M_DV, M_DK), lambda i: (i, 0, 0, 0)),
                  pl.BlockSpec((g, M_HEADS, M_DK), lambda i: (i, 0, 0)),
                  row(LANES), row(LANES), row(M_WIDTH), row(M_WIDTH), row(M_WIDTH),
                  pl.BlockSpec(vt.shape, lambda i: (0, 0))],
        out_specs=[pl.BlockSpec((g, M_HEADS, M_DV, M_DK), lambda i: (i, 0, 0, 0)),
                   pl.BlockSpec((g, M_HEADS, M_DK), lambda i: (i, 0, 0)),
                   row(LANES), row(M_WIDTH)],
        compiler_params=_cparams("arbitrary"), name="mlstm_step",
    )(c, n, m_pad, gc, q, k, v, vt)


def _swa_kernel(q_ref, kc_ref, kp_ref, vc_ref, vp_ref, km_ref, vm_ref, bias_ref, sink_ref, o_ref):
    j = pl.program_id(1)
    first = j == 0
    kp = jnp.where(first, km_ref[...], kp_ref[...])
    vp = jnp.where(first, vm_ref[...], vp_ref[...])
    k = jnp.concatenate([kp, kc_ref[...]], axis=0)
    v = jnp.concatenate([vp, vc_ref[...]], axis=0)
    lane = lax.broadcasted_iota(I32, k.shape, 1)
    zero = jnp.zeros_like(k)
    k_half = (jnp.where(lane < A_HD, k, zero), jnp.where(lane >= A_HD, k, zero))
    v_stack = jnp.concatenate([jnp.where(lane < A_HD, v, zero), jnp.where(lane >= A_HD, v, zero)], axis=0)
    lane_q = lax.broadcasted_iota(I32, (q_ref.shape[0], LANES), 1)
    for p in range(A_GROUP):
        qs = q_ref[:, p * LANES:(p + 1) * LANES]
        probs, inv = [], []
        for half in range(2):
            hd = HEAD_ORDER[2 * p + half]
            s = lax.dot_general(qs, k_half[half], (((1,), (1,)), ((), ())), preferred_element_type=F32)
            s = s * (A_HD ** -0.5) + bias_ref[0, hd]
            sk = sink_ref[hd:hd + 1, 0:1]
            m = jnp.maximum(jnp.max(s, axis=1, keepdims=True), sk)
            e = jnp.exp(s - m)
            probs.append(e.astype(BF16))
            inv.append(1.0 / (jnp.sum(e, axis=1, keepdims=True) + jnp.exp(sk - m)))
        o = jnp.dot(jnp.concatenate(probs, axis=1), v_stack, preferred_element_type=F32)
        o_ref[:, p * LANES:(p + 1) * LANES] = o * jnp.where(lane_q < A_HD, inv[0], inv[1])


def _swa(qa, ka, va, kmeta, vmeta, bias, sinks, batch):
    blk = WINDOW
    nq = qa.shape[0] // (batch * blk)
    kv_cur = pl.BlockSpec((blk, LANES), lambda b, j: (b * nq + j, 0))
    kv_prev = pl.BlockSpec((blk, LANES), lambda b, j: (b * nq + jnp.maximum(j - 1, 0), 0))
    const2 = lambda shape: pl.BlockSpec(shape, lambda b, j: (0, 0))
    return pl.pallas_call(
        _swa_kernel, out_shape=jax.ShapeDtypeStruct((qa.shape[0], A_WIDTH), F32), grid=(batch, nq),
        in_specs=[pl.BlockSpec((blk, A_WIDTH), lambda b, j: (b * nq + j, 0)),
                  kv_cur, kv_prev, kv_cur, kv_prev, const2((blk, LANES)), const2((blk, LANES)),
                  pl.BlockSpec((1, A_HEADS, blk, 2 * blk), lambda b, j: (jnp.minimum(j, 1), 0, 0, 0)),
                  const2((8, LANES))],
        out_specs=pl.BlockSpec((blk, A_WIDTH), lambda b, j: (b * nq + j, 0)),
        compiler_params=_cparams("arbitrary", "arbitrary"), name="swa_prompt",
    )(qa, ka, ka, va, va, kmeta, vmeta, bias, sinks)


def _swa_step_kernel(ck_ref, cv_ref, q_ref, k_ref, v_ref, bias_ref, sink_ref, ko_ref, vo_ref, o_ref):
    g = ck_ref.shape[0]
    lane = lax.broadcasted_iota(I32, (WINDOW, LANES), 1)
    lane_q = lax.broadcasted_iota(I32, (1, LANES), 1)
    lo = lane < A_HD
    for j in range(g):
        ko_ref[j, 0:WINDOW - 1, :] = ck_ref[j, 1:WINDOW, :]
        ko_ref[j, WINDOW - 1:WINDOW, :] = k_ref[j:j + 1, :]
        vo_ref[j, 0:WINDOW - 1, :] = cv_ref[j, 1:WINDOW, :]
        vo_ref[j, WINDOW - 1:WINDOW, :] = v_ref[j:j + 1, :]
        kk = ko_ref[j]
        vv = vo_ref[j]
        for p in range(A_GROUP):
            prod = kk * q_ref[j:j + 1, p * LANES:(p + 1) * LANES]
            e2, inv2 = [], []
            for half in range(2):
                hd = HEAD_ORDER[2 * p + half]
                sel = lo if half == 0 else jnp.logical_not(lo)
                s = jnp.sum(jnp.where(sel, prod, 0.0), axis=1, keepdims=True) * (A_HD ** -0.5)
                s = s + bias_ref[:, hd:hd + 1]
                sk = sink_ref[hd:hd + 1, 0:1]
                m = jnp.maximum(jnp.max(s, axis=0, keepdims=True), sk)
                e = jnp.exp(s - m)
                e2.append(e)
                inv2.append(1.0 / (jnp.sum(e, axis=0, keepdims=True) + jnp.exp(sk - m)))
            o = jnp.sum(jnp.where(lo, e2[0], e2[1]) * vv, axis=0, keepdims=True)
            o_ref[j:j + 1, p * LANES:(p + 1) * LANES] = o * jnp.where(lane_q < A_HD, inv2[0], inv2[1])


def _swa_step(ck, cv, q, k, v, bias_cols, sinks):
    nb = ck.shape[0]
    g = SAMPLE_GROUP
    cache = pl.BlockSpec((g, WINDOW, LANES), lambda i: (i, 0, 0))
    row = lambda w: pl.BlockSpec((g, w), lambda i: (i, 0))
    const = lambda a: pl.BlockSpec(a.shape, lambda i: (0, 0))
    return pl.pallas_call(
        _swa_step_kernel,
        out_shape=[jax.ShapeDtypeStruct(ck.shape, F32), jax.ShapeDtypeStruct(cv.shape, F32),
                   jax.ShapeDtypeStruct((nb, A_WIDTH), F32)],
        grid=(nb // g,),
        in_specs=[cache, cache, row(A_WIDTH), row(LANES), row(LANES), const(bias_cols), const(sinks)],
        out_specs=[cache, cache, row(A_WIDTH)],
        compiler_params=_cparams("arbitrary"), name="swa_step",
    )(ck, cv, q, k, v, bias_cols, sinks)


def _layer_norm(z, g, b):
    mu = jnp.mean(z, axis=1, keepdims=True)
    zc = z - mu
    var = jnp.mean(zc * zc, axis=1, keepdims=True)
    return zc * lax.rsqrt(var + LN_EPS) * g + b


def _pack_halves(x):
    w = x.shape[1] // 2
    lo = pltpu.bitcast(x[:, :w].astype(BF16).astype(F32), U32)
    hi = pltpu.bitcast(x[:, w:].astype(BF16).astype(F32), U32)
    return (lo >> 16) | (hi & jnp.uint32(0xFFFF0000))


def _unpack_halves(words):
    lo = pltpu.bitcast(words << 16, F32).astype(BF16)
    hi = pltpu.bitcast(words & jnp.uint32(0xFFFF0000), F32).astype(BF16)
    return lo, hi


def _merge_kernel(h_ref, om_ref, att_ref, x_ref, gm_ref, ga_ref, wo_ref, g1_ref, b1_ref, wr_ref, br_ref,
                  x1_ref, xp_ref, tk_ref):
    hm = h_ref[...] * _sigmoid(om_ref[...])
    ym = hm * lax.rsqrt(jnp.mean(hm * hm, axis=1, keepdims=True) + LN_EPS) * gm_ref[...]
    att = att_ref[...]
    ya = att * lax.rsqrt(jnp.mean(att * att, axis=1, keepdims=True) + LN_EPS) * ga_ref[...]
    mix = (jnp.dot(ym.astype(BF16), wo_ref[0:M_WIDTH, :], preferred_element_type=F32)
           + jnp.dot(ya.astype(BF16), wo_ref[M_WIDTH:, :], preferred_element_type=F32))
    x1 = _layer_norm(DN_ALPHA * x_ref[...] + mix, g1_ref[...], b1_ref[...])
    x1_ref[...] = x1
    xp_ref[...] = _pack_halves(x1)
    logits = jnp.dot(x1.astype(BF16), wr_ref[...], preferred_element_type=F32) + br_ref[...]
    lane = lax.broadcasted_iota(I32, logits.shape, 1).astype(F32)
    vals, idxs = [], []
    for _ in range(TOP_K):
        mx = jnp.max(logits, axis=1, keepdims=True)
        idx = jnp.min(jnp.where(logits == mx, lane, float(LANES)), axis=1, keepdims=True)
        vals.append(mx)
        idxs.append(idx)
        logits = jnp.where(lane == idx, 2.0 * NEG, logits)
    es = [jnp.exp(vk - vals[0]) for vk in vals]
    tot = es[0] + es[1] + es[2] + es[3]
    tk = jnp.zeros(logits.shape, F32)
    for k in range(TOP_K):
        tk = jnp.where(lane == float(k), es[k] / tot, tk)
        tk = jnp.where(lane == float(TOP_K + k), idxs[k], tk)
    tk_ref[...] = tk


def _merge(h, om, att, x, gm, ga, wo, g1, b1, wr, br, tile, name):
    t = x.shape[0]
    rows = lambda w: pl.BlockSpec((tile, w), lambda i: (i, 0))
    const = lambda a: pl.BlockSpec(a.shape, lambda i: (0, 0))
    return pl.pallas_call(
        _merge_kernel,
        out_shape=[jax.ShapeDtypeStruct((t, D_MODEL), F32), jax.ShapeDtypeStruct((t, D_MODEL // 2), U32),
                   jax.ShapeDtypeStruct((t, LANES), F32)],
        grid=(t // tile,),
        in_specs=[rows(M_WIDTH), rows(M_WIDTH), rows(A_WIDTH), rows(D_MODEL), const(gm), const(ga), const(wo),
                  const(g1), const(b1), const(wr), const(br)],
        out_specs=[rows(D_MODEL), rows(D_MODEL // 2), rows(LANES)],
        compiler_params=_cparams("arbitrary"), name=name,
    )(h, om, att, x, gm, ga, wo, g1, b1, wr, br)


def _rank_kernel(tk_ref, base_ref, rank_ref, cnt_ref, base_scr):
    i = pl.program_id(0)

    @pl.when(i == 0)
    def _():
        base_scr[...] = base_ref[...]

    tk = tk_ref[...]
    t = tk.shape[0]
    lane = lax.broadcasted_iota(I32, tk.shape, 1).astype(F32)
    onehots = [jnp.where(lane == tk[:, TOP_K + k:TOP_K + k + 1], 1.0, 0.0) for k in range(TOP_K)]
    tot = onehots[0] + onehots[1] + onehots[2] + onehots[3]
    r_i = lax.broadcasted_iota(I32, (t, t), 0)
    c_i = lax.broadcasted_iota(I32, (t, t), 1)
    strict = jnp.where(c_i < r_i, 1.0, 0.0).astype(BF16)
    before = jnp.dot(strict, tot.astype(BF16), preferred_element_type=F32) + base_scr[0:1, :]
    out = jnp.zeros(tk.shape, F32)
    for k in range(TOP_K):
        out = jnp.where(lane == float(k), jnp.sum(onehots[k] * before, axis=1, keepdims=True), out)
    rank_ref[...] = out
    base_scr[...] = base_scr[...] + jnp.sum(tot, axis=0, keepdims=True)
    cnt_ref[...] = base_scr[...]


def _rank(tk, base):
    t = tk.shape[0]
    tile = min(RANK_TILE, t)
    return pl.pallas_call(
        _rank_kernel,
        out_shape=[jax.ShapeDtypeStruct((t, LANES), F32), jax.ShapeDtypeStruct((8, LANES), F32)],
        grid=(t // tile,),
        in_specs=[pl.BlockSpec((tile, LANES), lambda i: (i, 0)), pl.BlockSpec((8, LANES), lambda i: (0, 0))],
        out_specs=[pl.BlockSpec((tile, LANES), lambda i: (i, 0)), pl.BlockSpec((8, LANES), lambda i: (0, 0))],
        scratch_shapes=[pltpu.VMEM((8, LANES), F32)],
        compiler_params=_cparams("arbitrary"), name="moe_rank",
    )(tk, base)


def _offsets_kernel(cnt_ref, off_ref, be_ref, nu_ref, *, tile):
    cnt = cnt_ref[...]
    nblk = jnp.floor((cnt + float(tile - 1)) * (1.0 / tile))
    r_i = lax.broadcasted_iota(I32, (LANES, LANES), 0)
    c_i = lax.broadcasted_iota(I32, (LANES, LANES), 1)
    incl = jnp.where(r_i <= c_i, 1.0, 0.0).astype(BF16)
    cum = jnp.dot(nblk.astype(BF16), incl, preferred_element_type=F32)
    off_ref[...] = (cum - nblk) * float(tile)
    rows = be_ref.shape[0]
    jb = (lax.broadcasted_iota(I32, (rows, LANES), 0) * LANES + lax.broadcasted_iota(I32, (rows, LANES), 1)).astype(F32)
    acc = jnp.zeros((rows, LANES), F32)
    for e in range(N_EXPERTS):
        acc = acc + jnp.where(jb >= cum[0:1, e:e + 1], 1.0, 0.0)
    be_ref[...] = jnp.minimum(acc, float(N_EXPERTS - 1)).astype(I32)
    nu_ref[...] = jnp.broadcast_to(cum[0:1, N_EXPERTS - 1:N_EXPERTS], nu_ref.shape).astype(I32)


def _offsets(cnt, n_blocks, tile):
    rows = -(-n_blocks // LANES)
    rows = -(-rows // 8) * 8
    return pl.pallas_call(
        functools.partial(_offsets_kernel, tile=tile),
        out_shape=[jax.ShapeDtypeStruct((8, LANES), F32), jax.ShapeDtypeStruct((rows, LANES), I32),
                   jax.ShapeDtypeStruct((8, LANES), I32)],
        name="moe_offsets",
    )(cnt)


def _dest_kernel(tk_ref, rank_ref, off_ref, dest_ref):
    tk = tk_ref[...]
    rank = rank_ref[...]
    lane = lax.broadcasted_iota(I32, tk.shape, 1).astype(F32)
    off = off_ref[0:1, :]
    out = jnp.zeros(tk.shape, F32)
    for k in range(TOP_K):
        first = jnp.sum(jnp.where(lane == tk[:, TOP_K + k:TOP_K + k + 1], off, 0.0), axis=1, keepdims=True)
        out = jnp.where(lane == float(k), first + rank[:, k:k + 1], out)
    dest_ref[...] = out.astype(I32)


def _dest(tk, rank, off):
    t = tk.shape[0]
    tile = min(RANK_TILE, t)
    blk = pl.BlockSpec((tile, LANES), lambda i: (i, 0))
    return pl.pallas_call(
        _dest_kernel, out_shape=jax.ShapeDtypeStruct((t, LANES), I32), grid=(t // tile,),
        in_specs=[blk, blk, pl.BlockSpec((8, LANES), lambda i: (0, 0))], out_specs=blk,
        compiler_params=_cparams("arbitrary"), name="moe_dest",
    )(tk, rank, off)


def _dispatch_kernel(dest_ref, xp_ref, xs_in_ref, xs_ref, sem):
    del xs_in_ref
    t = xp_ref.shape[0]

    def row_copy(tok, dst):
        return pltpu.make_async_copy(xp_ref.at[pl.ds(tok, 1), :], xs_ref.at[pl.ds(dst, 1), :], sem)

    def issue(grp, carry):
        base = pl.multiple_of(grp * ISSUE_GROUP, ISSUE_GROUP)
        for u in range(ISSUE_GROUP):
            for k in range(TOP_K):
                row_copy(base + u, dest_ref[(base + u) * TOP_K + k]).start(priority=k % 2)
        return carry

    lax.fori_loop(0, t // ISSUE_GROUP, issue, 0)
    for k in range(TOP_K):
        pltpu.make_async_copy(xp_ref, xs_ref.at[pl.ds(0, t), :], sem).wait()


def _dispatch(dest_flat, xp, xs):
    t = xp.shape[0]
    tile = min(ROW_TILE, t)
    return pl.pallas_call(
        _dispatch_kernel, out_shape=jax.ShapeDtypeStruct(xs.shape, xs.dtype), grid=(t // tile,),
        in_specs=[pl.BlockSpec((tile * TOP_K,), lambda i: (i,), memory_space=pltpu.SMEM),
                  pl.BlockSpec((tile, xp.shape[1]), lambda i: (i, 0)),
                  pl.BlockSpec(memory_space=pl.ANY)],
        out_specs=pl.BlockSpec(memory_space=pl.ANY),
        scratch_shapes=[pltpu.SemaphoreType.DMA(())],
        input_output_aliases={2: 0},
        compiler_params=_cparams("arbitrary"), name="moe_dispatch",
    )(dest_flat, xp, xs)


def _expert_kernel(be_ref, nu_ref, xs_ref, w1_ref, b1g_ref, b1l_ref, w2_ref, b2_ref, perm_ref, ys_ref,
                   w1g_scr, w1l_scr, w2_scr):
    j = pl.program_id(0)
    active = j < nu_ref[0]
    changed = jnp.logical_or(j == 0, be_ref[j] != be_ref[jnp.maximum(j - 1, 0)])
    half = D_MODEL // 2

    @pl.when(jnp.logical_and(active, changed))
    def _():
        for c in range(2 * D_FF // 256):
            wc = w1_ref[0, :, c * 256:(c + 1) * 256].astype(BF16)
            d = jnp.dot(wc, perm_ref[...], preferred_element_type=F32).astype(BF16)
            w1g_scr[:, c * 128:(c + 1) * 128] = d[:, :128]
            w1l_scr[:, c * 128:(c + 1) * 128] = d[:, 128:]
        for c in range(D_FF // 256):
            w2_scr[c * 256:(c + 1) * 256, :] = w2_ref[0, c * 256:(c + 1) * 256, :].astype(BF16)

    @pl.when(active)
    def _():
        lo, hi = _unpack_halves(xs_ref[...])
        y = jnp.zeros(ys_ref.shape, F32)
        nchunk = 512
        for c in range(D_FF // nchunk):
            cs = slice(c * nchunk, (c + 1) * nchunk)
            hg = (jnp.dot(lo, w1g_scr[0:half, cs], preferred_element_type=F32)
                  + jnp.dot(hi, w1g_scr[half:, cs], preferred_element_type=F32) + b1g_ref[0, :, cs])
            hl = (jnp.dot(lo, w1l_scr[0:half, cs], preferred_element_type=F32)
                  + jnp.dot(hi, w1l_scr[half:, cs], preferred_element_type=F32) + b1l_ref[0, :, cs])
            x_glu = jnp.minimum(hg, SWIGLU_LIMIT)
            x_lin = jnp.clip(hl, -SWIGLU_LIMIT, SWIGLU_LIMIT)
            a = x_glu * _sigmoid(SWIGLU_ALPHA * x_glu) * (x_lin + 1.0)
            y = y + jnp.dot(a.astype(BF16), w2_scr[cs, :], preferred_element_type=F32)
        ys_ref[...] = y + b2_ref[0]

    @pl.when(jnp.logical_not(active))
    def _():
        ys_ref[...] = jnp.zeros_like(ys_ref)


def _experts(be, nu, xs, w1, b1g, b1l, w2, b2, perm, tile):
    n_blocks = xs.shape[0] // tile
    last = lambda j, be, nu: jnp.minimum(j, nu[0] - 1)
    grid_spec = pltpu.PrefetchScalarGridSpec(
        num_scalar_prefetch=2, grid=(n_blocks,),
        in_specs=[pl.BlockSpec((tile, xs.shape[1]), lambda j, be, nu: (last(j, be, nu), 0)),
                  pl.BlockSpec((1, D_MODEL, 2 * D_FF), lambda j, be, nu: (be[j], 0, 0)),
                  pl.BlockSpec((1, 1, D_FF), lambda j, be, nu: (be[j], 0, 0)),
                  pl.BlockSpec((1, 1, D_FF), lambda j, be, nu: (be[j], 0, 0)),
                  pl.BlockSpec((1, D_FF, D_MODEL), lambda j, be, nu: (be[j], 0, 0)),
                  pl.BlockSpec((1, 1, D_MODEL), lambda j, be, nu: (be[j], 0, 0)),
                  pl.BlockSpec((256, 256), lambda j, be, nu: (0, 0))],
        out_specs=pl.BlockSpec((tile, D_MODEL), lambda j, be, nu: (j, 0)),
        scratch_shapes=[pltpu.VMEM((D_MODEL, D_FF), BF16), pltpu.VMEM((D_MODEL, D_FF), BF16),
                        pltpu.VMEM((D_FF, D_MODEL), BF16)])
    return pl.pallas_call(
        _expert_kernel, out_shape=jax.ShapeDtypeStruct((xs.shape[0], D_MODEL), F32), grid_spec=grid_spec,
        compiler_params=_cparams("arbitrary"), name="moe_experts",
    )(be, nu, xs, w1, b1g, b1l, w2, b2, perm)


def _combine_kernel(dest_ref, ys_ref, tk_ref, x1_ref, g2_ref, b2_ref, out_ref, buf, sem):
    t = x1_ref.shape[0]

    def issue(grp, carry):
        base = pl.multiple_of(grp * ISSUE_GROUP, ISSUE_GROUP)
        for u in range(ISSUE_GROUP):
            for k in range(TOP_K):
                pltpu.make_async_copy(ys_ref.at[pl.ds(dest_ref[(base + u) * TOP_K + k], 1), :],
                                      buf.at[k, pl.ds(base + u, 1), :], sem).start(priority=k % 2)
        return carry

    lax.fori_loop(0, t // ISSUE_GROUP, issue, 0)
    for k in range(TOP_K):
        pltpu.make_async_copy(ys_ref.at[pl.ds(0, t), :], buf.at[k], sem).wait()
    tk = tk_ref[...]
    ff = tk[:, 0:1] * buf[0]
    for k in range(1, TOP_K):
        ff = ff + tk[:, k:k + 1] * buf[k]
    out_ref[...] = _layer_norm(DN_ALPHA * x1_ref[...] + ff, g2_ref[...], b2_ref[...])


def _combine(dest_flat, ys, tk, x1, g2, b2):
    t = x1.shape[0]
    tile = min(ROW_TILE, t)
    return pl.pallas_call(
        _combine_kernel, out_shape=jax.ShapeDtypeStruct((t, D_MODEL), F32), grid=(t // tile,),
        in_specs=[pl.BlockSpec((tile * TOP_K,), lambda i: (i,), memory_space=pltpu.SMEM),
                  pl.BlockSpec(memory_space=pl.ANY),
                  pl.BlockSpec((tile, LANES), lambda i: (i, 0)),
                  pl.BlockSpec((tile, D_MODEL), lambda i: (i, 0)),
                  pl.BlockSpec((1, D_MODEL), lambda i: (0, 0)),
                  pl.BlockSpec((1, D_MODEL), lambda i: (0, 0))],
        out_specs=pl.BlockSpec((tile, D_MODEL), lambda i: (i, 0)),
        scratch_shapes=[pltpu.VMEM((TOP_K, tile, D_MODEL), F32), pltpu.SemaphoreType.DMA(())],
        compiler_params=_cparams("arbitrary"), name="moe_combine",
    )(dest_flat, ys, tk, x1, g2, b2)


def _rel_bucket(dist):
    exact = REL_BUCKETS // 2
    d = np.maximum(dist, 0)
    log_b = exact + (np.log(np.maximum(d, 1).astype(np.float32) / np.float32(exact))
                     / np.float32(math.log(REL_MAX_DIST / exact)) * np.float32(REL_BUCKETS - exact)).astype(np.int32)
    return np.where(d < exact, d, np.minimum(log_b, REL_BUCKETS - 1)).astype(np.int32)


def _bias_lookup(table, bucket, valid):
    bucket = jnp.asarray(bucket)[None]
    acc = jnp.zeros((table.shape[1],) + bucket.shape[1:], F32)
    for b in range(REL_BUCKETS):
        acc = jnp.where(bucket == b, table[b].reshape((-1,) + (1,) * (bucket.ndim - 1)), acc)
    return jnp.where(jnp.asarray(valid)[None], acc, NEG)


def _bias_tables(rel_bias):
    table = rel_bias.astype(F32)
    r = np.arange(WINDOW)[:, None]
    c = np.arange(2 * WINDOW)[None, :]
    dist = r + WINDOW - c
    valid = (dist >= 0) & (dist < WINDOW)
    dist0 = np.where(c < N_META, N_META + r - c, dist)
    valid0 = np.where(c < N_META, dist0 < WINDOW, (c >= WINDOW) & valid)
    both = jnp.stack([_bias_lookup(table, _rel_bucket(dist0), valid0), _bias_lookup(table, _rel_bucket(dist), valid)])
    dist_s = WINDOW - 1 - np.arange(WINDOW)
    cols = _bias_lookup(table, _rel_bucket(dist_s), np.ones_like(dist_s, bool)).T
    return both, jnp.pad(cols, ((0, 0), (0, LANES - A_HEADS)))


def _perm_heads(a, axis):
    parts = [lax.slice_in_dim(a, h * A_HD, (h + 1) * A_HD, axis=axis) for h in HEAD_ORDER]
    return jnp.concatenate(parts, axis=axis)


def _rep_rows(vec, rows=8):
    out = jnp.zeros((rows, LANES), F32)
    return out.at[:vec.shape[0], :].set(jnp.broadcast_to(vec.astype(F32)[:, None], (vec.shape[0], LANES)))


def kernel(x_prompt, x_sample, cache_swa_k, cache_swa_v, state_mlstm_C, state_mlstm_n, state_mlstm_m, meta_tokens, rel_bias, w_in, b_igate, b_fgate, attn_sinks, g_mlstm_out, g_attn_out, w_out, ln1_g, ln1_b, w_router, b_router, w_moe1, b_moe1, w_moe2, b_moe2, ln2_g, ln2_b):
    B, S, _ = x_prompt.shape
    NB = x_sample.shape[0]
    assert x_sample.shape[1] == 1 and w_in.shape[0] == 1
    assert S % PROJ_TILE == 0 and S % M_CHUNK == 0 and S % WINDOW == 0 and NB % SAMPLE_GROUP == 0
    l = 0

    pts = np.cumsum(IN_WIDTHS)[:-1].tolist()
    w_qm, w_km, w_vm, w_om, w_ig, w_fg, w_qa, w_ka, w_va = jnp.split(w_in[l], pts, axis=1)
    w_gate = jnp.pad(jnp.concatenate([w_ig, w_fg], axis=1), ((0, 0), (0, LANES - 2 * M_HEADS)))
    w_qa = _perm_heads(w_qa, 1)
    b_gate = jnp.concatenate([b_igate[l], b_fgate[l]]).astype(F32)
    brow = jnp.pad(b_gate, (0, LANES - 2 * M_HEADS))[None, :]
    bcol = b_gate[:, None]
    bf = lambda a: a.astype(BF16)
    wr_p = bf(jnp.concatenate([w_qm, w_vm, w_om, w_qa, w_ka, w_va, w_gate], axis=1))
    wt_p = bf(jnp.concatenate([w_km.T, w_ig.T, w_fg.T], axis=0))
    plan_p = ((0, 512, "plain", BF16), (512, 512, "plain", BF16), (1024, 512, "plain", F32),
              (1536, 512, "plain", BF16), (2048, 128, "plain", BF16), (2176, 128, "plain", BF16),
              (2304, 128, "gate", F32))
    tplan_p = ((0, 512, "plain", BF16), (512, 8, "gate", F32))
    wr_s = bf(jnp.concatenate([w_qm, w_km, w_vm, w_om, w_qa, w_ka, w_va, w_gate], axis=1))
    wt_s = bf(w_vm.T)
    plan_s = ((0, 512, "plain", F32), (512, 512, "plain", F32), (1024, 512, "plain", F32), (1536, 512, "plain", F32),
              (2048, 512, "plain", F32), (2560, 128, "plain", F32), (2688, 128, "plain", F32), (2816, 128, "gate", F32))
    tplan_s = ((0, 512, "plain", F32),)

    bias_tab, bias_cols = _bias_tables(rel_bias)
    sinks = _rep_rows(attn_sinks[l])
    g_m = g_mlstm_out[l].astype(F32)[None, :]
    g_a = _perm_heads(g_attn_out[l].astype(F32), 0)[None, :]
    wo = bf(jnp.concatenate([w_out[l][:M_WIDTH], _perm_heads(w_out[l][M_WIDTH:], 0)], axis=0))
    g1, b1 = ln1_g[l].astype(F32)[None, :], ln1_b[l].astype(F32)[None, :]
    g2, b2 = ln2_g[l].astype(F32)[None, :], ln2_b[l].astype(F32)[None, :]
    w_r = bf(jnp.pad(w_router[l], ((0, 0), (0, LANES - N_EXPERTS))))
    b_r = jnp.pad(b_router[l].astype(F32), (0, LANES - N_EXPERTS), constant_values=NEG)[None, :]
    b1g = b_moe1[l][:, 0::2].astype(F32)[:, None, :]
    b1l = b_moe1[l][:, 1::2].astype(F32)[:, None, :]
    b2e = b_moe2[l].astype(F32)[:, None, :]
    pj = np.zeros((256, 256), np.float32)
    pj[2 * np.arange(128), np.arange(128)] = 1.0
    pj[2 * np.arange(128) + 1, 128 + np.arange(128)] = 1.0
    perm = jnp.asarray(pj, BF16)

    xp2 = x_prompt.reshape(B * S, D_MODEL)
    qm, vm, om, qa, ka, va, gc, kt, gr, kv_tail = _proj(
        xp2, wr_p, wt_p, brow, bcol, plan_p, tplan_p, (2048, 256), PROJ_TILE, S, "proj_prompt")
    x_meta = jnp.pad(meta_tokens.astype(F32), ((0, M_CHUNK - N_META), (0, 0)))
    qm0, vm0, _, _, ka0, va0, gc0, kt0, gr0 = _proj(
        x_meta, wr_p, wt_p, brow, bcol, plan_p, tplan_p, None, M_CHUNK, M_CHUNK, "proj_meta")
    xs2 = x_sample.reshape(NB, D_MODEL)
    qm_s, km_s, vm_s, om_s, qa_s, ka_s, va_s, gc_s, vt_s = _proj(
        xs2, wr_s, wt_s, brow, bcol, plan_s, tplan_s, None, NB, NB, "proj_sample")

    zero_c = jnp.zeros((M_HEADS, M_DK, 2 * M_DV), F32)
    zero_m = jnp.zeros((8, LANES), F32)
    _, ct_meta, m_meta = _mlstm(qm0, vm0, kt0, gc0, gr0, zero_c, zero_m, 1, N_META, "mlstm_meta")
    c0 = jnp.swapaxes(ct_meta[0], 1, 2)
    h_p, ct_p, m_p = _mlstm(qm, vm, kt, gc, gr, c0, m_meta[0], B, M_CHUNK, "mlstm_prompt")
    C_p = ct_p[:, :, :M_DV, :]
    n_p = ct_p[:, :, M_DV, :]
    m_prompt = m_p[:, :M_HEADS, 0]
    m_pad = jnp.pad(state_mlstm_m[l].astype(F32), ((0, 0), (0, LANES - M_HEADS)))
    C_s, n_s, m_s, h_s = _mlstm_step(state_mlstm_C[l].astype(F32), state_mlstm_n[l].astype(F32), m_pad,
                                     gc_s, qm_s, km_s, vm_s, vt_s)

    att_p = _swa(qa, ka, va, ka0, va0, bias_tab, sinks, B)
    ck = cache_swa_k[l].reshape(NB, WINDOW, LANES)
    cv = cache_swa_v[l].reshape(NB, WINDOW, LANES)
    k_new, v_new, att_s = _swa_step(ck, cv, qa_s, ka_s, va_s, bias_cols, sinks)

    x1_p, xpk_p, tk_p = _merge(h_p, om, att_p, xp2, g_m, g_a, wo, g1, b1, w_r, b_r, MERGE_TILE, "merge_prompt")
    x1_s, xpk_s, tk_s = _merge(h_s, om_s, att_s, xs2, g_m, g_a, wo, g1, b1, w_r, b_r, NB, "merge_sample")

    T_p = B * S
    assert T_p % RANK_TILE == 0 and T_p % ROW_TILE == 0
    n_blocks = -(-((T_p + NB) * TOP_K) // EXPERT_TILE) + N_EXPERTS
    rank_p, cnt_p = _rank(tk_p, jnp.zeros((8, LANES), F32))
    rank_s, cnt = _rank(tk_s, cnt_p)
    off, be2, nu2 = _offsets(cnt, n_blocks, EXPERT_TILE)
    dest_p = _dest(tk_p, rank_p, off)[:, :TOP_K].reshape(-1)
    dest_s = _dest(tk_s, rank_s, off)[:, :TOP_K].reshape(-1)
    be = be2.reshape(-1)[:n_blocks]
    nu = nu2[0, :1]
    xs = jnp.zeros((n_blocks * EXPERT_TILE, D_MODEL // 2), U32)
    xs = _dispatch(dest_p, xpk_p, xs)
    xs = _dispatch(dest_s, xpk_s, xs)
    ys = _experts(be, nu, xs, w_moe1[l], b1g, b1l, w_moe2[l], b2e, perm, EXPERT_TILE)
    y_p = _combine(dest_p, ys, tk_p, x1_p, g2, b2)
    y_s = _combine(dest_s, ys, tk_s, x1_s, g2, b2)

    kv_tail = kv_tail.reshape(B, WINDOW, 2, A_KV_HEADS, A_HD)
    dt_k, dt_v = cache_swa_k.dtype, cache_swa_v.dtype
    return (y_p.reshape(B, S, D_MODEL).astype(x_prompt.dtype), y_s.reshape(NB, 1, D_MODEL).astype(x_sample.dtype),
            kv_tail[:, :, 0][None].astype(dt_k), kv_tail[:, :, 1][None].astype(dt_v),
            C_p[None].astype(state_mlstm_C.dtype), n_p[None].astype(state_mlstm_n.dtype),
            m_prompt[None].astype(state_mlstm_m.dtype),
            k_new.reshape(1, NB, WINDOW, A_KV_HEADS, A_HD).astype(dt_k),
            v_new.reshape(1, NB, WINDOW, A_KV_HEADS, A_HD).astype(dt_v),
            C_s[None].astype(state_mlstm_C.dtype), n_s[None].astype(state_mlstm_n.dtype),
            m_s[:, :M_HEADS][None].astype(state_mlstm_m.dtype))
```

```python
import functools
import math

import numpy as np
import jax
import jax.numpy as jnp
from jax import lax
from jax.experimental import pallas as pl
from jax.experimental.pallas import tpu as pltpu

F32 = jnp.float32
BF16 = jnp.bfloat16
I32 = jnp.int32
U32 = jnp.uint32

D_MODEL = 1024
N_META = 16
M_HEADS = 4
M_DK = 128
M_DV = 128
M_WIDTH = M_HEADS * M_DV
A_HD = 64
A_HEADS = 8
A_KV_HEADS = 2
A_GROUP = A_HEADS // A_KV_HEADS
A_WIDTH = A_HEADS * A_HD
WINDOW = 128
REL_BUCKETS = 32
REL_MAX_DIST = 128
N_EXPERTS = 32
TOP_K = 4
D_FF = D_MODEL
SWIGLU_LIMIT = 7.0
SWIGLU_ALPHA = 1.702
DEPTH = 1
DN_ALPHA = (2.0 * DEPTH) ** 0.25
LN_EPS = 1e-5
IN_WIDTHS = (M_WIDTH, M_WIDTH, M_WIDTH, M_WIDTH, M_HEADS, M_HEADS, A_WIDTH, A_KV_HEADS * A_HD, A_KV_HEADS * A_HD)

LANES = 128
NEG = -1e30
VMEM_LIMIT = 56 * 1024 * 1024

M_CHUNK = 128
PROJ_TILE = 512
MERGE_TILE = 512
RANK_TILE = 512
ROW_TILE = 256
EXPERT_TILE = 512
SAMPLE_GROUP = 8
ISSUE_GROUP = 8
SWA_QBLOCKS = 2
HEAD_ORDER = (0, 4, 1, 5, 2, 6, 3, 7)


def _cparams(*sem):
    return pltpu.CompilerParams(dimension_semantics=sem, vmem_limit_bytes=VMEM_LIMIT)


def _log_sigmoid(x):
    return jnp.minimum(x, 0.0) - jnp.log1p(jnp.exp(-jnp.abs(x)))


def _sigmoid(x):
    return 1.0 / (1.0 + jnp.exp(-x))


def _proj_kernel(x_ref, wr_ref, wt_ref, brow_ref, bcol_ref, *outs, row_plan, t_plan, tail_cols):
    xb = x_ref[...].astype(BF16)
    tm = xb.shape[0]
    o = 0
    for (c0, width, kind, _) in row_plan:
        r = jnp.dot(xb, wr_ref[:, c0:c0 + width], preferred_element_type=F32)
        if kind == "gate":
            r = r + brow_ref[...]
            lane = lax.broadcasted_iota(I32, r.shape, 1)
            r = jnp.where(lane < M_HEADS, r, _log_sigmoid(r))
        outs[o][...] = r.astype(outs[o].dtype)
        o += 1
    for (r0, nrows, kind, _) in t_plan:
        r = lax.dot_general(wt_ref[r0:r0 + nrows, :], xb, (((1,), (1,)), ((), ())), preferred_element_type=F32)
        if kind == "gate":
            r = r + bcol_ref[...]
            row = lax.broadcasted_iota(I32, r.shape, 0)
            r = jnp.where(row < M_HEADS, r, _log_sigmoid(r))
        outs[o][...] = r.astype(outs[o].dtype)
        o += 1
    if tail_cols is not None:
        c0, width = tail_cols
        outs[o][...] = jnp.dot(xb[tm - WINDOW:, :], wr_ref[:, c0:c0 + width], preferred_element_type=F32)


def _proj(x, wr, wt, brow, bcol, row_plan, t_plan, tail_cols, tile, rows_per_group, name):
    t = x.shape[0]
    nt = t // tile
    out_shape, out_specs = [], []
    for (_, width, _, dt) in row_plan:
        out_shape.append(jax.ShapeDtypeStruct((t, width), dt))
        out_specs.append(pl.BlockSpec((tile, width), lambda i: (i, 0)))
    for (_, nrows, _, dt) in t_plan:
        out_shape.append(jax.ShapeDtypeStruct((nrows, t), dt))
        out_specs.append(pl.BlockSpec((nrows, tile), lambda i: (0, i)))
    if tail_cols is not None:
        tiles_per_group = rows_per_group // tile
        out_shape.append(jax.ShapeDtypeStruct((t // rows_per_group * WINDOW, tail_cols[1]), F32))
        out_specs.append(pl.BlockSpec((WINDOW, tail_cols[1]), lambda i: (i // tiles_per_group, 0)))
    kern = functools.partial(_proj_kernel, row_plan=row_plan, t_plan=t_plan, tail_cols=tail_cols)
    return pl.pallas_call(
        kern, out_shape=out_shape, grid=(nt,),
        in_specs=[pl.BlockSpec((tile, D_MODEL), lambda i: (i, 0)),
                  pl.BlockSpec(wr.shape, lambda i: (0, 0)),
                  pl.BlockSpec(wt.shape, lambda i: (0, 0)),
                  pl.BlockSpec(brow.shape, lambda i: (0, 0)),
                  pl.BlockSpec(bcol.shape, lambda i: (0, 0))],
        out_specs=out_specs, compiler_params=_cparams("arbitrary"), name=name,
    )(x, wr, wt, brow, bcol)


def _split3(a):
    hi = a.astype(BF16)
    r1 = a - hi.astype(F32)
    mid = r1.astype(BF16)
    lo = (r1 - mid.astype(F32)).astype(BF16)
    return hi, mid, lo


def _mlstm_kernel(q_ref, v_ref, kt_ref, gc_ref, gr_ref, c0_ref, m0_ref, h_ref, ct_out_ref, m_out_ref,
                  ct_scr, m_scr, *, n_valid):
    c = pl.program_id(1)
    nc = pl.num_programs(1)
    L = q_ref.shape[0]

    @pl.when(c == 0)
    def _():
        ct_scr[...] = c0_ref[...]
        m_scr[...] = m0_ref[...]

    gc = gc_ref[...]
    gr = gr_ref[...]
    if n_valid < L:
        rowc = lax.broadcasted_iota(I32, gc.shape, 0)
        lanec = lax.broadcasted_iota(I32, gc.shape, 1)
        gc = jnp.where(rowc < n_valid, gc, jnp.where(lanec < M_HEADS, NEG, 0.0))
        rowr = lax.broadcasted_iota(I32, gr.shape, 0)
        colr = lax.broadcasted_iota(I32, gr.shape, 1)
        gr = jnp.where(colr < n_valid, gr, jnp.where(rowr < M_HEADS, NEG, 0.0))
    r_i = lax.broadcasted_iota(I32, (L, L), 0)
    c_i = lax.broadcasted_iota(I32, (L, L), 1)
    causal = c_i <= r_i
    tril = jnp.where(causal, 1.0, 0.0).astype(BF16)
    triu = jnp.where(r_i <= c_i, 1.0, 0.0).astype(BF16)
    b_cols = sum(jnp.dot(tril, part, preferred_element_type=F32) for part in _split3(gc))
    b_rows = sum(jnp.dot(part, triu, preferred_element_type=F32) for part in _split3(gr))
    lane_l = lax.broadcasted_iota(I32, (L, LANES), 1)
    e0 = jnp.where(lane_l == 0, 1.0, 0.0)
    scale = M_DK ** -0.5

    for h in range(M_HEADS):
        sl = slice(h * M_DK, (h + 1) * M_DK)
        q = q_ref[:, sl]
        v = v_ref[:, sl]
        kt = kt_ref[sl, :]
        ig_c = gc[:, h:h + 1]
        b_c = b_cols[:, M_HEADS + h:M_HEADS + h + 1]
        ig_r = gr[h:h + 1, :]
        b_r = b_rows[M_HEADS + h:M_HEADS + h + 1, :]
        m_prev = m_scr[h:h + 1, 0:1]
        ct = ct_scr[h]

        d = jnp.where(causal, b_c + (ig_r - b_r), NEG)
        m_t = jnp.maximum(b_c + m_prev, jnp.max(d, axis=1, keepdims=True))
        qk = jnp.dot(q, kt, preferred_element_type=F32) * scale
        s = qk * jnp.exp(d - m_t)
        inter = jnp.dot(q, ct.astype(BF16), preferred_element_type=F32)
        v_aug = jnp.concatenate([v, e0.astype(BF16)], axis=1)
        intra = jnp.dot(s.astype(BF16), v_aug, preferred_element_type=F32)
        nd = jnp.exp(b_c + m_prev - m_t) * inter + intra
        den = nd[:, M_DV:M_DV + 1]
        h_ref[:, sl] = nd[:, :M_DV] / jnp.maximum(jnp.abs(den), jnp.exp(-m_t))

        b_last = b_c[L - 1:L, :]
        g = ig_c + b_last - b_c
        m_new = jnp.maximum(b_last + m_prev, jnp.max(g, axis=0, keepdims=True))
        a = jnp.exp(b_last + m_prev - m_new)
        wg = jnp.exp(g - m_new)
        wv = jnp.concatenate([(v.astype(F32) * wg).astype(BF16), (e0 * wg).astype(BF16)], axis=1)
        upd = jnp.dot(kt, wv, preferred_element_type=F32)
        ct_scr[h] = a * ct + upd * scale
        m_scr[h:h + 1, :] = jnp.broadcast_to(m_new, (1, LANES))

    @pl.when(c == nc - 1)
    def _():
        for h in range(M_HEADS):
            ct_out_ref[0, h] = ct_scr[h].T
        m_out_ref[0] = m_scr[...]


def _mlstm(qm, vm, kt, gc, gr, c0, m0, batch, n_valid, name):
    L = M_CHUNK
    nc = qm.shape[0] // (batch * L)
    kern = functools.partial(_mlstm_kernel, n_valid=n_valid)
    return pl.pallas_call(
        kern,
        out_shape=[jax.ShapeDtypeStruct((batch * nc * L, M_WIDTH), F32),
                   jax.ShapeDtypeStruct((batch, M_HEADS, 2 * M_DV, M_DK), F32),
                   jax.ShapeDtypeStruct((batch, 8, LANES), F32)],
        grid=(batch, nc),
        in_specs=[pl.BlockSpec((L, M_WIDTH), lambda b, c: (b * nc + c, 0)),
                  pl.BlockSpec((L, M_WIDTH), lambda b, c: (b * nc + c, 0)),
                  pl.BlockSpec((M_WIDTH, L), lambda b, c: (0, b * nc + c)),
                  pl.BlockSpec((L, LANES), lambda b, c: (b * nc + c, 0)),
                  pl.BlockSpec((8, L), lambda b, c: (0, b * nc + c)),
                  pl.BlockSpec((M_HEADS, M_DK, 2 * M_DV), lambda b, c: (0, 0, 0)),
                  pl.BlockSpec((8, LANES), lambda b, c: (0, 0))],
        out_specs=[pl.BlockSpec((L, M_WIDTH), lambda b, c: (b * nc + c, 0)),
                   pl.BlockSpec((1, M_HEADS, 2 * M_DV, M_DK), lambda b, c: (b, 0, 0, 0)),
                   pl.BlockSpec((1, 8, LANES), lambda b, c: (b, 0, 0))],
        scratch_shapes=[pltpu.VMEM((M_HEADS, M_DK, 2 * M_DV), F32), pltpu.VMEM((8, LANES), F32)],
        compiler_params=_cparams("arbitrary", "arbitrary"), name=name,
    )(qm, vm, kt, gc, gr, c0, m0)


def _outer_f32(a, b):
    ah, am, al = (t.astype(F32) for t in _split3(a))
    bh, bm, bl = (t.astype(F32) for t in _split3(b))
    z = jnp.zeros_like(ah)
    lhs = jnp.concatenate([ah, ah, ah, am, am, al, z, z], axis=0).astype(BF16)
    rhs = jnp.concatenate([bh, bm, bl, bh, bm, bh, z, z], axis=0).astype(BF16)
    return lax.dot_general(lhs, rhs, (((0,), (0,)), ((), ())), preferred_element_type=F32)


def _mlstm_step_kernel(c_ref, n_ref, m_ref, gc_ref, q_ref, k_ref, v_ref,
                       c_out_ref, n_out_ref, m_out_ref, h_ref):
    g = c_ref.shape[0]
    scale = M_DK ** -0.5
    lane_m = lax.broadcasted_iota(I32, (1, LANES), 1)
    for j in range(g):
        m_row = jnp.zeros((1, LANES), F32)
        for h in range(M_HEADS):
            sl = slice(h * M_DK, (h + 1) * M_DK)
            q = q_ref[j:j + 1, sl]
            k = k_ref[j:j + 1, sl] * scale
            v = v_ref[j:j + 1, sl]
            ig = gc_ref[j:j + 1, h:h + 1]
            lf = gc_ref[j:j + 1, M_HEADS + h:M_HEADS + h + 1]
            m = m_ref[j:j + 1, h:h + 1]
            c = c_ref[j, h]
            n = n_ref[j, h:h + 1, :]
            m_t = jnp.maximum(lf + m, ig)
            w = jnp.exp(lf + m - m_t)
            wg = jnp.exp(ig - m_t)
            s = jnp.sum(q * k, axis=1, keepdims=True) * wg
            q8 = jnp.broadcast_to(q, (8, M_DK)).astype(BF16)
            cq = lax.dot_general(q8, c.astype(BF16), (((1,), (1,)), ((), ())), preferred_element_type=F32)[0:1, :]
            den = w * jnp.sum(n * q, axis=1, keepdims=True) + s
            h_ref[j:j + 1, sl] = (w * cq + s * v) / jnp.maximum(jnp.abs(den), jnp.exp(-m_t))
            c_out_ref[j, h] = w * c + _outer_f32(wg * v, k)
            n_out_ref[j, h:h + 1, :] = w * n + wg * k
            m_row = jnp.where(lane_m == h, m_t, m_row)
        m_out_ref[j:j + 1, :] = m_row


def _mlstm_step(c, n, m_pad, gc, q, k, v):
    nb = c.shape[0]
    g = SAMPLE_GROUP
    row = lambda w: pl.BlockSpec((g, w), lambda i: (i, 0))
    return pl.pallas_call(
        _mlstm_step_kernel,
        out_shape=[jax.ShapeDtypeStruct(c.shape, F32), jax.ShapeDtypeStruct(n.shape, F32),
                   jax.ShapeDtypeStruct((nb, LANES), F32), jax.ShapeDtypeStruct((nb, M_WIDTH), F32)],
        grid=(nb // g,),
        in_specs=[pl.BlockSpec((g, M_HEADS, M_DV, M_DK), lambda i: (i, 0, 0, 0)),
                  pl.BlockSpec((g, M_HEADS, M_DK), lambda i: (i, 0, 0)),
                  row(LANES), row(LANES), row(M_WIDTH), row(M_WIDTH), row(M_WIDTH)],
        out_specs=[pl.BlockSpec((g, M_HEADS, M_DV, M_DK), lambda i: (i, 0, 0, 0)),
                   pl.BlockSpec((g, M_HEADS, M_DK), lambda i: (i, 0, 0)),
                   row(LANES), row(M_WIDTH)],
        compiler_params=_cparams("arbitrary"), name="mlstm_step",
    )(c, n, m_pad, gc, q, k, v)


def _swa_kernel(q_ref, kc_ref, kp_ref, vc_ref, vp_ref, km_ref, vm_ref, bias_ref, sink_ref, o_ref):
    j = pl.program_id(1)
    first = j == 0
    blk = WINDOW
    nqb = q_ref.shape[0] // blk
    kp = jnp.where(first, km_ref[...], kp_ref[...])
    vp = jnp.where(first, vm_ref[...], vp_ref[...])
    k = jnp.concatenate([kp, kc_ref[...]], axis=0)
    v = jnp.concatenate([vp, vc_ref[...]], axis=0)
    lane = lax.broadcasted_iota(I32, k.shape, 1)
    zero = jnp.zeros_like(k)
    k_half = (jnp.where(lane < A_HD, k, zero), jnp.where(lane >= A_HD, k, zero))
    v_half = (jnp.where(lane < A_HD, v, zero), jnp.where(lane >= A_HD, v, zero))
    lane_q = lax.broadcasted_iota(I32, (blk, LANES), 1)
    for u in range(nqb):
        rows = slice(u * blk, (u + 1) * blk)
        keys = slice(u * blk, (u + 2) * blk)
        table = jnp.where(first, 0, 1) if u == 0 else 1
        v_stack = jnp.concatenate([v_half[0][keys], v_half[1][keys]], axis=0)
        for p in range(A_GROUP):
            qs = q_ref[rows, p * LANES:(p + 1) * LANES]
            probs, inv = [], []
            for half in range(2):
                hd = HEAD_ORDER[2 * p + half]
                s = lax.dot_general(qs, k_half[half][keys], (((1,), (1,)), ((), ())), preferred_element_type=F32)
                s = s * (A_HD ** -0.5) + bias_ref[table, hd]
                sk = sink_ref[hd:hd + 1, 0:1]
                m = jnp.maximum(jnp.max(s, axis=1, keepdims=True), sk)
                e = jnp.exp(s - m)
                probs.append(e.astype(BF16))
                inv.append(1.0 / (jnp.sum(e, axis=1, keepdims=True) + jnp.exp(sk - m)))
            o = jnp.dot(jnp.concatenate(probs, axis=1), v_stack, preferred_element_type=F32)
            o_ref[rows, p * LANES:(p + 1) * LANES] = o * jnp.where(lane_q < A_HD, inv[0], inv[1])


def _swa(qa, ka, va, kmeta, vmeta, bias, sinks, batch):
    blk = WINDOW
    nqb = SWA_QBLOCKS
    nq = qa.shape[0] // (batch * blk * nqb)
    kv_cur = pl.BlockSpec((nqb * blk, LANES), lambda b, j: (b * nq + j, 0))
    kv_prev = pl.BlockSpec((blk, LANES), lambda b, j: ((b * nq + j) * nqb + jnp.where(j == 0, 0, -1), 0))
    const2 = lambda shape: pl.BlockSpec(shape, lambda b, j: (0, 0))
    return pl.pallas_call(
        _swa_kernel, out_shape=jax.ShapeDtypeStruct((qa.shape[0], A_WIDTH), F32), grid=(batch, nq),
        in_specs=[pl.BlockSpec((nqb * blk, A_WIDTH), lambda b, j: (b * nq + j, 0)),
                  kv_cur, kv_prev, kv_cur, kv_prev, const2((blk, LANES)), const2((blk, LANES)),
                  pl.BlockSpec(bias.shape, lambda b, j: (0, 0, 0, 0)),
                  const2((8, LANES))],
        out_specs=pl.BlockSpec((nqb * blk, A_WIDTH), lambda b, j: (b * nq + j, 0)),
        compiler_params=_cparams("arbitrary", "arbitrary"), name="swa_prompt",
    )(qa, ka, ka, va, va, kmeta, vmeta, bias, sinks)


def _swa_step_kernel(ck_ref, cv_ref, q_ref, k_ref, v_ref, bias_ref, sink_ref, ko_ref, vo_ref, o_ref):
    g = ck_ref.shape[0]
    lane = lax.broadcasted_iota(I32, (A_HEADS, LANES), 1)
    row = lax.broadcasted_iota(I32, (A_HEADS, LANES), 0)
    own_half = (row % 2 == 0) == (lane < A_HD)
    bias = bias_ref[...]
    sk = sink_ref[:, 0:1]
    for j in range(g):
        ko_ref[j, 0:WINDOW - 1, :] = ck_ref[j, 1:WINDOW, :]
        ko_ref[j, WINDOW - 1:WINDOW, :] = k_ref[j:j + 1, :]
        vo_ref[j, 0:WINDOW - 1, :] = cv_ref[j, 1:WINDOW, :]
        vo_ref[j, WINDOW - 1:WINDOW, :] = v_ref[j:j + 1, :]
        kk = ko_ref[j].astype(BF16)
        vv = vo_ref[j].astype(BF16)
        slabs = [q_ref[j:j + 1, p * LANES:(p + 1) * LANES] for p in range(A_GROUP)]
        q8 = jnp.concatenate([slabs[r // 2] for r in range(A_HEADS)], axis=0)
        q8 = jnp.where(own_half, q8, 0.0).astype(BF16)
        s = lax.dot_general(q8, kk, (((1,), (1,)), ((), ())), preferred_element_type=F32)
        s = s * (A_HD ** -0.5) + bias
        m = jnp.maximum(jnp.max(s, axis=1, keepdims=True), sk)
        e = jnp.exp(s - m)
        inv = 1.0 / (jnp.sum(e, axis=1, keepdims=True) + jnp.exp(sk - m))
        o8 = jnp.where(own_half, jnp.dot(e.astype(BF16), vv, preferred_element_type=F32) * inv, 0.0)
        for p in range(A_GROUP):
            o_ref[j:j + 1, p * LANES:(p + 1) * LANES] = o8[2 * p:2 * p + 1, :] + o8[2 * p + 1:2 * p + 2, :]


def _swa_step(ck, cv, q, k, v, bias_rows, sinks):
    nb = ck.shape[0]
    g = SAMPLE_GROUP
    cache = pl.BlockSpec((g, WINDOW, LANES), lambda i: (i, 0, 0))
    row = lambda w: pl.BlockSpec((g, w), lambda i: (i, 0))
    const = lambda a: pl.BlockSpec(a.shape, lambda i: (0, 0))
    return pl.pallas_call(
        _swa_step_kernel,
        out_shape=[jax.ShapeDtypeStruct(ck.shape, F32), jax.ShapeDtypeStruct(cv.shape, F32),
                   jax.ShapeDtypeStruct((nb, A_WIDTH), F32)],
        grid=(nb // g,),
        in_specs=[cache, cache, row(A_WIDTH), row(LANES), row(LANES), const(bias_rows), const(sinks)],
        out_specs=[cache, cache, row(A_WIDTH)],
        compiler_params=_cparams("arbitrary"), name="swa_step",
    )(ck, cv, q, k, v, bias_rows, sinks)


def _layer_norm(z, g, b):
    mu = jnp.mean(z, axis=1, keepdims=True)
    zc = z - mu
    var = jnp.mean(zc * zc, axis=1, keepdims=True)
    return zc * lax.rsqrt(var + LN_EPS) * g + b


def _pack_halves(x):
    w = x.shape[1] // 2
    lo = pltpu.bitcast(x[:, :w].astype(BF16).astype(F32), U32)
    hi = pltpu.bitcast(x[:, w:].astype(BF16).astype(F32), U32)
    return (lo >> 16) | (hi & jnp.uint32(0xFFFF0000))


def _unpack_halves(words):
    lo = pltpu.bitcast(words << 16, F32).astype(BF16)
    hi = pltpu.bitcast(words & jnp.uint32(0xFFFF0000), F32).astype(BF16)
    return lo, hi


def _merge_kernel(h_ref, om_ref, att_ref, x_ref, gm_ref, ga_ref, wo_ref, g1_ref, b1_ref, wr_ref, br_ref,
                  x1_ref, xp_ref, tk_ref, cnt_ref):
    @pl.when(pl.program_id(0) == 0)
    def _():
        cnt_ref[...] = jnp.zeros_like(cnt_ref)

    hm = h_ref[...] * _sigmoid(om_ref[...])
    ym = hm * lax.rsqrt(jnp.mean(hm * hm, axis=1, keepdims=True) + LN_EPS) * gm_ref[...]
    att = att_ref[...]
    ya = att * lax.rsqrt(jnp.mean(att * att, axis=1, keepdims=True) + LN_EPS) * ga_ref[...]
    mix = (jnp.dot(ym.astype(BF16), wo_ref[0:M_WIDTH, :], preferred_element_type=F32)
           + jnp.dot(ya.astype(BF16), wo_ref[M_WIDTH:, :], preferred_element_type=F32))
    x1 = _layer_norm(DN_ALPHA * x_ref[...] + mix, g1_ref[...], b1_ref[...])
    x1_ref[...] = x1
    xp_ref[...] = _pack_halves(x1)
    logits = jnp.dot(x1.astype(BF16), wr_ref[...], preferred_element_type=F32) + br_ref[...]
    lane = lax.broadcasted_iota(I32, logits.shape, 1).astype(F32)
    vals, idxs = [], []
    for _ in range(TOP_K):
        mx = jnp.max(logits, axis=1, keepdims=True)
        idx = jnp.min(jnp.where(logits == mx, lane, float(LANES)), axis=1, keepdims=True)
        vals.append(mx)
        idxs.append(idx)
        logits = jnp.where(lane == idx, 2.0 * NEG, logits)
    es = [jnp.exp(vk - vals[0]) for vk in vals]
    tot = es[0] + es[1] + es[2] + es[3]
    tk = jnp.zeros(logits.shape, F32)
    picked = jnp.zeros(logits.shape, F32)
    for k in range(TOP_K):
        tk = jnp.where(lane == float(k), es[k] / tot, tk)
        tk = jnp.where(lane == float(TOP_K + k), idxs[k], tk)
        picked = jnp.where(lane == idxs[k], 1.0, picked)
    tk_ref[...] = tk
    cnt_ref[...] = cnt_ref[...] + jnp.sum(picked, axis=0, keepdims=True)


def _merge(h, om, att, x, gm, ga, wo, g1, b1, wr, br, tile, name):
    t = x.shape[0]
    rows = lambda w: pl.BlockSpec((tile, w), lambda i: (i, 0))
    const = lambda a: pl.BlockSpec(a.shape, lambda i: (0, 0))
    return pl.pallas_call(
        _merge_kernel,
        out_shape=[jax.ShapeDtypeStruct((t, D_MODEL), F32), jax.ShapeDtypeStruct((t, D_MODEL // 2), U32),
                   jax.ShapeDtypeStruct((t, LANES), F32), jax.ShapeDtypeStruct((8, LANES), F32)],
        grid=(t // tile,),
        in_specs=[rows(M_WIDTH), rows(M_WIDTH), rows(A_WIDTH), rows(D_MODEL), const(gm), const(ga), const(wo),
                  const(g1), const(b1), const(wr), const(br)],
        out_specs=[rows(D_MODEL), rows(D_MODEL // 2), rows(LANES), pl.BlockSpec((8, LANES), lambda i: (0, 0))],
        compiler_params=_cparams("arbitrary"), name=name,
    )(h, om, att, x, gm, ga, wo, g1, b1, wr, br)


def _route_kernel(tk_ref, first_ref, strict_ref, dest_ref, next_scr):
    @pl.when(pl.program_id(0) == 0)
    def _():
        next_scr[...] = first_ref[...]

    tk = tk_ref[...]
    lane = lax.broadcasted_iota(I32, tk.shape, 1).astype(F32)
    onehots = [jnp.where(lane == tk[:, TOP_K + k:TOP_K + k + 1], 1.0, 0.0) for k in range(TOP_K)]
    tot = onehots[0] + onehots[1] + onehots[2] + onehots[3]
    row = jnp.dot(strict_ref[...], tot.astype(BF16), preferred_element_type=F32) + next_scr[0:1, :]
    out = jnp.zeros(tk.shape, F32)
    for k in range(TOP_K):
        out = jnp.where(lane == float(k), jnp.sum(onehots[k] * row, axis=1, keepdims=True), out)
    dest_ref[...] = out.astype(I32)
    next_scr[...] = next_scr[...] + jnp.sum(tot, axis=0, keepdims=True)


def _route(tk, first):
    t = tk.shape[0]
    tile = min(RANK_TILE, t)
    strict = jnp.asarray(np.tril(np.ones((tile, tile), np.float32), -1), BF16)
    return pl.pallas_call(
        _route_kernel, out_shape=jax.ShapeDtypeStruct((t, LANES), I32), grid=(t // tile,),
        in_specs=[pl.BlockSpec((tile, LANES), lambda i: (i, 0)), pl.BlockSpec((8, LANES), lambda i: (0, 0)),
                  pl.BlockSpec((tile, tile), lambda i: (0, 0))],
        out_specs=pl.BlockSpec((tile, LANES), lambda i: (i, 0)),
        scratch_shapes=[pltpu.VMEM((8, LANES), F32)],
        compiler_params=_cparams("arbitrary"), name="moe_route",
    )(tk, first, strict)


def _offsets_kernel(cnt_ref, off_ref, be_ref, nu_ref, *, tile):
    cnt = cnt_ref[...]
    nblk = jnp.floor((cnt + float(tile - 1)) * (1.0 / tile))
    r_i = lax.broadcasted_iota(I32, (LANES, LANES), 0)
    c_i = lax.broadcasted_iota(I32, (LANES, LANES), 1)
    incl = jnp.where(r_i <= c_i, 1.0, 0.0).astype(BF16)
    cum = jnp.dot(nblk.astype(BF16), incl, preferred_element_type=F32)
    off_ref[...] = (cum - nblk) * float(tile)
    rows = be_ref.shape[0]
    jb = (lax.broadcasted_iota(I32, (rows, LANES), 0) * LANES + lax.broadcasted_iota(I32, (rows, LANES), 1)).astype(F32)
    acc = jnp.zeros((rows, LANES), F32)
    for e in range(N_EXPERTS):
        acc = acc + jnp.where(jb >= cum[0:1, e:e + 1], 1.0, 0.0)
    be_ref[...] = jnp.minimum(acc, float(N_EXPERTS - 1)).astype(I32)
    nu_ref[...] = jnp.broadcast_to(cum[0:1, N_EXPERTS - 1:N_EXPERTS], nu_ref.shape).astype(I32)


def _offsets(cnt, n_blocks, tile):
    rows = -(-n_blocks // LANES)
    rows = -(-rows // 8) * 8
    return pl.pallas_call(
        functools.partial(_offsets_kernel, tile=tile),
        out_shape=[jax.ShapeDtypeStruct((8, LANES), F32), jax.ShapeDtypeStruct((rows, LANES), I32),
                   jax.ShapeDtypeStruct((8, LANES), I32)],
        name="moe_offsets",
    )(cnt)


def _dispatch_kernel(dest_ref, xp_ref, xs_in_ref, xs_ref, sem):
    del xs_in_ref
    t = xp_ref.shape[0]

    def row_copy(tok, dst):
        return pltpu.make_async_copy(xp_ref.at[pl.ds(tok, 1), :], xs_ref.at[pl.ds(dst, 1), :], sem)

    def issue(grp, carry):
        base = pl.multiple_of(grp * ISSUE_GROUP, ISSUE_GROUP)
        for u in range(ISSUE_GROUP):
            for k in range(TOP_K):
                row_copy(base + u, dest_ref[(base + u) * TOP_K + k]).start(priority=k % 2)
        return carry

    lax.fori_loop(0, t // ISSUE_GROUP, issue, 0)
    for k in range(TOP_K):
        pltpu.make_async_copy(xp_ref, xs_ref.at[pl.ds(0, t), :], sem).wait()


def _dispatch(dest_flat, xp, xs):
    t = xp.shape[0]
    tile = min(ROW_TILE, t)
    return pl.pallas_call(
        _dispatch_kernel, out_shape=jax.ShapeDtypeStruct(xs.shape, xs.dtype), grid=(t // tile,),
        in_specs=[pl.BlockSpec((tile * TOP_K,), lambda i: (i,), memory_space=pltpu.SMEM),
                  pl.BlockSpec((tile, xp.shape[1]), lambda i: (i, 0)),
                  pl.BlockSpec(memory_space=pl.ANY)],
        out_specs=pl.BlockSpec(memory_space=pl.ANY),
        scratch_shapes=[pltpu.SemaphoreType.DMA(())],
        input_output_aliases={2: 0},
        compiler_params=_cparams("arbitrary"), name="moe_dispatch",
    )(dest_flat, xp, xs)


def _expert_kernel(be_ref, nu_ref, xs_ref, w1_ref, b1g_ref, b1l_ref, w2_ref, b2_ref, perm_ref, ys_ref,
                   w1g_scr, w1l_scr, w2_scr):
    j = pl.program_id(0)
    active = j < nu_ref[0]
    changed = jnp.logical_or(j == 0, be_ref[j] != be_ref[jnp.maximum(j - 1, 0)])
    half = D_MODEL // 2

    @pl.when(jnp.logical_and(active, changed))
    def _():
        for c in range(2 * D_FF // 256):
            wc = w1_ref[0, :, c * 256:(c + 1) * 256].astype(BF16)
            d = jnp.dot(wc, perm_ref[...], preferred_element_type=F32).astype(BF16)
            w1g_scr[:, c * 128:(c + 1) * 128] = d[:, :128]
            w1l_scr[:, c * 128:(c + 1) * 128] = d[:, 128:]
        for c in range(D_FF // 256):
            w2_scr[c * 256:(c + 1) * 256, :] = w2_ref[0, c * 256:(c + 1) * 256, :].astype(BF16)

    @pl.when(active)
    def _():
        lo, hi = _unpack_halves(xs_ref[...])
        y = jnp.zeros(ys_ref.shape, F32)
        nchunk = 512
        for c in range(D_FF // nchunk):
            cs = slice(c * nchunk, (c + 1) * nchunk)
            hg = (jnp.dot(lo, w1g_scr[0:half, cs], preferred_element_type=F32)
                  + jnp.dot(hi, w1g_scr[half:, cs], preferred_element_type=F32) + b1g_ref[0, :, cs])
            hl = (jnp.dot(lo, w1l_scr[0:half, cs], preferred_element_type=F32)
                  + jnp.dot(hi, w1l_scr[half:, cs], preferred_element_type=F32) + b1l_ref[0, :, cs])
            x_glu = jnp.minimum(hg, SWIGLU_LIMIT)
            x_lin = jnp.clip(hl, -SWIGLU_LIMIT, SWIGLU_LIMIT)
            a = x_glu * _sigmoid(SWIGLU_ALPHA * x_glu) * (x_lin + 1.0)
            y = y + jnp.dot(a.astype(BF16), w2_scr[cs, :], preferred_element_type=F32)
        ys_ref[...] = y + b2_ref[0]

    @pl.when(jnp.logical_not(active))
    def _():
        ys_ref[...] = jnp.zeros_like(ys_ref)


def _experts(be, nu, xs, w1, b1g, b1l, w2, b2, perm, tile):
    n_blocks = xs.shape[0] // tile
    last = lambda j, be, nu: jnp.minimum(j, nu[0] - 1)
    grid_spec = pltpu.PrefetchScalarGridSpec(
        num_scalar_prefetch=2, grid=(n_blocks,),
        in_specs=[pl.BlockSpec((tile, xs.shape[1]), lambda j, be, nu: (last(j, be, nu), 0)),
                  pl.BlockSpec((1, D_MODEL, 2 * D_FF), lambda j, be, nu: (be[j], 0, 0)),
                  pl.BlockSpec((1, 1, D_FF), lambda j, be, nu: (be[j], 0, 0)),
                  pl.BlockSpec((1, 1, D_FF), lambda j, be, nu: (be[j], 0, 0)),
                  pl.BlockSpec((1, D_FF, D_MODEL), lambda j, be, nu: (be[j], 0, 0)),
                  pl.BlockSpec((1, 1, D_MODEL), lambda j, be, nu: (be[j], 0, 0)),
                  pl.BlockSpec((256, 256), lambda j, be, nu: (0, 0))],
        out_specs=pl.BlockSpec((tile, D_MODEL), lambda j, be, nu: (j, 0)),
        scratch_shapes=[pltpu.VMEM((D_MODEL, D_FF), BF16), pltpu.VMEM((D_MODEL, D_FF), BF16),
                        pltpu.VMEM((D_FF, D_MODEL), BF16)])
    return pl.pallas_call(
        _expert_kernel, out_shape=jax.ShapeDtypeStruct((xs.shape[0], D_MODEL), F32), grid_spec=grid_spec,
        compiler_params=_cparams("arbitrary"), name="moe_experts",
    )(be, nu, xs, w1, b1g, b1l, w2, b2, perm)


def _combine_kernel(dest_ref, ys_ref, tk_ref, x1_ref, g2_ref, b2_ref, out_ref, buf, sem):
    t = x1_ref.shape[0]

    def issue(grp, carry):
        base = pl.multiple_of(grp * ISSUE_GROUP, ISSUE_GROUP)
        for u in range(ISSUE_GROUP):
            for k in range(TOP_K):
                pltpu.make_async_copy(ys_ref.at[pl.ds(dest_ref[(base + u) * TOP_K + k], 1), :],
                                      buf.at[k, pl.ds(base + u, 1), :], sem).start(priority=k % 2)
        return carry

    lax.fori_loop(0, t // ISSUE_GROUP, issue, 0)
    for k in range(TOP_K):
        pltpu.make_async_copy(ys_ref.at[pl.ds(0, t), :], buf.at[k], sem).wait()
    tk = tk_ref[...]
    ff = tk[:, 0:1] * buf[0]
    for k in range(1, TOP_K):
        ff = ff + tk[:, k:k + 1] * buf[k]
    out_ref[...] = _layer_norm(DN_ALPHA * x1_ref[...] + ff, g2_ref[...], b2_ref[...])


def _combine(dest_flat, ys, tk, x1, g2, b2):
    t = x1.shape[0]
    tile = min(ROW_TILE, t)
    return pl.pallas_call(
        _combine_kernel, out_shape=jax.ShapeDtypeStruct((t, D_MODEL), F32), grid=(t // tile,),
        in_specs=[pl.BlockSpec((tile * TOP_K,), lambda i: (i,), memory_space=pltpu.SMEM),
                  pl.BlockSpec(memory_space=pl.ANY),
                  pl.BlockSpec((tile, LANES), lambda i: (i, 0)),
                  pl.BlockSpec((tile, D_MODEL), lambda i: (i, 0)),
                  pl.BlockSpec((1, D_MODEL), lambda i: (0, 0)),
                  pl.BlockSpec((1, D_MODEL), lambda i: (0, 0))],
        out_specs=pl.BlockSpec((tile, D_MODEL), lambda i: (i, 0)),
        scratch_shapes=[pltpu.VMEM((TOP_K, tile, D_MODEL), F32), pltpu.SemaphoreType.DMA(())],
        compiler_params=_cparams("arbitrary"), name="moe_combine",
    )(dest_flat, ys, tk, x1, g2, b2)


def _rel_bucket(dist):
    exact = REL_BUCKETS // 2
    d = np.maximum(dist, 0)
    log_b = exact + (np.log(np.maximum(d, 1).astype(np.float32) / np.float32(exact))
                     / np.float32(math.log(REL_MAX_DIST / exact)) * np.float32(REL_BUCKETS - exact)).astype(np.int32)
    return np.where(d < exact, d, np.minimum(log_b, REL_BUCKETS - 1)).astype(np.int32)


def _bias_lookup(table, bucket, valid):
    bucket = jnp.asarray(bucket)[None]
    acc = jnp.zeros((table.shape[1],) + bucket.shape[1:], F32)
    for b in range(REL_BUCKETS):
        acc = jnp.where(bucket == b, table[b].reshape((-1,) + (1,) * (bucket.ndim - 1)), acc)
    return jnp.where(jnp.asarray(valid)[None], acc, NEG)


def _bias_tables(rel_bias):
    table = rel_bias.astype(F32)
    r = np.arange(WINDOW)[:, None]
    c = np.arange(2 * WINDOW)[None, :]
    dist = r + WINDOW - c
    valid = (dist >= 0) & (dist < WINDOW)
    dist0 = np.where(c < N_META, N_META + r - c, dist)
    valid0 = np.where(c < N_META, dist0 < WINDOW, (c >= WINDOW) & valid)
    both = jnp.stack([_bias_lookup(table, _rel_bucket(dist0), valid0), _bias_lookup(table, _rel_bucket(dist), valid)])
    dist_s = WINDOW - 1 - np.arange(WINDOW)
    rows = _bias_lookup(table[:, np.asarray(HEAD_ORDER)], _rel_bucket(dist_s), np.ones_like(dist_s, bool))
    return both, rows


def _perm_heads(a, axis):
    parts = [lax.slice_in_dim(a, h * A_HD, (h + 1) * A_HD, axis=axis) for h in HEAD_ORDER]
    return jnp.concatenate(parts, axis=axis)


def _rep_rows(vec, rows=8):
    out = jnp.zeros((rows, LANES), F32)
    return out.at[:vec.shape[0], :].set(jnp.broadcast_to(vec.astype(F32)[:, None], (vec.shape[0], LANES)))


def kernel(x_prompt, x_sample, cache_swa_k, cache_swa_v, state_mlstm_C, state_mlstm_n, state_mlstm_m, meta_tokens, rel_bias, w_in, b_igate, b_fgate, attn_sinks, g_mlstm_out, g_attn_out, w_out, ln1_g, ln1_b, w_router, b_router, w_moe1, b_moe1, w_moe2, b_moe2, ln2_g, ln2_b):
    B, S, _ = x_prompt.shape
    NB = x_sample.shape[0]
    assert x_sample.shape[1] == 1 and w_in.shape[0] == 1
    assert S % PROJ_TILE == 0 and S % M_CHUNK == 0 and S % WINDOW == 0 and NB % SAMPLE_GROUP == 0
    l = 0

    pts = np.cumsum(IN_WIDTHS)[:-1].tolist()
    w_qm, w_km, w_vm, w_om, w_ig, w_fg, w_qa, w_ka, w_va = jnp.split(w_in[l], pts, axis=1)
    w_gate = jnp.pad(jnp.concatenate([w_ig, w_fg], axis=1), ((0, 0), (0, LANES - 2 * M_HEADS)))
    w_qa = _perm_heads(w_qa, 1)
    b_gate = jnp.concatenate([b_igate[l], b_fgate[l]]).astype(F32)
    brow = jnp.pad(b_gate, (0, LANES - 2 * M_HEADS))[None, :]
    bcol = b_gate[:, None]
    bf = lambda a: a.astype(BF16)
    wr_p = bf(jnp.concatenate([w_qm, w_vm, w_om, w_qa, w_ka, w_va, w_gate], axis=1))
    wt_p = bf(jnp.concatenate([w_km.T, w_ig.T, w_fg.T], axis=0))
    plan_p = ((0, 512, "plain", BF16), (512, 512, "plain", BF16), (1024, 512, "plain", F32),
              (1536, 512, "plain", BF16), (2048, 128, "plain", BF16), (2176, 128, "plain", BF16),
              (2304, 128, "gate", F32))
    tplan_p = ((0, 512, "plain", BF16), (512, 8, "gate", F32))
    wr_s = bf(jnp.concatenate([w_qm, w_km, w_vm, w_om, w_qa, w_ka, w_va, w_gate], axis=1))
    plan_s = ((0, 512, "plain", F32), (512, 512, "plain", F32), (1024, 512, "plain", F32), (1536, 512, "plain", F32),
              (2048, 512, "plain", F32), (2560, 128, "plain", F32), (2688, 128, "plain", F32), (2816, 128, "gate", F32))

    bias_tab, bias_rows = _bias_tables(rel_bias)
    sinks = _rep_rows(attn_sinks[l])
    sinks_step = _rep_rows(attn_sinks[l][np.asarray(HEAD_ORDER)])
    g_m = g_mlstm_out[l].astype(F32)[None, :]
    g_a = _perm_heads(g_attn_out[l].astype(F32), 0)[None, :]
    wo = bf(jnp.concatenate([w_out[l][:M_WIDTH], _perm_heads(w_out[l][M_WIDTH:], 0)], axis=0))
    g1, b1 = ln1_g[l].astype(F32)[None, :], ln1_b[l].astype(F32)[None, :]
    g2, b2 = ln2_g[l].astype(F32)[None, :], ln2_b[l].astype(F32)[None, :]
    w_r = bf(jnp.pad(w_router[l], ((0, 0), (0, LANES - N_EXPERTS))))
    b_r = jnp.pad(b_router[l].astype(F32), (0, LANES - N_EXPERTS), constant_values=NEG)[None, :]
    b1g = b_moe1[l][:, 0::2].astype(F32)[:, None, :]
    b1l = b_moe1[l][:, 1::2].astype(F32)[:, None, :]
    b2e = b_moe2[l].astype(F32)[:, None, :]
    pj = np.zeros((256, 256), np.float32)
    pj[2 * np.arange(128), np.arange(128)] = 1.0
    pj[2 * np.arange(128) + 1, 128 + np.arange(128)] = 1.0
    perm = jnp.asarray(pj, BF16)

    xp2 = x_prompt.reshape(B * S, D_MODEL)
    qm, vm, om, qa, ka, va, gc, kt, gr, kv_tail = _proj(
        xp2, wr_p, wt_p, brow, bcol, plan_p, tplan_p, (2048, 256), PROJ_TILE, S, "proj_prompt")
    x_meta = jnp.pad(meta_tokens.astype(F32), ((0, M_CHUNK - N_META), (0, 0)))
    qm0, vm0, _, _, ka0, va0, gc0, kt0, gr0 = _proj(
        x_meta, wr_p, wt_p, brow, bcol, plan_p, tplan_p, None, M_CHUNK, M_CHUNK, "proj_meta")
    xs2 = x_sample.reshape(NB, D_MODEL)
    qm_s, km_s, vm_s, om_s, qa_s, ka_s, va_s, gc_s = _proj(
        xs2, wr_s, wt_p, brow, bcol, plan_s, (), None, NB, NB, "proj_sample")

    zero_c = jnp.zeros((M_HEADS, M_DK, 2 * M_DV), F32)
    zero_m = jnp.zeros((8, LANES), F32)
    _, ct_meta, m_meta = _mlstm(qm0, vm0, kt0, gc0, gr0, zero_c, zero_m, 1, N_META, "mlstm_meta")
    c0 = jnp.swapaxes(ct_meta[0], 1, 2)
    h_p, ct_p, m_p = _mlstm(qm, vm, kt, gc, gr, c0, m_meta[0], B, M_CHUNK, "mlstm_prompt")
    C_p = ct_p[:, :, :M_DV, :]
    n_p = ct_p[:, :, M_DV, :]
    m_prompt = m_p[:, :M_HEADS, 0]
    m_pad = jnp.pad(state_mlstm_m[l].astype(F32), ((0, 0), (0, LANES - M_HEADS)))
    C_s, n_s, m_s, h_s = _mlstm_step(state_mlstm_C[l].astype(F32), state_mlstm_n[l].astype(F32), m_pad,
                                     gc_s, qm_s, km_s, vm_s)

    att_p = _swa(qa, ka, va, ka0, va0, bias_tab, sinks, B)
    ck = cache_swa_k[l].reshape(NB, WINDOW, LANES)
    cv = cache_swa_v[l].reshape(NB, WINDOW, LANES)
    k_new, v_new, att_s = _swa_step(ck, cv, qa_s, ka_s, va_s, bias_rows, sinks_step)

    x1_p, xpk_p, tk_p, cnt_p = _merge(h_p, om, att_p, xp2, g_m, g_a, wo, g1, b1, w_r, b_r, MERGE_TILE, "merge_prompt")
    x1_s, xpk_s, tk_s, cnt_s = _merge(h_s, om_s, att_s, xs2, g_m, g_a, wo, g1, b1, w_r, b_r, NB, "merge_sample")

    T_p = B * S
    assert T_p % RANK_TILE == 0 and T_p % ROW_TILE == 0
    n_blocks = -(-((T_p + NB) * TOP_K) // EXPERT_TILE) + N_EXPERTS
    off, be2, nu2 = _offsets(cnt_p + cnt_s, n_blocks, EXPERT_TILE)
    dest_p = _route(tk_p, off)[:, :TOP_K].reshape(-1)
    dest_s = _route(tk_s, off + cnt_p)[:, :TOP_K].reshape(-1)
    be = be2.reshape(-1)[:n_blocks]
    nu = nu2[0, :1]
    xs = jnp.zeros((n_blocks * EXPERT_TILE, D_MODEL // 2), U32)
    xs = _dispatch(dest_p, xpk_p, xs)
    xs = _dispatch(dest_s, xpk_s, xs)
    ys = _experts(be, nu, xs, w_moe1[l], b1g, b1l, w_moe2[l], b2e, perm, EXPERT_TILE)
    y_p = _combine(dest_p, ys, tk_p, x1_p, g2, b2)
    y_s = _combine(dest_s, ys, tk_s, x1_s, g2, b2)

    kv_tail = kv_tail.reshape(B, WINDOW, 2, A_KV_HEADS, A_HD)
    dt_k, dt_v = cache_swa_k.dtype, cache_swa_v.dtype
    return (y_p.reshape(B, S, D_MODEL).astype(x_prompt.dtype), y_s.reshape(NB, 1, D_MODEL).astype(x_sample.dtype),
            kv_tail[:, :, 0][None].astype(dt_k), kv_tail[:, :, 1][None].astype(dt_v),
            C_p[None].astype(state_mlstm_C.dtype), n_p[None].astype(state_mlstm_n.dtype),
            m_prompt[None].astype(state_mlstm_m.dtype),
            k_new.reshape(1, NB, WINDOW, A_KV_HEADS, A_HD).astype(dt_k),
            v_new.reshape(1, NB, WINDOW, A_KV_HEADS, A_HD).astype(dt_v),
            C_s[None].astype(state_mlstm_C.dtype), n_s[None].astype(state_mlstm_n.dtype),
            m_s[:, :M_HEADS][None].astype(state_mlstm_m.dtype))
```

```python
import functools
import math

import numpy as np
import jax
import jax.numpy as jnp
from jax import lax
from jax.experimental import pallas as pl
from jax.experimental.pallas import tpu as pltpu

F32 = jnp.float32
BF16 = jnp.bfloat16
I32 = jnp.int32
U32 = jnp.uint32

D_MODEL = 1024
N_META = 16
M_HEADS = 4
M_DK = 128
M_DV = 128
M_WIDTH = M_HEADS * M_DV
A_HD = 64
A_HEADS = 8
A_KV_HEADS = 2
A_GROUP = A_HEADS // A_KV_HEADS
A_WIDTH = A_HEADS * A_HD
WINDOW = 128
REL_BUCKETS = 32
REL_MAX_DIST = 128
N_EXPERTS = 32
TOP_K = 4
D_FF = D_MODEL
SWIGLU_LIMIT = 7.0
SWIGLU_ALPHA = 1.702
DEPTH = 1
DN_ALPHA = (2.0 * DEPTH) ** 0.25
LN_EPS = 1e-5
IN_WIDTHS = (M_WIDTH, M_WIDTH, M_WIDTH, M_WIDTH, M_HEADS, M_HEADS, A_WIDTH, A_KV_HEADS * A_HD, A_KV_HEADS * A_HD)

LANES = 128
NEG = -1e30
VMEM_LIMIT = 56 * 1024 * 1024

M_CHUNK = 128
PROJ_TILE = 512
MERGE_TILE = 512
RANK_TILE = 512
ROW_TILE = 256
EXPERT_TILE = 512
SAMPLE_GROUP = 8
ISSUE_GROUP = 8
SWA_QBLOCKS = 2
XP_TILE = D_MODEL // 2 // LANES
HEAD_ORDER = (0, 4, 1, 5, 2, 6, 3, 7)


def _cparams(*sem):
    return pltpu.CompilerParams(dimension_semantics=sem, vmem_limit_bytes=VMEM_LIMIT)


def _log_sigmoid(x):
    return jnp.minimum(x, 0.0) - jnp.log1p(jnp.exp(-jnp.abs(x)))


def _sigmoid(x):
    return 1.0 / (1.0 + jnp.exp(-x))


def _proj_kernel(x_ref, wr_ref, wt_ref, brow_ref, bcol_ref, *outs, row_plan, t_plan, tail_cols):
    xb = x_ref[...].astype(BF16)
    tm = xb.shape[0]
    o = 0
    for (c0, width, kind, _) in row_plan:
        r = jnp.dot(xb, wr_ref[:, c0:c0 + width], preferred_element_type=F32)
        if kind == "gate":
            r = r + brow_ref[...]
            lane = lax.broadcasted_iota(I32, r.shape, 1)
            r = jnp.where(lane < M_HEADS, r, _log_sigmoid(r))
        outs[o][...] = r.astype(outs[o].dtype)
        o += 1
    for (r0, nrows, kind, _) in t_plan:
        r = lax.dot_general(wt_ref[r0:r0 + nrows, :], xb, (((1,), (1,)), ((), ())), preferred_element_type=F32)
        if kind == "gate":
            r = r + bcol_ref[...]
            row = lax.broadcasted_iota(I32, r.shape, 0)
            r = jnp.where(row < M_HEADS, r, _log_sigmoid(r))
        outs[o][...] = r.astype(outs[o].dtype)
        o += 1
    if tail_cols is not None:
        c0, width = tail_cols
        outs[o][...] = jnp.dot(xb[tm - WINDOW:, :], wr_ref[:, c0:c0 + width], preferred_element_type=F32)


def _proj(x, wr, wt, brow, bcol, row_plan, t_plan, tail_cols, tile, rows_per_group, name):
    t = x.shape[0]
    nt = t // tile
    out_shape, out_specs = [], []
    for (_, width, _, dt) in row_plan:
        out_shape.append(jax.ShapeDtypeStruct((t, width), dt))
        out_specs.append(pl.BlockSpec((tile, width), lambda i: (i, 0)))
    for (_, nrows, _, dt) in t_plan:
        out_shape.append(jax.ShapeDtypeStruct((nrows, t), dt))
        out_specs.append(pl.BlockSpec((nrows, tile), lambda i: (0, i)))
    if tail_cols is not None:
        tiles_per_group = rows_per_group // tile
        out_shape.append(jax.ShapeDtypeStruct((t // rows_per_group * WINDOW, tail_cols[1]), F32))
        out_specs.append(pl.BlockSpec((WINDOW, tail_cols[1]), lambda i: (i // tiles_per_group, 0)))
    kern = functools.partial(_proj_kernel, row_plan=row_plan, t_plan=t_plan, tail_cols=tail_cols)
    return pl.pallas_call(
        kern, out_shape=out_shape, grid=(nt,),
        in_specs=[pl.BlockSpec((tile, D_MODEL), lambda i: (i, 0)),
                  pl.BlockSpec(wr.shape, lambda i: (0, 0)),
                  pl.BlockSpec(wt.shape, lambda i: (0, 0)),
                  pl.BlockSpec(brow.shape, lambda i: (0, 0)),
                  pl.BlockSpec(bcol.shape, lambda i: (0, 0))],
        out_specs=out_specs, compiler_params=_cparams("arbitrary"), name=name,
    )(x, wr, wt, brow, bcol)


def _split3(a):
    hi = a.astype(BF16)
    r1 = a - hi.astype(F32)
    mid = r1.astype(BF16)
    lo = (r1 - mid.astype(F32)).astype(BF16)
    return hi, mid, lo


def _mlstm_kernel(q_ref, v_ref, kt_ref, gc_ref, gr_ref, c0_ref, m0_ref, h_ref, ct_out_ref, m_out_ref,
                  ct_scr, m_scr, *, n_valid):
    c = pl.program_id(1)
    nc = pl.num_programs(1)
    L = q_ref.shape[0]

    @pl.when(c == 0)
    def _():
        ct_scr[...] = c0_ref[...]
        m_scr[...] = m0_ref[...]

    gc = gc_ref[...]
    gr = gr_ref[...]
    if n_valid < L:
        rowc = lax.broadcasted_iota(I32, gc.shape, 0)
        lanec = lax.broadcasted_iota(I32, gc.shape, 1)
        gc = jnp.where(rowc < n_valid, gc, jnp.where(lanec < M_HEADS, NEG, 0.0))
        rowr = lax.broadcasted_iota(I32, gr.shape, 0)
        colr = lax.broadcasted_iota(I32, gr.shape, 1)
        gr = jnp.where(colr < n_valid, gr, jnp.where(rowr < M_HEADS, NEG, 0.0))
    r_i = lax.broadcasted_iota(I32, (L, L), 0)
    c_i = lax.broadcasted_iota(I32, (L, L), 1)
    causal = c_i <= r_i
    tril = jnp.where(causal, 1.0, 0.0).astype(BF16)
    triu = jnp.where(r_i <= c_i, 1.0, 0.0).astype(BF16)
    b_cols = sum(jnp.dot(tril, part, preferred_element_type=F32) for part in _split3(gc))
    b_rows = sum(jnp.dot(part, triu, preferred_element_type=F32) for part in _split3(gr))
    lane_l = lax.broadcasted_iota(I32, (L, LANES), 1)
    e0 = jnp.where(lane_l == 0, 1.0, 0.0)
    scale = M_DK ** -0.5

    for h in range(M_HEADS):
        sl = slice(h * M_DK, (h + 1) * M_DK)
        q = q_ref[:, sl]
        v = v_ref[:, sl]
        kt = kt_ref[sl, :]
        ig_c = gc[:, h:h + 1]
        b_c = b_cols[:, M_HEADS + h:M_HEADS + h + 1]
        ig_r = gr[h:h + 1, :]
        b_r = b_rows[M_HEADS + h:M_HEADS + h + 1, :]
        m_prev = m_scr[h:h + 1, 0:1]
        ct = ct_scr[h]

        d = jnp.where(causal, b_c + (ig_r - b_r), NEG)
        m_t = jnp.maximum(b_c + m_prev, jnp.max(d, axis=1, keepdims=True))
        qk = jnp.dot(q, kt, preferred_element_type=F32) * scale
        s = qk * jnp.exp(d - m_t)
        inter = jnp.dot(q, ct.astype(BF16), preferred_element_type=F32)
        v_aug = jnp.concatenate([v, e0.astype(BF16)], axis=1)
        intra = jnp.dot(s.astype(BF16), v_aug, preferred_element_type=F32)
        nd = jnp.exp(b_c + m_prev - m_t) * inter + intra
        den = nd[:, M_DV:M_DV + 1]
        h_ref[:, sl] = nd[:, :M_DV] / jnp.maximum(jnp.abs(den), jnp.exp(-m_t))

        b_last = b_c[L - 1:L, :]
        g = ig_c + b_last - b_c
        m_new = jnp.maximum(b_last + m_prev, jnp.max(g, axis=0, keepdims=True))
        a = jnp.exp(b_last + m_prev - m_new)
        wg = jnp.exp(g - m_new)
        wv = jnp.concatenate([(v.astype(F32) * wg).astype(BF16), (e0 * wg).astype(BF16)], axis=1)
        upd = jnp.dot(kt, wv, preferred_element_type=F32)
        ct_scr[h] = a * ct + upd * scale
        m_scr[h:h + 1, :] = jnp.broadcast_to(m_new, (1, LANES))

    @pl.when(c == nc - 1)
    def _():
        for h in range(M_HEADS):
            ct_out_ref[0, h] = ct_scr[h].T
        m_out_ref[0] = m_scr[...]


def _mlstm(qm, vm, kt, gc, gr, c0, m0, batch, n_valid, name):
    L = M_CHUNK
    nc = qm.shape[0] // (batch * L)
    kern = functools.partial(_mlstm_kernel, n_valid=n_valid)
    return pl.pallas_call(
        kern,
        out_shape=[jax.ShapeDtypeStruct((batch * nc * L, M_WIDTH), F32),
                   jax.ShapeDtypeStruct((batch, M_HEADS, 2 * M_DV, M_DK), F32),
                   jax.ShapeDtypeStruct((batch, 8, LANES), F32)],
        grid=(batch, nc),
        in_specs=[pl.BlockSpec((L, M_WIDTH), lambda b, c: (b * nc + c, 0)),
                  pl.BlockSpec((L, M_WIDTH), lambda b, c: (b * nc + c, 0)),
                  pl.BlockSpec((M_WIDTH, L), lambda b, c: (0, b * nc + c)),
                  pl.BlockSpec((L, LANES), lambda b, c: (b * nc + c, 0)),
                  pl.BlockSpec((8, L), lambda b, c: (0, b * nc + c)),
                  pl.BlockSpec((M_HEADS, M_DK, 2 * M_DV), lambda b, c: (0, 0, 0)),
                  pl.BlockSpec((8, LANES), lambda b, c: (0, 0))],
        out_specs=[pl.BlockSpec((L, M_WIDTH), lambda b, c: (b * nc + c, 0)),
                   pl.BlockSpec((1, M_HEADS, 2 * M_DV, M_DK), lambda b, c: (b, 0, 0, 0)),
                   pl.BlockSpec((1, 8, LANES), lambda b, c: (b, 0, 0))],
        scratch_shapes=[pltpu.VMEM((M_HEADS, M_DK, 2 * M_DV), F32), pltpu.VMEM((8, LANES), F32)],
        compiler_params=_cparams("arbitrary", "arbitrary"), name=name,
    )(qm, vm, kt, gc, gr, c0, m0)


def _outer_f32(a, b):
    ah, am, al = (t.astype(F32) for t in _split3(a))
    bh, bm, bl = (t.astype(F32) for t in _split3(b))
    z = jnp.zeros_like(ah)
    lhs = jnp.concatenate([ah, ah, ah, am, am, al, z, z], axis=0).astype(BF16)
    rhs = jnp.concatenate([bh, bm, bl, bh, bm, bh, z, z], axis=0).astype(BF16)
    return lax.dot_general(lhs, rhs, (((0,), (0,)), ((), ())), preferred_element_type=F32)


def _mlstm_step_kernel(c_ref, n_ref, m_ref, gc_ref, q_ref, k_ref, v_ref,
                       c_out_ref, n_out_ref, m_out_ref, h_ref):
    g = c_ref.shape[0]
    scale = M_DK ** -0.5
    lane_m = lax.broadcasted_iota(I32, (1, LANES), 1)
    for j in range(g):
        m_row = jnp.zeros((1, LANES), F32)
        for h in range(M_HEADS):
            sl = slice(h * M_DK, (h + 1) * M_DK)
            q = q_ref[j:j + 1, sl]
            k = k_ref[j:j + 1, sl] * scale
            v = v_ref[j:j + 1, sl]
            ig = gc_ref[j:j + 1, h:h + 1]
            lf = gc_ref[j:j + 1, M_HEADS + h:M_HEADS + h + 1]
            m = m_ref[j:j + 1, h:h + 1]
            c = c_ref[j, h]
            n = n_ref[j, h:h + 1, :]
            m_t = jnp.maximum(lf + m, ig)
            w = jnp.exp(lf + m - m_t)
            wg = jnp.exp(ig - m_t)
            s = jnp.sum(q * k, axis=1, keepdims=True) * wg
            q8 = jnp.broadcast_to(q, (8, M_DK)).astype(BF16)
            cq = lax.dot_general(q8, c.astype(BF16), (((1,), (1,)), ((), ())), preferred_element_type=F32)[0:1, :]
            den = w * jnp.sum(n * q, axis=1, keepdims=True) + s
            h_ref[j:j + 1, sl] = (w * cq + s * v) / jnp.maximum(jnp.abs(den), jnp.exp(-m_t))
            c_out_ref[j, h] = w * c + _outer_f32(wg * v, k)
            n_out_ref[j, h:h + 1, :] = w * n + wg * k
            m_row = jnp.where(lane_m == h, m_t, m_row)
        m_out_ref[j:j + 1, :] = m_row


def _mlstm_step(c, n, m_pad, gc, q, k, v):
    nb = c.shape[0]
    g = SAMPLE_GROUP
    row = lambda w: pl.BlockSpec((g, w), lambda i: (i, 0))
    return pl.pallas_call(
        _mlstm_step_kernel,
        out_shape=[jax.ShapeDtypeStruct(c.shape, F32), jax.ShapeDtypeStruct(n.shape, F32),
                   jax.ShapeDtypeStruct((nb, LANES), F32), jax.ShapeDtypeStruct((nb, M_WIDTH), F32)],
        grid=(nb // g,),
        in_specs=[pl.BlockSpec((g, M_HEADS, M_DV, M_DK), lambda i: (i, 0, 0, 0)),
                  pl.BlockSpec((g, M_HEADS, M_DK), lambda i: (i, 0, 0)),
                  row(LANES), row(LANES), row(M_WIDTH), row(M_WIDTH), row(M_WIDTH)],
        out_specs=[pl.BlockSpec((g, M_HEADS, M_DV, M_DK), lambda i: (i, 0, 0, 0)),
                   pl.BlockSpec((g, M_HEADS, M_DK), lambda i: (i, 0, 0)),
                   row(LANES), row(M_WIDTH)],
        compiler_params=_cparams("arbitrary"), name="mlstm_step",
    )(c, n, m_pad, gc, q, k, v)


def _swa_kernel(q_ref, kc_ref, kp_ref, vc_ref, vp_ref, km_ref, vm_ref, bias_ref, sink_ref, o_ref):
    j = pl.program_id(1)
    first = j == 0
    blk = WINDOW
    nqb = q_ref.shape[0] // blk
    kp = jnp.where(first, km_ref[...], kp_ref[...])
    vp = jnp.where(first, vm_ref[...], vp_ref[...])
    k = jnp.concatenate([kp, kc_ref[...]], axis=0)
    v = jnp.concatenate([vp, vc_ref[...]], axis=0)
    lane = lax.broadcasted_iota(I32, k.shape, 1)
    zero = jnp.zeros_like(k)
    k_half = (jnp.where(lane < A_HD, k, zero), jnp.where(lane >= A_HD, k, zero))
    v_half = (jnp.where(lane < A_HD, v, zero), jnp.where(lane >= A_HD, v, zero))
    lane_q = lax.broadcasted_iota(I32, (blk, LANES), 1)
    for u in range(nqb):
        rows = slice(u * blk, (u + 1) * blk)
        keys = slice(u * blk, (u + 2) * blk)
        table = jnp.where(first, 0, 1) if u == 0 else 1
        v_stack = jnp.concatenate([v_half[0][keys], v_half[1][keys]], axis=0)
        for p in range(A_GROUP):
            qs = q_ref[rows, p * LANES:(p + 1) * LANES]
            probs, inv = [], []
            for half in range(2):
                hd = HEAD_ORDER[2 * p + half]
                s = lax.dot_general(qs, k_half[half][keys], (((1,), (1,)), ((), ())), preferred_element_type=F32)
                s = s * (A_HD ** -0.5) + bias_ref[table, hd]
                sk = sink_ref[hd:hd + 1, 0:1]
                m = jnp.maximum(jnp.max(s, axis=1, keepdims=True), sk)
                e = jnp.exp(s - m)
                probs.append(e.astype(BF16))
                inv.append(1.0 / (jnp.sum(e, axis=1, keepdims=True) + jnp.exp(sk - m)))
            o = jnp.dot(jnp.concatenate(probs, axis=1), v_stack, preferred_element_type=F32)
            o_ref[rows, p * LANES:(p + 1) * LANES] = o * jnp.where(lane_q < A_HD, inv[0], inv[1])


def _swa(qa, ka, va, kmeta, vmeta, bias, sinks, batch):
    blk = WINDOW
    nqb = SWA_QBLOCKS
    nq = qa.shape[0] // (batch * blk * nqb)
    kv_cur = pl.BlockSpec((nqb * blk, LANES), lambda b, j: (b * nq + j, 0))
    kv_prev = pl.BlockSpec((blk, LANES), lambda b, j: ((b * nq + j) * nqb + jnp.where(j == 0, 0, -1), 0))
    const2 = lambda shape: pl.BlockSpec(shape, lambda b, j: (0, 0))
    return pl.pallas_call(
        _swa_kernel, out_shape=jax.ShapeDtypeStruct((qa.shape[0], A_WIDTH), F32), grid=(batch, nq),
        in_specs=[pl.BlockSpec((nqb * blk, A_WIDTH), lambda b, j: (b * nq + j, 0)),
                  kv_cur, kv_prev, kv_cur, kv_prev, const2((blk, LANES)), const2((blk, LANES)),
                  pl.BlockSpec(bias.shape, lambda b, j: (0, 0, 0, 0)),
                  const2((8, LANES))],
        out_specs=pl.BlockSpec((nqb * blk, A_WIDTH), lambda b, j: (b * nq + j, 0)),
        compiler_params=_cparams("arbitrary", "arbitrary"), name="swa_prompt",
    )(qa, ka, ka, va, va, kmeta, vmeta, bias, sinks)


def _swa_step_kernel(ck_ref, cv_ref, q_ref, k_ref, v_ref, bias_ref, sink_ref, ko_ref, vo_ref, o_ref):
    g = ck_ref.shape[0]
    lane = lax.broadcasted_iota(I32, (A_HEADS, LANES), 1)
    row = lax.broadcasted_iota(I32, (A_HEADS, LANES), 0)
    own_half = (row % 2 == 0) == (lane < A_HD)
    bias = bias_ref[...]
    sk = sink_ref[:, 0:1]
    for j in range(g):
        ko_ref[j, 0:WINDOW - 1, :] = ck_ref[j, 1:WINDOW, :]
        ko_ref[j, WINDOW - 1:WINDOW, :] = k_ref[j:j + 1, :]
        vo_ref[j, 0:WINDOW - 1, :] = cv_ref[j, 1:WINDOW, :]
        vo_ref[j, WINDOW - 1:WINDOW, :] = v_ref[j:j + 1, :]
        kk = ko_ref[j].astype(BF16)
        vv = vo_ref[j].astype(BF16)
        slabs = [q_ref[j:j + 1, p * LANES:(p + 1) * LANES] for p in range(A_GROUP)]
        q8 = jnp.concatenate([slabs[r // 2] for r in range(A_HEADS)], axis=0)
        q8 = jnp.where(own_half, q8, 0.0).astype(BF16)
        s = lax.dot_general(q8, kk, (((1,), (1,)), ((), ())), preferred_element_type=F32)
        s = s * (A_HD ** -0.5) + bias
        m = jnp.maximum(jnp.max(s, axis=1, keepdims=True), sk)
        e = jnp.exp(s - m)
        inv = 1.0 / (jnp.sum(e, axis=1, keepdims=True) + jnp.exp(sk - m))
        o8 = jnp.where(own_half, jnp.dot(e.astype(BF16), vv, preferred_element_type=F32) * inv, 0.0)
        for p in range(A_GROUP):
            o_ref[j:j + 1, p * LANES:(p + 1) * LANES] = o8[2 * p:2 * p + 1, :] + o8[2 * p + 1:2 * p + 2, :]


def _swa_step(ck, cv, q, k, v, bias_rows, sinks):
    nb = ck.shape[0]
    g = SAMPLE_GROUP
    cache = pl.BlockSpec((g, WINDOW, LANES), lambda i: (i, 0, 0))
    row = lambda w: pl.BlockSpec((g, w), lambda i: (i, 0))
    const = lambda a: pl.BlockSpec(a.shape, lambda i: (0, 0))
    return pl.pallas_call(
        _swa_step_kernel,
        out_shape=[jax.ShapeDtypeStruct(ck.shape, F32), jax.ShapeDtypeStruct(cv.shape, F32),
                   jax.ShapeDtypeStruct((nb, A_WIDTH), F32)],
        grid=(nb // g,),
        in_specs=[cache, cache, row(A_WIDTH), row(LANES), row(LANES), const(bias_rows), const(sinks)],
        out_specs=[cache, cache, row(A_WIDTH)],
        compiler_params=_cparams("arbitrary"), name="swa_step",
    )(ck, cv, q, k, v, bias_rows, sinks)


def _layer_norm(z, g, b):
    mu = jnp.mean(z, axis=1, keepdims=True)
    zc = z - mu
    var = jnp.mean(zc * zc, axis=1, keepdims=True)
    return zc * lax.rsqrt(var + LN_EPS) * g + b


def _pack_halves(x):
    w = x.shape[1] // 2
    lo = pltpu.bitcast(x[:, :w].astype(BF16).astype(F32), U32)
    hi = pltpu.bitcast(x[:, w:].astype(BF16).astype(F32), U32)
    return (lo >> 16) | (hi & jnp.uint32(0xFFFF0000))


def _unpack_halves(words):
    lo = pltpu.bitcast(words << 16, F32).astype(BF16)
    hi = pltpu.bitcast(words & jnp.uint32(0xFFFF0000), F32).astype(BF16)
    return lo, hi


def _to_token_tiles(ref, x):
    for q in range(x.shape[1] // LANES):
        ref[:, q, :] = x[:, q * LANES:(q + 1) * LANES]


def _merge_kernel(h_ref, om_ref, att_ref, x_ref, gm_ref, ga_ref, wo_ref, g1_ref, b1_ref, wr_ref, br_ref,
                  x1_ref, xp_ref, tk_ref, cnt_ref):
    @pl.when(pl.program_id(0) == 0)
    def _():
        cnt_ref[...] = jnp.zeros_like(cnt_ref)

    hm = h_ref[...] * _sigmoid(om_ref[...])
    ym = hm * lax.rsqrt(jnp.mean(hm * hm, axis=1, keepdims=True) + LN_EPS) * gm_ref[...]
    att = att_ref[...]
    ya = att * lax.rsqrt(jnp.mean(att * att, axis=1, keepdims=True) + LN_EPS) * ga_ref[...]
    mix = (jnp.dot(ym.astype(BF16), wo_ref[0:M_WIDTH, :], preferred_element_type=F32)
           + jnp.dot(ya.astype(BF16), wo_ref[M_WIDTH:, :], preferred_element_type=F32))
    x1 = _layer_norm(DN_ALPHA * x_ref[...] + mix, g1_ref[...], b1_ref[...])
    x1_ref[...] = x1
    _to_token_tiles(xp_ref, _pack_halves(x1))
    logits = jnp.dot(x1.astype(BF16), wr_ref[...], preferred_element_type=F32) + br_ref[...]
    lane = lax.broadcasted_iota(I32, logits.shape, 1).astype(F32)
    vals, idxs = [], []
    for _ in range(TOP_K):
        mx = jnp.max(logits, axis=1, keepdims=True)
        idx = jnp.min(jnp.where(logits == mx, lane, float(LANES)), axis=1, keepdims=True)
        vals.append(mx)
        idxs.append(idx)
        logits = jnp.where(lane == idx, 2.0 * NEG, logits)
    es = [jnp.exp(vk - vals[0]) for vk in vals]
    tot = es[0] + es[1] + es[2] + es[3]
    tk = jnp.zeros(logits.shape, F32)
    picked = jnp.zeros(logits.shape, F32)
    for k in range(TOP_K):
        tk = jnp.where(lane == float(k), es[k] / tot, tk)
        tk = jnp.where(lane == float(TOP_K + k), idxs[k], tk)
        picked = jnp.where(lane == idxs[k], 1.0, picked)
    tk_ref[...] = tk
    cnt_ref[...] = cnt_ref[...] + jnp.sum(picked, axis=0, keepdims=True)


def _merge(h, om, att, x, gm, ga, wo, g1, b1, wr, br, tile, name):
    t = x.shape[0]
    rows = lambda w: pl.BlockSpec((tile, w), lambda i: (i, 0))
    const = lambda a: pl.BlockSpec(a.shape, lambda i: (0, 0))
    return pl.pallas_call(
        _merge_kernel,
        out_shape=[jax.ShapeDtypeStruct((t, D_MODEL), F32), jax.ShapeDtypeStruct((t, XP_TILE, LANES), U32),
                   jax.ShapeDtypeStruct((t, LANES), F32), jax.ShapeDtypeStruct((8, LANES), F32)],
        grid=(t // tile,),
        in_specs=[rows(M_WIDTH), rows(M_WIDTH), rows(A_WIDTH), rows(D_MODEL), const(gm), const(ga), const(wo),
                  const(g1), const(b1), const(wr), const(br)],
        out_specs=[rows(D_MODEL), pl.BlockSpec((tile, XP_TILE, LANES), lambda i: (i, 0, 0)), rows(LANES),
                   pl.BlockSpec((8, LANES), lambda i: (0, 0))],
        compiler_params=_cparams("arbitrary"), name=name,
    )(h, om, att, x, gm, ga, wo, g1, b1, wr, br)


def _route_kernel(tk_ref, first_ref, strict_ref, dest_ref, next_scr):
    @pl.when(pl.program_id(0) == 0)
    def _():
        next_scr[...] = first_ref[...]

    tk = tk_ref[...]
    lane = lax.broadcasted_iota(I32, tk.shape, 1).astype(F32)
    onehots = [jnp.where(lane == tk[:, TOP_K + k:TOP_K + k + 1], 1.0, 0.0) for k in range(TOP_K)]
    tot = onehots[0] + onehots[1] + onehots[2] + onehots[3]
    row = jnp.dot(strict_ref[...], tot.astype(BF16), preferred_element_type=F32) + next_scr[0:1, :]
    out = jnp.zeros(tk.shape, F32)
    for k in range(TOP_K):
        out = jnp.where(lane == float(k), jnp.sum(onehots[k] * row, axis=1, keepdims=True), out)
    dest_ref[...] = out.astype(I32)
    next_scr[...] = next_scr[...] + jnp.sum(tot, axis=0, keepdims=True)


def _route(tk, first):
    t = tk.shape[0]
    tile = min(RANK_TILE, t)
    strict = jnp.asarray(np.tril(np.ones((tile, tile), np.float32), -1), BF16)
    return pl.pallas_call(
        _route_kernel, out_shape=jax.ShapeDtypeStruct((t, LANES), I32), grid=(t // tile,),
        in_specs=[pl.BlockSpec((tile, LANES), lambda i: (i, 0)), pl.BlockSpec((8, LANES), lambda i: (0, 0)),
                  pl.BlockSpec((tile, tile), lambda i: (0, 0))],
        out_specs=pl.BlockSpec((tile, LANES), lambda i: (i, 0)),
        scratch_shapes=[pltpu.VMEM((8, LANES), F32)],
        compiler_params=_cparams("arbitrary"), name="moe_route",
    )(tk, first, strict)


def _offsets_kernel(cnt_ref, off_ref, be_ref, nu_ref, *, tile):
    cnt = cnt_ref[...]
    nblk = jnp.floor((cnt + float(tile - 1)) * (1.0 / tile))
    r_i = lax.broadcasted_iota(I32, (LANES, LANES), 0)
    c_i = lax.broadcasted_iota(I32, (LANES, LANES), 1)
    incl = jnp.where(r_i <= c_i, 1.0, 0.0).astype(BF16)
    cum = jnp.dot(nblk.astype(BF16), incl, preferred_element_type=F32)
    off_ref[...] = (cum - nblk) * float(tile)
    rows = be_ref.shape[0]
    jb = (lax.broadcasted_iota(I32, (rows, LANES), 0) * LANES + lax.broadcasted_iota(I32, (rows, LANES), 1)).astype(F32)
    acc = jnp.zeros((rows, LANES), F32)
    for e in range(N_EXPERTS):
        acc = acc + jnp.where(jb >= cum[0:1, e:e + 1], 1.0, 0.0)
    be_ref[...] = jnp.minimum(acc, float(N_EXPERTS - 1)).astype(I32)
    nu_ref[...] = jnp.broadcast_to(cum[0:1, N_EXPERTS - 1:N_EXPERTS], nu_ref.shape).astype(I32)


def _offsets(cnt, n_blocks, tile):
    rows = -(-n_blocks // LANES)
    rows = -(-rows // 8) * 8
    return pl.pallas_call(
        functools.partial(_offsets_kernel, tile=tile),
        out_shape=[jax.ShapeDtypeStruct((8, LANES), F32), jax.ShapeDtypeStruct((rows, LANES), I32),
                   jax.ShapeDtypeStruct((8, LANES), I32)],
        name="moe_offsets",
    )(cnt)


def _dispatch_kernel(dest_ref, xp_ref, xs_in_ref, xs_ref, sem):
    del xs_in_ref
    t = xp_ref.shape[0]

    def row_copy(tok, dst):
        return pltpu.make_async_copy(xp_ref.at[pl.ds(tok, 1)], xs_ref.at[pl.ds(dst, 1)], sem)

    def issue(grp, carry):
        base = pl.multiple_of(grp * ISSUE_GROUP, ISSUE_GROUP)
        for u in range(ISSUE_GROUP):
            for k in range(TOP_K):
                row_copy(base + u, dest_ref[(base + u) * TOP_K + k]).start(priority=k % 2)
        return carry

    lax.fori_loop(0, t // ISSUE_GROUP, issue, 0)
    for k in range(TOP_K):
        pltpu.make_async_copy(xp_ref, xs_ref.at[pl.ds(0, t)], sem).wait()


def _dispatch(dest_flat, xp, xs):
    t = xp.shape[0]
    tile = min(ROW_TILE, t)
    return pl.pallas_call(
        _dispatch_kernel, out_shape=jax.ShapeDtypeStruct(xs.shape, xs.dtype), grid=(t // tile,),
        in_specs=[pl.BlockSpec((tile * TOP_K,), lambda i: (i,), memory_space=pltpu.SMEM),
                  pl.BlockSpec((tile,) + xp.shape[1:], lambda i: (i, 0, 0)),
                  pl.BlockSpec(memory_space=pl.ANY)],
        out_specs=pl.BlockSpec(memory_space=pl.ANY),
        scratch_shapes=[pltpu.SemaphoreType.DMA(())],
        input_output_aliases={2: 0},
        compiler_params=_cparams("arbitrary"), name="moe_dispatch",
    )(dest_flat, xp, xs)


def _expert_kernel(be_ref, nu_ref, xs_ref, w1_ref, b1g_ref, b1l_ref, w2_ref, b2_ref, perm_ref, ys_ref,
                   w1g_scr, w1l_scr, w2_scr, xq_scr, sem):
    j = pl.program_id(0)
    active = j < nu_ref[0]
    changed = jnp.logical_or(j == 0, be_ref[j] != be_ref[jnp.maximum(j - 1, 0)])
    half = D_MODEL // 2
    tm = ys_ref.shape[0]
    slot = lax.rem(j, 2)

    def fetch(blk, slot):
        row0 = pl.multiple_of(blk * tm, tm)
        return [pltpu.make_async_copy(xs_ref.at[pl.ds(row0, tm), q, :], xq_scr.at[slot, q], sem.at[slot])
                for q in range(XP_TILE)]

    @pl.when(j == 0)
    def _():
        for cp in fetch(0, 0):
            cp.start()

    @pl.when(j + 1 < nu_ref[0])
    def _():
        for cp in fetch(j + 1, 1 - slot):
            cp.start()

    @pl.when(jnp.logical_and(active, changed))
    def _():
        for c in range(2 * D_FF // 256):
            wc = w1_ref[0, :, c * 256:(c + 1) * 256].astype(BF16)
            d = jnp.dot(wc, perm_ref[...], preferred_element_type=F32).astype(BF16)
            w1g_scr[:, c * 128:(c + 1) * 128] = d[:, :128]
            w1l_scr[:, c * 128:(c + 1) * 128] = d[:, 128:]
        for c in range(D_FF // 256):
            w2_scr[c * 256:(c + 1) * 256, :] = w2_ref[0, c * 256:(c + 1) * 256, :].astype(BF16)

    @pl.when(active)
    def _():
        for cp in fetch(j, slot):
            cp.wait()
        lo, hi = _unpack_halves(jnp.concatenate([xq_scr[slot, q] for q in range(XP_TILE)], axis=1))
        y = jnp.zeros(ys_ref.shape, F32)
        nchunk = 512
        for c in range(D_FF // nchunk):
            cs = slice(c * nchunk, (c + 1) * nchunk)
            hg = (jnp.dot(lo, w1g_scr[0:half, cs], preferred_element_type=F32)
                  + jnp.dot(hi, w1g_scr[half:, cs], preferred_element_type=F32) + b1g_ref[0, :, cs])
            hl = (jnp.dot(lo, w1l_scr[0:half, cs], preferred_element_type=F32)
                  + jnp.dot(hi, w1l_scr[half:, cs], preferred_element_type=F32) + b1l_ref[0, :, cs])
            x_glu = jnp.minimum(hg, SWIGLU_LIMIT)
            x_lin = jnp.clip(hl, -SWIGLU_LIMIT, SWIGLU_LIMIT)
            a = x_glu * _sigmoid(SWIGLU_ALPHA * x_glu) * (x_lin + 1.0)
            y = y + jnp.dot(a.astype(BF16), w2_scr[cs, :], preferred_element_type=F32)
        ys_ref[...] = y + b2_ref[0]

    @pl.when(jnp.logical_not(active))
    def _():
        ys_ref[...] = jnp.zeros_like(ys_ref)


def _experts(be, nu, xs, w1, b1g, b1l, w2, b2, perm, tile):
    n_blocks = xs.shape[0] // tile
    grid_spec = pltpu.PrefetchScalarGridSpec(
        num_scalar_prefetch=2, grid=(n_blocks,),
        in_specs=[pl.BlockSpec(memory_space=pl.ANY),
                  pl.BlockSpec((1, D_MODEL, 2 * D_FF), lambda j, be, nu: (be[j], 0, 0)),
                  pl.BlockSpec((1, 1, D_FF), lambda j, be, nu: (be[j], 0, 0)),
                  pl.BlockSpec((1, 1, D_FF), lambda j, be, nu: (be[j], 0, 0)),
                  pl.BlockSpec((1, D_FF, D_MODEL), lambda j, be, nu: (be[j], 0, 0)),
                  pl.BlockSpec((1, 1, D_MODEL), lambda j, be, nu: (be[j], 0, 0)),
                  pl.BlockSpec((256, 256), lambda j, be, nu: (0, 0))],
        out_specs=pl.BlockSpec((tile, D_MODEL), lambda j, be, nu: (j, 0)),
        scratch_shapes=[pltpu.VMEM((D_MODEL, D_FF), BF16), pltpu.VMEM((D_MODEL, D_FF), BF16),
                        pltpu.VMEM((D_FF, D_MODEL), BF16), pltpu.VMEM((2, XP_TILE, tile, LANES), U32),
                        pltpu.SemaphoreType.DMA((2,))])
    return pl.pallas_call(
        _expert_kernel, out_shape=jax.ShapeDtypeStruct((xs.shape[0], D_MODEL), F32), grid_spec=grid_spec,
        compiler_params=_cparams("arbitrary"), name="moe_experts",
    )(be, nu, xs, w1, b1g, b1l, w2, b2, perm)


def _combine_kernel(dest_ref, ys_ref, tk_ref, x1_ref, g2_ref, b2_ref, out_ref, buf, sem):
    t = x1_ref.shape[0]

    def issue(grp, carry):
        base = pl.multiple_of(grp * ISSUE_GROUP, ISSUE_GROUP)
        for u in range(ISSUE_GROUP):
            for k in range(TOP_K):
                pltpu.make_async_copy(ys_ref.at[pl.ds(dest_ref[(base + u) * TOP_K + k], 1)],
                                      buf.at[k, pl.ds(base + u, 1)], sem).start(priority=k % 2)
        return carry

    lax.fori_loop(0, t // ISSUE_GROUP, issue, 0)
    for k in range(TOP_K):
        pltpu.make_async_copy(ys_ref.at[pl.ds(0, t)], buf.at[k], sem).wait()
    tk = tk_ref[...]
    ff = tk[:, 0:1] * buf[0]
    for k in range(1, TOP_K):
        ff = ff + tk[:, k:k + 1] * buf[k]
    out_ref[...] = _layer_norm(DN_ALPHA * x1_ref[...] + ff, g2_ref[...], b2_ref[...])


def _combine(dest_flat, ys, tk, x1, g2, b2):
    t = x1.shape[0]
    tile = min(ROW_TILE, t)
    return pl.pallas_call(
        _combine_kernel, out_shape=jax.ShapeDtypeStruct((t, D_MODEL), F32), grid=(t // tile,),
        in_specs=[pl.BlockSpec((tile * TOP_K,), lambda i: (i,), memory_space=pltpu.SMEM),
                  pl.BlockSpec(memory_space=pl.ANY),
                  pl.BlockSpec((tile, LANES), lambda i: (i, 0)),
                  pl.BlockSpec((tile, D_MODEL), lambda i: (i, 0)),
                  pl.BlockSpec((1, D_MODEL), lambda i: (0, 0)),
                  pl.BlockSpec((1, D_MODEL), lambda i: (0, 0))],
        out_specs=pl.BlockSpec((tile, D_MODEL), lambda i: (i, 0)),
        scratch_shapes=[pltpu.VMEM((TOP_K, tile, D_MODEL), F32), pltpu.SemaphoreType.DMA(())],
        compiler_params=_cparams("arbitrary"), name="moe_combine",
    )(dest_flat, ys, tk, x1, g2, b2)


def _rel_bucket(dist):
    exact = REL_BUCKETS // 2
    d = np.maximum(dist, 0)
    log_b = exact + (np.log(np.maximum(d, 1).astype(np.float32) / np.float32(exact))
                     / np.float32(math.log(REL_MAX_DIST / exact)) * np.float32(REL_BUCKETS - exact)).astype(np.int32)
    return np.where(d < exact, d, np.minimum(log_b, REL_BUCKETS - 1)).astype(np.int32)


def _bias_lookup(table, bucket, valid):
    bucket = jnp.asarray(bucket)[None]
    acc = jnp.zeros((table.shape[1],) + bucket.shape[1:], F32)
    for b in range(REL_BUCKETS):
        acc = jnp.where(bucket == b, table[b].reshape((-1,) + (1,) * (bucket.ndim - 1)), acc)
    return jnp.where(jnp.asarray(valid)[None], acc, NEG)


def _bias_tables(rel_bias):
    table = rel_bias.astype(F32)
    r = np.arange(WINDOW)[:, None]
    c = np.arange(2 * WINDOW)[None, :]
    dist = r + WINDOW - c
    valid = (dist >= 0) & (dist < WINDOW)
    dist0 = np.where(c < N_META, N_META + r - c, dist)
    valid0 = np.where(c < N_META, dist0 < WINDOW, (c >= WINDOW) & valid)
    both = jnp.stack([_bias_lookup(table, _rel_bucket(dist0), valid0), _bias_lookup(table, _rel_bucket(dist), valid)])
    dist_s = WINDOW - 1 - np.arange(WINDOW)
    rows = _bias_lookup(table[:, np.asarray(HEAD_ORDER)], _rel_bucket(dist_s), np.ones_like(dist_s, bool))
    return both, rows


def _perm_heads(a, axis):
    parts = [lax.slice_in_dim(a, h * A_HD, (h + 1) * A_HD, axis=axis) for h in HEAD_ORDER]
    return jnp.concatenate(parts, axis=axis)


def _rep_rows(vec, rows=8):
    out = jnp.zeros((rows, LANES), F32)
    return out.at[:vec.shape[0], :].set(jnp.broadcast_to(vec.astype(F32)[:, None], (vec.shape[0], LANES)))


def kernel(x_prompt, x_sample, cache_swa_k, cache_swa_v, state_mlstm_C, state_mlstm_n, state_mlstm_m, meta_tokens, rel_bias, w_in, b_igate, b_fgate, attn_sinks, g_mlstm_out, g_attn_out, w_out, ln1_g, ln1_b, w_router, b_router, w_moe1, b_moe1, w_moe2, b_moe2, ln2_g, ln2_b):
    B, S, _ = x_prompt.shape
    NB = x_sample.shape[0]
    assert x_sample.shape[1] == 1 and w_in.shape[0] == 1
    assert S % PROJ_TILE == 0 and S % M_CHUNK == 0 and S % WINDOW == 0 and NB % SAMPLE_GROUP == 0
    l = 0

    pts = np.cumsum(IN_WIDTHS)[:-1].tolist()
    w_qm, w_km, w_vm, w_om, w_ig, w_fg, w_qa, w_ka, w_va = jnp.split(w_in[l], pts, axis=1)
    w_gate = jnp.pad(jnp.concatenate([w_ig, w_fg], axis=1), ((0, 0), (0, LANES - 2 * M_HEADS)))
    w_qa = _perm_heads(w_qa, 1)
    b_gate = jnp.concatenate([b_igate[l], b_fgate[l]]).astype(F32)
    brow = jnp.pad(b_gate, (0, LANES - 2 * M_HEADS))[None, :]
    bcol = b_gate[:, None]
    bf = lambda a: a.astype(BF16)
    wr_p = bf(jnp.concatenate([w_qm, w_vm, w_om, w_qa, w_ka, w_va, w_gate], axis=1))
    wt_p = bf(jnp.concatenate([w_km.T, w_ig.T, w_fg.T], axis=0))
    plan_p = ((0, 512, "plain", BF16), (512, 512, "plain", BF16), (1024, 512, "plain", F32),
              (1536, 512, "plain", BF16), (2048, 128, "plain", BF16), (2176, 128, "plain", BF16),
              (2304, 128, "gate", F32))
    tplan_p = ((0, 512, "plain", BF16), (512, 8, "gate", F32))
    wr_s = bf(jnp.concatenate([w_qm, w_km, w_vm, w_om, w_qa, w_ka, w_va, w_gate], axis=1))
    plan_s = ((0, 512, "plain", F32), (512, 512, "plain", F32), (1024, 512, "plain", F32), (1536, 512, "plain", F32),
              (2048, 512, "plain", F32), (2560, 128, "plain", F32), (2688, 128, "plain", F32), (2816, 128, "gate", F32))

    bias_tab, bias_rows = _bias_tables(rel_bias)
    sinks = _rep_rows(attn_sinks[l])
    sinks_step = _rep_rows(attn_sinks[l][np.asarray(HEAD_ORDER)])
    g_m = g_mlstm_out[l].astype(F32)[None, :]
    g_a = _perm_heads(g_attn_out[l].astype(F32), 0)[None, :]
    wo = bf(jnp.concatenate([w_out[l][:M_WIDTH], _perm_heads(w_out[l][M_WIDTH:], 0)], axis=0))
    g1, b1 = ln1_g[l].astype(F32)[None, :], ln1_b[l].astype(F32)[None, :]
    g2, b2 = ln2_g[l].astype(F32)[None, :], ln2_b[l].astype(F32)[None, :]
    w_r = bf(jnp.pad(w_router[l], ((0, 0), (0, LANES - N_EXPERTS))))
    b_r = jnp.pad(b_router[l].astype(F32), (0, LANES - N_EXPERTS), constant_values=NEG)[None, :]
    b1g = b_moe1[l][:, 0::2].astype(F32)[:, None, :]
    b1l = b_moe1[l][:, 1::2].astype(F32)[:, None, :]
    b2e = b_moe2[l].astype(F32)[:, None, :]
    pj = np.zeros((256, 256), np.float32)
    pj[2 * np.arange(128), np.arange(128)] = 1.0
    pj[2 * np.arange(128) + 1, 128 + np.arange(128)] = 1.0
    perm = jnp.asarray(pj, BF16)

    xp2 = x_prompt.reshape(B * S, D_MODEL)
    qm, vm, om, qa, ka, va, gc, kt, gr, kv_tail = _proj(
        xp2, wr_p, wt_p, brow, bcol, plan_p, tplan_p, (2048, 256), PROJ_TILE, S, "proj_prompt")
    x_meta = jnp.pad(meta_tokens.astype(F32), ((0, M_CHUNK - N_META), (0, 0)))
    qm0, vm0, _, _, ka0, va0, gc0, kt0, gr0 = _proj(
        x_meta, wr_p, wt_p, brow, bcol, plan_p, tplan_p, None, M_CHUNK, M_CHUNK, "proj_meta")
    xs2 = x_sample.reshape(NB, D_MODEL)
    qm_s, km_s, vm_s, om_s, qa_s, ka_s, va_s, gc_s = _proj(
        xs2, wr_s, wt_p, brow, bcol, plan_s, (), None, NB, NB, "proj_sample")

    zero_c = jnp.zeros((M_HEADS, M_DK, 2 * M_DV), F32)
    zero_m = jnp.zeros((8, LANES), F32)
    _, ct_meta, m_meta = _mlstm(qm0, vm0, kt0, gc0, gr0, zero_c, zero_m, 1, N_META, "mlstm_meta")
    c0 = jnp.swapaxes(ct_meta[0], 1, 2)
    h_p, ct_p, m_p = _mlstm(qm, vm, kt, gc, gr, c0, m_meta[0], B, M_CHUNK, "mlstm_prompt")
    C_p = ct_p[:, :, :M_DV, :]
    n_p = ct_p[:, :, M_DV, :]
    m_prompt = m_p[:, :M_HEADS, 0]
    m_pad = jnp.pad(state_mlstm_m[l].astype(F32), ((0, 0), (0, LANES - M_HEADS)))
    C_s, n_s, m_s, h_s = _mlstm_step(state_mlstm_C[l].astype(F32), state_mlstm_n[l].astype(F32), m_pad,
                                     gc_s, qm_s, km_s, vm_s)

    att_p = _swa(qa, ka, va, ka0, va0, bias_tab, sinks, B)
    ck = cache_swa_k[l].reshape(NB, WINDOW, LANES)
    cv = cache_swa_v[l].reshape(NB, WINDOW, LANES)
    k_new, v_new, att_s = _swa_step(ck, cv, qa_s, ka_s, va_s, bias_rows, sinks_step)

    x1_p, xpk_p, tk_p, cnt_p = _merge(h_p, om, att_p, xp2, g_m, g_a, wo, g1, b1, w_r, b_r, MERGE_TILE, "merge_prompt")
    x1_s, xpk_s, tk_s, cnt_s = _merge(h_s, om_s, att_s, xs2, g_m, g_a, wo, g1, b1, w_r, b_r, NB, "merge_sample")

    T_p = B * S
    assert T_p % RANK_TILE == 0 and T_p % ROW_TILE == 0
    n_blocks = -(-((T_p + NB) * TOP_K) // EXPERT_TILE) + N_EXPERTS
    off, be2, nu2 = _offsets(cnt_p + cnt_s, n_blocks, EXPERT_TILE)
    dest_p = _route(tk_p, off)[:, :TOP_K].reshape(-1)
    dest_s = _route(tk_s, off + cnt_p)[:, :TOP_K].reshape(-1)
    be = be2.reshape(-1)[:n_blocks]
    nu = nu2[0, :1]
    xs = jnp.zeros((n_blocks * EXPERT_TILE, XP_TILE, LANES), U32)
    xs = _dispatch(dest_p, xpk_p, xs)
    xs = _dispatch(dest_s, xpk_s, xs)
    ys = _experts(be, nu, xs, w_moe1[l], b1g, b1l, w_moe2[l], b2e, perm, EXPERT_TILE)
    y_p = _combine(dest_p, ys, tk_p, x1_p, g2, b2)
    y_s = _combine(dest_s, ys, tk_s, x1_s, g2, b2)

    kv_tail = kv_tail.reshape(B, WINDOW, 2, A_KV_HEADS, A_HD)
    dt_k, dt_v = cache_swa_k.dtype, cache_swa_v.dtype
    return (y_p.reshape(B, S, D_MODEL).astype(x_prompt.dtype), y_s.reshape(NB, 1, D_MODEL).astype(x_sample.dtype),
            kv_tail[:, :, 0][None].astype(dt_k), kv_tail[:, :, 1][None].astype(dt_v),
            C_p[None].astype(state_mlstm_C.dtype), n_p[None].astype(state_mlstm_n.dtype),
            m_prompt[None].astype(state_mlstm_m.dtype),
            k_new.reshape(1, NB, WINDOW, A_KV_HEADS, A_HD).astype(dt_k),
            v_new.reshape(1, NB, WINDOW, A_KV_HEADS, A_HD).astype(dt_v),
            C_s[None].astype(state_mlstm_C.dtype), n_s[None].astype(state_mlstm_n.dtype),
            m_s[:, :M_HEADS][None].astype(state_mlstm_m.dtype))
```

```python
import functools
import math

import numpy as np
import jax
import jax.numpy as jnp
from jax import lax
from jax.experimental import pallas as pl
from jax.experimental.pallas import tpu as pltpu

F32 = jnp.float32
BF16 = jnp.bfloat16
I32 = jnp.int32
U32 = jnp.uint32

D_MODEL = 1024
N_META = 16
M_HEADS = 4
M_DK = 128
M_DV = 128
M_WIDTH = M_HEADS * M_DV
A_HD = 64
A_HEADS = 8
A_KV_HEADS = 2
A_GROUP = A_HEADS // A_KV_HEADS
A_WIDTH = A_HEADS * A_HD
WINDOW = 128
REL_BUCKETS = 32
REL_MAX_DIST = 128
N_EXPERTS = 32
TOP_K = 4
D_FF = D_MODEL
SWIGLU_LIMIT = 7.0
SWIGLU_ALPHA = 1.702
DEPTH = 1
DN_ALPHA = (2.0 * DEPTH) ** 0.25
LN_EPS = 1e-5
IN_WIDTHS = (M_WIDTH, M_WIDTH, M_WIDTH, M_WIDTH, M_HEADS, M_HEADS, A_WIDTH, A_KV_HEADS * A_HD, A_KV_HEADS * A_HD)

LANES = 128
NEG = -1e30
VMEM_LIMIT = 56 * 1024 * 1024

M_CHUNK = 128
PROJ_TILE = 512
MERGE_TILE = 512
RANK_TILE = 512
ROW_TILE = 256
EXPERT_TILE = 512
SAMPLE_GROUP = 8
ISSUE_GROUP = 8
SWA_QBLOCKS = 2
XP_TILE = D_MODEL // 2 // LANES
YS_TILE = D_MODEL // LANES
HEAD_ORDER = (0, 4, 1, 5, 2, 6, 3, 7)


def _cparams(*sem):
    return pltpu.CompilerParams(dimension_semantics=sem, vmem_limit_bytes=VMEM_LIMIT)


def _log_sigmoid(x):
    return jnp.minimum(x, 0.0) - jnp.log1p(jnp.exp(-jnp.abs(x)))


def _sigmoid(x):
    return 1.0 / (1.0 + jnp.exp(-x))


def _proj_kernel(x_ref, wr_ref, wt_ref, brow_ref, bcol_ref, *outs, row_plan, t_plan, tail_cols):
    xb = x_ref[...].astype(BF16)
    tm = xb.shape[0]
    o = 0
    for (c0, width, kind, _) in row_plan:
        r = jnp.dot(xb, wr_ref[:, c0:c0 + width], preferred_element_type=F32)
        if kind == "gate":
            r = r + brow_ref[...]
            lane = lax.broadcasted_iota(I32, r.shape, 1)
            r = jnp.where(lane < M_HEADS, r, _log_sigmoid(r))
        outs[o][...] = r.astype(outs[o].dtype)
        o += 1
    for (r0, nrows, kind, _) in t_plan:
        r = lax.dot_general(wt_ref[r0:r0 + nrows, :], xb, (((1,), (1,)), ((), ())), preferred_element_type=F32)
        if kind == "gate":
            r = r + bcol_ref[...]
            row = lax.broadcasted_iota(I32, r.shape, 0)
            r = jnp.where(row < M_HEADS, r, _log_sigmoid(r))
        outs[o][...] = r.astype(outs[o].dtype)
        o += 1
    if tail_cols is not None:
        c0, width = tail_cols
        outs[o][...] = jnp.dot(xb[tm - WINDOW:, :], wr_ref[:, c0:c0 + width], preferred_element_type=F32)


def _proj(x, wr, wt, brow, bcol, row_plan, t_plan, tail_cols, tile, rows_per_group, name):
    t = x.shape[0]
    nt = t // tile
    out_shape, out_specs = [], []
    for (_, width, _, dt) in row_plan:
        out_shape.append(jax.ShapeDtypeStruct((t, width), dt))
        out_specs.append(pl.BlockSpec((tile, width), lambda i: (i, 0)))
    for (_, nrows, _, dt) in t_plan:
        out_shape.append(jax.ShapeDtypeStruct((nrows, t), dt))
        out_specs.append(pl.BlockSpec((nrows, tile), lambda i: (0, i)))
    if tail_cols is not None:
        tiles_per_group = rows_per_group // tile
        out_shape.append(jax.ShapeDtypeStruct((t // rows_per_group * WINDOW, tail_cols[1]), F32))
        out_specs.append(pl.BlockSpec((WINDOW, tail_cols[1]), lambda i: (i // tiles_per_group, 0)))
    kern = functools.partial(_proj_kernel, row_plan=row_plan, t_plan=t_plan, tail_cols=tail_cols)
    return pl.pallas_call(
        kern, out_shape=out_shape, grid=(nt,),
        in_specs=[pl.BlockSpec((tile, D_MODEL), lambda i: (i, 0)),
                  pl.BlockSpec(wr.shape, lambda i: (0, 0)),
                  pl.BlockSpec(wt.shape, lambda i: (0, 0)),
                  pl.BlockSpec(brow.shape, lambda i: (0, 0)),
                  pl.BlockSpec(bcol.shape, lambda i: (0, 0))],
        out_specs=out_specs, compiler_params=_cparams("arbitrary"), name=name,
    )(x, wr, wt, brow, bcol)


def _split3(a):
    hi = a.astype(BF16)
    r1 = a - hi.astype(F32)
    mid = r1.astype(BF16)
    lo = (r1 - mid.astype(F32)).astype(BF16)
    return hi, mid, lo


def _mlstm_kernel(q_ref, v_ref, kt_ref, gc_ref, gr_ref, c0_ref, m0_ref, h_ref, ct_out_ref, m_out_ref,
                  ct_scr, m_scr, *, n_valid):
    c = pl.program_id(1)
    nc = pl.num_programs(1)
    L = q_ref.shape[0]

    @pl.when(c == 0)
    def _():
        ct_scr[...] = c0_ref[...]
        m_scr[...] = m0_ref[...]

    gc = gc_ref[...]
    gr = gr_ref[...]
    if n_valid < L:
        rowc = lax.broadcasted_iota(I32, gc.shape, 0)
        lanec = lax.broadcasted_iota(I32, gc.shape, 1)
        gc = jnp.where(rowc < n_valid, gc, jnp.where(lanec < M_HEADS, NEG, 0.0))
        rowr = lax.broadcasted_iota(I32, gr.shape, 0)
        colr = lax.broadcasted_iota(I32, gr.shape, 1)
        gr = jnp.where(colr < n_valid, gr, jnp.where(rowr < M_HEADS, NEG, 0.0))
    r_i = lax.broadcasted_iota(I32, (L, L), 0)
    c_i = lax.broadcasted_iota(I32, (L, L), 1)
    causal = c_i <= r_i
    tril = jnp.where(causal, 1.0, 0.0).astype(BF16)
    triu = jnp.where(r_i <= c_i, 1.0, 0.0).astype(BF16)
    b_cols = sum(jnp.dot(tril, part, preferred_element_type=F32) for part in _split3(gc))
    b_rows = sum(jnp.dot(part, triu, preferred_element_type=F32) for part in _split3(gr))
    lane_l = lax.broadcasted_iota(I32, (L, LANES), 1)
    e0 = jnp.where(lane_l == 0, 1.0, 0.0)
    scale = M_DK ** -0.5

    for h in range(M_HEADS):
        sl = slice(h * M_DK, (h + 1) * M_DK)
        q = q_ref[:, sl]
        v = v_ref[:, sl]
        kt = kt_ref[sl, :]
        ig_c = gc[:, h:h + 1]
        b_c = b_cols[:, M_HEADS + h:M_HEADS + h + 1]
        ig_r = gr[h:h + 1, :]
        b_r = b_rows[M_HEADS + h:M_HEADS + h + 1, :]
        m_prev = m_scr[h:h + 1, 0:1]
        ct = ct_scr[h]

        d = jnp.where(causal, b_c + (ig_r - b_r), NEG)
        m_t = jnp.maximum(b_c + m_prev, jnp.max(d, axis=1, keepdims=True))
        qk = jnp.dot(q, kt, preferred_element_type=F32) * scale
        s = qk * jnp.exp(d - m_t)
        inter = jnp.dot(q, ct.astype(BF16), preferred_element_type=F32)
        v_aug = jnp.concatenate([v, e0.astype(BF16)], axis=1)
        intra = jnp.dot(s.astype(BF16), v_aug, preferred_element_type=F32)
        nd = jnp.exp(b_c + m_prev - m_t) * inter + intra
        den = nd[:, M_DV:M_DV + 1]
        h_ref[:, sl] = nd[:, :M_DV] / jnp.maximum(jnp.abs(den), jnp.exp(-m_t))

        b_last = b_c[L - 1:L, :]
        g = ig_c + b_last - b_c
        m_new = jnp.maximum(b_last + m_prev, jnp.max(g, axis=0, keepdims=True))
        a = jnp.exp(b_last + m_prev - m_new)
        wg = jnp.exp(g - m_new)
        wv = jnp.concatenate([(v.astype(F32) * wg).astype(BF16), (e0 * wg).astype(BF16)], axis=1)
        upd = jnp.dot(kt, wv, preferred_element_type=F32)
        ct_scr[h] = a * ct + upd * scale
        m_scr[h:h + 1, :] = jnp.broadcast_to(m_new, (1, LANES))

    @pl.when(c == nc - 1)
    def _():
        for h in range(M_HEADS):
            ct_out_ref[0, h] = ct_scr[h].T
        m_out_ref[0] = m_scr[...]


def _mlstm(qm, vm, kt, gc, gr, c0, m0, batch, n_valid, name):
    L = M_CHUNK
    nc = qm.shape[0] // (batch * L)
    kern = functools.partial(_mlstm_kernel, n_valid=n_valid)
    return pl.pallas_call(
        kern,
        out_shape=[jax.ShapeDtypeStruct((batch * nc * L, M_WIDTH), F32),
                   jax.ShapeDtypeStruct((batch, M_HEADS, 2 * M_DV, M_DK), F32),
                   jax.ShapeDtypeStruct((batch, 8, LANES), F32)],
        grid=(batch, nc),
        in_specs=[pl.BlockSpec((L, M_WIDTH), lambda b, c: (b * nc + c, 0)),
                  pl.BlockSpec((L, M_WIDTH), lambda b, c: (b * nc + c, 0)),
                  pl.BlockSpec((M_WIDTH, L), lambda b, c: (0, b * nc + c)),
                  pl.BlockSpec((L, LANES), lambda b, c: (b * nc + c, 0)),
                  pl.BlockSpec((8, L), lambda b, c: (0, b * nc + c)),
                  pl.BlockSpec((M_HEADS, M_DK, 2 * M_DV), lambda b, c: (0, 0, 0)),
                  pl.BlockSpec((8, LANES), lambda b, c: (0, 0))],
        out_specs=[pl.BlockSpec((L, M_WIDTH), lambda b, c: (b * nc + c, 0)),
                   pl.BlockSpec((1, M_HEADS, 2 * M_DV, M_DK), lambda b, c: (b, 0, 0, 0)),
                   pl.BlockSpec((1, 8, LANES), lambda b, c: (b, 0, 0))],
        scratch_shapes=[pltpu.VMEM((M_HEADS, M_DK, 2 * M_DV), F32), pltpu.VMEM((8, LANES), F32)],
        compiler_params=_cparams("arbitrary", "arbitrary"), name=name,
    )(qm, vm, kt, gc, gr, c0, m0)


def _outer_f32(a, b):
    ah, am, al = (t.astype(F32) for t in _split3(a))
    bh, bm, bl = (t.astype(F32) for t in _split3(b))
    z = jnp.zeros_like(ah)
    lhs = jnp.concatenate([ah, ah, ah, am, am, al, z, z], axis=0).astype(BF16)
    rhs = jnp.concatenate([bh, bm, bl, bh, bm, bh, z, z], axis=0).astype(BF16)
    return lax.dot_general(lhs, rhs, (((0,), (0,)), ((), ())), preferred_element_type=F32)


def _mlstm_step_kernel(c_ref, n_ref, m_ref, gc_ref, q_ref, k_ref, v_ref,
                       c_out_ref, n_out_ref, m_out_ref, h_ref):
    g = c_ref.shape[0]
    scale = M_DK ** -0.5
    lane_m = lax.broadcasted_iota(I32, (1, LANES), 1)
    for j in range(g):
        m_row = jnp.zeros((1, LANES), F32)
        for h in range(M_HEADS):
            sl = slice(h * M_DK, (h + 1) * M_DK)
            q = q_ref[j:j + 1, sl]
            k = k_ref[j:j + 1, sl] * scale
            v = v_ref[j:j + 1, sl]
            ig = gc_ref[j:j + 1, h:h + 1]
            lf = gc_ref[j:j + 1, M_HEADS + h:M_HEADS + h + 1]
            m = m_ref[j:j + 1, h:h + 1]
            c = c_ref[j, h]
            n = n_ref[j, h:h + 1, :]
            m_t = jnp.maximum(lf + m, ig)
            w = jnp.exp(lf + m - m_t)
            wg = jnp.exp(ig - m_t)
            s = jnp.sum(q * k, axis=1, keepdims=True) * wg
            q8 = jnp.broadcast_to(q, (8, M_DK)).astype(BF16)
            cq = lax.dot_general(q8, c.astype(BF16), (((1,), (1,)), ((), ())), preferred_element_type=F32)[0:1, :]
            den = w * jnp.sum(n * q, axis=1, keepdims=True) + s
            h_ref[j:j + 1, sl] = (w * cq + s * v) / jnp.maximum(jnp.abs(den), jnp.exp(-m_t))
            c_out_ref[j, h] = w * c + _outer_f32(wg * v, k)
            n_out_ref[j, h:h + 1, :] = w * n + wg * k
            m_row = jnp.where(lane_m == h, m_t, m_row)
        m_out_ref[j:j + 1, :] = m_row


def _mlstm_step(c, n, m_pad, gc, q, k, v):
    nb = c.shape[0]
    g = SAMPLE_GROUP
    row = lambda w: pl.BlockSpec((g, w), lambda i: (i, 0))
    return pl.pallas_call(
        _mlstm_step_kernel,
        out_shape=[jax.ShapeDtypeStruct(c.shape, F32), jax.ShapeDtypeStruct(n.shape, F32),
                   jax.ShapeDtypeStruct((nb, LANES), F32), jax.ShapeDtypeStruct((nb, M_WIDTH), F32)],
        grid=(nb // g,),
        in_specs=[pl.BlockSpec((g, M_HEADS, M_DV, M_DK), lambda i: (i, 0, 0, 0)),
                  pl.BlockSpec((g, M_HEADS, M_DK), lambda i: (i, 0, 0)),
                  row(LANES), row(LANES), row(M_WIDTH), row(M_WIDTH), row(M_WIDTH)],
        out_specs=[pl.BlockSpec((g, M_HEADS, M_DV, M_DK), lambda i: (i, 0, 0, 0)),
                   pl.BlockSpec((g, M_HEADS, M_DK), lambda i: (i, 0, 0)),
                   row(LANES), row(M_WIDTH)],
        compiler_params=_cparams("arbitrary"), name="mlstm_step",
    )(c, n, m_pad, gc, q, k, v)


def _swa_kernel(q_ref, kc_ref, kp_ref, vc_ref, vp_ref, km_ref, vm_ref, bias_ref, sink_ref, o_ref):
    j = pl.program_id(1)
    first = j == 0
    blk = WINDOW
    nqb = q_ref.shape[0] // blk
    kp = jnp.where(first, km_ref[...], kp_ref[...])
    vp = jnp.where(first, vm_ref[...], vp_ref[...])
    k = jnp.concatenate([kp, kc_ref[...]], axis=0)
    v = jnp.concatenate([vp, vc_ref[...]], axis=0)
    lane = lax.broadcasted_iota(I32, k.shape, 1)
    zero = jnp.zeros_like(k)
    k_half = (jnp.where(lane < A_HD, k, zero), jnp.where(lane >= A_HD, k, zero))
    v_half = (jnp.where(lane < A_HD, v, zero), jnp.where(lane >= A_HD, v, zero))
    lane_q = lax.broadcasted_iota(I32, (blk, LANES), 1)
    for u in range(nqb):
        rows = slice(u * blk, (u + 1) * blk)
        keys = slice(u * blk, (u + 2) * blk)
        table = jnp.where(first, 0, 1) if u == 0 else 1
        v_stack = jnp.concatenate([v_half[0][keys], v_half[1][keys]], axis=0)
        for p in range(A_GROUP):
            qs = q_ref[rows, p * LANES:(p + 1) * LANES]
            probs, inv = [], []
            for half in range(2):
                hd = HEAD_ORDER[2 * p + half]
                s = lax.dot_general(qs, k_half[half][keys], (((1,), (1,)), ((), ())), preferred_element_type=F32)
                s = s * (A_HD ** -0.5) + bias_ref[table, hd]
                sk = sink_ref[hd:hd + 1, 0:1]
                m = jnp.maximum(jnp.max(s, axis=1, keepdims=True), sk)
                e = jnp.exp(s - m)
                probs.append(e.astype(BF16))
                inv.append(1.0 / (jnp.sum(e, axis=1, keepdims=True) + jnp.exp(sk - m)))
            o = jnp.dot(jnp.concatenate(probs, axis=1), v_stack, preferred_element_type=F32)
            o_ref[rows, p * LANES:(p + 1) * LANES] = o * jnp.where(lane_q < A_HD, inv[0], inv[1])


def _swa(qa, ka, va, kmeta, vmeta, bias, sinks, batch):
    blk = WINDOW
    nqb = SWA_QBLOCKS
    nq = qa.shape[0] // (batch * blk * nqb)
    kv_cur = pl.BlockSpec((nqb * blk, LANES), lambda b, j: (b * nq + j, 0))
    kv_prev = pl.BlockSpec((blk, LANES), lambda b, j: ((b * nq + j) * nqb + jnp.where(j == 0, 0, -1), 0))
    const2 = lambda shape: pl.BlockSpec(shape, lambda b, j: (0, 0))
    return pl.pallas_call(
        _swa_kernel, out_shape=jax.ShapeDtypeStruct((qa.shape[0], A_WIDTH), F32), grid=(batch, nq),
        in_specs=[pl.BlockSpec((nqb * blk, A_WIDTH), lambda b, j: (b * nq + j, 0)),
                  kv_cur, kv_prev, kv_cur, kv_prev, const2((blk, LANES)), const2((blk, LANES)),
                  pl.BlockSpec(bias.shape, lambda b, j: (0, 0, 0, 0)),
                  const2((8, LANES))],
        out_specs=pl.BlockSpec((nqb * blk, A_WIDTH), lambda b, j: (b * nq + j, 0)),
        compiler_params=_cparams("arbitrary", "arbitrary"), name="swa_prompt",
    )(qa, ka, ka, va, va, kmeta, vmeta, bias, sinks)


def _swa_step_kernel(ck_ref, cv_ref, q_ref, k_ref, v_ref, bias_ref, sink_ref, ko_ref, vo_ref, o_ref):
    g = ck_ref.shape[0]
    lane = lax.broadcasted_iota(I32, (A_HEADS, LANES), 1)
    row = lax.broadcasted_iota(I32, (A_HEADS, LANES), 0)
    own_half = (row % 2 == 0) == (lane < A_HD)
    bias = bias_ref[...]
    sk = sink_ref[:, 0:1]
    for j in range(g):
        ko_ref[j, 0:WINDOW - 1, :] = ck_ref[j, 1:WINDOW, :]
        ko_ref[j, WINDOW - 1:WINDOW, :] = k_ref[j:j + 1, :]
        vo_ref[j, 0:WINDOW - 1, :] = cv_ref[j, 1:WINDOW, :]
        vo_ref[j, WINDOW - 1:WINDOW, :] = v_ref[j:j + 1, :]
        kk = ko_ref[j].astype(BF16)
        vv = vo_ref[j].astype(BF16)
        slabs = [q_ref[j:j + 1, p * LANES:(p + 1) * LANES] for p in range(A_GROUP)]
        q8 = jnp.concatenate([slabs[r // 2] for r in range(A_HEADS)], axis=0)
        q8 = jnp.where(own_half, q8, 0.0).astype(BF16)
        s = lax.dot_general(q8, kk, (((1,), (1,)), ((), ())), preferred_element_type=F32)
        s = s * (A_HD ** -0.5) + bias
        m = jnp.maximum(jnp.max(s, axis=1, keepdims=True), sk)
        e = jnp.exp(s - m)
        inv = 1.0 / (jnp.sum(e, axis=1, keepdims=True) + jnp.exp(sk - m))
        o8 = jnp.where(own_half, jnp.dot(e.astype(BF16), vv, preferred_element_type=F32) * inv, 0.0)
        for p in range(A_GROUP):
            o_ref[j:j + 1, p * LANES:(p + 1) * LANES] = o8[2 * p:2 * p + 1, :] + o8[2 * p + 1:2 * p + 2, :]


def _swa_step(ck, cv, q, k, v, bias_rows, sinks):
    nb = ck.shape[0]
    g = SAMPLE_GROUP
    cache = pl.BlockSpec((g, WINDOW, LANES), lambda i: (i, 0, 0))
    row = lambda w: pl.BlockSpec((g, w), lambda i: (i, 0))
    const = lambda a: pl.BlockSpec(a.shape, lambda i: (0, 0))
    return pl.pallas_call(
        _swa_step_kernel,
        out_shape=[jax.ShapeDtypeStruct(ck.shape, F32), jax.ShapeDtypeStruct(cv.shape, F32),
                   jax.ShapeDtypeStruct((nb, A_WIDTH), F32)],
        grid=(nb // g,),
        in_specs=[cache, cache, row(A_WIDTH), row(LANES), row(LANES), const(bias_rows), const(sinks)],
        out_specs=[cache, cache, row(A_WIDTH)],
        compiler_params=_cparams("arbitrary"), name="swa_step",
    )(ck, cv, q, k, v, bias_rows, sinks)


def _layer_norm(z, g, b):
    mu = jnp.mean(z, axis=1, keepdims=True)
    zc = z - mu
    var = jnp.mean(zc * zc, axis=1, keepdims=True)
    return zc * lax.rsqrt(var + LN_EPS) * g + b


def _pack_halves(x):
    w = x.shape[1] // 2
    lo = pltpu.bitcast(x[:, :w].astype(BF16).astype(F32), U32)
    hi = pltpu.bitcast(x[:, w:].astype(BF16).astype(F32), U32)
    return (lo >> 16) | (hi & jnp.uint32(0xFFFF0000))


def _unpack_halves(words):
    lo = pltpu.bitcast(words << 16, F32).astype(BF16)
    hi = pltpu.bitcast(words & jnp.uint32(0xFFFF0000), F32).astype(BF16)
    return lo, hi


def _to_token_tiles(ref, x):
    for q in range(x.shape[1] // LANES):
        ref[:, q, :] = x[:, q * LANES:(q + 1) * LANES]


def _merge_kernel(h_ref, om_ref, att_ref, x_ref, gm_ref, ga_ref, wo_ref, g1_ref, b1_ref, wr_ref, br_ref,
                  x1_ref, xp_ref, tk_ref, cnt_ref):
    @pl.when(pl.program_id(0) == 0)
    def _():
        cnt_ref[...] = jnp.zeros_like(cnt_ref)

    hm = h_ref[...] * _sigmoid(om_ref[...])
    ym = hm * lax.rsqrt(jnp.mean(hm * hm, axis=1, keepdims=True) + LN_EPS) * gm_ref[...]
    att = att_ref[...]
    ya = att * lax.rsqrt(jnp.mean(att * att, axis=1, keepdims=True) + LN_EPS) * ga_ref[...]
    mix = (jnp.dot(ym.astype(BF16), wo_ref[0:M_WIDTH, :], preferred_element_type=F32)
           + jnp.dot(ya.astype(BF16), wo_ref[M_WIDTH:, :], preferred_element_type=F32))
    x1 = _layer_norm(DN_ALPHA * x_ref[...] + mix, g1_ref[...], b1_ref[...])
    x1_ref[...] = x1
    _to_token_tiles(xp_ref, _pack_halves(x1))
    logits = jnp.dot(x1.astype(BF16), wr_ref[...], preferred_element_type=F32) + br_ref[...]
    lane = lax.broadcasted_iota(I32, logits.shape, 1).astype(F32)
    vals, idxs = [], []
    for _ in range(TOP_K):
        mx = jnp.max(logits, axis=1, keepdims=True)
        idx = jnp.min(jnp.where(logits == mx, lane, float(LANES)), axis=1, keepdims=True)
        vals.append(mx)
        idxs.append(idx)
        logits = jnp.where(lane == idx, 2.0 * NEG, logits)
    es = [jnp.exp(vk - vals[0]) for vk in vals]
    tot = es[0] + es[1] + es[2] + es[3]
    tk = jnp.zeros(logits.shape, F32)
    picked = jnp.zeros(logits.shape, F32)
    for k in range(TOP_K):
        tk = jnp.where(lane == float(k), es[k] / tot, tk)
        tk = jnp.where(lane == float(TOP_K + k), idxs[k], tk)
        picked = jnp.where(lane == idxs[k], 1.0, picked)
    tk_ref[...] = tk
    cnt_ref[...] = cnt_ref[...] + jnp.sum(picked, axis=0, keepdims=True)


def _merge(h, om, att, x, gm, ga, wo, g1, b1, wr, br, tile, name):
    t = x.shape[0]
    rows = lambda w: pl.BlockSpec((tile, w), lambda i: (i, 0))
    const = lambda a: pl.BlockSpec(a.shape, lambda i: (0, 0))
    return pl.pallas_call(
        _merge_kernel,
        out_shape=[jax.ShapeDtypeStruct((t, D_MODEL), F32), jax.ShapeDtypeStruct((t, XP_TILE, LANES), U32),
                   jax.ShapeDtypeStruct((t, LANES), F32), jax.ShapeDtypeStruct((8, LANES), F32)],
        grid=(t // tile,),
        in_specs=[rows(M_WIDTH), rows(M_WIDTH), rows(A_WIDTH), rows(D_MODEL), const(gm), const(ga), const(wo),
                  const(g1), const(b1), const(wr), const(br)],
        out_specs=[rows(D_MODEL), pl.BlockSpec((tile, XP_TILE, LANES), lambda i: (i, 0, 0)), rows(LANES),
                   pl.BlockSpec((8, LANES), lambda i: (0, 0))],
        compiler_params=_cparams("arbitrary"), name=name,
    )(h, om, att, x, gm, ga, wo, g1, b1, wr, br)


def _route_kernel(tk_ref, first_ref, strict_ref, dest_ref, next_scr):
    @pl.when(pl.program_id(0) == 0)
    def _():
        next_scr[...] = first_ref[...]

    tk = tk_ref[...]
    lane = lax.broadcasted_iota(I32, tk.shape, 1).astype(F32)
    onehots = [jnp.where(lane == tk[:, TOP_K + k:TOP_K + k + 1], 1.0, 0.0) for k in range(TOP_K)]
    tot = onehots[0] + onehots[1] + onehots[2] + onehots[3]
    row = jnp.dot(strict_ref[...], tot.astype(BF16), preferred_element_type=F32) + next_scr[0:1, :]
    out = jnp.zeros(tk.shape, F32)
    for k in range(TOP_K):
        out = jnp.where(lane == float(k), jnp.sum(onehots[k] * row, axis=1, keepdims=True), out)
    dest_ref[...] = out.astype(I32)
    next_scr[...] = next_scr[...] + jnp.sum(tot, axis=0, keepdims=True)


def _route(tk, first):
    t = tk.shape[0]
    tile = min(RANK_TILE, t)
    strict = jnp.asarray(np.tril(np.ones((tile, tile), np.float32), -1), BF16)
    return pl.pallas_call(
        _route_kernel, out_shape=jax.ShapeDtypeStruct((t, LANES), I32), grid=(t // tile,),
        in_specs=[pl.BlockSpec((tile, LANES), lambda i: (i, 0)), pl.BlockSpec((8, LANES), lambda i: (0, 0)),
                  pl.BlockSpec((tile, tile), lambda i: (0, 0))],
        out_specs=pl.BlockSpec((tile, LANES), lambda i: (i, 0)),
        scratch_shapes=[pltpu.VMEM((8, LANES), F32)],
        compiler_params=_cparams("arbitrary"), name="moe_route",
    )(tk, first, strict)


def _offsets_kernel(cnt_ref, off_ref, be_ref, nu_ref, *, tile):
    cnt = cnt_ref[...]
    nblk = jnp.floor((cnt + float(tile - 1)) * (1.0 / tile))
    r_i = lax.broadcasted_iota(I32, (LANES, LANES), 0)
    c_i = lax.broadcasted_iota(I32, (LANES, LANES), 1)
    incl = jnp.where(r_i <= c_i, 1.0, 0.0).astype(BF16)
    cum = jnp.dot(nblk.astype(BF16), incl, preferred_element_type=F32)
    off_ref[...] = (cum - nblk) * float(tile)
    rows = be_ref.shape[0]
    jb = (lax.broadcasted_iota(I32, (rows, LANES), 0) * LANES + lax.broadcasted_iota(I32, (rows, LANES), 1)).astype(F32)
    acc = jnp.zeros((rows, LANES), F32)
    for e in range(N_EXPERTS):
        acc = acc + jnp.where(jb >= cum[0:1, e:e + 1], 1.0, 0.0)
    be_ref[...] = jnp.minimum(acc, float(N_EXPERTS - 1)).astype(I32)
    nu_ref[...] = jnp.broadcast_to(cum[0:1, N_EXPERTS - 1:N_EXPERTS], nu_ref.shape).astype(I32)


def _offsets(cnt, n_blocks, tile):
    rows = -(-n_blocks // LANES)
    rows = -(-rows // 8) * 8
    return pl.pallas_call(
        functools.partial(_offsets_kernel, tile=tile),
        out_shape=[jax.ShapeDtypeStruct((8, LANES), F32), jax.ShapeDtypeStruct((rows, LANES), I32),
                   jax.ShapeDtypeStruct((8, LANES), I32)],
        name="moe_offsets",
    )(cnt)


def _dispatch_kernel(dest_ref, xp_ref, xs_in_ref, xs_ref, sem):
    del xs_in_ref
    t = xp_ref.shape[0]

    def row_copy(tok, dst):
        return pltpu.make_async_copy(xp_ref.at[pl.ds(tok, 1)], xs_ref.at[pl.ds(dst, 1)], sem)

    def issue(grp, carry):
        base = pl.multiple_of(grp * ISSUE_GROUP, ISSUE_GROUP)
        for u in range(ISSUE_GROUP):
            for k in range(TOP_K):
                row_copy(base + u, dest_ref[(base + u) * TOP_K + k]).start(priority=k % 2)
        return carry

    lax.fori_loop(0, t // ISSUE_GROUP, issue, 0)
    for k in range(TOP_K):
        pltpu.make_async_copy(xp_ref, xs_ref.at[pl.ds(0, t)], sem).wait()


def _dispatch(dest_flat, xp, xs):
    t = xp.shape[0]
    tile = min(ROW_TILE, t)
    return pl.pallas_call(
        _dispatch_kernel, out_shape=jax.ShapeDtypeStruct(xs.shape, xs.dtype), grid=(t // tile,),
        in_specs=[pl.BlockSpec((tile * TOP_K,), lambda i: (i,), memory_space=pltpu.SMEM),
                  pl.BlockSpec((tile,) + xp.shape[1:], lambda i: (i, 0, 0)),
                  pl.BlockSpec(memory_space=pl.ANY)],
        out_specs=pl.BlockSpec(memory_space=pl.ANY),
        scratch_shapes=[pltpu.SemaphoreType.DMA(())],
        input_output_aliases={2: 0},
        compiler_params=_cparams("arbitrary"), name="moe_dispatch",
    )(dest_flat, xp, xs)


def _expert_kernel(be_ref, nu_ref, xs_ref, w1_ref, b1g_ref, b1l_ref, w2_ref, b2_ref, perm_ref, ys_ref,
                   w1g_scr, w1l_scr, w2_scr, xq_scr, y_scr, sem, osem):
    j = pl.program_id(0)
    active = j < nu_ref[0]
    changed = jnp.logical_or(j == 0, be_ref[j] != be_ref[jnp.maximum(j - 1, 0)])
    half = D_MODEL // 2
    tm = y_scr.shape[0]
    slot = lax.rem(j, 2)

    def fetch(blk, slot):
        row0 = pl.multiple_of(blk * tm, tm)
        return [pltpu.make_async_copy(xs_ref.at[pl.ds(row0, tm), q, :], xq_scr.at[slot, q], sem.at[slot])
                for q in range(XP_TILE)]

    def put(blk):
        row0 = pl.multiple_of(blk * tm, tm)
        return [pltpu.make_async_copy(y_scr.at[:, q * LANES:(q + 1) * LANES], ys_ref.at[pl.ds(row0, tm), q, :], osem)
                for q in range(YS_TILE)]

    def emit(y):
        @pl.when(j > 0)
        def _():
            for cp in put(j - 1):
                cp.wait()

        y_scr[...] = y
        for cp in put(j):
            cp.start()

    @pl.when(j == 0)
    def _():
        for cp in fetch(0, 0):
            cp.start()

    @pl.when(j + 1 < nu_ref[0])
    def _():
        for cp in fetch(j + 1, 1 - slot):
            cp.start()

    @pl.when(jnp.logical_and(active, changed))
    def _():
        for c in range(2 * D_FF // 256):
            wc = w1_ref[0, :, c * 256:(c + 1) * 256].astype(BF16)
            d = jnp.dot(wc, perm_ref[...], preferred_element_type=F32).astype(BF16)
            w1g_scr[:, c * 128:(c + 1) * 128] = d[:, :128]
            w1l_scr[:, c * 128:(c + 1) * 128] = d[:, 128:]
        for c in range(D_FF // 256):
            w2_scr[c * 256:(c + 1) * 256, :] = w2_ref[0, c * 256:(c + 1) * 256, :].astype(BF16)

    @pl.when(active)
    def _():
        for cp in fetch(j, slot):
            cp.wait()
        lo, hi = _unpack_halves(jnp.concatenate([xq_scr[slot, q] for q in range(XP_TILE)], axis=1))
        y = jnp.zeros(y_scr.shape, F32)
        nchunk = 512
        for c in range(D_FF // nchunk):
            cs = slice(c * nchunk, (c + 1) * nchunk)
            hg = (jnp.dot(lo, w1g_scr[0:half, cs], preferred_element_type=F32)
                  + jnp.dot(hi, w1g_scr[half:, cs], preferred_element_type=F32) + b1g_ref[0, :, cs])
            hl = (jnp.dot(lo, w1l_scr[0:half, cs], preferred_element_type=F32)
                  + jnp.dot(hi, w1l_scr[half:, cs], preferred_element_type=F32) + b1l_ref[0, :, cs])
            x_glu = jnp.minimum(hg, SWIGLU_LIMIT)
            x_lin = jnp.clip(hl, -SWIGLU_LIMIT, SWIGLU_LIMIT)
            a = x_glu * _sigmoid(SWIGLU_ALPHA * x_glu) * (x_lin + 1.0)
            y = y + jnp.dot(a.astype(BF16), w2_scr[cs, :], preferred_element_type=F32)
        emit(y + b2_ref[0])

    @pl.when(jnp.logical_not(active))
    def _():
        emit(jnp.zeros(y_scr.shape, F32))

    @pl.when(j == pl.num_programs(0) - 1)
    def _():
        for cp in put(j):
            cp.wait()


def _experts(be, nu, xs, w1, b1g, b1l, w2, b2, perm, tile):
    n_blocks = xs.shape[0] // tile
    grid_spec = pltpu.PrefetchScalarGridSpec(
        num_scalar_prefetch=2, grid=(n_blocks,),
        in_specs=[pl.BlockSpec(memory_space=pl.ANY),
                  pl.BlockSpec((1, D_MODEL, 2 * D_FF), lambda j, be, nu: (be[j], 0, 0)),
                  pl.BlockSpec((1, 1, D_FF), lambda j, be, nu: (be[j], 0, 0)),
                  pl.BlockSpec((1, 1, D_FF), lambda j, be, nu: (be[j], 0, 0)),
                  pl.BlockSpec((1, D_FF, D_MODEL), lambda j, be, nu: (be[j], 0, 0)),
                  pl.BlockSpec((1, 1, D_MODEL), lambda j, be, nu: (be[j], 0, 0)),
                  pl.BlockSpec((256, 256), lambda j, be, nu: (0, 0))],
        out_specs=pl.BlockSpec(memory_space=pl.ANY),
        scratch_shapes=[pltpu.VMEM((D_MODEL, D_FF), BF16), pltpu.VMEM((D_MODEL, D_FF), BF16),
                        pltpu.VMEM((D_FF, D_MODEL), BF16), pltpu.VMEM((2, XP_TILE, tile, LANES), U32),
                        pltpu.VMEM((tile, D_MODEL), F32), pltpu.SemaphoreType.DMA((2,)),
                        pltpu.SemaphoreType.DMA(())])
    return pl.pallas_call(
        _expert_kernel, out_shape=jax.ShapeDtypeStruct((xs.shape[0], YS_TILE, LANES), F32), grid_spec=grid_spec,
        compiler_params=_cparams("arbitrary"), name="moe_experts",
    )(be, nu, xs, w1, b1g, b1l, w2, b2, perm)


def _combine_kernel(dest_ref, ys_ref, tk_ref, x1_ref, g2_ref, b2_ref, out_ref, buf, sem):
    t = x1_ref.shape[0]

    def issue(grp, carry):
        base = pl.multiple_of(grp * 8, 8)
        for u in range(8):
            for k in range(TOP_K):
                pltpu.make_async_copy(ys_ref.at[dest_ref[(base + u) * TOP_K + k]],
                                      buf.at[k, grp, :, u, :], sem).start(priority=k % 2)
        return carry

    lax.fori_loop(0, t // 8, issue, 0)
    for k in range(TOP_K):
        for u in range(8):
            pltpu.make_async_copy(ys_ref.at[pl.ds(0, t // 8)], buf.at[k, :, :, u, :], sem).wait()
    tk = tk_ref[...]
    chunks = []
    for q in range(YS_TILE):
        c = tk[:, 0:1] * buf[0, :, q].reshape(t, LANES)
        for k in range(1, TOP_K):
            c = c + tk[:, k:k + 1] * buf[k, :, q].reshape(t, LANES)
        chunks.append(c)
    ff = jnp.concatenate(chunks, axis=1)
    out_ref[...] = _layer_norm(DN_ALPHA * x1_ref[...] + ff, g2_ref[...], b2_ref[...])


def _combine(dest_flat, ys, tk, x1, g2, b2):
    t = x1.shape[0]
    tile = min(ROW_TILE, t)
    return pl.pallas_call(
        _combine_kernel, out_shape=jax.ShapeDtypeStruct((t, D_MODEL), F32), grid=(t // tile,),
        in_specs=[pl.BlockSpec((tile * TOP_K,), lambda i: (i,), memory_space=pltpu.SMEM),
                  pl.BlockSpec(memory_space=pl.ANY),
                  pl.BlockSpec((tile, LANES), lambda i: (i, 0)),
                  pl.BlockSpec((tile, D_MODEL), lambda i: (i, 0)),
                  pl.BlockSpec((1, D_MODEL), lambda i: (0, 0)),
                  pl.BlockSpec((1, D_MODEL), lambda i: (0, 0))],
        out_specs=pl.BlockSpec((tile, D_MODEL), lambda i: (i, 0)),
        scratch_shapes=[pltpu.VMEM((TOP_K, tile // 8, YS_TILE, 8, LANES), F32), pltpu.SemaphoreType.DMA(())],
        compiler_params=_cparams("arbitrary"), name="moe_combine",
    )(dest_flat, ys, tk, x1, g2, b2)


def _rel_bucket(dist):
    exact = REL_BUCKETS // 2
    d = np.maximum(dist, 0)
    log_b = exact + (np.log(np.maximum(d, 1).astype(np.float32) / np.float32(exact))
                     / np.float32(math.log(REL_MAX_DIST / exact)) * np.float32(REL_BUCKETS - exact)).astype(np.int32)
    return np.where(d < exact, d, np.minimum(log_b, REL_BUCKETS - 1)).astype(np.int32)


def _bias_lookup(table, bucket, valid):
    bucket = jnp.asarray(bucket)[None]
    acc = jnp.zeros((table.shape[1],) + bucket.shape[1:], F32)
    for b in range(REL_BUCKETS):
        acc = jnp.where(bucket == b, table[b].reshape((-1,) + (1,) * (bucket.ndim - 1)), acc)
    return jnp.where(jnp.asarray(valid)[None], acc, NEG)


def _bias_tables(rel_bias):
    table = rel_bias.astype(F32)
    r = np.arange(WINDOW)[:, None]
    c = np.arange(2 * WINDOW)[None, :]
    dist = r + WINDOW - c
    valid = (dist >= 0) & (dist < WINDOW)
    dist0 = np.where(c < N_META, N_META + r - c, dist)
    valid0 = np.where(c < N_META, dist0 < WINDOW, (c >= WINDOW) & valid)
    both = jnp.stack([_bias_lookup(table, _rel_bucket(dist0), valid0), _bias_lookup(table, _rel_bucket(dist), valid)])
    dist_s = WINDOW - 1 - np.arange(WINDOW)
    rows = _bias_lookup(table[:, np.asarray(HEAD_ORDER)], _rel_bucket(dist_s), np.ones_like(dist_s, bool))
    return both, rows


def _perm_heads(a, axis):
    parts = [lax.slice_in_dim(a, h * A_HD, (h + 1) * A_HD, axis=axis) for h in HEAD_ORDER]
    return jnp.concatenate(parts, axis=axis)


def _rep_rows(vec, rows=8):
    out = jnp.zeros((rows, LANES), F32)
    return out.at[:vec.shape[0], :].set(jnp.broadcast_to(vec.astype(F32)[:, None], (vec.shape[0], LANES)))


def kernel(x_prompt, x_sample, cache_swa_k, cache_swa_v, state_mlstm_C, state_mlstm_n, state_mlstm_m, meta_tokens, rel_bias, w_in, b_igate, b_fgate, attn_sinks, g_mlstm_out, g_attn_out, w_out, ln1_g, ln1_b, w_router, b_router, w_moe1, b_moe1, w_moe2, b_moe2, ln2_g, ln2_b):
    B, S, _ = x_prompt.shape
    NB = x_sample.shape[0]
    assert x_sample.shape[1] == 1 and w_in.shape[0] == 1
    assert S % PROJ_TILE == 0 and S % M_CHUNK == 0 and S % WINDOW == 0 and NB % SAMPLE_GROUP == 0
    l = 0

    pts = np.cumsum(IN_WIDTHS)[:-1].tolist()
    w_qm, w_km, w_vm, w_om, w_ig, w_fg, w_qa, w_ka, w_va = jnp.split(w_in[l], pts, axis=1)
    w_gate = jnp.pad(jnp.concatenate([w_ig, w_fg], axis=1), ((0, 0), (0, LANES - 2 * M_HEADS)))
    w_qa = _perm_heads(w_qa, 1)
    b_gate = jnp.concatenate([b_igate[l], b_fgate[l]]).astype(F32)
    brow = jnp.pad(b_gate, (0, LANES - 2 * M_HEADS))[None, :]
    bcol = b_gate[:, None]
    bf = lambda a: a.astype(BF16)
    wr_p = bf(jnp.concatenate([w_qm, w_vm, w_om, w_qa, w_ka, w_va, w_gate], axis=1))
    wt_p = bf(jnp.concatenate([w_km.T, w_ig.T, w_fg.T], axis=0))
    plan_p = ((0, 512, "plain", BF16), (512, 512, "plain", BF16), (1024, 512, "plain", F32),
              (1536, 512, "plain", BF16), (2048, 128, "plain", BF16), (2176, 128, "plain", BF16),
              (2304, 128, "gate", F32))
    tplan_p = ((0, 512, "plain", BF16), (512, 8, "gate", F32))
    wr_s = bf(jnp.concatenate([w_qm, w_km, w_vm, w_om, w_qa, w_ka, w_va, w_gate], axis=1))
    plan_s = ((0, 512, "plain", F32), (512, 512, "plain", F32), (1024, 512, "plain", F32), (1536, 512, "plain", F32),
              (2048, 512, "plain", F32), (2560, 128, "plain", F32), (2688, 128, "plain", F32), (2816, 128, "gate", F32))

    bias_tab, bias_rows = _bias_tables(rel_bias)
    sinks = _rep_rows(attn_sinks[l])
    sinks_step = _rep_rows(attn_sinks[l][np.asarray(HEAD_ORDER)])
    g_m = g_mlstm_out[l].astype(F32)[None, :]
    g_a = _perm_heads(g_attn_out[l].astype(F32), 0)[None, :]
    wo = bf(jnp.concatenate([w_out[l][:M_WIDTH], _perm_heads(w_out[l][M_WIDTH:], 0)], axis=0))
    g1, b1 = ln1_g[l].astype(F32)[None, :], ln1_b[l].astype(F32)[None, :]
    g2, b2 = ln2_g[l].astype(F32)[None, :], ln2_b[l].astype(F32)[None, :]
    w_r = bf(jnp.pad(w_router[l], ((0, 0), (0, LANES - N_EXPERTS))))
    b_r = jnp.pad(b_router[l].astype(F32), (0, LANES - N_EXPERTS), constant_values=NEG)[None, :]
    b1g = b_moe1[l][:, 0::2].astype(F32)[:, None, :]
    b1l = b_moe1[l][:, 1::2].astype(F32)[:, None, :]
    b2e = b_moe2[l].astype(F32)[:, None, :]
    pj = np.zeros((256, 256), np.float32)
    pj[2 * np.arange(128), np.arange(128)] = 1.0
    pj[2 * np.arange(128) + 1, 128 + np.arange(128)] = 1.0
    perm = jnp.asarray(pj, BF16)

    xp2 = x_prompt.reshape(B * S, D_MODEL)
    qm, vm, om, qa, ka, va, gc, kt, gr, kv_tail = _proj(
        xp2, wr_p, wt_p, brow, bcol, plan_p, tplan_p, (2048, 256), PROJ_TILE, S, "proj_prompt")
    x_meta = jnp.pad(meta_tokens.astype(F32), ((0, M_CHUNK - N_META), (0, 0)))
    qm0, vm0, _, _, ka0, va0, gc0, kt0, gr0 = _proj(
        x_meta, wr_p, wt_p, brow, bcol, plan_p, tplan_p, None, M_CHUNK, M_CHUNK, "proj_meta")
    xs2 = x_sample.reshape(NB, D_MODEL)
    qm_s, km_s, vm_s, om_s, qa_s, ka_s, va_s, gc_s = _proj(
        xs2, wr_s, wt_p, brow, bcol, plan_s, (), None, NB, NB, "proj_sample")

    zero_c = jnp.zeros((M_HEADS, M_DK, 2 * M_DV), F32)
    zero_m = jnp.zeros((8, LANES), F32)
    _, ct_meta, m_meta = _mlstm(qm0, vm0, kt0, gc0, gr0, zero_c, zero_m, 1, N_META, "mlstm_meta")
    c0 = jnp.swapaxes(ct_meta[0], 1, 2)
    h_p, ct_p, m_p = _mlstm(qm, vm, kt, gc, gr, c0, m_meta[0], B, M_CHUNK, "mlstm_prompt")
    C_p = ct_p[:, :, :M_DV, :]
    n_p = ct_p[:, :, M_DV, :]
    m_prompt = m_p[:, :M_HEADS, 0]
    m_pad = jnp.pad(state_mlstm_m[l].astype(F32), ((0, 0), (0, LANES - M_HEADS)))
    C_s, n_s, m_s, h_s = _mlstm_step(state_mlstm_C[l].astype(F32), state_mlstm_n[l].astype(F32), m_pad,
                                     gc_s, qm_s, km_s, vm_s)

    att_p = _swa(qa, ka, va, ka0, va0, bias_tab, sinks, B)
    ck = cache_swa_k[l].reshape(NB, WINDOW, LANES)
    cv = cache_swa_v[l].reshape(NB, WINDOW, LANES)
    k_new, v_new, att_s = _swa_step(ck, cv, qa_s, ka_s, va_s, bias_rows, sinks_step)

    x1_p, xpk_p, tk_p, cnt_p = _merge(h_p, om, att_p, xp2, g_m, g_a, wo, g1, b1, w_r, b_r, MERGE_TILE, "merge_prompt")
    x1_s, xpk_s, tk_s, cnt_s = _merge(h_s, om_s, att_s, xs2, g_m, g_a, wo, g1, b1, w_r, b_r, NB, "merge_sample")

    T_p = B * S
    assert T_p % RANK_TILE == 0 and T_p % ROW_TILE == 0
    n_blocks = -(-((T_p + NB) * TOP_K) // EXPERT_TILE) + N_EXPERTS
    off, be2, nu2 = _offsets(cnt_p + cnt_s, n_blocks, EXPERT_TILE)
    dest_p = _route(tk_p, off)[:, :TOP_K].reshape(-1)
    dest_s = _route(tk_s, off + cnt_p)[:, :TOP_K].reshape(-1)
    be = be2.reshape(-1)[:n_blocks]
    nu = nu2[0, :1]
    xs = jnp.zeros((n_blocks * EXPERT_TILE, XP_TILE, LANES), U32)
    xs = _dispatch(dest_p, xpk_p, xs)
    xs = _dispatch(dest_s, xpk_s, xs)
    ys = _experts(be, nu, xs, w_moe1[l], b1g, b1l, w_moe2[l], b2e, perm, EXPERT_TILE)
    y_p = _combine(dest_p, ys, tk_p, x1_p, g2, b2)
    y_s = _combine(dest_s, ys, tk_s, x1_s, g2, b2)

    kv_tail = kv_tail.reshape(B, WINDOW, 2, A_KV_HEADS, A_HD)
    dt_k, dt_v = cache_swa_k.dtype, cache_swa_v.dtype
    return (y_p.reshape(B, S, D_MODEL).astype(x_prompt.dtype), y_s.reshape(NB, 1, D_MODEL).astype(x_sample.dtype),
            kv_tail[:, :, 0][None].astype(dt_k), kv_tail[:, :, 1][None].astype(dt_v),
            C_p[None].astype(state_mlstm_C.dtype), n_p[None].astype(state_mlstm_n.dtype),
            m_prompt[None].astype(state_mlstm_m.dtype),
            k_new.reshape(1, NB, WINDOW, A_KV_HEADS, A_HD).astype(dt_k),
            v_new.reshape(1, NB, WINDOW, A_KV_HEADS, A_HD).astype(dt_v),
            C_s[None].astype(state_mlstm_C.dtype), n_s[None].astype(state_mlstm_n.dtype),
            m_s[:, :M_HEADS][None].astype(state_mlstm_m.dtype))
```

```python
import functools
import math

import numpy as np
import jax
import jax.numpy as jnp
from jax import lax
from jax.experimental import pallas as pl
from jax.experimental.pallas import tpu as pltpu

F32 = jnp.float32
BF16 = jnp.bfloat16
I32 = jnp.int32
U32 = jnp.uint32

D_MODEL = 1024
N_META = 16
M_HEADS = 4
M_DK = 128
M_DV = 128
M_WIDTH = M_HEADS * M_DV
A_HD = 64
A_HEADS = 8
A_KV_HEADS = 2
A_GROUP = A_HEADS // A_KV_HEADS
A_WIDTH = A_HEADS * A_HD
WINDOW = 128
REL_BUCKETS = 32
REL_MAX_DIST = 128
N_EXPERTS = 32
TOP_K = 4
D_FF = D_MODEL
SWIGLU_LIMIT = 7.0
SWIGLU_ALPHA = 1.702
DEPTH = 1
DN_ALPHA = (2.0 * DEPTH) ** 0.25
LN_EPS = 1e-5
IN_WIDTHS = (M_WIDTH, M_WIDTH, M_WIDTH, M_WIDTH, M_HEADS, M_HEADS, A_WIDTH, A_KV_HEADS * A_HD, A_KV_HEADS * A_HD)

LANES = 128
NEG = -1e30
VMEM_LIMIT = 56 * 1024 * 1024

M_CHUNK = 128
PROJ_TILE = 512
MERGE_TILE = 512
RANK_TILE = 512
ROW_TILE = 256
EXPERT_TILE = 512
SAMPLE_GROUP = 8
ISSUE_GROUP = 8
SWA_QBLOCKS = 2
XP_TILE = D_MODEL // 2 // LANES
YS_TILE = D_MODEL // 2 // LANES
HEAD_ORDER = (0, 4, 1, 5, 2, 6, 3, 7)


def _cparams(*sem):
    return pltpu.CompilerParams(dimension_semantics=sem, vmem_limit_bytes=VMEM_LIMIT)


def _log_sigmoid(x):
    return jnp.minimum(x, 0.0) - jnp.log1p(jnp.exp(-jnp.abs(x)))


def _sigmoid(x):
    return 1.0 / (1.0 + jnp.exp(-x))


def _proj_kernel(x_ref, wr_ref, wt_ref, brow_ref, bcol_ref, *outs, row_plan, t_plan, tail_cols):
    xb = x_ref[...].astype(BF16)
    tm = xb.shape[0]
    o = 0
    for (c0, width, kind, _) in row_plan:
        r = jnp.dot(xb, wr_ref[:, c0:c0 + width], preferred_element_type=F32)
        if kind == "gate":
            r = r + brow_ref[...]
            lane = lax.broadcasted_iota(I32, r.shape, 1)
            r = jnp.where(lane < M_HEADS, r, _log_sigmoid(r))
        outs[o][...] = r.astype(outs[o].dtype)
        o += 1
    for (r0, nrows, kind, _) in t_plan:
        r = lax.dot_general(wt_ref[r0:r0 + nrows, :], xb, (((1,), (1,)), ((), ())), preferred_element_type=F32)
        if kind == "gate":
            r = r + bcol_ref[...]
            row = lax.broadcasted_iota(I32, r.shape, 0)
            r = jnp.where(row < M_HEADS, r, _log_sigmoid(r))
        outs[o][...] = r.astype(outs[o].dtype)
        o += 1
    if tail_cols is not None:
        c0, width = tail_cols
        outs[o][...] = jnp.dot(xb[tm - WINDOW:, :], wr_ref[:, c0:c0 + width], preferred_element_type=F32)


def _proj(x, wr, wt, brow, bcol, row_plan, t_plan, tail_cols, tile, rows_per_group, name):
    t = x.shape[0]
    nt = t // tile
    out_shape, out_specs = [], []
    for (_, width, _, dt) in row_plan:
        out_shape.append(jax.ShapeDtypeStruct((t, width), dt))
        out_specs.append(pl.BlockSpec((tile, width), lambda i: (i, 0)))
    for (_, nrows, _, dt) in t_plan:
        out_shape.append(jax.ShapeDtypeStruct((nrows, t), dt))
        out_specs.append(pl.BlockSpec((nrows, tile), lambda i: (0, i)))
    if tail_cols is not None:
        tiles_per_group = rows_per_group // tile
        out_shape.append(jax.ShapeDtypeStruct((t // rows_per_group * WINDOW, tail_cols[1]), F32))
        out_specs.append(pl.BlockSpec((WINDOW, tail_cols[1]), lambda i: (i // tiles_per_group, 0)))
    kern = functools.partial(_proj_kernel, row_plan=row_plan, t_plan=t_plan, tail_cols=tail_cols)
    return pl.pallas_call(
        kern, out_shape=out_shape, grid=(nt,),
        in_specs=[pl.BlockSpec((tile, D_MODEL), lambda i: (i, 0)),
                  pl.BlockSpec(wr.shape, lambda i: (0, 0)),
                  pl.BlockSpec(wt.shape, lambda i: (0, 0)),
                  pl.BlockSpec(brow.shape, lambda i: (0, 0)),
                  pl.BlockSpec(bcol.shape, lambda i: (0, 0))],
        out_specs=out_specs, compiler_params=_cparams("arbitrary"), name=name,
    )(x, wr, wt, brow, bcol)


def _split3(a):
    hi = a.astype(BF16)
    r1 = a - hi.astype(F32)
    mid = r1.astype(BF16)
    lo = (r1 - mid.astype(F32)).astype(BF16)
    return hi, mid, lo


def _mlstm_kernel(q_ref, v_ref, kt_ref, gc_ref, gr_ref, c0_ref, m0_ref, h_ref, ct_out_ref, m_out_ref,
                  ct_scr, m_scr, *, n_valid):
    c = pl.program_id(1)
    nc = pl.num_programs(1)
    L = q_ref.shape[0]

    @pl.when(c == 0)
    def _():
        ct_scr[...] = c0_ref[...]
        m_scr[...] = m0_ref[...]

    gc = gc_ref[...]
    gr = gr_ref[...]
    if n_valid < L:
        rowc = lax.broadcasted_iota(I32, gc.shape, 0)
        lanec = lax.broadcasted_iota(I32, gc.shape, 1)
        gc = jnp.where(rowc < n_valid, gc, jnp.where(lanec < M_HEADS, NEG, 0.0))
        rowr = lax.broadcasted_iota(I32, gr.shape, 0)
        colr = lax.broadcasted_iota(I32, gr.shape, 1)
        gr = jnp.where(colr < n_valid, gr, jnp.where(rowr < M_HEADS, NEG, 0.0))
    r_i = lax.broadcasted_iota(I32, (L, L), 0)
    c_i = lax.broadcasted_iota(I32, (L, L), 1)
    causal = c_i <= r_i
    tril = jnp.where(causal, 1.0, 0.0).astype(BF16)
    triu = jnp.where(r_i <= c_i, 1.0, 0.0).astype(BF16)
    b_cols = sum(jnp.dot(tril, part, preferred_element_type=F32) for part in _split3(gc))
    b_rows = sum(jnp.dot(part, triu, preferred_element_type=F32) for part in _split3(gr))
    lane_l = lax.broadcasted_iota(I32, (L, LANES), 1)
    e0 = jnp.where(lane_l == 0, 1.0, 0.0)
    scale = M_DK ** -0.5

    for h in range(M_HEADS):
        sl = slice(h * M_DK, (h + 1) * M_DK)
        q = q_ref[:, sl]
        v = v_ref[:, sl]
        kt = kt_ref[sl, :]
        ig_c = gc[:, h:h + 1]
        b_c = b_cols[:, M_HEADS + h:M_HEADS + h + 1]
        ig_r = gr[h:h + 1, :]
        b_r = b_rows[M_HEADS + h:M_HEADS + h + 1, :]
        m_prev = m_scr[h:h + 1, 0:1]
        ct = ct_scr[h]

        d = jnp.where(causal, b_c + (ig_r - b_r), NEG)
        m_t = jnp.maximum(b_c + m_prev, jnp.max(d, axis=1, keepdims=True))
        qk = jnp.dot(q, kt, preferred_element_type=F32) * scale
        s = qk * jnp.exp(d - m_t)
        inter = jnp.dot(q, ct.astype(BF16), preferred_element_type=F32)
        v_aug = jnp.concatenate([v, e0.astype(BF16)], axis=1)
        intra = jnp.dot(s.astype(BF16), v_aug, preferred_element_type=F32)
        nd = jnp.exp(b_c + m_prev - m_t) * inter + intra
        den = nd[:, M_DV:M_DV + 1]
        h_ref[:, sl] = nd[:, :M_DV] / jnp.maximum(jnp.abs(den), jnp.exp(-m_t))

        b_last = b_c[L - 1:L, :]
        g = ig_c + b_last - b_c
        m_new = jnp.maximum(b_last + m_prev, jnp.max(g, axis=0, keepdims=True))
        a = jnp.exp(b_last + m_prev - m_new)
        wg = jnp.exp(g - m_new)
        wv = jnp.concatenate([(v.astype(F32) * wg).astype(BF16), (e0 * wg).astype(BF16)], axis=1)
        upd = jnp.dot(kt, wv, preferred_element_type=F32)
        ct_scr[h] = a * ct + upd * scale
        m_scr[h:h + 1, :] = jnp.broadcast_to(m_new, (1, LANES))

    @pl.when(c == nc - 1)
    def _():
        for h in range(M_HEADS):
            ct_out_ref[0, h] = ct_scr[h].T
        m_out_ref[0] = m_scr[...]


def _mlstm(qm, vm, kt, gc, gr, c0, m0, batch, n_valid, name):
    L = M_CHUNK
    nc = qm.shape[0] // (batch * L)
    kern = functools.partial(_mlstm_kernel, n_valid=n_valid)
    return pl.pallas_call(
        kern,
        out_shape=[jax.ShapeDtypeStruct((batch * nc * L, M_WIDTH), F32),
                   jax.ShapeDtypeStruct((batch, M_HEADS, 2 * M_DV, M_DK), F32),
                   jax.ShapeDtypeStruct((batch, 8, LANES), F32)],
        grid=(batch, nc),
        in_specs=[pl.BlockSpec((L, M_WIDTH), lambda b, c: (b * nc + c, 0)),
                  pl.BlockSpec((L, M_WIDTH), lambda b, c: (b * nc + c, 0)),
                  pl.BlockSpec((M_WIDTH, L), lambda b, c: (0, b * nc + c)),
                  pl.BlockSpec((L, LANES), lambda b, c: (b * nc + c, 0)),
                  pl.BlockSpec((8, L), lambda b, c: (0, b * nc + c)),
                  pl.BlockSpec((M_HEADS, M_DK, 2 * M_DV), lambda b, c: (0, 0, 0)),
                  pl.BlockSpec((8, LANES), lambda b, c: (0, 0))],
        out_specs=[pl.BlockSpec((L, M_WIDTH), lambda b, c: (b * nc + c, 0)),
                   pl.BlockSpec((1, M_HEADS, 2 * M_DV, M_DK), lambda b, c: (b, 0, 0, 0)),
                   pl.BlockSpec((1, 8, LANES), lambda b, c: (b, 0, 0))],
        scratch_shapes=[pltpu.VMEM((M_HEADS, M_DK, 2 * M_DV), F32), pltpu.VMEM((8, LANES), F32)],
        compiler_params=_cparams("arbitrary", "arbitrary"), name=name,
    )(qm, vm, kt, gc, gr, c0, m0)


def _outer_f32(a, b):
    ah, am, al = (t.astype(F32) for t in _split3(a))
    bh, bm, bl = (t.astype(F32) for t in _split3(b))
    z = jnp.zeros_like(ah)
    lhs = jnp.concatenate([ah, ah, ah, am, am, al, z, z], axis=0).astype(BF16)
    rhs = jnp.concatenate([bh, bm, bl, bh, bm, bh, z, z], axis=0).astype(BF16)
    return lax.dot_general(lhs, rhs, (((0,), (0,)), ((), ())), preferred_element_type=F32)


def _mlstm_step_kernel(c_ref, n_ref, m_ref, gc_ref, q_ref, k_ref, v_ref,
                       c_out_ref, n_out_ref, m_out_ref, h_ref):
    g = c_ref.shape[0]
    scale = M_DK ** -0.5
    lane_m = lax.broadcasted_iota(I32, (1, LANES), 1)
    for j in range(g):
        m_row = jnp.zeros((1, LANES), F32)
        for h in range(M_HEADS):
            sl = slice(h * M_DK, (h + 1) * M_DK)
            q = q_ref[j:j + 1, sl]
            k = k_ref[j:j + 1, sl] * scale
            v = v_ref[j:j + 1, sl]
            ig = gc_ref[j:j + 1, h:h + 1]
            lf = gc_ref[j:j + 1, M_HEADS + h:M_HEADS + h + 1]
            m = m_ref[j:j + 1, h:h + 1]
            c = c_ref[j, h]
            n = n_ref[j, h:h + 1, :]
            m_t = jnp.maximum(lf + m, ig)
            w = jnp.exp(lf + m - m_t)
            wg = jnp.exp(ig - m_t)
            s = jnp.sum(q * k, axis=1, keepdims=True) * wg
            q8 = jnp.broadcast_to(q, (8, M_DK)).astype(BF16)
            cq = lax.dot_general(q8, c.astype(BF16), (((1,), (1,)), ((), ())), preferred_element_type=F32)[0:1, :]
            den = w * jnp.sum(n * q, axis=1, keepdims=True) + s
            h_ref[j:j + 1, sl] = (w * cq + s * v) / jnp.maximum(jnp.abs(den), jnp.exp(-m_t))
            c_out_ref[j, h] = w * c + _outer_f32(wg * v, k)
            n_out_ref[j, h:h + 1, :] = w * n + wg * k
            m_row = jnp.where(lane_m == h, m_t, m_row)
        m_out_ref[j:j + 1, :] = m_row


def _mlstm_step(c, n, m_pad, gc, q, k, v):
    nb = c.shape[0]
    g = SAMPLE_GROUP
    row = lambda w: pl.BlockSpec((g, w), lambda i: (i, 0))
    return pl.pallas_call(
        _mlstm_step_kernel,
        out_shape=[jax.ShapeDtypeStruct(c.shape, F32), jax.ShapeDtypeStruct(n.shape, F32),
                   jax.ShapeDtypeStruct((nb, LANES), F32), jax.ShapeDtypeStruct((nb, M_WIDTH), F32)],
        grid=(nb // g,),
        in_specs=[pl.BlockSpec((g, M_HEADS, M_DV, M_DK), lambda i: (i, 0, 0, 0)),
                  pl.BlockSpec((g, M_HEADS, M_DK), lambda i: (i, 0, 0)),
                  row(LANES), row(LANES), row(M_WIDTH), row(M_WIDTH), row(M_WIDTH)],
        out_specs=[pl.BlockSpec((g, M_HEADS, M_DV, M_DK), lambda i: (i, 0, 0, 0)),
                   pl.BlockSpec((g, M_HEADS, M_DK), lambda i: (i, 0, 0)),
                   row(LANES), row(M_WIDTH)],
        compiler_params=_cparams("arbitrary"), name="mlstm_step",
    )(c, n, m_pad, gc, q, k, v)


def _swa_kernel(q_ref, kc_ref, kp_ref, vc_ref, vp_ref, km_ref, vm_ref, bias_ref, sink_ref, o_ref):
    j = pl.program_id(1)
    first = j == 0
    blk = WINDOW
    nqb = q_ref.shape[0] // blk
    kp = jnp.where(first, km_ref[...], kp_ref[...])
    vp = jnp.where(first, vm_ref[...], vp_ref[...])
    k = jnp.concatenate([kp, kc_ref[...]], axis=0)
    v = jnp.concatenate([vp, vc_ref[...]], axis=0)
    lane = lax.broadcasted_iota(I32, k.shape, 1)
    zero = jnp.zeros_like(k)
    k_half = (jnp.where(lane < A_HD, k, zero), jnp.where(lane >= A_HD, k, zero))
    v_half = (jnp.where(lane < A_HD, v, zero), jnp.where(lane >= A_HD, v, zero))
    lane_q = lax.broadcasted_iota(I32, (blk, LANES), 1)
    for u in range(nqb):
        rows = slice(u * blk, (u + 1) * blk)
        keys = slice(u * blk, (u + 2) * blk)
        table = jnp.where(first, 0, 1) if u == 0 else 1
        v_stack = jnp.concatenate([v_half[0][keys], v_half[1][keys]], axis=0)
        for p in range(A_GROUP):
            qs = q_ref[rows, p * LANES:(p + 1) * LANES]
            probs, inv = [], []
            for half in range(2):
                hd = HEAD_ORDER[2 * p + half]
                s = lax.dot_general(qs, k_half[half][keys], (((1,), (1,)), ((), ())), preferred_element_type=F32)
                s = s * (A_HD ** -0.5) + bias_ref[table, hd]
                sk = sink_ref[hd:hd + 1, 0:1]
                m = jnp.maximum(jnp.max(s, axis=1, keepdims=True), sk)
                e = jnp.exp(s - m)
                probs.append(e.astype(BF16))
                inv.append(1.0 / (jnp.sum(e, axis=1, keepdims=True) + jnp.exp(sk - m)))
            o = jnp.dot(jnp.concatenate(probs, axis=1), v_stack, preferred_element_type=F32)
            o_ref[rows, p * LANES:(p + 1) * LANES] = o * jnp.where(lane_q < A_HD, inv[0], inv[1])


def _swa(qa, ka, va, kmeta, vmeta, bias, sinks, batch):
    blk = WINDOW
    nqb = SWA_QBLOCKS
    nq = qa.shape[0] // (batch * blk * nqb)
    kv_cur = pl.BlockSpec((nqb * blk, LANES), lambda b, j: (b * nq + j, 0))
    kv_prev = pl.BlockSpec((blk, LANES), lambda b, j: ((b * nq + j) * nqb + jnp.where(j == 0, 0, -1), 0))
    const2 = lambda shape: pl.BlockSpec(shape, lambda b, j: (0, 0))
    return pl.pallas_call(
        _swa_kernel, out_shape=jax.ShapeDtypeStruct((qa.shape[0], A_WIDTH), F32), grid=(batch, nq),
        in_specs=[pl.BlockSpec((nqb * blk, A_WIDTH), lambda b, j: (b * nq + j, 0)),
                  kv_cur, kv_prev, kv_cur, kv_prev, const2((blk, LANES)), const2((blk, LANES)),
                  pl.BlockSpec(bias.shape, lambda b, j: (0, 0, 0, 0)),
                  const2((8, LANES))],
        out_specs=pl.BlockSpec((nqb * blk, A_WIDTH), lambda b, j: (b * nq + j, 0)),
        compiler_params=_cparams("arbitrary", "arbitrary"), name="swa_prompt",
    )(qa, ka, ka, va, va, kmeta, vmeta, bias, sinks)


def _swa_step_kernel(ck_ref, cv_ref, q_ref, k_ref, v_ref, bias_ref, sink_ref, ko_ref, vo_ref, o_ref):
    g = ck_ref.shape[0]
    lane = lax.broadcasted_iota(I32, (A_HEADS, LANES), 1)
    row = lax.broadcasted_iota(I32, (A_HEADS, LANES), 0)
    own_half = (row % 2 == 0) == (lane < A_HD)
    bias = bias_ref[...]
    sk = sink_ref[:, 0:1]
    for j in range(g):
        ko_ref[j, 0:WINDOW - 1, :] = ck_ref[j, 1:WINDOW, :]
        ko_ref[j, WINDOW - 1:WINDOW, :] = k_ref[j:j + 1, :]
        vo_ref[j, 0:WINDOW - 1, :] = cv_ref[j, 1:WINDOW, :]
        vo_ref[j, WINDOW - 1:WINDOW, :] = v_ref[j:j + 1, :]
        kk = ko_ref[j].astype(BF16)
        vv = vo_ref[j].astype(BF16)
        slabs = [q_ref[j:j + 1, p * LANES:(p + 1) * LANES] for p in range(A_GROUP)]
        q8 = jnp.concatenate([slabs[r // 2] for r in range(A_HEADS)], axis=0)
        q8 = jnp.where(own_half, q8, 0.0).astype(BF16)
        s = lax.dot_general(q8, kk, (((1,), (1,)), ((), ())), preferred_element_type=F32)
        s = s * (A_HD ** -0.5) + bias
        m = jnp.maximum(jnp.max(s, axis=1, keepdims=True), sk)
        e = jnp.exp(s - m)
        inv = 1.0 / (jnp.sum(e, axis=1, keepdims=True) + jnp.exp(sk - m))
        o8 = jnp.where(own_half, jnp.dot(e.astype(BF16), vv, preferred_element_type=F32) * inv, 0.0)
        for p in range(A_GROUP):
            o_ref[j:j + 1, p * LANES:(p + 1) * LANES] = o8[2 * p:2 * p + 1, :] + o8[2 * p + 1:2 * p + 2, :]


def _swa_step(ck, cv, q, k, v, bias_rows, sinks):
    nb = ck.shape[0]
    g = SAMPLE_GROUP
    cache = pl.BlockSpec((g, WINDOW, LANES), lambda i: (i, 0, 0))
    row = lambda w: pl.BlockSpec((g, w), lambda i: (i, 0))
    const = lambda a: pl.BlockSpec(a.shape, lambda i: (0, 0))
    return pl.pallas_call(
        _swa_step_kernel,
        out_shape=[jax.ShapeDtypeStruct(ck.shape, F32), jax.ShapeDtypeStruct(cv.shape, F32),
                   jax.ShapeDtypeStruct((nb, A_WIDTH), F32)],
        grid=(nb // g,),
        in_specs=[cache, cache, row(A_WIDTH), row(LANES), row(LANES), const(bias_rows), const(sinks)],
        out_specs=[cache, cache, row(A_WIDTH)],
        compiler_params=_cparams("arbitrary"), name="swa_step",
    )(ck, cv, q, k, v, bias_rows, sinks)


def _layer_norm(z, g, b):
    mu = jnp.mean(z, axis=1, keepdims=True)
    zc = z - mu
    var = jnp.mean(zc * zc, axis=1, keepdims=True)
    return zc * lax.rsqrt(var + LN_EPS) * g + b


def _pack_halves(x):
    w = x.shape[1] // 2
    lo = pltpu.bitcast(x[:, :w].astype(BF16).astype(F32), U32)
    hi = pltpu.bitcast(x[:, w:].astype(BF16).astype(F32), U32)
    return (lo >> 16) | (hi & jnp.uint32(0xFFFF0000))


def _unpack_halves(words):
    lo = pltpu.bitcast(words << 16, F32).astype(BF16)
    hi = pltpu.bitcast(words & jnp.uint32(0xFFFF0000), F32).astype(BF16)
    return lo, hi


def _to_token_tiles(ref, x):
    for q in range(x.shape[1] // LANES):
        ref[:, q, :] = x[:, q * LANES:(q + 1) * LANES]


def _merge_kernel(h_ref, om_ref, att_ref, x_ref, gm_ref, ga_ref, wo_ref, g1_ref, b1_ref, wr_ref, br_ref,
                  x1_ref, xp_ref, tk_ref, cnt_ref):
    @pl.when(pl.program_id(0) == 0)
    def _():
        cnt_ref[...] = jnp.zeros_like(cnt_ref)

    hm = h_ref[...] * _sigmoid(om_ref[...])
    ym = hm * lax.rsqrt(jnp.mean(hm * hm, axis=1, keepdims=True) + LN_EPS) * gm_ref[...]
    att = att_ref[...]
    ya = att * lax.rsqrt(jnp.mean(att * att, axis=1, keepdims=True) + LN_EPS) * ga_ref[...]
    mix = (jnp.dot(ym.astype(BF16), wo_ref[0:M_WIDTH, :], preferred_element_type=F32)
           + jnp.dot(ya.astype(BF16), wo_ref[M_WIDTH:, :], preferred_element_type=F32))
    x1 = _layer_norm(DN_ALPHA * x_ref[...] + mix, g1_ref[...], b1_ref[...])
    x1_ref[...] = x1
    _to_token_tiles(xp_ref, _pack_halves(x1))
    logits = jnp.dot(x1.astype(BF16), wr_ref[...], preferred_element_type=F32) + br_ref[...]
    lane = lax.broadcasted_iota(I32, logits.shape, 1).astype(F32)
    vals, idxs = [], []
    for _ in range(TOP_K):
        mx = jnp.max(logits, axis=1, keepdims=True)
        idx = jnp.min(jnp.where(logits == mx, lane, float(LANES)), axis=1, keepdims=True)
        vals.append(mx)
        idxs.append(idx)
        logits = jnp.where(lane == idx, 2.0 * NEG, logits)
    es = [jnp.exp(vk - vals[0]) for vk in vals]
    tot = es[0] + es[1] + es[2] + es[3]
    tk = jnp.zeros(logits.shape, F32)
    picked = jnp.zeros(logits.shape, F32)
    for k in range(TOP_K):
        tk = jnp.where(lane == float(k), es[k] / tot, tk)
        tk = jnp.where(lane == float(TOP_K + k), idxs[k], tk)
        picked = jnp.where(lane == idxs[k], 1.0, picked)
    tk_ref[...] = tk
    cnt_ref[...] = cnt_ref[...] + jnp.sum(picked, axis=0, keepdims=True)


def _merge(h, om, att, x, gm, ga, wo, g1, b1, wr, br, tile, name):
    t = x.shape[0]
    rows = lambda w: pl.BlockSpec((tile, w), lambda i: (i, 0))
    const = lambda a: pl.BlockSpec(a.shape, lambda i: (0, 0))
    return pl.pallas_call(
        _merge_kernel,
        out_shape=[jax.ShapeDtypeStruct((t, D_MODEL), F32), jax.ShapeDtypeStruct((t, XP_TILE, LANES), U32),
                   jax.ShapeDtypeStruct((t, LANES), F32), jax.ShapeDtypeStruct((8, LANES), F32)],
        grid=(t // tile,),
        in_specs=[rows(M_WIDTH), rows(M_WIDTH), rows(A_WIDTH), rows(D_MODEL), const(gm), const(ga), const(wo),
                  const(g1), const(b1), const(wr), const(br)],
        out_specs=[rows(D_MODEL), pl.BlockSpec((tile, XP_TILE, LANES), lambda i: (i, 0, 0)), rows(LANES),
                   pl.BlockSpec((8, LANES), lambda i: (0, 0))],
        compiler_params=_cparams("arbitrary"), name=name,
    )(h, om, att, x, gm, ga, wo, g1, b1, wr, br)


def _route_kernel(tk_ref, first_ref, strict_ref, dest_ref, next_scr):
    @pl.when(pl.program_id(0) == 0)
    def _():
        next_scr[...] = first_ref[...]

    tk = tk_ref[...]
    lane = lax.broadcasted_iota(I32, tk.shape, 1).astype(F32)
    onehots = [jnp.where(lane == tk[:, TOP_K + k:TOP_K + k + 1], 1.0, 0.0) for k in range(TOP_K)]
    tot = onehots[0] + onehots[1] + onehots[2] + onehots[3]
    row = jnp.dot(strict_ref[...], tot.astype(BF16), preferred_element_type=F32) + next_scr[0:1, :]
    out = jnp.zeros(tk.shape, F32)
    for k in range(TOP_K):
        out = jnp.where(lane == float(k), jnp.sum(onehots[k] * row, axis=1, keepdims=True), out)
    dest_ref[...] = out.astype(I32)
    next_scr[...] = next_scr[...] + jnp.sum(tot, axis=0, keepdims=True)


def _route(tk, first):
    t = tk.shape[0]
    tile = min(RANK_TILE, t)
    strict = jnp.asarray(np.tril(np.ones((tile, tile), np.float32), -1), BF16)
    return pl.pallas_call(
        _route_kernel, out_shape=jax.ShapeDtypeStruct((t, LANES), I32), grid=(t // tile,),
        in_specs=[pl.BlockSpec((tile, LANES), lambda i: (i, 0)), pl.BlockSpec((8, LANES), lambda i: (0, 0)),
                  pl.BlockSpec((tile, tile), lambda i: (0, 0))],
        out_specs=pl.BlockSpec((tile, LANES), lambda i: (i, 0)),
        scratch_shapes=[pltpu.VMEM((8, LANES), F32)],
        compiler_params=_cparams("arbitrary"), name="moe_route",
    )(tk, first, strict)


def _offsets_kernel(cnt_ref, off_ref, be_ref, nu_ref, *, tile):
    cnt = cnt_ref[...]
    nblk = jnp.floor((cnt + float(tile - 1)) * (1.0 / tile))
    r_i = lax.broadcasted_iota(I32, (LANES, LANES), 0)
    c_i = lax.broadcasted_iota(I32, (LANES, LANES), 1)
    incl = jnp.where(r_i <= c_i, 1.0, 0.0).astype(BF16)
    cum = jnp.dot(nblk.astype(BF16), incl, preferred_element_type=F32)
    off_ref[...] = (cum - nblk) * float(tile)
    rows = be_ref.shape[0]
    jb = (lax.broadcasted_iota(I32, (rows, LANES), 0) * LANES + lax.broadcasted_iota(I32, (rows, LANES), 1)).astype(F32)
    acc = jnp.zeros((rows, LANES), F32)
    for e in range(N_EXPERTS):
        acc = acc + jnp.where(jb >= cum[0:1, e:e + 1], 1.0, 0.0)
    be_ref[...] = jnp.minimum(acc, float(N_EXPERTS - 1)).astype(I32)
    nu_ref[...] = jnp.broadcast_to(cum[0:1, N_EXPERTS - 1:N_EXPERTS], nu_ref.shape).astype(I32)


def _offsets(cnt, n_blocks, tile):
    rows = -(-n_blocks // LANES)
    rows = -(-rows // 8) * 8
    return pl.pallas_call(
        functools.partial(_offsets_kernel, tile=tile),
        out_shape=[jax.ShapeDtypeStruct((8, LANES), F32), jax.ShapeDtypeStruct((rows, LANES), I32),
                   jax.ShapeDtypeStruct((8, LANES), I32)],
        name="moe_offsets",
    )(cnt)


def _dispatch_kernel(dest_ref, xp_ref, xs_in_ref, xs_ref, sem):
    del xs_in_ref
    t = xp_ref.shape[0]

    def row_copy(tok, dst):
        return pltpu.make_async_copy(xp_ref.at[pl.ds(tok, 1)], xs_ref.at[pl.ds(dst, 1)], sem)

    def issue(grp, carry):
        base = pl.multiple_of(grp * ISSUE_GROUP, ISSUE_GROUP)
        for u in range(ISSUE_GROUP):
            for k in range(TOP_K):
                row_copy(base + u, dest_ref[(base + u) * TOP_K + k]).start(priority=k % 2)
        return carry

    lax.fori_loop(0, t // ISSUE_GROUP, issue, 0)
    for k in range(TOP_K):
        pltpu.make_async_copy(xp_ref, xs_ref.at[pl.ds(0, t)], sem).wait()


def _dispatch(dest_flat, xp, xs):
    t = xp.shape[0]
    tile = min(ROW_TILE, t)
    return pl.pallas_call(
        _dispatch_kernel, out_shape=jax.ShapeDtypeStruct(xs.shape, xs.dtype), grid=(t // tile,),
        in_specs=[pl.BlockSpec((tile * TOP_K,), lambda i: (i,), memory_space=pltpu.SMEM),
                  pl.BlockSpec((tile,) + xp.shape[1:], lambda i: (i, 0, 0)),
                  pl.BlockSpec(memory_space=pl.ANY)],
        out_specs=pl.BlockSpec(memory_space=pl.ANY),
        scratch_shapes=[pltpu.SemaphoreType.DMA(())],
        input_output_aliases={2: 0},
        compiler_params=_cparams("arbitrary"), name="moe_dispatch",
    )(dest_flat, xp, xs)


def _expert_kernel(be_ref, nu_ref, xs_ref, w1_ref, b1g_ref, b1l_ref, w2_ref, b2_ref, perm_ref, ys_ref,
                   w1g_scr, w1l_scr, w2_scr, xq_scr, y_scr, sem, osem):
    j = pl.program_id(0)
    active = j < nu_ref[0]
    changed = jnp.logical_or(j == 0, be_ref[j] != be_ref[jnp.maximum(j - 1, 0)])
    half = D_MODEL // 2
    tm = y_scr.shape[0]
    slot = lax.rem(j, 2)

    def fetch(blk, slot):
        row0 = pl.multiple_of(blk * tm, tm)
        return [pltpu.make_async_copy(xs_ref.at[pl.ds(row0, tm), q, :], xq_scr.at[slot, q], sem.at[slot])
                for q in range(XP_TILE)]

    def put(blk):
        row0 = pl.multiple_of(blk * tm, tm)
        return [pltpu.make_async_copy(y_scr.at[:, q * LANES:(q + 1) * LANES], ys_ref.at[pl.ds(row0, tm), q, :], osem)
                for q in range(YS_TILE)]

    def emit(y):
        @pl.when(j > 0)
        def _():
            for cp in put(j - 1):
                cp.wait()

        y_scr[...] = _pack_halves(y)
        for cp in put(j):
            cp.start()

    @pl.when(j == 0)
    def _():
        for cp in fetch(0, 0):
            cp.start()

    @pl.when(j + 1 < nu_ref[0])
    def _():
        for cp in fetch(j + 1, 1 - slot):
            cp.start()

    @pl.when(jnp.logical_and(active, changed))
    def _():
        for c in range(2 * D_FF // 256):
            wc = w1_ref[0, :, c * 256:(c + 1) * 256].astype(BF16)
            d = jnp.dot(wc, perm_ref[...], preferred_element_type=F32).astype(BF16)
            w1g_scr[:, c * 128:(c + 1) * 128] = d[:, :128]
            w1l_scr[:, c * 128:(c + 1) * 128] = d[:, 128:]
        for c in range(D_FF // 256):
            w2_scr[c * 256:(c + 1) * 256, :] = w2_ref[0, c * 256:(c + 1) * 256, :].astype(BF16)

    @pl.when(active)
    def _():
        for cp in fetch(j, slot):
            cp.wait()
        lo, hi = _unpack_halves(jnp.concatenate([xq_scr[slot, q] for q in range(XP_TILE)], axis=1))
        y = jnp.zeros((tm, D_MODEL), F32)
        nchunk = 512
        for c in range(D_FF // nchunk):
            cs = slice(c * nchunk, (c + 1) * nchunk)
            hg = (jnp.dot(lo, w1g_scr[0:half, cs], preferred_element_type=F32)
                  + jnp.dot(hi, w1g_scr[half:, cs], preferred_element_type=F32) + b1g_ref[0, :, cs])
            hl = (jnp.dot(lo, w1l_scr[0:half, cs], preferred_element_type=F32)
                  + jnp.dot(hi, w1l_scr[half:, cs], preferred_element_type=F32) + b1l_ref[0, :, cs])
            x_glu = jnp.minimum(hg, SWIGLU_LIMIT)
            x_lin = jnp.clip(hl, -SWIGLU_LIMIT, SWIGLU_LIMIT)
            a = x_glu * _sigmoid(SWIGLU_ALPHA * x_glu) * (x_lin + 1.0)
            y = y + jnp.dot(a.astype(BF16), w2_scr[cs, :], preferred_element_type=F32)
        emit(y + b2_ref[0])

    @pl.when(jnp.logical_not(active))
    def _():
        emit(jnp.zeros((tm, D_MODEL), F32))

    @pl.when(j == pl.num_programs(0) - 1)
    def _():
        for cp in put(j):
            cp.wait()


def _experts(be, nu, xs, w1, b1g, b1l, w2, b2, perm, tile):
    n_blocks = xs.shape[0] // tile
    grid_spec = pltpu.PrefetchScalarGridSpec(
        num_scalar_prefetch=2, grid=(n_blocks,),
        in_specs=[pl.BlockSpec(memory_space=pl.ANY),
                  pl.BlockSpec((1, D_MODEL, 2 * D_FF), lambda j, be, nu: (be[j], 0, 0)),
                  pl.BlockSpec((1, 1, D_FF), lambda j, be, nu: (be[j], 0, 0)),
                  pl.BlockSpec((1, 1, D_FF), lambda j, be, nu: (be[j], 0, 0)),
                  pl.BlockSpec((1, D_FF, D_MODEL), lambda j, be, nu: (be[j], 0, 0)),
                  pl.BlockSpec((1, 1, D_MODEL), lambda j, be, nu: (be[j], 0, 0)),
                  pl.BlockSpec((256, 256), lambda j, be, nu: (0, 0))],
        out_specs=pl.BlockSpec(memory_space=pl.ANY),
        scratch_shapes=[pltpu.VMEM((D_MODEL, D_FF), BF16), pltpu.VMEM((D_MODEL, D_FF), BF16),
                        pltpu.VMEM((D_FF, D_MODEL), BF16), pltpu.VMEM((2, XP_TILE, tile, LANES), U32),
                        pltpu.VMEM((tile, YS_TILE * LANES), U32), pltpu.SemaphoreType.DMA((2,)),
                        pltpu.SemaphoreType.DMA(())])
    return pl.pallas_call(
        _expert_kernel, out_shape=jax.ShapeDtypeStruct((xs.shape[0], YS_TILE, LANES), U32), grid_spec=grid_spec,
        compiler_params=_cparams("arbitrary"), name="moe_experts",
    )(be, nu, xs, w1, b1g, b1l, w2, b2, perm)


def _combine_kernel(dest_ref, ys_ref, tk_ref, x1_ref, g2_ref, b2_ref, out_ref, buf, sem):
    t = x1_ref.shape[0]

    def issue(grp, carry):
        base = pl.multiple_of(grp * 8, 8)
        for u in range(8):
            for k in range(TOP_K):
                pltpu.make_async_copy(ys_ref.at[dest_ref[(base + u) * TOP_K + k]],
                                      buf.at[k, grp, :, u, :], sem).start(priority=k % 2)
        return carry

    lax.fori_loop(0, t // 8, issue, 0)
    for k in range(TOP_K):
        for u in range(8):
            pltpu.make_async_copy(ys_ref.at[pl.ds(0, t // 8)], buf.at[k, :, :, u, :], sem).wait()
    tk = tk_ref[...]
    los, his = [], []
    for q in range(YS_TILE):
        lo = hi = None
        for k in range(TOP_K):
            words = buf[k, :, q].reshape(t, LANES)
            g = tk[:, k:k + 1]
            lo_k = g * pltpu.bitcast(words << 16, F32)
            hi_k = g * pltpu.bitcast(words & jnp.uint32(0xFFFF0000), F32)
            lo = lo_k if lo is None else lo + lo_k
            hi = hi_k if hi is None else hi + hi_k
        los.append(lo)
        his.append(hi)
    ff = jnp.concatenate(los + his, axis=1)
    out_ref[...] = _layer_norm(DN_ALPHA * x1_ref[...] + ff, g2_ref[...], b2_ref[...])


def _combine(dest_flat, ys, tk, x1, g2, b2):
    t = x1.shape[0]
    tile = min(ROW_TILE, t)
    return pl.pallas_call(
        _combine_kernel, out_shape=jax.ShapeDtypeStruct((t, D_MODEL), F32), grid=(t // tile,),
        in_specs=[pl.BlockSpec((tile * TOP_K,), lambda i: (i,), memory_space=pltpu.SMEM),
                  pl.BlockSpec(memory_space=pl.ANY),
                  pl.BlockSpec((tile, LANES), lambda i: (i, 0)),
                  pl.BlockSpec((tile, D_MODEL), lambda i: (i, 0)),
                  pl.BlockSpec((1, D_MODEL), lambda i: (0, 0)),
                  pl.BlockSpec((1, D_MODEL), lambda i: (0, 0))],
        out_specs=pl.BlockSpec((tile, D_MODEL), lambda i: (i, 0)),
        scratch_shapes=[pltpu.VMEM((TOP_K, tile // 8, YS_TILE, 8, LANES), U32), pltpu.SemaphoreType.DMA(())],
        compiler_params=_cparams("arbitrary"), name="moe_combine",
    )(dest_flat, ys, tk, x1, g2, b2)


def _rel_bucket(dist):
    exact = REL_BUCKETS // 2
    d = np.maximum(dist, 0)
    log_b = exact + (np.log(np.maximum(d, 1).astype(np.float32) / np.float32(exact))
                     / np.float32(math.log(REL_MAX_DIST / exact)) * np.float32(REL_BUCKETS - exact)).astype(np.int32)
    return np.where(d < exact, d, np.minimum(log_b, REL_BUCKETS - 1)).astype(np.int32)


def _bias_lookup(table, bucket, valid):
    bucket = jnp.asarray(bucket)[None]
    acc = jnp.zeros((table.shape[1],) + bucket.shape[1:], F32)
    for b in range(REL_BUCKETS):
        acc = jnp.where(bucket == b, table[b].reshape((-1,) + (1,) * (bucket.ndim - 1)), acc)
    return jnp.where(jnp.asarray(valid)[None], acc, NEG)


def _bias_tables(rel_bias):
    table = rel_bias.astype(F32)
    r = np.arange(WINDOW)[:, None]
    c = np.arange(2 * WINDOW)[None, :]
    dist = r + WINDOW - c
    valid = (dist >= 0) & (dist < WINDOW)
    dist0 = np.where(c < N_META, N_META + r - c, dist)
    valid0 = np.where(c < N_META, dist0 < WINDOW, (c >= WINDOW) & valid)
    both = jnp.stack([_bias_lookup(table, _rel_bucket(dist0), valid0), _bias_lookup(table, _rel_bucket(dist), valid)])
    dist_s = WINDOW - 1 - np.arange(WINDOW)
    rows = _bias_lookup(table[:, np.asarray(HEAD_ORDER)], _rel_bucket(dist_s), np.ones_like(dist_s, bool))
    return both, rows


def _perm_heads(a, axis):
    parts = [lax.slice_in_dim(a, h * A_HD, (h + 1) * A_HD, axis=axis) for h in HEAD_ORDER]
    return jnp.concatenate(parts, axis=axis)


def _rep_rows(vec, rows=8):
    out = jnp.zeros((rows, LANES), F32)
    return out.at[:vec.shape[0], :].set(jnp.broadcast_to(vec.astype(F32)[:, None], (vec.shape[0], LANES)))


def kernel(x_prompt, x_sample, cache_swa_k, cache_swa_v, state_mlstm_C, state_mlstm_n, state_mlstm_m, meta_tokens, rel_bias, w_in, b_igate, b_fgate, attn_sinks, g_mlstm_out, g_attn_out, w_out, ln1_g, ln1_b, w_router, b_router, w_moe1, b_moe1, w_moe2, b_moe2, ln2_g, ln2_b):
    B, S, _ = x_prompt.shape
    NB = x_sample.shape[0]
    assert x_sample.shape[1] == 1 and w_in.shape[0] == 1
    assert S % PROJ_TILE == 0 and S % M_CHUNK == 0 and S % WINDOW == 0 and NB % SAMPLE_GROUP == 0
    l = 0

    pts = np.cumsum(IN_WIDTHS)[:-1].tolist()
    w_qm, w_km, w_vm, w_om, w_ig, w_fg, w_qa, w_ka, w_va = jnp.split(w_in[l], pts, axis=1)
    w_gate = jnp.pad(jnp.concatenate([w_ig, w_fg], axis=1), ((0, 0), (0, LANES - 2 * M_HEADS)))
    w_qa = _perm_heads(w_qa, 1)
    b_gate = jnp.concatenate([b_igate[l], b_fgate[l]]).astype(F32)
    brow = jnp.pad(b_gate, (0, LANES - 2 * M_HEADS))[None, :]
    bcol = b_gate[:, None]
    bf = lambda a: a.astype(BF16)
    wr_p = bf(jnp.concatenate([w_qm, w_vm, w_om, w_qa, w_ka, w_va, w_gate], axis=1))
    wt_p = bf(jnp.concatenate([w_km.T, w_ig.T, w_fg.T], axis=0))
    plan_p = ((0, 512, "plain", BF16), (512, 512, "plain", BF16), (1024, 512, "plain", F32),
              (1536, 512, "plain", BF16), (2048, 128, "plain", BF16), (2176, 128, "plain", BF16),
              (2304, 128, "gate", F32))
    tplan_p = ((0, 512, "plain", BF16), (512, 8, "gate", F32))
    wr_s = bf(jnp.concatenate([w_qm, w_km, w_vm, w_om, w_qa, w_ka, w_va, w_gate], axis=1))
    plan_s = ((0, 512, "plain", F32), (512, 512, "plain", F32), (1024, 512, "plain", F32), (1536, 512, "plain", F32),
              (2048, 512, "plain", F32), (2560, 128, "plain", F32), (2688, 128, "plain", F32), (2816, 128, "gate", F32))

    bias_tab, bias_rows = _bias_tables(rel_bias)
    sinks = _rep_rows(attn_sinks[l])
    sinks_step = _rep_rows(attn_sinks[l][np.asarray(HEAD_ORDER)])
    g_m = g_mlstm_out[l].astype(F32)[None, :]
    g_a = _perm_heads(g_attn_out[l].astype(F32), 0)[None, :]
    wo = bf(jnp.concatenate([w_out[l][:M_WIDTH], _perm_heads(w_out[l][M_WIDTH:], 0)], axis=0))
    g1, b1 = ln1_g[l].astype(F32)[None, :], ln1_b[l].astype(F32)[None, :]
    g2, b2 = ln2_g[l].astype(F32)[None, :], ln2_b[l].astype(F32)[None, :]
    w_r = bf(jnp.pad(w_router[l], ((0, 0), (0, LANES - N_EXPERTS))))
    b_r = jnp.pad(b_router[l].astype(F32), (0, LANES - N_EXPERTS), constant_values=NEG)[None, :]
    b1g = b_moe1[l][:, 0::2].astype(F32)[:, None, :]
    b1l = b_moe1[l][:, 1::2].astype(F32)[:, None, :]
    b2e = b_moe2[l].astype(F32)[:, None, :]
    pj = np.zeros((256, 256), np.float32)
    pj[2 * np.arange(128), np.arange(128)] = 1.0
    pj[2 * np.arange(128) + 1, 128 + np.arange(128)] = 1.0
    perm = jnp.asarray(pj, BF16)

    xp2 = x_prompt.reshape(B * S, D_MODEL)
    qm, vm, om, qa, ka, va, gc, kt, gr, kv_tail = _proj(
        xp2, wr_p, wt_p, brow, bcol, plan_p, tplan_p, (2048, 256), PROJ_TILE, S, "proj_prompt")
    x_meta = jnp.pad(meta_tokens.astype(F32), ((0, M_CHUNK - N_META), (0, 0)))
    qm0, vm0, _, _, ka0, va0, gc0, kt0, gr0 = _proj(
        x_meta, wr_p, wt_p, brow, bcol, plan_p, tplan_p, None, M_CHUNK, M_CHUNK, "proj_meta")
    xs2 = x_sample.reshape(NB, D_MODEL)
    qm_s, km_s, vm_s, om_s, qa_s, ka_s, va_s, gc_s = _proj(
        xs2, wr_s, wt_p, brow, bcol, plan_s, (), None, NB, NB, "proj_sample")

    zero_c = jnp.zeros((M_HEADS, M_DK, 2 * M_DV), F32)
    zero_m = jnp.zeros((8, LANES), F32)
    _, ct_meta, m_meta = _mlstm(qm0, vm0, kt0, gc0, gr0, zero_c, zero_m, 1, N_META, "mlstm_meta")
    c0 = jnp.swapaxes(ct_meta[0], 1, 2)
    h_p, ct_p, m_p = _mlstm(qm, vm, kt, gc, gr, c0, m_meta[0], B, M_CHUNK, "mlstm_prompt")
    C_p = ct_p[:, :, :M_DV, :]
    n_p = ct_p[:, :, M_DV, :]
    m_prompt = m_p[:, :M_HEADS, 0]
    m_pad = jnp.pad(state_mlstm_m[l].astype(F32), ((0, 0), (0, LANES - M_HEADS)))
    C_s, n_s, m_s, h_s = _mlstm_step(state_mlstm_C[l].astype(F32), state_mlstm_n[l].astype(F32), m_pad,
                                     gc_s, qm_s, km_s, vm_s)

    att_p = _swa(qa, ka, va, ka0, va0, bias_tab, sinks, B)
    ck = cache_swa_k[l].reshape(NB, WINDOW, LANES)
    cv = cache_swa_v[l].reshape(NB, WINDOW, LANES)
    k_new, v_new, att_s = _swa_step(ck, cv, qa_s, ka_s, va_s, bias_rows, sinks_step)

    x1_p, xpk_p, tk_p, cnt_p = _merge(h_p, om, att_p, xp2, g_m, g_a, wo, g1, b1, w_r, b_r, MERGE_TILE, "merge_prompt")
    x1_s, xpk_s, tk_s, cnt_s = _merge(h_s, om_s, att_s, xs2, g_m, g_a, wo, g1, b1, w_r, b_r, NB, "merge_sample")

    T_p = B * S
    assert T_p % RANK_TILE == 0 and T_p % ROW_TILE == 0
    n_blocks = -(-((T_p + NB) * TOP_K) // EXPERT_TILE) + N_EXPERTS
    off, be2, nu2 = _offsets(cnt_p + cnt_s, n_blocks, EXPERT_TILE)
    dest_p = _route(tk_p, off)[:, :TOP_K].reshape(-1)
    dest_s = _route(tk_s, off + cnt_p)[:, :TOP_K].reshape(-1)
    be = be2.reshape(-1)[:n_blocks]
    nu = nu2[0, :1]
    xs = jnp.zeros((n_blocks * EXPERT_TILE, XP_TILE, LANES), U32)
    xs = _dispatch(dest_p, xpk_p, xs)
    xs = _dispatch(dest_s, xpk_s, xs)
    ys = _experts(be, nu, xs, w_moe1[l], b1g, b1l, w_moe2[l], b2e, perm, EXPERT_TILE)
    y_p = _combine(dest_p, ys, tk_p, x1_p, g2, b2)
    y_s = _combine(dest_s, ys, tk_s, x1_s, g2, b2)

    kv_tail = kv_tail.reshape(B, WINDOW, 2, A_KV_HEADS, A_HD)
    dt_k, dt_v = cache_swa_k.dtype, cache_swa_v.dtype
    return (y_p.reshape(B, S, D_MODEL).astype(x_prompt.dtype), y_s.reshape(NB, 1, D_MODEL).astype(x_sample.dtype),
            kv_tail[:, :, 0][None].astype(dt_k), kv_tail[:, :, 1][None].astype(dt_v),
            C_p[None].astype(state_mlstm_C.dtype), n_p[None].astype(state_mlstm_n.dtype),
            m_prompt[None].astype(state_mlstm_m.dtype),
            k_new.reshape(1, NB, WINDOW, A_KV_HEADS, A_HD).astype(dt_k),
            v_new.reshape(1, NB, WINDOW, A_KV_HEADS, A_HD).astype(dt_v),
            C_s[None].astype(state_mlstm_C.dtype), n_s[None].astype(state_mlstm_n.dtype),
            m_s[:, :M_HEADS][None].astype(state_mlstm_m.dtype))
```

```python
import functools
import math

import numpy as np
import jax
import jax.numpy as jnp
from jax import lax
from jax.experimental import pallas as pl
from jax.experimental.pallas import tpu as pltpu

F32 = jnp.float32
BF16 = jnp.bfloat16
I32 = jnp.int32
U32 = jnp.uint32

D_MODEL = 1024
N_META = 16
M_HEADS = 4
M_DK = 128
M_DV = 128
M_WIDTH = M_HEADS * M_DV
A_HD = 64
A_HEADS = 8
A_KV_HEADS = 2
A_GROUP = A_HEADS // A_KV_HEADS
A_WIDTH = A_HEADS * A_HD
WINDOW = 128
REL_BUCKETS = 32
REL_MAX_DIST = 128
N_EXPERTS = 32
TOP_K = 4
D_FF = D_MODEL
SWIGLU_LIMIT = 7.0
SWIGLU_ALPHA = 1.702
DEPTH = 1
DN_ALPHA = (2.0 * DEPTH) ** 0.25
LN_EPS = 1e-5
IN_WIDTHS = (M_WIDTH, M_WIDTH, M_WIDTH, M_WIDTH, M_HEADS, M_HEADS, A_WIDTH, A_KV_HEADS * A_HD, A_KV_HEADS * A_HD)

LANES = 128
NEG = -1e30
VMEM_LIMIT = 56 * 1024 * 1024

M_CHUNK = 128
PROJ_TILE = 512
MERGE_TILE = 512
RANK_TILE = 512
ROW_TILE = 256
EXPERT_TILE = 512
SAMPLE_GROUP = 8
ISSUE_GROUP = 8
SWA_QBLOCKS = 2
XP_TILE = D_MODEL // 2 // LANES
YS_TILE = D_MODEL // 2 // LANES
HEAD_ORDER = (0, 4, 1, 5, 2, 6, 3, 7)


def _cparams(*sem):
    return pltpu.CompilerParams(dimension_semantics=sem, vmem_limit_bytes=VMEM_LIMIT)


def _log_sigmoid(x):
    return jnp.minimum(x, 0.0) - jnp.log1p(jnp.exp(-jnp.abs(x)))


def _sigmoid(x):
    return 1.0 / (1.0 + jnp.exp(-x))


def _proj_kernel(x_ref, wr_ref, wt_ref, brow_ref, bcol_ref, *outs, row_plan, t_plan, tail_cols):
    xb = x_ref[...].astype(BF16)
    tm = xb.shape[0]
    o = 0
    for (c0, width, kind, _) in row_plan:
        r = jnp.dot(xb, wr_ref[:, c0:c0 + width], preferred_element_type=F32)
        if kind == "gate":
            r = r + brow_ref[...]
            lane = lax.broadcasted_iota(I32, r.shape, 1)
            r = jnp.where(lane < M_HEADS, r, _log_sigmoid(r))
        outs[o][...] = r.astype(outs[o].dtype)
        o += 1
    for (r0, nrows, kind, _) in t_plan:
        r = lax.dot_general(wt_ref[r0:r0 + nrows, :], xb, (((1,), (1,)), ((), ())), preferred_element_type=F32)
        if kind == "gate":
            r = r + bcol_ref[...]
            row = lax.broadcasted_iota(I32, r.shape, 0)
            r = jnp.where(row < M_HEADS, r, _log_sigmoid(r))
        outs[o][...] = r.astype(outs[o].dtype)
        o += 1
    if tail_cols is not None:
        c0, width = tail_cols
        outs[o][...] = jnp.dot(xb[tm - WINDOW:, :], wr_ref[:, c0:c0 + width], preferred_element_type=F32)


def _proj(x, wr, wt, brow, bcol, row_plan, t_plan, tail_cols, tile, rows_per_group, name):
    t = x.shape[0]
    nt = t // tile
    out_shape, out_specs = [], []
    for (_, width, _, dt) in row_plan:
        out_shape.append(jax.ShapeDtypeStruct((t, width), dt))
        out_specs.append(pl.BlockSpec((tile, width), lambda i: (i, 0)))
    for (_, nrows, _, dt) in t_plan:
        out_shape.append(jax.ShapeDtypeStruct((nrows, t), dt))
        out_specs.append(pl.BlockSpec((nrows, tile), lambda i: (0, i)))
    if tail_cols is not None:
        tiles_per_group = rows_per_group // tile
        out_shape.append(jax.ShapeDtypeStruct((t // rows_per_group * WINDOW, tail_cols[1]), F32))
        out_specs.append(pl.BlockSpec((WINDOW, tail_cols[1]), lambda i: (i // tiles_per_group, 0)))
    kern = functools.partial(_proj_kernel, row_plan=row_plan, t_plan=t_plan, tail_cols=tail_cols)
    return pl.pallas_call(
        kern, out_shape=out_shape, grid=(nt,),
        in_specs=[pl.BlockSpec((tile, D_MODEL), lambda i: (i, 0)),
                  pl.BlockSpec(wr.shape, lambda i: (0, 0)),
                  pl.BlockSpec(wt.shape, lambda i: (0, 0)),
                  pl.BlockSpec(brow.shape, lambda i: (0, 0)),
                  pl.BlockSpec(bcol.shape, lambda i: (0, 0))],
        out_specs=out_specs, compiler_params=_cparams("arbitrary"), name=name,
    )(x, wr, wt, brow, bcol)


def _split3(a):
    hi = a.astype(BF16)
    r1 = a - hi.astype(F32)
    mid = r1.astype(BF16)
    lo = (r1 - mid.astype(F32)).astype(BF16)
    return hi, mid, lo


def _mlstm_kernel(q_ref, v_ref, kt_ref, gc_ref, gr_ref, c0_ref, m0_ref, h_ref, ct_out_ref, m_out_ref,
                  ct_scr, m_scr, *, n_valid):
    c = pl.program_id(1)
    nc = pl.num_programs(1)
    L = q_ref.shape[0]

    @pl.when(c == 0)
    def _():
        ct_scr[...] = c0_ref[...]
        m_scr[...] = m0_ref[...]

    gc = gc_ref[...]
    gr = gr_ref[...]
    if n_valid < L:
        rowc = lax.broadcasted_iota(I32, gc.shape, 0)
        lanec = lax.broadcasted_iota(I32, gc.shape, 1)
        gc = jnp.where(rowc < n_valid, gc, jnp.where(lanec < M_HEADS, NEG, 0.0))
        rowr = lax.broadcasted_iota(I32, gr.shape, 0)
        colr = lax.broadcasted_iota(I32, gr.shape, 1)
        gr = jnp.where(colr < n_valid, gr, jnp.where(rowr < M_HEADS, NEG, 0.0))
    r_i = lax.broadcasted_iota(I32, (L, L), 0)
    c_i = lax.broadcasted_iota(I32, (L, L), 1)
    causal = c_i <= r_i
    tril = jnp.where(causal, 1.0, 0.0).astype(BF16)
    triu = jnp.where(r_i <= c_i, 1.0, 0.0).astype(BF16)
    b_cols = sum(jnp.dot(tril, part, preferred_element_type=F32) for part in _split3(gc))
    b_rows = sum(jnp.dot(part, triu, preferred_element_type=F32) for part in _split3(gr))
    lane_l = lax.broadcasted_iota(I32, (L, LANES), 1)
    e0 = jnp.where(lane_l == 0, 1.0, 0.0)
    scale = M_DK ** -0.5

    for h in range(M_HEADS):
        sl = slice(h * M_DK, (h + 1) * M_DK)
        q = q_ref[:, sl]
        v = v_ref[:, sl]
        kt = kt_ref[sl, :]
        ig_c = gc[:, h:h + 1]
        b_c = b_cols[:, M_HEADS + h:M_HEADS + h + 1]
        ig_r = gr[h:h + 1, :]
        b_r = b_rows[M_HEADS + h:M_HEADS + h + 1, :]
        m_prev = m_scr[h:h + 1, 0:1]
        ct = ct_scr[h]

        d = jnp.where(causal, b_c + (ig_r - b_r), NEG)
        m_t = jnp.maximum(b_c + m_prev, jnp.max(d, axis=1, keepdims=True))
        qk = jnp.dot(q, kt, preferred_element_type=F32) * scale
        s = qk * jnp.exp(d - m_t)
        inter = jnp.dot(q, ct.astype(BF16), preferred_element_type=F32)
        v_aug = jnp.concatenate([v, e0.astype(BF16)], axis=1)
        intra = jnp.dot(s.astype(BF16), v_aug, preferred_element_type=F32)
        nd = jnp.exp(b_c + m_prev - m_t) * inter + intra
        den = nd[:, M_DV:M_DV + 1]
        h_ref[:, sl] = nd[:, :M_DV] / jnp.maximum(jnp.abs(den), jnp.exp(-m_t))

        b_last = b_c[L - 1:L, :]
        g = ig_c + b_last - b_c
        m_new = jnp.maximum(b_last + m_prev, jnp.max(g, axis=0, keepdims=True))
        a = jnp.exp(b_last + m_prev - m_new)
        wg = jnp.exp(g - m_new)
        wv = jnp.concatenate([(v.astype(F32) * wg).astype(BF16), (e0 * wg).astype(BF16)], axis=1)
        upd = jnp.dot(kt, wv, preferred_element_type=F32)
        ct_scr[h] = a * ct + upd * scale
        m_scr[h:h + 1, :] = jnp.broadcast_to(m_new, (1, LANES))

    @pl.when(c == nc - 1)
    def _():
        for h in range(M_HEADS):
            ct_out_ref[0, h] = ct_scr[h].T
        m_out_ref[0] = m_scr[...]


def _mlstm(qm, vm, kt, gc, gr, c0, m0, batch, n_valid, name):
    L = M_CHUNK
    nc = qm.shape[0] // (batch * L)
    kern = functools.partial(_mlstm_kernel, n_valid=n_valid)
    return pl.pallas_call(
        kern,
        out_shape=[jax.ShapeDtypeStruct((batch * nc * L, M_WIDTH), F32),
                   jax.ShapeDtypeStruct((batch, M_HEADS, 2 * M_DV, M_DK), F32),
                   jax.ShapeDtypeStruct((batch, 8, LANES), F32)],
        grid=(batch, nc),
        in_specs=[pl.BlockSpec((L, M_WIDTH), lambda b, c: (b * nc + c, 0)),
                  pl.BlockSpec((L, M_WIDTH), lambda b, c: (b * nc + c, 0)),
                  pl.BlockSpec((M_WIDTH, L), lambda b, c: (0, b * nc + c)),
                  pl.BlockSpec((L, LANES), lambda b, c: (b * nc + c, 0)),
                  pl.BlockSpec((8, L), lambda b, c: (0, b * nc + c)),
                  pl.BlockSpec((M_HEADS, M_DK, 2 * M_DV), lambda b, c: (0, 0, 0)),
                  pl.BlockSpec((8, LANES), lambda b, c: (0, 0))],
        out_specs=[pl.BlockSpec((L, M_WIDTH), lambda b, c: (b * nc + c, 0)),
                   pl.BlockSpec((1, M_HEADS, 2 * M_DV, M_DK), lambda b, c: (b, 0, 0, 0)),
                   pl.BlockSpec((1, 8, LANES), lambda b, c: (b, 0, 0))],
        scratch_shapes=[pltpu.VMEM((M_HEADS, M_DK, 2 * M_DV), F32), pltpu.VMEM((8, LANES), F32)],
        compiler_params=_cparams("arbitrary", "arbitrary"), name=name,
    )(qm, vm, kt, gc, gr, c0, m0)


def _outer_f32(a, b):
    ah, am, al = (t.astype(F32) for t in _split3(a))
    bh, bm, bl = (t.astype(F32) for t in _split3(b))
    z = jnp.zeros_like(ah)
    lhs = jnp.concatenate([ah, ah, ah, am, am, al, z, z], axis=0).astype(BF16)
    rhs = jnp.concatenate([bh, bm, bl, bh, bm, bh, z, z], axis=0).astype(BF16)
    return lax.dot_general(lhs, rhs, (((0,), (0,)), ((), ())), preferred_element_type=F32)


def _mlstm_step_kernel(c_ref, n_ref, m_ref, gc_ref, q_ref, k_ref, v_ref,
                       c_out_ref, n_out_ref, m_out_ref, h_ref):
    g = c_ref.shape[0]
    scale = M_DK ** -0.5
    lane_m = lax.broadcasted_iota(I32, (1, LANES), 1)
    for j in range(g):
        m_row = jnp.zeros((1, LANES), F32)
        for h in range(M_HEADS):
            sl = slice(h * M_DK, (h + 1) * M_DK)
            q = q_ref[j:j + 1, sl]
            k = k_ref[j:j + 1, sl] * scale
            v = v_ref[j:j + 1, sl]
            ig = gc_ref[j:j + 1, h:h + 1]
            lf = gc_ref[j:j + 1, M_HEADS + h:M_HEADS + h + 1]
            m = m_ref[j:j + 1, h:h + 1]
            c = c_ref[j, h]
            n = n_ref[j, h:h + 1, :]
            m_t = jnp.maximum(lf + m, ig)
            w = jnp.exp(lf + m - m_t)
            wg = jnp.exp(ig - m_t)
            s = jnp.sum(q * k, axis=1, keepdims=True) * wg
            q8 = jnp.broadcast_to(q, (8, M_DK)).astype(BF16)
            cq = lax.dot_general(q8, c.astype(BF16), (((1,), (1,)), ((), ())), preferred_element_type=F32)[0:1, :]
            den = w * jnp.sum(n * q, axis=1, keepdims=True) + s
            h_ref[j:j + 1, sl] = (w * cq + s * v) / jnp.maximum(jnp.abs(den), jnp.exp(-m_t))
            c_out_ref[j, h] = w * c + _outer_f32(wg * v, k)
            n_out_ref[j, h:h + 1, :] = w * n + wg * k
            m_row = jnp.where(lane_m == h, m_t, m_row)
        m_out_ref[j:j + 1, :] = m_row


def _mlstm_step(c, n, m_pad, gc, q, k, v):
    nb = c.shape[0]
    g = SAMPLE_GROUP
    row = lambda w: pl.BlockSpec((g, w), lambda i: (i, 0))
    return pl.pallas_call(
        _mlstm_step_kernel,
        out_shape=[jax.ShapeDtypeStruct(c.shape, F32), jax.ShapeDtypeStruct(n.shape, F32),
                   jax.ShapeDtypeStruct((nb, LANES), F32), jax.ShapeDtypeStruct((nb, M_WIDTH), F32)],
        grid=(nb // g,),
        in_specs=[pl.BlockSpec((g, M_HEADS, M_DV, M_DK), lambda i: (i, 0, 0, 0)),
                  pl.BlockSpec((g, M_HEADS, M_DK), lambda i: (i, 0, 0)),
                  row(LANES), row(LANES), row(M_WIDTH), row(M_WIDTH), row(M_WIDTH)],
        out_specs=[pl.BlockSpec((g, M_HEADS, M_DV, M_DK), lambda i: (i, 0, 0, 0)),
                   pl.BlockSpec((g, M_HEADS, M_DK), lambda i: (i, 0, 0)),
                   row(LANES), row(M_WIDTH)],
        compiler_params=_cparams("arbitrary"), name="mlstm_step",
    )(c, n, m_pad, gc, q, k, v)


def _swa_kernel(q_ref, kc_ref, kp_ref, vc_ref, vp_ref, km_ref, vm_ref, bias_ref, sink_ref, o_ref):
    j = pl.program_id(1)
    first = j == 0
    blk = WINDOW
    nqb = q_ref.shape[0] // blk
    kp = jnp.where(first, km_ref[...], kp_ref[...])
    vp = jnp.where(first, vm_ref[...], vp_ref[...])
    k = jnp.concatenate([kp, kc_ref[...]], axis=0)
    v = jnp.concatenate([vp, vc_ref[...]], axis=0)
    lane = lax.broadcasted_iota(I32, k.shape, 1)
    zero = jnp.zeros_like(k)
    k_half = (jnp.where(lane < A_HD, k, zero), jnp.where(lane >= A_HD, k, zero))
    v_half = (jnp.where(lane < A_HD, v, zero), jnp.where(lane >= A_HD, v, zero))
    lane_q = lax.broadcasted_iota(I32, (blk, LANES), 1)
    for u in range(nqb):
        rows = slice(u * blk, (u + 1) * blk)
        keys = slice(u * blk, (u + 2) * blk)
        table = jnp.where(first, 0, 1) if u == 0 else 1
        v_stack = jnp.concatenate([v_half[0][keys], v_half[1][keys]], axis=0)
        for p in range(A_GROUP):
            qs = q_ref[rows, p * LANES:(p + 1) * LANES]
            probs, inv = [], []
            for half in range(2):
                hd = HEAD_ORDER[2 * p + half]
                s = lax.dot_general(qs, k_half[half][keys], (((1,), (1,)), ((), ())), preferred_element_type=F32)
                s = s * (A_HD ** -0.5) + bias_ref[table, hd]
                sk = sink_ref[hd:hd + 1, 0:1]
                m = jnp.maximum(jnp.max(s, axis=1, keepdims=True), sk)
                e = jnp.exp(s - m)
                probs.append(e.astype(BF16))
                inv.append(1.0 / (jnp.sum(e, axis=1, keepdims=True) + jnp.exp(sk - m)))
            o = jnp.dot(jnp.concatenate(probs, axis=1), v_stack, preferred_element_type=F32)
            o_ref[rows, p * LANES:(p + 1) * LANES] = o * jnp.where(lane_q < A_HD, inv[0], inv[1])


def _swa(qa, ka, va, kmeta, vmeta, bias, sinks, batch):
    blk = WINDOW
    nqb = SWA_QBLOCKS
    nq = qa.shape[0] // (batch * blk * nqb)
    kv_cur = pl.BlockSpec((nqb * blk, LANES), lambda b, j: (b * nq + j, 0))
    kv_prev = pl.BlockSpec((blk, LANES), lambda b, j: ((b * nq + j) * nqb + jnp.where(j == 0, 0, -1), 0))
    const2 = lambda shape: pl.BlockSpec(shape, lambda b, j: (0, 0))
    return pl.pallas_call(
        _swa_kernel, out_shape=jax.ShapeDtypeStruct((qa.shape[0], A_WIDTH), F32), grid=(batch, nq),
        in_specs=[pl.BlockSpec((nqb * blk, A_WIDTH), lambda b, j: (b * nq + j, 0)),
                  kv_cur, kv_prev, kv_cur, kv_prev, const2((blk, LANES)), const2((blk, LANES)),
                  pl.BlockSpec(bias.shape, lambda b, j: (0, 0, 0, 0)),
                  const2((8, LANES))],
        out_specs=pl.BlockSpec((nqb * blk, A_WIDTH), lambda b, j: (b * nq + j, 0)),
        compiler_params=_cparams("arbitrary", "arbitrary"), name="swa_prompt",
    )(qa, ka, ka, va, va, kmeta, vmeta, bias, sinks)


def _swa_step_kernel(ck_ref, cv_ref, q_ref, k_ref, v_ref, bias_ref, sink_ref, ko_ref, vo_ref, o_ref):
    g = ck_ref.shape[0]
    lane = lax.broadcasted_iota(I32, (A_HEADS, LANES), 1)
    row = lax.broadcasted_iota(I32, (A_HEADS, LANES), 0)
    own_half = (row % 2 == 0) == (lane < A_HD)
    bias = bias_ref[...]
    sk = sink_ref[:, 0:1]
    for j in range(g):
        ko_ref[j, 0:WINDOW - 1, :] = ck_ref[j, 1:WINDOW, :]
        ko_ref[j, WINDOW - 1:WINDOW, :] = k_ref[j:j + 1, :]
        vo_ref[j, 0:WINDOW - 1, :] = cv_ref[j, 1:WINDOW, :]
        vo_ref[j, WINDOW - 1:WINDOW, :] = v_ref[j:j + 1, :]
        kk = ko_ref[j].astype(BF16)
        vv = vo_ref[j].astype(BF16)
        slabs = [q_ref[j:j + 1, p * LANES:(p + 1) * LANES] for p in range(A_GROUP)]
        q8 = jnp.concatenate([slabs[r // 2] for r in range(A_HEADS)], axis=0)
        q8 = jnp.where(own_half, q8, 0.0).astype(BF16)
        s = lax.dot_general(q8, kk, (((1,), (1,)), ((), ())), preferred_element_type=F32)
        s = s * (A_HD ** -0.5) + bias
        m = jnp.maximum(jnp.max(s, axis=1, keepdims=True), sk)
        e = jnp.exp(s - m)
        inv = 1.0 / (jnp.sum(e, axis=1, keepdims=True) + jnp.exp(sk - m))
        o8 = jnp.where(own_half, jnp.dot(e.astype(BF16), vv, preferred_element_type=F32) * inv, 0.0)
        for p in range(A_GROUP):
            o_ref[j:j + 1, p * LANES:(p + 1) * LANES] = o8[2 * p:2 * p + 1, :] + o8[2 * p + 1:2 * p + 2, :]


def _swa_step(ck, cv, q, k, v, bias_rows, sinks):
    nb = ck.shape[0]
    g = SAMPLE_GROUP
    cache = pl.BlockSpec((g, WINDOW, LANES), lambda i: (i, 0, 0))
    row = lambda w: pl.BlockSpec((g, w), lambda i: (i, 0))
    const = lambda a: pl.BlockSpec(a.shape, lambda i: (0, 0))
    return pl.pallas_call(
        _swa_step_kernel,
        out_shape=[jax.ShapeDtypeStruct(ck.shape, F32), jax.ShapeDtypeStruct(cv.shape, F32),
                   jax.ShapeDtypeStruct((nb, A_WIDTH), F32)],
        grid=(nb // g,),
        in_specs=[cache, cache, row(A_WIDTH), row(LANES), row(LANES), const(bias_rows), const(sinks)],
        out_specs=[cache, cache, row(A_WIDTH)],
        compiler_params=_cparams("arbitrary"), name="swa_step",
    )(ck, cv, q, k, v, bias_rows, sinks)


def _layer_norm(z, g, b):
    mu = jnp.mean(z, axis=1, keepdims=True)
    zc = z - mu
    var = jnp.mean(zc * zc, axis=1, keepdims=True)
    return zc * lax.rsqrt(var + LN_EPS) * g + b


def _pack_halves(x):
    w = x.shape[1] // 2
    lo = pltpu.bitcast(x[:, :w].astype(BF16).astype(F32), U32)
    hi = pltpu.bitcast(x[:, w:].astype(BF16).astype(F32), U32)
    return (lo >> 16) | (hi & jnp.uint32(0xFFFF0000))


def _unpack_halves(words):
    lo = pltpu.bitcast(words << 16, F32).astype(BF16)
    hi = pltpu.bitcast(words & jnp.uint32(0xFFFF0000), F32).astype(BF16)
    return lo, hi


def _to_token_tiles(ref, x):
    for q in range(x.shape[1] // LANES):
        ref[:, q, :] = x[:, q * LANES:(q + 1) * LANES]


def _merge_kernel(h_ref, om_ref, att_ref, x_ref, gm_ref, ga_ref, wo_ref, g1_ref, b1_ref, wr_ref, br_ref,
                  x1_ref, xp_ref, tk_ref, cnt_ref):
    @pl.when(pl.program_id(0) == 0)
    def _():
        cnt_ref[...] = jnp.zeros_like(cnt_ref)

    hm = h_ref[...] * _sigmoid(om_ref[...])
    ym = hm * lax.rsqrt(jnp.mean(hm * hm, axis=1, keepdims=True) + LN_EPS) * gm_ref[...]
    att = att_ref[...]
    ya = att * lax.rsqrt(jnp.mean(att * att, axis=1, keepdims=True) + LN_EPS) * ga_ref[...]
    mix = (jnp.dot(ym.astype(BF16), wo_ref[0:M_WIDTH, :], preferred_element_type=F32)
           + jnp.dot(ya.astype(BF16), wo_ref[M_WIDTH:, :], preferred_element_type=F32))
    x1 = _layer_norm(DN_ALPHA * x_ref[...] + mix, g1_ref[...], b1_ref[...])
    x1_ref[...] = x1
    _to_token_tiles(xp_ref, _pack_halves(x1))
    logits = jnp.dot(x1.astype(BF16), wr_ref[...], preferred_element_type=F32) + br_ref[...]
    lane = lax.broadcasted_iota(I32, logits.shape, 1).astype(F32)
    vals, idxs = [], []
    for _ in range(TOP_K):
        mx = jnp.max(logits, axis=1, keepdims=True)
        idx = jnp.min(jnp.where(logits == mx, lane, float(LANES)), axis=1, keepdims=True)
        vals.append(mx)
        idxs.append(idx)
        logits = jnp.where(lane == idx, 2.0 * NEG, logits)
    es = [jnp.exp(vk - vals[0]) for vk in vals]
    tot = es[0] + es[1] + es[2] + es[3]
    tk = jnp.zeros(logits.shape, F32)
    picked = jnp.zeros(logits.shape, F32)
    for k in range(TOP_K):
        tk = jnp.where(lane == float(k), es[k] / tot, tk)
        tk = jnp.where(lane == float(TOP_K + k), idxs[k], tk)
        picked = jnp.where(lane == idxs[k], 1.0, picked)
    tk_ref[...] = tk
    cnt_ref[...] = cnt_ref[...] + jnp.sum(picked, axis=0, keepdims=True)


def _merge(h, om, att, x, gm, ga, wo, g1, b1, wr, br, tile, name):
    t = x.shape[0]
    rows = lambda w: pl.BlockSpec((tile, w), lambda i: (i, 0))
    const = lambda a: pl.BlockSpec(a.shape, lambda i: (0, 0))
    return pl.pallas_call(
        _merge_kernel,
        out_shape=[jax.ShapeDtypeStruct((t, D_MODEL), F32), jax.ShapeDtypeStruct((t, XP_TILE, LANES), U32),
                   jax.ShapeDtypeStruct((t, LANES), F32), jax.ShapeDtypeStruct((8, LANES), F32)],
        grid=(t // tile,),
        in_specs=[rows(M_WIDTH), rows(M_WIDTH), rows(A_WIDTH), rows(D_MODEL), const(gm), const(ga), const(wo),
                  const(g1), const(b1), const(wr), const(br)],
        out_specs=[rows(D_MODEL), pl.BlockSpec((tile, XP_TILE, LANES), lambda i: (i, 0, 0)), rows(LANES),
                   pl.BlockSpec((8, LANES), lambda i: (0, 0))],
        compiler_params=_cparams("arbitrary"), name=name,
    )(h, om, att, x, gm, ga, wo, g1, b1, wr, br)


def _route_kernel(tk_ref, first_ref, strict_ref, dest_ref, next_scr):
    @pl.when(pl.program_id(0) == 0)
    def _():
        next_scr[...] = first_ref[...]

    tk = tk_ref[...]
    lane = lax.broadcasted_iota(I32, tk.shape, 1).astype(F32)
    onehots = [jnp.where(lane == tk[:, TOP_K + k:TOP_K + k + 1], 1.0, 0.0) for k in range(TOP_K)]
    tot = onehots[0] + onehots[1] + onehots[2] + onehots[3]
    row = jnp.dot(strict_ref[...], tot.astype(BF16), preferred_element_type=F32) + next_scr[0:1, :]
    out = jnp.zeros(tk.shape, F32)
    for k in range(TOP_K):
        out = jnp.where(lane == float(k), jnp.sum(onehots[k] * row, axis=1, keepdims=True), out)
    dest_ref[...] = out.astype(I32)
    next_scr[...] = next_scr[...] + jnp.sum(tot, axis=0, keepdims=True)


def _route(tk, first):
    t = tk.shape[0]
    tile = min(RANK_TILE, t)
    strict = jnp.asarray(np.tril(np.ones((tile, tile), np.float32), -1), BF16)
    return pl.pallas_call(
        _route_kernel, out_shape=jax.ShapeDtypeStruct((t, LANES), I32), grid=(t // tile,),
        in_specs=[pl.BlockSpec((tile, LANES), lambda i: (i, 0)), pl.BlockSpec((8, LANES), lambda i: (0, 0)),
                  pl.BlockSpec((tile, tile), lambda i: (0, 0))],
        out_specs=pl.BlockSpec((tile, LANES), lambda i: (i, 0)),
        scratch_shapes=[pltpu.VMEM((8, LANES), F32)],
        compiler_params=_cparams("arbitrary"), name="moe_route",
    )(tk, first, strict)


def _offsets_kernel(cnt_ref, off_ref, be_ref, nu_ref, *, tile):
    cnt = cnt_ref[...]
    nblk = jnp.floor((cnt + float(tile - 1)) * (1.0 / tile))
    r_i = lax.broadcasted_iota(I32, (LANES, LANES), 0)
    c_i = lax.broadcasted_iota(I32, (LANES, LANES), 1)
    incl = jnp.where(r_i <= c_i, 1.0, 0.0).astype(BF16)
    cum = jnp.dot(nblk.astype(BF16), incl, preferred_element_type=F32)
    off_ref[...] = (cum - nblk) * float(tile)
    rows = be_ref.shape[0]
    jb = (lax.broadcasted_iota(I32, (rows, LANES), 0) * LANES + lax.broadcasted_iota(I32, (rows, LANES), 1)).astype(F32)
    acc = jnp.zeros((rows, LANES), F32)
    for e in range(N_EXPERTS):
        acc = acc + jnp.where(jb >= cum[0:1, e:e + 1], 1.0, 0.0)
    be_ref[...] = jnp.minimum(acc, float(N_EXPERTS - 1)).astype(I32)
    nu_ref[...] = jnp.broadcast_to(cum[0:1, N_EXPERTS - 1:N_EXPERTS], nu_ref.shape).astype(I32)


def _offsets(cnt, n_blocks, tile):
    rows = -(-n_blocks // LANES)
    rows = -(-rows // 8) * 8
    return pl.pallas_call(
        functools.partial(_offsets_kernel, tile=tile),
        out_shape=[jax.ShapeDtypeStruct((8, LANES), F32), jax.ShapeDtypeStruct((rows, LANES), I32),
                   jax.ShapeDtypeStruct((8, LANES), I32)],
        name="moe_offsets",
    )(cnt)


def _dispatch_kernel(dest_ref, xp_ref, xs_in_ref, xs_ref, sem):
    del xs_in_ref
    t = xp_ref.shape[0]

    def row_copy(tok, dst):
        return pltpu.make_async_copy(xp_ref.at[pl.ds(tok, 1)], xs_ref.at[pl.ds(dst, 1)], sem)

    def issue(grp, carry):
        base = pl.multiple_of(grp * ISSUE_GROUP, ISSUE_GROUP)
        for u in range(ISSUE_GROUP):
            for k in range(TOP_K):
                row_copy(base + u, dest_ref[(base + u) * TOP_K + k]).start(priority=k % 2)
        return carry

    lax.fori_loop(0, t // ISSUE_GROUP, issue, 0)
    for k in range(TOP_K):
        pltpu.make_async_copy(xp_ref, xs_ref.at[pl.ds(0, t)], sem).wait()


def _dispatch(dest_flat, xp, xs):
    t = xp.shape[0]
    tile = min(ROW_TILE, t)
    return pl.pallas_call(
        _dispatch_kernel, out_shape=jax.ShapeDtypeStruct(xs.shape, xs.dtype), grid=(t // tile,),
        in_specs=[pl.BlockSpec((tile * TOP_K,), lambda i: (i,), memory_space=pltpu.SMEM),
                  pl.BlockSpec((tile,) + xp.shape[1:], lambda i: (i, 0, 0)),
                  pl.BlockSpec(memory_space=pl.ANY)],
        out_specs=pl.BlockSpec(memory_space=pl.ANY),
        scratch_shapes=[pltpu.SemaphoreType.DMA(())],
        input_output_aliases={2: 0},
        compiler_params=_cparams("arbitrary"), name="moe_dispatch",
    )(dest_flat, xp, xs)


def _expert_kernel(be_ref, nu_ref, xs_ref, w1_ref, b1g_ref, b1l_ref, w2_ref, b2_ref, perm_ref, ys_ref,
                   w1g_scr, w1l_scr, w2_scr, xq_scr, y_scr, sem, osem):
    j = pl.program_id(0)
    active = j < nu_ref[0]
    changed = jnp.logical_or(j == 0, be_ref[j] != be_ref[jnp.maximum(j - 1, 0)])
    half = D_MODEL // 2
    tm = y_scr.shape[0]
    slot = lax.rem(j, 2)

    def fetch(blk, slot):
        row0 = pl.multiple_of(blk * tm, tm)
        return [pltpu.make_async_copy(xs_ref.at[pl.ds(row0, tm), q, :], xq_scr.at[slot, q], sem.at[slot])
                for q in range(XP_TILE)]

    def put(blk):
        row0 = pl.multiple_of(blk * tm, tm)
        return [pltpu.make_async_copy(y_scr.at[:, q * LANES:(q + 1) * LANES], ys_ref.at[pl.ds(row0, tm), q, :], osem)
                for q in range(YS_TILE)]

    def emit(y):
        @pl.when(j > 0)
        def _():
            for cp in put(j - 1):
                cp.wait()

        y_scr[...] = _pack_halves(y)
        for cp in put(j):
            cp.start()

    @pl.when(j == 0)
    def _():
        for cp in fetch(0, 0):
            cp.start()

    @pl.when(j + 1 < nu_ref[0])
    def _():
        for cp in fetch(j + 1, 1 - slot):
            cp.start()

    @pl.when(jnp.logical_and(active, changed))
    def _():
        for c in range(2 * D_FF // 256):
            wc = w1_ref[0, :, c * 256:(c + 1) * 256].astype(BF16)
            d = jnp.dot(wc, perm_ref[...], preferred_element_type=F32).astype(BF16)
            w1g_scr[:, c * 128:(c + 1) * 128] = d[:, :128]
            w1l_scr[:, c * 128:(c + 1) * 128] = d[:, 128:]
        for c in range(D_FF // 256):
            w2_scr[c * 256:(c + 1) * 256, :] = w2_ref[0, c * 256:(c + 1) * 256, :].astype(BF16)

    @pl.when(active)
    def _():
        for cp in fetch(j, slot):
            cp.wait()
        lo, hi = _unpack_halves(jnp.concatenate([xq_scr[slot, q] for q in range(XP_TILE)], axis=1))
        y = jnp.zeros((tm, D_MODEL), F32)
        nchunk = 512
        for c in range(D_FF // nchunk):
            cs = slice(c * nchunk, (c + 1) * nchunk)
            hg = (jnp.dot(lo, w1g_scr[0:half, cs], preferred_element_type=F32)
                  + jnp.dot(hi, w1g_scr[half:, cs], preferred_element_type=F32) + b1g_ref[0, :, cs])
            hl = (jnp.dot(lo, w1l_scr[0:half, cs], preferred_element_type=F32)
                  + jnp.dot(hi, w1l_scr[half:, cs], preferred_element_type=F32) + b1l_ref[0, :, cs])
            x_glu = jnp.minimum(hg, SWIGLU_LIMIT)
            x_lin = jnp.clip(hl, -SWIGLU_LIMIT, SWIGLU_LIMIT)
            a = x_glu * _sigmoid(SWIGLU_ALPHA * x_glu) * (x_lin + 1.0)
            y = y + jnp.dot(a.astype(BF16), w2_scr[cs, :], preferred_element_type=F32)
        emit(y + b2_ref[0])

    @pl.when(jnp.logical_not(active))
    def _():
        emit(jnp.zeros((tm, D_MODEL), F32))

    @pl.when(j == pl.num_programs(0) - 1)
    def _():
        for cp in put(j):
            cp.wait()


def _experts(be, nu, xs, w1, b1g, b1l, w2, b2, perm, tile):
    n_blocks = xs.shape[0] // tile
    grid_spec = pltpu.PrefetchScalarGridSpec(
        num_scalar_prefetch=2, grid=(n_blocks,),
        in_specs=[pl.BlockSpec(memory_space=pl.ANY),
                  pl.BlockSpec((1, D_MODEL, 2 * D_FF), lambda j, be, nu: (be[j], 0, 0)),
                  pl.BlockSpec((1, 1, D_FF), lambda j, be, nu: (be[j], 0, 0)),
                  pl.BlockSpec((1, 1, D_FF), lambda j, be, nu: (be[j], 0, 0)),
                  pl.BlockSpec((1, D_FF, D_MODEL), lambda j, be, nu: (be[j], 0, 0)),
                  pl.BlockSpec((1, 1, D_MODEL), lambda j, be, nu: (be[j], 0, 0)),
                  pl.BlockSpec((256, 256), lambda j, be, nu: (0, 0))],
        out_specs=pl.BlockSpec(memory_space=pl.ANY),
        scratch_shapes=[pltpu.VMEM((D_MODEL, D_FF), BF16), pltpu.VMEM((D_MODEL, D_FF), BF16),
                        pltpu.VMEM((D_FF, D_MODEL), BF16), pltpu.VMEM((2, XP_TILE, tile, LANES), U32),
                        pltpu.VMEM((tile, YS_TILE * LANES), U32), pltpu.SemaphoreType.DMA((2,)),
                        pltpu.SemaphoreType.DMA(())])
    return pl.pallas_call(
        _expert_kernel, out_shape=jax.ShapeDtypeStruct((xs.shape[0], YS_TILE, LANES), U32), grid_spec=grid_spec,
        compiler_params=_cparams("arbitrary"), name="moe_experts",
    )(be, nu, xs, w1, b1g, b1l, w2, b2, perm)


def _combine_kernel(dest_ref, next_ref, ys_ref, tk_ref, x1_ref, g2_ref, b2_ref, out_ref, buf, sem):
    i = pl.program_id(0)
    t = x1_ref.shape[0]
    slot = lax.rem(i, 2)

    def gather(idx_ref, s):
        def issue(grp, carry):
            base = pl.multiple_of(grp * 8, 8)
            for u in range(8):
                for k in range(TOP_K):
                    pltpu.make_async_copy(ys_ref.at[idx_ref[(base + u) * TOP_K + k]],
                                          buf.at[s, k, grp, :, u, :], sem.at[s]).start(priority=k % 2)
            return carry

        lax.fori_loop(0, t // 8, issue, 0)

    @pl.when(i == 0)
    def _():
        gather(dest_ref, 0)

    @pl.when(i + 1 < pl.num_programs(0))
    def _():
        gather(next_ref, 1 - slot)

    for k in range(TOP_K):
        for u in range(8):
            pltpu.make_async_copy(ys_ref.at[pl.ds(0, t // 8)], buf.at[slot, k, :, :, u, :], sem.at[slot]).wait()
    tk = tk_ref[...]
    los, his = [], []
    for q in range(YS_TILE):
        lo = hi = None
        for k in range(TOP_K):
            words = buf[slot, k, :, q].reshape(t, LANES)
            g = tk[:, k:k + 1]
            lo_k = g * pltpu.bitcast(words << 16, F32)
            hi_k = g * pltpu.bitcast(words & jnp.uint32(0xFFFF0000), F32)
            lo = lo_k if lo is None else lo + lo_k
            hi = hi_k if hi is None else hi + hi_k
        los.append(lo)
        his.append(hi)
    ff = jnp.concatenate(los + his, axis=1)
    out_ref[...] = _layer_norm(DN_ALPHA * x1_ref[...] + ff, g2_ref[...], b2_ref[...])


def _combine(dest_flat, ys, tk, x1, g2, b2):
    t = x1.shape[0]
    tile = min(ROW_TILE, t)
    n = t // tile
    return pl.pallas_call(
        _combine_kernel, out_shape=jax.ShapeDtypeStruct((t, D_MODEL), F32), grid=(n,),
        in_specs=[pl.BlockSpec((tile * TOP_K,), lambda i: (i,), memory_space=pltpu.SMEM),
                  pl.BlockSpec((tile * TOP_K,), lambda i: (jnp.minimum(i + 1, n - 1),), memory_space=pltpu.SMEM),
                  pl.BlockSpec(memory_space=pl.ANY),
                  pl.BlockSpec((tile, LANES), lambda i: (i, 0)),
                  pl.BlockSpec((tile, D_MODEL), lambda i: (i, 0)),
                  pl.BlockSpec((1, D_MODEL), lambda i: (0, 0)),
                  pl.BlockSpec((1, D_MODEL), lambda i: (0, 0))],
        out_specs=pl.BlockSpec((tile, D_MODEL), lambda i: (i, 0)),
        scratch_shapes=[pltpu.VMEM((2, TOP_K, tile // 8, YS_TILE, 8, LANES), U32), pltpu.SemaphoreType.DMA((2,))],
        compiler_params=_cparams("arbitrary"), name="moe_combine",
    )(dest_flat, dest_flat, ys, tk, x1, g2, b2)


def _rel_bucket(dist):
    exact = REL_BUCKETS // 2
    d = np.maximum(dist, 0)
    log_b = exact + (np.log(np.maximum(d, 1).astype(np.float32) / np.float32(exact))
                     / np.float32(math.log(REL_MAX_DIST / exact)) * np.float32(REL_BUCKETS - exact)).astype(np.int32)
    return np.where(d < exact, d, np.minimum(log_b, REL_BUCKETS - 1)).astype(np.int32)


def _bias_lookup(table, bucket, valid):
    bucket = jnp.asarray(bucket)[None]
    acc = jnp.zeros((table.shape[1],) + bucket.shape[1:], F32)
    for b in range(REL_BUCKETS):
        acc = jnp.where(bucket == b, table[b].reshape((-1,) + (1,) * (bucket.ndim - 1)), acc)
    return jnp.where(jnp.asarray(valid)[None], acc, NEG)


def _bias_tables(rel_bias):
    table = rel_bias.astype(F32)
    r = np.arange(WINDOW)[:, None]
    c = np.arange(2 * WINDOW)[None, :]
    dist = r + WINDOW - c
    valid = (dist >= 0) & (dist < WINDOW)
    dist0 = np.where(c < N_META, N_META + r - c, dist)
    valid0 = np.where(c < N_META, dist0 < WINDOW, (c >= WINDOW) & valid)
    both = jnp.stack([_bias_lookup(table, _rel_bucket(dist0), valid0), _bias_lookup(table, _rel_bucket(dist), valid)])
    dist_s = WINDOW - 1 - np.arange(WINDOW)
    rows = _bias_lookup(table[:, np.asarray(HEAD_ORDER)], _rel_bucket(dist_s), np.ones_like(dist_s, bool))
    return both, rows


def _perm_heads(a, axis):
    parts = [lax.slice_in_dim(a, h * A_HD, (h + 1) * A_HD, axis=axis) for h in HEAD_ORDER]
    return jnp.concatenate(parts, axis=axis)


def _rep_rows(vec, rows=8):
    out = jnp.zeros((rows, LANES), F32)
    return out.at[:vec.shape[0], :].set(jnp.broadcast_to(vec.astype(F32)[:, None], (vec.shape[0], LANES)))


def kernel(x_prompt, x_sample, cache_swa_k, cache_swa_v, state_mlstm_C, state_mlstm_n, state_mlstm_m, meta_tokens, rel_bias, w_in, b_igate, b_fgate, attn_sinks, g_mlstm_out, g_attn_out, w_out, ln1_g, ln1_b, w_router, b_router, w_moe1, b_moe1, w_moe2, b_moe2, ln2_g, ln2_b):
    B, S, _ = x_prompt.shape
    NB = x_sample.shape[0]
    assert x_sample.shape[1] == 1 and w_in.shape[0] == 1
    assert S % PROJ_TILE == 0 and S % M_CHUNK == 0 and S % WINDOW == 0 and NB % SAMPLE_GROUP == 0
    l = 0

    pts = np.cumsum(IN_WIDTHS)[:-1].tolist()
    w_qm, w_km, w_vm, w_om, w_ig, w_fg, w_qa, w_ka, w_va = jnp.split(w_in[l], pts, axis=1)
    w_gate = jnp.pad(jnp.concatenate([w_ig, w_fg], axis=1), ((0, 0), (0, LANES - 2 * M_HEADS)))
    w_qa = _perm_heads(w_qa, 1)
    b_gate = jnp.concatenate([b_igate[l], b_fgate[l]]).astype(F32)
    brow = jnp.pad(b_gate, (0, LANES - 2 * M_HEADS))[None, :]
    bcol = b_gate[:, None]
    bf = lambda a: a.astype(BF16)
    wr_p = bf(jnp.concatenate([w_qm, w_vm, w_om, w_qa, w_ka, w_va, w_gate], axis=1))
    wt_p = bf(jnp.concatenate([w_km.T, w_ig.T, w_fg.T], axis=0))
    plan_p = ((0, 512, "plain", BF16), (512, 512, "plain", BF16), (1024, 512, "plain", F32),
              (1536, 512, "plain", BF16), (2048, 128, "plain", BF16), (2176, 128, "plain", BF16),
              (2304, 128, "gate", F32))
    tplan_p = ((0, 512, "plain", BF16), (512, 8, "gate", F32))
    wr_s = bf(jnp.concatenate([w_qm, w_km, w_vm, w_om, w_qa, w_ka, w_va, w_gate], axis=1))
    plan_s = ((0, 512, "plain", F32), (512, 512, "plain", F32), (1024, 512, "plain", F32), (1536, 512, "plain", F32),
              (2048, 512, "plain", F32), (2560, 128, "plain", F32), (2688, 128, "plain", F32), (2816, 128, "gate", F32))

    bias_tab, bias_rows = _bias_tables(rel_bias)
    sinks = _rep_rows(attn_sinks[l])
    sinks_step = _rep_rows(attn_sinks[l][np.asarray(HEAD_ORDER)])
    g_m = g_mlstm_out[l].astype(F32)[None, :]
    g_a = _perm_heads(g_attn_out[l].astype(F32), 0)[None, :]
    wo = bf(jnp.concatenate([w_out[l][:M_WIDTH], _perm_heads(w_out[l][M_WIDTH:], 0)], axis=0))
    g1, b1 = ln1_g[l].astype(F32)[None, :], ln1_b[l].astype(F32)[None, :]
    g2, b2 = ln2_g[l].astype(F32)[None, :], ln2_b[l].astype(F32)[None, :]
    w_r = bf(jnp.pad(w_router[l], ((0, 0), (0, LANES - N_EXPERTS))))
    b_r = jnp.pad(b_router[l].astype(F32), (0, LANES - N_EXPERTS), constant_values=NEG)[None, :]
    b1g = b_moe1[l][:, 0::2].astype(F32)[:, None, :]
    b1l = b_moe1[l][:, 1::2].astype(F32)[:, None, :]
    b2e = b_moe2[l].astype(F32)[:, None, :]
    pj = np.zeros((256, 256), np.float32)
    pj[2 * np.arange(128), np.arange(128)] = 1.0
    pj[2 * np.arange(128) + 1, 128 + np.arange(128)] = 1.0
    perm = jnp.asarray(pj, BF16)

    xp2 = x_prompt.reshape(B * S, D_MODEL)
    qm, vm, om, qa, ka, va, gc, kt, gr, kv_tail = _proj(
        xp2, wr_p, wt_p, brow, bcol, plan_p, tplan_p, (2048, 256), PROJ_TILE, S, "proj_prompt")
    x_meta = jnp.pad(meta_tokens.astype(F32), ((0, M_CHUNK - N_META), (0, 0)))
    qm0, vm0, _, _, ka0, va0, gc0, kt0, gr0 = _proj(
        x_meta, wr_p, wt_p, brow, bcol, plan_p, tplan_p, None, M_CHUNK, M_CHUNK, "proj_meta")
    xs2 = x_sample.reshape(NB, D_MODEL)
    qm_s, km_s, vm_s, om_s, qa_s, ka_s, va_s, gc_s = _proj(
        xs2, wr_s, wt_p, brow, bcol, plan_s, (), None, NB, NB, "proj_sample")

    zero_c = jnp.zeros((M_HEADS, M_DK, 2 * M_DV), F32)
    zero_m = jnp.zeros((8, LANES), F32)
    _, ct_meta, m_meta = _mlstm(qm0, vm0, kt0, gc0, gr0, zero_c, zero_m, 1, N_META, "mlstm_meta")
    c0 = jnp.swapaxes(ct_meta[0], 1, 2)
    h_p, ct_p, m_p = _mlstm(qm, vm, kt, gc, gr, c0, m_meta[0], B, M_CHUNK, "mlstm_prompt")
    C_p = ct_p[:, :, :M_DV, :]
    n_p = ct_p[:, :, M_DV, :]
    m_prompt = m_p[:, :M_HEADS, 0]
    m_pad = jnp.pad(state_mlstm_m[l].astype(F32), ((0, 0), (0, LANES - M_HEADS)))
    C_s, n_s, m_s, h_s = _mlstm_step(state_mlstm_C[l].astype(F32), state_mlstm_n[l].astype(F32), m_pad,
                                     gc_s, qm_s, km_s, vm_s)

    att_p = _swa(qa, ka, va, ka0, va0, bias_tab, sinks, B)
    ck = cache_swa_k[l].reshape(NB, WINDOW, LANES)
    cv = cache_swa_v[l].reshape(NB, WINDOW, LANES)
    k_new, v_new, att_s = _swa_step(ck, cv, qa_s, ka_s, va_s, bias_rows, sinks_step)

    x1_p, xpk_p, tk_p, cnt_p = _merge(h_p, om, att_p, xp2, g_m, g_a, wo, g1, b1, w_r, b_r, MERGE_TILE, "merge_prompt")
    x1_s, xpk_s, tk_s, cnt_s = _merge(h_s, om_s, att_s, xs2, g_m, g_a, wo, g1, b1, w_r, b_r, NB, "merge_sample")

    T_p = B * S
    assert T_p % RANK_TILE == 0 and T_p % ROW_TILE == 0
    n_blocks = -(-((T_p + NB) * TOP_K) // EXPERT_TILE) + N_EXPERTS
    off, be2, nu2 = _offsets(cnt_p + cnt_s, n_blocks, EXPERT_TILE)
    dest_p = _route(tk_p, off)[:, :TOP_K].reshape(-1)
    dest_s = _route(tk_s, off + cnt_p)[:, :TOP_K].reshape(-1)
    be = be2.reshape(-1)[:n_blocks]
    nu = nu2[0, :1]
    xs = jnp.zeros((n_blocks * EXPERT_TILE, XP_TILE, LANES), U32)
    xs = _dispatch(dest_p, xpk_p, xs)
    xs = _dispatch(dest_s, xpk_s, xs)
    ys = _experts(be, nu, xs, w_moe1[l], b1g, b1l, w_moe2[l], b2e, perm, EXPERT_TILE)
    y_p = _combine(dest_p, ys, tk_p, x1_p, g2, b2)
    y_s = _combine(dest_s, ys, tk_s, x1_s, g2, b2)

    kv_tail = kv_tail.reshape(B, WINDOW, 2, A_KV_HEADS, A_HD)
    dt_k, dt_v = cache_swa_k.dtype, cache_swa_v.dtype
    return (y_p.reshape(B, S, D_MODEL).astype(x_prompt.dtype), y_s.reshape(NB, 1, D_MODEL).astype(x_sample.dtype),
            kv_tail[:, :, 0][None].astype(dt_k), kv_tail[:, :, 1][None].astype(dt_v),
            C_p[None].astype(state_mlstm_C.dtype), n_p[None].astype(state_mlstm_n.dtype),
            m_prompt[None].astype(state_mlstm_m.dtype),
            k_new.reshape(1, NB, WINDOW, A_KV_HEADS, A_HD).astype(dt_k),
            v_new.reshape(1, NB, WINDOW, A_KV_HEADS, A_HD).astype(dt_v),
            C_s[None].astype(state_mlstm_C.dtype), n_s[None].astype(state_mlstm_n.dtype),
            m_s[:, :M_HEADS][None].astype(state_mlstm_m.dtype))
```

```python
import functools
import math

import numpy as np
import jax
import jax.numpy as jnp
from jax import lax
from jax.experimental import pallas as pl
from jax.experimental.pallas import tpu as pltpu

F32 = jnp.float32
BF16 = jnp.bfloat16
I32 = jnp.int32
U32 = jnp.uint32

D_MODEL = 1024
N_META = 16
M_HEADS = 4
M_DK = 128
M_DV = 128
M_WIDTH = M_HEADS * M_DV
A_HD = 64
A_HEADS = 8
A_KV_HEADS = 2
A_GROUP = A_HEADS // A_KV_HEADS
A_WIDTH = A_HEADS * A_HD
WINDOW = 128
REL_BUCKETS = 32
REL_MAX_DIST = 128
N_EXPERTS = 32
TOP_K = 4
D_FF = D_MODEL
SWIGLU_LIMIT = 7.0
SWIGLU_ALPHA = 1.702
DEPTH = 1
DN_ALPHA = (2.0 * DEPTH) ** 0.25
LN_EPS = 1e-5
IN_WIDTHS = (M_WIDTH, M_WIDTH, M_WIDTH, M_WIDTH, M_HEADS, M_HEADS, A_WIDTH, A_KV_HEADS * A_HD, A_KV_HEADS * A_HD)

LANES = 128
NEG = -1e30
VMEM_LIMIT = 56 * 1024 * 1024

M_CHUNK = 128
PROJ_TILE = 512
MERGE_TILE = 512
RANK_TILE = 512
ROW_TILE = 256
EXPERT_TILE = 1024
SAMPLE_GROUP = 8
ISSUE_GROUP = 8
SWA_QBLOCKS = 2
XP_TILE = D_MODEL // 2 // LANES
YS_TILE = D_MODEL // 2 // LANES
HEAD_ORDER = (0, 4, 1, 5, 2, 6, 3, 7)


def _cparams(*sem):
    return pltpu.CompilerParams(dimension_semantics=sem, vmem_limit_bytes=VMEM_LIMIT)


def _log_sigmoid(x):
    return jnp.minimum(x, 0.0) - jnp.log1p(jnp.exp(-jnp.abs(x)))


def _sigmoid(x):
    return 1.0 / (1.0 + jnp.exp(-x))


def _proj_kernel(x_ref, wr_ref, wt_ref, brow_ref, bcol_ref, *outs, row_plan, t_plan, tail_cols):
    xb = x_ref[...].astype(BF16)
    tm = xb.shape[0]
    o = 0
    for (c0, width, kind, _) in row_plan:
        r = jnp.dot(xb, wr_ref[:, c0:c0 + width], preferred_element_type=F32)
        if kind == "gate":
            r = r + brow_ref[...]
            lane = lax.broadcasted_iota(I32, r.shape, 1)
            r = jnp.where(lane < M_HEADS, r, _log_sigmoid(r))
        outs[o][...] = r.astype(outs[o].dtype)
        o += 1
    for (r0, nrows, kind, _) in t_plan:
        r = lax.dot_general(wt_ref[r0:r0 + nrows, :], xb, (((1,), (1,)), ((), ())), preferred_element_type=F32)
        if kind == "gate":
            r = r + bcol_ref[...]
            row = lax.broadcasted_iota(I32, r.shape, 0)
            r = jnp.where(row < M_HEADS, r, _log_sigmoid(r))
        outs[o][...] = r.astype(outs[o].dtype)
        o += 1
    if tail_cols is not None:
        c0, width = tail_cols
        outs[o][...] = jnp.dot(xb[tm - WINDOW:, :], wr_ref[:, c0:c0 + width], preferred_element_type=F32)


def _proj(x, wr, wt, brow, bcol, row_plan, t_plan, tail_cols, tile, rows_per_group, name):
    t = x.shape[0]
    nt = t // tile
    out_shape, out_specs = [], []
    for (_, width, _, dt) in row_plan:
        out_shape.append(jax.ShapeDtypeStruct((t, width), dt))
        out_specs.append(pl.BlockSpec((tile, width), lambda i: (i, 0)))
    for (_, nrows, _, dt) in t_plan:
        out_shape.append(jax.ShapeDtypeStruct((nrows, t), dt))
        out_specs.append(pl.BlockSpec((nrows, tile), lambda i: (0, i)))
    if tail_cols is not None:
        tiles_per_group = rows_per_group // tile
        out_shape.append(jax.ShapeDtypeStruct((t // rows_per_group * WINDOW, tail_cols[1]), F32))
        out_specs.append(pl.BlockSpec((WINDOW, tail_cols[1]), lambda i: (i // tiles_per_group, 0)))
    kern = functools.partial(_proj_kernel, row_plan=row_plan, t_plan=t_plan, tail_cols=tail_cols)
    return pl.pallas_call(
        kern, out_shape=out_shape, grid=(nt,),
        in_specs=[pl.BlockSpec((tile, D_MODEL), lambda i: (i, 0)),
                  pl.BlockSpec(wr.shape, lambda i: (0, 0)),
                  pl.BlockSpec(wt.shape, lambda i: (0, 0)),
                  pl.BlockSpec(brow.shape, lambda i: (0, 0)),
                  pl.BlockSpec(bcol.shape, lambda i: (0, 0))],
        out_specs=out_specs, compiler_params=_cparams("arbitrary"), name=name,
    )(x, wr, wt, brow, bcol)


def _split3(a):
    hi = a.astype(BF16)
    r1 = a - hi.astype(F32)
    mid = r1.astype(BF16)
    lo = (r1 - mid.astype(F32)).astype(BF16)
    return hi, mid, lo


def _mlstm_kernel(q_ref, v_ref, kt_ref, gc_ref, gr_ref, c0_ref, m0_ref, h_ref, ct_out_ref, m_out_ref,
                  ct_scr, m_scr, *, n_valid):
    c = pl.program_id(1)
    nc = pl.num_programs(1)
    L = q_ref.shape[0]

    @pl.when(c == 0)
    def _():
        ct_scr[...] = c0_ref[...]
        m_scr[...] = m0_ref[...]

    gc = gc_ref[...]
    gr = gr_ref[...]
    if n_valid < L:
        rowc = lax.broadcasted_iota(I32, gc.shape, 0)
        lanec = lax.broadcasted_iota(I32, gc.shape, 1)
        gc = jnp.where(rowc < n_valid, gc, jnp.where(lanec < M_HEADS, NEG, 0.0))
        rowr = lax.broadcasted_iota(I32, gr.shape, 0)
        colr = lax.broadcasted_iota(I32, gr.shape, 1)
        gr = jnp.where(colr < n_valid, gr, jnp.where(rowr < M_HEADS, NEG, 0.0))
    r_i = lax.broadcasted_iota(I32, (L, L), 0)
    c_i = lax.broadcasted_iota(I32, (L, L), 1)
    causal = c_i <= r_i
    tril = jnp.where(causal, 1.0, 0.0).astype(BF16)
    triu = jnp.where(r_i <= c_i, 1.0, 0.0).astype(BF16)
    b_cols = sum(jnp.dot(tril, part, preferred_element_type=F32) for part in _split3(gc))
    b_rows = sum(jnp.dot(part, triu, preferred_element_type=F32) for part in _split3(gr))
    lane_l = lax.broadcasted_iota(I32, (L, LANES), 1)
    e0 = jnp.where(lane_l == 0, 1.0, 0.0)
    scale = M_DK ** -0.5

    for h in range(M_HEADS):
        sl = slice(h * M_DK, (h + 1) * M_DK)
        q = q_ref[:, sl]
        v = v_ref[:, sl]
        kt = kt_ref[sl, :]
        ig_c = gc[:, h:h + 1]
        b_c = b_cols[:, M_HEADS + h:M_HEADS + h + 1]
        ig_r = gr[h:h + 1, :]
        b_r = b_rows[M_HEADS + h:M_HEADS + h + 1, :]
        m_prev = m_scr[h:h + 1, 0:1]
        ct = ct_scr[h]

        d = jnp.where(causal, b_c + (ig_r - b_r), NEG)
        m_t = jnp.maximum(b_c + m_prev, jnp.max(d, axis=1, keepdims=True))
        qk = jnp.dot(q, kt, preferred_element_type=F32) * scale
        s = qk * jnp.exp(d - m_t)
        inter = jnp.dot(q, ct.astype(BF16), preferred_element_type=F32)
        v_aug = jnp.concatenate([v, e0.astype(BF16)], axis=1)
        intra = jnp.dot(s.astype(BF16), v_aug, preferred_element_type=F32)
        nd = jnp.exp(b_c + m_prev - m_t) * inter + intra
        den = nd[:, M_DV:M_DV + 1]
        h_ref[:, sl] = nd[:, :M_DV] / jnp.maximum(jnp.abs(den), jnp.exp(-m_t))

        b_last = b_c[L - 1:L, :]
        g = ig_c + b_last - b_c
        m_new = jnp.maximum(b_last + m_prev, jnp.max(g, axis=0, keepdims=True))
        a = jnp.exp(b_last + m_prev - m_new)
        wg = jnp.exp(g - m_new)
        wv = jnp.concatenate([(v.astype(F32) * wg).astype(BF16), (e0 * wg).astype(BF16)], axis=1)
        upd = jnp.dot(kt, wv, preferred_element_type=F32)
        ct_scr[h] = a * ct + upd * scale
        m_scr[h:h + 1, :] = jnp.broadcast_to(m_new, (1, LANES))

    @pl.when(c == nc - 1)
    def _():
        for h in range(M_HEADS):
            ct_out_ref[0, h] = ct_scr[h].T
        m_out_ref[0] = m_scr[...]


def _mlstm(qm, vm, kt, gc, gr, c0, m0, batch, n_valid, name):
    L = M_CHUNK
    nc = qm.shape[0] // (batch * L)
    kern = functools.partial(_mlstm_kernel, n_valid=n_valid)
    return pl.pallas_call(
        kern,
        out_shape=[jax.ShapeDtypeStruct((batch * nc * L, M_WIDTH), F32),
                   jax.ShapeDtypeStruct((batch, M_HEADS, 2 * M_DV, M_DK), F32),
                   jax.ShapeDtypeStruct((batch, 8, LANES), F32)],
        grid=(batch, nc),
        in_specs=[pl.BlockSpec((L, M_WIDTH), lambda b, c: (b * nc + c, 0)),
                  pl.BlockSpec((L, M_WIDTH), lambda b, c: (b * nc + c, 0)),
                  pl.BlockSpec((M_WIDTH, L), lambda b, c: (0, b * nc + c)),
                  pl.BlockSpec((L, LANES), lambda b, c: (b * nc + c, 0)),
                  pl.BlockSpec((8, L), lambda b, c: (0, b * nc + c)),
                  pl.BlockSpec((M_HEADS, M_DK, 2 * M_DV), lambda b, c: (0, 0, 0)),
                  pl.BlockSpec((8, LANES), lambda b, c: (0, 0))],
        out_specs=[pl.BlockSpec((L, M_WIDTH), lambda b, c: (b * nc + c, 0)),
                   pl.BlockSpec((1, M_HEADS, 2 * M_DV, M_DK), lambda b, c: (b, 0, 0, 0)),
                   pl.BlockSpec((1, 8, LANES), lambda b, c: (b, 0, 0))],
        scratch_shapes=[pltpu.VMEM((M_HEADS, M_DK, 2 * M_DV), F32), pltpu.VMEM((8, LANES), F32)],
        compiler_params=_cparams("arbitrary", "arbitrary"), name=name,
    )(qm, vm, kt, gc, gr, c0, m0)


def _outer_f32(a, b):
    ah, am, al = (t.astype(F32) for t in _split3(a))
    bh, bm, bl = (t.astype(F32) for t in _split3(b))
    z = jnp.zeros_like(ah)
    lhs = jnp.concatenate([ah, ah, ah, am, am, al, z, z], axis=0).astype(BF16)
    rhs = jnp.concatenate([bh, bm, bl, bh, bm, bh, z, z], axis=0).astype(BF16)
    return lax.dot_general(lhs, rhs, (((0,), (0,)), ((), ())), preferred_element_type=F32)


def _mlstm_step_kernel(c_ref, n_ref, m_ref, gc_ref, q_ref, k_ref, v_ref,
                       c_out_ref, n_out_ref, m_out_ref, h_ref):
    g = c_ref.shape[0]
    scale = M_DK ** -0.5
    lane_m = lax.broadcasted_iota(I32, (1, LANES), 1)
    for j in range(g):
        m_row = jnp.zeros((1, LANES), F32)
        for h in range(M_HEADS):
            sl = slice(h * M_DK, (h + 1) * M_DK)
            q = q_ref[j:j + 1, sl]
            k = k_ref[j:j + 1, sl] * scale
            v = v_ref[j:j + 1, sl]
            ig = gc_ref[j:j + 1, h:h + 1]
            lf = gc_ref[j:j + 1, M_HEADS + h:M_HEADS + h + 1]
            m = m_ref[j:j + 1, h:h + 1]
            c = c_ref[j, h]
            n = n_ref[j, h:h + 1, :]
            m_t = jnp.maximum(lf + m, ig)
            w = jnp.exp(lf + m - m_t)
            wg = jnp.exp(ig - m_t)
            s = jnp.sum(q * k, axis=1, keepdims=True) * wg
            q8 = jnp.broadcast_to(q, (8, M_DK)).astype(BF16)
            cq = lax.dot_general(q8, c.astype(BF16), (((1,), (1,)), ((), ())), preferred_element_type=F32)[0:1, :]
            den = w * jnp.sum(n * q, axis=1, keepdims=True) + s
            h_ref[j:j + 1, sl] = (w * cq + s * v) / jnp.maximum(jnp.abs(den), jnp.exp(-m_t))
            c_out_ref[j, h] = w * c + _outer_f32(wg * v, k)
            n_out_ref[j, h:h + 1, :] = w * n + wg * k
            m_row = jnp.where(lane_m == h, m_t, m_row)
        m_out_ref[j:j + 1, :] = m_row


def _mlstm_step(c, n, m_pad, gc, q, k, v):
    nb = c.shape[0]
    g = SAMPLE_GROUP
    row = lambda w: pl.BlockSpec((g, w), lambda i: (i, 0))
    return pl.pallas_call(
        _mlstm_step_kernel,
        out_shape=[jax.ShapeDtypeStruct(c.shape, F32), jax.ShapeDtypeStruct(n.shape, F32),
                   jax.ShapeDtypeStruct((nb, LANES), F32), jax.ShapeDtypeStruct((nb, M_WIDTH), F32)],
        grid=(nb // g,),
        in_specs=[pl.BlockSpec((g, M_HEADS, M_DV, M_DK), lambda i: (i, 0, 0, 0)),
                  pl.BlockSpec((g, M_HEADS, M_DK), lambda i: (i, 0, 0)),
                  row(LANES), row(LANES), row(M_WIDTH), row(M_WIDTH), row(M_WIDTH)],
        out_specs=[pl.BlockSpec((g, M_HEADS, M_DV, M_DK), lambda i: (i, 0, 0, 0)),
                   pl.BlockSpec((g, M_HEADS, M_DK), lambda i: (i, 0, 0)),
                   row(LANES), row(M_WIDTH)],
        compiler_params=_cparams("arbitrary"), name="mlstm_step",
    )(c, n, m_pad, gc, q, k, v)


def _swa_kernel(q_ref, kc_ref, kp_ref, vc_ref, vp_ref, km_ref, vm_ref, bias_ref, sink_ref, o_ref):
    j = pl.program_id(1)
    first = j == 0
    blk = WINDOW
    nqb = q_ref.shape[0] // blk
    kp = jnp.where(first, km_ref[...], kp_ref[...])
    vp = jnp.where(first, vm_ref[...], vp_ref[...])
    k = jnp.concatenate([kp, kc_ref[...]], axis=0)
    v = jnp.concatenate([vp, vc_ref[...]], axis=0)
    lane = lax.broadcasted_iota(I32, k.shape, 1)
    zero = jnp.zeros_like(k)
    k_half = (jnp.where(lane < A_HD, k, zero), jnp.where(lane >= A_HD, k, zero))
    v_half = (jnp.where(lane < A_HD, v, zero), jnp.where(lane >= A_HD, v, zero))
    lane_q = lax.broadcasted_iota(I32, (blk, LANES), 1)
    for u in range(nqb):
        rows = slice(u * blk, (u + 1) * blk)
        keys = slice(u * blk, (u + 2) * blk)
        table = jnp.where(first, 0, 1) if u == 0 else 1
        v_stack = jnp.concatenate([v_half[0][keys], v_half[1][keys]], axis=0)
        for p in range(A_GROUP):
            qs = q_ref[rows, p * LANES:(p + 1) * LANES]
            probs, inv = [], []
            for half in range(2):
                hd = HEAD_ORDER[2 * p + half]
                s = lax.dot_general(qs, k_half[half][keys], (((1,), (1,)), ((), ())), preferred_element_type=F32)
                s = s * (A_HD ** -0.5) + bias_ref[table, hd]
                sk = sink_ref[hd:hd + 1, 0:1]
                m = jnp.maximum(jnp.max(s, axis=1, keepdims=True), sk)
                e = jnp.exp(s - m)
                probs.append(e.astype(BF16))
                inv.append(1.0 / (jnp.sum(e, axis=1, keepdims=True) + jnp.exp(sk - m)))
            o = jnp.dot(jnp.concatenate(probs, axis=1), v_stack, preferred_element_type=F32)
            o_ref[rows, p * LANES:(p + 1) * LANES] = o * jnp.where(lane_q < A_HD, inv[0], inv[1])


def _swa(qa, ka, va, kmeta, vmeta, bias, sinks, batch):
    blk = WINDOW
    nqb = SWA_QBLOCKS
    nq = qa.shape[0] // (batch * blk * nqb)
    kv_cur = pl.BlockSpec((nqb * blk, LANES), lambda b, j: (b * nq + j, 0))
    kv_prev = pl.BlockSpec((blk, LANES), lambda b, j: ((b * nq + j) * nqb + jnp.where(j == 0, 0, -1), 0))
    const2 = lambda shape: pl.BlockSpec(shape, lambda b, j: (0, 0))
    return pl.pallas_call(
        _swa_kernel, out_shape=jax.ShapeDtypeStruct((qa.shape[0], A_WIDTH), F32), grid=(batch, nq),
        in_specs=[pl.BlockSpec((nqb * blk, A_WIDTH), lambda b, j: (b * nq + j, 0)),
                  kv_cur, kv_prev, kv_cur, kv_prev, const2((blk, LANES)), const2((blk, LANES)),
                  pl.BlockSpec(bias.shape, lambda b, j: (0, 0, 0, 0)),
                  const2((8, LANES))],
        out_specs=pl.BlockSpec((nqb * blk, A_WIDTH), lambda b, j: (b * nq + j, 0)),
        compiler_params=_cparams("arbitrary", "arbitrary"), name="swa_prompt",
    )(qa, ka, ka, va, va, kmeta, vmeta, bias, sinks)


def _swa_step_kernel(ck_ref, cv_ref, q_ref, k_ref, v_ref, bias_ref, sink_ref, ko_ref, vo_ref, o_ref):
    g = ck_ref.shape[0]
    lane = lax.broadcasted_iota(I32, (A_HEADS, LANES), 1)
    row = lax.broadcasted_iota(I32, (A_HEADS, LANES), 0)
    own_half = (row % 2 == 0) == (lane < A_HD)
    bias = bias_ref[...]
    sk = sink_ref[:, 0:1]
    for j in range(g):
        ko_ref[j, 0:WINDOW - 1, :] = ck_ref[j, 1:WINDOW, :]
        ko_ref[j, WINDOW - 1:WINDOW, :] = k_ref[j:j + 1, :]
        vo_ref[j, 0:WINDOW - 1, :] = cv_ref[j, 1:WINDOW, :]
        vo_ref[j, WINDOW - 1:WINDOW, :] = v_ref[j:j + 1, :]
        kk = ko_ref[j].astype(BF16)
        vv = vo_ref[j].astype(BF16)
        slabs = [q_ref[j:j + 1, p * LANES:(p + 1) * LANES] for p in range(A_GROUP)]
        q8 = jnp.concatenate([slabs[r // 2] for r in range(A_HEADS)], axis=0)
        q8 = jnp.where(own_half, q8, 0.0).astype(BF16)
        s = lax.dot_general(q8, kk, (((1,), (1,)), ((), ())), preferred_element_type=F32)
        s = s * (A_HD ** -0.5) + bias
        m = jnp.maximum(jnp.max(s, axis=1, keepdims=True), sk)
        e = jnp.exp(s - m)
        inv = 1.0 / (jnp.sum(e, axis=1, keepdims=True) + jnp.exp(sk - m))
        o8 = jnp.where(own_half, jnp.dot(e.astype(BF16), vv, preferred_element_type=F32) * inv, 0.0)
        for p in range(A_GROUP):
            o_ref[j:j + 1, p * LANES:(p + 1) * LANES] = o8[2 * p:2 * p + 1, :] + o8[2 * p + 1:2 * p + 2, :]


def _swa_step(ck, cv, q, k, v, bias_rows, sinks):
    nb = ck.shape[0]
    g = SAMPLE_GROUP
    cache = pl.BlockSpec((g, WINDOW, LANES), lambda i: (i, 0, 0))
    row = lambda w: pl.BlockSpec((g, w), lambda i: (i, 0))
    const = lambda a: pl.BlockSpec(a.shape, lambda i: (0, 0))
    return pl.pallas_call(
        _swa_step_kernel,
        out_shape=[jax.ShapeDtypeStruct(ck.shape, F32), jax.ShapeDtypeStruct(cv.shape, F32),
                   jax.ShapeDtypeStruct((nb, A_WIDTH), F32)],
        grid=(nb // g,),
        in_specs=[cache, cache, row(A_WIDTH), row(LANES), row(LANES), const(bias_rows), const(sinks)],
        out_specs=[cache, cache, row(A_WIDTH)],
        compiler_params=_cparams("arbitrary"), name="swa_step",
    )(ck, cv, q, k, v, bias_rows, sinks)


def _layer_norm(z, g, b):
    mu = jnp.mean(z, axis=1, keepdims=True)
    zc = z - mu
    var = jnp.mean(zc * zc, axis=1, keepdims=True)
    return zc * lax.rsqrt(var + LN_EPS) * g + b


def _pack_halves(x):
    w = x.shape[1] // 2
    lo = pltpu.bitcast(x[:, :w].astype(BF16).astype(F32), U32)
    hi = pltpu.bitcast(x[:, w:].astype(BF16).astype(F32), U32)
    return (lo >> 16) | (hi & jnp.uint32(0xFFFF0000))


def _unpack_halves(words):
    lo = pltpu.bitcast(words << 16, F32).astype(BF16)
    hi = pltpu.bitcast(words & jnp.uint32(0xFFFF0000), F32).astype(BF16)
    return lo, hi


def _to_token_tiles(ref, x):
    for q in range(x.shape[1] // LANES):
        ref[:, q, :] = x[:, q * LANES:(q + 1) * LANES]


def _merge_kernel(h_ref, om_ref, att_ref, x_ref, gm_ref, ga_ref, wo_ref, g1_ref, b1_ref, wr_ref, br_ref,
                  x1_ref, xp_ref, tk_ref, cnt_ref):
    @pl.when(pl.program_id(0) == 0)
    def _():
        cnt_ref[...] = jnp.zeros_like(cnt_ref)

    hm = h_ref[...] * _sigmoid(om_ref[...])
    ym = hm * lax.rsqrt(jnp.mean(hm * hm, axis=1, keepdims=True) + LN_EPS) * gm_ref[...]
    att = att_ref[...]
    ya = att * lax.rsqrt(jnp.mean(att * att, axis=1, keepdims=True) + LN_EPS) * ga_ref[...]
    mix = (jnp.dot(ym.astype(BF16), wo_ref[0:M_WIDTH, :], preferred_element_type=F32)
           + jnp.dot(ya.astype(BF16), wo_ref[M_WIDTH:, :], preferred_element_type=F32))
    x1 = _layer_norm(DN_ALPHA * x_ref[...] + mix, g1_ref[...], b1_ref[...])
    x1_ref[...] = x1
    _to_token_tiles(xp_ref, _pack_halves(x1))
    logits = jnp.dot(x1.astype(BF16), wr_ref[...], preferred_element_type=F32) + br_ref[...]
    lane = lax.broadcasted_iota(I32, logits.shape, 1).astype(F32)
    vals, idxs = [], []
    for _ in range(TOP_K):
        mx = jnp.max(logits, axis=1, keepdims=True)
        idx = jnp.min(jnp.where(logits == mx, lane, float(LANES)), axis=1, keepdims=True)
        vals.append(mx)
        idxs.append(idx)
        logits = jnp.where(lane == idx, 2.0 * NEG, logits)
    es = [jnp.exp(vk - vals[0]) for vk in vals]
    tot = es[0] + es[1] + es[2] + es[3]
    tk = jnp.zeros(logits.shape, F32)
    picked = jnp.zeros(logits.shape, F32)
    for k in range(TOP_K):
        tk = jnp.where(lane == float(k), es[k] / tot, tk)
        tk = jnp.where(lane == float(TOP_K + k), idxs[k], tk)
        picked = jnp.where(lane == idxs[k], 1.0, picked)
    tk_ref[...] = tk
    cnt_ref[...] = cnt_ref[...] + jnp.sum(picked, axis=0, keepdims=True)


def _merge(h, om, att, x, gm, ga, wo, g1, b1, wr, br, tile, name):
    t = x.shape[0]
    rows = lambda w: pl.BlockSpec((tile, w), lambda i: (i, 0))
    const = lambda a: pl.BlockSpec(a.shape, lambda i: (0, 0))
    return pl.pallas_call(
        _merge_kernel,
        out_shape=[jax.ShapeDtypeStruct((t, D_MODEL), F32), jax.ShapeDtypeStruct((t, XP_TILE, LANES), U32),
                   jax.ShapeDtypeStruct((t, LANES), F32), jax.ShapeDtypeStruct((8, LANES), F32)],
        grid=(t // tile,),
        in_specs=[rows(M_WIDTH), rows(M_WIDTH), rows(A_WIDTH), rows(D_MODEL), const(gm), const(ga), const(wo),
                  const(g1), const(b1), const(wr), const(br)],
        out_specs=[rows(D_MODEL), pl.BlockSpec((tile, XP_TILE, LANES), lambda i: (i, 0, 0)), rows(LANES),
                   pl.BlockSpec((8, LANES), lambda i: (0, 0))],
        compiler_params=_cparams("arbitrary"), name=name,
    )(h, om, att, x, gm, ga, wo, g1, b1, wr, br)


def _route_kernel(tk_ref, first_ref, strict_ref, dest_ref, next_scr):
    @pl.when(pl.program_id(0) == 0)
    def _():
        next_scr[...] = first_ref[...]

    tk = tk_ref[...]
    lane = lax.broadcasted_iota(I32, tk.shape, 1).astype(F32)
    onehots = [jnp.where(lane == tk[:, TOP_K + k:TOP_K + k + 1], 1.0, 0.0) for k in range(TOP_K)]
    tot = onehots[0] + onehots[1] + onehots[2] + onehots[3]
    row = jnp.dot(strict_ref[...], tot.astype(BF16), preferred_element_type=F32) + next_scr[0:1, :]
    out = jnp.zeros(tk.shape, F32)
    for k in range(TOP_K):
        out = jnp.where(lane == float(k), jnp.sum(onehots[k] * row, axis=1, keepdims=True), out)
    dest_ref[...] = out.astype(I32)
    next_scr[...] = next_scr[...] + jnp.sum(tot, axis=0, keepdims=True)


def _route(tk, first):
    t = tk.shape[0]
    tile = min(RANK_TILE, t)
    strict = jnp.asarray(np.tril(np.ones((tile, tile), np.float32), -1), BF16)
    return pl.pallas_call(
        _route_kernel, out_shape=jax.ShapeDtypeStruct((t, LANES), I32), grid=(t // tile,),
        in_specs=[pl.BlockSpec((tile, LANES), lambda i: (i, 0)), pl.BlockSpec((8, LANES), lambda i: (0, 0)),
                  pl.BlockSpec((tile, tile), lambda i: (0, 0))],
        out_specs=pl.BlockSpec((tile, LANES), lambda i: (i, 0)),
        scratch_shapes=[pltpu.VMEM((8, LANES), F32)],
        compiler_params=_cparams("arbitrary"), name="moe_route",
    )(tk, first, strict)


def _offsets_kernel(cnt_ref, off_ref, be_ref, nu_ref, *, tile):
    cnt = cnt_ref[...]
    nblk = jnp.floor((cnt + float(tile - 1)) * (1.0 / tile))
    r_i = lax.broadcasted_iota(I32, (LANES, LANES), 0)
    c_i = lax.broadcasted_iota(I32, (LANES, LANES), 1)
    incl = jnp.where(r_i <= c_i, 1.0, 0.0).astype(BF16)
    cum = jnp.dot(nblk.astype(BF16), incl, preferred_element_type=F32)
    off_ref[...] = (cum - nblk) * float(tile)
    rows = be_ref.shape[0]
    jb = (lax.broadcasted_iota(I32, (rows, LANES), 0) * LANES + lax.broadcasted_iota(I32, (rows, LANES), 1)).astype(F32)
    acc = jnp.zeros((rows, LANES), F32)
    for e in range(N_EXPERTS):
        acc = acc + jnp.where(jb >= cum[0:1, e:e + 1], 1.0, 0.0)
    be_ref[...] = jnp.minimum(acc, float(N_EXPERTS - 1)).astype(I32)
    nu_ref[...] = jnp.broadcast_to(cum[0:1, N_EXPERTS - 1:N_EXPERTS], nu_ref.shape).astype(I32)


def _offsets(cnt, n_blocks, tile):
    rows = -(-n_blocks // LANES)
    rows = -(-rows // 8) * 8
    return pl.pallas_call(
        functools.partial(_offsets_kernel, tile=tile),
        out_shape=[jax.ShapeDtypeStruct((8, LANES), F32), jax.ShapeDtypeStruct((rows, LANES), I32),
                   jax.ShapeDtypeStruct((8, LANES), I32)],
        name="moe_offsets",
    )(cnt)


def _dispatch_kernel(dest_ref, xp_ref, xs_in_ref, xs_ref, sem):
    del xs_in_ref
    t = xp_ref.shape[0]

    def row_copy(tok, dst):
        return pltpu.make_async_copy(xp_ref.at[pl.ds(tok, 1)], xs_ref.at[pl.ds(dst, 1)], sem)

    def issue(grp, carry):
        base = pl.multiple_of(grp * ISSUE_GROUP, ISSUE_GROUP)
        for u in range(ISSUE_GROUP):
            for k in range(TOP_K):
                row_copy(base + u, dest_ref[(base + u) * TOP_K + k]).start(priority=k % 2)
        return carry

    lax.fori_loop(0, t // ISSUE_GROUP, issue, 0)
    for k in range(TOP_K):
        pltpu.make_async_copy(xp_ref, xs_ref.at[pl.ds(0, t)], sem).wait()


def _dispatch(dest_flat, xp, xs):
    t = xp.shape[0]
    tile = min(ROW_TILE, t)
    return pl.pallas_call(
        _dispatch_kernel, out_shape=jax.ShapeDtypeStruct(xs.shape, xs.dtype), grid=(t // tile,),
        in_specs=[pl.BlockSpec((tile * TOP_K,), lambda i: (i,), memory_space=pltpu.SMEM),
                  pl.BlockSpec((tile,) + xp.shape[1:], lambda i: (i, 0, 0)),
                  pl.BlockSpec(memory_space=pl.ANY)],
        out_specs=pl.BlockSpec(memory_space=pl.ANY),
        scratch_shapes=[pltpu.SemaphoreType.DMA(())],
        input_output_aliases={2: 0},
        compiler_params=_cparams("arbitrary"), name="moe_dispatch",
    )(dest_flat, xp, xs)


def _expert_kernel(be_ref, nu_ref, xs_ref, w1_ref, b1g_ref, b1l_ref, w2_ref, b2_ref, perm_ref, ys_ref,
                   w1g_scr, w1l_scr, w2_scr, xq_scr, y_scr, sem, osem):
    j = pl.program_id(0)
    active = j < nu_ref[0]
    changed = jnp.logical_or(j == 0, be_ref[j] != be_ref[jnp.maximum(j - 1, 0)])
    half = D_MODEL // 2
    tm = y_scr.shape[0]
    slot = lax.rem(j, 2)

    def fetch(blk, slot):
        row0 = pl.multiple_of(blk * tm, tm)
        return [pltpu.make_async_copy(xs_ref.at[pl.ds(row0, tm), q, :], xq_scr.at[slot, q], sem.at[slot])
                for q in range(XP_TILE)]

    def put(blk):
        row0 = pl.multiple_of(blk * tm, tm)
        return [pltpu.make_async_copy(y_scr.at[:, q * LANES:(q + 1) * LANES], ys_ref.at[pl.ds(row0, tm), q, :], osem)
                for q in range(YS_TILE)]

    def emit(y):
        @pl.when(j > 0)
        def _():
            for cp in put(j - 1):
                cp.wait()

        y_scr[...] = _pack_halves(y)
        for cp in put(j):
            cp.start()

    @pl.when(j == 0)
    def _():
        for cp in fetch(0, 0):
            cp.start()

    @pl.when(j + 1 < nu_ref[0])
    def _():
        for cp in fetch(j + 1, 1 - slot):
            cp.start()

    @pl.when(jnp.logical_and(active, changed))
    def _():
        for c in range(2 * D_FF // 256):
            wc = w1_ref[0, :, c * 256:(c + 1) * 256].astype(BF16)
            d = jnp.dot(wc, perm_ref[...], preferred_element_type=F32).astype(BF16)
            w1g_scr[:, c * 128:(c + 1) * 128] = d[:, :128]
            w1l_scr[:, c * 128:(c + 1) * 128] = d[:, 128:]
        for c in range(D_FF // 256):
            w2_scr[c * 256:(c + 1) * 256, :] = w2_ref[0, c * 256:(c + 1) * 256, :].astype(BF16)

    @pl.when(active)
    def _():
        for cp in fetch(j, slot):
            cp.wait()
        lo, hi = _unpack_halves(jnp.concatenate([xq_scr[slot, q] for q in range(XP_TILE)], axis=1))
        y = jnp.zeros((tm, D_MODEL), F32)
        nchunk = 512
        for c in range(D_FF // nchunk):
            cs = slice(c * nchunk, (c + 1) * nchunk)
            hg = (jnp.dot(lo, w1g_scr[0:half, cs], preferred_element_type=F32)
                  + jnp.dot(hi, w1g_scr[half:, cs], preferred_element_type=F32) + b1g_ref[0, :, cs])
            hl = (jnp.dot(lo, w1l_scr[0:half, cs], preferred_element_type=F32)
                  + jnp.dot(hi, w1l_scr[half:, cs], preferred_element_type=F32) + b1l_ref[0, :, cs])
            x_glu = jnp.minimum(hg, SWIGLU_LIMIT)
            x_lin = jnp.clip(hl, -SWIGLU_LIMIT, SWIGLU_LIMIT)
            a = x_glu * _sigmoid(SWIGLU_ALPHA * x_glu) * (x_lin + 1.0)
            y = y + jnp.dot(a.astype(BF16), w2_scr[cs, :], preferred_element_type=F32)
        emit(y + b2_ref[0])

    @pl.when(jnp.logical_not(active))
    def _():
        emit(jnp.zeros((tm, D_MODEL), F32))

    @pl.when(j == pl.num_programs(0) - 1)
    def _():
        for cp in put(j):
            cp.wait()


def _experts(be, nu, xs, w1, b1g, b1l, w2, b2, perm, tile):
    n_blocks = xs.shape[0] // tile
    grid_spec = pltpu.PrefetchScalarGridSpec(
        num_scalar_prefetch=2, grid=(n_blocks,),
        in_specs=[pl.BlockSpec(memory_space=pl.ANY),
                  pl.BlockSpec((1, D_MODEL, 2 * D_FF), lambda j, be, nu: (be[j], 0, 0)),
                  pl.BlockSpec((1, 1, D_FF), lambda j, be, nu: (be[j], 0, 0)),
                  pl.BlockSpec((1, 1, D_FF), lambda j, be, nu: (be[j], 0, 0)),
                  pl.BlockSpec((1, D_FF, D_MODEL), lambda j, be, nu: (be[j], 0, 0)),
                  pl.BlockSpec((1, 1, D_MODEL), lambda j, be, nu: (be[j], 0, 0)),
                  pl.BlockSpec((256, 256), lambda j, be, nu: (0, 0))],
        out_specs=pl.BlockSpec(memory_space=pl.ANY),
        scratch_shapes=[pltpu.VMEM((D_MODEL, D_FF), BF16), pltpu.VMEM((D_MODEL, D_FF), BF16),
                        pltpu.VMEM((D_FF, D_MODEL), BF16), pltpu.VMEM((2, XP_TILE, tile, LANES), U32),
                        pltpu.VMEM((tile, YS_TILE * LANES), U32), pltpu.SemaphoreType.DMA((2,)),
                        pltpu.SemaphoreType.DMA(())])
    return pl.pallas_call(
        _expert_kernel, out_shape=jax.ShapeDtypeStruct((xs.shape[0], YS_TILE, LANES), U32), grid_spec=grid_spec,
        compiler_params=_cparams("arbitrary"), name="moe_experts",
    )(be, nu, xs, w1, b1g, b1l, w2, b2, perm)


def _combine_kernel(dest_ref, next_ref, ys_ref, tk_ref, x1_ref, g2_ref, b2_ref, out_ref, buf, sem):
    i = pl.program_id(0)
    t = x1_ref.shape[0]
    slot = lax.rem(i, 2)

    def gather(idx_ref, s):
        def issue(grp, carry):
            base = pl.multiple_of(grp * 8, 8)
            for u in range(8):
                for k in range(TOP_K):
                    pltpu.make_async_copy(ys_ref.at[idx_ref[(base + u) * TOP_K + k]],
                                          buf.at[s, k, grp, :, u, :], sem.at[s]).start(priority=k % 2)
            return carry

        lax.fori_loop(0, t // 8, issue, 0)

    @pl.when(i == 0)
    def _():
        gather(dest_ref, 0)

    @pl.when(i + 1 < pl.num_programs(0))
    def _():
        gather(next_ref, 1 - slot)

    for k in range(TOP_K):
        for u in range(8):
            pltpu.make_async_copy(ys_ref.at[pl.ds(0, t // 8)], buf.at[slot, k, :, :, u, :], sem.at[slot]).wait()
    tk = tk_ref[...]
    los, his = [], []
    for q in range(YS_TILE):
        lo = hi = None
        for k in range(TOP_K):
            words = buf[slot, k, :, q].reshape(t, LANES)
            g = tk[:, k:k + 1]
            lo_k = g * pltpu.bitcast(words << 16, F32)
            hi_k = g * pltpu.bitcast(words & jnp.uint32(0xFFFF0000), F32)
            lo = lo_k if lo is None else lo + lo_k
            hi = hi_k if hi is None else hi + hi_k
        los.append(lo)
        his.append(hi)
    ff = jnp.concatenate(los + his, axis=1)
    out_ref[...] = _layer_norm(DN_ALPHA * x1_ref[...] + ff, g2_ref[...], b2_ref[...])


def _combine(dest_flat, ys, tk, x1, g2, b2):
    t = x1.shape[0]
    tile = min(ROW_TILE, t)
    n = t // tile
    return pl.pallas_call(
        _combine_kernel, out_shape=jax.ShapeDtypeStruct((t, D_MODEL), F32), grid=(n,),
        in_specs=[pl.BlockSpec((tile * TOP_K,), lambda i: (i,), memory_space=pltpu.SMEM),
                  pl.BlockSpec((tile * TOP_K,), lambda i: (jnp.minimum(i + 1, n - 1),), memory_space=pltpu.SMEM),
                  pl.BlockSpec(memory_space=pl.ANY),
                  pl.BlockSpec((tile, LANES), lambda i: (i, 0)),
                  pl.BlockSpec((tile, D_MODEL), lambda i: (i, 0)),
                  pl.BlockSpec((1, D_MODEL), lambda i: (0, 0)),
                  pl.BlockSpec((1, D_MODEL), lambda i: (0, 0))],
        out_specs=pl.BlockSpec((tile, D_MODEL), lambda i: (i, 0)),
        scratch_shapes=[pltpu.VMEM((2, TOP_K, tile // 8, YS_TILE, 8, LANES), U32), pltpu.SemaphoreType.DMA((2,))],
        compiler_params=_cparams("arbitrary"), name="moe_combine",
    )(dest_flat, dest_flat, ys, tk, x1, g2, b2)


def _rel_bucket(dist):
    exact = REL_BUCKETS // 2
    d = np.maximum(dist, 0)
    log_b = exact + (np.log(np.maximum(d, 1).astype(np.float32) / np.float32(exact))
                     / np.float32(math.log(REL_MAX_DIST / exact)) * np.float32(REL_BUCKETS - exact)).astype(np.int32)
    return np.where(d < exact, d, np.minimum(log_b, REL_BUCKETS - 1)).astype(np.int32)


def _bias_lookup(table, bucket, valid):
    bucket = jnp.asarray(bucket)[None]
    acc = jnp.zeros((table.shape[1],) + bucket.shape[1:], F32)
    for b in range(REL_BUCKETS):
        acc = jnp.where(bucket == b, table[b].reshape((-1,) + (1,) * (bucket.ndim - 1)), acc)
    return jnp.where(jnp.asarray(valid)[None], acc, NEG)


def _bias_tables(rel_bias):
    table = rel_bias.astype(F32)
    r = np.arange(WINDOW)[:, None]
    c = np.arange(2 * WINDOW)[None, :]
    dist = r + WINDOW - c
    valid = (dist >= 0) & (dist < WINDOW)
    dist0 = np.where(c < N_META, N_META + r - c, dist)
    valid0 = np.where(c < N_META, dist0 < WINDOW, (c >= WINDOW) & valid)
    both = jnp.stack([_bias_lookup(table, _rel_bucket(dist0), valid0), _bias_lookup(table, _rel_bucket(dist), valid)])
    dist_s = WINDOW - 1 - np.arange(WINDOW)
    rows = _bias_lookup(table[:, np.asarray(HEAD_ORDER)], _rel_bucket(dist_s), np.ones_like(dist_s, bool))
    return both, rows


def _perm_heads(a, axis):
    parts = [lax.slice_in_dim(a, h * A_HD, (h + 1) * A_HD, axis=axis) for h in HEAD_ORDER]
    return jnp.concatenate(parts, axis=axis)


def _rep_rows(vec, rows=8):
    out = jnp.zeros((rows, LANES), F32)
    return out.at[:vec.shape[0], :].set(jnp.broadcast_to(vec.astype(F32)[:, None], (vec.shape[0], LANES)))


def kernel(x_prompt, x_sample, cache_swa_k, cache_swa_v, state_mlstm_C, state_mlstm_n, state_mlstm_m, meta_tokens, rel_bias, w_in, b_igate, b_fgate, attn_sinks, g_mlstm_out, g_attn_out, w_out, ln1_g, ln1_b, w_router, b_router, w_moe1, b_moe1, w_moe2, b_moe2, ln2_g, ln2_b):
    B, S, _ = x_prompt.shape
    NB = x_sample.shape[0]
    assert x_sample.shape[1] == 1 and w_in.shape[0] == 1
    assert S % PROJ_TILE == 0 and S % M_CHUNK == 0 and S % WINDOW == 0 and NB % SAMPLE_GROUP == 0
    l = 0

    pts = np.cumsum(IN_WIDTHS)[:-1].tolist()
    w_qm, w_km, w_vm, w_om, w_ig, w_fg, w_qa, w_ka, w_va = jnp.split(w_in[l], pts, axis=1)
    w_gate = jnp.pad(jnp.concatenate([w_ig, w_fg], axis=1), ((0, 0), (0, LANES - 2 * M_HEADS)))
    w_qa = _perm_heads(w_qa, 1)
    b_gate = jnp.concatenate([b_igate[l], b_fgate[l]]).astype(F32)
    brow = jnp.pad(b_gate, (0, LANES - 2 * M_HEADS))[None, :]
    bcol = b_gate[:, None]
    bf = lambda a: a.astype(BF16)
    wr_p = bf(jnp.concatenate([w_qm, w_vm, w_om, w_qa, w_ka, w_va, w_gate], axis=1))
    wt_p = bf(jnp.concatenate([w_km.T, w_ig.T, w_fg.T], axis=0))
    plan_p = ((0, 512, "plain", BF16), (512, 512, "plain", BF16), (1024, 512, "plain", F32),
              (1536, 512, "plain", BF16), (2048, 128, "plain", BF16), (2176, 128, "plain", BF16),
              (2304, 128, "gate", F32))
    tplan_p = ((0, 512, "plain", BF16), (512, 8, "gate", F32))
    wr_s = bf(jnp.concatenate([w_qm, w_km, w_vm, w_om, w_qa, w_ka, w_va, w_gate], axis=1))
    plan_s = ((0, 512, "plain", F32), (512, 512, "plain", F32), (1024, 512, "plain", F32), (1536, 512, "plain", F32),
              (2048, 512, "plain", F32), (2560, 128, "plain", F32), (2688, 128, "plain", F32), (2816, 128, "gate", F32))

    bias_tab, bias_rows = _bias_tables(rel_bias)
    sinks = _rep_rows(attn_sinks[l])
    sinks_step = _rep_rows(attn_sinks[l][np.asarray(HEAD_ORDER)])
    g_m = g_mlstm_out[l].astype(F32)[None, :]
    g_a = _perm_heads(g_attn_out[l].astype(F32), 0)[None, :]
    wo = bf(jnp.concatenate([w_out[l][:M_WIDTH], _perm_heads(w_out[l][M_WIDTH:], 0)], axis=0))
    g1, b1 = ln1_g[l].astype(F32)[None, :], ln1_b[l].astype(F32)[None, :]
    g2, b2 = ln2_g[l].astype(F32)[None, :], ln2_b[l].astype(F32)[None, :]
    w_r = bf(jnp.pad(w_router[l], ((0, 0), (0, LANES - N_EXPERTS))))
    b_r = jnp.pad(b_router[l].astype(F32), (0, LANES - N_EXPERTS), constant_values=NEG)[None, :]
    b1g = b_moe1[l][:, 0::2].astype(F32)[:, None, :]
    b1l = b_moe1[l][:, 1::2].astype(F32)[:, None, :]
    b2e = b_moe2[l].astype(F32)[:, None, :]
    pj = np.zeros((256, 256), np.float32)
    pj[2 * np.arange(128), np.arange(128)] = 1.0
    pj[2 * np.arange(128) + 1, 128 + np.arange(128)] = 1.0
    perm = jnp.asarray(pj, BF16)

    xp2 = x_prompt.reshape(B * S, D_MODEL)
    qm, vm, om, qa, ka, va, gc, kt, gr, kv_tail = _proj(
        xp2, wr_p, wt_p, brow, bcol, plan_p, tplan_p, (2048, 256), PROJ_TILE, S, "proj_prompt")
    x_meta = jnp.pad(meta_tokens.astype(F32), ((0, M_CHUNK - N_META), (0, 0)))
    qm0, vm0, _, _, ka0, va0, gc0, kt0, gr0 = _proj(
        x_meta, wr_p, wt_p, brow, bcol, plan_p, tplan_p, None, M_CHUNK, M_CHUNK, "proj_meta")
    xs2 = x_sample.reshape(NB, D_MODEL)
    qm_s, km_s, vm_s, om_s, qa_s, ka_s, va_s, gc_s = _proj(
        xs2, wr_s, wt_p, brow, bcol, plan_s, (), None, NB, NB, "proj_sample")

    zero_c = jnp.zeros((M_HEADS, M_DK, 2 * M_DV), F32)
    zero_m = jnp.zeros((8, LANES), F32)
    _, ct_meta, m_meta = _mlstm(qm0, vm0, kt0, gc0, gr0, zero_c, zero_m, 1, N_META, "mlstm_meta")
    c0 = jnp.swapaxes(ct_meta[0], 1, 2)
    h_p, ct_p, m_p = _mlstm(qm, vm, kt, gc, gr, c0, m_meta[0], B, M_CHUNK, "mlstm_prompt")
    C_p = ct_p[:, :, :M_DV, :]
    n_p = ct_p[:, :, M_DV, :]
    m_prompt = m_p[:, :M_HEADS, 0]
    m_pad = jnp.pad(state_mlstm_m[l].astype(F32), ((0, 0), (0, LANES - M_HEADS)))
    C_s, n_s, m_s, h_s = _mlstm_step(state_mlstm_C[l].astype(F32), state_mlstm_n[l].astype(F32), m_pad,
                                     gc_s, qm_s, km_s, vm_s)

    att_p = _swa(qa, ka, va, ka0, va0, bias_tab, sinks, B)
    ck = cache_swa_k[l].reshape(NB, WINDOW, LANES)
    cv = cache_swa_v[l].reshape(NB, WINDOW, LANES)
    k_new, v_new, att_s = _swa_step(ck, cv, qa_s, ka_s, va_s, bias_rows, sinks_step)

    x1_p, xpk_p, tk_p, cnt_p = _merge(h_p, om, att_p, xp2, g_m, g_a, wo, g1, b1, w_r, b_r, MERGE_TILE, "merge_prompt")
    x1_s, xpk_s, tk_s, cnt_s = _merge(h_s, om_s, att_s, xs2, g_m, g_a, wo, g1, b1, w_r, b_r, NB, "merge_sample")

    T_p = B * S
    assert T_p % RANK_TILE == 0 and T_p % ROW_TILE == 0
    n_blocks = -(-((T_p + NB) * TOP_K) // EXPERT_TILE) + N_EXPERTS
    off, be2, nu2 = _offsets(cnt_p + cnt_s, n_blocks, EXPERT_TILE)
    dest_p = _route(tk_p, off)[:, :TOP_K].reshape(-1)
    dest_s = _route(tk_s, off + cnt_p)[:, :TOP_K].reshape(-1)
    be = be2.reshape(-1)[:n_blocks]
    nu = nu2[0, :1]
    xs = jnp.zeros((n_blocks * EXPERT_TILE, XP_TILE, LANES), U32)
    xs = _dispatch(dest_p, xpk_p, xs)
    xs = _dispatch(dest_s, xpk_s, xs)
    ys = _experts(be, nu, xs, w_moe1[l], b1g, b1l, w_moe2[l], b2e, perm, EXPERT_TILE)
    y_p = _combine(dest_p, ys, tk_p, x1_p, g2, b2)
    y_s = _combine(dest_s, ys, tk_s, x1_s, g2, b2)

    kv_tail = kv_tail.reshape(B, WINDOW, 2, A_KV_HEADS, A_HD)
    dt_k, dt_v = cache_swa_k.dtype, cache_swa_v.dtype
    return (y_p.reshape(B, S, D_MODEL).astype(x_prompt.dtype), y_s.reshape(NB, 1, D_MODEL).astype(x_sample.dtype),
            kv_tail[:, :, 0][None].astype(dt_k), kv_tail[:, :, 1][None].astype(dt_v),
            C_p[None].astype(state_mlstm_C.dtype), n_p[None].astype(state_mlstm_n.dtype),
            m_prompt[None].astype(state_mlstm_m.dtype),
            k_new.reshape(1, NB, WINDOW, A_KV_HEADS, A_HD).astype(dt_k),
            v_new.reshape(1, NB, WINDOW, A_KV_HEADS, A_HD).astype(dt_v),
            C_s[None].astype(state_mlstm_C.dtype), n_s[None].astype(state_mlstm_n.dtype),
            m_s[:, :M_HEADS][None].astype(state_mlstm_m.dtype))
```

```python
import functools
import math

import numpy as np
import jax
import jax.numpy as jnp
from jax import lax
from jax.experimental import pallas as pl
from jax.experimental.pallas import tpu as pltpu

F32 = jnp.float32
BF16 = jnp.bfloat16
I32 = jnp.int32
U32 = jnp.uint32

D_MODEL = 1024
N_META = 16
M_HEADS = 4
M_DK = 128
M_DV = 128
M_WIDTH = M_HEADS * M_DV
A_HD = 64
A_HEADS = 8
A_KV_HEADS = 2
A_GROUP = A_HEADS // A_KV_HEADS
A_WIDTH = A_HEADS * A_HD
WINDOW = 128
REL_BUCKETS = 32
REL_MAX_DIST = 128
N_EXPERTS = 32
TOP_K = 4
D_FF = D_MODEL
SWIGLU_LIMIT = 7.0
SWIGLU_ALPHA = 1.702
DEPTH = 1
DN_ALPHA = (2.0 * DEPTH) ** 0.25
LN_EPS = 1e-5
IN_WIDTHS = (M_WIDTH, M_WIDTH, M_WIDTH, M_WIDTH, M_HEADS, M_HEADS, A_WIDTH, A_KV_HEADS * A_HD, A_KV_HEADS * A_HD)

LANES = 128
NEG = -1e30
VMEM_LIMIT = 56 * 1024 * 1024

M_CHUNK = 128
PROJ_TILE = 512
MERGE_TILE = 512
RANK_TILE = 512
ROW_TILE = 256
EXPERT_TILE = 512
SAMPLE_GROUP = 8
ISSUE_GROUP = 8
SWA_QBLOCKS = 2
XP_TILE = D_MODEL // 2 // LANES
YS_TILE = D_MODEL // 2 // LANES
HEAD_ORDER = (0, 4, 1, 5, 2, 6, 3, 7)


def _cparams(*sem):
    return pltpu.CompilerParams(dimension_semantics=sem, vmem_limit_bytes=VMEM_LIMIT)


def _log_sigmoid(x):
    return jnp.minimum(x, 0.0) - jnp.log1p(jnp.exp(-jnp.abs(x)))


def _sigmoid(x):
    return 1.0 / (1.0 + jnp.exp(-x))


def _proj_kernel(x_ref, wr_ref, wt_ref, brow_ref, bcol_ref, *outs, row_plan, t_plan, tail_cols):
    xb = x_ref[...].astype(BF16)
    tm = xb.shape[0]
    o = 0
    for (c0, width, kind, _) in row_plan:
        r = jnp.dot(xb, wr_ref[:, c0:c0 + width], preferred_element_type=F32)
        if kind == "gate":
            r = r + brow_ref[...]
            lane = lax.broadcasted_iota(I32, r.shape, 1)
            r = jnp.where(lane < M_HEADS, r, _log_sigmoid(r))
        outs[o][...] = r.astype(outs[o].dtype)
        o += 1
    for (r0, nrows, kind, _) in t_plan:
        r = lax.dot_general(wt_ref[r0:r0 + nrows, :], xb, (((1,), (1,)), ((), ())), preferred_element_type=F32)
        if kind == "gate":
            r = r + bcol_ref[...]
            row = lax.broadcasted_iota(I32, r.shape, 0)
            r = jnp.where(row < M_HEADS, r, _log_sigmoid(r))
        outs[o][...] = r.astype(outs[o].dtype)
        o += 1
    if tail_cols is not None:
        c0, width = tail_cols
        outs[o][...] = jnp.dot(xb[tm - WINDOW:, :], wr_ref[:, c0:c0 + width], preferred_element_type=F32)


def _proj(x, wr, wt, brow, bcol, row_plan, t_plan, tail_cols, tile, rows_per_group, name):
    t = x.shape[0]
    nt = t // tile
    out_shape, out_specs = [], []
    for (_, width, _, dt) in row_plan:
        out_shape.append(jax.ShapeDtypeStruct((t, width), dt))
        out_specs.append(pl.BlockSpec((tile, width), lambda i: (i, 0)))
    for (_, nrows, _, dt) in t_plan:
        out_shape.append(jax.ShapeDtypeStruct((nrows, t), dt))
        out_specs.append(pl.BlockSpec((nrows, tile), lambda i: (0, i)))
    if tail_cols is not None:
        tiles_per_group = rows_per_group // tile
        out_shape.append(jax.ShapeDtypeStruct((t // rows_per_group * WINDOW, tail_cols[1]), F32))
        out_specs.append(pl.BlockSpec((WINDOW, tail_cols[1]), lambda i: (i // tiles_per_group, 0)))
    kern = functools.partial(_proj_kernel, row_plan=row_plan, t_plan=t_plan, tail_cols=tail_cols)
    return pl.pallas_call(
        kern, out_shape=out_shape, grid=(nt,),
        in_specs=[pl.BlockSpec((tile, D_MODEL), lambda i: (i, 0)),
                  pl.BlockSpec(wr.shape, lambda i: (0, 0)),
                  pl.BlockSpec(wt.shape, lambda i: (0, 0)),
                  pl.BlockSpec(brow.shape, lambda i: (0, 0)),
                  pl.BlockSpec(bcol.shape, lambda i: (0, 0))],
        out_specs=out_specs, compiler_params=_cparams("arbitrary"), name=name,
    )(x, wr, wt, brow, bcol)


def _split3(a):
    hi = a.astype(BF16)
    r1 = a - hi.astype(F32)
    mid = r1.astype(BF16)
    lo = (r1 - mid.astype(F32)).astype(BF16)
    return hi, mid, lo


def _mlstm_kernel(q_ref, v_ref, kt_ref, gc_ref, gr_ref, c0_ref, m0_ref, h_ref, ct_out_ref, m_out_ref,
                  ct_scr, m_scr, *, n_valid):
    c = pl.program_id(1)
    nc = pl.num_programs(1)
    L = q_ref.shape[0]

    @pl.when(c == 0)
    def _():
        ct_scr[...] = c0_ref[...]
        m_scr[...] = m0_ref[...]

    gc = gc_ref[...]
    gr = gr_ref[...]
    if n_valid < L:
        rowc = lax.broadcasted_iota(I32, gc.shape, 0)
        lanec = lax.broadcasted_iota(I32, gc.shape, 1)
        gc = jnp.where(rowc < n_valid, gc, jnp.where(lanec < M_HEADS, NEG, 0.0))
        rowr = lax.broadcasted_iota(I32, gr.shape, 0)
        colr = lax.broadcasted_iota(I32, gr.shape, 1)
        gr = jnp.where(colr < n_valid, gr, jnp.where(rowr < M_HEADS, NEG, 0.0))
    r_i = lax.broadcasted_iota(I32, (L, L), 0)
    c_i = lax.broadcasted_iota(I32, (L, L), 1)
    causal = c_i <= r_i
    tril = jnp.where(causal, 1.0, 0.0).astype(BF16)
    triu = jnp.where(r_i <= c_i, 1.0, 0.0).astype(BF16)
    b_cols = sum(jnp.dot(tril, part, preferred_element_type=F32) for part in _split3(gc))
    b_rows = sum(jnp.dot(part, triu, preferred_element_type=F32) for part in _split3(gr))
    lane_l = lax.broadcasted_iota(I32, (L, LANES), 1)
    e0 = jnp.where(lane_l == 0, 1.0, 0.0)
    scale = M_DK ** -0.5

    for h in range(M_HEADS):
        sl = slice(h * M_DK, (h + 1) * M_DK)
        q = q_ref[:, sl]
        v = v_ref[:, sl]
        kt = kt_ref[sl, :]
        ig_c = gc[:, h:h + 1]
        b_c = b_cols[:, M_HEADS + h:M_HEADS + h + 1]
        ig_r = gr[h:h + 1, :]
        b_r = b_rows[M_HEADS + h:M_HEADS + h + 1, :]
        m_prev = m_scr[h:h + 1, 0:1]
        ct = ct_scr[h]

        d = jnp.where(causal, b_c + (ig_r - b_r), NEG)
        m_t = jnp.maximum(b_c + m_prev, jnp.max(d, axis=1, keepdims=True))
        qk = jnp.dot(q, kt, preferred_element_type=F32) * scale
        s = qk * jnp.exp(d - m_t)
        inter = jnp.dot(q, ct.astype(BF16), preferred_element_type=F32)
        v_aug = jnp.concatenate([v, e0.astype(BF16)], axis=1)
        intra = jnp.dot(s.astype(BF16), v_aug, preferred_element_type=F32)
        nd = jnp.exp(b_c + m_prev - m_t) * inter + intra
        den = nd[:, M_DV:M_DV + 1]
        h_ref[:, sl] = nd[:, :M_DV] / jnp.maximum(jnp.abs(den), jnp.exp(-m_t))

        b_last = b_c[L - 1:L, :]
        g = ig_c + b_last - b_c
        m_new = jnp.maximum(b_last + m_prev, jnp.max(g, axis=0, keepdims=True))
        a = jnp.exp(b_last + m_prev - m_new)
        wg = jnp.exp(g - m_new)
        wv = jnp.concatenate([(v.astype(F32) * wg).astype(BF16), (e0 * wg).astype(BF16)], axis=1)
        upd = jnp.dot(kt, wv, preferred_element_type=F32)
        ct_scr[h] = a * ct + upd * scale
        m_scr[h:h + 1, :] = jnp.broadcast_to(m_new, (1, LANES))

    @pl.when(c == nc - 1)
    def _():
        for h in range(M_HEADS):
            ct_out_ref[0, h] = ct_scr[h].T
        m_out_ref[0] = m_scr[...]


def _mlstm(qm, vm, kt, gc, gr, c0, m0, batch, n_valid, name):
    L = M_CHUNK
    nc = qm.shape[0] // (batch * L)
    kern = functools.partial(_mlstm_kernel, n_valid=n_valid)
    return pl.pallas_call(
        kern,
        out_shape=[jax.ShapeDtypeStruct((batch * nc * L, M_WIDTH), F32),
                   jax.ShapeDtypeStruct((batch, M_HEADS, 2 * M_DV, M_DK), F32),
                   jax.ShapeDtypeStruct((batch, 8, LANES), F32)],
        grid=(batch, nc),
        in_specs=[pl.BlockSpec((L, M_WIDTH), lambda b, c: (b * nc + c, 0)),
                  pl.BlockSpec((L, M_WIDTH), lambda b, c: (b * nc + c, 0)),
                  pl.BlockSpec((M_WIDTH, L), lambda b, c: (0, b * nc + c)),
                  pl.BlockSpec((L, LANES), lambda b, c: (b * nc + c, 0)),
                  pl.BlockSpec((8, L), lambda b, c: (0, b * nc + c)),
                  pl.BlockSpec((M_HEADS, M_DK, 2 * M_DV), lambda b, c: (0, 0, 0)),
                  pl.BlockSpec((8, LANES), lambda b, c: (0, 0))],
        out_specs=[pl.BlockSpec((L, M_WIDTH), lambda b, c: (b * nc + c, 0)),
                   pl.BlockSpec((1, M_HEADS, 2 * M_DV, M_DK), lambda b, c: (b, 0, 0, 0)),
                   pl.BlockSpec((1, 8, LANES), lambda b, c: (b, 0, 0))],
        scratch_shapes=[pltpu.VMEM((M_HEADS, M_DK, 2 * M_DV), F32), pltpu.VMEM((8, LANES), F32)],
        compiler_params=_cparams("arbitrary", "arbitrary"), name=name,
    )(qm, vm, kt, gc, gr, c0, m0)


def _outer_f32(a, b):
    ah, am, al = (t.astype(F32) for t in _split3(a))
    bh, bm, bl = (t.astype(F32) for t in _split3(b))
    z = jnp.zeros_like(ah)
    lhs = jnp.concatenate([ah, ah, ah, am, am, al, z, z], axis=0).astype(BF16)
    rhs = jnp.concatenate([bh, bm, bl, bh, bm, bh, z, z], axis=0).astype(BF16)
    return lax.dot_general(lhs, rhs, (((0,), (0,)), ((), ())), preferred_element_type=F32)


def _mlstm_step_kernel(c_ref, n_ref, m_ref, gc_ref, q_ref, k_ref, v_ref,
                       c_out_ref, n_out_ref, m_out_ref, h_ref):
    g = c_ref.shape[0]
    scale = M_DK ** -0.5
    lane_m = lax.broadcasted_iota(I32, (1, LANES), 1)
    for j in range(g):
        m_row = jnp.zeros((1, LANES), F32)
        for h in range(M_HEADS):
            sl = slice(h * M_DK, (h + 1) * M_DK)
            q = q_ref[j:j + 1, sl]
            k = k_ref[j:j + 1, sl] * scale
            v = v_ref[j:j + 1, sl]
            ig = gc_ref[j:j + 1, h:h + 1]
            lf = gc_ref[j:j + 1, M_HEADS + h:M_HEADS + h + 1]
            m = m_ref[j:j + 1, h:h + 1]
            c = c_ref[j, h]
            n = n_ref[j, h:h + 1, :]
            m_t = jnp.maximum(lf + m, ig)
            w = jnp.exp(lf + m - m_t)
            wg = jnp.exp(ig - m_t)
            s = jnp.sum(q * k, axis=1, keepdims=True) * wg
            q8 = jnp.broadcast_to(q, (8, M_DK)).astype(BF16)
            cq = lax.dot_general(q8, c.astype(BF16), (((1,), (1,)), ((), ())), preferred_element_type=F32)[0:1, :]
            den = w * jnp.sum(n * q, axis=1, keepdims=True) + s
            h_ref[j:j + 1, sl] = (w * cq + s * v) / jnp.maximum(jnp.abs(den), jnp.exp(-m_t))
            c_out_ref[j, h] = w * c + _outer_f32(wg * v, k)
            n_out_ref[j, h:h + 1, :] = w * n + wg * k
            m_row = jnp.where(lane_m == h, m_t, m_row)
        m_out_ref[j:j + 1, :] = m_row


def _mlstm_step(c, n, m_pad, gc, q, k, v):
    nb = c.shape[0]
    g = SAMPLE_GROUP
    row = lambda w: pl.BlockSpec((g, w), lambda i: (i, 0))
    return pl.pallas_call(
        _mlstm_step_kernel,
        out_shape=[jax.ShapeDtypeStruct(c.shape, F32), jax.ShapeDtypeStruct(n.shape, F32),
                   jax.ShapeDtypeStruct((nb, LANES), F32), jax.ShapeDtypeStruct((nb, M_WIDTH), F32)],
        grid=(nb // g,),
        in_specs=[pl.BlockSpec((g, M_HEADS, M_DV, M_DK), lambda i: (i, 0, 0, 0)),
                  pl.BlockSpec((g, M_HEADS, M_DK), lambda i: (i, 0, 0)),
                  row(LANES), row(LANES), row(M_WIDTH), row(M_WIDTH), row(M_WIDTH)],
        out_specs=[pl.BlockSpec((g, M_HEADS, M_DV, M_DK), lambda i: (i, 0, 0, 0)),
                   pl.BlockSpec((g, M_HEADS, M_DK), lambda i: (i, 0, 0)),
                   row(LANES), row(M_WIDTH)],
        compiler_params=_cparams("arbitrary"), name="mlstm_step",
    )(c, n, m_pad, gc, q, k, v)


def _swa_kernel(q_ref, kc_ref, kp_ref, vc_ref, vp_ref, km_ref, vm_ref, bias_ref, sink_ref, o_ref):
    j = pl.program_id(1)
    first = j == 0
    blk = WINDOW
    nqb = q_ref.shape[0] // blk
    kp = jnp.where(first, km_ref[...], kp_ref[...])
    vp = jnp.where(first, vm_ref[...], vp_ref[...])
    k = jnp.concatenate([kp, kc_ref[...]], axis=0)
    v = jnp.concatenate([vp, vc_ref[...]], axis=0)
    lane = lax.broadcasted_iota(I32, k.shape, 1)
    zero = jnp.zeros_like(k)
    k_half = (jnp.where(lane < A_HD, k, zero), jnp.where(lane >= A_HD, k, zero))
    v_half = (jnp.where(lane < A_HD, v, zero), jnp.where(lane >= A_HD, v, zero))
    lane_q = lax.broadcasted_iota(I32, (blk, LANES), 1)
    for u in range(nqb):
        rows = slice(u * blk, (u + 1) * blk)
        keys = slice(u * blk, (u + 2) * blk)
        table = jnp.where(first, 0, 1) if u == 0 else 1
        v_stack = jnp.concatenate([v_half[0][keys], v_half[1][keys]], axis=0)
        for p in range(A_GROUP):
            qs = q_ref[rows, p * LANES:(p + 1) * LANES]
            probs, inv = [], []
            for half in range(2):
                hd = HEAD_ORDER[2 * p + half]
                s = lax.dot_general(qs, k_half[half][keys], (((1,), (1,)), ((), ())), preferred_element_type=F32)
                s = s * (A_HD ** -0.5) + bias_ref[table, hd]
                sk = sink_ref[hd:hd + 1, 0:1]
                m = jnp.maximum(jnp.max(s, axis=1, keepdims=True), sk)
                e = jnp.exp(s - m)
                probs.append(e.astype(BF16))
                inv.append(1.0 / (jnp.sum(e, axis=1, keepdims=True) + jnp.exp(sk - m)))
            o = jnp.dot(jnp.concatenate(probs, axis=1), v_stack, preferred_element_type=F32)
            o_ref[rows, p * LANES:(p + 1) * LANES] = o * jnp.where(lane_q < A_HD, inv[0], inv[1])


def _swa(qa, ka, va, kmeta, vmeta, bias, sinks, batch):
    blk = WINDOW
    nqb = SWA_QBLOCKS
    nq = qa.shape[0] // (batch * blk * nqb)
    kv_cur = pl.BlockSpec((nqb * blk, LANES), lambda b, j: (b * nq + j, 0))
    kv_prev = pl.BlockSpec((blk, LANES), lambda b, j: ((b * nq + j) * nqb + jnp.where(j == 0, 0, -1), 0))
    const2 = lambda shape: pl.BlockSpec(shape, lambda b, j: (0, 0))
    return pl.pallas_call(
        _swa_kernel, out_shape=jax.ShapeDtypeStruct((qa.shape[0], A_WIDTH), F32), grid=(batch, nq),
        in_specs=[pl.BlockSpec((nqb * blk, A_WIDTH), lambda b, j: (b * nq + j, 0)),
                  kv_cur, kv_prev, kv_cur, kv_prev, const2((blk, LANES)), const2((blk, LANES)),
                  pl.BlockSpec(bias.shape, lambda b, j: (0, 0, 0, 0)),
                  const2((8, LANES))],
        out_specs=pl.BlockSpec((nqb * blk, A_WIDTH), lambda b, j: (b * nq + j, 0)),
        compiler_params=_cparams("arbitrary", "arbitrary"), name="swa_prompt",
    )(qa, ka, ka, va, va, kmeta, vmeta, bias, sinks)


def _swa_step_kernel(ck_ref, cv_ref, q_ref, k_ref, v_ref, bias_ref, sink_ref, ko_ref, vo_ref, o_ref):
    g = ck_ref.shape[0]
    lane = lax.broadcasted_iota(I32, (A_HEADS, LANES), 1)
    row = lax.broadcasted_iota(I32, (A_HEADS, LANES), 0)
    own_half = (row % 2 == 0) == (lane < A_HD)
    bias = bias_ref[...]
    sk = sink_ref[:, 0:1]
    for j in range(g):
        ko_ref[j, 0:WINDOW - 1, :] = ck_ref[j, 1:WINDOW, :]
        ko_ref[j, WINDOW - 1:WINDOW, :] = k_ref[j:j + 1, :]
        vo_ref[j, 0:WINDOW - 1, :] = cv_ref[j, 1:WINDOW, :]
        vo_ref[j, WINDOW - 1:WINDOW, :] = v_ref[j:j + 1, :]
        kk = ko_ref[j].astype(BF16)
        vv = vo_ref[j].astype(BF16)
        slabs = [q_ref[j:j + 1, p * LANES:(p + 1) * LANES] for p in range(A_GROUP)]
        q8 = jnp.concatenate([slabs[r // 2] for r in range(A_HEADS)], axis=0)
        q8 = jnp.where(own_half, q8, 0.0).astype(BF16)
        s = lax.dot_general(q8, kk, (((1,), (1,)), ((), ())), preferred_element_type=F32)
        s = s * (A_HD ** -0.5) + bias
        m = jnp.maximum(jnp.max(s, axis=1, keepdims=True), sk)
        e = jnp.exp(s - m)
        inv = 1.0 / (jnp.sum(e, axis=1, keepdims=True) + jnp.exp(sk - m))
        o8 = jnp.where(own_half, jnp.dot(e.astype(BF16), vv, preferred_element_type=F32) * inv, 0.0)
        for p in range(A_GROUP):
            o_ref[j:j + 1, p * LANES:(p + 1) * LANES] = o8[2 * p:2 * p + 1, :] + o8[2 * p + 1:2 * p + 2, :]


def _swa_step(ck, cv, q, k, v, bias_rows, sinks):
    nb = ck.shape[0]
    g = SAMPLE_GROUP
    cache = pl.BlockSpec((g, WINDOW, LANES), lambda i: (i, 0, 0))
    row = lambda w: pl.BlockSpec((g, w), lambda i: (i, 0))
    const = lambda a: pl.BlockSpec(a.shape, lambda i: (0, 0))
    return pl.pallas_call(
        _swa_step_kernel,
        out_shape=[jax.ShapeDtypeStruct(ck.shape, F32), jax.ShapeDtypeStruct(cv.shape, F32),
                   jax.ShapeDtypeStruct((nb, A_WIDTH), F32)],
        grid=(nb // g,),
        in_specs=[cache, cache, row(A_WIDTH), row(LANES), row(LANES), const(bias_rows), const(sinks)],
        out_specs=[cache, cache, row(A_WIDTH)],
        compiler_params=_cparams("arbitrary"), name="swa_step",
    )(ck, cv, q, k, v, bias_rows, sinks)


def _layer_norm(z, g, b):
    mu = jnp.mean(z, axis=1, keepdims=True)
    zc = z - mu
    var = jnp.mean(zc * zc, axis=1, keepdims=True)
    return zc * lax.rsqrt(var + LN_EPS) * g + b


def _pack_halves(x):
    w = x.shape[1] // 2
    lo = pltpu.bitcast(x[:, :w].astype(BF16).astype(F32), U32)
    hi = pltpu.bitcast(x[:, w:].astype(BF16).astype(F32), U32)
    return (lo >> 16) | (hi & jnp.uint32(0xFFFF0000))


def _unpack_halves(words):
    lo = pltpu.bitcast(words << 16, F32).astype(BF16)
    hi = pltpu.bitcast(words & jnp.uint32(0xFFFF0000), F32).astype(BF16)
    return lo, hi


def _to_token_tiles(ref, x):
    for q in range(x.shape[1] // LANES):
        ref[:, q, :] = x[:, q * LANES:(q + 1) * LANES]


def _merge_kernel(h_ref, om_ref, att_ref, x_ref, gm_ref, ga_ref, wo_ref, g1_ref, b1_ref, wr_ref, br_ref,
                  x1_ref, xp_ref, tk_ref, cnt_ref):
    @pl.when(pl.program_id(0) == 0)
    def _():
        cnt_ref[...] = jnp.zeros_like(cnt_ref)

    hm = h_ref[...] * _sigmoid(om_ref[...])
    ym = hm * lax.rsqrt(jnp.mean(hm * hm, axis=1, keepdims=True) + LN_EPS) * gm_ref[...]
    att = att_ref[...]
    ya = att * lax.rsqrt(jnp.mean(att * att, axis=1, keepdims=True) + LN_EPS) * ga_ref[...]
    mix = (jnp.dot(ym.astype(BF16), wo_ref[0:M_WIDTH, :], preferred_element_type=F32)
           + jnp.dot(ya.astype(BF16), wo_ref[M_WIDTH:, :], preferred_element_type=F32))
    x1 = _layer_norm(DN_ALPHA * x_ref[...] + mix, g1_ref[...], b1_ref[...])
    x1_ref[...] = x1
    _to_token_tiles(xp_ref, _pack_halves(x1))
    logits = jnp.dot(x1.astype(BF16), wr_ref[...], preferred_element_type=F32) + br_ref[...]
    lane = lax.broadcasted_iota(I32, logits.shape, 1).astype(F32)
    vals, idxs = [], []
    for _ in range(TOP_K):
        mx = jnp.max(logits, axis=1, keepdims=True)
        idx = jnp.min(jnp.where(logits == mx, lane, float(LANES)), axis=1, keepdims=True)
        vals.append(mx)
        idxs.append(idx)
        logits = jnp.where(lane == idx, 2.0 * NEG, logits)
    es = [jnp.exp(vk - vals[0]) for vk in vals]
    tot = es[0] + es[1] + es[2] + es[3]
    tk = jnp.zeros(logits.shape, F32)
    picked = jnp.zeros(logits.shape, F32)
    for k in range(TOP_K):
        tk = jnp.where(lane == float(k), es[k] / tot, tk)
        tk = jnp.where(lane == float(TOP_K + k), idxs[k], tk)
        picked = jnp.where(lane == idxs[k], 1.0, picked)
    tk_ref[...] = tk
    cnt_ref[...] = cnt_ref[...] + jnp.sum(picked, axis=0, keepdims=True)


def _merge(h, om, att, x, gm, ga, wo, g1, b1, wr, br, tile, name):
    t = x.shape[0]
    rows = lambda w: pl.BlockSpec((tile, w), lambda i: (i, 0))
    const = lambda a: pl.BlockSpec(a.shape, lambda i: (0, 0))
    return pl.pallas_call(
        _merge_kernel,
        out_shape=[jax.ShapeDtypeStruct((t, D_MODEL), F32), jax.ShapeDtypeStruct((t, XP_TILE, LANES), U32),
                   jax.ShapeDtypeStruct((t, LANES), F32), jax.ShapeDtypeStruct((8, LANES), F32)],
        grid=(t // tile,),
        in_specs=[rows(M_WIDTH), rows(M_WIDTH), rows(A_WIDTH), rows(D_MODEL), const(gm), const(ga), const(wo),
                  const(g1), const(b1), const(wr), const(br)],
        out_specs=[rows(D_MODEL), pl.BlockSpec((tile, XP_TILE, LANES), lambda i: (i, 0, 0)), rows(LANES),
                   pl.BlockSpec((8, LANES), lambda i: (0, 0))],
        compiler_params=_cparams("arbitrary"), name=name,
    )(h, om, att, x, gm, ga, wo, g1, b1, wr, br)


def _route_kernel(tk_ref, first_ref, strict_ref, dest_ref, next_scr):
    @pl.when(pl.program_id(0) == 0)
    def _():
        next_scr[...] = first_ref[...]

    tk = tk_ref[...]
    lane = lax.broadcasted_iota(I32, tk.shape, 1).astype(F32)
    onehots = [jnp.where(lane == tk[:, TOP_K + k:TOP_K + k + 1], 1.0, 0.0) for k in range(TOP_K)]
    tot = onehots[0] + onehots[1] + onehots[2] + onehots[3]
    row = jnp.dot(strict_ref[...], tot.astype(BF16), preferred_element_type=F32) + next_scr[0:1, :]
    out = jnp.zeros(tk.shape, F32)
    for k in range(TOP_K):
        out = jnp.where(lane == float(k), jnp.sum(onehots[k] * row, axis=1, keepdims=True), out)
    dest_ref[...] = out.astype(I32)
    next_scr[...] = next_scr[...] + jnp.sum(tot, axis=0, keepdims=True)


def _route(tk, first):
    t = tk.shape[0]
    tile = min(RANK_TILE, t)
    strict = jnp.asarray(np.tril(np.ones((tile, tile), np.float32), -1), BF16)
    return pl.pallas_call(
        _route_kernel, out_shape=jax.ShapeDtypeStruct((t, LANES), I32), grid=(t // tile,),
        in_specs=[pl.BlockSpec((tile, LANES), lambda i: (i, 0)), pl.BlockSpec((8, LANES), lambda i: (0, 0)),
                  pl.BlockSpec((tile, tile), lambda i: (0, 0))],
        out_specs=pl.BlockSpec((tile, LANES), lambda i: (i, 0)),
        scratch_shapes=[pltpu.VMEM((8, LANES), F32)],
        compiler_params=_cparams("arbitrary"), name="moe_route",
    )(tk, first, strict)


def _offsets_kernel(cnt_ref, off_ref, be_ref, nu_ref, pad_ref, *, tile):
    cnt = cnt_ref[...]
    nblk = jnp.floor((cnt + float(tile - 1)) * (1.0 / tile))
    r_i = lax.broadcasted_iota(I32, (LANES, LANES), 0)
    c_i = lax.broadcasted_iota(I32, (LANES, LANES), 1)
    incl = jnp.where(r_i <= c_i, 1.0, 0.0).astype(BF16)
    cum = jnp.dot(nblk.astype(BF16), incl, preferred_element_type=F32)
    off = (cum - nblk) * float(tile)
    off_ref[...] = off
    which = lax.broadcasted_iota(I32, cnt.shape, 0)
    pad_ref[...] = jnp.where(which == 0, off + cnt, jnp.where(which == 1, nblk * float(tile) - cnt, 0.0)).astype(I32)
    rows = be_ref.shape[0]
    jb = (lax.broadcasted_iota(I32, (rows, LANES), 0) * LANES + lax.broadcasted_iota(I32, (rows, LANES), 1)).astype(F32)
    acc = jnp.zeros((rows, LANES), F32)
    for e in range(N_EXPERTS):
        acc = acc + jnp.where(jb >= cum[0:1, e:e + 1], 1.0, 0.0)
    be_ref[...] = jnp.minimum(acc, float(N_EXPERTS - 1)).astype(I32)
    nu_ref[...] = jnp.broadcast_to(cum[0:1, N_EXPERTS - 1:N_EXPERTS], nu_ref.shape).astype(I32)


def _offsets(cnt, n_blocks, tile):
    rows = -(-n_blocks // LANES)
    rows = -(-rows // 8) * 8
    return pl.pallas_call(
        functools.partial(_offsets_kernel, tile=tile),
        out_shape=[jax.ShapeDtypeStruct((8, LANES), F32), jax.ShapeDtypeStruct((rows, LANES), I32),
                   jax.ShapeDtypeStruct((8, LANES), I32), jax.ShapeDtypeStruct((8, LANES), I32)],
        name="moe_offsets",
    )(cnt)


def _scatter_rows(dest_ref, xp_ref, xs_ref, sem):
    t = xp_ref.shape[0]

    def row_copy(tok, dst):
        return pltpu.make_async_copy(xp_ref.at[pl.ds(tok, 1)], xs_ref.at[pl.ds(dst, 1)], sem)

    def issue(grp, carry):
        base = pl.multiple_of(grp * ISSUE_GROUP, ISSUE_GROUP)
        for u in range(ISSUE_GROUP):
            for k in range(TOP_K):
                row_copy(base + u, dest_ref[(base + u) * TOP_K + k]).start(priority=k % 2)
        return carry

    lax.fori_loop(0, t // ISSUE_GROUP, issue, 0)
    for k in range(TOP_K):
        pltpu.make_async_copy(xp_ref, xs_ref.at[pl.ds(0, t)], sem).wait()


def _dispatch_kernel(dest_ref, xp_ref, xs_in_ref, xs_ref, sem):
    del xs_in_ref
    _scatter_rows(dest_ref, xp_ref, xs_ref, sem)


def _dispatch_first_kernel(dest_ref, pads_ref, xp_ref, xs_ref, sem, zsem, zbuf, *, block_rows):
    zr = zbuf.shape[0]
    n_blocks = xs_ref.shape[0] // block_rows

    @pl.when(pl.program_id(0) == 0)
    def _():
        zbuf[...] = jnp.zeros_like(zbuf)
        used = pads_ref[2 * N_EXPERTS]

        def pieces(e, act):
            start, n = pads_ref[e], pads_ref[N_EXPERTS + e]
            for sh in range(zr.bit_length() - 1, -1, -1):
                b = 1 << sh
                before = lax.shift_left(lax.shift_right_logical(n, sh + 1), sh + 1)

                @pl.when((n & b) != 0)
                def _():
                    act(pltpu.make_async_copy(zbuf.at[pl.ds(0, b)], xs_ref.at[pl.ds(start + before, b)], zsem))

        def tail(jb, act):
            for h in range(block_rows // zr):
                act(pltpu.make_async_copy(zbuf, xs_ref.at[pl.ds(jb * block_rows + h * zr, zr)], zsem))

        for act in (lambda cp: cp.start(), lambda cp: cp.wait()):
            lax.fori_loop(0, N_EXPERTS, lambda e, c: (pieces(e, act), c)[1], 0)
            lax.fori_loop(used, n_blocks, lambda jb, c: (tail(jb, act), c)[1], 0)

    _scatter_rows(dest_ref, xp_ref, xs_ref, sem)


def _dispatch(dest_flat, xp, xs):
    t = xp.shape[0]
    tile = min(ROW_TILE, t)
    return pl.pallas_call(
        _dispatch_kernel, out_shape=jax.ShapeDtypeStruct(xs.shape, xs.dtype), grid=(t // tile,),
        in_specs=[pl.BlockSpec((tile * TOP_K,), lambda i: (i,), memory_space=pltpu.SMEM),
                  pl.BlockSpec((tile,) + xp.shape[1:], lambda i: (i, 0, 0)),
                  pl.BlockSpec(memory_space=pl.ANY)],
        out_specs=pl.BlockSpec(memory_space=pl.ANY),
        scratch_shapes=[pltpu.SemaphoreType.DMA(())],
        input_output_aliases={2: 0},
        compiler_params=_cparams("arbitrary"), name="moe_dispatch",
    )(dest_flat, xp, xs)


def _dispatch_first(dest_flat, pads, xp, n_rows, block_rows):
    t = xp.shape[0]
    tile = min(ROW_TILE, t)
    return pl.pallas_call(
        functools.partial(_dispatch_first_kernel, block_rows=block_rows),
        out_shape=jax.ShapeDtypeStruct((n_rows,) + xp.shape[1:], xp.dtype), grid=(t // tile,),
        in_specs=[pl.BlockSpec((tile * TOP_K,), lambda i: (i,), memory_space=pltpu.SMEM),
                  pl.BlockSpec(memory_space=pltpu.SMEM),
                  pl.BlockSpec((tile,) + xp.shape[1:], lambda i: (i, 0, 0))],
        out_specs=pl.BlockSpec(memory_space=pl.ANY),
        scratch_shapes=[pltpu.SemaphoreType.DMA(()), pltpu.SemaphoreType.DMA(()),
                        pltpu.VMEM((EXPERT_TILE // 2,) + xp.shape[1:], xp.dtype)],
        compiler_params=_cparams("arbitrary"), name="moe_dispatch_first",
    )(dest_flat, pads, xp)


def _expert_kernel(be_ref, nu_ref, xs_ref, w1_ref, b1g_ref, b1l_ref, w2_ref, b2_ref, perm_ref, ys_ref,
                   w1g_scr, w1l_scr, w2_scr, xq_scr, y_scr, sem, osem):
    j = pl.program_id(0)
    active = j < nu_ref[0]
    changed = jnp.logical_or(j == 0, be_ref[j] != be_ref[jnp.maximum(j - 1, 0)])
    tm = y_scr.shape[0]
    slot = lax.rem(j, 2)

    def fetch(blk, slot):
        row0 = pl.multiple_of(blk * tm, tm)
        return [pltpu.make_async_copy(xs_ref.at[pl.ds(row0, tm), q, :], xq_scr.at[slot, q], sem.at[slot])
                for q in range(XP_TILE)]

    def put(blk):
        row0 = pl.multiple_of(blk * tm, tm)
        return [pltpu.make_async_copy(y_scr.at[:, q * LANES:(q + 1) * LANES], ys_ref.at[pl.ds(row0, tm), q, :], osem)
                for q in range(YS_TILE)]

    def emit(y):
        @pl.when(j > 0)
        def _():
            for cp in put(j - 1):
                cp.wait()

        y_scr[...] = _pack_halves(y)
        for cp in put(j):
            cp.start()

    @pl.when(j == 0)
    def _():
        for cp in fetch(0, 0):
            cp.start()

    @pl.when(j + 1 < nu_ref[0])
    def _():
        for cp in fetch(j + 1, 1 - slot):
            cp.start()

    @pl.when(jnp.logical_and(active, changed))
    def _():
        for c in range(2 * D_FF // 256):
            wc = w1_ref[0, :, c * 256:(c + 1) * 256].astype(BF16)
            d = jnp.dot(wc, perm_ref[...], preferred_element_type=F32).astype(BF16)
            w1g_scr[:, c * 128:(c + 1) * 128] = d[:, :128]
            w1l_scr[:, c * 128:(c + 1) * 128] = d[:, 128:]
        for c in range(D_FF // 256):
            w2_scr[c * 256:(c + 1) * 256, :] = w2_ref[0, c * 256:(c + 1) * 256, :].astype(BF16)

    @pl.when(active)
    def _():
        for cp in fetch(j, slot):
            cp.wait()
        lo, hi = _unpack_halves(jnp.concatenate([xq_scr[slot, q] for q in range(XP_TILE)], axis=1))
        xb = jnp.concatenate([lo, hi], axis=1)
        hg = jnp.dot(xb, w1g_scr[...], preferred_element_type=F32) + b1g_ref[0]
        hl = jnp.dot(xb, w1l_scr[...], preferred_element_type=F32) + b1l_ref[0]
        x_glu = jnp.minimum(hg, SWIGLU_LIMIT)
        x_lin = jnp.clip(hl, -SWIGLU_LIMIT, SWIGLU_LIMIT)
        a = x_glu * _sigmoid(SWIGLU_ALPHA * x_glu) * (x_lin + 1.0)
        emit(jnp.dot(a.astype(BF16), w2_scr[...], preferred_element_type=F32) + b2_ref[0])

    @pl.when(jnp.logical_not(active))
    def _():
        emit(jnp.zeros((tm, D_MODEL), F32))

    @pl.when(j == pl.num_programs(0) - 1)
    def _():
        for cp in put(j):
            cp.wait()


def _experts(be, nu, xs, w1, b1g, b1l, w2, b2, perm, tile):
    n_blocks = xs.shape[0] // tile
    grid_spec = pltpu.PrefetchScalarGridSpec(
        num_scalar_prefetch=2, grid=(n_blocks,),
        in_specs=[pl.BlockSpec(memory_space=pl.ANY),
                  pl.BlockSpec((1, D_MODEL, 2 * D_FF), lambda j, be, nu: (be[j], 0, 0)),
                  pl.BlockSpec((1, 1, D_FF), lambda j, be, nu: (be[j], 0, 0)),
                  pl.BlockSpec((1, 1, D_FF), lambda j, be, nu: (be[j], 0, 0)),
                  pl.BlockSpec((1, D_FF, D_MODEL), lambda j, be, nu: (be[j], 0, 0)),
                  pl.BlockSpec((1, 1, D_MODEL), lambda j, be, nu: (be[j], 0, 0)),
                  pl.BlockSpec((256, 256), lambda j, be, nu: (0, 0))],
        out_specs=pl.BlockSpec(memory_space=pl.ANY),
        scratch_shapes=[pltpu.VMEM((D_MODEL, D_FF), BF16), pltpu.VMEM((D_MODEL, D_FF), BF16),
                        pltpu.VMEM((D_FF, D_MODEL), BF16), pltpu.VMEM((2, XP_TILE, tile, LANES), U32),
                        pltpu.VMEM((tile, YS_TILE * LANES), U32), pltpu.SemaphoreType.DMA((2,)),
                        pltpu.SemaphoreType.DMA(())])
    return pl.pallas_call(
        _expert_kernel, out_shape=jax.ShapeDtypeStruct((xs.shape[0], YS_TILE, LANES), U32), grid_spec=grid_spec,
        compiler_params=_cparams("arbitrary"), name="moe_experts",
    )(be, nu, xs, w1, b1g, b1l, w2, b2, perm)


def _combine_kernel(dest_ref, next_ref, ys_ref, tk_ref, x1_ref, g2_ref, b2_ref, out_ref, buf, sem):
    i = pl.program_id(0)
    t = x1_ref.shape[0]
    slot = lax.rem(i, 2)

    def gather(idx_ref, s):
        def issue(grp, carry):
            base = pl.multiple_of(grp * 8, 8)
            for u in range(8):
                for k in range(TOP_K):
                    pltpu.make_async_copy(ys_ref.at[idx_ref[(base + u) * TOP_K + k]],
                                          buf.at[s, k, grp, :, u, :], sem.at[s]).start(priority=k % 2)
            return carry

        lax.fori_loop(0, t // 8, issue, 0)

    @pl.when(i == 0)
    def _():
        gather(dest_ref, 0)

    @pl.when(i + 1 < pl.num_programs(0))
    def _():
        gather(next_ref, 1 - slot)

    for k in range(TOP_K):
        for u in range(8):
            pltpu.make_async_copy(ys_ref.at[pl.ds(0, t // 8)], buf.at[slot, k, :, :, u, :], sem.at[slot]).wait()
    tk = tk_ref[...]
    los, his = [], []
    for q in range(YS_TILE):
        lo = hi = None
        for k in range(TOP_K):
            words = buf[slot, k, :, q].reshape(t, LANES)
            g = tk[:, k:k + 1]
            lo_k = g * pltpu.bitcast(words << 16, F32)
            hi_k = g * pltpu.bitcast(words & jnp.uint32(0xFFFF0000), F32)
            lo = lo_k if lo is None else lo + lo_k
            hi = hi_k if hi is None else hi + hi_k
        los.append(lo)
        his.append(hi)
    ff = jnp.concatenate(los + his, axis=1)
    out_ref[...] = _layer_norm(DN_ALPHA * x1_ref[...] + ff, g2_ref[...], b2_ref[...])


def _combine(dest_flat, ys, tk, x1, g2, b2):
    t = x1.shape[0]
    tile = min(ROW_TILE, t)
    n = t // tile
    return pl.pallas_call(
        _combine_kernel, out_shape=jax.ShapeDtypeStruct((t, D_MODEL), F32), grid=(n,),
        in_specs=[pl.BlockSpec((tile * TOP_K,), lambda i: (i,), memory_space=pltpu.SMEM),
                  pl.BlockSpec((tile * TOP_K,), lambda i: (jnp.minimum(i + 1, n - 1),), memory_space=pltpu.SMEM),
                  pl.BlockSpec(memory_space=pl.ANY),
                  pl.BlockSpec((tile, LANES), lambda i: (i, 0)),
                  pl.BlockSpec((tile, D_MODEL), lambda i: (i, 0)),
                  pl.BlockSpec((1, D_MODEL), lambda i: (0, 0)),
                  pl.BlockSpec((1, D_MODEL), lambda i: (0, 0))],
        out_specs=pl.BlockSpec((tile, D_MODEL), lambda i: (i, 0)),
        scratch_shapes=[pltpu.VMEM((2, TOP_K, tile // 8, YS_TILE, 8, LANES), U32), pltpu.SemaphoreType.DMA((2,))],
        compiler_params=_cparams("arbitrary"), name="moe_combine",
    )(dest_flat, dest_flat, ys, tk, x1, g2, b2)


def _rel_bucket(dist):
    exact = REL_BUCKETS // 2
    d = np.maximum(dist, 0)
    log_b = exact + (np.log(np.maximum(d, 1).astype(np.float32) / np.float32(exact))
                     / np.float32(math.log(REL_MAX_DIST / exact)) * np.float32(REL_BUCKETS - exact)).astype(np.int32)
    return np.where(d < exact, d, np.minimum(log_b, REL_BUCKETS - 1)).astype(np.int32)


def _bias_lookup(table, bucket, valid):
    bucket = jnp.asarray(bucket)[None]
    acc = jnp.zeros((table.shape[1],) + bucket.shape[1:], F32)
    for b in range(REL_BUCKETS):
        acc = jnp.where(bucket == b, table[b].reshape((-1,) + (1,) * (bucket.ndim - 1)), acc)
    return jnp.where(jnp.asarray(valid)[None], acc, NEG)


def _bias_tables(rel_bias):
    table = rel_bias.astype(F32)
    r = np.arange(WINDOW)[:, None]
    c = np.arange(2 * WINDOW)[None, :]
    dist = r + WINDOW - c
    valid = (dist >= 0) & (dist < WINDOW)
    dist0 = np.where(c < N_META, N_META + r - c, dist)
    valid0 = np.where(c < N_META, dist0 < WINDOW, (c >= WINDOW) & valid)
    both = jnp.stack([_bias_lookup(table, _rel_bucket(dist0), valid0), _bias_lookup(table, _rel_bucket(dist), valid)])
    dist_s = WINDOW - 1 - np.arange(WINDOW)
    rows = _bias_lookup(table[:, np.asarray(HEAD_ORDER)], _rel_bucket(dist_s), np.ones_like(dist_s, bool))
    return both, rows


def _perm_heads(a, axis):
    assert HEAD_ORDER == tuple(kv * A_GROUP + g for g in range(A_GROUP) for kv in range(A_KV_HEADS))
    shape = a.shape
    a = a.reshape(shape[:axis] + (A_KV_HEADS, A_GROUP, A_HD) + shape[axis + 1:])
    return jnp.swapaxes(a, axis, axis + 1).reshape(shape)


def _rep_rows(vec, rows=8):
    out = jnp.zeros((rows, LANES), F32)
    return out.at[:vec.shape[0], :].set(jnp.broadcast_to(vec.astype(F32)[:, None], (vec.shape[0], LANES)))


def kernel(x_prompt, x_sample, cache_swa_k, cache_swa_v, state_mlstm_C, state_mlstm_n, state_mlstm_m, meta_tokens, rel_bias, w_in, b_igate, b_fgate, attn_sinks, g_mlstm_out, g_attn_out, w_out, ln1_g, ln1_b, w_router, b_router, w_moe1, b_moe1, w_moe2, b_moe2, ln2_g, ln2_b):
    B, S, _ = x_prompt.shape
    NB = x_sample.shape[0]
    assert x_sample.shape[1] == 1 and w_in.shape[0] == 1
    assert S % PROJ_TILE == 0 and S % M_CHUNK == 0 and S % WINDOW == 0 and NB % SAMPLE_GROUP == 0
    l = 0

    assert IN_WIDTHS == (512, 512, 512, 512, 4, 4, 512, 128, 128)
    bf = lambda a: a.astype(BF16)
    w = w_in[l]
    n_main, n_gate = 4 * M_WIDTH, 2 * M_HEADS
    w_gate = w[:, n_main:n_main + n_gate]
    w_att = w[:, n_main + n_gate:]
    wr = bf(jnp.concatenate([w[:, :n_main], _perm_heads(w_att[:, :A_WIDTH], 1), w_att[:, A_WIDTH:],
                             jnp.pad(w_gate, ((0, 0), (0, LANES - n_gate)))], axis=1))
    wt = bf(jnp.concatenate([w[:, M_WIDTH:2 * M_WIDTH], w_gate], axis=1).T)
    b_gate = jnp.concatenate([b_igate[l], b_fgate[l]]).astype(F32)
    brow = jnp.pad(b_gate, (0, LANES - n_gate))[None, :]
    bcol = b_gate[:, None]
    plan_p = ((0, 512, "plain", BF16), (1024, 512, "plain", BF16), (1536, 512, "plain", F32),
              (2048, 512, "plain", BF16), (2560, 128, "plain", BF16), (2688, 128, "plain", BF16),
              (2816, 128, "gate", F32))
    tplan_p = ((0, 512, "plain", BF16), (512, 8, "gate", F32))
    plan_s = ((0, 512, "plain", F32), (512, 512, "plain", F32), (1024, 512, "plain", F32), (1536, 512, "plain", F32),
              (2048, 512, "plain", F32), (2560, 128, "plain", F32), (2688, 128, "plain", F32), (2816, 128, "gate", F32))

    bias_tab, bias_rows = _bias_tables(rel_bias)
    sinks = _rep_rows(attn_sinks[l])
    sinks_step = _rep_rows(attn_sinks[l][np.asarray(HEAD_ORDER)])
    g_m = g_mlstm_out[l].astype(F32)[None, :]
    g_a = _perm_heads(g_attn_out[l].astype(F32), 0)[None, :]
    wo = bf(jnp.concatenate([w_out[l][:M_WIDTH], _perm_heads(w_out[l][M_WIDTH:], 0)], axis=0))
    g1, b1 = ln1_g[l].astype(F32)[None, :], ln1_b[l].astype(F32)[None, :]
    g2, b2 = ln2_g[l].astype(F32)[None, :], ln2_b[l].astype(F32)[None, :]
    w_r = bf(jnp.pad(w_router[l], ((0, 0), (0, LANES - N_EXPERTS))))
    b_r = jnp.pad(b_router[l].astype(F32), (0, LANES - N_EXPERTS), constant_values=NEG)[None, :]
    b1g = b_moe1[l][:, 0::2].astype(F32)[:, None, :]
    b1l = b_moe1[l][:, 1::2].astype(F32)[:, None, :]
    b2e = b_moe2[l].astype(F32)[:, None, :]
    pj = np.zeros((256, 256), np.float32)
    pj[2 * np.arange(128), np.arange(128)] = 1.0
    pj[2 * np.arange(128) + 1, 128 + np.arange(128)] = 1.0
    perm = jnp.asarray(pj, BF16)

    xp2 = x_prompt.reshape(B * S, D_MODEL)
    qm, vm, om, qa, ka, va, gc, kt, gr, kv_tail = _proj(
        xp2, wr, wt, brow, bcol, plan_p, tplan_p, (2560, 256), PROJ_TILE, S, "proj_prompt")
    x_meta = jnp.pad(meta_tokens.astype(F32), ((0, M_CHUNK - N_META), (0, 0)))
    qm0, vm0, _, _, ka0, va0, gc0, kt0, gr0 = _proj(
        x_meta, wr, wt, brow, bcol, plan_p, tplan_p, None, M_CHUNK, M_CHUNK, "proj_meta")
    xs2 = x_sample.reshape(NB, D_MODEL)
    qm_s, km_s, vm_s, om_s, qa_s, ka_s, va_s, gc_s = _proj(
        xs2, wr, wt, brow, bcol, plan_s, (), None, NB, NB, "proj_sample")

    zero_c = jnp.zeros((M_HEADS, M_DK, 2 * M_DV), F32)
    zero_m = jnp.zeros((8, LANES), F32)
    _, ct_meta, m_meta = _mlstm(qm0, vm0, kt0, gc0, gr0, zero_c, zero_m, 1, N_META, "mlstm_meta")
    c0 = jnp.swapaxes(ct_meta[0], 1, 2)
    h_p, ct_p, m_p = _mlstm(qm, vm, kt, gc, gr, c0, m_meta[0], B, M_CHUNK, "mlstm_prompt")
    C_p = ct_p[:, :, :M_DV, :]
    n_p = ct_p[:, :, M_DV, :]
    m_prompt = m_p[:, :M_HEADS, 0]
    m_pad = jnp.pad(state_mlstm_m[l].astype(F32), ((0, 0), (0, LANES - M_HEADS)))
    C_s, n_s, m_s, h_s = _mlstm_step(state_mlstm_C[l].astype(F32), state_mlstm_n[l].astype(F32), m_pad,
                                     gc_s, qm_s, km_s, vm_s)

    att_p = _swa(qa, ka, va, ka0, va0, bias_tab, sinks, B)
    ck = cache_swa_k[l].reshape(NB, WINDOW, LANES)
    cv = cache_swa_v[l].reshape(NB, WINDOW, LANES)
    k_new, v_new, att_s = _swa_step(ck, cv, qa_s, ka_s, va_s, bias_rows, sinks_step)

    x1_p, xpk_p, tk_p, cnt_p = _merge(h_p, om, att_p, xp2, g_m, g_a, wo, g1, b1, w_r, b_r, MERGE_TILE, "merge_prompt")
    x1_s, xpk_s, tk_s, cnt_s = _merge(h_s, om_s, att_s, xs2, g_m, g_a, wo, g1, b1, w_r, b_r, NB, "merge_sample")

    T_p = B * S
    assert T_p % RANK_TILE == 0 and T_p % ROW_TILE == 0
    n_blocks = -(-((T_p + NB) * TOP_K) // EXPERT_TILE) + N_EXPERTS
    off, be2, nu2, pad = _offsets(cnt_p + cnt_s, n_blocks, EXPERT_TILE)
    pads = jnp.concatenate([pad[0, :N_EXPERTS], pad[1, :N_EXPERTS], nu2[0, :1]])
    dest_p = _route(tk_p, off)[:, :TOP_K].reshape(-1)
    dest_s = _route(tk_s, off + cnt_p)[:, :TOP_K].reshape(-1)
    be = be2.reshape(-1)[:n_blocks]
    nu = nu2[0, :1]
    xs = _dispatch_first(dest_p, pads, xpk_p, n_blocks * EXPERT_TILE, EXPERT_TILE)
    xs = _dispatch(dest_s, xpk_s, xs)
    ys = _experts(be, nu, xs, w_moe1[l], b1g, b1l, w_moe2[l], b2e, perm, EXPERT_TILE)
    y_p = _combine(dest_p, ys, tk_p, x1_p, g2, b2)
    y_s = _combine(dest_s, ys, tk_s, x1_s, g2, b2)

    kv_tail = kv_tail.reshape(B, WINDOW, 2, A_KV_HEADS, A_HD)
    dt_k, dt_v = cache_swa_k.dtype, cache_swa_v.dtype
    return (y_p.reshape(B, S, D_MODEL).astype(x_prompt.dtype), y_s.reshape(NB, 1, D_MODEL).astype(x_sample.dtype),
            kv_tail[:, :, 0][None].astype(dt_k), kv_tail[:, :, 1][None].astype(dt_v),
            C_p[None].astype(state_mlstm_C.dtype), n_p[None].astype(state_mlstm_n.dtype),
            m_prompt[None].astype(state_mlstm_m.dtype),
            k_new.reshape(1, NB, WINDOW, A_KV_HEADS, A_HD).astype(dt_k),
            v_new.reshape(1, NB, WINDOW, A_KV_HEADS, A_HD).astype(dt_v),
            C_s[None].astype(state_mlstm_C.dtype), n_s[None].astype(state_mlstm_n.dtype),
            m_s[:, :M_HEADS][None].astype(state_mlstm_m.dtype))
```

```python
import functools
import math

import numpy as np
import jax
import jax.numpy as jnp
from jax import lax
from jax.experimental import pallas as pl
from jax.experimental.pallas import tpu as pltpu

F32 = jnp.float32
BF16 = jnp.bfloat16
I32 = jnp.int32
U32 = jnp.uint32

D_MODEL = 1024
N_META = 16
M_HEADS = 4
M_DK = 128
M_DV = 128
M_WIDTH = M_HEADS * M_DV
A_HD = 64
A_HEADS = 8
A_KV_HEADS = 2
A_GROUP = A_HEADS // A_KV_HEADS
A_WIDTH = A_HEADS * A_HD
WINDOW = 128
REL_BUCKETS = 32
REL_MAX_DIST = 128
N_EXPERTS = 32
TOP_K = 4
D_FF = D_MODEL
SWIGLU_LIMIT = 7.0
SWIGLU_ALPHA = 1.702
DEPTH = 1
DN_ALPHA = (2.0 * DEPTH) ** 0.25
LN_EPS = 1e-5
IN_WIDTHS = (M_WIDTH, M_WIDTH, M_WIDTH, M_WIDTH, M_HEADS, M_HEADS, A_WIDTH, A_KV_HEADS * A_HD, A_KV_HEADS * A_HD)

LANES = 128
NEG = -1e30
VMEM_LIMIT = 56 * 1024 * 1024

M_CHUNK = 128
PROJ_TILE = 512
MERGE_TILE = 512
RANK_TILE = 512
ROW_TILE = 256
EXPERT_TILE = 512
SAMPLE_GROUP = 8
ISSUE_GROUP = 8
SWA_QBLOCKS = 2
XP_TILE = D_MODEL // 2 // LANES
YS_TILE = D_MODEL // 2 // LANES
HEAD_ORDER = (0, 4, 1, 5, 2, 6, 3, 7)


def _cparams(*sem):
    return pltpu.CompilerParams(dimension_semantics=sem, vmem_limit_bytes=VMEM_LIMIT)


def _log_sigmoid(x):
    return jnp.minimum(x, 0.0) - jnp.log1p(jnp.exp(-jnp.abs(x)))


def _sigmoid(x):
    return 1.0 / (1.0 + jnp.exp(-x))


def _proj_kernel(x_ref, wr_ref, wt_ref, brow_ref, bcol_ref, *outs, row_plan, t_plan, tail_cols):
    xb = x_ref[...].astype(BF16)
    tm = xb.shape[0]
    o = 0
    for (c0, width, kind, _) in row_plan:
        r = jnp.dot(xb, wr_ref[:, c0:c0 + width], preferred_element_type=F32)
        if kind == "gate":
            r = r + brow_ref[...]
            lane = lax.broadcasted_iota(I32, r.shape, 1)
            r = jnp.where(lane < M_HEADS, r, _log_sigmoid(r))
        outs[o][...] = r.astype(outs[o].dtype)
        o += 1
    for (r0, nrows, kind, _) in t_plan:
        r = lax.dot_general(wt_ref[r0:r0 + nrows, :], xb, (((1,), (1,)), ((), ())), preferred_element_type=F32)
        if kind == "gate":
            r = r + bcol_ref[...]
            row = lax.broadcasted_iota(I32, r.shape, 0)
            r = jnp.where(row < M_HEADS, r, _log_sigmoid(r))
        outs[o][...] = r.astype(outs[o].dtype)
        o += 1
    if tail_cols is not None:
        c0, width = tail_cols
        outs[o][...] = jnp.dot(xb[tm - WINDOW:, :], wr_ref[:, c0:c0 + width], preferred_element_type=F32)


def _proj(x, wr, wt, brow, bcol, row_plan, t_plan, tail_cols, tile, rows_per_group, name):
    t = x.shape[0]
    nt = t // tile
    out_shape, out_specs = [], []
    for (_, width, _, dt) in row_plan:
        out_shape.append(jax.ShapeDtypeStruct((t, width), dt))
        out_specs.append(pl.BlockSpec((tile, width), lambda i: (i, 0)))
    for (_, nrows, _, dt) in t_plan:
        out_shape.append(jax.ShapeDtypeStruct((nrows, t), dt))
        out_specs.append(pl.BlockSpec((nrows, tile), lambda i: (0, i)))
    if tail_cols is not None:
        tiles_per_group = rows_per_group // tile
        out_shape.append(jax.ShapeDtypeStruct((t // rows_per_group * WINDOW, tail_cols[1]), F32))
        out_specs.append(pl.BlockSpec((WINDOW, tail_cols[1]), lambda i: (i // tiles_per_group, 0)))
    kern = functools.partial(_proj_kernel, row_plan=row_plan, t_plan=t_plan, tail_cols=tail_cols)
    return pl.pallas_call(
        kern, out_shape=out_shape, grid=(nt,),
        in_specs=[pl.BlockSpec((tile, D_MODEL), lambda i: (i, 0)),
                  pl.BlockSpec(wr.shape, lambda i: (0, 0)),
                  pl.BlockSpec(wt.shape, lambda i: (0, 0)),
                  pl.BlockSpec(brow.shape, lambda i: (0, 0)),
                  pl.BlockSpec(bcol.shape, lambda i: (0, 0))],
        out_specs=out_specs, compiler_params=_cparams("arbitrary"), name=name,
    )(x, wr, wt, brow, bcol)


def _split3(a):
    hi = a.astype(BF16)
    r1 = a - hi.astype(F32)
    mid = r1.astype(BF16)
    lo = (r1 - mid.astype(F32)).astype(BF16)
    return hi, mid, lo


def _mlstm_kernel(q_ref, v_ref, kt_ref, gc_ref, gr_ref, c0_ref, m0_ref, h_ref, ct_out_ref, m_out_ref,
                  ct_scr, m_scr, *, n_valid):
    c = pl.program_id(1)
    nc = pl.num_programs(1)
    L = q_ref.shape[0]

    @pl.when(c == 0)
    def _():
        ct_scr[...] = c0_ref[...]
        m_scr[...] = m0_ref[...]

    gc = gc_ref[...]
    gr = gr_ref[...]
    if n_valid < L:
        rowc = lax.broadcasted_iota(I32, gc.shape, 0)
        lanec = lax.broadcasted_iota(I32, gc.shape, 1)
        gc = jnp.where(rowc < n_valid, gc, jnp.where(lanec < M_HEADS, NEG, 0.0))
        rowr = lax.broadcasted_iota(I32, gr.shape, 0)
        colr = lax.broadcasted_iota(I32, gr.shape, 1)
        gr = jnp.where(colr < n_valid, gr, jnp.where(rowr < M_HEADS, NEG, 0.0))
    r_i = lax.broadcasted_iota(I32, (L, L), 0)
    c_i = lax.broadcasted_iota(I32, (L, L), 1)
    causal = c_i <= r_i
    tril = jnp.where(causal, 1.0, 0.0).astype(BF16)
    triu = jnp.where(r_i <= c_i, 1.0, 0.0).astype(BF16)
    b_cols = sum(jnp.dot(tril, part, preferred_element_type=F32) for part in _split3(gc))
    b_rows = sum(jnp.dot(part, triu, preferred_element_type=F32) for part in _split3(gr))
    lane_l = lax.broadcasted_iota(I32, (L, LANES), 1)
    e0 = jnp.where(lane_l == 0, 1.0, 0.0)
    scale = M_DK ** -0.5

    for h in range(M_HEADS):
        sl = slice(h * M_DK, (h + 1) * M_DK)
        q = q_ref[:, sl]
        v = v_ref[:, sl]
        kt = kt_ref[sl, :]
        ig_c = gc[:, h:h + 1]
        b_c = b_cols[:, M_HEADS + h:M_HEADS + h + 1]
        ig_r = gr[h:h + 1, :]
        b_r = b_rows[M_HEADS + h:M_HEADS + h + 1, :]
        m_prev = m_scr[h:h + 1, 0:1]
        ct = ct_scr[h]

        d = jnp.where(causal, b_c + (ig_r - b_r), NEG)
        m_t = jnp.maximum(b_c + m_prev, jnp.max(d, axis=1, keepdims=True))
        qk = jnp.dot(q, kt, preferred_element_type=F32) * scale
        s = qk * jnp.exp(d - m_t)
        inter = jnp.dot(q, ct.astype(BF16), preferred_element_type=F32)
        v_aug = jnp.concatenate([v, e0.astype(BF16)], axis=1)
        intra = jnp.dot(s.astype(BF16), v_aug, preferred_element_type=F32)
        nd = jnp.exp(b_c + m_prev - m_t) * inter + intra
        den = nd[:, M_DV:M_DV + 1]
        h_ref[:, sl] = nd[:, :M_DV] / jnp.maximum(jnp.abs(den), jnp.exp(-m_t))

        b_last = b_c[L - 1:L, :]
        g = ig_c + b_last - b_c
        m_new = jnp.maximum(b_last + m_prev, jnp.max(g, axis=0, keepdims=True))
        a = jnp.exp(b_last + m_prev - m_new)
        wg = jnp.exp(g - m_new)
        wv = jnp.concatenate([(v.astype(F32) * wg).astype(BF16), (e0 * wg).astype(BF16)], axis=1)
        upd = jnp.dot(kt, wv, preferred_element_type=F32)
        ct_scr[h] = a * ct + upd * scale
        m_scr[h:h + 1, :] = jnp.broadcast_to(m_new, (1, LANES))

    @pl.when(c == nc - 1)
    def _():
        for h in range(M_HEADS):
            ct_out_ref[0, h] = ct_scr[h].T
        m_out_ref[0] = m_scr[...]


def _mlstm(qm, vm, kt, gc, gr, c0, m0, batch, n_valid, name):
    L = M_CHUNK
    nc = qm.shape[0] // (batch * L)
    kern = functools.partial(_mlstm_kernel, n_valid=n_valid)
    return pl.pallas_call(
        kern,
        out_shape=[jax.ShapeDtypeStruct((batch * nc * L, M_WIDTH), F32),
                   jax.ShapeDtypeStruct((batch, M_HEADS, 2 * M_DV, M_DK), F32),
                   jax.ShapeDtypeStruct((batch, 8, LANES), F32)],
        grid=(batch, nc),
        in_specs=[pl.BlockSpec((L, M_WIDTH), lambda b, c: (b * nc + c, 0)),
                  pl.BlockSpec((L, M_WIDTH), lambda b, c: (b * nc + c, 0)),
                  pl.BlockSpec((M_WIDTH, L), lambda b, c: (0, b * nc + c)),
                  pl.BlockSpec((L, LANES), lambda b, c: (b * nc + c, 0)),
                  pl.BlockSpec((8, L), lambda b, c: (0, b * nc + c)),
                  pl.BlockSpec((M_HEADS, M_DK, 2 * M_DV), lambda b, c: (0, 0, 0)),
                  pl.BlockSpec((8, LANES), lambda b, c: (0, 0))],
        out_specs=[pl.BlockSpec((L, M_WIDTH), lambda b, c: (b * nc + c, 0)),
                   pl.BlockSpec((1, M_HEADS, 2 * M_DV, M_DK), lambda b, c: (b, 0, 0, 0)),
                   pl.BlockSpec((1, 8, LANES), lambda b, c: (b, 0, 0))],
        scratch_shapes=[pltpu.VMEM((M_HEADS, M_DK, 2 * M_DV), F32), pltpu.VMEM((8, LANES), F32)],
        compiler_params=_cparams("arbitrary", "arbitrary"), name=name,
    )(qm, vm, kt, gc, gr, c0, m0)


def _outer_f32(a, b):
    ah, am, al = (t.astype(F32) for t in _split3(a))
    bh, bm, bl = (t.astype(F32) for t in _split3(b))
    z = jnp.zeros_like(ah)
    lhs = jnp.concatenate([ah, ah, ah, am, am, al, z, z], axis=0).astype(BF16)
    rhs = jnp.concatenate([bh, bm, bl, bh, bm, bh, z, z], axis=0).astype(BF16)
    return lax.dot_general(lhs, rhs, (((0,), (0,)), ((), ())), preferred_element_type=F32)


def _mlstm_step_kernel(c_ref, n_ref, m_ref, gc_ref, q_ref, k_ref, v_ref,
                       c_out_ref, n_out_ref, m_out_ref, h_ref):
    g = c_ref.shape[0]
    scale = M_DK ** -0.5
    lane_m = lax.broadcasted_iota(I32, (1, LANES), 1)
    for j in range(g):
        m_row = jnp.zeros((1, LANES), F32)
        for h in range(M_HEADS):
            sl = slice(h * M_DK, (h + 1) * M_DK)
            q = q_ref[j:j + 1, sl]
            k = k_ref[j:j + 1, sl] * scale
            v = v_ref[j:j + 1, sl]
            ig = gc_ref[j:j + 1, h:h + 1]
            lf = gc_ref[j:j + 1, M_HEADS + h:M_HEADS + h + 1]
            m = m_ref[j:j + 1, h:h + 1]
            c = c_ref[j, h]
            n = n_ref[j, h:h + 1, :]
            m_t = jnp.maximum(lf + m, ig)
            w = jnp.exp(lf + m - m_t)
            wg = jnp.exp(ig - m_t)
            s = jnp.sum(q * k, axis=1, keepdims=True) * wg
            q8 = jnp.broadcast_to(q, (8, M_DK)).astype(BF16)
            cq = lax.dot_general(q8, c.astype(BF16), (((1,), (1,)), ((), ())), preferred_element_type=F32)[0:1, :]
            den = w * jnp.sum(n * q, axis=1, keepdims=True) + s
            h_ref[j:j + 1, sl] = (w * cq + s * v) / jnp.maximum(jnp.abs(den), jnp.exp(-m_t))
            c_out_ref[j, h] = w * c + _outer_f32(wg * v, k)
            n_out_ref[j, h:h + 1, :] = w * n + wg * k
            m_row = jnp.where(lane_m == h, m_t, m_row)
        m_out_ref[j:j + 1, :] = m_row


def _mlstm_step(c, n, m_pad, gc, q, k, v):
    nb = c.shape[0]
    g = SAMPLE_GROUP
    row = lambda w: pl.BlockSpec((g, w), lambda i: (i, 0))
    return pl.pallas_call(
        _mlstm_step_kernel,
        out_shape=[jax.ShapeDtypeStruct(c.shape, F32), jax.ShapeDtypeStruct(n.shape, F32),
                   jax.ShapeDtypeStruct((nb, LANES), F32), jax.ShapeDtypeStruct((nb, M_WIDTH), F32)],
        grid=(nb // g,),
        in_specs=[pl.BlockSpec((g, M_HEADS, M_DV, M_DK), lambda i: (i, 0, 0, 0)),
                  pl.BlockSpec((g, M_HEADS, M_DK), lambda i: (i, 0, 0)),
                  row(LANES), row(LANES), row(M_WIDTH), row(M_WIDTH), row(M_WIDTH)],
        out_specs=[pl.BlockSpec((g, M_HEADS, M_DV, M_DK), lambda i: (i, 0, 0, 0)),
                   pl.BlockSpec((g, M_HEADS, M_DK), lambda i: (i, 0, 0)),
                   row(LANES), row(M_WIDTH)],
        compiler_params=_cparams("arbitrary"), name="mlstm_step",
    )(c, n, m_pad, gc, q, k, v)


def _swa_kernel(q_ref, kc_ref, kp_ref, vc_ref, vp_ref, km_ref, vm_ref, bias_ref, sink_ref, o_ref):
    j = pl.program_id(1)
    first = j == 0
    blk = WINDOW
    nqb = q_ref.shape[0] // blk
    kp = jnp.where(first, km_ref[...], kp_ref[...])
    vp = jnp.where(first, vm_ref[...], vp_ref[...])
    k = jnp.concatenate([kp, kc_ref[...]], axis=0)
    v = jnp.concatenate([vp, vc_ref[...]], axis=0)
    lane = lax.broadcasted_iota(I32, k.shape, 1)
    zero = jnp.zeros_like(k)
    k_half = (jnp.where(lane < A_HD, k, zero), jnp.where(lane >= A_HD, k, zero))
    v_half = (jnp.where(lane < A_HD, v, zero), jnp.where(lane >= A_HD, v, zero))
    lane_q = lax.broadcasted_iota(I32, (blk, LANES), 1)
    for u in range(nqb):
        rows = slice(u * blk, (u + 1) * blk)
        keys = slice(u * blk, (u + 2) * blk)
        table = jnp.where(first, 0, 1) if u == 0 else 1
        v_stack = jnp.concatenate([v_half[0][keys], v_half[1][keys]], axis=0)
        for p in range(A_GROUP):
            qs = q_ref[rows, p * LANES:(p + 1) * LANES]
            probs, inv = [], []
            for half in range(2):
                hd = HEAD_ORDER[2 * p + half]
                s = lax.dot_general(qs, k_half[half][keys], (((1,), (1,)), ((), ())), preferred_element_type=F32)
                s = s * (A_HD ** -0.5) + bias_ref[table, hd]
                sk = sink_ref[hd:hd + 1, 0:1]
                m = jnp.maximum(jnp.max(s, axis=1, keepdims=True), sk)
                e = jnp.exp(s - m)
                probs.append(e.astype(BF16))
                inv.append(1.0 / (jnp.sum(e, axis=1, keepdims=True) + jnp.exp(sk - m)))
            o = jnp.dot(jnp.concatenate(probs, axis=1), v_stack, preferred_element_type=F32)
            o_ref[rows, p * LANES:(p + 1) * LANES] = o * jnp.where(lane_q < A_HD, inv[0], inv[1])


def _swa(qa, ka, va, kmeta, vmeta, bias, sinks, batch):
    blk = WINDOW
    nqb = SWA_QBLOCKS
    nq = qa.shape[0] // (batch * blk * nqb)
    kv_cur = pl.BlockSpec((nqb * blk, LANES), lambda b, j: (b * nq + j, 0))
    kv_prev = pl.BlockSpec((blk, LANES), lambda b, j: ((b * nq + j) * nqb + jnp.where(j == 0, 0, -1), 0))
    const2 = lambda shape: pl.BlockSpec(shape, lambda b, j: (0, 0))
    return pl.pallas_call(
        _swa_kernel, out_shape=jax.ShapeDtypeStruct((qa.shape[0], A_WIDTH), F32), grid=(batch, nq),
        in_specs=[pl.BlockSpec((nqb * blk, A_WIDTH), lambda b, j: (b * nq + j, 0)),
                  kv_cur, kv_prev, kv_cur, kv_prev, const2((blk, LANES)), const2((blk, LANES)),
                  pl.BlockSpec(bias.shape, lambda b, j: (0, 0, 0, 0)),
                  const2((8, LANES))],
        out_specs=pl.BlockSpec((nqb * blk, A_WIDTH), lambda b, j: (b * nq + j, 0)),
        compiler_params=_cparams("arbitrary", "arbitrary"), name="swa_prompt",
    )(qa, ka, ka, va, va, kmeta, vmeta, bias, sinks)


def _swa_step_kernel(ck_ref, cv_ref, q_ref, k_ref, v_ref, bias_ref, sink_ref, ko_ref, vo_ref, o_ref):
    g = ck_ref.shape[0]
    lane = lax.broadcasted_iota(I32, (A_HEADS, LANES), 1)
    row = lax.broadcasted_iota(I32, (A_HEADS, LANES), 0)
    own_half = (row % 2 == 0) == (lane < A_HD)
    bias = bias_ref[...]
    sk = sink_ref[:, 0:1]
    for j in range(g):
        ko_ref[j, 0:WINDOW - 1, :] = ck_ref[j, 1:WINDOW, :]
        ko_ref[j, WINDOW - 1:WINDOW, :] = k_ref[j:j + 1, :]
        vo_ref[j, 0:WINDOW - 1, :] = cv_ref[j, 1:WINDOW, :]
        vo_ref[j, WINDOW - 1:WINDOW, :] = v_ref[j:j + 1, :]
        kk = ko_ref[j].astype(BF16)
        vv = vo_ref[j].astype(BF16)
        slabs = [q_ref[j:j + 1, p * LANES:(p + 1) * LANES] for p in range(A_GROUP)]
        q8 = jnp.concatenate([slabs[r // 2] for r in range(A_HEADS)], axis=0)
        q8 = jnp.where(own_half, q8, 0.0).astype(BF16)
        s = lax.dot_general(q8, kk, (((1,), (1,)), ((), ())), preferred_element_type=F32)
        s = s * (A_HD ** -0.5) + bias
        m = jnp.maximum(jnp.max(s, axis=1, keepdims=True), sk)
        e = jnp.exp(s - m)
        inv = 1.0 / (jnp.sum(e, axis=1, keepdims=True) + jnp.exp(sk - m))
        o8 = jnp.where(own_half, jnp.dot(e.astype(BF16), vv, preferred_element_type=F32) * inv, 0.0)
        for p in range(A_GROUP):
            o_ref[j:j + 1, p * LANES:(p + 1) * LANES] = o8[2 * p:2 * p + 1, :] + o8[2 * p + 1:2 * p + 2, :]


def _swa_step(ck, cv, q, k, v, bias_rows, sinks):
    nb = ck.shape[0]
    g = SAMPLE_GROUP
    cache = pl.BlockSpec((g, WINDOW, LANES), lambda i: (i, 0, 0))
    row = lambda w: pl.BlockSpec((g, w), lambda i: (i, 0))
    const = lambda a: pl.BlockSpec(a.shape, lambda i: (0, 0))
    return pl.pallas_call(
        _swa_step_kernel,
        out_shape=[jax.ShapeDtypeStruct(ck.shape, F32), jax.ShapeDtypeStruct(cv.shape, F32),
                   jax.ShapeDtypeStruct((nb, A_WIDTH), F32)],
        grid=(nb // g,),
        in_specs=[cache, cache, row(A_WIDTH), row(LANES), row(LANES), const(bias_rows), const(sinks)],
        out_specs=[cache, cache, row(A_WIDTH)],
        compiler_params=_cparams("arbitrary"), name="swa_step",
    )(ck, cv, q, k, v, bias_rows, sinks)


def _layer_norm(z, g, b):
    mu = jnp.mean(z, axis=1, keepdims=True)
    zc = z - mu
    var = jnp.mean(zc * zc, axis=1, keepdims=True)
    return zc * lax.rsqrt(var + LN_EPS) * g + b


def _pack_halves(x):
    w = x.shape[1] // 2
    lo = pltpu.bitcast(x[:, :w].astype(BF16).astype(F32), U32)
    hi = pltpu.bitcast(x[:, w:].astype(BF16).astype(F32), U32)
    return (lo >> 16) | (hi & jnp.uint32(0xFFFF0000))


def _unpack_halves(words):
    lo = pltpu.bitcast(words << 16, F32).astype(BF16)
    hi = pltpu.bitcast(words & jnp.uint32(0xFFFF0000), F32).astype(BF16)
    return lo, hi


def _to_token_tiles(ref, x):
    for q in range(x.shape[1] // LANES):
        ref[:, q, :] = x[:, q * LANES:(q + 1) * LANES]


def _merge_kernel(h_ref, om_ref, att_ref, x_ref, gm_ref, ga_ref, wo_ref, g1_ref, b1_ref, wr_ref, br_ref,
                  x1_ref, xp_ref, tk_ref, cnt_ref):
    @pl.when(pl.program_id(0) == 0)
    def _():
        cnt_ref[...] = jnp.zeros_like(cnt_ref)

    hm = h_ref[...] * _sigmoid(om_ref[...])
    ym = hm * lax.rsqrt(jnp.mean(hm * hm, axis=1, keepdims=True) + LN_EPS) * gm_ref[...]
    att = att_ref[...]
    ya = att * lax.rsqrt(jnp.mean(att * att, axis=1, keepdims=True) + LN_EPS) * ga_ref[...]
    mix = (jnp.dot(ym.astype(BF16), wo_ref[0:M_WIDTH, :], preferred_element_type=F32)
           + jnp.dot(ya.astype(BF16), wo_ref[M_WIDTH:, :], preferred_element_type=F32))
    x1 = _layer_norm(DN_ALPHA * x_ref[...] + mix, g1_ref[...], b1_ref[...])
    x1_ref[...] = x1
    _to_token_tiles(xp_ref, _pack_halves(x1))
    logits = jnp.dot(x1.astype(BF16), wr_ref[...], preferred_element_type=F32) + br_ref[...]
    lane = lax.broadcasted_iota(I32, logits.shape, 1).astype(F32)
    vals, idxs = [], []
    for _ in range(TOP_K):
        mx = jnp.max(logits, axis=1, keepdims=True)
        idx = jnp.min(jnp.where(logits == mx, lane, float(LANES)), axis=1, keepdims=True)
        vals.append(mx)
        idxs.append(idx)
        logits = jnp.where(lane == idx, 2.0 * NEG, logits)
    es = [jnp.exp(vk - vals[0]) for vk in vals]
    tot = es[0] + es[1] + es[2] + es[3]
    tk = jnp.zeros(logits.shape, F32)
    picked = jnp.zeros(logits.shape, F32)
    for k in range(TOP_K):
        tk = jnp.where(lane == float(k), es[k] / tot, tk)
        tk = jnp.where(lane == float(TOP_K + k), idxs[k], tk)
        picked = jnp.where(lane == idxs[k], 1.0, picked)
    tk_ref[...] = tk
    cnt_ref[...] = cnt_ref[...] + jnp.sum(picked, axis=0, keepdims=True)


def _merge(h, om, att, x, gm, ga, wo, g1, b1, wr, br, tile, name):
    t = x.shape[0]
    rows = lambda w: pl.BlockSpec((tile, w), lambda i: (i, 0))
    const = lambda a: pl.BlockSpec(a.shape, lambda i: (0, 0))
    return pl.pallas_call(
        _merge_kernel,
        out_shape=[jax.ShapeDtypeStruct((t, D_MODEL), F32), jax.ShapeDtypeStruct((t, XP_TILE, LANES), U32),
                   jax.ShapeDtypeStruct((t, LANES), F32), jax.ShapeDtypeStruct((8, LANES), F32)],
        grid=(t // tile,),
        in_specs=[rows(M_WIDTH), rows(M_WIDTH), rows(A_WIDTH), rows(D_MODEL), const(gm), const(ga), const(wo),
                  const(g1), const(b1), const(wr), const(br)],
        out_specs=[rows(D_MODEL), pl.BlockSpec((tile, XP_TILE, LANES), lambda i: (i, 0, 0)), rows(LANES),
                   pl.BlockSpec((8, LANES), lambda i: (0, 0))],
        compiler_params=_cparams("arbitrary"), name=name,
    )(h, om, att, x, gm, ga, wo, g1, b1, wr, br)


def _route_kernel(tk_ref, first_ref, strict_ref, dest_ref, next_scr):
    @pl.when(pl.program_id(0) == 0)
    def _():
        next_scr[...] = first_ref[...]

    tk = tk_ref[...]
    lane = lax.broadcasted_iota(I32, tk.shape, 1).astype(F32)
    onehots = [jnp.where(lane == tk[:, TOP_K + k:TOP_K + k + 1], 1.0, 0.0) for k in range(TOP_K)]
    tot = onehots[0] + onehots[1] + onehots[2] + onehots[3]
    row = jnp.dot(strict_ref[...], tot.astype(BF16), preferred_element_type=F32) + next_scr[0:1, :]
    out = jnp.zeros(tk.shape, F32)
    for k in range(TOP_K):
        out = jnp.where(lane == float(k), jnp.sum(onehots[k] * row, axis=1, keepdims=True), out)
    dest_ref[...] = out.astype(I32)
    next_scr[...] = next_scr[...] + jnp.sum(tot, axis=0, keepdims=True)


def _route(tk, first):
    t = tk.shape[0]
    tile = min(RANK_TILE, t)
    strict = jnp.asarray(np.tril(np.ones((tile, tile), np.float32), -1), BF16)
    return pl.pallas_call(
        _route_kernel, out_shape=jax.ShapeDtypeStruct((t, LANES), I32), grid=(t // tile,),
        in_specs=[pl.BlockSpec((tile, LANES), lambda i: (i, 0)), pl.BlockSpec((8, LANES), lambda i: (0, 0)),
                  pl.BlockSpec((tile, tile), lambda i: (0, 0))],
        out_specs=pl.BlockSpec((tile, LANES), lambda i: (i, 0)),
        scratch_shapes=[pltpu.VMEM((8, LANES), F32)],
        compiler_params=_cparams("arbitrary"), name="moe_route",
    )(tk, first, strict)


def _offsets_kernel(cnt_ref, off_ref, be_ref, nu_ref, pad_ref, *, tile):
    cnt = cnt_ref[...]
    nblk = jnp.floor((cnt + float(tile - 1)) * (1.0 / tile))
    r_i = lax.broadcasted_iota(I32, (LANES, LANES), 0)
    c_i = lax.broadcasted_iota(I32, (LANES, LANES), 1)
    incl = jnp.where(r_i <= c_i, 1.0, 0.0).astype(BF16)
    cum = jnp.dot(nblk.astype(BF16), incl, preferred_element_type=F32)
    off = (cum - nblk) * float(tile)
    off_ref[...] = off
    which = lax.broadcasted_iota(I32, cnt.shape, 0)
    pad_ref[...] = jnp.where(which == 0, off + cnt, jnp.where(which == 1, nblk * float(tile) - cnt, 0.0)).astype(I32)
    rows = be_ref.shape[0]
    jb = (lax.broadcasted_iota(I32, (rows, LANES), 0) * LANES + lax.broadcasted_iota(I32, (rows, LANES), 1)).astype(F32)
    acc = jnp.zeros((rows, LANES), F32)
    for e in range(N_EXPERTS):
        acc = acc + jnp.where(jb >= cum[0:1, e:e + 1], 1.0, 0.0)
    be_ref[...] = jnp.minimum(acc, float(N_EXPERTS - 1)).astype(I32)
    nu_ref[...] = jnp.broadcast_to(cum[0:1, N_EXPERTS - 1:N_EXPERTS], nu_ref.shape).astype(I32)


def _offsets(cnt, n_blocks, tile):
    rows = -(-n_blocks // LANES)
    rows = -(-rows // 8) * 8
    return pl.pallas_call(
        functools.partial(_offsets_kernel, tile=tile),
        out_shape=[jax.ShapeDtypeStruct((8, LANES), F32), jax.ShapeDtypeStruct((rows, LANES), I32),
                   jax.ShapeDtypeStruct((8, LANES), I32), jax.ShapeDtypeStruct((8, LANES), I32)],
        name="moe_offsets",
    )(cnt)


def _scatter_rows(dest_ref, xp_ref, xs_ref, sem):
    t = xp_ref.shape[0]

    def row_copy(tok, dst):
        return pltpu.make_async_copy(xp_ref.at[pl.ds(tok, 1)], xs_ref.at[pl.ds(dst, 1)], sem)

    def issue(grp, carry):
        base = pl.multiple_of(grp * ISSUE_GROUP, ISSUE_GROUP)
        for u in range(ISSUE_GROUP):
            for k in range(TOP_K):
                row_copy(base + u, dest_ref[(base + u) * TOP_K + k]).start(priority=k % 2)
        return carry

    lax.fori_loop(0, t // ISSUE_GROUP, issue, 0)
    for k in range(TOP_K):
        pltpu.make_async_copy(xp_ref, xs_ref.at[pl.ds(0, t)], sem).wait()


def _dispatch_kernel(dest_ref, pads_ref, dest2_ref, xp_ref, xp2_ref, xs_ref, sem, zsem, zbuf, *, block_rows):
    zr = zbuf.shape[0]
    n_blocks = xs_ref.shape[0] // block_rows

    @pl.when(pl.program_id(0) == 0)
    def _():
        zbuf[...] = jnp.zeros_like(zbuf)
        used = pads_ref[2 * N_EXPERTS]

        def pieces(e, act):
            start, n = pads_ref[e], pads_ref[N_EXPERTS + e]
            for sh in range(zr.bit_length() - 1, -1, -1):
                b = 1 << sh
                before = lax.shift_left(lax.shift_right_logical(n, sh + 1), sh + 1)

                @pl.when((n & b) != 0)
                def _():
                    act(pltpu.make_async_copy(zbuf.at[pl.ds(0, b)], xs_ref.at[pl.ds(start + before, b)], zsem))

        def tail(jb, act):
            for h in range(block_rows // zr):
                act(pltpu.make_async_copy(zbuf, xs_ref.at[pl.ds(jb * block_rows + h * zr, zr)], zsem))

        for act in (lambda cp: cp.start(), lambda cp: cp.wait()):
            lax.fori_loop(0, N_EXPERTS, lambda e, c: (pieces(e, act), c)[1], 0)
            lax.fori_loop(used, n_blocks, lambda jb, c: (tail(jb, act), c)[1], 0)

    _scatter_rows(dest_ref, xp_ref, xs_ref, sem)

    @pl.when(pl.program_id(0) == pl.num_programs(0) - 1)
    def _():
        _scatter_rows(dest2_ref, xp2_ref, xs_ref, sem)


def _dispatch(dest, pads, dest2, xp, xp2, n_rows, block_rows):
    t = xp.shape[0]
    tile = min(ROW_TILE, t)
    return pl.pallas_call(
        functools.partial(_dispatch_kernel, block_rows=block_rows),
        out_shape=jax.ShapeDtypeStruct((n_rows,) + xp.shape[1:], xp.dtype), grid=(t // tile,),
        in_specs=[pl.BlockSpec((tile * TOP_K,), lambda i: (i,), memory_space=pltpu.SMEM),
                  pl.BlockSpec(memory_space=pltpu.SMEM),
                  pl.BlockSpec(memory_space=pltpu.SMEM),
                  pl.BlockSpec((tile,) + xp.shape[1:], lambda i: (i, 0, 0)),
                  pl.BlockSpec(xp2.shape, lambda i: (0, 0, 0))],
        out_specs=pl.BlockSpec(memory_space=pl.ANY),
        scratch_shapes=[pltpu.SemaphoreType.DMA(()), pltpu.SemaphoreType.DMA(()),
                        pltpu.VMEM((EXPERT_TILE // 2,) + xp.shape[1:], xp.dtype)],
        compiler_params=_cparams("arbitrary"), name="moe_dispatch",
    )(dest, pads, dest2, xp, xp2)


def _expert_kernel(be_ref, nu_ref, xs_ref, w1_ref, b1g_ref, b1l_ref, w2_ref, b2_ref, perm_ref, ys_ref,
                   w1g_scr, w1l_scr, w2_scr, xq_scr, y_scr, sem, osem):
    j = pl.program_id(0)
    active = j < nu_ref[0]
    changed = jnp.logical_or(j == 0, be_ref[j] != be_ref[jnp.maximum(j - 1, 0)])
    tm = y_scr.shape[0]
    slot = lax.rem(j, 2)

    def fetch(blk, slot):
        row0 = pl.multiple_of(blk * tm, tm)
        return [pltpu.make_async_copy(xs_ref.at[pl.ds(row0, tm), q, :], xq_scr.at[slot, q], sem.at[slot])
                for q in range(XP_TILE)]

    def put(blk):
        row0 = pl.multiple_of(blk * tm, tm)
        return [pltpu.make_async_copy(y_scr.at[:, q * LANES:(q + 1) * LANES], ys_ref.at[pl.ds(row0, tm), q, :], osem)
                for q in range(YS_TILE)]

    def emit(y):
        @pl.when(j > 0)
        def _():
            for cp in put(j - 1):
                cp.wait()

        y_scr[...] = _pack_halves(y)
        for cp in put(j):
            cp.start()

    @pl.when(j == 0)
    def _():
        for cp in fetch(0, 0):
            cp.start()

    @pl.when(j + 1 < nu_ref[0])
    def _():
        for cp in fetch(j + 1, 1 - slot):
            cp.start()

    @pl.when(jnp.logical_and(active, changed))
    def _():
        for c in range(2 * D_FF // 256):
            wc = w1_ref[0, :, c * 256:(c + 1) * 256].astype(BF16)
            d = jnp.dot(wc, perm_ref[...], preferred_element_type=F32).astype(BF16)
            w1g_scr[:, c * 128:(c + 1) * 128] = d[:, :128]
            w1l_scr[:, c * 128:(c + 1) * 128] = d[:, 128:]
        for c in range(D_FF // 256):
            w2_scr[c * 256:(c + 1) * 256, :] = w2_ref[0, c * 256:(c + 1) * 256, :].astype(BF16)

    @pl.when(active)
    def _():
        for cp in fetch(j, slot):
            cp.wait()
        lo, hi = _unpack_halves(jnp.concatenate([xq_scr[slot, q] for q in range(XP_TILE)], axis=1))
        xb = jnp.concatenate([lo, hi], axis=1)
        hg = jnp.dot(xb, w1g_scr[...], preferred_element_type=F32) + b1g_ref[0]
        hl = jnp.dot(xb, w1l_scr[...], preferred_element_type=F32) + b1l_ref[0]
        x_glu = jnp.minimum(hg, SWIGLU_LIMIT)
        x_lin = jnp.clip(hl, -SWIGLU_LIMIT, SWIGLU_LIMIT)
        a = x_glu * _sigmoid(SWIGLU_ALPHA * x_glu) * (x_lin + 1.0)
        emit(jnp.dot(a.astype(BF16), w2_scr[...], preferred_element_type=F32) + b2_ref[0])

    @pl.when(jnp.logical_not(active))
    def _():
        emit(jnp.zeros((tm, D_MODEL), F32))

    @pl.when(j == pl.num_programs(0) - 1)
    def _():
        for cp in put(j):
            cp.wait()


def _experts(be, nu, xs, w1, b1g, b1l, w2, b2, perm, tile):
    n_blocks = xs.shape[0] // tile
    grid_spec = pltpu.PrefetchScalarGridSpec(
        num_scalar_prefetch=2, grid=(n_blocks,),
        in_specs=[pl.BlockSpec(memory_space=pl.ANY),
                  pl.BlockSpec((1, D_MODEL, 2 * D_FF), lambda j, be, nu: (be[j], 0, 0)),
                  pl.BlockSpec((1, 1, D_FF), lambda j, be, nu: (be[j], 0, 0)),
                  pl.BlockSpec((1, 1, D_FF), lambda j, be, nu: (be[j], 0, 0)),
                  pl.BlockSpec((1, D_FF, D_MODEL), lambda j, be, nu: (be[j], 0, 0)),
                  pl.BlockSpec((1, 1, D_MODEL), lambda j, be, nu: (be[j], 0, 0)),
                  pl.BlockSpec((256, 256), lambda j, be, nu: (0, 0))],
        out_specs=pl.BlockSpec(memory_space=pl.ANY),
        scratch_shapes=[pltpu.VMEM((D_MODEL, D_FF), BF16), pltpu.VMEM((D_MODEL, D_FF), BF16),
                        pltpu.VMEM((D_FF, D_MODEL), BF16), pltpu.VMEM((2, XP_TILE, tile, LANES), U32),
                        pltpu.VMEM((tile, YS_TILE * LANES), U32), pltpu.SemaphoreType.DMA((2,)),
                        pltpu.SemaphoreType.DMA(())])
    return pl.pallas_call(
        _expert_kernel, out_shape=jax.ShapeDtypeStruct((xs.shape[0], YS_TILE, LANES), U32), grid_spec=grid_spec,
        compiler_params=_cparams("arbitrary"), name="moe_experts",
    )(be, nu, xs, w1, b1g, b1l, w2, b2, perm)


def _combine_kernel(dest_ref, next_ref, ys_ref, tk_ref, x1_ref, g2_ref, b2_ref, out_ref, buf, sem):
    i = pl.program_id(0)
    t = x1_ref.shape[0]
    slot = lax.rem(i, 2)

    def gather(idx_ref, s):
        def issue(grp, carry):
            base = pl.multiple_of(grp * 8, 8)
            for u in range(8):
                for k in range(TOP_K):
                    pltpu.make_async_copy(ys_ref.at[idx_ref[(base + u) * TOP_K + k]],
                                          buf.at[s, k, grp, :, u, :], sem.at[s]).start(priority=k % 2)
            return carry

        lax.fori_loop(0, t // 8, issue, 0)

    @pl.when(i == 0)
    def _():
        gather(dest_ref, 0)

    @pl.when(i + 1 < pl.num_programs(0))
    def _():
        gather(next_ref, 1 - slot)

    for k in range(TOP_K):
        for u in range(8):
            pltpu.make_async_copy(ys_ref.at[pl.ds(0, t // 8)], buf.at[slot, k, :, :, u, :], sem.at[slot]).wait()
    tk = tk_ref[...]
    los, his = [], []
    for q in range(YS_TILE):
        lo = hi = None
        for k in range(TOP_K):
            words = buf[slot, k, :, q].reshape(t, LANES)
            g = tk[:, k:k + 1]
            lo_k = g * pltpu.bitcast(words << 16, F32)
            hi_k = g * pltpu.bitcast(words & jnp.uint32(0xFFFF0000), F32)
            lo = lo_k if lo is None else lo + lo_k
            hi = hi_k if hi is None else hi + hi_k
        los.append(lo)
        his.append(hi)
    ff = jnp.concatenate(los + his, axis=1)
    out_ref[...] = _layer_norm(DN_ALPHA * x1_ref[...] + ff, g2_ref[...], b2_ref[...])


def _combine(dest_flat, ys, tk, x1, g2, b2):
    t = x1.shape[0]
    tile = min(ROW_TILE, t)
    n = t // tile
    return pl.pallas_call(
        _combine_kernel, out_shape=jax.ShapeDtypeStruct((t, D_MODEL), F32), grid=(n,),
        in_specs=[pl.BlockSpec((tile * TOP_K,), lambda i: (i,), memory_space=pltpu.SMEM),
                  pl.BlockSpec((tile * TOP_K,), lambda i: (jnp.minimum(i + 1, n - 1),), memory_space=pltpu.SMEM),
                  pl.BlockSpec(memory_space=pl.ANY),
                  pl.BlockSpec((tile, LANES), lambda i: (i, 0)),
                  pl.BlockSpec((tile, D_MODEL), lambda i: (i, 0)),
                  pl.BlockSpec((1, D_MODEL), lambda i: (0, 0)),
                  pl.BlockSpec((1, D_MODEL), lambda i: (0, 0))],
        out_specs=pl.BlockSpec((tile, D_MODEL), lambda i: (i, 0)),
        scratch_shapes=[pltpu.VMEM((2, TOP_K, tile // 8, YS_TILE, 8, LANES), U32), pltpu.SemaphoreType.DMA((2,))],
        compiler_params=_cparams("arbitrary"), name="moe_combine",
    )(dest_flat, dest_flat, ys, tk, x1, g2, b2)


def _rel_bucket(dist):
    exact = REL_BUCKETS // 2
    d = np.maximum(dist, 0)
    log_b = exact + (np.log(np.maximum(d, 1).astype(np.float32) / np.float32(exact))
                     / np.float32(math.log(REL_MAX_DIST / exact)) * np.float32(REL_BUCKETS - exact)).astype(np.int32)
    return np.where(d < exact, d, np.minimum(log_b, REL_BUCKETS - 1)).astype(np.int32)


def _bias_lookup(table, bucket, valid):
    bucket = jnp.asarray(bucket)[None]
    acc = jnp.zeros((table.shape[1],) + bucket.shape[1:], F32)
    for b in range(REL_BUCKETS):
        acc = jnp.where(bucket == b, table[b].reshape((-1,) + (1,) * (bucket.ndim - 1)), acc)
    return jnp.where(jnp.asarray(valid)[None], acc, NEG)


def _bias_tables(rel_bias):
    table = rel_bias.astype(F32)
    r = np.arange(WINDOW)[:, None]
    c = np.arange(2 * WINDOW)[None, :]
    dist = r + WINDOW - c
    valid = (dist >= 0) & (dist < WINDOW)
    dist0 = np.where(c < N_META, N_META + r - c, dist)
    valid0 = np.where(c < N_META, dist0 < WINDOW, (c >= WINDOW) & valid)
    both = jnp.stack([_bias_lookup(table, _rel_bucket(dist0), valid0), _bias_lookup(table, _rel_bucket(dist), valid)])
    dist_s = WINDOW - 1 - np.arange(WINDOW)
    rows = _bias_lookup(table[:, np.asarray(HEAD_ORDER)], _rel_bucket(dist_s), np.ones_like(dist_s, bool))
    return both, rows


def _perm_heads(a, axis):
    assert HEAD_ORDER == tuple(kv * A_GROUP + g for g in range(A_GROUP) for kv in range(A_KV_HEADS))
    shape = a.shape
    a = a.reshape(shape[:axis] + (A_KV_HEADS, A_GROUP, A_HD) + shape[axis + 1:])
    return jnp.swapaxes(a, axis, axis + 1).reshape(shape)


def _rep_rows(vec, rows=8):
    out = jnp.zeros((rows, LANES), F32)
    return out.at[:vec.shape[0], :].set(jnp.broadcast_to(vec.astype(F32)[:, None], (vec.shape[0], LANES)))


def kernel(x_prompt, x_sample, cache_swa_k, cache_swa_v, state_mlstm_C, state_mlstm_n, state_mlstm_m, meta_tokens, rel_bias, w_in, b_igate, b_fgate, attn_sinks, g_mlstm_out, g_attn_out, w_out, ln1_g, ln1_b, w_router, b_router, w_moe1, b_moe1, w_moe2, b_moe2, ln2_g, ln2_b):
    B, S, _ = x_prompt.shape
    NB = x_sample.shape[0]
    assert x_sample.shape[1] == 1 and w_in.shape[0] == 1
    assert S % PROJ_TILE == 0 and S % M_CHUNK == 0 and S % WINDOW == 0 and NB % SAMPLE_GROUP == 0
    l = 0

    assert IN_WIDTHS == (512, 512, 512, 512, 4, 4, 512, 128, 128)
    bf = lambda a: a.astype(BF16)
    w = w_in[l]
    n_main, n_gate = 4 * M_WIDTH, 2 * M_HEADS
    w_gate = w[:, n_main:n_main + n_gate]
    w_att = w[:, n_main + n_gate:]
    wr = bf(jnp.concatenate([w[:, :n_main], _perm_heads(w_att[:, :A_WIDTH], 1), w_att[:, A_WIDTH:],
                             jnp.pad(w_gate, ((0, 0), (0, LANES - n_gate)))], axis=1))
    wt = bf(jnp.concatenate([w[:, M_WIDTH:2 * M_WIDTH], w_gate], axis=1).T)
    b_gate = jnp.concatenate([b_igate[l], b_fgate[l]]).astype(F32)
    brow = jnp.pad(b_gate, (0, LANES - n_gate))[None, :]
    bcol = b_gate[:, None]
    plan_p = ((0, 512, "plain", BF16), (1024, 512, "plain", BF16), (1536, 512, "plain", F32),
              (2048, 512, "plain", BF16), (2560, 128, "plain", BF16), (2688, 128, "plain", BF16),
              (2816, 128, "gate", F32))
    tplan_p = ((0, 512, "plain", BF16), (512, 8, "gate", F32))
    plan_s = ((0, 512, "plain", F32), (512, 512, "plain", F32), (1024, 512, "plain", F32), (1536, 512, "plain", F32),
              (2048, 512, "plain", F32), (2560, 128, "plain", F32), (2688, 128, "plain", F32), (2816, 128, "gate", F32))

    bias_tab, bias_rows = _bias_tables(rel_bias)
    sinks = _rep_rows(attn_sinks[l])
    sinks_step = _rep_rows(attn_sinks[l][np.asarray(HEAD_ORDER)])
    g_m = g_mlstm_out[l].astype(F32)[None, :]
    g_a = _perm_heads(g_attn_out[l].astype(F32), 0)[None, :]
    wo = bf(jnp.concatenate([w_out[l][:M_WIDTH], _perm_heads(w_out[l][M_WIDTH:], 0)], axis=0))
    g1, b1 = ln1_g[l].astype(F32)[None, :], ln1_b[l].astype(F32)[None, :]
    g2, b2 = ln2_g[l].astype(F32)[None, :], ln2_b[l].astype(F32)[None, :]
    w_r = bf(jnp.pad(w_router[l], ((0, 0), (0, LANES - N_EXPERTS))))
    b_r = jnp.pad(b_router[l].astype(F32), (0, LANES - N_EXPERTS), constant_values=NEG)[None, :]
    b1g = b_moe1[l][:, 0::2].astype(F32)[:, None, :]
    b1l = b_moe1[l][:, 1::2].astype(F32)[:, None, :]
    b2e = b_moe2[l].astype(F32)[:, None, :]
    pj = np.zeros((256, 256), np.float32)
    pj[2 * np.arange(128), np.arange(128)] = 1.0
    pj[2 * np.arange(128) + 1, 128 + np.arange(128)] = 1.0
    perm = jnp.asarray(pj, BF16)

    xp2 = x_prompt.reshape(B * S, D_MODEL)
    qm, vm, om, qa, ka, va, gc, kt, gr, kv_tail = _proj(
        xp2, wr, wt, brow, bcol, plan_p, tplan_p, (2560, 256), PROJ_TILE, S, "proj_prompt")
    x_meta = jnp.pad(meta_tokens.astype(F32), ((0, M_CHUNK - N_META), (0, 0)))
    qm0, vm0, _, _, ka0, va0, gc0, kt0, gr0 = _proj(
        x_meta, wr, wt, brow, bcol, plan_p, tplan_p, None, M_CHUNK, M_CHUNK, "proj_meta")
    xs2 = x_sample.reshape(NB, D_MODEL)
    qm_s, km_s, vm_s, om_s, qa_s, ka_s, va_s, gc_s = _proj(
        xs2, wr, wt, brow, bcol, plan_s, (), None, NB, NB, "proj_sample")

    zero_c = jnp.zeros((M_HEADS, M_DK, 2 * M_DV), F32)
    zero_m = jnp.zeros((8, LANES), F32)
    _, ct_meta, m_meta = _mlstm(qm0, vm0, kt0, gc0, gr0, zero_c, zero_m, 1, N_META, "mlstm_meta")
    c0 = jnp.swapaxes(ct_meta[0], 1, 2)
    h_p, ct_p, m_p = _mlstm(qm, vm, kt, gc, gr, c0, m_meta[0], B, M_CHUNK, "mlstm_prompt")
    C_p = ct_p[:, :, :M_DV, :]
    n_p = ct_p[:, :, M_DV, :]
    m_prompt = m_p[:, :M_HEADS, 0]
    m_pad = jnp.pad(state_mlstm_m[l].astype(F32), ((0, 0), (0, LANES - M_HEADS)))
    C_s, n_s, m_s, h_s = _mlstm_step(state_mlstm_C[l].astype(F32), state_mlstm_n[l].astype(F32), m_pad,
                                     gc_s, qm_s, km_s, vm_s)

    att_p = _swa(qa, ka, va, ka0, va0, bias_tab, sinks, B)
    ck = cache_swa_k[l].reshape(NB, WINDOW, LANES)
    cv = cache_swa_v[l].reshape(NB, WINDOW, LANES)
    k_new, v_new, att_s = _swa_step(ck, cv, qa_s, ka_s, va_s, bias_rows, sinks_step)

    x1_p, xpk_p, tk_p, cnt_p = _merge(h_p, om, att_p, xp2, g_m, g_a, wo, g1, b1, w_r, b_r, MERGE_TILE, "merge_prompt")
    x1_s, xpk_s, tk_s, cnt_s = _merge(h_s, om_s, att_s, xs2, g_m, g_a, wo, g1, b1, w_r, b_r, NB, "merge_sample")

    T_p = B * S
    assert T_p % RANK_TILE == 0 and T_p % ROW_TILE == 0
    n_blocks = -(-((T_p + NB) * TOP_K) // EXPERT_TILE) + N_EXPERTS
    off, be2, nu2, pad = _offsets(cnt_p + cnt_s, n_blocks, EXPERT_TILE)
    pads = jnp.concatenate([pad[0, :N_EXPERTS], pad[1, :N_EXPERTS], nu2[0, :1]])
    dest_p = _route(tk_p, off)[:, :TOP_K].reshape(-1)
    dest_s = _route(tk_s, off + cnt_p)[:, :TOP_K].reshape(-1)
    be = be2.reshape(-1)[:n_blocks]
    nu = nu2[0, :1]
    xs = _dispatch(dest_p, pads, dest_s, xpk_p, xpk_s, n_blocks * EXPERT_TILE, EXPERT_TILE)
    ys = _experts(be, nu, xs, w_moe1[l], b1g, b1l, w_moe2[l], b2e, perm, EXPERT_TILE)
    y_p = _combine(dest_p, ys, tk_p, x1_p, g2, b2)
    y_s = _combine(dest_s, ys, tk_s, x1_s, g2, b2)

    kv_tail = kv_tail.reshape(B, WINDOW, 2, A_KV_HEADS, A_HD)
    dt_k, dt_v = cache_swa_k.dtype, cache_swa_v.dtype
    return (y_p.reshape(B, S, D_MODEL).astype(x_prompt.dtype), y_s.reshape(NB, 1, D_MODEL).astype(x_sample.dtype),
            kv_tail[:, :, 0][None].astype(dt_k), kv_tail[:, :, 1][None].astype(dt_v),
            C_p[None].astype(state_mlstm_C.dtype), n_p[None].astype(state_mlstm_n.dtype),
            m_prompt[None].astype(state_mlstm_m.dtype),
            k_new.reshape(1, NB, WINDOW, A_KV_HEADS, A_HD).astype(dt_k),
            v_new.reshape(1, NB, WINDOW, A_KV_HEADS, A_HD).astype(dt_v),
            C_s[None].astype(state_mlstm_C.dtype), n_s[None].astype(state_mlstm_n.dtype),
            m_s[:, :M_HEADS][None].astype(state_mlstm_m.dtype))
```

```python
import functools
import math

import numpy as np
import jax
import jax.numpy as jnp
from jax import lax
from jax.experimental import pallas as pl
from jax.experimental.pallas import tpu as pltpu

F32 = jnp.float32
BF16 = jnp.bfloat16
I32 = jnp.int32
U32 = jnp.uint32

D_MODEL = 1024
N_META = 16
M_HEADS = 4
M_DK = 128
M_DV = 128
M_WIDTH = M_HEADS * M_DV
A_HD = 64
A_HEADS = 8
A_KV_HEADS = 2
A_GROUP = A_HEADS // A_KV_HEADS
A_WIDTH = A_HEADS * A_HD
WINDOW = 128
REL_BUCKETS = 32
REL_MAX_DIST = 128
N_EXPERTS = 32
TOP_K = 4
D_FF = D_MODEL
SWIGLU_LIMIT = 7.0
SWIGLU_ALPHA = 1.702
DEPTH = 1
DN_ALPHA = (2.0 * DEPTH) ** 0.25
LN_EPS = 1e-5
IN_WIDTHS = (M_WIDTH, M_WIDTH, M_WIDTH, M_WIDTH, M_HEADS, M_HEADS, A_WIDTH, A_KV_HEADS * A_HD, A_KV_HEADS * A_HD)

LANES = 128
NEG = -1e30
VMEM_LIMIT = 56 * 1024 * 1024

M_CHUNK = 128
PROJ_TILE = 512
MERGE_TILE = 512
RANK_TILE = 512
ROW_TILE = 256
EXPERT_TILE = 512
SAMPLE_GROUP = 8
ISSUE_GROUP = 8
SWA_QBLOCKS = 2
XP_TILE = D_MODEL // 2 // LANES
YS_TILE = D_MODEL // 2 // LANES
HEAD_ORDER = (0, 4, 1, 5, 2, 6, 3, 7)


def _cparams(*sem):
    return pltpu.CompilerParams(dimension_semantics=sem, vmem_limit_bytes=VMEM_LIMIT)


def _log_sigmoid(x):
    return jnp.minimum(x, 0.0) - jnp.log1p(jnp.exp(-jnp.abs(x)))


def _sigmoid(x):
    return 1.0 / (1.0 + jnp.exp(-x))


def _proj_kernel(x_ref, wr_ref, wt_ref, brow_ref, bcol_ref, *outs, row_plan, t_plan, tail_cols):
    xb = x_ref[...].astype(BF16)
    tm = xb.shape[0]
    o = 0
    for (c0, width, kind, _) in row_plan:
        r = jnp.dot(xb, wr_ref[:, c0:c0 + width], preferred_element_type=F32)
        if kind == "gate":
            r = r + brow_ref[...]
            lane = lax.broadcasted_iota(I32, r.shape, 1)
            r = jnp.where(lane < M_HEADS, r, _log_sigmoid(r))
        outs[o][...] = r.astype(outs[o].dtype)
        o += 1
    for (r0, nrows, kind, _) in t_plan:
        r = lax.dot_general(wt_ref[r0:r0 + nrows, :], xb, (((1,), (1,)), ((), ())), preferred_element_type=F32)
        if kind == "gate":
            r = r + bcol_ref[...]
            row = lax.broadcasted_iota(I32, r.shape, 0)
            r = jnp.where(row < M_HEADS, r, _log_sigmoid(r))
        outs[o][...] = r.astype(outs[o].dtype)
        o += 1
    if tail_cols is not None:
        c0, width = tail_cols
        outs[o][...] = jnp.dot(xb[tm - WINDOW:, :], wr_ref[:, c0:c0 + width], preferred_element_type=F32)


def _proj(x, wr, wt, brow, bcol, row_plan, t_plan, tail_cols, tile, rows_per_group, name):
    t = x.shape[0]
    nt = t // tile
    out_shape, out_specs = [], []
    for (_, width, _, dt) in row_plan:
        out_shape.append(jax.ShapeDtypeStruct((t, width), dt))
        out_specs.append(pl.BlockSpec((tile, width), lambda i: (i, 0)))
    for (_, nrows, _, dt) in t_plan:
        out_shape.append(jax.ShapeDtypeStruct((nrows, t), dt))
        out_specs.append(pl.BlockSpec((nrows, tile), lambda i: (0, i)))
    if tail_cols is not None:
        tiles_per_group = rows_per_group // tile
        out_shape.append(jax.ShapeDtypeStruct((t // rows_per_group * WINDOW, tail_cols[1]), F32))
        out_specs.append(pl.BlockSpec((WINDOW, tail_cols[1]), lambda i: (i // tiles_per_group, 0)))
    kern = functools.partial(_proj_kernel, row_plan=row_plan, t_plan=t_plan, tail_cols=tail_cols)
    return pl.pallas_call(
        kern, out_shape=out_shape, grid=(nt,),
        in_specs=[pl.BlockSpec((tile, D_MODEL), lambda i: (i, 0)),
                  pl.BlockSpec(wr.shape, lambda i: (0, 0)),
                  pl.BlockSpec(wt.shape, lambda i: (0, 0)),
                  pl.BlockSpec(brow.shape, lambda i: (0, 0)),
                  pl.BlockSpec(bcol.shape, lambda i: (0, 0))],
        out_specs=out_specs, compiler_params=_cparams("arbitrary"), name=name,
    )(x, wr, wt, brow, bcol)


def _split3(a):
    hi = a.astype(BF16)
    r1 = a - hi.astype(F32)
    mid = r1.astype(BF16)
    lo = (r1 - mid.astype(F32)).astype(BF16)
    return hi, mid, lo


def _mlstm_kernel(qt_ref, k_ref, vt_ref, gc_ref, gr_ref, c0_ref, m0_ref, h_ref, c_out_ref, m_out_ref,
                  c_scr, m_scr, *, n_valid):
    c = pl.program_id(1)
    nc = pl.num_programs(1)
    L = k_ref.shape[0]

    @pl.when(c == 0)
    def _():
        c_scr[...] = c0_ref[...]
        m_scr[...] = m0_ref[...]

    gc = gc_ref[...]
    gr = gr_ref[...]
    if n_valid < L:
        rowc = lax.broadcasted_iota(I32, gc.shape, 0)
        lanec = lax.broadcasted_iota(I32, gc.shape, 1)
        gc = jnp.where(rowc < n_valid, gc, jnp.where(lanec < M_HEADS, NEG, 0.0))
        rowr = lax.broadcasted_iota(I32, gr.shape, 0)
        colr = lax.broadcasted_iota(I32, gr.shape, 1)
        gr = jnp.where(colr < n_valid, gr, jnp.where(rowr < M_HEADS, NEG, 0.0))
    r_i = lax.broadcasted_iota(I32, (L, L), 0)
    c_i = lax.broadcasted_iota(I32, (L, L), 1)
    upper = r_i <= c_i
    tril = jnp.where(c_i <= r_i, 1.0, 0.0).astype(BF16)
    triu = jnp.where(upper, 1.0, 0.0).astype(BF16)
    b_cols = sum(jnp.dot(tril, part, preferred_element_type=F32) for part in _split3(gc))
    b_rows = sum(jnp.dot(part, triu, preferred_element_type=F32) for part in _split3(gr))
    scale = M_DK ** -0.5
    ones_rows = jnp.where(lax.broadcasted_iota(I32, (8, L), 0) == 0, 1.0, 0.0).astype(BF16)

    m_all = m_scr[...]
    c_all = [c_scr[h] for h in range(M_HEADS)]
    h_new, c_new, m_new_all = [], [], []
    for h in range(M_HEADS):
        sl = slice(h * M_DK, (h + 1) * M_DK)
        qt = qt_ref[sl, :]
        k = k_ref[:, sl]
        vt_aug = jnp.concatenate([vt_ref[sl, :], ones_rows], axis=0)
        r_c = gc[:, h:h + 1] - b_cols[:, M_HEADS + h:M_HEADS + h + 1]
        ig_r = gr[h:h + 1, :]
        b_r = b_rows[M_HEADS + h:M_HEADS + h + 1, :]
        m_prev = m_all[h:h + 1, 0:1]
        cs = c_all[h]

        dt = jnp.where(upper, b_r + r_c, NEG)
        m_t = jnp.maximum(b_r + m_prev, jnp.max(dt, axis=0, keepdims=True))
        st = jnp.dot(k, qt, preferred_element_type=F32) * (scale * jnp.exp(dt - m_t))
        inter = jnp.dot(cs.astype(BF16), qt, preferred_element_type=F32)
        intra = jnp.dot(vt_aug, st.astype(BF16), preferred_element_type=F32)
        nd = jnp.exp(b_r + m_prev - m_t) * inter + intra
        den = nd[M_DV:M_DV + 1, :]
        h_new.append((nd[:M_DV, :] / jnp.maximum(jnp.abs(den), jnp.exp(-m_t))).T)

        b_last = b_r[:, L - 1:L]
        g = ig_r + b_last - b_r
        m_new = jnp.maximum(b_last + m_prev, jnp.max(g, axis=1, keepdims=True))
        a = jnp.exp(b_last + m_prev - m_new)
        wv = (vt_aug.astype(F32) * jnp.exp(g - m_new)).astype(BF16)
        c_new.append(a * cs + jnp.dot(wv, k, preferred_element_type=F32) * scale)
        m_new_all.append(jnp.broadcast_to(m_new, (1, LANES)))

    h_ref[...] = jnp.concatenate(h_new, axis=1)
    for h in range(M_HEADS):
        c_scr[h] = c_new[h]
    m_scr[0:M_HEADS, :] = jnp.concatenate(m_new_all, axis=0)

    @pl.when(c == nc - 1)
    def _():
        c_out_ref[0] = c_scr[...]
        m_out_ref[0] = m_scr[...]


def _mlstm(qt, km, vt, gc, gr, c0, m0, batch, n_valid, name):
    L = M_CHUNK
    nc = km.shape[0] // (batch * L)
    kern = functools.partial(_mlstm_kernel, n_valid=n_valid)
    rows = pl.BlockSpec((L, M_WIDTH), lambda b, c: (b * nc + c, 0))
    cols = pl.BlockSpec((M_WIDTH, L), lambda b, c: (0, b * nc + c))
    return pl.pallas_call(
        kern,
        out_shape=[jax.ShapeDtypeStruct((batch * nc * L, M_WIDTH), F32),
                   jax.ShapeDtypeStruct((batch, M_HEADS, M_DV + 8, M_DK), F32),
                   jax.ShapeDtypeStruct((batch, 8, LANES), F32)],
        grid=(batch, nc),
        in_specs=[cols, rows, cols,
                  pl.BlockSpec((L, LANES), lambda b, c: (b * nc + c, 0)),
                  pl.BlockSpec((8, L), lambda b, c: (0, b * nc + c)),
                  pl.BlockSpec((M_HEADS, M_DV + 8, M_DK), lambda b, c: (0, 0, 0)),
                  pl.BlockSpec((8, LANES), lambda b, c: (0, 0))],
        out_specs=[rows,
                   pl.BlockSpec((1, M_HEADS, M_DV + 8, M_DK), lambda b, c: (b, 0, 0, 0)),
                   pl.BlockSpec((1, 8, LANES), lambda b, c: (b, 0, 0))],
        scratch_shapes=[pltpu.VMEM((M_HEADS, M_DV + 8, M_DK), F32), pltpu.VMEM((8, LANES), F32)],
        compiler_params=_cparams("arbitrary", "arbitrary"), name=name,
    )(qt, km, vt, gc, gr, c0, m0)


def _outer_f32(a, b):
    ah, am, al = (t.astype(F32) for t in _split3(a))
    bh, bm, bl = (t.astype(F32) for t in _split3(b))
    z = jnp.zeros_like(ah)
    lhs = jnp.concatenate([ah, ah, ah, am, am, al, z, z], axis=0).astype(BF16)
    rhs = jnp.concatenate([bh, bm, bl, bh, bm, bh, z, z], axis=0).astype(BF16)
    return lax.dot_general(lhs, rhs, (((0,), (0,)), ((), ())), preferred_element_type=F32)


def _mlstm_step_kernel(c_ref, n_ref, m_ref, gc_ref, q_ref, k_ref, v_ref,
                       c_out_ref, n_out_ref, m_out_ref, h_ref):
    g = c_ref.shape[0]
    scale = M_DK ** -0.5
    lane_m = lax.broadcasted_iota(I32, (1, LANES), 1)
    for j in range(g):
        m_row = jnp.zeros((1, LANES), F32)
        for h in range(M_HEADS):
            sl = slice(h * M_DK, (h + 1) * M_DK)
            q = q_ref[j:j + 1, sl]
            k = k_ref[j:j + 1, sl] * scale
            v = v_ref[j:j + 1, sl]
            ig = gc_ref[j:j + 1, h:h + 1]
            lf = gc_ref[j:j + 1, M_HEADS + h:M_HEADS + h + 1]
            m = m_ref[j:j + 1, h:h + 1]
            c = c_ref[j, h]
            n = n_ref[j, h:h + 1, :]
            m_t = jnp.maximum(lf + m, ig)
            w = jnp.exp(lf + m - m_t)
            wg = jnp.exp(ig - m_t)
            s = jnp.sum(q * k, axis=1, keepdims=True) * wg
            q8 = jnp.broadcast_to(q, (8, M_DK)).astype(BF16)
            cq = lax.dot_general(q8, c.astype(BF16), (((1,), (1,)), ((), ())), preferred_element_type=F32)[0:1, :]
            den = w * jnp.sum(n * q, axis=1, keepdims=True) + s
            h_ref[j:j + 1, sl] = (w * cq + s * v) / jnp.maximum(jnp.abs(den), jnp.exp(-m_t))
            c_out_ref[j, h] = w * c + _outer_f32(wg * v, k)
            n_out_ref[j, h:h + 1, :] = w * n + wg * k
            m_row = jnp.where(lane_m == h, m_t, m_row)
        m_out_ref[j:j + 1, :] = m_row


def _mlstm_step(c, n, m_pad, gc, q, k, v):
    nb = c.shape[0]
    g = SAMPLE_GROUP
    row = lambda w: pl.BlockSpec((g, w), lambda i: (i, 0))
    return pl.pallas_call(
        _mlstm_step_kernel,
        out_shape=[jax.ShapeDtypeStruct(c.shape, F32), jax.ShapeDtypeStruct(n.shape, F32),
                   jax.ShapeDtypeStruct((nb, LANES), F32), jax.ShapeDtypeStruct((nb, M_WIDTH), F32)],
        grid=(nb // g,),
        in_specs=[pl.BlockSpec((g, M_HEADS, M_DV, M_DK), lambda i: (i, 0, 0, 0)),
                  pl.BlockSpec((g, M_HEADS, M_DK), lambda i: (i, 0, 0)),
                  row(LANES), row(LANES), row(M_WIDTH), row(M_WIDTH), row(M_WIDTH)],
        out_specs=[pl.BlockSpec((g, M_HEADS, M_DV, M_DK), lambda i: (i, 0, 0, 0)),
                   pl.BlockSpec((g, M_HEADS, M_DK), lambda i: (i, 0, 0)),
                   row(LANES), row(M_WIDTH)],
        compiler_params=_cparams("arbitrary"), name="mlstm_step",
    )(c, n, m_pad, gc, q, k, v)


def _swa_kernel(q_ref, kc_ref, kp_ref, vc_ref, vp_ref, km_ref, vm_ref, bias_ref, sink_ref, o_ref):
    j = pl.program_id(1)
    first = j == 0
    blk = WINDOW
    nqb = q_ref.shape[0] // blk
    kp = jnp.where(first, km_ref[...], kp_ref[...])
    vp = jnp.where(first, vm_ref[...], vp_ref[...])
    k = jnp.concatenate([kp, kc_ref[...]], axis=0)
    v = jnp.concatenate([vp, vc_ref[...]], axis=0)
    lane = lax.broadcasted_iota(I32, k.shape, 1)
    zero = jnp.zeros_like(k)
    k_half = (jnp.where(lane < A_HD, k, zero), jnp.where(lane >= A_HD, k, zero))
    v_half = (jnp.where(lane < A_HD, v, zero), jnp.where(lane >= A_HD, v, zero))
    lane_q = lax.broadcasted_iota(I32, (blk, LANES), 1)
    for u in range(nqb):
        rows = slice(u * blk, (u + 1) * blk)
        keys = slice(u * blk, (u + 2) * blk)
        table = jnp.where(first, 0, 1) if u == 0 else 1
        v_stack = jnp.concatenate([v_half[0][keys], v_half[1][keys]], axis=0)
        for p in range(A_GROUP):
            qs = q_ref[rows, p * LANES:(p + 1) * LANES]
            probs, inv = [], []
            for half in range(2):
                hd = HEAD_ORDER[2 * p + half]
                s = lax.dot_general(qs, k_half[half][keys], (((1,), (1,)), ((), ())), preferred_element_type=F32)
                s = s * (A_HD ** -0.5) + bias_ref[table, hd]
                sk = sink_ref[hd:hd + 1, 0:1]
                m = jnp.maximum(jnp.max(s, axis=1, keepdims=True), sk)
                e = jnp.exp(s - m)
                probs.append(e.astype(BF16))
                inv.append(1.0 / (jnp.sum(e, axis=1, keepdims=True) + jnp.exp(sk - m)))
            o = jnp.dot(jnp.concatenate(probs, axis=1), v_stack, preferred_element_type=F32)
            o_ref[rows, p * LANES:(p + 1) * LANES] = o * jnp.where(lane_q < A_HD, inv[0], inv[1])


def _swa(qa, ka, va, kmeta, vmeta, bias, sinks, batch):
    blk = WINDOW
    nqb = SWA_QBLOCKS
    nq = qa.shape[0] // (batch * blk * nqb)
    kv_cur = pl.BlockSpec((nqb * blk, LANES), lambda b, j: (b * nq + j, 0))
    kv_prev = pl.BlockSpec((blk, LANES), lambda b, j: ((b * nq + j) * nqb + jnp.where(j == 0, 0, -1), 0))
    const2 = lambda shape: pl.BlockSpec(shape, lambda b, j: (0, 0))
    return pl.pallas_call(
        _swa_kernel, out_shape=jax.ShapeDtypeStruct((qa.shape[0], A_WIDTH), F32), grid=(batch, nq),
        in_specs=[pl.BlockSpec((nqb * blk, A_WIDTH), lambda b, j: (b * nq + j, 0)),
                  kv_cur, kv_prev, kv_cur, kv_prev, const2((blk, LANES)), const2((blk, LANES)),
                  pl.BlockSpec(bias.shape, lambda b, j: (0, 0, 0, 0)),
                  const2((8, LANES))],
        out_specs=pl.BlockSpec((nqb * blk, A_WIDTH), lambda b, j: (b * nq + j, 0)),
        compiler_params=_cparams("arbitrary", "arbitrary"), name="swa_prompt",
    )(qa, ka, ka, va, va, kmeta, vmeta, bias, sinks)


def _swa_step_kernel(ck_ref, cv_ref, q_ref, k_ref, v_ref, bias_ref, sink_ref, ko_ref, vo_ref, o_ref):
    g = ck_ref.shape[0]
    lane = lax.broadcasted_iota(I32, (A_HEADS, LANES), 1)
    row = lax.broadcasted_iota(I32, (A_HEADS, LANES), 0)
    own_half = (row % 2 == 0) == (lane < A_HD)
    bias = bias_ref[...]
    sk = sink_ref[:, 0:1]
    for j in range(g):
        ko_ref[j, 0:WINDOW - 1, :] = ck_ref[j, 1:WINDOW, :]
        ko_ref[j, WINDOW - 1:WINDOW, :] = k_ref[j:j + 1, :]
        vo_ref[j, 0:WINDOW - 1, :] = cv_ref[j, 1:WINDOW, :]
        vo_ref[j, WINDOW - 1:WINDOW, :] = v_ref[j:j + 1, :]
        kk = ko_ref[j].astype(BF16)
        vv = vo_ref[j].astype(BF16)
        slabs = [q_ref[j:j + 1, p * LANES:(p + 1) * LANES] for p in range(A_GROUP)]
        q8 = jnp.concatenate([slabs[r // 2] for r in range(A_HEADS)], axis=0)
        q8 = jnp.where(own_half, q8, 0.0).astype(BF16)
        s = lax.dot_general(q8, kk, (((1,), (1,)), ((), ())), preferred_element_type=F32)
        s = s * (A_HD ** -0.5) + bias
        m = jnp.maximum(jnp.max(s, axis=1, keepdims=True), sk)
        e = jnp.exp(s - m)
        inv = 1.0 / (jnp.sum(e, axis=1, keepdims=True) + jnp.exp(sk - m))
        o8 = jnp.where(own_half, jnp.dot(e.astype(BF16), vv, preferred_element_type=F32) * inv, 0.0)
        for p in range(A_GROUP):
            o_ref[j:j + 1, p * LANES:(p + 1) * LANES] = o8[2 * p:2 * p + 1, :] + o8[2 * p + 1:2 * p + 2, :]


def _swa_step(ck, cv, q, k, v, bias_rows, sinks):
    nb = ck.shape[0]
    g = SAMPLE_GROUP
    cache = pl.BlockSpec((g, WINDOW, LANES), lambda i: (i, 0, 0))
    row = lambda w: pl.BlockSpec((g, w), lambda i: (i, 0))
    const = lambda a: pl.BlockSpec(a.shape, lambda i: (0, 0))
    return pl.pallas_call(
        _swa_step_kernel,
        out_shape=[jax.ShapeDtypeStruct(ck.shape, F32), jax.ShapeDtypeStruct(cv.shape, F32),
                   jax.ShapeDtypeStruct((nb, A_WIDTH), F32)],
        grid=(nb // g,),
        in_specs=[cache, cache, row(A_WIDTH), row(LANES), row(LANES), const(bias_rows), const(sinks)],
        out_specs=[cache, cache, row(A_WIDTH)],
        compiler_params=_cparams("arbitrary"), name="swa_step",
    )(ck, cv, q, k, v, bias_rows, sinks)


def _layer_norm(z, g, b):
    mu = jnp.mean(z, axis=1, keepdims=True)
    zc = z - mu
    var = jnp.mean(zc * zc, axis=1, keepdims=True)
    return zc * lax.rsqrt(var + LN_EPS) * g + b


def _pack_halves(x):
    w = x.shape[1] // 2
    lo = pltpu.bitcast(x[:, :w].astype(BF16).astype(F32), U32)
    hi = pltpu.bitcast(x[:, w:].astype(BF16).astype(F32), U32)
    return (lo >> 16) | (hi & jnp.uint32(0xFFFF0000))


def _unpack_halves(words):
    lo = pltpu.bitcast(words << 16, F32).astype(BF16)
    hi = pltpu.bitcast(words & jnp.uint32(0xFFFF0000), F32).astype(BF16)
    return lo, hi


def _to_token_tiles(ref, x):
    for q in range(x.shape[1] // LANES):
        ref[:, q, :] = x[:, q * LANES:(q + 1) * LANES]


def _merge_kernel(h_ref, om_ref, att_ref, x_ref, gm_ref, ga_ref, wo_ref, g1_ref, b1_ref, wr_ref, br_ref,
                  x1_ref, xp_ref, tk_ref, cnt_ref):
    @pl.when(pl.program_id(0) == 0)
    def _():
        cnt_ref[...] = jnp.zeros_like(cnt_ref)

    hm = h_ref[...] * _sigmoid(om_ref[...])
    ym = hm * lax.rsqrt(jnp.mean(hm * hm, axis=1, keepdims=True) + LN_EPS) * gm_ref[...]
    att = att_ref[...]
    ya = att * lax.rsqrt(jnp.mean(att * att, axis=1, keepdims=True) + LN_EPS) * ga_ref[...]
    mix = (jnp.dot(ym.astype(BF16), wo_ref[0:M_WIDTH, :], preferred_element_type=F32)
           + jnp.dot(ya.astype(BF16), wo_ref[M_WIDTH:, :], preferred_element_type=F32))
    x1 = _layer_norm(DN_ALPHA * x_ref[...] + mix, g1_ref[...], b1_ref[...])
    x1_ref[...] = x1
    _to_token_tiles(xp_ref, _pack_halves(x1))
    logits = jnp.dot(x1.astype(BF16), wr_ref[...], preferred_element_type=F32) + br_ref[...]
    lane = lax.broadcasted_iota(I32, logits.shape, 1).astype(F32)
    vals, idxs = [], []
    for _ in range(TOP_K):
        mx = jnp.max(logits, axis=1, keepdims=True)
        idx = jnp.min(jnp.where(logits == mx, lane, float(LANES)), axis=1, keepdims=True)
        vals.append(mx)
        idxs.append(idx)
        logits = jnp.where(lane == idx, 2.0 * NEG, logits)
    es = [jnp.exp(vk - vals[0]) for vk in vals]
    tot = es[0] + es[1] + es[2] + es[3]
    tk = jnp.zeros(logits.shape, F32)
    picked = jnp.zeros(logits.shape, F32)
    for k in range(TOP_K):
        tk = jnp.where(lane == float(k), es[k] / tot, tk)
        tk = jnp.where(lane == float(TOP_K + k), idxs[k], tk)
        picked = jnp.where(lane == idxs[k], 1.0, picked)
    tk_ref[...] = tk
    cnt_ref[...] = cnt_ref[...] + jnp.sum(picked, axis=0, keepdims=True)


def _merge(h, om, att, x, gm, ga, wo, g1, b1, wr, br, tile, name):
    t = x.shape[0]
    rows = lambda w: pl.BlockSpec((tile, w), lambda i: (i, 0))
    const = lambda a: pl.BlockSpec(a.shape, lambda i: (0, 0))
    return pl.pallas_call(
        _merge_kernel,
        out_shape=[jax.ShapeDtypeStruct((t, D_MODEL), F32), jax.ShapeDtypeStruct((t, XP_TILE, LANES), U32),
                   jax.ShapeDtypeStruct((t, LANES), F32), jax.ShapeDtypeStruct((8, LANES), F32)],
        grid=(t // tile,),
        in_specs=[rows(M_WIDTH), rows(M_WIDTH), rows(A_WIDTH), rows(D_MODEL), const(gm), const(ga), const(wo),
                  const(g1), const(b1), const(wr), const(br)],
        out_specs=[rows(D_MODEL), pl.BlockSpec((tile, XP_TILE, LANES), lambda i: (i, 0, 0)), rows(LANES),
                   pl.BlockSpec((8, LANES), lambda i: (0, 0))],
        compiler_params=_cparams("arbitrary"), name=name,
    )(h, om, att, x, gm, ga, wo, g1, b1, wr, br)


def _route_kernel(tk_ref, first_ref, strict_ref, dest_ref, next_scr):
    @pl.when(pl.program_id(0) == 0)
    def _():
        next_scr[...] = first_ref[...]

    tk = tk_ref[...]
    lane = lax.broadcasted_iota(I32, tk.shape, 1).astype(F32)
    onehots = [jnp.where(lane == tk[:, TOP_K + k:TOP_K + k + 1], 1.0, 0.0) for k in range(TOP_K)]
    tot = onehots[0] + onehots[1] + onehots[2] + onehots[3]
    row = jnp.dot(strict_ref[...], tot.astype(BF16), preferred_element_type=F32) + next_scr[0:1, :]
    out = jnp.zeros(tk.shape, F32)
    for k in range(TOP_K):
        out = jnp.where(lane == float(k), jnp.sum(onehots[k] * row, axis=1, keepdims=True), out)
    dest_ref[...] = out.astype(I32)
    next_scr[...] = next_scr[...] + jnp.sum(tot, axis=0, keepdims=True)


def _route(tk, first):
    t = tk.shape[0]
    tile = min(RANK_TILE, t)
    strict = jnp.asarray(np.tril(np.ones((tile, tile), np.float32), -1), BF16)
    return pl.pallas_call(
        _route_kernel, out_shape=jax.ShapeDtypeStruct((t, LANES), I32), grid=(t // tile,),
        in_specs=[pl.BlockSpec((tile, LANES), lambda i: (i, 0)), pl.BlockSpec((8, LANES), lambda i: (0, 0)),
                  pl.BlockSpec((tile, tile), lambda i: (0, 0))],
        out_specs=pl.BlockSpec((tile, LANES), lambda i: (i, 0)),
        scratch_shapes=[pltpu.VMEM((8, LANES), F32)],
        compiler_params=_cparams("arbitrary"), name="moe_route",
    )(tk, first, strict)


def _offsets_kernel(cnt_ref, off_ref, be_ref, nu_ref, pad_ref, *, tile):
    cnt = cnt_ref[...]
    nblk = jnp.floor((cnt + float(tile - 1)) * (1.0 / tile))
    r_i = lax.broadcasted_iota(I32, (LANES, LANES), 0)
    c_i = lax.broadcasted_iota(I32, (LANES, LANES), 1)
    incl = jnp.where(r_i <= c_i, 1.0, 0.0).astype(BF16)
    cum = jnp.dot(nblk.astype(BF16), incl, preferred_element_type=F32)
    off = (cum - nblk) * float(tile)
    off_ref[...] = off
    which = lax.broadcasted_iota(I32, cnt.shape, 0)
    pad_ref[...] = jnp.where(which == 0, off + cnt, jnp.where(which == 1, nblk * float(tile) - cnt, 0.0)).astype(I32)
    rows = be_ref.shape[0]
    jb = (lax.broadcasted_iota(I32, (rows, LANES), 0) * LANES + lax.broadcasted_iota(I32, (rows, LANES), 1)).astype(F32)
    acc = jnp.zeros((rows, LANES), F32)
    for e in range(N_EXPERTS):
        acc = acc + jnp.where(jb >= cum[0:1, e:e + 1], 1.0, 0.0)
    be_ref[...] = jnp.minimum(acc, float(N_EXPERTS - 1)).astype(I32)
    nu_ref[...] = jnp.broadcast_to(cum[0:1, N_EXPERTS - 1:N_EXPERTS], nu_ref.shape).astype(I32)


def _offsets(cnt, n_blocks, tile):
    rows = -(-n_blocks // LANES)
    rows = -(-rows // 8) * 8
    return pl.pallas_call(
        functools.partial(_offsets_kernel, tile=tile),
        out_shape=[jax.ShapeDtypeStruct((8, LANES), F32), jax.ShapeDtypeStruct((rows, LANES), I32),
                   jax.ShapeDtypeStruct((8, LANES), I32), jax.ShapeDtypeStruct((8, LANES), I32)],
        name="moe_offsets",
    )(cnt)


def _scatter_rows(dest_ref, xp_ref, xs_ref, sem):
    t = xp_ref.shape[0]

    def row_copy(tok, dst):
        return pltpu.make_async_copy(xp_ref.at[pl.ds(tok, 1)], xs_ref.at[pl.ds(dst, 1)], sem)

    def issue(grp, carry):
        base = pl.multiple_of(grp * ISSUE_GROUP, ISSUE_GROUP)
        for u in range(ISSUE_GROUP):
            for k in range(TOP_K):
                row_copy(base + u, dest_ref[(base + u) * TOP_K + k]).start(priority=k % 2)
        return carry

    lax.fori_loop(0, t // ISSUE_GROUP, issue, 0)
    for k in range(TOP_K):
        pltpu.make_async_copy(xp_ref, xs_ref.at[pl.ds(0, t)], sem).wait()


def _dispatch_kernel(dest_ref, pads_ref, dest2_ref, xp_ref, xp2_ref, xs_ref, sem, zsem, zbuf, *, block_rows):
    zr = zbuf.shape[0]
    n_blocks = xs_ref.shape[0] // block_rows

    @pl.when(pl.program_id(0) == 0)
    def _():
        zbuf[...] = jnp.zeros_like(zbuf)
        used = pads_ref[2 * N_EXPERTS]

        def pieces(e, act):
            start, n = pads_ref[e], pads_ref[N_EXPERTS + e]
            for sh in range(zr.bit_length() - 1, -1, -1):
                b = 1 << sh
                before = lax.shift_left(lax.shift_right_logical(n, sh + 1), sh + 1)

                @pl.when((n & b) != 0)
                def _():
                    act(pltpu.make_async_copy(zbuf.at[pl.ds(0, b)], xs_ref.at[pl.ds(start + before, b)], zsem))

        def tail(jb, act):
            for h in range(block_rows // zr):
                act(pltpu.make_async_copy(zbuf, xs_ref.at[pl.ds(jb * block_rows + h * zr, zr)], zsem))

        for act in (lambda cp: cp.start(), lambda cp: cp.wait()):
            lax.fori_loop(0, N_EXPERTS, lambda e, c: (pieces(e, act), c)[1], 0)
            lax.fori_loop(used, n_blocks, lambda jb, c: (tail(jb, act), c)[1], 0)

    _scatter_rows(dest_ref, xp_ref, xs_ref, sem)

    @pl.when(pl.program_id(0) == pl.num_programs(0) - 1)
    def _():
        _scatter_rows(dest2_ref, xp2_ref, xs_ref, sem)


def _dispatch(dest, pads, dest2, xp, xp2, n_rows, block_rows):
    t = xp.shape[0]
    tile = min(ROW_TILE, t)
    return pl.pallas_call(
        functools.partial(_dispatch_kernel, block_rows=block_rows),
        out_shape=jax.ShapeDtypeStruct((n_rows,) + xp.shape[1:], xp.dtype), grid=(t // tile,),
        in_specs=[pl.BlockSpec((tile * TOP_K,), lambda i: (i,), memory_space=pltpu.SMEM),
                  pl.BlockSpec(memory_space=pltpu.SMEM),
                  pl.BlockSpec(memory_space=pltpu.SMEM),
                  pl.BlockSpec((tile,) + xp.shape[1:], lambda i: (i, 0, 0)),
                  pl.BlockSpec(xp2.shape, lambda i: (0, 0, 0))],
        out_specs=pl.BlockSpec(memory_space=pl.ANY),
        scratch_shapes=[pltpu.SemaphoreType.DMA(()), pltpu.SemaphoreType.DMA(()),
                        pltpu.VMEM((EXPERT_TILE // 2,) + xp.shape[1:], xp.dtype)],
        compiler_params=_cparams("arbitrary"), name="moe_dispatch",
    )(dest, pads, dest2, xp, xp2)


def _expert_kernel(be_ref, nu_ref, xs_ref, w1_ref, b1g_ref, b1l_ref, w2_ref, b2_ref, perm_ref, ys_ref,
                   w1g_scr, w1l_scr, w2_scr, xq_scr, y_scr, sem, osem):
    j = pl.program_id(0)
    active = j < nu_ref[0]
    changed = jnp.logical_or(j == 0, be_ref[j] != be_ref[jnp.maximum(j - 1, 0)])
    tm = y_scr.shape[0]
    slot = lax.rem(j, 2)

    def fetch(blk, slot):
        row0 = pl.multiple_of(blk * tm, tm)
        return [pltpu.make_async_copy(xs_ref.at[pl.ds(row0, tm), q, :], xq_scr.at[slot, q], sem.at[slot])
                for q in range(XP_TILE)]

    def put(blk):
        row0 = pl.multiple_of(blk * tm, tm)
        return [pltpu.make_async_copy(y_scr.at[:, q * LANES:(q + 1) * LANES], ys_ref.at[pl.ds(row0, tm), q, :], osem)
                for q in range(YS_TILE)]

    def emit(y):
        @pl.when(j > 0)
        def _():
            for cp in put(j - 1):
                cp.wait()

        y_scr[...] = _pack_halves(y)
        for cp in put(j):
            cp.start()

    @pl.when(j == 0)
    def _():
        for cp in fetch(0, 0):
            cp.start()

    @pl.when(j + 1 < nu_ref[0])
    def _():
        for cp in fetch(j + 1, 1 - slot):
            cp.start()

    @pl.when(jnp.logical_and(active, changed))
    def _():
        for c in range(2 * D_FF // 256):
            wc = w1_ref[0, :, c * 256:(c + 1) * 256].astype(BF16)
            d = jnp.dot(wc, perm_ref[...], preferred_element_type=F32).astype(BF16)
            w1g_scr[:, c * 128:(c + 1) * 128] = d[:, :128]
            w1l_scr[:, c * 128:(c + 1) * 128] = d[:, 128:]
        for c in range(D_FF // 256):
            w2_scr[c * 256:(c + 1) * 256, :] = w2_ref[0, c * 256:(c + 1) * 256, :].astype(BF16)

    @pl.when(active)
    def _():
        for cp in fetch(j, slot):
            cp.wait()
        lo, hi = _unpack_halves(jnp.concatenate([xq_scr[slot, q] for q in range(XP_TILE)], axis=1))
        xb = jnp.concatenate([lo, hi], axis=1)
        hg = jnp.dot(xb, w1g_scr[...], preferred_element_type=F32) + b1g_ref[0]
        hl = jnp.dot(xb, w1l_scr[...], preferred_element_type=F32) + b1l_ref[0]
        x_glu = jnp.minimum(hg, SWIGLU_LIMIT)
        x_lin = jnp.clip(hl, -SWIGLU_LIMIT, SWIGLU_LIMIT)
        a = x_glu * _sigmoid(SWIGLU_ALPHA * x_glu) * (x_lin + 1.0)
        emit(jnp.dot(a.astype(BF16), w2_scr[...], preferred_element_type=F32) + b2_ref[0])

    @pl.when(jnp.logical_not(active))
    def _():
        emit(jnp.zeros((tm, D_MODEL), F32))

    @pl.when(j == pl.num_programs(0) - 1)
    def _():
        for cp in put(j):
            cp.wait()


def _experts(be, nu, xs, w1, b1g, b1l, w2, b2, perm, tile):
    n_blocks = xs.shape[0] // tile
    grid_spec = pltpu.PrefetchScalarGridSpec(
        num_scalar_prefetch=2, grid=(n_blocks,),
        in_specs=[pl.BlockSpec(memory_space=pl.ANY),
                  pl.BlockSpec((1, D_MODEL, 2 * D_FF), lambda j, be, nu: (be[j], 0, 0)),
                  pl.BlockSpec((1, 1, D_FF), lambda j, be, nu: (be[j], 0, 0)),
                  pl.BlockSpec((1, 1, D_FF), lambda j, be, nu: (be[j], 0, 0)),
                  pl.BlockSpec((1, D_FF, D_MODEL), lambda j, be, nu: (be[j], 0, 0)),
                  pl.BlockSpec((1, 1, D_MODEL), lambda j, be, nu: (be[j], 0, 0)),
                  pl.BlockSpec((256, 256), lambda j, be, nu: (0, 0))],
        out_specs=pl.BlockSpec(memory_space=pl.ANY),
        scratch_shapes=[pltpu.VMEM((D_MODEL, D_FF), BF16), pltpu.VMEM((D_MODEL, D_FF), BF16),
                        pltpu.VMEM((D_FF, D_MODEL), BF16), pltpu.VMEM((2, XP_TILE, tile, LANES), U32),
                        pltpu.VMEM((tile, YS_TILE * LANES), U32), pltpu.SemaphoreType.DMA((2,)),
                        pltpu.SemaphoreType.DMA(())])
    return pl.pallas_call(
        _expert_kernel, out_shape=jax.ShapeDtypeStruct((xs.shape[0], YS_TILE, LANES), U32), grid_spec=grid_spec,
        compiler_params=_cparams("arbitrary"), name="moe_experts",
    )(be, nu, xs, w1, b1g, b1l, w2, b2, perm)


def _combine_kernel(dest_ref, next_ref, ys_ref, tk_ref, x1_ref, g2_ref, b2_ref, out_ref, buf, sem):
    i = pl.program_id(0)
    t = x1_ref.shape[0]
    slot = lax.rem(i, 2)

    def gather(idx_ref, s):
        def issue(grp, carry):
            base = pl.multiple_of(grp * 8, 8)
            for u in range(8):
                for k in range(TOP_K):
                    pltpu.make_async_copy(ys_ref.at[idx_ref[(base + u) * TOP_K + k]],
                                          buf.at[s, k, grp, :, u, :], sem.at[s]).start(priority=k % 2)
            return carry

        lax.fori_loop(0, t // 8, issue, 0)

    @pl.when(i == 0)
    def _():
        gather(dest_ref, 0)

    @pl.when(i + 1 < pl.num_programs(0))
    def _():
        gather(next_ref, 1 - slot)

    for k in range(TOP_K):
        for u in range(8):
            pltpu.make_async_copy(ys_ref.at[pl.ds(0, t // 8)], buf.at[slot, k, :, :, u, :], sem.at[slot]).wait()
    tk = tk_ref[...]
    los, his = [], []
    for q in range(YS_TILE):
        lo = hi = None
        for k in range(TOP_K):
            words = buf[slot, k, :, q].reshape(t, LANES)
            g = tk[:, k:k + 1]
            lo_k = g * pltpu.bitcast(words << 16, F32)
            hi_k = g * pltpu.bitcast(words & jnp.uint32(0xFFFF0000), F32)
            lo = lo_k if lo is None else lo + lo_k
            hi = hi_k if hi is None else hi + hi_k
        los.append(lo)
        his.append(hi)
    ff = jnp.concatenate(los + his, axis=1)
    out_ref[...] = _layer_norm(DN_ALPHA * x1_ref[...] + ff, g2_ref[...], b2_ref[...])


def _combine(dest_flat, ys, tk, x1, g2, b2):
    t = x1.shape[0]
    tile = min(ROW_TILE, t)
    n = t // tile
    return pl.pallas_call(
        _combine_kernel, out_shape=jax.ShapeDtypeStruct((t, D_MODEL), F32), grid=(n,),
        in_specs=[pl.BlockSpec((tile * TOP_K,), lambda i: (i,), memory_space=pltpu.SMEM),
                  pl.BlockSpec((tile * TOP_K,), lambda i: (jnp.minimum(i + 1, n - 1),), memory_space=pltpu.SMEM),
                  pl.BlockSpec(memory_space=pl.ANY),
                  pl.BlockSpec((tile, LANES), lambda i: (i, 0)),
                  pl.BlockSpec((tile, D_MODEL), lambda i: (i, 0)),
                  pl.BlockSpec((1, D_MODEL), lambda i: (0, 0)),
                  pl.BlockSpec((1, D_MODEL), lambda i: (0, 0))],
        out_specs=pl.BlockSpec((tile, D_MODEL), lambda i: (i, 0)),
        scratch_shapes=[pltpu.VMEM((2, TOP_K, tile // 8, YS_TILE, 8, LANES), U32), pltpu.SemaphoreType.DMA((2,))],
        compiler_params=_cparams("arbitrary"), name="moe_combine",
    )(dest_flat, dest_flat, ys, tk, x1, g2, b2)


def _rel_bucket(dist):
    exact = REL_BUCKETS // 2
    d = np.maximum(dist, 0)
    log_b = exact + (np.log(np.maximum(d, 1).astype(np.float32) / np.float32(exact))
                     / np.float32(math.log(REL_MAX_DIST / exact)) * np.float32(REL_BUCKETS - exact)).astype(np.int32)
    return np.where(d < exact, d, np.minimum(log_b, REL_BUCKETS - 1)).astype(np.int32)


def _bias_lookup(table, bucket, valid):
    bucket = jnp.asarray(bucket)[None]
    acc = jnp.zeros((table.shape[1],) + bucket.shape[1:], F32)
    for b in range(REL_BUCKETS):
        acc = jnp.where(bucket == b, table[b].reshape((-1,) + (1,) * (bucket.ndim - 1)), acc)
    return jnp.where(jnp.asarray(valid)[None], acc, NEG)


def _bias_tables(rel_bias):
    table = rel_bias.astype(F32)
    r = np.arange(WINDOW)[:, None]
    c = np.arange(2 * WINDOW)[None, :]
    dist = r + WINDOW - c
    valid = (dist >= 0) & (dist < WINDOW)
    dist0 = np.where(c < N_META, N_META + r - c, dist)
    valid0 = np.where(c < N_META, dist0 < WINDOW, (c >= WINDOW) & valid)
    both = jnp.stack([_bias_lookup(table, _rel_bucket(dist0), valid0), _bias_lookup(table, _rel_bucket(dist), valid)])
    dist_s = WINDOW - 1 - np.arange(WINDOW)
    rows = _bias_lookup(table[:, np.asarray(HEAD_ORDER)], _rel_bucket(dist_s), np.ones_like(dist_s, bool))
    return both, rows


def _perm_heads(a, axis):
    assert HEAD_ORDER == tuple(kv * A_GROUP + g for g in range(A_GROUP) for kv in range(A_KV_HEADS))
    shape = a.shape
    a = a.reshape(shape[:axis] + (A_KV_HEADS, A_GROUP, A_HD) + shape[axis + 1:])
    return jnp.swapaxes(a, axis, axis + 1).reshape(shape)


def _rep_rows(vec, rows=8):
    out = jnp.zeros((rows, LANES), F32)
    return out.at[:vec.shape[0], :].set(jnp.broadcast_to(vec.astype(F32)[:, None], (vec.shape[0], LANES)))


def kernel(x_prompt, x_sample, cache_swa_k, cache_swa_v, state_mlstm_C, state_mlstm_n, state_mlstm_m, meta_tokens, rel_bias, w_in, b_igate, b_fgate, attn_sinks, g_mlstm_out, g_attn_out, w_out, ln1_g, ln1_b, w_router, b_router, w_moe1, b_moe1, w_moe2, b_moe2, ln2_g, ln2_b):
    B, S, _ = x_prompt.shape
    NB = x_sample.shape[0]
    assert x_sample.shape[1] == 1 and w_in.shape[0] == 1
    assert S % PROJ_TILE == 0 and S % M_CHUNK == 0 and S % WINDOW == 0 and NB % SAMPLE_GROUP == 0
    l = 0

    assert IN_WIDTHS == (512, 512, 512, 512, 4, 4, 512, 128, 128)
    bf = lambda a: a.astype(BF16)
    w = w_in[l]
    n_main, n_gate = 4 * M_WIDTH, 2 * M_HEADS
    w_gate = w[:, n_main:n_main + n_gate]
    w_att = w[:, n_main + n_gate:]
    wr = bf(jnp.concatenate([w[:, :n_main], _perm_heads(w_att[:, :A_WIDTH], 1), w_att[:, A_WIDTH:],
                             jnp.pad(w_gate, ((0, 0), (0, LANES - n_gate)))], axis=1))
    wt = bf(jnp.concatenate([w[:, :M_WIDTH], w[:, 2 * M_WIDTH:3 * M_WIDTH], w_gate], axis=1).T)
    b_gate = jnp.concatenate([b_igate[l], b_fgate[l]]).astype(F32)
    brow = jnp.pad(b_gate, (0, LANES - n_gate))[None, :]
    bcol = b_gate[:, None]
    plan_p = ((512, 512, "plain", BF16), (1536, 512, "plain", F32),
              (2048, 512, "plain", BF16), (2560, 128, "plain", BF16), (2688, 128, "plain", BF16),
              (2816, 128, "gate", F32))
    tplan_p = ((0, 512, "plain", BF16), (512, 512, "plain", BF16), (1024, 8, "gate", F32))
    plan_s = ((0, 512, "plain", F32), (512, 512, "plain", F32), (1024, 512, "plain", F32), (1536, 512, "plain", F32),
              (2048, 512, "plain", F32), (2560, 128, "plain", F32), (2688, 128, "plain", F32), (2816, 128, "gate", F32))

    bias_tab, bias_rows = _bias_tables(rel_bias)
    sinks = _rep_rows(attn_sinks[l])
    sinks_step = _rep_rows(attn_sinks[l][np.asarray(HEAD_ORDER)])
    g_m = g_mlstm_out[l].astype(F32)[None, :]
    g_a = _perm_heads(g_attn_out[l].astype(F32), 0)[None, :]
    wo = bf(jnp.concatenate([w_out[l][:M_WIDTH], _perm_heads(w_out[l][M_WIDTH:], 0)], axis=0))
    g1, b1 = ln1_g[l].astype(F32)[None, :], ln1_b[l].astype(F32)[None, :]
    g2, b2 = ln2_g[l].astype(F32)[None, :], ln2_b[l].astype(F32)[None, :]
    w_r = bf(jnp.pad(w_router[l], ((0, 0), (0, LANES - N_EXPERTS))))
    b_r = jnp.pad(b_router[l].astype(F32), (0, LANES - N_EXPERTS), constant_values=NEG)[None, :]
    b1g = b_moe1[l][:, 0::2].astype(F32)[:, None, :]
    b1l = b_moe1[l][:, 1::2].astype(F32)[:, None, :]
    b2e = b_moe2[l].astype(F32)[:, None, :]
    pj = np.zeros((256, 256), np.float32)
    pj[2 * np.arange(128), np.arange(128)] = 1.0
    pj[2 * np.arange(128) + 1, 128 + np.arange(128)] = 1.0
    perm = jnp.asarray(pj, BF16)

    xp2 = x_prompt.reshape(B * S, D_MODEL)
    km, om, qa, ka, va, gc, qt, vt, gr, kv_tail = _proj(
        xp2, wr, wt, brow, bcol, plan_p, tplan_p, (2560, 256), PROJ_TILE, S, "proj_prompt")
    x_meta = jnp.pad(meta_tokens.astype(F32), ((0, M_CHUNK - N_META), (0, 0)))
    km0, _, _, ka0, va0, gc0, qt0, vt0, gr0 = _proj(
        x_meta, wr, wt, brow, bcol, plan_p, tplan_p, None, M_CHUNK, M_CHUNK, "proj_meta")
    xs2 = x_sample.reshape(NB, D_MODEL)
    qm_s, km_s, vm_s, om_s, qa_s, ka_s, va_s, gc_s = _proj(
        xs2, wr, wt, brow, bcol, plan_s, (), None, NB, NB, "proj_sample")

    zero_c = jnp.zeros((M_HEADS, M_DV + 8, M_DK), F32)
    zero_m = jnp.zeros((8, LANES), F32)
    _, c_meta, m_meta = _mlstm(qt0, km0, vt0, gc0, gr0, zero_c, zero_m, 1, N_META, "mlstm_meta")
    h_p, c_p, m_p = _mlstm(qt, km, vt, gc, gr, c_meta[0], m_meta[0], B, M_CHUNK, "mlstm_prompt")
    C_p = c_p[:, :, :M_DV, :]
    n_p = c_p[:, :, M_DV, :]
    m_prompt = m_p[:, :M_HEADS, 0]
    m_pad = jnp.pad(state_mlstm_m[l].astype(F32), ((0, 0), (0, LANES - M_HEADS)))
    C_s, n_s, m_s, h_s = _mlstm_step(state_mlstm_C[l].astype(F32), state_mlstm_n[l].astype(F32), m_pad,
                                     gc_s, qm_s, km_s, vm_s)

    att_p = _swa(qa, ka, va, ka0, va0, bias_tab, sinks, B)
    ck = cache_swa_k[l].reshape(NB, WINDOW, LANES)
    cv = cache_swa_v[l].reshape(NB, WINDOW, LANES)
    k_new, v_new, att_s = _swa_step(ck, cv, qa_s, ka_s, va_s, bias_rows, sinks_step)

    x1_p, xpk_p, tk_p, cnt_p = _merge(h_p, om, att_p, xp2, g_m, g_a, wo, g1, b1, w_r, b_r, MERGE_TILE, "merge_prompt")
    x1_s, xpk_s, tk_s, cnt_s = _merge(h_s, om_s, att_s, xs2, g_m, g_a, wo, g1, b1, w_r, b_r, NB, "merge_sample")

    T_p = B * S
    assert T_p % RANK_TILE == 0 and T_p % ROW_TILE == 0
    n_blocks = -(-((T_p + NB) * TOP_K) // EXPERT_TILE) + N_EXPERTS
    off, be2, nu2, pad = _offsets(cnt_p + cnt_s, n_blocks, EXPERT_TILE)
    pads = jnp.concatenate([pad[0, :N_EXPERTS], pad[1, :N_EXPERTS], nu2[0, :1]])
    dest_p = _route(tk_p, off)[:, :TOP_K].reshape(-1)
    dest_s = _route(tk_s, off + cnt_p)[:, :TOP_K].reshape(-1)
    be = be2.reshape(-1)[:n_blocks]
    nu = nu2[0, :1]
    xs = _dispatch(dest_p, pads, dest_s, xpk_p, xpk_s, n_blocks * EXPERT_TILE, EXPERT_TILE)
    ys = _experts(be, nu, xs, w_moe1[l], b1g, b1l, w_moe2[l], b2e, perm, EXPERT_TILE)
    y_p = _combine(dest_p, ys, tk_p, x1_p, g2, b2)
    y_s = _combine(dest_s, ys, tk_s, x1_s, g2, b2)

    kv_tail = kv_tail.reshape(B, WINDOW, 2, A_KV_HEADS, A_HD)
    dt_k, dt_v = cache_swa_k.dtype, cache_swa_v.dtype
    return (y_p.reshape(B, S, D_MODEL).astype(x_prompt.dtype), y_s.reshape(NB, 1, D_MODEL).astype(x_sample.dtype),
            kv_tail[:, :, 0][None].astype(dt_k), kv_tail[:, :, 1][None].astype(dt_v),
            C_p[None].astype(state_mlstm_C.dtype), n_p[None].astype(state_mlstm_n.dtype),
            m_prompt[None].astype(state_mlstm_m.dtype),
            k_new.reshape(1, NB, WINDOW, A_KV_HEADS, A_HD).astype(dt_k),
            v_new.reshape(1, NB, WINDOW, A_KV_HEADS, A_HD).astype(dt_v),
            C_s[None].astype(state_mlstm_C.dtype), n_s[None].astype(state_mlstm_n.dtype),
            m_s[:, :M_HEADS][None].astype(state_mlstm_m.dtype))
```

```python
import functools
import math

import numpy as np
import jax
import jax.numpy as jnp
from jax import lax
from jax.experimental import pallas as pl
from jax.experimental.pallas import tpu as pltpu

F32 = jnp.float32
BF16 = jnp.bfloat16
I32 = jnp.int32
U32 = jnp.uint32

D_MODEL = 1024
N_META = 16
M_HEADS = 4
M_DK = 128
M_DV = 128
M_WIDTH = M_HEADS * M_DV
A_HD = 64
A_HEADS = 8
A_KV_HEADS = 2
A_GROUP = A_HEADS // A_KV_HEADS
A_WIDTH = A_HEADS * A_HD
WINDOW = 128
REL_BUCKETS = 32
REL_MAX_DIST = 128
N_EXPERTS = 32
TOP_K = 4
D_FF = D_MODEL
SWIGLU_LIMIT = 7.0
SWIGLU_ALPHA = 1.702
DEPTH = 1
DN_ALPHA = (2.0 * DEPTH) ** 0.25
LN_EPS = 1e-5
IN_WIDTHS = (M_WIDTH, M_WIDTH, M_WIDTH, M_WIDTH, M_HEADS, M_HEADS, A_WIDTH, A_KV_HEADS * A_HD, A_KV_HEADS * A_HD)

LANES = 128
NEG = -1e30
VMEM_LIMIT = 56 * 1024 * 1024

M_CHUNK = 256
PROJ_TILE = 512
MERGE_TILE = 512
RANK_TILE = 512
ROW_TILE = 256
EXPERT_TILE = 512
SAMPLE_GROUP = 8
ISSUE_GROUP = 8
SWA_QBLOCKS = 2
XP_TILE = D_MODEL // 2 // LANES
YS_TILE = D_MODEL // 2 // LANES
HEAD_ORDER = (0, 4, 1, 5, 2, 6, 3, 7)


def _cparams(*sem):
    return pltpu.CompilerParams(dimension_semantics=sem, vmem_limit_bytes=VMEM_LIMIT)


def _log_sigmoid(x):
    return jnp.minimum(x, 0.0) - jnp.log1p(jnp.exp(-jnp.abs(x)))


def _sigmoid(x):
    return 1.0 / (1.0 + jnp.exp(-x))


def _proj_kernel(x_ref, wr_ref, wt_ref, brow_ref, bcol_ref, *outs, row_plan, t_plan, tail_cols):
    xb = x_ref[...].astype(BF16)
    tm = xb.shape[0]
    o = 0
    for (c0, width, kind, _) in row_plan:
        r = jnp.dot(xb, wr_ref[:, c0:c0 + width], preferred_element_type=F32)
        if kind == "gate":
            r = r + brow_ref[...]
            lane = lax.broadcasted_iota(I32, r.shape, 1)
            r = jnp.where(lane < M_HEADS, r, _log_sigmoid(r))
        outs[o][...] = r.astype(outs[o].dtype)
        o += 1
    for (r0, nrows, kind, _) in t_plan:
        r = lax.dot_general(wt_ref[r0:r0 + nrows, :], xb, (((1,), (1,)), ((), ())), preferred_element_type=F32)
        if kind == "gate":
            r = r + bcol_ref[...]
            row = lax.broadcasted_iota(I32, r.shape, 0)
            r = jnp.where(row < M_HEADS, r, _log_sigmoid(r))
        outs[o][...] = r.astype(outs[o].dtype)
        o += 1
    if tail_cols is not None:
        c0, width = tail_cols
        outs[o][...] = jnp.dot(xb[tm - WINDOW:, :], wr_ref[:, c0:c0 + width], preferred_element_type=F32)


def _proj(x, wr, wt, brow, bcol, row_plan, t_plan, tail_cols, tile, rows_per_group, name):
    t = x.shape[0]
    nt = t // tile
    out_shape, out_specs = [], []
    for (_, width, _, dt) in row_plan:
        out_shape.append(jax.ShapeDtypeStruct((t, width), dt))
        out_specs.append(pl.BlockSpec((tile, width), lambda i: (i, 0)))
    for (_, nrows, _, dt) in t_plan:
        out_shape.append(jax.ShapeDtypeStruct((nrows, t), dt))
        out_specs.append(pl.BlockSpec((nrows, tile), lambda i: (0, i)))
    if tail_cols is not None:
        tiles_per_group = rows_per_group // tile
        out_shape.append(jax.ShapeDtypeStruct((t // rows_per_group * WINDOW, tail_cols[1]), F32))
        out_specs.append(pl.BlockSpec((WINDOW, tail_cols[1]), lambda i: (i // tiles_per_group, 0)))
    kern = functools.partial(_proj_kernel, row_plan=row_plan, t_plan=t_plan, tail_cols=tail_cols)
    return pl.pallas_call(
        kern, out_shape=out_shape, grid=(nt,),
        in_specs=[pl.BlockSpec((tile, D_MODEL), lambda i: (i, 0)),
                  pl.BlockSpec(wr.shape, lambda i: (0, 0)),
                  pl.BlockSpec(wt.shape, lambda i: (0, 0)),
                  pl.BlockSpec(brow.shape, lambda i: (0, 0)),
                  pl.BlockSpec(bcol.shape, lambda i: (0, 0))],
        out_specs=out_specs, compiler_params=_cparams("arbitrary"), name=name,
    )(x, wr, wt, brow, bcol)


def _split3(a):
    hi = a.astype(BF16)
    r1 = a - hi.astype(F32)
    mid = r1.astype(BF16)
    lo = (r1 - mid.astype(F32)).astype(BF16)
    return hi, mid, lo


def _mlstm_kernel(qt_ref, k_ref, vt_ref, gc_ref, gr_ref, c0_ref, m0_ref, h_ref, c_out_ref, m_out_ref,
                  c_scr, m_scr, *, n_valid):
    c = pl.program_id(1)
    nc = pl.num_programs(1)
    L = k_ref.shape[0]

    @pl.when(c == 0)
    def _():
        c_scr[...] = c0_ref[...]
        m_scr[...] = m0_ref[...]

    gc = gc_ref[...]
    gr = gr_ref[...]
    if n_valid < L:
        rowc = lax.broadcasted_iota(I32, gc.shape, 0)
        lanec = lax.broadcasted_iota(I32, gc.shape, 1)
        gc = jnp.where(rowc < n_valid, gc, jnp.where(lanec < M_HEADS, NEG, 0.0))
        rowr = lax.broadcasted_iota(I32, gr.shape, 0)
        colr = lax.broadcasted_iota(I32, gr.shape, 1)
        gr = jnp.where(colr < n_valid, gr, jnp.where(rowr < M_HEADS, NEG, 0.0))
    r_i = lax.broadcasted_iota(I32, (L, L), 0)
    c_i = lax.broadcasted_iota(I32, (L, L), 1)
    upper = r_i <= c_i
    tril = jnp.where(c_i <= r_i, 1.0, 0.0).astype(BF16)
    triu = jnp.where(upper, 1.0, 0.0).astype(BF16)
    b_cols = sum(jnp.dot(tril, part, preferred_element_type=F32) for part in _split3(gc))
    b_rows = sum(jnp.dot(part, triu, preferred_element_type=F32) for part in _split3(gr))
    scale = M_DK ** -0.5
    ones_rows = jnp.where(lax.broadcasted_iota(I32, (8, L), 0) == 0, 1.0, 0.0).astype(BF16)

    m_all = m_scr[...]
    c_all = [c_scr[h] for h in range(M_HEADS)]
    h_new, c_new, m_new_all = [], [], []
    for h in range(M_HEADS):
        sl = slice(h * M_DK, (h + 1) * M_DK)
        qt = qt_ref[sl, :]
        k = k_ref[:, sl]
        vt_aug = jnp.concatenate([vt_ref[sl, :], ones_rows], axis=0)
        r_c = gc[:, h:h + 1] - b_cols[:, M_HEADS + h:M_HEADS + h + 1]
        ig_r = gr[h:h + 1, :]
        b_r = b_rows[M_HEADS + h:M_HEADS + h + 1, :]
        m_prev = m_all[h:h + 1, 0:1]
        cs = c_all[h]

        dt = jnp.where(upper, b_r + r_c, NEG)
        m_t = jnp.maximum(b_r + m_prev, jnp.max(dt, axis=0, keepdims=True))
        st = jnp.dot(k, qt, preferred_element_type=F32) * (scale * jnp.exp(dt - m_t))
        inter = jnp.dot(cs.astype(BF16), qt, preferred_element_type=F32)
        intra = jnp.dot(vt_aug, st.astype(BF16), preferred_element_type=F32)
        nd = jnp.exp(b_r + m_prev - m_t) * inter + intra
        den = nd[M_DV:M_DV + 1, :]
        h_new.append((nd[:M_DV, :] / jnp.maximum(jnp.abs(den), jnp.exp(-m_t))).T)

        b_last = b_r[:, L - 1:L]
        g = ig_r + b_last - b_r
        m_new = jnp.maximum(b_last + m_prev, jnp.max(g, axis=1, keepdims=True))
        a = jnp.exp(b_last + m_prev - m_new)
        wv = (vt_aug.astype(F32) * jnp.exp(g - m_new)).astype(BF16)
        c_new.append(a * cs + jnp.dot(wv, k, preferred_element_type=F32) * scale)
        m_new_all.append(jnp.broadcast_to(m_new, (1, LANES)))

    h_ref[...] = jnp.concatenate(h_new, axis=1)
    for h in range(M_HEADS):
        c_scr[h] = c_new[h]
    m_scr[0:M_HEADS, :] = jnp.concatenate(m_new_all, axis=0)

    @pl.when(c == nc - 1)
    def _():
        c_out_ref[0] = c_scr[...]
        m_out_ref[0] = m_scr[...]


def _mlstm(qt, km, vt, gc, gr, c0, m0, batch, n_valid, name):
    L = M_CHUNK
    nc = km.shape[0] // (batch * L)
    kern = functools.partial(_mlstm_kernel, n_valid=n_valid)
    rows = pl.BlockSpec((L, M_WIDTH), lambda b, c: (b * nc + c, 0))
    cols = pl.BlockSpec((M_WIDTH, L), lambda b, c: (0, b * nc + c))
    return pl.pallas_call(
        kern,
        out_shape=[jax.ShapeDtypeStruct((batch * nc * L, M_WIDTH), F32),
                   jax.ShapeDtypeStruct((batch, M_HEADS, M_DV + 8, M_DK), F32),
                   jax.ShapeDtypeStruct((batch, 8, LANES), F32)],
        grid=(batch, nc),
        in_specs=[cols, rows, cols,
                  pl.BlockSpec((L, LANES), lambda b, c: (b * nc + c, 0)),
                  pl.BlockSpec((8, L), lambda b, c: (0, b * nc + c)),
                  pl.BlockSpec((M_HEADS, M_DV + 8, M_DK), lambda b, c: (0, 0, 0)),
                  pl.BlockSpec((8, LANES), lambda b, c: (0, 0))],
        out_specs=[rows,
                   pl.BlockSpec((1, M_HEADS, M_DV + 8, M_DK), lambda b, c: (b, 0, 0, 0)),
                   pl.BlockSpec((1, 8, LANES), lambda b, c: (b, 0, 0))],
        scratch_shapes=[pltpu.VMEM((M_HEADS, M_DV + 8, M_DK), F32), pltpu.VMEM((8, LANES), F32)],
        compiler_params=_cparams("arbitrary", "arbitrary"), name=name,
    )(qt, km, vt, gc, gr, c0, m0)


def _outer_f32(a, b):
    ah, am, al = (t.astype(F32) for t in _split3(a))
    bh, bm, bl = (t.astype(F32) for t in _split3(b))
    z = jnp.zeros_like(ah)
    lhs = jnp.concatenate([ah, ah, ah, am, am, al, z, z], axis=0).astype(BF16)
    rhs = jnp.concatenate([bh, bm, bl, bh, bm, bh, z, z], axis=0).astype(BF16)
    return lax.dot_general(lhs, rhs, (((0,), (0,)), ((), ())), preferred_element_type=F32)


def _mlstm_step_kernel(c_ref, n_ref, m_ref, gc_ref, q_ref, k_ref, v_ref,
                       c_out_ref, n_out_ref, m_out_ref, h_ref):
    g = c_ref.shape[0]
    scale = M_DK ** -0.5
    lane_m = lax.broadcasted_iota(I32, (1, LANES), 1)
    for j in range(g):
        m_row = jnp.zeros((1, LANES), F32)
        for h in range(M_HEADS):
            sl = slice(h * M_DK, (h + 1) * M_DK)
            q = q_ref[j:j + 1, sl]
            k = k_ref[j:j + 1, sl] * scale
            v = v_ref[j:j + 1, sl]
            ig = gc_ref[j:j + 1, h:h + 1]
            lf = gc_ref[j:j + 1, M_HEADS + h:M_HEADS + h + 1]
            m = m_ref[j:j + 1, h:h + 1]
            c = c_ref[j, h]
            n = n_ref[j, h:h + 1, :]
            m_t = jnp.maximum(lf + m, ig)
            w = jnp.exp(lf + m - m_t)
            wg = jnp.exp(ig - m_t)
            s = jnp.sum(q * k, axis=1, keepdims=True) * wg
            q8 = jnp.broadcast_to(q, (8, M_DK)).astype(BF16)
            cq = lax.dot_general(q8, c.astype(BF16), (((1,), (1,)), ((), ())), preferred_element_type=F32)[0:1, :]
            den = w * jnp.sum(n * q, axis=1, keepdims=True) + s
            h_ref[j:j + 1, sl] = (w * cq + s * v) / jnp.maximum(jnp.abs(den), jnp.exp(-m_t))
            c_out_ref[j, h] = w * c + _outer_f32(wg * v, k)
            n_out_ref[j, h:h + 1, :] = w * n + wg * k
            m_row = jnp.where(lane_m == h, m_t, m_row)
        m_out_ref[j:j + 1, :] = m_row


def _mlstm_step(c, n, m_pad, gc, q, k, v):
    nb = c.shape[0]
    g = SAMPLE_GROUP
    row = lambda w: pl.BlockSpec((g, w), lambda i: (i, 0))
    return pl.pallas_call(
        _mlstm_step_kernel,
        out_shape=[jax.ShapeDtypeStruct(c.shape, F32), jax.ShapeDtypeStruct(n.shape, F32),
                   jax.ShapeDtypeStruct((nb, LANES), F32), jax.ShapeDtypeStruct((nb, M_WIDTH), F32)],
        grid=(nb // g,),
        in_specs=[pl.BlockSpec((g, M_HEADS, M_DV, M_DK), lambda i: (i, 0, 0, 0)),
                  pl.BlockSpec((g, M_HEADS, M_DK), lambda i: (i, 0, 0)),
                  row(LANES), row(LANES), row(M_WIDTH), row(M_WIDTH), row(M_WIDTH)],
        out_specs=[pl.BlockSpec((g, M_HEADS, M_DV, M_DK), lambda i: (i, 0, 0, 0)),
                   pl.BlockSpec((g, M_HEADS, M_DK), lambda i: (i, 0, 0)),
                   row(LANES), row(M_WIDTH)],
        compiler_params=_cparams("arbitrary"), name="mlstm_step",
    )(c, n, m_pad, gc, q, k, v)


def _swa_kernel(qt_ref, kc_ref, kp_ref, vtc_ref, vtp_ref, km_ref, vtm_ref, bias_ref, sink_ref, o_ref):
    j = pl.program_id(1)
    first = j == 0
    blk = WINDOW
    nqb = qt_ref.shape[1] // blk
    kp = jnp.where(first, km_ref[...], kp_ref[...])
    vtp = jnp.where(first, vtm_ref[...], vtp_ref[...])
    k = jnp.concatenate([kp, kc_ref[...]], axis=0)
    vt = jnp.concatenate([vtp, vtc_ref[...]], axis=1)
    row_v = lax.broadcasted_iota(I32, vt.shape, 0)
    zero_v = jnp.zeros_like(vt)
    vt_half = (jnp.where(row_v < A_HD, vt, zero_v), jnp.where(row_v >= A_HD, vt, zero_v))
    row_q = lax.broadcasted_iota(I32, (LANES, blk), 0)
    lo_rows = row_q < A_HD
    for u in range(nqb):
        cols = slice(u * blk, (u + 1) * blk)
        keys = slice(u * blk, (u + 2) * blk)
        table = jnp.where(first, 0, 1) if u == 0 else 1
        vt_stack = jnp.concatenate([vt_half[0][:, keys], vt_half[1][:, keys]], axis=1)
        outs = []
        for p in range(A_GROUP):
            qs = qt_ref[p * LANES:(p + 1) * LANES, cols]
            zero_q = jnp.zeros_like(qs)
            probs, inv = [], []
            for half in range(2):
                hd = HEAD_ORDER[2 * p + half]
                q_own = jnp.where(lo_rows if half == 0 else jnp.logical_not(lo_rows), qs, zero_q)
                s = jnp.dot(k[keys], q_own, preferred_element_type=F32) + bias_ref[table, hd]
                sk = sink_ref[hd:hd + 1, 0:1]
                m = jnp.maximum(jnp.max(s, axis=0, keepdims=True), sk)
                e = jnp.exp(s - m)
                probs.append(e.astype(BF16))
                inv.append(1.0 / (jnp.sum(e, axis=0, keepdims=True) + jnp.exp(sk - m)))
            ot = jnp.dot(vt_stack, jnp.concatenate(probs, axis=0), preferred_element_type=F32)
            outs.append((ot * jnp.where(lo_rows, inv[0], inv[1])).T)
        o_ref[cols, :] = jnp.concatenate(outs, axis=1)


def _swa(qat, ka, vat, kmeta, vtmeta, bias_t, sinks, batch):
    blk = WINDOW
    nqb = SWA_QBLOCKS
    t = ka.shape[0]
    nq = t // (batch * blk * nqb)
    prev = lambda b, j: (b * nq + j) * nqb + jnp.where(j == 0, 0, -1)
    const2 = lambda shape: pl.BlockSpec(shape, lambda b, j: (0, 0))
    return pl.pallas_call(
        _swa_kernel, out_shape=jax.ShapeDtypeStruct((t, A_WIDTH), F32), grid=(batch, nq),
        in_specs=[pl.BlockSpec((A_WIDTH, nqb * blk), lambda b, j: (0, b * nq + j)),
                  pl.BlockSpec((nqb * blk, LANES), lambda b, j: (b * nq + j, 0)),
                  pl.BlockSpec((blk, LANES), lambda b, j: (prev(b, j), 0)),
                  pl.BlockSpec((LANES, nqb * blk), lambda b, j: (0, b * nq + j)),
                  pl.BlockSpec((LANES, blk), lambda b, j: (0, prev(b, j))),
                  const2((blk, LANES)), const2((LANES, blk)),
                  pl.BlockSpec(bias_t.shape, lambda b, j: (0, 0, 0, 0)),
                  const2((8, LANES))],
        out_specs=pl.BlockSpec((nqb * blk, A_WIDTH), lambda b, j: (b * nq + j, 0)),
        compiler_params=_cparams("arbitrary", "arbitrary"), name="swa_prompt",
    )(qat, ka, ka, vat, vat, kmeta, vtmeta, bias_t, sinks)


def _swa_step_kernel(ck_ref, cv_ref, q_ref, k_ref, v_ref, bias_ref, sink_ref, ko_ref, vo_ref, o_ref):
    g = ck_ref.shape[0]
    lane = lax.broadcasted_iota(I32, (A_HEADS, LANES), 1)
    row = lax.broadcasted_iota(I32, (A_HEADS, LANES), 0)
    own_half = (row % 2 == 0) == (lane < A_HD)
    bias = bias_ref[...]
    sk = sink_ref[:, 0:1]
    for j in range(g):
        ko_ref[j, 0:WINDOW - 1, :] = ck_ref[j, 1:WINDOW, :]
        ko_ref[j, WINDOW - 1:WINDOW, :] = k_ref[j:j + 1, :]
        vo_ref[j, 0:WINDOW - 1, :] = cv_ref[j, 1:WINDOW, :]
        vo_ref[j, WINDOW - 1:WINDOW, :] = v_ref[j:j + 1, :]
        kk = ko_ref[j].astype(BF16)
        vv = vo_ref[j].astype(BF16)
        slabs = [q_ref[j:j + 1, p * LANES:(p + 1) * LANES] for p in range(A_GROUP)]
        q8 = jnp.concatenate([slabs[r // 2] for r in range(A_HEADS)], axis=0)
        q8 = jnp.where(own_half, q8, 0.0).astype(BF16)
        s = lax.dot_general(q8, kk, (((1,), (1,)), ((), ())), preferred_element_type=F32)
        s = s + bias
        m = jnp.maximum(jnp.max(s, axis=1, keepdims=True), sk)
        e = jnp.exp(s - m)
        inv = 1.0 / (jnp.sum(e, axis=1, keepdims=True) + jnp.exp(sk - m))
        o8 = jnp.where(own_half, jnp.dot(e.astype(BF16), vv, preferred_element_type=F32) * inv, 0.0)
        for p in range(A_GROUP):
            o_ref[j:j + 1, p * LANES:(p + 1) * LANES] = o8[2 * p:2 * p + 1, :] + o8[2 * p + 1:2 * p + 2, :]


def _swa_step(ck, cv, q, k, v, bias_rows, sinks):
    nb = ck.shape[0]
    g = SAMPLE_GROUP
    cache = pl.BlockSpec((g, WINDOW, LANES), lambda i: (i, 0, 0))
    row = lambda w: pl.BlockSpec((g, w), lambda i: (i, 0))
    const = lambda a: pl.BlockSpec(a.shape, lambda i: (0, 0))
    return pl.pallas_call(
        _swa_step_kernel,
        out_shape=[jax.ShapeDtypeStruct(ck.shape, F32), jax.ShapeDtypeStruct(cv.shape, F32),
                   jax.ShapeDtypeStruct((nb, A_WIDTH), F32)],
        grid=(nb // g,),
        in_specs=[cache, cache, row(A_WIDTH), row(LANES), row(LANES), const(bias_rows), const(sinks)],
        out_specs=[cache, cache, row(A_WIDTH)],
        compiler_params=_cparams("arbitrary"), name="swa_step",
    )(ck, cv, q, k, v, bias_rows, sinks)


def _layer_norm(z, g, b):
    mu = jnp.mean(z, axis=1, keepdims=True)
    zc = z - mu
    var = jnp.mean(zc * zc, axis=1, keepdims=True)
    return zc * lax.rsqrt(var + LN_EPS) * g + b


def _pack_halves(x):
    w = x.shape[1] // 2
    lo = pltpu.bitcast(x[:, :w].astype(BF16).astype(F32), U32)
    hi = pltpu.bitcast(x[:, w:].astype(BF16).astype(F32), U32)
    return (lo >> 16) | (hi & jnp.uint32(0xFFFF0000))


def _unpack_halves(words):
    lo = pltpu.bitcast(words << 16, F32).astype(BF16)
    hi = pltpu.bitcast(words & jnp.uint32(0xFFFF0000), F32).astype(BF16)
    return lo, hi


def _to_token_tiles(ref, x):
    for q in range(x.shape[1] // LANES):
        ref[:, q, :] = x[:, q * LANES:(q + 1) * LANES]


def _merge_kernel(h_ref, om_ref, att_ref, x_ref, gm_ref, ga_ref, wo_ref, g1_ref, b1_ref, wr_ref, br_ref,
                  x1_ref, xp_ref, tk_ref, cnt_ref):
    @pl.when(pl.program_id(0) == 0)
    def _():
        cnt_ref[...] = jnp.zeros_like(cnt_ref)

    hm = h_ref[...] * _sigmoid(om_ref[...])
    ym = hm * lax.rsqrt(jnp.mean(hm * hm, axis=1, keepdims=True) + LN_EPS) * gm_ref[...]
    att = att_ref[...]
    ya = att * lax.rsqrt(jnp.mean(att * att, axis=1, keepdims=True) + LN_EPS) * ga_ref[...]
    mix = (jnp.dot(ym.astype(BF16), wo_ref[0:M_WIDTH, :], preferred_element_type=F32)
           + jnp.dot(ya.astype(BF16), wo_ref[M_WIDTH:, :], preferred_element_type=F32))
    x1 = _layer_norm(DN_ALPHA * x_ref[...] + mix, g1_ref[...], b1_ref[...])
    x1_ref[...] = x1
    _to_token_tiles(xp_ref, _pack_halves(x1))
    logits = jnp.dot(x1.astype(BF16), wr_ref[...], preferred_element_type=F32) + br_ref[...]
    lane = lax.broadcasted_iota(I32, logits.shape, 1).astype(F32)
    vals, idxs = [], []
    for _ in range(TOP_K):
        mx = jnp.max(logits, axis=1, keepdims=True)
        idx = jnp.min(jnp.where(logits == mx, lane, float(LANES)), axis=1, keepdims=True)
        vals.append(mx)
        idxs.append(idx)
        logits = jnp.where(lane == idx, 2.0 * NEG, logits)
    es = [jnp.exp(vk - vals[0]) for vk in vals]
    tot = es[0] + es[1] + es[2] + es[3]
    tk = jnp.zeros(logits.shape, F32)
    picked = jnp.zeros(logits.shape, F32)
    for k in range(TOP_K):
        tk = jnp.where(lane == float(k), es[k] / tot, tk)
        tk = jnp.where(lane == float(TOP_K + k), idxs[k], tk)
        picked = jnp.where(lane == idxs[k], 1.0, picked)
    tk_ref[...] = tk
    cnt_ref[...] = cnt_ref[...] + jnp.sum(picked, axis=0, keepdims=True)


def _merge(h, om, att, x, gm, ga, wo, g1, b1, wr, br, tile, name):
    t = x.shape[0]
    rows = lambda w: pl.BlockSpec((tile, w), lambda i: (i, 0))
    const = lambda a: pl.BlockSpec(a.shape, lambda i: (0, 0))
    return pl.pallas_call(
        _merge_kernel,
        out_shape=[jax.ShapeDtypeStruct((t, D_MODEL), F32), jax.ShapeDtypeStruct((t, XP_TILE, LANES), U32),
                   jax.ShapeDtypeStruct((t, LANES), F32), jax.ShapeDtypeStruct((8, LANES), F32)],
        grid=(t // tile,),
        in_specs=[rows(M_WIDTH), rows(M_WIDTH), rows(A_WIDTH), rows(D_MODEL), const(gm), const(ga), const(wo),
                  const(g1), const(b1), const(wr), const(br)],
        out_specs=[rows(D_MODEL), pl.BlockSpec((tile, XP_TILE, LANES), lambda i: (i, 0, 0)), rows(LANES),
                   pl.BlockSpec((8, LANES), lambda i: (0, 0))],
        compiler_params=_cparams("arbitrary"), name=name,
    )(h, om, att, x, gm, ga, wo, g1, b1, wr, br)


def _route_kernel(tk_ref, first_ref, strict_ref, dest_ref, next_scr):
    @pl.when(pl.program_id(0) == 0)
    def _():
        next_scr[...] = first_ref[...]

    tk = tk_ref[...]
    lane = lax.broadcasted_iota(I32, tk.shape, 1).astype(F32)
    onehots = [jnp.where(lane == tk[:, TOP_K + k:TOP_K + k + 1], 1.0, 0.0) for k in range(TOP_K)]
    tot = onehots[0] + onehots[1] + onehots[2] + onehots[3]
    row = jnp.dot(strict_ref[...], tot.astype(BF16), preferred_element_type=F32) + next_scr[0:1, :]
    out = jnp.zeros(tk.shape, F32)
    for k in range(TOP_K):
        out = jnp.where(lane == float(k), jnp.sum(onehots[k] * row, axis=1, keepdims=True), out)
    dest_ref[...] = out.astype(I32)
    next_scr[...] = next_scr[...] + jnp.sum(tot, axis=0, keepdims=True)


def _route(tk, first):
    t = tk.shape[0]
    tile = min(RANK_TILE, t)
    strict = jnp.asarray(np.tril(np.ones((tile, tile), np.float32), -1), BF16)
    return pl.pallas_call(
        _route_kernel, out_shape=jax.ShapeDtypeStruct((t, LANES), I32), grid=(t // tile,),
        in_specs=[pl.BlockSpec((tile, LANES), lambda i: (i, 0)), pl.BlockSpec((8, LANES), lambda i: (0, 0)),
                  pl.BlockSpec((tile, tile), lambda i: (0, 0))],
        out_specs=pl.BlockSpec((tile, LANES), lambda i: (i, 0)),
        scratch_shapes=[pltpu.VMEM((8, LANES), F32)],
        compiler_params=_cparams("arbitrary"), name="moe_route",
    )(tk, first, strict)


def _offsets_kernel(cnt_ref, off_ref, be_ref, nu_ref, pad_ref, *, tile):
    cnt = cnt_ref[...]
    nblk = jnp.floor((cnt + float(tile - 1)) * (1.0 / tile))
    r_i = lax.broadcasted_iota(I32, (LANES, LANES), 0)
    c_i = lax.broadcasted_iota(I32, (LANES, LANES), 1)
    incl = jnp.where(r_i <= c_i, 1.0, 0.0).astype(BF16)
    cum = jnp.dot(nblk.astype(BF16), incl, preferred_element_type=F32)
    off = (cum - nblk) * float(tile)
    off_ref[...] = off
    which = lax.broadcasted_iota(I32, cnt.shape, 0)
    pad_ref[...] = jnp.where(which == 0, off + cnt, jnp.where(which == 1, nblk * float(tile) - cnt, 0.0)).astype(I32)
    rows = be_ref.shape[0]
    jb = (lax.broadcasted_iota(I32, (rows, LANES), 0) * LANES + lax.broadcasted_iota(I32, (rows, LANES), 1)).astype(F32)
    acc = jnp.zeros((rows, LANES), F32)
    for e in range(N_EXPERTS):
        acc = acc + jnp.where(jb >= cum[0:1, e:e + 1], 1.0, 0.0)
    be_ref[...] = jnp.minimum(acc, float(N_EXPERTS - 1)).astype(I32)
    nu_ref[...] = jnp.broadcast_to(cum[0:1, N_EXPERTS - 1:N_EXPERTS], nu_ref.shape).astype(I32)


def _offsets(cnt, n_blocks, tile):
    rows = -(-n_blocks // LANES)
    rows = -(-rows // 8) * 8
    return pl.pallas_call(
        functools.partial(_offsets_kernel, tile=tile),
        out_shape=[jax.ShapeDtypeStruct((8, LANES), F32), jax.ShapeDtypeStruct((rows, LANES), I32),
                   jax.ShapeDtypeStruct((8, LANES), I32), jax.ShapeDtypeStruct((8, LANES), I32)],
        name="moe_offsets",
    )(cnt)


def _scatter_rows(dest_ref, xp_ref, xs_ref, sem):
    t = xp_ref.shape[0]

    def row_copy(tok, dst):
        return pltpu.make_async_copy(xp_ref.at[pl.ds(tok, 1)], xs_ref.at[pl.ds(dst, 1)], sem)

    def issue(grp, carry):
        base = pl.multiple_of(grp * ISSUE_GROUP, ISSUE_GROUP)
        for u in range(ISSUE_GROUP):
            for k in range(TOP_K):
                row_copy(base + u, dest_ref[(base + u) * TOP_K + k]).start(priority=k % 2)
        return carry

    lax.fori_loop(0, t // ISSUE_GROUP, issue, 0)
    for k in range(TOP_K):
        pltpu.make_async_copy(xp_ref, xs_ref.at[pl.ds(0, t)], sem).wait()


def _dispatch_kernel(dest_ref, pads_ref, dest2_ref, xp_ref, xp2_ref, xs_ref, sem, zsem, zbuf, *, block_rows):
    zr = zbuf.shape[0]
    n_blocks = xs_ref.shape[0] // block_rows

    @pl.when(pl.program_id(0) == 0)
    def _():
        zbuf[...] = jnp.zeros_like(zbuf)
        used = pads_ref[2 * N_EXPERTS]

        def pieces(e, act):
            start, n = pads_ref[e], pads_ref[N_EXPERTS + e]
            for sh in range(zr.bit_length() - 1, -1, -1):
                b = 1 << sh
                before = lax.shift_left(lax.shift_right_logical(n, sh + 1), sh + 1)

                @pl.when((n & b) != 0)
                def _():
                    act(pltpu.make_async_copy(zbuf.at[pl.ds(0, b)], xs_ref.at[pl.ds(start + before, b)], zsem))

        def tail(jb, act):
            for h in range(block_rows // zr):
                act(pltpu.make_async_copy(zbuf, xs_ref.at[pl.ds(jb * block_rows + h * zr, zr)], zsem))

        for act in (lambda cp: cp.start(), lambda cp: cp.wait()):
            lax.fori_loop(0, N_EXPERTS, lambda e, c: (pieces(e, act), c)[1], 0)
            lax.fori_loop(used, n_blocks, lambda jb, c: (tail(jb, act), c)[1], 0)

    _scatter_rows(dest_ref, xp_ref, xs_ref, sem)

    @pl.when(pl.program_id(0) == pl.num_programs(0) - 1)
    def _():
        _scatter_rows(dest2_ref, xp2_ref, xs_ref, sem)


def _dispatch(dest, pads, dest2, xp, xp2, n_rows, block_rows):
    t = xp.shape[0]
    tile = min(ROW_TILE, t)
    return pl.pallas_call(
        functools.partial(_dispatch_kernel, block_rows=block_rows),
        out_shape=jax.ShapeDtypeStruct((n_rows,) + xp.shape[1:], xp.dtype), grid=(t // tile,),
        in_specs=[pl.BlockSpec((tile * TOP_K,), lambda i: (i,), memory_space=pltpu.SMEM),
                  pl.BlockSpec(memory_space=pltpu.SMEM),
                  pl.BlockSpec(memory_space=pltpu.SMEM),
                  pl.BlockSpec((tile,) + xp.shape[1:], lambda i: (i, 0, 0)),
                  pl.BlockSpec(xp2.shape, lambda i: (0, 0, 0))],
        out_specs=pl.BlockSpec(memory_space=pl.ANY),
        scratch_shapes=[pltpu.SemaphoreType.DMA(()), pltpu.SemaphoreType.DMA(()),
                        pltpu.VMEM((EXPERT_TILE // 2,) + xp.shape[1:], xp.dtype)],
        compiler_params=_cparams("arbitrary"), name="moe_dispatch",
    )(dest, pads, dest2, xp, xp2)


def _expert_kernel(be_ref, nu_ref, xs_ref, w1_ref, b1g_ref, b1l_ref, w2_ref, b2_ref, perm_ref, ys_ref,
                   w1g_scr, w1l_scr, w2_scr, xq_scr, y_scr, sem, osem):
    j = pl.program_id(0)
    active = j < nu_ref[0]
    changed = jnp.logical_or(j == 0, be_ref[j] != be_ref[jnp.maximum(j - 1, 0)])
    tm = y_scr.shape[0]
    slot = lax.rem(j, 2)

    def fetch(blk, slot):
        row0 = pl.multiple_of(blk * tm, tm)
        return [pltpu.make_async_copy(xs_ref.at[pl.ds(row0, tm), q, :], xq_scr.at[slot, q], sem.at[slot])
                for q in range(XP_TILE)]

    def put(blk):
        row0 = pl.multiple_of(blk * tm, tm)
        return [pltpu.make_async_copy(y_scr.at[:, q * LANES:(q + 1) * LANES], ys_ref.at[pl.ds(row0, tm), q, :], osem)
                for q in range(YS_TILE)]

    def emit(y):
        @pl.when(j > 0)
        def _():
            for cp in put(j - 1):
                cp.wait()

        y_scr[...] = _pack_halves(y)
        for cp in put(j):
            cp.start()

    @pl.when(j == 0)
    def _():
        for cp in fetch(0, 0):
            cp.start()

    @pl.when(j + 1 < nu_ref[0])
    def _():
        for cp in fetch(j + 1, 1 - slot):
            cp.start()

    @pl.when(jnp.logical_and(active, changed))
    def _():
        for c in range(2 * D_FF // 256):
            wc = w1_ref[0, :, c * 256:(c + 1) * 256].astype(BF16)
            d = jnp.dot(wc, perm_ref[...], preferred_element_type=F32).astype(BF16)
            w1g_scr[:, c * 128:(c + 1) * 128] = d[:, :128]
            w1l_scr[:, c * 128:(c + 1) * 128] = d[:, 128:]
        for c in range(D_FF // 256):
            w2_scr[c * 256:(c + 1) * 256, :] = w2_ref[0, c * 256:(c + 1) * 256, :].astype(BF16)

    @pl.when(active)
    def _():
        for cp in fetch(j, slot):
            cp.wait()
        lo, hi = _unpack_halves(jnp.concatenate([xq_scr[slot, q] for q in range(XP_TILE)], axis=1))
        xb = jnp.concatenate([lo, hi], axis=1)
        hg = jnp.dot(xb, w1g_scr[...], preferred_element_type=F32) + b1g_ref[0]
        hl = jnp.dot(xb, w1l_scr[...], preferred_element_type=F32) + b1l_ref[0]
        x_glu = jnp.minimum(hg, SWIGLU_LIMIT)
        x_lin = jnp.clip(hl, -SWIGLU_LIMIT, SWIGLU_LIMIT)
        a = x_glu * _sigmoid(SWIGLU_ALPHA * x_glu) * (x_lin + 1.0)
        emit(jnp.dot(a.astype(BF16), w2_scr[...], preferred_element_type=F32) + b2_ref[0])

    @pl.when(jnp.logical_not(active))
    def _():
        emit(jnp.zeros((tm, D_MODEL), F32))

    @pl.when(j == pl.num_programs(0) - 1)
    def _():
        for cp in put(j):
            cp.wait()


def _experts(be, nu, xs, w1, b1g, b1l, w2, b2, perm, tile):
    n_blocks = xs.shape[0] // tile
    grid_spec = pltpu.PrefetchScalarGridSpec(
        num_scalar_prefetch=2, grid=(n_blocks,),
        in_specs=[pl.BlockSpec(memory_space=pl.ANY),
                  pl.BlockSpec((1, D_MODEL, 2 * D_FF), lambda j, be, nu: (be[j], 0, 0)),
                  pl.BlockSpec((1, 1, D_FF), lambda j, be, nu: (be[j], 0, 0)),
                  pl.BlockSpec((1, 1, D_FF), lambda j, be, nu: (be[j], 0, 0)),
                  pl.BlockSpec((1, D_FF, D_MODEL), lambda j, be, nu: (be[j], 0, 0)),
                  pl.BlockSpec((1, 1, D_MODEL), lambda j, be, nu: (be[j], 0, 0)),
                  pl.BlockSpec((256, 256), lambda j, be, nu: (0, 0))],
        out_specs=pl.BlockSpec(memory_space=pl.ANY),
        scratch_shapes=[pltpu.VMEM((D_MODEL, D_FF), BF16), pltpu.VMEM((D_MODEL, D_FF), BF16),
                        pltpu.VMEM((D_FF, D_MODEL), BF16), pltpu.VMEM((2, XP_TILE, tile, LANES), U32),
                        pltpu.VMEM((tile, YS_TILE * LANES), U32), pltpu.SemaphoreType.DMA((2,)),
                        pltpu.SemaphoreType.DMA(())])
    return pl.pallas_call(
        _expert_kernel, out_shape=jax.ShapeDtypeStruct((xs.shape[0], YS_TILE, LANES), U32), grid_spec=grid_spec,
        compiler_params=_cparams("arbitrary"), name="moe_experts",
    )(be, nu, xs, w1, b1g, b1l, w2, b2, perm)


def _combine_kernel(dest_ref, next_ref, ys_ref, tk_ref, x1_ref, g2_ref, b2_ref, out_ref, buf, sem):
    i = pl.program_id(0)
    t = x1_ref.shape[0]
    slot = lax.rem(i, 2)

    def gather(idx_ref, s):
        def issue(grp, carry):
            base = pl.multiple_of(grp * 8, 8)
            for u in range(8):
                for k in range(TOP_K):
                    pltpu.make_async_copy(ys_ref.at[idx_ref[(base + u) * TOP_K + k]],
                                          buf.at[s, k, grp, :, u, :], sem.at[s]).start(priority=k % 2)
            return carry

        lax.fori_loop(0, t // 8, issue, 0)

    @pl.when(i == 0)
    def _():
        gather(dest_ref, 0)

    @pl.when(i + 1 < pl.num_programs(0))
    def _():
        gather(next_ref, 1 - slot)

    for k in range(TOP_K):
        for u in range(8):
            pltpu.make_async_copy(ys_ref.at[pl.ds(0, t // 8)], buf.at[slot, k, :, :, u, :], sem.at[slot]).wait()
    tk = tk_ref[...]
    los, his = [], []
    for q in range(YS_TILE):
        lo = hi = None
        for k in range(TOP_K):
            words = buf[slot, k, :, q].reshape(t, LANES)
            g = tk[:, k:k + 1]
            lo_k = g * pltpu.bitcast(words << 16, F32)
            hi_k = g * pltpu.bitcast(words & jnp.uint32(0xFFFF0000), F32)
            lo = lo_k if lo is None else lo + lo_k
            hi = hi_k if hi is None else hi + hi_k
        los.append(lo)
        his.append(hi)
    ff = jnp.concatenate(los + his, axis=1)
    out_ref[...] = _layer_norm(DN_ALPHA * x1_ref[...] + ff, g2_ref[...], b2_ref[...])


def _combine(dest_flat, ys, tk, x1, g2, b2):
    t = x1.shape[0]
    tile = min(ROW_TILE, t)
    n = t // tile
    return pl.pallas_call(
        _combine_kernel, out_shape=jax.ShapeDtypeStruct((t, D_MODEL), F32), grid=(n,),
        in_specs=[pl.BlockSpec((tile * TOP_K,), lambda i: (i,), memory_space=pltpu.SMEM),
                  pl.BlockSpec((tile * TOP_K,), lambda i: (jnp.minimum(i + 1, n - 1),), memory_space=pltpu.SMEM),
                  pl.BlockSpec(memory_space=pl.ANY),
                  pl.BlockSpec((tile, LANES), lambda i: (i, 0)),
                  pl.BlockSpec((tile, D_MODEL), lambda i: (i, 0)),
                  pl.BlockSpec((1, D_MODEL), lambda i: (0, 0)),
                  pl.BlockSpec((1, D_MODEL), lambda i: (0, 0))],
        out_specs=pl.BlockSpec((tile, D_MODEL), lambda i: (i, 0)),
        scratch_shapes=[pltpu.VMEM((2, TOP_K, tile // 8, YS_TILE, 8, LANES), U32), pltpu.SemaphoreType.DMA((2,))],
        compiler_params=_cparams("arbitrary"), name="moe_combine",
    )(dest_flat, dest_flat, ys, tk, x1, g2, b2)


def _rel_bucket(dist):
    exact = REL_BUCKETS // 2
    d = np.maximum(dist, 0)
    log_b = exact + (np.log(np.maximum(d, 1).astype(np.float32) / np.float32(exact))
                     / np.float32(math.log(REL_MAX_DIST / exact)) * np.float32(REL_BUCKETS - exact)).astype(np.int32)
    return np.where(d < exact, d, np.minimum(log_b, REL_BUCKETS - 1)).astype(np.int32)


def _bias_lookup(table, bucket, valid):
    bucket = jnp.asarray(bucket)[None]
    acc = jnp.zeros((table.shape[1],) + bucket.shape[1:], F32)
    for b in range(REL_BUCKETS):
        acc = jnp.where(bucket == b, table[b].reshape((-1,) + (1,) * (bucket.ndim - 1)), acc)
    return jnp.where(jnp.asarray(valid)[None], acc, NEG)


def _bias_tables(rel_bias):
    table = rel_bias.astype(F32)
    r = np.arange(WINDOW)[:, None]
    c = np.arange(2 * WINDOW)[None, :]
    dist = r + WINDOW - c
    valid = (dist >= 0) & (dist < WINDOW)
    dist0 = np.where(c < N_META, N_META + r - c, dist)
    valid0 = np.where(c < N_META, dist0 < WINDOW, (c >= WINDOW) & valid)
    both = jnp.stack([_bias_lookup(table, _rel_bucket(dist0), valid0), _bias_lookup(table, _rel_bucket(dist), valid)])
    dist_s = WINDOW - 1 - np.arange(WINDOW)
    rows = _bias_lookup(table[:, np.asarray(HEAD_ORDER)], _rel_bucket(dist_s), np.ones_like(dist_s, bool))
    return both, rows


def _perm_heads(a, axis):
    assert HEAD_ORDER == tuple(kv * A_GROUP + g for g in range(A_GROUP) for kv in range(A_KV_HEADS))
    shape = a.shape
    a = a.reshape(shape[:axis] + (A_KV_HEADS, A_GROUP, A_HD) + shape[axis + 1:])
    return jnp.swapaxes(a, axis, axis + 1).reshape(shape)


def _rep_rows(vec, rows=8):
    out = jnp.zeros((rows, LANES), F32)
    return out.at[:vec.shape[0], :].set(jnp.broadcast_to(vec.astype(F32)[:, None], (vec.shape[0], LANES)))


def kernel(x_prompt, x_sample, cache_swa_k, cache_swa_v, state_mlstm_C, state_mlstm_n, state_mlstm_m, meta_tokens, rel_bias, w_in, b_igate, b_fgate, attn_sinks, g_mlstm_out, g_attn_out, w_out, ln1_g, ln1_b, w_router, b_router, w_moe1, b_moe1, w_moe2, b_moe2, ln2_g, ln2_b):
    B, S, _ = x_prompt.shape
    NB = x_sample.shape[0]
    assert x_sample.shape[1] == 1 and w_in.shape[0] == 1
    assert S % PROJ_TILE == 0 and S % M_CHUNK == 0 and S % WINDOW == 0 and NB % SAMPLE_GROUP == 0
    l = 0

    assert IN_WIDTHS == (512, 512, 512, 512, 4, 4, 512, 128, 128)
    bf = lambda a: a.astype(BF16)
    w = w_in[l]
    n_main, n_gate = 4 * M_WIDTH, 2 * M_HEADS
    w_gate = w[:, n_main:n_main + n_gate]
    w_att = w[:, n_main + n_gate:]
    assert math.frexp(A_HD ** -0.5)[0] == 0.5
    w_qa = _perm_heads(w_att[:, :A_WIDTH], 1) * (A_HD ** -0.5)
    wr = bf(jnp.concatenate([w[:, :n_main], w_qa, w_att[:, A_WIDTH:],
                             jnp.pad(w_gate, ((0, 0), (0, LANES - n_gate)))], axis=1))
    wt = bf(jnp.concatenate([w[:, :M_WIDTH], w[:, 2 * M_WIDTH:3 * M_WIDTH], w_qa, w_att[:, A_WIDTH + LANES:],
                             w_gate], axis=1).T)
    b_gate = jnp.concatenate([b_igate[l], b_fgate[l]]).astype(F32)
    brow = jnp.pad(b_gate, (0, LANES - n_gate))[None, :]
    bcol = b_gate[:, None]
    plan_p = ((512, 512, "plain", BF16), (1536, 512, "plain", F32), (2560, 128, "plain", BF16),
              (2816, 128, "gate", F32))
    tplan_p = ((0, 512, "plain", BF16), (512, 512, "plain", BF16), (1024, 512, "plain", BF16),
               (1536, 128, "plain", BF16), (1664, 8, "gate", F32))
    plan_s = ((0, 512, "plain", F32), (512, 512, "plain", F32), (1024, 512, "plain", F32), (1536, 512, "plain", F32),
              (2048, 512, "plain", F32), (2560, 128, "plain", F32), (2688, 128, "plain", F32), (2816, 128, "gate", F32))

    bias_tab, bias_rows = _bias_tables(rel_bias)
    sinks = _rep_rows(attn_sinks[l])
    sinks_step = _rep_rows(attn_sinks[l][np.asarray(HEAD_ORDER)])
    g_m = g_mlstm_out[l].astype(F32)[None, :]
    g_a = _perm_heads(g_attn_out[l].astype(F32), 0)[None, :]
    wo = bf(jnp.concatenate([w_out[l][:M_WIDTH], _perm_heads(w_out[l][M_WIDTH:], 0)], axis=0))
    g1, b1 = ln1_g[l].astype(F32)[None, :], ln1_b[l].astype(F32)[None, :]
    g2, b2 = ln2_g[l].astype(F32)[None, :], ln2_b[l].astype(F32)[None, :]
    w_r = bf(jnp.pad(w_router[l], ((0, 0), (0, LANES - N_EXPERTS))))
    b_r = jnp.pad(b_router[l].astype(F32), (0, LANES - N_EXPERTS), constant_values=NEG)[None, :]
    b1g = b_moe1[l][:, 0::2].astype(F32)[:, None, :]
    b1l = b_moe1[l][:, 1::2].astype(F32)[:, None, :]
    b2e = b_moe2[l].astype(F32)[:, None, :]
    pj = np.zeros((256, 256), np.float32)
    pj[2 * np.arange(128), np.arange(128)] = 1.0
    pj[2 * np.arange(128) + 1, 128 + np.arange(128)] = 1.0
    perm = jnp.asarray(pj, BF16)

    xp2 = x_prompt.reshape(B * S, D_MODEL)
    km, om, ka, gc, qt, vt, qat, vat, gr, kv_tail = _proj(
        xp2, wr, wt, brow, bcol, plan_p, tplan_p, (2560, 256), PROJ_TILE, S, "proj_prompt")
    x_meta = jnp.pad(meta_tokens.astype(F32), ((0, M_CHUNK - N_META), (0, 0)))
    km0, _, ka0, gc0, qt0, vt0, _, vat0, gr0 = _proj(
        x_meta, wr, wt, brow, bcol, plan_p, tplan_p, None, M_CHUNK, M_CHUNK, "proj_meta")
    xs2 = x_sample.reshape(NB, D_MODEL)
    qm_s, km_s, vm_s, om_s, qa_s, ka_s, va_s, gc_s = _proj(
        xs2, wr, wt, brow, bcol, plan_s, (), None, NB, NB, "proj_sample")

    zero_c = jnp.zeros((M_HEADS, M_DV + 8, M_DK), F32)
    zero_m = jnp.zeros((8, LANES), F32)
    _, c_meta, m_meta = _mlstm(qt0, km0, vt0, gc0, gr0, zero_c, zero_m, 1, N_META, "mlstm_meta")
    h_p, c_p, m_p = _mlstm(qt, km, vt, gc, gr, c_meta[0], m_meta[0], B, M_CHUNK, "mlstm_prompt")
    C_p = c_p[:, :, :M_DV, :]
    n_p = c_p[:, :, M_DV, :]
    m_prompt = m_p[:, :M_HEADS, 0]
    m_pad = jnp.pad(state_mlstm_m[l].astype(F32), ((0, 0), (0, LANES - M_HEADS)))
    C_s, n_s, m_s, h_s = _mlstm_step(state_mlstm_C[l].astype(F32), state_mlstm_n[l].astype(F32), m_pad,
                                     gc_s, qm_s, km_s, vm_s)

    att_p = _swa(qat, ka, vat, ka0, vat0, jnp.swapaxes(bias_tab, 2, 3), sinks, B)
    ck = cache_swa_k[l].reshape(NB, WINDOW, LANES)
    cv = cache_swa_v[l].reshape(NB, WINDOW, LANES)
    k_new, v_new, att_s = _swa_step(ck, cv, qa_s, ka_s, va_s, bias_rows, sinks_step)

    x1_p, xpk_p, tk_p, cnt_p = _merge(h_p, om, att_p, xp2, g_m, g_a, wo, g1, b1, w_r, b_r, MERGE_TILE, "merge_prompt")
    x1_s, xpk_s, tk_s, cnt_s = _merge(h_s, om_s, att_s, xs2, g_m, g_a, wo, g1, b1, w_r, b_r, NB, "merge_sample")

    T_p = B * S
    assert T_p % RANK_TILE == 0 and T_p % ROW_TILE == 0
    n_blocks = -(-((T_p + NB) * TOP_K) // EXPERT_TILE) + N_EXPERTS
    off, be2, nu2, pad = _offsets(cnt_p + cnt_s, n_blocks, EXPERT_TILE)
    pads = jnp.concatenate([pad[0, :N_EXPERTS], pad[1, :N_EXPERTS], nu2[0, :1]])
    dest_p = _route(tk_p, off)[:, :TOP_K].reshape(-1)
    dest_s = _route(tk_s, off + cnt_p)[:, :TOP_K].reshape(-1)
    be = be2.reshape(-1)[:n_blocks]
    nu = nu2[0, :1]
    xs = _dispatch(dest_p, pads, dest_s, xpk_p, xpk_s, n_blocks * EXPERT_TILE, EXPERT_TILE)
    ys = _experts(be, nu, xs, w_moe1[l], b1g, b1l, w_moe2[l], b2e, perm, EXPERT_TILE)
    y_p = _combine(dest_p, ys, tk_p, x1_p, g2, b2)
    y_s = _combine(dest_s, ys, tk_s, x1_s, g2, b2)

    kv_tail = kv_tail.reshape(B, WINDOW, 2, A_KV_HEADS, A_HD)
    dt_k, dt_v = cache_swa_k.dtype, cache_swa_v.dtype
    return (y_p.reshape(B, S, D_MODEL).astype(x_prompt.dtype), y_s.reshape(NB, 1, D_MODEL).astype(x_sample.dtype),
            kv_tail[:, :, 0][None].astype(dt_k), kv_tail[:, :, 1][None].astype(dt_v),
            C_p[None].astype(state_mlstm_C.dtype), n_p[None].astype(state_mlstm_n.dtype),
            m_prompt[None].astype(state_mlstm_m.dtype),
            k_new.reshape(1, NB, WINDOW, A_KV_HEADS, A_HD).astype(dt_k),
            v_new.reshape(1, NB, WINDOW, A_KV_HEADS, A_HD).astype(dt_v),
            C_s[None].astype(state_mlstm_C.dtype), n_s[None].astype(state_mlstm_n.dtype),
            m_s[:, :M_HEADS][None].astype(state_mlstm_m.dtype))
```

```python
import functools
import math

import numpy as np
import jax
import jax.numpy as jnp
from jax import lax
from jax.experimental import pallas as pl
from jax.experimental.pallas import tpu as pltpu

F32 = jnp.float32
BF16 = jnp.bfloat16
I32 = jnp.int32
U32 = jnp.uint32

D_MODEL = 1024
N_META = 16
M_HEADS = 4
M_DK = 128
M_DV = 128
M_WIDTH = M_HEADS * M_DV
A_HD = 64
A_HEADS = 8
A_KV_HEADS = 2
A_GROUP = A_HEADS // A_KV_HEADS
A_WIDTH = A_HEADS * A_HD
WINDOW = 128
REL_BUCKETS = 32
REL_MAX_DIST = 128
N_EXPERTS = 32
TOP_K = 4
D_FF = D_MODEL
SWIGLU_LIMIT = 7.0
SWIGLU_ALPHA = 1.702
DEPTH = 1
DN_ALPHA = (2.0 * DEPTH) ** 0.25
LN_EPS = 1e-5
IN_WIDTHS = (M_WIDTH, M_WIDTH, M_WIDTH, M_WIDTH, M_HEADS, M_HEADS, A_WIDTH, A_KV_HEADS * A_HD, A_KV_HEADS * A_HD)

LANES = 128
NEG = -1e30
VMEM_LIMIT = 56 * 1024 * 1024

M_CHUNK = 256
PROJ_TILE = 1024
MERGE_TILE = 1024
RANK_TILE = 512
ROW_TILE = 512
EXPERT_TILE = 512
SAMPLE_GROUP = 8
ISSUE_GROUP = 8
SWA_QBLOCKS = 8
XP_TILE = D_MODEL // 2 // LANES
YS_TILE = D_MODEL // 2 // LANES
HEAD_ORDER = (0, 4, 1, 5, 2, 6, 3, 7)


def _cparams(*sem):
    return pltpu.CompilerParams(dimension_semantics=sem, vmem_limit_bytes=VMEM_LIMIT)


def _log_sigmoid(x):
    return jnp.minimum(x, 0.0) - jnp.log1p(jnp.exp(-jnp.abs(x)))


def _sigmoid(x):
    return 1.0 / (1.0 + jnp.exp(-x))


def _proj_kernel(x_ref, wr_ref, wt_ref, brow_ref, bcol_ref, *outs, row_plan, t_plan, tail_cols):
    xb = x_ref[...].astype(BF16)
    tm = xb.shape[0]
    o = 0
    for (c0, width, kind, _) in row_plan:
        r = jnp.dot(xb, wr_ref[:, c0:c0 + width], preferred_element_type=F32)
        if kind == "gate":
            r = r + brow_ref[...]
            lane = lax.broadcasted_iota(I32, r.shape, 1)
            r = jnp.where(lane < M_HEADS, r, _log_sigmoid(r))
        outs[o][...] = r.astype(outs[o].dtype)
        o += 1
    for (r0, nrows, kind, _) in t_plan:
        r = lax.dot_general(wt_ref[r0:r0 + nrows, :], xb, (((1,), (1,)), ((), ())), preferred_element_type=F32)
        if kind == "gate":
            r = r + bcol_ref[...]
            row = lax.broadcasted_iota(I32, r.shape, 0)
            r = jnp.where(row < M_HEADS, r, _log_sigmoid(r))
        outs[o][...] = r.astype(outs[o].dtype)
        o += 1
    if tail_cols is not None:
        c0, width = tail_cols
        outs[o][...] = jnp.dot(xb[tm - WINDOW:, :], wr_ref[:, c0:c0 + width], preferred_element_type=F32)


def _proj(x, wr, wt, brow, bcol, row_plan, t_plan, tail_cols, tile, rows_per_group, name):
    t = x.shape[0]
    nt = t // tile
    out_shape, out_specs = [], []
    for (_, width, _, dt) in row_plan:
        out_shape.append(jax.ShapeDtypeStruct((t, width), dt))
        out_specs.append(pl.BlockSpec((tile, width), lambda i: (i, 0)))
    for (_, nrows, _, dt) in t_plan:
        out_shape.append(jax.ShapeDtypeStruct((nrows, t), dt))
        out_specs.append(pl.BlockSpec((nrows, tile), lambda i: (0, i)))
    if tail_cols is not None:
        tiles_per_group = rows_per_group // tile
        out_shape.append(jax.ShapeDtypeStruct((t // rows_per_group * WINDOW, tail_cols[1]), F32))
        out_specs.append(pl.BlockSpec((WINDOW, tail_cols[1]), lambda i: (i // tiles_per_group, 0)))
    kern = functools.partial(_proj_kernel, row_plan=row_plan, t_plan=t_plan, tail_cols=tail_cols)
    return pl.pallas_call(
        kern, out_shape=out_shape, grid=(nt,),
        in_specs=[pl.BlockSpec((tile, D_MODEL), lambda i: (i, 0)),
                  pl.BlockSpec(wr.shape, lambda i: (0, 0)),
                  pl.BlockSpec(wt.shape, lambda i: (0, 0)),
                  pl.BlockSpec(brow.shape, lambda i: (0, 0)),
                  pl.BlockSpec(bcol.shape, lambda i: (0, 0))],
        out_specs=out_specs, compiler_params=_cparams("arbitrary"), name=name,
    )(x, wr, wt, brow, bcol)


def _split3(a):
    hi = a.astype(BF16)
    r1 = a - hi.astype(F32)
    mid = r1.astype(BF16)
    lo = (r1 - mid.astype(F32)).astype(BF16)
    return hi, mid, lo


def _mlstm_kernel(qt_ref, k_ref, vt_ref, gc_ref, gr_ref, c0_ref, m0_ref, h_ref, c_out_ref, m_out_ref,
                  c_scr, m_scr, *, n_valid):
    c = pl.program_id(1)
    nc = pl.num_programs(1)
    L = k_ref.shape[0]

    @pl.when(c == 0)
    def _():
        c_scr[...] = c0_ref[...]
        m_scr[...] = m0_ref[...]

    gc = gc_ref[...]
    gr = gr_ref[...]
    if n_valid < L:
        rowc = lax.broadcasted_iota(I32, gc.shape, 0)
        lanec = lax.broadcasted_iota(I32, gc.shape, 1)
        gc = jnp.where(rowc < n_valid, gc, jnp.where(lanec < M_HEADS, NEG, 0.0))
        rowr = lax.broadcasted_iota(I32, gr.shape, 0)
        colr = lax.broadcasted_iota(I32, gr.shape, 1)
        gr = jnp.where(colr < n_valid, gr, jnp.where(rowr < M_HEADS, NEG, 0.0))
    r_i = lax.broadcasted_iota(I32, (L, L), 0)
    c_i = lax.broadcasted_iota(I32, (L, L), 1)
    upper = r_i <= c_i
    tril = jnp.where(c_i <= r_i, 1.0, 0.0).astype(BF16)
    triu = jnp.where(upper, 1.0, 0.0).astype(BF16)
    b_cols = sum(jnp.dot(tril, part, preferred_element_type=F32) for part in _split3(gc))
    b_rows = sum(jnp.dot(part, triu, preferred_element_type=F32) for part in _split3(gr))
    scale = M_DK ** -0.5
    ones_rows = jnp.where(lax.broadcasted_iota(I32, (8, L), 0) == 0, 1.0, 0.0).astype(BF16)

    m_all = m_scr[...]
    c_all = [c_scr[h] for h in range(M_HEADS)]
    h_new, c_new, m_new_all = [], [], []
    for h in range(M_HEADS):
        sl = slice(h * M_DK, (h + 1) * M_DK)
        qt = qt_ref[sl, :]
        k = k_ref[:, sl]
        vt_aug = jnp.concatenate([vt_ref[sl, :], ones_rows], axis=0)
        r_c = gc[:, h:h + 1] - b_cols[:, M_HEADS + h:M_HEADS + h + 1]
        ig_r = gr[h:h + 1, :]
        b_r = b_rows[M_HEADS + h:M_HEADS + h + 1, :]
        m_prev = m_all[h:h + 1, 0:1]
        cs = c_all[h]

        dt = jnp.where(upper, b_r + r_c, NEG)
        m_t = jnp.maximum(b_r + m_prev, jnp.max(dt, axis=0, keepdims=True))
        st = jnp.dot(k, qt, preferred_element_type=F32) * (scale * jnp.exp(dt - m_t))
        inter = jnp.dot(cs.astype(BF16), qt, preferred_element_type=F32)
        intra = jnp.dot(vt_aug, st.astype(BF16), preferred_element_type=F32)
        nd = jnp.exp(b_r + m_prev - m_t) * inter + intra
        den = nd[M_DV:M_DV + 1, :]
        h_new.append((nd[:M_DV, :] / jnp.maximum(jnp.abs(den), jnp.exp(-m_t))).T)

        b_last = b_r[:, L - 1:L]
        g = ig_r + b_last - b_r
        m_new = jnp.maximum(b_last + m_prev, jnp.max(g, axis=1, keepdims=True))
        a = jnp.exp(b_last + m_prev - m_new)
        wv = (vt_aug.astype(F32) * jnp.exp(g - m_new)).astype(BF16)
        c_new.append(a * cs + jnp.dot(wv, k, preferred_element_type=F32) * scale)
        m_new_all.append(jnp.broadcast_to(m_new, (1, LANES)))

    h_ref[...] = jnp.concatenate(h_new, axis=1)
    for h in range(M_HEADS):
        c_scr[h] = c_new[h]
    m_scr[0:M_HEADS, :] = jnp.concatenate(m_new_all, axis=0)

    @pl.when(c == nc - 1)
    def _():
        c_out_ref[0] = c_scr[...]
        m_out_ref[0] = m_scr[...]


def _mlstm(qt, km, vt, gc, gr, c0, m0, batch, n_valid, name):
    L = M_CHUNK
    nc = km.shape[0] // (batch * L)
    kern = functools.partial(_mlstm_kernel, n_valid=n_valid)
    rows = pl.BlockSpec((L, M_WIDTH), lambda b, c: (b * nc + c, 0))
    cols = pl.BlockSpec((M_WIDTH, L), lambda b, c: (0, b * nc + c))
    return pl.pallas_call(
        kern,
        out_shape=[jax.ShapeDtypeStruct((batch * nc * L, M_WIDTH), F32),
                   jax.ShapeDtypeStruct((batch, M_HEADS, M_DV + 8, M_DK), F32),
                   jax.ShapeDtypeStruct((batch, 8, LANES), F32)],
        grid=(batch, nc),
        in_specs=[cols, rows, cols,
                  pl.BlockSpec((L, LANES), lambda b, c: (b * nc + c, 0)),
                  pl.BlockSpec((8, L), lambda b, c: (0, b * nc + c)),
                  pl.BlockSpec((M_HEADS, M_DV + 8, M_DK), lambda b, c: (0, 0, 0)),
                  pl.BlockSpec((8, LANES), lambda b, c: (0, 0))],
        out_specs=[rows,
                   pl.BlockSpec((1, M_HEADS, M_DV + 8, M_DK), lambda b, c: (b, 0, 0, 0)),
                   pl.BlockSpec((1, 8, LANES), lambda b, c: (b, 0, 0))],
        scratch_shapes=[pltpu.VMEM((M_HEADS, M_DV + 8, M_DK), F32), pltpu.VMEM((8, LANES), F32)],
        compiler_params=_cparams("arbitrary", "arbitrary"), name=name,
    )(qt, km, vt, gc, gr, c0, m0)


def _outer_f32(a, b):
    ah, am, al = (t.astype(F32) for t in _split3(a))
    bh, bm, bl = (t.astype(F32) for t in _split3(b))
    z = jnp.zeros_like(ah)
    lhs = jnp.concatenate([ah, ah, ah, am, am, al, z, z], axis=0).astype(BF16)
    rhs = jnp.concatenate([bh, bm, bl, bh, bm, bh, z, z], axis=0).astype(BF16)
    return lax.dot_general(lhs, rhs, (((0,), (0,)), ((), ())), preferred_element_type=F32)


def _mlstm_step_kernel(c_ref, n_ref, m_ref, gc_ref, q_ref, k_ref, v_ref,
                       c_out_ref, n_out_ref, m_out_ref, h_ref):
    g = c_ref.shape[0]
    scale = M_DK ** -0.5
    lane_m = lax.broadcasted_iota(I32, (1, LANES), 1)
    for j in range(g):
        m_row = jnp.zeros((1, LANES), F32)
        for h in range(M_HEADS):
            sl = slice(h * M_DK, (h + 1) * M_DK)
            q = q_ref[j:j + 1, sl]
            k = k_ref[j:j + 1, sl] * scale
            v = v_ref[j:j + 1, sl]
            ig = gc_ref[j:j + 1, h:h + 1]
            lf = gc_ref[j:j + 1, M_HEADS + h:M_HEADS + h + 1]
            m = m_ref[j:j + 1, h:h + 1]
            c = c_ref[j, h]
            n = n_ref[j, h:h + 1, :]
            m_t = jnp.maximum(lf + m, ig)
            w = jnp.exp(lf + m - m_t)
            wg = jnp.exp(ig - m_t)
            s = jnp.sum(q * k, axis=1, keepdims=True) * wg
            q8 = jnp.broadcast_to(q, (8, M_DK)).astype(BF16)
            cq = lax.dot_general(q8, c.astype(BF16), (((1,), (1,)), ((), ())), preferred_element_type=F32)[0:1, :]
            den = w * jnp.sum(n * q, axis=1, keepdims=True) + s
            h_ref[j:j + 1, sl] = (w * cq + s * v) / jnp.maximum(jnp.abs(den), jnp.exp(-m_t))
            c_out_ref[j, h] = w * c + _outer_f32(wg * v, k)
            n_out_ref[j, h:h + 1, :] = w * n + wg * k
            m_row = jnp.where(lane_m == h, m_t, m_row)
        m_out_ref[j:j + 1, :] = m_row


def _mlstm_step(c, n, m_pad, gc, q, k, v):
    nb = c.shape[0]
    g = SAMPLE_GROUP
    row = lambda w: pl.BlockSpec((g, w), lambda i: (i, 0))
    return pl.pallas_call(
        _mlstm_step_kernel,
        out_shape=[jax.ShapeDtypeStruct(c.shape, F32), jax.ShapeDtypeStruct(n.shape, F32),
                   jax.ShapeDtypeStruct((nb, LANES), F32), jax.ShapeDtypeStruct((nb, M_WIDTH), F32)],
        grid=(nb // g,),
        in_specs=[pl.BlockSpec((g, M_HEADS, M_DV, M_DK), lambda i: (i, 0, 0, 0)),
                  pl.BlockSpec((g, M_HEADS, M_DK), lambda i: (i, 0, 0)),
                  row(LANES), row(LANES), row(M_WIDTH), row(M_WIDTH), row(M_WIDTH)],
        out_specs=[pl.BlockSpec((g, M_HEADS, M_DV, M_DK), lambda i: (i, 0, 0, 0)),
                   pl.BlockSpec((g, M_HEADS, M_DK), lambda i: (i, 0, 0)),
                   row(LANES), row(M_WIDTH)],
        compiler_params=_cparams("arbitrary"), name="mlstm_step",
    )(c, n, m_pad, gc, q, k, v)


def _swa_kernel(qt_ref, kc_ref, kp_ref, vtc_ref, vtp_ref, km_ref, vtm_ref, bias_ref, sink_ref, o_ref):
    j = pl.program_id(1)
    first = j == 0
    blk = WINDOW
    nqb = qt_ref.shape[1] // blk
    kp = jnp.where(first, km_ref[...], kp_ref[...])
    vtp = jnp.where(first, vtm_ref[...], vtp_ref[...])
    k = jnp.concatenate([kp, kc_ref[...]], axis=0)
    vt = jnp.concatenate([vtp, vtc_ref[...]], axis=1)
    row_v = lax.broadcasted_iota(I32, vt.shape, 0)
    zero_v = jnp.zeros_like(vt)
    vt_half = (jnp.where(row_v < A_HD, vt, zero_v), jnp.where(row_v >= A_HD, vt, zero_v))
    row_q = lax.broadcasted_iota(I32, (LANES, blk), 0)
    lo_rows = row_q < A_HD
    for u in range(nqb):
        cols = slice(u * blk, (u + 1) * blk)
        keys = slice(u * blk, (u + 2) * blk)
        table = jnp.where(first, 0, 1) if u == 0 else 1
        vt_stack = jnp.concatenate([vt_half[0][:, keys], vt_half[1][:, keys]], axis=1)
        outs = []
        for p in range(A_GROUP):
            qs = qt_ref[p * LANES:(p + 1) * LANES, cols]
            zero_q = jnp.zeros_like(qs)
            probs, inv = [], []
            for half in range(2):
                hd = HEAD_ORDER[2 * p + half]
                q_own = jnp.where(lo_rows if half == 0 else jnp.logical_not(lo_rows), qs, zero_q)
                s = jnp.dot(k[keys], q_own, preferred_element_type=F32) + bias_ref[table, hd]
                sk = sink_ref[hd:hd + 1, 0:1]
                m = jnp.maximum(jnp.max(s, axis=0, keepdims=True), sk)
                e = jnp.exp(s - m)
                probs.append(e.astype(BF16))
                inv.append(1.0 / (jnp.sum(e, axis=0, keepdims=True) + jnp.exp(sk - m)))
            ot = jnp.dot(vt_stack, jnp.concatenate(probs, axis=0), preferred_element_type=F32)
            outs.append((ot * jnp.where(lo_rows, inv[0], inv[1])).T)
        o_ref[cols, :] = jnp.concatenate(outs, axis=1)


def _swa(qat, ka, vat, kmeta, vtmeta, bias_t, sinks, batch):
    blk = WINDOW
    nqb = SWA_QBLOCKS
    t = ka.shape[0]
    nq = t // (batch * blk * nqb)
    prev = lambda b, j: (b * nq + j) * nqb + jnp.where(j == 0, 0, -1)
    const2 = lambda shape: pl.BlockSpec(shape, lambda b, j: (0, 0))
    return pl.pallas_call(
        _swa_kernel, out_shape=jax.ShapeDtypeStruct((t, A_WIDTH), F32), grid=(batch, nq),
        in_specs=[pl.BlockSpec((A_WIDTH, nqb * blk), lambda b, j: (0, b * nq + j)),
                  pl.BlockSpec((nqb * blk, LANES), lambda b, j: (b * nq + j, 0)),
                  pl.BlockSpec((blk, LANES), lambda b, j: (prev(b, j), 0)),
                  pl.BlockSpec((LANES, nqb * blk), lambda b, j: (0, b * nq + j)),
                  pl.BlockSpec((LANES, blk), lambda b, j: (0, prev(b, j))),
                  const2((blk, LANES)), const2((LANES, blk)),
                  pl.BlockSpec(bias_t.shape, lambda b, j: (0, 0, 0, 0)),
                  const2((8, LANES))],
        out_specs=pl.BlockSpec((nqb * blk, A_WIDTH), lambda b, j: (b * nq + j, 0)),
        compiler_params=_cparams("arbitrary", "arbitrary"), name="swa_prompt",
    )(qat, ka, ka, vat, vat, kmeta, vtmeta, bias_t, sinks)


def _swa_step_kernel(ck_ref, cv_ref, q_ref, k_ref, v_ref, bias_ref, sink_ref, ko_ref, vo_ref, o_ref):
    g = ck_ref.shape[0]
    lane = lax.broadcasted_iota(I32, (A_HEADS, LANES), 1)
    row = lax.broadcasted_iota(I32, (A_HEADS, LANES), 0)
    own_half = (row % 2 == 0) == (lane < A_HD)
    bias = bias_ref[...]
    sk = sink_ref[:, 0:1]
    for j in range(g):
        ko_ref[j, 0:WINDOW - 1, :] = ck_ref[j, 1:WINDOW, :]
        ko_ref[j, WINDOW - 1:WINDOW, :] = k_ref[j:j + 1, :]
        vo_ref[j, 0:WINDOW - 1, :] = cv_ref[j, 1:WINDOW, :]
        vo_ref[j, WINDOW - 1:WINDOW, :] = v_ref[j:j + 1, :]
        kk = ko_ref[j].astype(BF16)
        vv = vo_ref[j].astype(BF16)
        slabs = [q_ref[j:j + 1, p * LANES:(p + 1) * LANES] for p in range(A_GROUP)]
        q8 = jnp.concatenate([slabs[r // 2] for r in range(A_HEADS)], axis=0)
        q8 = jnp.where(own_half, q8, 0.0).astype(BF16)
        s = lax.dot_general(q8, kk, (((1,), (1,)), ((), ())), preferred_element_type=F32)
        s = s + bias
        m = jnp.maximum(jnp.max(s, axis=1, keepdims=True), sk)
        e = jnp.exp(s - m)
        inv = 1.0 / (jnp.sum(e, axis=1, keepdims=True) + jnp.exp(sk - m))
        o8 = jnp.where(own_half, jnp.dot(e.astype(BF16), vv, preferred_element_type=F32) * inv, 0.0)
        for p in range(A_GROUP):
            o_ref[j:j + 1, p * LANES:(p + 1) * LANES] = o8[2 * p:2 * p + 1, :] + o8[2 * p + 1:2 * p + 2, :]


def _swa_step(ck, cv, q, k, v, bias_rows, sinks):
    nb = ck.shape[0]
    g = SAMPLE_GROUP
    cache = pl.BlockSpec((g, WINDOW, LANES), lambda i: (i, 0, 0))
    row = lambda w: pl.BlockSpec((g, w), lambda i: (i, 0))
    const = lambda a: pl.BlockSpec(a.shape, lambda i: (0, 0))
    return pl.pallas_call(
        _swa_step_kernel,
        out_shape=[jax.ShapeDtypeStruct(ck.shape, F32), jax.ShapeDtypeStruct(cv.shape, F32),
                   jax.ShapeDtypeStruct((nb, A_WIDTH), F32)],
        grid=(nb // g,),
        in_specs=[cache, cache, row(A_WIDTH), row(LANES), row(LANES), const(bias_rows), const(sinks)],
        out_specs=[cache, cache, row(A_WIDTH)],
        compiler_params=_cparams("arbitrary"), name="swa_step",
    )(ck, cv, q, k, v, bias_rows, sinks)


def _layer_norm(z, g, b):
    mu = jnp.mean(z, axis=1, keepdims=True)
    zc = z - mu
    var = jnp.mean(zc * zc, axis=1, keepdims=True)
    return zc * lax.rsqrt(var + LN_EPS) * g + b


def _pack_halves(x):
    w = x.shape[1] // 2
    lo = pltpu.bitcast(x[:, :w].astype(BF16).astype(F32), U32)
    hi = pltpu.bitcast(x[:, w:].astype(BF16).astype(F32), U32)
    return (lo >> 16) | (hi & jnp.uint32(0xFFFF0000))


def _unpack_halves(words):
    lo = pltpu.bitcast(words << 16, F32).astype(BF16)
    hi = pltpu.bitcast(words & jnp.uint32(0xFFFF0000), F32).astype(BF16)
    return lo, hi


def _to_token_tiles(ref, x):
    for q in range(x.shape[1] // LANES):
        ref[:, q, :] = x[:, q * LANES:(q + 1) * LANES]


def _merge_kernel(h_ref, om_ref, att_ref, x_ref, gm_ref, ga_ref, wo_ref, g1_ref, b1_ref, wr_ref, br_ref,
                  x1_ref, xp_ref, tk_ref, cnt_ref):
    @pl.when(pl.program_id(0) == 0)
    def _():
        cnt_ref[...] = jnp.zeros_like(cnt_ref)

    hm = h_ref[...] * _sigmoid(om_ref[...])
    ym = hm * lax.rsqrt(jnp.mean(hm * hm, axis=1, keepdims=True) + LN_EPS) * gm_ref[...]
    att = att_ref[...]
    ya = att * lax.rsqrt(jnp.mean(att * att, axis=1, keepdims=True) + LN_EPS) * ga_ref[...]
    mix = (jnp.dot(ym.astype(BF16), wo_ref[0:M_WIDTH, :], preferred_element_type=F32)
           + jnp.dot(ya.astype(BF16), wo_ref[M_WIDTH:, :], preferred_element_type=F32))
    x1 = _layer_norm(DN_ALPHA * x_ref[...] + mix, g1_ref[...], b1_ref[...])
    x1_ref[...] = x1
    _to_token_tiles(xp_ref, _pack_halves(x1))
    logits = jnp.dot(x1.astype(BF16), wr_ref[...], preferred_element_type=F32) + br_ref[...]
    lane = lax.broadcasted_iota(I32, logits.shape, 1).astype(F32)
    vals, idxs = [], []
    for _ in range(TOP_K):
        mx = jnp.max(logits, axis=1, keepdims=True)
        idx = jnp.min(jnp.where(logits == mx, lane, float(LANES)), axis=1, keepdims=True)
        vals.append(mx)
        idxs.append(idx)
        logits = jnp.where(lane == idx, 2.0 * NEG, logits)
    es = [jnp.exp(vk - vals[0]) for vk in vals]
    tot = es[0] + es[1] + es[2] + es[3]
    tk = jnp.zeros(logits.shape, F32)
    picked = jnp.zeros(logits.shape, F32)
    for k in range(TOP_K):
        tk = jnp.where(lane == float(k), es[k] / tot, tk)
        tk = jnp.where(lane == float(TOP_K + k), idxs[k], tk)
        picked = jnp.where(lane == idxs[k], 1.0, picked)
    tk_ref[...] = tk
    cnt_ref[...] = cnt_ref[...] + jnp.sum(picked, axis=0, keepdims=True)


def _merge(h, om, att, x, gm, ga, wo, g1, b1, wr, br, tile, name):
    t = x.shape[0]
    rows = lambda w: pl.BlockSpec((tile, w), lambda i: (i, 0))
    const = lambda a: pl.BlockSpec(a.shape, lambda i: (0, 0))
    return pl.pallas_call(
        _merge_kernel,
        out_shape=[jax.ShapeDtypeStruct((t, D_MODEL), F32), jax.ShapeDtypeStruct((t, XP_TILE, LANES), U32),
                   jax.ShapeDtypeStruct((t, LANES), F32), jax.ShapeDtypeStruct((8, LANES), F32)],
        grid=(t // tile,),
        in_specs=[rows(M_WIDTH), rows(M_WIDTH), rows(A_WIDTH), rows(D_MODEL), const(gm), const(ga), const(wo),
                  const(g1), const(b1), const(wr), const(br)],
        out_specs=[rows(D_MODEL), pl.BlockSpec((tile, XP_TILE, LANES), lambda i: (i, 0, 0)), rows(LANES),
                   pl.BlockSpec((8, LANES), lambda i: (0, 0))],
        compiler_params=_cparams("arbitrary"), name=name,
    )(h, om, att, x, gm, ga, wo, g1, b1, wr, br)


def _route_kernel(tk_ref, first_ref, strict_ref, dest_ref, next_scr):
    @pl.when(pl.program_id(0) == 0)
    def _():
        next_scr[...] = first_ref[...]

    tk = tk_ref[...]
    lane = lax.broadcasted_iota(I32, tk.shape, 1).astype(F32)
    onehots = [jnp.where(lane == tk[:, TOP_K + k:TOP_K + k + 1], 1.0, 0.0) for k in range(TOP_K)]
    tot = onehots[0] + onehots[1] + onehots[2] + onehots[3]
    row = jnp.dot(strict_ref[...], tot.astype(BF16), preferred_element_type=F32) + next_scr[0:1, :]
    out = jnp.zeros(tk.shape, F32)
    for k in range(TOP_K):
        out = jnp.where(lane == float(k), jnp.sum(onehots[k] * row, axis=1, keepdims=True), out)
    dest_ref[...] = out.astype(I32)
    next_scr[...] = next_scr[...] + jnp.sum(tot, axis=0, keepdims=True)


def _route(tk, first):
    t = tk.shape[0]
    tile = min(RANK_TILE, t)
    strict = jnp.asarray(np.tril(np.ones((tile, tile), np.float32), -1), BF16)
    return pl.pallas_call(
        _route_kernel, out_shape=jax.ShapeDtypeStruct((t, LANES), I32), grid=(t // tile,),
        in_specs=[pl.BlockSpec((tile, LANES), lambda i: (i, 0)), pl.BlockSpec((8, LANES), lambda i: (0, 0)),
                  pl.BlockSpec((tile, tile), lambda i: (0, 0))],
        out_specs=pl.BlockSpec((tile, LANES), lambda i: (i, 0)),
        scratch_shapes=[pltpu.VMEM((8, LANES), F32)],
        compiler_params=_cparams("arbitrary"), name="moe_route",
    )(tk, first, strict)


def _offsets_kernel(cnt_ref, off_ref, be_ref, nu_ref, pad_ref, *, tile):
    cnt = cnt_ref[...]
    nblk = jnp.floor((cnt + float(tile - 1)) * (1.0 / tile))
    r_i = lax.broadcasted_iota(I32, (LANES, LANES), 0)
    c_i = lax.broadcasted_iota(I32, (LANES, LANES), 1)
    incl = jnp.where(r_i <= c_i, 1.0, 0.0).astype(BF16)
    cum = jnp.dot(nblk.astype(BF16), incl, preferred_element_type=F32)
    off = (cum - nblk) * float(tile)
    off_ref[...] = off
    which = lax.broadcasted_iota(I32, cnt.shape, 0)
    pad_ref[...] = jnp.where(which == 0, off + cnt, jnp.where(which == 1, nblk * float(tile) - cnt, 0.0)).astype(I32)
    rows = be_ref.shape[0]
    jb = (lax.broadcasted_iota(I32, (rows, LANES), 0) * LANES + lax.broadcasted_iota(I32, (rows, LANES), 1)).astype(F32)
    acc = jnp.zeros((rows, LANES), F32)
    for e in range(N_EXPERTS):
        acc = acc + jnp.where(jb >= cum[0:1, e:e + 1], 1.0, 0.0)
    be_ref[...] = jnp.minimum(acc, float(N_EXPERTS - 1)).astype(I32)
    nu_ref[...] = jnp.broadcast_to(cum[0:1, N_EXPERTS - 1:N_EXPERTS], nu_ref.shape).astype(I32)


def _offsets(cnt, n_blocks, tile):
    rows = -(-n_blocks // LANES)
    rows = -(-rows // 8) * 8
    return pl.pallas_call(
        functools.partial(_offsets_kernel, tile=tile),
        out_shape=[jax.ShapeDtypeStruct((8, LANES), F32), jax.ShapeDtypeStruct((rows, LANES), I32),
                   jax.ShapeDtypeStruct((8, LANES), I32), jax.ShapeDtypeStruct((8, LANES), I32)],
        name="moe_offsets",
    )(cnt)


def _scatter_rows(dest_ref, xp_ref, xs_ref, sem):
    t = xp_ref.shape[0]

    def row_copy(tok, dst):
        return pltpu.make_async_copy(xp_ref.at[pl.ds(tok, 1)], xs_ref.at[pl.ds(dst, 1)], sem)

    def issue(grp, carry):
        base = pl.multiple_of(grp * ISSUE_GROUP, ISSUE_GROUP)
        for u in range(ISSUE_GROUP):
            for k in range(TOP_K):
                row_copy(base + u, dest_ref[(base + u) * TOP_K + k]).start(priority=k % 2)
        return carry

    lax.fori_loop(0, t // ISSUE_GROUP, issue, 0)
    for k in range(TOP_K):
        pltpu.make_async_copy(xp_ref, xs_ref.at[pl.ds(0, t)], sem).wait()


def _dispatch_kernel(dest_ref, pads_ref, dest2_ref, xp_ref, xp2_ref, xs_ref, sem, zsem, zbuf, *, block_rows):
    zr = zbuf.shape[0]
    n_blocks = xs_ref.shape[0] // block_rows

    @pl.when(pl.program_id(0) == 0)
    def _():
        zbuf[...] = jnp.zeros_like(zbuf)
        used = pads_ref[2 * N_EXPERTS]

        def pieces(e, act):
            start, n = pads_ref[e], pads_ref[N_EXPERTS + e]
            for sh in range(zr.bit_length() - 1, -1, -1):
                b = 1 << sh
                before = lax.shift_left(lax.shift_right_logical(n, sh + 1), sh + 1)

                @pl.when((n & b) != 0)
                def _():
                    act(pltpu.make_async_copy(zbuf.at[pl.ds(0, b)], xs_ref.at[pl.ds(start + before, b)], zsem))

        def tail(jb, act):
            for h in range(block_rows // zr):
                act(pltpu.make_async_copy(zbuf, xs_ref.at[pl.ds(jb * block_rows + h * zr, zr)], zsem))

        for act in (lambda cp: cp.start(), lambda cp: cp.wait()):
            lax.fori_loop(0, N_EXPERTS, lambda e, c: (pieces(e, act), c)[1], 0)
            lax.fori_loop(used, n_blocks, lambda jb, c: (tail(jb, act), c)[1], 0)

    _scatter_rows(dest_ref, xp_ref, xs_ref, sem)

    @pl.when(pl.program_id(0) == pl.num_programs(0) - 1)
    def _():
        _scatter_rows(dest2_ref, xp2_ref, xs_ref, sem)


def _dispatch(dest, pads, dest2, xp, xp2, n_rows, block_rows):
    t = xp.shape[0]
    tile = min(ROW_TILE, t)
    return pl.pallas_call(
        functools.partial(_dispatch_kernel, block_rows=block_rows),
        out_shape=jax.ShapeDtypeStruct((n_rows,) + xp.shape[1:], xp.dtype), grid=(t // tile,),
        in_specs=[pl.BlockSpec((tile * TOP_K,), lambda i: (i,), memory_space=pltpu.SMEM),
                  pl.BlockSpec(memory_space=pltpu.SMEM),
                  pl.BlockSpec(memory_space=pltpu.SMEM),
                  pl.BlockSpec((tile,) + xp.shape[1:], lambda i: (i, 0, 0)),
                  pl.BlockSpec(xp2.shape, lambda i: (0, 0, 0))],
        out_specs=pl.BlockSpec(memory_space=pl.ANY),
        scratch_shapes=[pltpu.SemaphoreType.DMA(()), pltpu.SemaphoreType.DMA(()),
                        pltpu.VMEM((EXPERT_TILE // 2,) + xp.shape[1:], xp.dtype)],
        compiler_params=_cparams("arbitrary"), name="moe_dispatch",
    )(dest, pads, dest2, xp, xp2)


def _expert_kernel(be_ref, nu_ref, xs_ref, w1_ref, b1g_ref, b1l_ref, w2_ref, b2_ref, perm_ref, ys_ref,
                   w1g_scr, w1l_scr, w2_scr, xq_scr, y_scr, sem, osem):
    j = pl.program_id(0)
    active = j < nu_ref[0]
    changed = jnp.logical_or(j == 0, be_ref[j] != be_ref[jnp.maximum(j - 1, 0)])
    tm = y_scr.shape[0]
    slot = lax.rem(j, 2)

    def fetch(blk, slot):
        row0 = pl.multiple_of(blk * tm, tm)
        return [pltpu.make_async_copy(xs_ref.at[pl.ds(row0, tm), q, :], xq_scr.at[slot, q], sem.at[slot])
                for q in range(XP_TILE)]

    def put(blk):
        row0 = pl.multiple_of(blk * tm, tm)
        return [pltpu.make_async_copy(y_scr.at[:, q * LANES:(q + 1) * LANES], ys_ref.at[pl.ds(row0, tm), q, :], osem)
                for q in range(YS_TILE)]

    def emit(y):
        @pl.when(j > 0)
        def _():
            for cp in put(j - 1):
                cp.wait()

        y_scr[...] = _pack_halves(y)
        for cp in put(j):
            cp.start()

    @pl.when(j == 0)
    def _():
        for cp in fetch(0, 0):
            cp.start()

    @pl.when(j + 1 < nu_ref[0])
    def _():
        for cp in fetch(j + 1, 1 - slot):
            cp.start()

    @pl.when(jnp.logical_and(active, changed))
    def _():
        for c in range(2 * D_FF // 256):
            wc = w1_ref[0, :, c * 256:(c + 1) * 256].astype(BF16)
            d = jnp.dot(wc, perm_ref[...], preferred_element_type=F32).astype(BF16)
            w1g_scr[:, c * 128:(c + 1) * 128] = d[:, :128]
            w1l_scr[:, c * 128:(c + 1) * 128] = d[:, 128:]
        for c in range(D_FF // 256):
            w2_scr[c * 256:(c + 1) * 256, :] = w2_ref[0, c * 256:(c + 1) * 256, :].astype(BF16)

    @pl.when(active)
    def _():
        for cp in fetch(j, slot):
            cp.wait()
        lo, hi = _unpack_halves(jnp.concatenate([xq_scr[slot, q] for q in range(XP_TILE)], axis=1))
        xb = jnp.concatenate([lo, hi], axis=1)
        hg = jnp.dot(xb, w1g_scr[...], preferred_element_type=F32) + b1g_ref[0]
        hl = jnp.dot(xb, w1l_scr[...], preferred_element_type=F32) + b1l_ref[0]
        x_glu = jnp.minimum(hg, SWIGLU_LIMIT)
        x_lin = jnp.clip(hl, -SWIGLU_LIMIT, SWIGLU_LIMIT)
        a = x_glu * _sigmoid(SWIGLU_ALPHA * x_glu) * (x_lin + 1.0)
        emit(jnp.dot(a.astype(BF16), w2_scr[...], preferred_element_type=F32) + b2_ref[0])

    @pl.when(jnp.logical_not(active))
    def _():
        emit(jnp.zeros((tm, D_MODEL), F32))

    @pl.when(j == pl.num_programs(0) - 1)
    def _():
        for cp in put(j):
            cp.wait()


def _experts(be, nu, xs, w1, b1g, b1l, w2, b2, perm, tile):
    n_blocks = xs.shape[0] // tile
    grid_spec = pltpu.PrefetchScalarGridSpec(
        num_scalar_prefetch=2, grid=(n_blocks,),
        in_specs=[pl.BlockSpec(memory_space=pl.ANY),
                  pl.BlockSpec((1, D_MODEL, 2 * D_FF), lambda j, be, nu: (be[j], 0, 0)),
                  pl.BlockSpec((1, 1, D_FF), lambda j, be, nu: (be[j], 0, 0)),
                  pl.BlockSpec((1, 1, D_FF), lambda j, be, nu: (be[j], 0, 0)),
                  pl.BlockSpec((1, D_FF, D_MODEL), lambda j, be, nu: (be[j], 0, 0)),
                  pl.BlockSpec((1, 1, D_MODEL), lambda j, be, nu: (be[j], 0, 0)),
                  pl.BlockSpec((256, 256), lambda j, be, nu: (0, 0))],
        out_specs=pl.BlockSpec(memory_space=pl.ANY),
        scratch_shapes=[pltpu.VMEM((D_MODEL, D_FF), BF16), pltpu.VMEM((D_MODEL, D_FF), BF16),
                        pltpu.VMEM((D_FF, D_MODEL), BF16), pltpu.VMEM((2, XP_TILE, tile, LANES), U32),
                        pltpu.VMEM((tile, YS_TILE * LANES), U32), pltpu.SemaphoreType.DMA((2,)),
                        pltpu.SemaphoreType.DMA(())])
    return pl.pallas_call(
        _expert_kernel, out_shape=jax.ShapeDtypeStruct((xs.shape[0], YS_TILE, LANES), U32), grid_spec=grid_spec,
        compiler_params=_cparams("arbitrary"), name="moe_experts",
    )(be, nu, xs, w1, b1g, b1l, w2, b2, perm)


def _combine_kernel(dest_ref, next_ref, ys_ref, tk_ref, x1_ref, g2_ref, b2_ref, out_ref, buf, sem):
    i = pl.program_id(0)
    t = x1_ref.shape[0]
    slot = lax.rem(i, 2)

    def gather(idx_ref, s):
        def issue(grp, carry):
            base = pl.multiple_of(grp * 8, 8)
            for u in range(8):
                for k in range(TOP_K):
                    pltpu.make_async_copy(ys_ref.at[idx_ref[(base + u) * TOP_K + k]],
                                          buf.at[s, k, grp, :, u, :], sem.at[s]).start(priority=k % 2)
            return carry

        lax.fori_loop(0, t // 8, issue, 0)

    @pl.when(i == 0)
    def _():
        gather(dest_ref, 0)

    @pl.when(i + 1 < pl.num_programs(0))
    def _():
        gather(next_ref, 1 - slot)

    for k in range(TOP_K):
        for u in range(8):
            pltpu.make_async_copy(ys_ref.at[pl.ds(0, t // 8)], buf.at[slot, k, :, :, u, :], sem.at[slot]).wait()
    tk = tk_ref[...]
    los, his = [], []
    for q in range(YS_TILE):
        lo = hi = None
        for k in range(TOP_K):
            words = buf[slot, k, :, q].reshape(t, LANES)
            g = tk[:, k:k + 1]
            lo_k = g * pltpu.bitcast(words << 16, F32)
            hi_k = g * pltpu.bitcast(words & jnp.uint32(0xFFFF0000), F32)
            lo = lo_k if lo is None else lo + lo_k
            hi = hi_k if hi is None else hi + hi_k
        los.append(lo)
        his.append(hi)
    ff = jnp.concatenate(los + his, axis=1)
    out_ref[...] = _layer_norm(DN_ALPHA * x1_ref[...] + ff, g2_ref[...], b2_ref[...])


def _combine(dest_flat, ys, tk, x1, g2, b2):
    t = x1.shape[0]
    tile = min(ROW_TILE, t)
    n = t // tile
    return pl.pallas_call(
        _combine_kernel, out_shape=jax.ShapeDtypeStruct((t, D_MODEL), F32), grid=(n,),
        in_specs=[pl.BlockSpec((tile * TOP_K,), lambda i: (i,), memory_space=pltpu.SMEM),
                  pl.BlockSpec((tile * TOP_K,), lambda i: (jnp.minimum(i + 1, n - 1),), memory_space=pltpu.SMEM),
                  pl.BlockSpec(memory_space=pl.ANY),
                  pl.BlockSpec((tile, LANES), lambda i: (i, 0)),
                  pl.BlockSpec((tile, D_MODEL), lambda i: (i, 0)),
                  pl.BlockSpec((1, D_MODEL), lambda i: (0, 0)),
                  pl.BlockSpec((1, D_MODEL), lambda i: (0, 0))],
        out_specs=pl.BlockSpec((tile, D_MODEL), lambda i: (i, 0)),
        scratch_shapes=[pltpu.VMEM((2, TOP_K, tile // 8, YS_TILE, 8, LANES), U32), pltpu.SemaphoreType.DMA((2,))],
        compiler_params=_cparams("arbitrary"), name="moe_combine",
    )(dest_flat, dest_flat, ys, tk, x1, g2, b2)


def _rel_bucket(dist):
    exact = REL_BUCKETS // 2
    d = np.maximum(dist, 0)
    log_b = exact + (np.log(np.maximum(d, 1).astype(np.float32) / np.float32(exact))
                     / np.float32(math.log(REL_MAX_DIST / exact)) * np.float32(REL_BUCKETS - exact)).astype(np.int32)
    return np.where(d < exact, d, np.minimum(log_b, REL_BUCKETS - 1)).astype(np.int32)


def _bias_lookup(table, bucket, valid):
    bucket = jnp.asarray(bucket)[None]
    acc = jnp.zeros((table.shape[1],) + bucket.shape[1:], F32)
    for b in range(REL_BUCKETS):
        acc = jnp.where(bucket == b, table[b].reshape((-1,) + (1,) * (bucket.ndim - 1)), acc)
    return jnp.where(jnp.asarray(valid)[None], acc, NEG)


def _bias_tables(rel_bias):
    table = rel_bias.astype(F32)
    r = np.arange(WINDOW)[:, None]
    c = np.arange(2 * WINDOW)[None, :]
    dist = r + WINDOW - c
    valid = (dist >= 0) & (dist < WINDOW)
    dist0 = np.where(c < N_META, N_META + r - c, dist)
    valid0 = np.where(c < N_META, dist0 < WINDOW, (c >= WINDOW) & valid)
    both = jnp.stack([_bias_lookup(table, _rel_bucket(dist0), valid0), _bias_lookup(table, _rel_bucket(dist), valid)])
    dist_s = WINDOW - 1 - np.arange(WINDOW)
    rows = _bias_lookup(table[:, np.asarray(HEAD_ORDER)], _rel_bucket(dist_s), np.ones_like(dist_s, bool))
    return both, rows


def _perm_heads(a, axis):
    assert HEAD_ORDER == tuple(kv * A_GROUP + g for g in range(A_GROUP) for kv in range(A_KV_HEADS))
    shape = a.shape
    a = a.reshape(shape[:axis] + (A_KV_HEADS, A_GROUP, A_HD) + shape[axis + 1:])
    return jnp.swapaxes(a, axis, axis + 1).reshape(shape)


def _rep_rows(vec, rows=8):
    out = jnp.zeros((rows, LANES), F32)
    return out.at[:vec.shape[0], :].set(jnp.broadcast_to(vec.astype(F32)[:, None], (vec.shape[0], LANES)))


def kernel(x_prompt, x_sample, cache_swa_k, cache_swa_v, state_mlstm_C, state_mlstm_n, state_mlstm_m, meta_tokens, rel_bias, w_in, b_igate, b_fgate, attn_sinks, g_mlstm_out, g_attn_out, w_out, ln1_g, ln1_b, w_router, b_router, w_moe1, b_moe1, w_moe2, b_moe2, ln2_g, ln2_b):
    B, S, _ = x_prompt.shape
    NB = x_sample.shape[0]
    assert x_sample.shape[1] == 1 and w_in.shape[0] == 1
    assert S % PROJ_TILE == 0 and S % M_CHUNK == 0 and S % WINDOW == 0 and NB % SAMPLE_GROUP == 0
    l = 0

    assert IN_WIDTHS == (512, 512, 512, 512, 4, 4, 512, 128, 128)
    bf = lambda a: a.astype(BF16)
    w = w_in[l]
    n_main, n_gate = 4 * M_WIDTH, 2 * M_HEADS
    w_gate = w[:, n_main:n_main + n_gate]
    w_att = w[:, n_main + n_gate:]
    assert math.frexp(A_HD ** -0.5)[0] == 0.5
    w_qa = _perm_heads(w_att[:, :A_WIDTH], 1) * (A_HD ** -0.5)
    wr = bf(jnp.concatenate([w[:, :n_main], w_qa, w_att[:, A_WIDTH:],
                             jnp.pad(w_gate, ((0, 0), (0, LANES - n_gate)))], axis=1))
    wt = bf(jnp.concatenate([w[:, :M_WIDTH], w[:, 2 * M_WIDTH:3 * M_WIDTH], w_qa, w_att[:, A_WIDTH + LANES:],
                             w_gate], axis=1).T)
    b_gate = jnp.concatenate([b_igate[l], b_fgate[l]]).astype(F32)
    brow = jnp.pad(b_gate, (0, LANES - n_gate))[None, :]
    bcol = b_gate[:, None]
    plan_p = ((512, 512, "plain", BF16), (1536, 512, "plain", F32), (2560, 128, "plain", BF16),
              (2816, 128, "gate", F32))
    tplan_p = ((0, 512, "plain", BF16), (512, 512, "plain", BF16), (1024, 512, "plain", BF16),
               (1536, 128, "plain", BF16), (1664, 8, "gate", F32))
    plan_s = ((0, 512, "plain", F32), (512, 512, "plain", F32), (1024, 512, "plain", F32), (1536, 512, "plain", F32),
              (2048, 512, "plain", F32), (2560, 128, "plain", F32), (2688, 128, "plain", F32), (2816, 128, "gate", F32))

    bias_tab, bias_rows = _bias_tables(rel_bias)
    sinks = _rep_rows(attn_sinks[l])
    sinks_step = _rep_rows(attn_sinks[l][np.asarray(HEAD_ORDER)])
    g_m = g_mlstm_out[l].astype(F32)[None, :]
    g_a = _perm_heads(g_attn_out[l].astype(F32), 0)[None, :]
    wo = bf(jnp.concatenate([w_out[l][:M_WIDTH], _perm_heads(w_out[l][M_WIDTH:], 0)], axis=0))
    g1, b1 = ln1_g[l].astype(F32)[None, :], ln1_b[l].astype(F32)[None, :]
    g2, b2 = ln2_g[l].astype(F32)[None, :], ln2_b[l].astype(F32)[None, :]
    w_r = bf(jnp.pad(w_router[l], ((0, 0), (0, LANES - N_EXPERTS))))
    b_r = jnp.pad(b_router[l].astype(F32), (0, LANES - N_EXPERTS), constant_values=NEG)[None, :]
    b1g = b_moe1[l][:, 0::2].astype(F32)[:, None, :]
    b1l = b_moe1[l][:, 1::2].astype(F32)[:, None, :]
    b2e = b_moe2[l].astype(F32)[:, None, :]
    pj = np.zeros((256, 256), np.float32)
    pj[2 * np.arange(128), np.arange(128)] = 1.0
    pj[2 * np.arange(128) + 1, 128 + np.arange(128)] = 1.0
    perm = jnp.asarray(pj, BF16)

    xp2 = x_prompt.reshape(B * S, D_MODEL)
    km, om, ka, gc, qt, vt, qat, vat, gr, kv_tail = _proj(
        xp2, wr, wt, brow, bcol, plan_p, tplan_p, (2560, 256), PROJ_TILE, S, "proj_prompt")
    x_meta = jnp.pad(meta_tokens.astype(F32), ((0, M_CHUNK - N_META), (0, 0)))
    km0, _, ka0, gc0, qt0, vt0, _, vat0, gr0 = _proj(
        x_meta, wr, wt, brow, bcol, plan_p, tplan_p, None, M_CHUNK, M_CHUNK, "proj_meta")
    xs2 = x_sample.reshape(NB, D_MODEL)
    qm_s, km_s, vm_s, om_s, qa_s, ka_s, va_s, gc_s = _proj(
        xs2, wr, wt, brow, bcol, plan_s, (), None, NB, NB, "proj_sample")

    zero_c = jnp.zeros((M_HEADS, M_DV + 8, M_DK), F32)
    zero_m = jnp.zeros((8, LANES), F32)
    _, c_meta, m_meta = _mlstm(qt0, km0, vt0, gc0, gr0, zero_c, zero_m, 1, N_META, "mlstm_meta")
    h_p, c_p, m_p = _mlstm(qt, km, vt, gc, gr, c_meta[0], m_meta[0], B, M_CHUNK, "mlstm_prompt")
    C_p = c_p[:, :, :M_DV, :]
    n_p = c_p[:, :, M_DV, :]
    m_prompt = m_p[:, :M_HEADS, 0]
    m_pad = jnp.pad(state_mlstm_m[l].astype(F32), ((0, 0), (0, LANES - M_HEADS)))
    C_s, n_s, m_s, h_s = _mlstm_step(state_mlstm_C[l].astype(F32), state_mlstm_n[l].astype(F32), m_pad,
                                     gc_s, qm_s, km_s, vm_s)

    att_p = _swa(qat, ka, vat, ka0, vat0, jnp.swapaxes(bias_tab, 2, 3), sinks, B)
    ck = cache_swa_k[l].reshape(NB, WINDOW, LANES)
    cv = cache_swa_v[l].reshape(NB, WINDOW, LANES)
    k_new, v_new, att_s = _swa_step(ck, cv, qa_s, ka_s, va_s, bias_rows, sinks_step)

    x1_p, xpk_p, tk_p, cnt_p = _merge(h_p, om, att_p, xp2, g_m, g_a, wo, g1, b1, w_r, b_r, MERGE_TILE, "merge_prompt")
    x1_s, xpk_s, tk_s, cnt_s = _merge(h_s, om_s, att_s, xs2, g_m, g_a, wo, g1, b1, w_r, b_r, NB, "merge_sample")

    T_p = B * S
    assert T_p % RANK_TILE == 0 and T_p % ROW_TILE == 0
    n_blocks = -(-((T_p + NB) * TOP_K) // EXPERT_TILE) + N_EXPERTS
    off, be2, nu2, pad = _offsets(cnt_p + cnt_s, n_blocks, EXPERT_TILE)
    pads = jnp.concatenate([pad[0, :N_EXPERTS], pad[1, :N_EXPERTS], nu2[0, :1]])
    dest_p = _route(tk_p, off)[:, :TOP_K].reshape(-1)
    dest_s = _route(tk_s, off + cnt_p)[:, :TOP_K].reshape(-1)
    be = be2.reshape(-1)[:n_blocks]
    nu = nu2[0, :1]
    xs = _dispatch(dest_p, pads, dest_s, xpk_p, xpk_s, n_blocks * EXPERT_TILE, EXPERT_TILE)
    ys = _experts(be, nu, xs, w_moe1[l], b1g, b1l, w_moe2[l], b2e, perm, EXPERT_TILE)
    y_p = _combine(dest_p, ys, tk_p, x1_p, g2, b2)
    y_s = _combine(dest_s, ys, tk_s, x1_s, g2, b2)

    kv_tail = kv_tail.reshape(B, WINDOW, 2, A_KV_HEADS, A_HD)
    dt_k, dt_v = cache_swa_k.dtype, cache_swa_v.dtype
    return (y_p.reshape(B, S, D_MODEL).astype(x_prompt.dtype), y_s.reshape(NB, 1, D_MODEL).astype(x_sample.dtype),
            kv_tail[:, :, 0][None].astype(dt_k), kv_tail[:, :, 1][None].astype(dt_v),
            C_p[None].astype(state_mlstm_C.dtype), n_p[None].astype(state_mlstm_n.dtype),
            m_prompt[None].astype(state_mlstm_m.dtype),
            k_new.reshape(1, NB, WINDOW, A_KV_HEADS, A_HD).astype(dt_k),
            v_new.reshape(1, NB, WINDOW, A_KV_HEADS, A_HD).astype(dt_v),
            C_s[None].astype(state_mlstm_C.dtype), n_s[None].astype(state_mlstm_n.dtype),
            m_s[:, :M_HEADS][None].astype(state_mlstm_m.dtype))
```

```python
import functools
import math

import numpy as np
import jax
import jax.numpy as jnp
from jax import lax
from jax.experimental import pallas as pl
from jax.experimental.pallas import tpu as pltpu

F32 = jnp.float32
BF16 = jnp.bfloat16
I32 = jnp.int32
U32 = jnp.uint32

D_MODEL = 1024
N_META = 16
M_HEADS = 4
M_DK = 128
M_DV = 128
M_WIDTH = M_HEADS * M_DV
A_HD = 64
A_HEADS = 8
A_KV_HEADS = 2
A_GROUP = A_HEADS // A_KV_HEADS
A_WIDTH = A_HEADS * A_HD
WINDOW = 128
REL_BUCKETS = 32
REL_MAX_DIST = 128
N_EXPERTS = 32
TOP_K = 4
D_FF = D_MODEL
SWIGLU_LIMIT = 7.0
SWIGLU_ALPHA = 1.702
DEPTH = 1
DN_ALPHA = (2.0 * DEPTH) ** 0.25
LN_EPS = 1e-5
IN_WIDTHS = (M_WIDTH, M_WIDTH, M_WIDTH, M_WIDTH, M_HEADS, M_HEADS, A_WIDTH, A_KV_HEADS * A_HD, A_KV_HEADS * A_HD)

LANES = 128
NEG = -1e30
VMEM_LIMIT = 56 * 1024 * 1024

M_CHUNK = 256
PROJ_TILE = 1024
MERGE_TILE = 1024
RANK_TILE = 512
ROW_TILE = 512
EXPERT_TILE = 512
SAMPLE_GROUP = 8
ISSUE_GROUP = 8
SWA_QBLOCKS = 8
SWA_PIPE_LAG = 2
XP_TILE = D_MODEL // 2 // LANES
YS_TILE = D_MODEL // 2 // LANES
HEAD_ORDER = (0, 4, 1, 5, 2, 6, 3, 7)


def _cparams(*sem):
    return pltpu.CompilerParams(dimension_semantics=sem, vmem_limit_bytes=VMEM_LIMIT)


def _log_sigmoid(x):
    return jnp.minimum(x, 0.0) - jnp.log1p(jnp.exp(-jnp.abs(x)))


def _sigmoid(x):
    return 1.0 / (1.0 + jnp.exp(-x))


def _proj_kernel(x_ref, wr_ref, wt_ref, brow_ref, bcol_ref, *outs, row_plan, t_plan, tail_cols):
    xb = x_ref[...].astype(BF16)
    tm = xb.shape[0]
    o = 0
    for (c0, width, kind, _) in row_plan:
        r = jnp.dot(xb, wr_ref[:, c0:c0 + width], preferred_element_type=F32)
        if kind == "gate":
            r = r + brow_ref[...]
            lane = lax.broadcasted_iota(I32, r.shape, 1)
            r = jnp.where(lane < M_HEADS, r, _log_sigmoid(r))
        outs[o][...] = r.astype(outs[o].dtype)
        o += 1
    for (r0, nrows, kind, _) in t_plan:
        r = lax.dot_general(wt_ref[r0:r0 + nrows, :], xb, (((1,), (1,)), ((), ())), preferred_element_type=F32)
        if kind == "gate":
            r = r + bcol_ref[...]
            row = lax.broadcasted_iota(I32, r.shape, 0)
            r = jnp.where(row < M_HEADS, r, _log_sigmoid(r))
        outs[o][...] = r.astype(outs[o].dtype)
        o += 1
    if tail_cols is not None:
        c0, width = tail_cols
        outs[o][...] = jnp.dot(xb[tm - WINDOW:, :], wr_ref[:, c0:c0 + width], preferred_element_type=F32)


def _proj(x, wr, wt, brow, bcol, row_plan, t_plan, tail_cols, tile, rows_per_group, name):
    t = x.shape[0]
    nt = t // tile
    out_shape, out_specs = [], []
    for (_, width, _, dt) in row_plan:
        out_shape.append(jax.ShapeDtypeStruct((t, width), dt))
        out_specs.append(pl.BlockSpec((tile, width), lambda i: (i, 0)))
    for (_, nrows, _, dt) in t_plan:
        out_shape.append(jax.ShapeDtypeStruct((nrows, t), dt))
        out_specs.append(pl.BlockSpec((nrows, tile), lambda i: (0, i)))
    if tail_cols is not None:
        tiles_per_group = rows_per_group // tile
        out_shape.append(jax.ShapeDtypeStruct((t // rows_per_group * WINDOW, tail_cols[1]), F32))
        out_specs.append(pl.BlockSpec((WINDOW, tail_cols[1]), lambda i: (i // tiles_per_group, 0)))
    kern = functools.partial(_proj_kernel, row_plan=row_plan, t_plan=t_plan, tail_cols=tail_cols)
    return pl.pallas_call(
        kern, out_shape=out_shape, grid=(nt,),
        in_specs=[pl.BlockSpec((tile, D_MODEL), lambda i: (i, 0)),
                  pl.BlockSpec(wr.shape, lambda i: (0, 0)),
                  pl.BlockSpec(wt.shape, lambda i: (0, 0)),
                  pl.BlockSpec(brow.shape, lambda i: (0, 0)),
                  pl.BlockSpec(bcol.shape, lambda i: (0, 0))],
        out_specs=out_specs, compiler_params=_cparams("arbitrary"), name=name,
    )(x, wr, wt, brow, bcol)


def _split3(a):
    hi = a.astype(BF16)
    r1 = a - hi.astype(F32)
    mid = r1.astype(BF16)
    lo = (r1 - mid.astype(F32)).astype(BF16)
    return hi, mid, lo


def _mlstm_kernel(qt_ref, k_ref, vt_ref, gc_ref, gr_ref, c0_ref, m0_ref, h_ref, c_out_ref, m_out_ref,
                  c_scr, m_scr, *, n_valid):
    c = pl.program_id(1)
    nc = pl.num_programs(1)
    L = k_ref.shape[0]

    @pl.when(c == 0)
    def _():
        c_scr[...] = c0_ref[...]
        m_scr[...] = m0_ref[...]

    gc = gc_ref[...]
    gr = gr_ref[...]
    if n_valid < L:
        rowc = lax.broadcasted_iota(I32, gc.shape, 0)
        lanec = lax.broadcasted_iota(I32, gc.shape, 1)
        gc = jnp.where(rowc < n_valid, gc, jnp.where(lanec < M_HEADS, NEG, 0.0))
        rowr = lax.broadcasted_iota(I32, gr.shape, 0)
        colr = lax.broadcasted_iota(I32, gr.shape, 1)
        gr = jnp.where(colr < n_valid, gr, jnp.where(rowr < M_HEADS, NEG, 0.0))
    r_i = lax.broadcasted_iota(I32, (L, L), 0)
    c_i = lax.broadcasted_iota(I32, (L, L), 1)
    upper = r_i <= c_i
    tril = jnp.where(c_i <= r_i, 1.0, 0.0).astype(BF16)
    triu = jnp.where(upper, 1.0, 0.0).astype(BF16)
    b_cols = sum(jnp.dot(tril, part, preferred_element_type=F32) for part in _split3(gc))
    b_rows = sum(jnp.dot(part, triu, preferred_element_type=F32) for part in _split3(gr))
    scale = M_DK ** -0.5
    ones_rows = jnp.where(lax.broadcasted_iota(I32, (8, L), 0) == 0, 1.0, 0.0).astype(BF16)

    m_all = m_scr[...]
    c_all = [c_scr[h] for h in range(M_HEADS)]
    h_new, c_new, m_new_all = {}, {}, {}

    def operands(h):
        sl = slice(h * M_DK, (h + 1) * M_DK)
        return qt_ref[sl, :], k_ref[:, sl], jnp.concatenate([vt_ref[sl, :], ones_rows], axis=0)

    def stage_a(h):
        qt, k, vt_aug = operands(h)
        ig_r = gr[h:h + 1, :]
        b_r = b_rows[M_HEADS + h:M_HEADS + h + 1, :]
        m_prev = m_all[h:h + 1, 0:1]
        cs = c_all[h]
        qk = jnp.dot(k, qt, preferred_element_type=F32)
        inter = jnp.dot(cs.astype(BF16), qt, preferred_element_type=F32)
        b_last = b_r[:, L - 1:L]
        g = ig_r + b_last - b_r
        m_new = jnp.maximum(b_last + m_prev, jnp.max(g, axis=1, keepdims=True))
        a = jnp.exp(b_last + m_prev - m_new)
        wv = (vt_aug.astype(F32) * jnp.exp(g - m_new)).astype(BF16)
        c_new[h] = a * cs + jnp.dot(wv, k, preferred_element_type=F32) * scale
        m_new_all[h] = jnp.broadcast_to(m_new, (1, LANES))
        return qk, inter

    def stage_b(h, qk, inter):
        b_r = b_rows[M_HEADS + h:M_HEADS + h + 1, :]
        m_prev = m_all[h:h + 1, 0:1]
        r_c = gc[:, h:h + 1] - b_cols[:, M_HEADS + h:M_HEADS + h + 1]
        dt = jnp.where(upper, b_r + r_c, NEG)
        m_t = jnp.maximum(b_r + m_prev, jnp.max(dt, axis=0, keepdims=True))
        st = (qk * (scale * jnp.exp(dt - m_t))).astype(BF16)
        return st, jnp.exp(b_r + m_prev - m_t) * inter, jnp.exp(-m_t)

    def stage_c(h, st, inter_w, floor):
        _, _, vt_aug = operands(h)
        nd = inter_w + jnp.dot(vt_aug, st, preferred_element_type=F32)
        den = nd[M_DV:M_DV + 1, :]
        h_new[h] = (nd[:M_DV, :] / jnp.maximum(jnp.abs(den), floor)).T

    a_q, b_q = {}, {}
    for i in range(M_HEADS + 2):
        if i < M_HEADS:
            a_q[i] = stage_a(i)
        if 0 <= i - 1 < M_HEADS:
            b_q[i - 1] = stage_b(i - 1, *a_q.pop(i - 1))
        if 0 <= i - 2 < M_HEADS:
            stage_c(i - 2, *b_q.pop(i - 2))

    h_ref[...] = jnp.concatenate([h_new[h] for h in range(M_HEADS)], axis=1)
    for h in range(M_HEADS):
        c_scr[h] = c_new[h]
    m_scr[0:M_HEADS, :] = jnp.concatenate([m_new_all[h] for h in range(M_HEADS)], axis=0)

    @pl.when(c == nc - 1)
    def _():
        c_out_ref[0] = c_scr[...]
        m_out_ref[0] = m_scr[...]


def _mlstm(qt, km, vt, gc, gr, c0, m0, batch, n_valid, name):
    L = M_CHUNK
    nc = km.shape[0] // (batch * L)
    kern = functools.partial(_mlstm_kernel, n_valid=n_valid)
    rows = pl.BlockSpec((L, M_WIDTH), lambda b, c: (b * nc + c, 0))
    cols = pl.BlockSpec((M_WIDTH, L), lambda b, c: (0, b * nc + c))
    return pl.pallas_call(
        kern,
        out_shape=[jax.ShapeDtypeStruct((batch * nc * L, M_WIDTH), F32),
                   jax.ShapeDtypeStruct((batch, M_HEADS, M_DV + 8, M_DK), F32),
                   jax.ShapeDtypeStruct((batch, 8, LANES), F32)],
        grid=(batch, nc),
        in_specs=[cols, rows, cols,
                  pl.BlockSpec((L, LANES), lambda b, c: (b * nc + c, 0)),
                  pl.BlockSpec((8, L), lambda b, c: (0, b * nc + c)),
                  pl.BlockSpec((M_HEADS, M_DV + 8, M_DK), lambda b, c: (0, 0, 0)),
                  pl.BlockSpec((8, LANES), lambda b, c: (0, 0))],
        out_specs=[rows,
                   pl.BlockSpec((1, M_HEADS, M_DV + 8, M_DK), lambda b, c: (b, 0, 0, 0)),
                   pl.BlockSpec((1, 8, LANES), lambda b, c: (b, 0, 0))],
        scratch_shapes=[pltpu.VMEM((M_HEADS, M_DV + 8, M_DK), F32), pltpu.VMEM((8, LANES), F32)],
        compiler_params=_cparams("arbitrary", "arbitrary"), name=name,
    )(qt, km, vt, gc, gr, c0, m0)


def _outer_f32(a, b):
    ah, am, al = (t.astype(F32) for t in _split3(a))
    bh, bm, bl = (t.astype(F32) for t in _split3(b))
    z = jnp.zeros_like(ah)
    lhs = jnp.concatenate([ah, ah, ah, am, am, al, z, z], axis=0).astype(BF16)
    rhs = jnp.concatenate([bh, bm, bl, bh, bm, bh, z, z], axis=0).astype(BF16)
    return lax.dot_general(lhs, rhs, (((0,), (0,)), ((), ())), preferred_element_type=F32)


def _mlstm_step_kernel(c_ref, n_ref, m_ref, gc_ref, q_ref, k_ref, v_ref,
                       c_out_ref, n_out_ref, m_out_ref, h_ref):
    g = c_ref.shape[0]
    scale = M_DK ** -0.5
    lane_m = lax.broadcasted_iota(I32, (1, LANES), 1)
    for j in range(g):
        m_row = jnp.zeros((1, LANES), F32)
        for h in range(M_HEADS):
            sl = slice(h * M_DK, (h + 1) * M_DK)
            q = q_ref[j:j + 1, sl]
            k = k_ref[j:j + 1, sl] * scale
            v = v_ref[j:j + 1, sl]
            ig = gc_ref[j:j + 1, h:h + 1]
            lf = gc_ref[j:j + 1, M_HEADS + h:M_HEADS + h + 1]
            m = m_ref[j:j + 1, h:h + 1]
            c = c_ref[j, h]
            n = n_ref[j, h:h + 1, :]
            m_t = jnp.maximum(lf + m, ig)
            w = jnp.exp(lf + m - m_t)
            wg = jnp.exp(ig - m_t)
            s = jnp.sum(q * k, axis=1, keepdims=True) * wg
            q8 = jnp.broadcast_to(q, (8, M_DK)).astype(BF16)
            cq = lax.dot_general(q8, c.astype(BF16), (((1,), (1,)), ((), ())), preferred_element_type=F32)[0:1, :]
            den = w * jnp.sum(n * q, axis=1, keepdims=True) + s
            h_ref[j:j + 1, sl] = (w * cq + s * v) / jnp.maximum(jnp.abs(den), jnp.exp(-m_t))
            c_out_ref[j, h] = w * c + _outer_f32(wg * v, k)
            n_out_ref[j, h:h + 1, :] = w * n + wg * k
            m_row = jnp.where(lane_m == h, m_t, m_row)
        m_out_ref[j:j + 1, :] = m_row


def _mlstm_step(c, n, m_pad, gc, q, k, v):
    nb = c.shape[0]
    g = SAMPLE_GROUP
    row = lambda w: pl.BlockSpec((g, w), lambda i: (i, 0))
    return pl.pallas_call(
        _mlstm_step_kernel,
        out_shape=[jax.ShapeDtypeStruct(c.shape, F32), jax.ShapeDtypeStruct(n.shape, F32),
                   jax.ShapeDtypeStruct((nb, LANES), F32), jax.ShapeDtypeStruct((nb, M_WIDTH), F32)],
        grid=(nb // g,),
        in_specs=[pl.BlockSpec((g, M_HEADS, M_DV, M_DK), lambda i: (i, 0, 0, 0)),
                  pl.BlockSpec((g, M_HEADS, M_DK), lambda i: (i, 0, 0)),
                  row(LANES), row(LANES), row(M_WIDTH), row(M_WIDTH), row(M_WIDTH)],
        out_specs=[pl.BlockSpec((g, M_HEADS, M_DV, M_DK), lambda i: (i, 0, 0, 0)),
                   pl.BlockSpec((g, M_HEADS, M_DK), lambda i: (i, 0, 0)),
                   row(LANES), row(M_WIDTH)],
        compiler_params=_cparams("arbitrary"), name="mlstm_step",
    )(c, n, m_pad, gc, q, k, v)


def _swa_kernel(qt_ref, kc_ref, kp_ref, vtc_ref, vtp_ref, km_ref, vtm_ref, bias_ref, sink_ref, o_ref):
    j = pl.program_id(1)
    first = j == 0
    blk = WINDOW
    nqb = qt_ref.shape[1] // blk
    kp = jnp.where(first, km_ref[...], kp_ref[...])
    vtp = jnp.where(first, vtm_ref[...], vtp_ref[...])
    k = jnp.concatenate([kp, kc_ref[...]], axis=0)
    vt = jnp.concatenate([vtp, vtc_ref[...]], axis=1)
    row_v = lax.broadcasted_iota(I32, vt.shape, 0)
    zero_v = jnp.zeros_like(vt)
    vt_half = (jnp.where(row_v < A_HD, vt, zero_v), jnp.where(row_v >= A_HD, vt, zero_v))
    row_q = lax.broadcasted_iota(I32, (LANES, blk), 0)
    lo_rows = row_q < A_HD
    def scores(u, p):
        cols = slice(u * blk, (u + 1) * blk)
        keys = slice(u * blk, (u + 2) * blk)
        qs = qt_ref[p * LANES:(p + 1) * LANES, cols]
        zero_q = jnp.zeros_like(qs)
        q_own = (jnp.where(lo_rows, qs, zero_q), jnp.where(lo_rows, zero_q, qs))
        return [jnp.dot(k[keys], q_own[half], preferred_element_type=F32) for half in range(2)]

    def softmax(u, p, s2):
        table = jnp.where(first, 0, 1) if u == 0 else 1
        probs, inv = [], []
        for half in range(2):
            hd = HEAD_ORDER[2 * p + half]
            s = s2[half] + bias_ref[table, hd]
            sk = sink_ref[hd:hd + 1, 0:1]
            m = jnp.maximum(jnp.max(s, axis=0, keepdims=True), sk)
            e = jnp.exp(s - m)
            probs.append(e.astype(BF16))
            inv.append(1.0 / (jnp.sum(e, axis=0, keepdims=True) + jnp.exp(sk - m)))
        return jnp.concatenate(probs, axis=0), jnp.where(lo_rows, inv[0], inv[1])

    def values(u, p, probs, inv):
        keys = slice(u * blk, (u + 2) * blk)
        vt_stack = jnp.concatenate([vt_half[0][:, keys], vt_half[1][:, keys]], axis=1)
        ot = jnp.dot(vt_stack, probs, preferred_element_type=F32)
        o_ref[u * blk:(u + 1) * blk, p * LANES:(p + 1) * LANES] = (ot * inv).T

    units = [(u, p) for u in range(nqb) for p in range(A_GROUP)]
    s_q, p_q = {}, {}
    lag = SWA_PIPE_LAG
    for i in range(len(units) + 2 * lag):
        if i < len(units):
            s_q[i] = scores(*units[i])
        if 0 <= i - lag < len(units):
            p_q[i - lag] = softmax(*units[i - lag], s_q.pop(i - lag))
        if 0 <= i - 2 * lag < len(units):
            values(*units[i - 2 * lag], *p_q.pop(i - 2 * lag))


def _swa(qat, ka, vat, kmeta, vtmeta, bias_t, sinks, batch):
    blk = WINDOW
    nqb = SWA_QBLOCKS
    t = ka.shape[0]
    nq = t // (batch * blk * nqb)
    prev = lambda b, j: (b * nq + j) * nqb + jnp.where(j == 0, 0, -1)
    const2 = lambda shape: pl.BlockSpec(shape, lambda b, j: (0, 0))
    return pl.pallas_call(
        _swa_kernel, out_shape=jax.ShapeDtypeStruct((t, A_WIDTH), F32), grid=(batch, nq),
        in_specs=[pl.BlockSpec((A_WIDTH, nqb * blk), lambda b, j: (0, b * nq + j)),
                  pl.BlockSpec((nqb * blk, LANES), lambda b, j: (b * nq + j, 0)),
                  pl.BlockSpec((blk, LANES), lambda b, j: (prev(b, j), 0)),
                  pl.BlockSpec((LANES, nqb * blk), lambda b, j: (0, b * nq + j)),
                  pl.BlockSpec((LANES, blk), lambda b, j: (0, prev(b, j))),
                  const2((blk, LANES)), const2((LANES, blk)),
                  pl.BlockSpec(bias_t.shape, lambda b, j: (0, 0, 0, 0)),
                  const2((8, LANES))],
        out_specs=pl.BlockSpec((nqb * blk, A_WIDTH), lambda b, j: (b * nq + j, 0)),
        compiler_params=_cparams("arbitrary", "arbitrary"), name="swa_prompt",
    )(qat, ka, ka, vat, vat, kmeta, vtmeta, bias_t, sinks)


def _swa_step_kernel(ck_ref, cv_ref, q_ref, k_ref, v_ref, bias_ref, sink_ref, ko_ref, vo_ref, o_ref):
    g = ck_ref.shape[0]
    lane = lax.broadcasted_iota(I32, (A_HEADS, LANES), 1)
    row = lax.broadcasted_iota(I32, (A_HEADS, LANES), 0)
    own_half = (row % 2 == 0) == (lane < A_HD)
    bias = bias_ref[...]
    sk = sink_ref[:, 0:1]
    for j in range(g):
        ko_ref[j, 0:WINDOW - 1, :] = ck_ref[j, 1:WINDOW, :]
        ko_ref[j, WINDOW - 1:WINDOW, :] = k_ref[j:j + 1, :]
        vo_ref[j, 0:WINDOW - 1, :] = cv_ref[j, 1:WINDOW, :]
        vo_ref[j, WINDOW - 1:WINDOW, :] = v_ref[j:j + 1, :]
        kk = ko_ref[j].astype(BF16)
        vv = vo_ref[j].astype(BF16)
        slabs = [q_ref[j:j + 1, p * LANES:(p + 1) * LANES] for p in range(A_GROUP)]
        q8 = jnp.concatenate([slabs[r // 2] for r in range(A_HEADS)], axis=0)
        q8 = jnp.where(own_half, q8, 0.0).astype(BF16)
        s = lax.dot_general(q8, kk, (((1,), (1,)), ((), ())), preferred_element_type=F32)
        s = s + bias
        m = jnp.maximum(jnp.max(s, axis=1, keepdims=True), sk)
        e = jnp.exp(s - m)
        inv = 1.0 / (jnp.sum(e, axis=1, keepdims=True) + jnp.exp(sk - m))
        o8 = jnp.where(own_half, jnp.dot(e.astype(BF16), vv, preferred_element_type=F32) * inv, 0.0)
        for p in range(A_GROUP):
            o_ref[j:j + 1, p * LANES:(p + 1) * LANES] = o8[2 * p:2 * p + 1, :] + o8[2 * p + 1:2 * p + 2, :]


def _swa_step(ck, cv, q, k, v, bias_rows, sinks):
    nb = ck.shape[0]
    g = SAMPLE_GROUP
    cache = pl.BlockSpec((g, WINDOW, LANES), lambda i: (i, 0, 0))
    row = lambda w: pl.BlockSpec((g, w), lambda i: (i, 0))
    const = lambda a: pl.BlockSpec(a.shape, lambda i: (0, 0))
    return pl.pallas_call(
        _swa_step_kernel,
        out_shape=[jax.ShapeDtypeStruct(ck.shape, F32), jax.ShapeDtypeStruct(cv.shape, F32),
                   jax.ShapeDtypeStruct((nb, A_WIDTH), F32)],
        grid=(nb // g,),
        in_specs=[cache, cache, row(A_WIDTH), row(LANES), row(LANES), const(bias_rows), const(sinks)],
        out_specs=[cache, cache, row(A_WIDTH)],
        compiler_params=_cparams("arbitrary"), name="swa_step",
    )(ck, cv, q, k, v, bias_rows, sinks)


def _layer_norm(z, g, b):
    mu = jnp.mean(z, axis=1, keepdims=True)
    zc = z - mu
    var = jnp.mean(zc * zc, axis=1, keepdims=True)
    return zc * lax.rsqrt(var + LN_EPS) * g + b


def _pack_halves(x):
    w = x.shape[1] // 2
    lo = pltpu.bitcast(x[:, :w].astype(BF16).astype(F32), U32)
    hi = pltpu.bitcast(x[:, w:].astype(BF16).astype(F32), U32)
    return (lo >> 16) | (hi & jnp.uint32(0xFFFF0000))


def _unpack_halves(words):
    lo = pltpu.bitcast(words << 16, F32).astype(BF16)
    hi = pltpu.bitcast(words & jnp.uint32(0xFFFF0000), F32).astype(BF16)
    return lo, hi


def _to_token_tiles(ref, x):
    for q in range(x.shape[1] // LANES):
        ref[:, q, :] = x[:, q * LANES:(q + 1) * LANES]


def _merge_kernel(h_ref, om_ref, att_ref, x_ref, gm_ref, ga_ref, wo_ref, g1_ref, b1_ref, wr_ref, br_ref,
                  x1_ref, xp_ref, tk_ref, cnt_ref):
    @pl.when(pl.program_id(0) == 0)
    def _():
        cnt_ref[...] = jnp.zeros_like(cnt_ref)

    hm = h_ref[...] * _sigmoid(om_ref[...])
    ym = hm * lax.rsqrt(jnp.mean(hm * hm, axis=1, keepdims=True) + LN_EPS) * gm_ref[...]
    att = att_ref[...]
    ya = att * lax.rsqrt(jnp.mean(att * att, axis=1, keepdims=True) + LN_EPS) * ga_ref[...]
    mix = (jnp.dot(ym.astype(BF16), wo_ref[0:M_WIDTH, :], preferred_element_type=F32)
           + jnp.dot(ya.astype(BF16), wo_ref[M_WIDTH:, :], preferred_element_type=F32))
    x1 = _layer_norm(DN_ALPHA * x_ref[...] + mix, g1_ref[...], b1_ref[...])
    x1_ref[...] = x1
    _to_token_tiles(xp_ref, _pack_halves(x1))
    logits = jnp.dot(x1.astype(BF16), wr_ref[...], preferred_element_type=F32) + br_ref[...]
    lane = lax.broadcasted_iota(I32, logits.shape, 1).astype(F32)
    vals, idxs = [], []
    for _ in range(TOP_K):
        mx = jnp.max(logits, axis=1, keepdims=True)
        idx = jnp.min(jnp.where(logits == mx, lane, float(LANES)), axis=1, keepdims=True)
        vals.append(mx)
        idxs.append(idx)
        logits = jnp.where(lane == idx, 2.0 * NEG, logits)
    es = [jnp.exp(vk - vals[0]) for vk in vals]
    tot = es[0] + es[1] + es[2] + es[3]
    tk = jnp.zeros(logits.shape, F32)
    picked = jnp.zeros(logits.shape, F32)
    for k in range(TOP_K):
        tk = jnp.where(lane == float(k), es[k] / tot, tk)
        tk = jnp.where(lane == float(TOP_K + k), idxs[k], tk)
        picked = jnp.where(lane == idxs[k], 1.0, picked)
    tk_ref[...] = tk
    cnt_ref[...] = cnt_ref[...] + jnp.sum(picked, axis=0, keepdims=True)


def _merge(h, om, att, x, gm, ga, wo, g1, b1, wr, br, tile, name):
    t = x.shape[0]
    rows = lambda w: pl.BlockSpec((tile, w), lambda i: (i, 0))
    const = lambda a: pl.BlockSpec(a.shape, lambda i: (0, 0))
    return pl.pallas_call(
        _merge_kernel,
        out_shape=[jax.ShapeDtypeStruct((t, D_MODEL), F32), jax.ShapeDtypeStruct((t, XP_TILE, LANES), U32),
                   jax.ShapeDtypeStruct((t, LANES), F32), jax.ShapeDtypeStruct((8, LANES), F32)],
        grid=(t // tile,),
        in_specs=[rows(M_WIDTH), rows(M_WIDTH), rows(A_WIDTH), rows(D_MODEL), const(gm), const(ga), const(wo),
                  const(g1), const(b1), const(wr), const(br)],
        out_specs=[rows(D_MODEL), pl.BlockSpec((tile, XP_TILE, LANES), lambda i: (i, 0, 0)), rows(LANES),
                   pl.BlockSpec((8, LANES), lambda i: (0, 0))],
        compiler_params=_cparams("arbitrary"), name=name,
    )(h, om, att, x, gm, ga, wo, g1, b1, wr, br)


def _route_kernel(tk_ref, first_ref, strict_ref, dest_ref, next_scr):
    @pl.when(pl.program_id(0) == 0)
    def _():
        next_scr[...] = first_ref[...]

    tk = tk_ref[...]
    lane = lax.broadcasted_iota(I32, tk.shape, 1).astype(F32)
    onehots = [jnp.where(lane == tk[:, TOP_K + k:TOP_K + k + 1], 1.0, 0.0) for k in range(TOP_K)]
    tot = onehots[0] + onehots[1] + onehots[2] + onehots[3]
    row = jnp.dot(strict_ref[...], tot.astype(BF16), preferred_element_type=F32) + next_scr[0:1, :]
    out = jnp.zeros(tk.shape, F32)
    for k in range(TOP_K):
        out = jnp.where(lane == float(k), jnp.sum(onehots[k] * row, axis=1, keepdims=True), out)
    dest_ref[...] = out.astype(I32)
    next_scr[...] = next_scr[...] + jnp.sum(tot, axis=0, keepdims=True)


def _route(tk, first):
    t = tk.shape[0]
    tile = min(RANK_TILE, t)
    strict = jnp.asarray(np.tril(np.ones((tile, tile), np.float32), -1), BF16)
    return pl.pallas_call(
        _route_kernel, out_shape=jax.ShapeDtypeStruct((t, LANES), I32), grid=(t // tile,),
        in_specs=[pl.BlockSpec((tile, LANES), lambda i: (i, 0)), pl.BlockSpec((8, LANES), lambda i: (0, 0)),
                  pl.BlockSpec((tile, tile), lambda i: (0, 0))],
        out_specs=pl.BlockSpec((tile, LANES), lambda i: (i, 0)),
        scratch_shapes=[pltpu.VMEM((8, LANES), F32)],
        compiler_params=_cparams("arbitrary"), name="moe_route",
    )(tk, first, strict)


def _offsets_kernel(cnt_ref, off_ref, be_ref, nu_ref, pad_ref, *, tile):
    cnt = cnt_ref[...]
    nblk = jnp.floor((cnt + float(tile - 1)) * (1.0 / tile))
    r_i = lax.broadcasted_iota(I32, (LANES, LANES), 0)
    c_i = lax.broadcasted_iota(I32, (LANES, LANES), 1)
    incl = jnp.where(r_i <= c_i, 1.0, 0.0).astype(BF16)
    cum = jnp.dot(nblk.astype(BF16), incl, preferred_element_type=F32)
    off = (cum - nblk) * float(tile)
    off_ref[...] = off
    which = lax.broadcasted_iota(I32, cnt.shape, 0)
    pad_ref[...] = jnp.where(which == 0, off + cnt, jnp.where(which == 1, nblk * float(tile) - cnt, 0.0)).astype(I32)
    rows = be_ref.shape[0]
    jb = (lax.broadcasted_iota(I32, (rows, LANES), 0) * LANES + lax.broadcasted_iota(I32, (rows, LANES), 1)).astype(F32)
    acc = jnp.zeros((rows, LANES), F32)
    for e in range(N_EXPERTS):
        acc = acc + jnp.where(jb >= cum[0:1, e:e + 1], 1.0, 0.0)
    be_ref[...] = jnp.minimum(acc, float(N_EXPERTS - 1)).astype(I32)
    nu_ref[...] = jnp.broadcast_to(cum[0:1, N_EXPERTS - 1:N_EXPERTS], nu_ref.shape).astype(I32)


def _offsets(cnt, n_blocks, tile):
    rows = -(-n_blocks // LANES)
    rows = -(-rows // 8) * 8
    return pl.pallas_call(
        functools.partial(_offsets_kernel, tile=tile),
        out_shape=[jax.ShapeDtypeStruct((8, LANES), F32), jax.ShapeDtypeStruct((rows, LANES), I32),
                   jax.ShapeDtypeStruct((8, LANES), I32), jax.ShapeDtypeStruct((8, LANES), I32)],
        name="moe_offsets",
    )(cnt)


def _scatter_rows(dest_ref, xp_ref, xs_ref, sem):
    t = xp_ref.shape[0]

    def row_copy(tok, dst):
        return pltpu.make_async_copy(xp_ref.at[pl.ds(tok, 1)], xs_ref.at[pl.ds(dst, 1)], sem)

    def issue(grp, carry):
        base = pl.multiple_of(grp * ISSUE_GROUP, ISSUE_GROUP)
        for u in range(ISSUE_GROUP):
            for k in range(TOP_K):
                row_copy(base + u, dest_ref[(base + u) * TOP_K + k]).start(priority=k % 2)
        return carry

    lax.fori_loop(0, t // ISSUE_GROUP, issue, 0)
    for k in range(TOP_K):
        pltpu.make_async_copy(xp_ref, xs_ref.at[pl.ds(0, t)], sem).wait()


def _dispatch_kernel(dest_ref, pads_ref, dest2_ref, xp_ref, xp2_ref, xs_ref, sem, zsem, zbuf, *, block_rows):
    zr = zbuf.shape[0]
    n_blocks = xs_ref.shape[0] // block_rows

    @pl.when(pl.program_id(0) == 0)
    def _():
        zbuf[...] = jnp.zeros_like(zbuf)
        used = pads_ref[2 * N_EXPERTS]

        def pieces(e, act):
            start, n = pads_ref[e], pads_ref[N_EXPERTS + e]
            for sh in range(zr.bit_length() - 1, -1, -1):
                b = 1 << sh
                before = lax.shift_left(lax.shift_right_logical(n, sh + 1), sh + 1)

                @pl.when((n & b) != 0)
                def _():
                    act(pltpu.make_async_copy(zbuf.at[pl.ds(0, b)], xs_ref.at[pl.ds(start + before, b)], zsem))

        def tail(jb, act):
            for h in range(block_rows // zr):
                act(pltpu.make_async_copy(zbuf, xs_ref.at[pl.ds(jb * block_rows + h * zr, zr)], zsem))

        for act in (lambda cp: cp.start(), lambda cp: cp.wait()):
            lax.fori_loop(0, N_EXPERTS, lambda e, c: (pieces(e, act), c)[1], 0)
            lax.fori_loop(used, n_blocks, lambda jb, c: (tail(jb, act), c)[1], 0)

    _scatter_rows(dest_ref, xp_ref, xs_ref, sem)

    @pl.when(pl.program_id(0) == pl.num_programs(0) - 1)
    def _():
        _scatter_rows(dest2_ref, xp2_ref, xs_ref, sem)


def _dispatch(dest, pads, dest2, xp, xp2, n_rows, block_rows):
    t = xp.shape[0]
    tile = min(ROW_TILE, t)
    return pl.pallas_call(
        functools.partial(_dispatch_kernel, block_rows=block_rows),
        out_shape=jax.ShapeDtypeStruct((n_rows,) + xp.shape[1:], xp.dtype), grid=(t // tile,),
        in_specs=[pl.BlockSpec((tile * TOP_K,), lambda i: (i,), memory_space=pltpu.SMEM),
                  pl.BlockSpec(memory_space=pltpu.SMEM),
                  pl.BlockSpec(memory_space=pltpu.SMEM),
                  pl.BlockSpec((tile,) + xp.shape[1:], lambda i: (i, 0, 0)),
                  pl.BlockSpec(xp2.shape, lambda i: (0, 0, 0))],
        out_specs=pl.BlockSpec(memory_space=pl.ANY),
        scratch_shapes=[pltpu.SemaphoreType.DMA(()), pltpu.SemaphoreType.DMA(()),
                        pltpu.VMEM((EXPERT_TILE // 2,) + xp.shape[1:], xp.dtype)],
        compiler_params=_cparams("arbitrary"), name="moe_dispatch",
    )(dest, pads, dest2, xp, xp2)


def _expert_kernel(be_ref, nu_ref, xs_ref, w1_ref, b1g_ref, b1l_ref, w2_ref, b2_ref, perm_ref, ys_ref,
                   w1g_scr, w1l_scr, w2_scr, xq_scr, y_scr, sem, osem):
    j = pl.program_id(0)
    active = j < nu_ref[0]
    changed = jnp.logical_or(j == 0, be_ref[j] != be_ref[jnp.maximum(j - 1, 0)])
    tm = y_scr.shape[0]
    slot = lax.rem(j, 2)

    def fetch(blk, slot):
        row0 = pl.multiple_of(blk * tm, tm)
        return [pltpu.make_async_copy(xs_ref.at[pl.ds(row0, tm), q, :], xq_scr.at[slot, q], sem.at[slot])
                for q in range(XP_TILE)]

    def put(blk):
        row0 = pl.multiple_of(blk * tm, tm)
        return [pltpu.make_async_copy(y_scr.at[:, q * LANES:(q + 1) * LANES], ys_ref.at[pl.ds(row0, tm), q, :], osem)
                for q in range(YS_TILE)]

    def emit(y):
        @pl.when(j > 0)
        def _():
            for cp in put(j - 1):
                cp.wait()

        y_scr[...] = _pack_halves(y)
        for cp in put(j):
            cp.start()

    @pl.when(j == 0)
    def _():
        for cp in fetch(0, 0):
            cp.start()

    @pl.when(j + 1 < nu_ref[0])
    def _():
        for cp in fetch(j + 1, 1 - slot):
            cp.start()

    @pl.when(jnp.logical_and(active, changed))
    def _():
        for c in range(2 * D_FF // 256):
            wc = w1_ref[0, :, c * 256:(c + 1) * 256].astype(BF16)
            d = jnp.dot(wc, perm_ref[...], preferred_element_type=F32).astype(BF16)
            w1g_scr[:, c * 128:(c + 1) * 128] = d[:, :128]
            w1l_scr[:, c * 128:(c + 1) * 128] = d[:, 128:]
        for c in range(D_FF // 256):
            w2_scr[c * 256:(c + 1) * 256, :] = w2_ref[0, c * 256:(c + 1) * 256, :].astype(BF16)

    @pl.when(active)
    def _():
        for cp in fetch(j, slot):
            cp.wait()
        lo, hi = _unpack_halves(jnp.concatenate([xq_scr[slot, q] for q in range(XP_TILE)], axis=1))
        xb = jnp.concatenate([lo, hi], axis=1)
        hg = jnp.dot(xb, w1g_scr[...], preferred_element_type=F32) + b1g_ref[0]
        hl = jnp.dot(xb, w1l_scr[...], preferred_element_type=F32) + b1l_ref[0]
        x_glu = jnp.minimum(hg, SWIGLU_LIMIT)
        x_lin = jnp.clip(hl, -SWIGLU_LIMIT, SWIGLU_LIMIT)
        a = x_glu * _sigmoid(SWIGLU_ALPHA * x_glu) * (x_lin + 1.0)
        emit(jnp.dot(a.astype(BF16), w2_scr[...], preferred_element_type=F32) + b2_ref[0])

    @pl.when(jnp.logical_not(active))
    def _():
        emit(jnp.zeros((tm, D_MODEL), F32))

    @pl.when(j == pl.num_programs(0) - 1)
    def _():
        for cp in put(j):
            cp.wait()


def _experts(be, nu, xs, w1, b1g, b1l, w2, b2, perm, tile):
    n_blocks = xs.shape[0] // tile
    grid_spec = pltpu.PrefetchScalarGridSpec(
        num_scalar_prefetch=2, grid=(n_blocks,),
        in_specs=[pl.BlockSpec(memory_space=pl.ANY),
                  pl.BlockSpec((1, D_MODEL, 2 * D_FF), lambda j, be, nu: (be[j], 0, 0)),
                  pl.BlockSpec((1, 1, D_FF), lambda j, be, nu: (be[j], 0, 0)),
                  pl.BlockSpec((1, 1, D_FF), lambda j, be, nu: (be[j], 0, 0)),
                  pl.BlockSpec((1, D_FF, D_MODEL), lambda j, be, nu: (be[j], 0, 0)),
                  pl.BlockSpec((1, 1, D_MODEL), lambda j, be, nu: (be[j], 0, 0)),
                  pl.BlockSpec((256, 256), lambda j, be, nu: (0, 0))],
        out_specs=pl.BlockSpec(memory_space=pl.ANY),
        scratch_shapes=[pltpu.VMEM((D_MODEL, D_FF), BF16), pltpu.VMEM((D_MODEL, D_FF), BF16),
                        pltpu.VMEM((D_FF, D_MODEL), BF16), pltpu.VMEM((2, XP_TILE, tile, LANES), U32),
                        pltpu.VMEM((tile, YS_TILE * LANES), U32), pltpu.SemaphoreType.DMA((2,)),
                        pltpu.SemaphoreType.DMA(())])
    return pl.pallas_call(
        _expert_kernel, out_shape=jax.ShapeDtypeStruct((xs.shape[0], YS_TILE, LANES), U32), grid_spec=grid_spec,
        compiler_params=_cparams("arbitrary"), name="moe_experts",
    )(be, nu, xs, w1, b1g, b1l, w2, b2, perm)


def _combine_kernel(dest_ref, next_ref, ys_ref, tk_ref, x1_ref, g2_ref, b2_ref, out_ref, buf, sem):
    i = pl.program_id(0)
    t = x1_ref.shape[0]
    slot = lax.rem(i, 2)

    def gather(idx_ref, s):
        def issue(grp, carry):
            base = pl.multiple_of(grp * 8, 8)
            for u in range(8):
                for k in range(TOP_K):
                    pltpu.make_async_copy(ys_ref.at[idx_ref[(base + u) * TOP_K + k]],
                                          buf.at[s, k, grp, :, u, :], sem.at[s]).start(priority=k % 2)
            return carry

        lax.fori_loop(0, t // 8, issue, 0)

    @pl.when(i == 0)
    def _():
        gather(dest_ref, 0)

    @pl.when(i + 1 < pl.num_programs(0))
    def _():
        gather(next_ref, 1 - slot)

    for k in range(TOP_K):
        for u in range(8):
            pltpu.make_async_copy(ys_ref.at[pl.ds(0, t // 8)], buf.at[slot, k, :, :, u, :], sem.at[slot]).wait()
    tk = tk_ref[...]
    los, his = [], []
    for q in range(YS_TILE):
        lo = hi = None
        for k in range(TOP_K):
            words = buf[slot, k, :, q].reshape(t, LANES)
            g = tk[:, k:k + 1]
            lo_k = g * pltpu.bitcast(words << 16, F32)
            hi_k = g * pltpu.bitcast(words & jnp.uint32(0xFFFF0000), F32)
            lo = lo_k if lo is None else lo + lo_k
            hi = hi_k if hi is None else hi + hi_k
        los.append(lo)
        his.append(hi)
    ff = jnp.concatenate(los + his, axis=1)
    out_ref[...] = _layer_norm(DN_ALPHA * x1_ref[...] + ff, g2_ref[...], b2_ref[...])


def _combine(dest_flat, ys, tk, x1, g2, b2):
    t = x1.shape[0]
    tile = min(ROW_TILE, t)
    n = t // tile
    return pl.pallas_call(
        _combine_kernel, out_shape=jax.ShapeDtypeStruct((t, D_MODEL), F32), grid=(n,),
        in_specs=[pl.BlockSpec((tile * TOP_K,), lambda i: (i,), memory_space=pltpu.SMEM),
                  pl.BlockSpec((tile * TOP_K,), lambda i: (jnp.minimum(i + 1, n - 1),), memory_space=pltpu.SMEM),
                  pl.BlockSpec(memory_space=pl.ANY),
                  pl.BlockSpec((tile, LANES), lambda i: (i, 0)),
                  pl.BlockSpec((tile, D_MODEL), lambda i: (i, 0)),
                  pl.BlockSpec((1, D_MODEL), lambda i: (0, 0)),
                  pl.BlockSpec((1, D_MODEL), lambda i: (0, 0))],
        out_specs=pl.BlockSpec((tile, D_MODEL), lambda i: (i, 0)),
        scratch_shapes=[pltpu.VMEM((2, TOP_K, tile // 8, YS_TILE, 8, LANES), U32), pltpu.SemaphoreType.DMA((2,))],
        compiler_params=_cparams("arbitrary"), name="moe_combine",
    )(dest_flat, dest_flat, ys, tk, x1, g2, b2)


def _rel_bucket(dist):
    exact = REL_BUCKETS // 2
    d = np.maximum(dist, 0)
    log_b = exact + (np.log(np.maximum(d, 1).astype(np.float32) / np.float32(exact))
                     / np.float32(math.log(REL_MAX_DIST / exact)) * np.float32(REL_BUCKETS - exact)).astype(np.int32)
    return np.where(d < exact, d, np.minimum(log_b, REL_BUCKETS - 1)).astype(np.int32)


def _bias_lookup(table, bucket, valid):
    bucket = jnp.asarray(bucket)[None]
    acc = jnp.zeros((table.shape[1],) + bucket.shape[1:], F32)
    for b in range(REL_BUCKETS):
        acc = jnp.where(bucket == b, table[b].reshape((-1,) + (1,) * (bucket.ndim - 1)), acc)
    return jnp.where(jnp.asarray(valid)[None], acc, NEG)


def _bias_tables(rel_bias):
    table = rel_bias.astype(F32)
    r = np.arange(WINDOW)[:, None]
    c = np.arange(2 * WINDOW)[None, :]
    dist = r + WINDOW - c
    valid = (dist >= 0) & (dist < WINDOW)
    dist0 = np.where(c < N_META, N_META + r - c, dist)
    valid0 = np.where(c < N_META, dist0 < WINDOW, (c >= WINDOW) & valid)
    both = jnp.stack([_bias_lookup(table, _rel_bucket(dist0), valid0), _bias_lookup(table, _rel_bucket(dist), valid)])
    dist_s = WINDOW - 1 - np.arange(WINDOW)
    rows = _bias_lookup(table[:, np.asarray(HEAD_ORDER)], _rel_bucket(dist_s), np.ones_like(dist_s, bool))
    return both, rows


def _perm_heads(a, axis):
    assert HEAD_ORDER == tuple(kv * A_GROUP + g for g in range(A_GROUP) for kv in range(A_KV_HEADS))
    shape = a.shape
    a = a.reshape(shape[:axis] + (A_KV_HEADS, A_GROUP, A_HD) + shape[axis + 1:])
    return jnp.swapaxes(a, axis, axis + 1).reshape(shape)


def _rep_rows(vec, rows=8):
    out = jnp.zeros((rows, LANES), F32)
    return out.at[:vec.shape[0], :].set(jnp.broadcast_to(vec.astype(F32)[:, None], (vec.shape[0], LANES)))


def kernel(x_prompt, x_sample, cache_swa_k, cache_swa_v, state_mlstm_C, state_mlstm_n, state_mlstm_m, meta_tokens, rel_bias, w_in, b_igate, b_fgate, attn_sinks, g_mlstm_out, g_attn_out, w_out, ln1_g, ln1_b, w_router, b_router, w_moe1, b_moe1, w_moe2, b_moe2, ln2_g, ln2_b):
    B, S, _ = x_prompt.shape
    NB = x_sample.shape[0]
    assert x_sample.shape[1] == 1 and w_in.shape[0] == 1
    assert S % PROJ_TILE == 0 and S % M_CHUNK == 0 and S % WINDOW == 0 and NB % SAMPLE_GROUP == 0
    l = 0

    assert IN_WIDTHS == (512, 512, 512, 512, 4, 4, 512, 128, 128)
    bf = lambda a: a.astype(BF16)
    w = w_in[l]
    n_main, n_gate = 4 * M_WIDTH, 2 * M_HEADS
    w_gate = w[:, n_main:n_main + n_gate]
    w_att = w[:, n_main + n_gate:]
    assert math.frexp(A_HD ** -0.5)[0] == 0.5
    w_qa = _perm_heads(w_att[:, :A_WIDTH], 1) * (A_HD ** -0.5)
    wr = bf(jnp.concatenate([w[:, :n_main], w_qa, w_att[:, A_WIDTH:],
                             jnp.pad(w_gate, ((0, 0), (0, LANES - n_gate)))], axis=1))
    wt = bf(jnp.concatenate([w[:, :M_WIDTH], w[:, 2 * M_WIDTH:3 * M_WIDTH], w_qa, w_att[:, A_WIDTH + LANES:],
                             w_gate], axis=1).T)
    b_gate = jnp.concatenate([b_igate[l], b_fgate[l]]).astype(F32)
    brow = jnp.pad(b_gate, (0, LANES - n_gate))[None, :]
    bcol = b_gate[:, None]
    plan_p = ((512, 512, "plain", BF16), (1536, 512, "plain", F32), (2560, 128, "plain", BF16),
              (2816, 128, "gate", F32))
    tplan_p = ((0, 512, "plain", BF16), (512, 512, "plain", BF16), (1024, 512, "plain", BF16),
               (1536, 128, "plain", BF16), (1664, 8, "gate", F32))
    plan_s = ((0, 512, "plain", F32), (512, 512, "plain", F32), (1024, 512, "plain", F32), (1536, 512, "plain", F32),
              (2048, 512, "plain", F32), (2560, 128, "plain", F32), (2688, 128, "plain", F32), (2816, 128, "gate", F32))

    bias_tab, bias_rows = _bias_tables(rel_bias)
    sinks = _rep_rows(attn_sinks[l])
    sinks_step = _rep_rows(attn_sinks[l][np.asarray(HEAD_ORDER)])
    g_m = g_mlstm_out[l].astype(F32)[None, :]
    g_a = _perm_heads(g_attn_out[l].astype(F32), 0)[None, :]
    wo = bf(jnp.concatenate([w_out[l][:M_WIDTH], _perm_heads(w_out[l][M_WIDTH:], 0)], axis=0))
    g1, b1 = ln1_g[l].astype(F32)[None, :], ln1_b[l].astype(F32)[None, :]
    g2, b2 = ln2_g[l].astype(F32)[None, :], ln2_b[l].astype(F32)[None, :]
    w_r = bf(jnp.pad(w_router[l], ((0, 0), (0, LANES - N_EXPERTS))))
    b_r = jnp.pad(b_router[l].astype(F32), (0, LANES - N_EXPERTS), constant_values=NEG)[None, :]
    b1g = b_moe1[l][:, 0::2].astype(F32)[:, None, :]
    b1l = b_moe1[l][:, 1::2].astype(F32)[:, None, :]
    b2e = b_moe2[l].astype(F32)[:, None, :]
    pj = np.zeros((256, 256), np.float32)
    pj[2 * np.arange(128), np.arange(128)] = 1.0
    pj[2 * np.arange(128) + 1, 128 + np.arange(128)] = 1.0
    perm = jnp.asarray(pj, BF16)

    xp2 = x_prompt.reshape(B * S, D_MODEL)
    km, om, ka, gc, qt, vt, qat, vat, gr, kv_tail = _proj(
        xp2, wr, wt, brow, bcol, plan_p, tplan_p, (2560, 256), PROJ_TILE, S, "proj_prompt")
    x_meta = jnp.pad(meta_tokens.astype(F32), ((0, M_CHUNK - N_META), (0, 0)))
    km0, _, ka0, gc0, qt0, vt0, _, vat0, gr0 = _proj(
        x_meta, wr, wt, brow, bcol, plan_p, tplan_p, None, M_CHUNK, M_CHUNK, "proj_meta")
    xs2 = x_sample.reshape(NB, D_MODEL)
    qm_s, km_s, vm_s, om_s, qa_s, ka_s, va_s, gc_s = _proj(
        xs2, wr, wt, brow, bcol, plan_s, (), None, NB, NB, "proj_sample")

    zero_c = jnp.zeros((M_HEADS, M_DV + 8, M_DK), F32)
    zero_m = jnp.zeros((8, LANES), F32)
    _, c_meta, m_meta = _mlstm(qt0, km0, vt0, gc0, gr0, zero_c, zero_m, 1, N_META, "mlstm_meta")
    h_p, c_p, m_p = _mlstm(qt, km, vt, gc, gr, c_meta[0], m_meta[0], B, M_CHUNK, "mlstm_prompt")
    C_p = c_p[:, :, :M_DV, :]
    n_p = c_p[:, :, M_DV, :]
    m_prompt = m_p[:, :M_HEADS, 0]
    m_pad = jnp.pad(state_mlstm_m[l].astype(F32), ((0, 0), (0, LANES - M_HEADS)))
    C_s, n_s, m_s, h_s = _mlstm_step(state_mlstm_C[l].astype(F32), state_mlstm_n[l].astype(F32), m_pad,
                                     gc_s, qm_s, km_s, vm_s)

    att_p = _swa(qat, ka, vat, ka0, vat0, jnp.swapaxes(bias_tab, 2, 3), sinks, B)
    ck = cache_swa_k[l].reshape(NB, WINDOW, LANES)
    cv = cache_swa_v[l].reshape(NB, WINDOW, LANES)
    k_new, v_new, att_s = _swa_step(ck, cv, qa_s, ka_s, va_s, bias_rows, sinks_step)

    x1_p, xpk_p, tk_p, cnt_p = _merge(h_p, om, att_p, xp2, g_m, g_a, wo, g1, b1, w_r, b_r, MERGE_TILE, "merge_prompt")
    x1_s, xpk_s, tk_s, cnt_s = _merge(h_s, om_s, att_s, xs2, g_m, g_a, wo, g1, b1, w_r, b_r, NB, "merge_sample")

    T_p = B * S
    assert T_p % RANK_TILE == 0 and T_p % ROW_TILE == 0
    n_blocks = -(-((T_p + NB) * TOP_K) // EXPERT_TILE) + N_EXPERTS
    off, be2, nu2, pad = _offsets(cnt_p + cnt_s, n_blocks, EXPERT_TILE)
    pads = jnp.concatenate([pad[0, :N_EXPERTS], pad[1, :N_EXPERTS], nu2[0, :1]])
    dest_p = _route(tk_p, off)[:, :TOP_K].reshape(-1)
    dest_s = _route(tk_s, off + cnt_p)[:, :TOP_K].reshape(-1)
    be = be2.reshape(-1)[:n_blocks]
    nu = nu2[0, :1]
    xs = _dispatch(dest_p, pads, dest_s, xpk_p, xpk_s, n_blocks * EXPERT_TILE, EXPERT_TILE)
    ys = _experts(be, nu, xs, w_moe1[l], b1g, b1l, w_moe2[l], b2e, perm, EXPERT_TILE)
    y_p = _combine(dest_p, ys, tk_p, x1_p, g2, b2)
    y_s = _combine(dest_s, ys, tk_s, x1_s, g2, b2)

    kv_tail = kv_tail.reshape(B, WINDOW, 2, A_KV_HEADS, A_HD)
    dt_k, dt_v = cache_swa_k.dtype, cache_swa_v.dtype
    return (y_p.reshape(B, S, D_MODEL).astype(x_prompt.dtype), y_s.reshape(NB, 1, D_MODEL).astype(x_sample.dtype),
            kv_tail[:, :, 0][None].astype(dt_k), kv_tail[:, :, 1][None].astype(dt_v),
            C_p[None].astype(state_mlstm_C.dtype), n_p[None].astype(state_mlstm_n.dtype),
            m_prompt[None].astype(state_mlstm_m.dtype),
            k_new.reshape(1, NB, WINDOW, A_KV_HEADS, A_HD).astype(dt_k),
            v_new.reshape(1, NB, WINDOW, A_KV_HEADS, A_HD).astype(dt_v),
            C_s[None].astype(state_mlstm_C.dtype), n_s[None].astype(state_mlstm_n.dtype),
            m_s[:, :M_HEADS][None].astype(state_mlstm_m.dtype))
```

```python
import functools
import math

import numpy as np
import jax
import jax.numpy as jnp
from jax import lax
from jax.experimental import pallas as pl
from jax.experimental.pallas import tpu as pltpu

F32 = jnp.float32
BF16 = jnp.bfloat16
I32 = jnp.int32
U32 = jnp.uint32

D_MODEL = 1024
N_META = 16
M_HEADS = 4
M_DK = 128
M_DV = 128
M_WIDTH = M_HEADS * M_DV
A_HD = 64
A_HEADS = 8
A_KV_HEADS = 2
A_GROUP = A_HEADS // A_KV_HEADS
A_WIDTH = A_HEADS * A_HD
WINDOW = 128
REL_BUCKETS = 32
REL_MAX_DIST = 128
N_EXPERTS = 32
TOP_K = 4
D_FF = D_MODEL
SWIGLU_LIMIT = 7.0
SWIGLU_ALPHA = 1.702
DEPTH = 1
DN_ALPHA = (2.0 * DEPTH) ** 0.25
LN_EPS = 1e-5
IN_WIDTHS = (M_WIDTH, M_WIDTH, M_WIDTH, M_WIDTH, M_HEADS, M_HEADS, A_WIDTH, A_KV_HEADS * A_HD, A_KV_HEADS * A_HD)

LANES = 128
NEG = -1e30
VMEM_LIMIT = 56 * 1024 * 1024

M_CHUNK = 256
PROJ_TILE = 1024
MERGE_TILE = 1024
RANK_TILE = 512
ROW_TILE = 512
EXPERT_TILE = 512
EXPERT_CHUNK = 256
EXPERT_PIPE_LAG = 1
SAMPLE_GROUP = 8
ISSUE_GROUP = 8
SWA_QBLOCKS = 8
SWA_PIPE_LAG = 2
XP_TILE = D_MODEL // 2 // LANES
YS_TILE = D_MODEL // 2 // LANES
HEAD_ORDER = (0, 4, 1, 5, 2, 6, 3, 7)


def _cparams(*sem):
    return pltpu.CompilerParams(dimension_semantics=sem, vmem_limit_bytes=VMEM_LIMIT)


def _log_sigmoid(x):
    return jnp.minimum(x, 0.0) - jnp.log1p(jnp.exp(-jnp.abs(x)))


def _sigmoid(x):
    return 1.0 / (1.0 + jnp.exp(-x))


def _proj_kernel(x_ref, wr_ref, wt_ref, brow_ref, bcol_ref, *outs, row_plan, t_plan, tail_cols):
    xb = x_ref[...].astype(BF16)
    tm = xb.shape[0]
    o = 0
    for (c0, width, kind, _) in row_plan:
        r = jnp.dot(xb, wr_ref[:, c0:c0 + width], preferred_element_type=F32)
        if kind == "gate":
            r = r + brow_ref[...]
            lane = lax.broadcasted_iota(I32, r.shape, 1)
            r = jnp.where(lane < M_HEADS, r, _log_sigmoid(r))
        outs[o][...] = r.astype(outs[o].dtype)
        o += 1
    for (r0, nrows, kind, _) in t_plan:
        r = lax.dot_general(wt_ref[r0:r0 + nrows, :], xb, (((1,), (1,)), ((), ())), preferred_element_type=F32)
        if kind == "gate":
            r = r + bcol_ref[...]
            row = lax.broadcasted_iota(I32, r.shape, 0)
            r = jnp.where(row < M_HEADS, r, _log_sigmoid(r))
        outs[o][...] = r.astype(outs[o].dtype)
        o += 1
    if tail_cols is not None:
        c0, width = tail_cols
        outs[o][...] = jnp.dot(xb[tm - WINDOW:, :], wr_ref[:, c0:c0 + width], preferred_element_type=F32)


def _proj(x, wr, wt, brow, bcol, row_plan, t_plan, tail_cols, tile, rows_per_group, name):
    t = x.shape[0]
    nt = t // tile
    out_shape, out_specs = [], []
    for (_, width, _, dt) in row_plan:
        out_shape.append(jax.ShapeDtypeStruct((t, width), dt))
        out_specs.append(pl.BlockSpec((tile, width), lambda i: (i, 0)))
    for (_, nrows, _, dt) in t_plan:
        out_shape.append(jax.ShapeDtypeStruct((nrows, t), dt))
        out_specs.append(pl.BlockSpec((nrows, tile), lambda i: (0, i)))
    if tail_cols is not None:
        tiles_per_group = rows_per_group // tile
        out_shape.append(jax.ShapeDtypeStruct((t // rows_per_group * WINDOW, tail_cols[1]), F32))
        out_specs.append(pl.BlockSpec((WINDOW, tail_cols[1]), lambda i: (i // tiles_per_group, 0)))
    kern = functools.partial(_proj_kernel, row_plan=row_plan, t_plan=t_plan, tail_cols=tail_cols)
    return pl.pallas_call(
        kern, out_shape=out_shape, grid=(nt,),
        in_specs=[pl.BlockSpec((tile, D_MODEL), lambda i: (i, 0)),
                  pl.BlockSpec(wr.shape, lambda i: (0, 0)),
                  pl.BlockSpec(wt.shape, lambda i: (0, 0)),
                  pl.BlockSpec(brow.shape, lambda i: (0, 0)),
                  pl.BlockSpec(bcol.shape, lambda i: (0, 0))],
        out_specs=out_specs, compiler_params=_cparams("arbitrary"), name=name,
    )(x, wr, wt, brow, bcol)


def _split3(a):
    hi = a.astype(BF16)
    r1 = a - hi.astype(F32)
    mid = r1.astype(BF16)
    lo = (r1 - mid.astype(F32)).astype(BF16)
    return hi, mid, lo


def _mlstm_kernel(qt_ref, k_ref, vt_ref, gc_ref, gr_ref, c0_ref, m0_ref, h_ref, c_out_ref, m_out_ref,
                  c_scr, m_scr, *, n_valid):
    c = pl.program_id(1)
    nc = pl.num_programs(1)
    L = k_ref.shape[0]

    @pl.when(c == 0)
    def _():
        c_scr[...] = c0_ref[...]
        m_scr[...] = m0_ref[...]

    gc = gc_ref[...]
    gr = gr_ref[...]
    if n_valid < L:
        rowc = lax.broadcasted_iota(I32, gc.shape, 0)
        lanec = lax.broadcasted_iota(I32, gc.shape, 1)
        gc = jnp.where(rowc < n_valid, gc, jnp.where(lanec < M_HEADS, NEG, 0.0))
        rowr = lax.broadcasted_iota(I32, gr.shape, 0)
        colr = lax.broadcasted_iota(I32, gr.shape, 1)
        gr = jnp.where(colr < n_valid, gr, jnp.where(rowr < M_HEADS, NEG, 0.0))
    r_i = lax.broadcasted_iota(I32, (L, L), 0)
    c_i = lax.broadcasted_iota(I32, (L, L), 1)
    upper = r_i <= c_i
    tril = jnp.where(c_i <= r_i, 1.0, 0.0).astype(BF16)
    triu = jnp.where(upper, 1.0, 0.0).astype(BF16)
    b_cols = sum(jnp.dot(tril, part, preferred_element_type=F32) for part in _split3(gc))
    b_rows = sum(jnp.dot(part, triu, preferred_element_type=F32) for part in _split3(gr))
    scale = M_DK ** -0.5
    ones_rows = jnp.where(lax.broadcasted_iota(I32, (8, L), 0) == 0, 1.0, 0.0).astype(BF16)

    m_all = m_scr[...]
    c_all = [c_scr[h] for h in range(M_HEADS)]
    h_new, c_new, m_new_all = {}, {}, {}

    def operands(h):
        sl = slice(h * M_DK, (h + 1) * M_DK)
        return qt_ref[sl, :], k_ref[:, sl], jnp.concatenate([vt_ref[sl, :], ones_rows], axis=0)

    def stage_a(h):
        qt, k, vt_aug = operands(h)
        ig_r = gr[h:h + 1, :]
        b_r = b_rows[M_HEADS + h:M_HEADS + h + 1, :]
        m_prev = m_all[h:h + 1, 0:1]
        cs = c_all[h]
        qk = jnp.dot(k, qt, preferred_element_type=F32)
        inter = jnp.dot(cs.astype(BF16), qt, preferred_element_type=F32)
        b_last = b_r[:, L - 1:L]
        g = ig_r + b_last - b_r
        m_new = jnp.maximum(b_last + m_prev, jnp.max(g, axis=1, keepdims=True))
        a = jnp.exp(b_last + m_prev - m_new)
        wv = (vt_aug.astype(F32) * jnp.exp(g - m_new)).astype(BF16)
        c_new[h] = a * cs + jnp.dot(wv, k, preferred_element_type=F32) * scale
        m_new_all[h] = jnp.broadcast_to(m_new, (1, LANES))
        return qk, inter

    def stage_b(h, qk, inter):
        b_r = b_rows[M_HEADS + h:M_HEADS + h + 1, :]
        m_prev = m_all[h:h + 1, 0:1]
        r_c = gc[:, h:h + 1] - b_cols[:, M_HEADS + h:M_HEADS + h + 1]
        dt = jnp.where(upper, b_r + r_c, NEG)
        m_t = jnp.maximum(b_r + m_prev, jnp.max(dt, axis=0, keepdims=True))
        st = (qk * (scale * jnp.exp(dt - m_t))).astype(BF16)
        return st, jnp.exp(b_r + m_prev - m_t) * inter, jnp.exp(-m_t)

    def stage_c(h, st, inter_w, floor):
        _, _, vt_aug = operands(h)
        nd = inter_w + jnp.dot(vt_aug, st, preferred_element_type=F32)
        den = nd[M_DV:M_DV + 1, :]
        h_new[h] = (nd[:M_DV, :] / jnp.maximum(jnp.abs(den), floor)).T

    a_q, b_q = {}, {}
    for i in range(M_HEADS + 2):
        if i < M_HEADS:
            a_q[i] = stage_a(i)
        if 0 <= i - 1 < M_HEADS:
            b_q[i - 1] = stage_b(i - 1, *a_q.pop(i - 1))
        if 0 <= i - 2 < M_HEADS:
            stage_c(i - 2, *b_q.pop(i - 2))

    h_ref[...] = jnp.concatenate([h_new[h] for h in range(M_HEADS)], axis=1)
    for h in range(M_HEADS):
        c_scr[h] = c_new[h]
    m_scr[0:M_HEADS, :] = jnp.concatenate([m_new_all[h] for h in range(M_HEADS)], axis=0)

    @pl.when(c == nc - 1)
    def _():
        c_out_ref[0] = c_scr[...]
        m_out_ref[0] = m_scr[...]


def _mlstm(qt, km, vt, gc, gr, c0, m0, batch, n_valid, name):
    L = M_CHUNK
    nc = km.shape[0] // (batch * L)
    kern = functools.partial(_mlstm_kernel, n_valid=n_valid)
    rows = pl.BlockSpec((L, M_WIDTH), lambda b, c: (b * nc + c, 0))
    cols = pl.BlockSpec((M_WIDTH, L), lambda b, c: (0, b * nc + c))
    return pl.pallas_call(
        kern,
        out_shape=[jax.ShapeDtypeStruct((batch * nc * L, M_WIDTH), F32),
                   jax.ShapeDtypeStruct((batch, M_HEADS, M_DV + 8, M_DK), F32),
                   jax.ShapeDtypeStruct((batch, 8, LANES), F32)],
        grid=(batch, nc),
        in_specs=[cols, rows, cols,
                  pl.BlockSpec((L, LANES), lambda b, c: (b * nc + c, 0)),
                  pl.BlockSpec((8, L), lambda b, c: (0, b * nc + c)),
                  pl.BlockSpec((M_HEADS, M_DV + 8, M_DK), lambda b, c: (0, 0, 0)),
                  pl.BlockSpec((8, LANES), lambda b, c: (0, 0))],
        out_specs=[rows,
                   pl.BlockSpec((1, M_HEADS, M_DV + 8, M_DK), lambda b, c: (b, 0, 0, 0)),
                   pl.BlockSpec((1, 8, LANES), lambda b, c: (b, 0, 0))],
        scratch_shapes=[pltpu.VMEM((M_HEADS, M_DV + 8, M_DK), F32), pltpu.VMEM((8, LANES), F32)],
        compiler_params=_cparams("arbitrary", "arbitrary"), name=name,
    )(qt, km, vt, gc, gr, c0, m0)


def _outer_f32(a, b):
    ah, am, al = (t.astype(F32) for t in _split3(a))
    bh, bm, bl = (t.astype(F32) for t in _split3(b))
    z = jnp.zeros_like(ah)
    lhs = jnp.concatenate([ah, ah, ah, am, am, al, z, z], axis=0).astype(BF16)
    rhs = jnp.concatenate([bh, bm, bl, bh, bm, bh, z, z], axis=0).astype(BF16)
    return lax.dot_general(lhs, rhs, (((0,), (0,)), ((), ())), preferred_element_type=F32)


def _mlstm_step_kernel(c_ref, n_ref, m_ref, gc_ref, q_ref, k_ref, v_ref,
                       c_out_ref, n_out_ref, m_out_ref, h_ref):
    g = c_ref.shape[0]
    scale = M_DK ** -0.5
    lane_m = lax.broadcasted_iota(I32, (1, LANES), 1)
    for j in range(g):
        m_row = jnp.zeros((1, LANES), F32)
        for h in range(M_HEADS):
            sl = slice(h * M_DK, (h + 1) * M_DK)
            q = q_ref[j:j + 1, sl]
            k = k_ref[j:j + 1, sl] * scale
            v = v_ref[j:j + 1, sl]
            ig = gc_ref[j:j + 1, h:h + 1]
            lf = gc_ref[j:j + 1, M_HEADS + h:M_HEADS + h + 1]
            m = m_ref[j:j + 1, h:h + 1]
            c = c_ref[j, h]
            n = n_ref[j, h:h + 1, :]
            m_t = jnp.maximum(lf + m, ig)
            w = jnp.exp(lf + m - m_t)
            wg = jnp.exp(ig - m_t)
            s = jnp.sum(q * k, axis=1, keepdims=True) * wg
            q8 = jnp.broadcast_to(q, (8, M_DK)).astype(BF16)
            cq = lax.dot_general(q8, c.astype(BF16), (((1,), (1,)), ((), ())), preferred_element_type=F32)[0:1, :]
            den = w * jnp.sum(n * q, axis=1, keepdims=True) + s
            h_ref[j:j + 1, sl] = (w * cq + s * v) / jnp.maximum(jnp.abs(den), jnp.exp(-m_t))
            c_out_ref[j, h] = w * c + _outer_f32(wg * v, k)
            n_out_ref[j, h:h + 1, :] = w * n + wg * k
            m_row = jnp.where(lane_m == h, m_t, m_row)
        m_out_ref[j:j + 1, :] = m_row


def _mlstm_step(c, n, m_pad, gc, q, k, v):
    nb = c.shape[0]
    g = SAMPLE_GROUP
    row = lambda w: pl.BlockSpec((g, w), lambda i: (i, 0))
    return pl.pallas_call(
        _mlstm_step_kernel,
        out_shape=[jax.ShapeDtypeStruct(c.shape, F32), jax.ShapeDtypeStruct(n.shape, F32),
                   jax.ShapeDtypeStruct((nb, LANES), F32), jax.ShapeDtypeStruct((nb, M_WIDTH), F32)],
        grid=(nb // g,),
        in_specs=[pl.BlockSpec((g, M_HEADS, M_DV, M_DK), lambda i: (i, 0, 0, 0)),
                  pl.BlockSpec((g, M_HEADS, M_DK), lambda i: (i, 0, 0)),
                  row(LANES), row(LANES), row(M_WIDTH), row(M_WIDTH), row(M_WIDTH)],
        out_specs=[pl.BlockSpec((g, M_HEADS, M_DV, M_DK), lambda i: (i, 0, 0, 0)),
                   pl.BlockSpec((g, M_HEADS, M_DK), lambda i: (i, 0, 0)),
                   row(LANES), row(M_WIDTH)],
        compiler_params=_cparams("arbitrary"), name="mlstm_step",
    )(c, n, m_pad, gc, q, k, v)


def _swa_kernel(qt_ref, kc_ref, kp_ref, vtc_ref, vtp_ref, km_ref, vtm_ref, bias_ref, sink_ref, o_ref):
    j = pl.program_id(1)
    first = j == 0
    blk = WINDOW
    nqb = qt_ref.shape[1] // blk
    kp = jnp.where(first, km_ref[...], kp_ref[...])
    vtp = jnp.where(first, vtm_ref[...], vtp_ref[...])
    k = jnp.concatenate([kp, kc_ref[...]], axis=0)
    vt = jnp.concatenate([vtp, vtc_ref[...]], axis=1)
    row_v = lax.broadcasted_iota(I32, vt.shape, 0)
    zero_v = jnp.zeros_like(vt)
    vt_half = (jnp.where(row_v < A_HD, vt, zero_v), jnp.where(row_v >= A_HD, vt, zero_v))
    row_q = lax.broadcasted_iota(I32, (LANES, blk), 0)
    lo_rows = row_q < A_HD
    def scores(u, p):
        cols = slice(u * blk, (u + 1) * blk)
        keys = slice(u * blk, (u + 2) * blk)
        qs = qt_ref[p * LANES:(p + 1) * LANES, cols]
        zero_q = jnp.zeros_like(qs)
        q_own = (jnp.where(lo_rows, qs, zero_q), jnp.where(lo_rows, zero_q, qs))
        return [jnp.dot(k[keys], q_own[half], preferred_element_type=F32) for half in range(2)]

    def softmax(u, p, s2):
        table = jnp.where(first, 0, 1) if u == 0 else 1
        probs, inv = [], []
        for half in range(2):
            hd = HEAD_ORDER[2 * p + half]
            s = s2[half] + bias_ref[table, hd]
            sk = sink_ref[hd:hd + 1, 0:1]
            m = jnp.maximum(jnp.max(s, axis=0, keepdims=True), sk)
            e = jnp.exp(s - m)
            probs.append(e.astype(BF16))
            inv.append(1.0 / (jnp.sum(e, axis=0, keepdims=True) + jnp.exp(sk - m)))
        return jnp.concatenate(probs, axis=0), jnp.where(lo_rows, inv[0], inv[1])

    def values(u, p, probs, inv):
        keys = slice(u * blk, (u + 2) * blk)
        vt_stack = jnp.concatenate([vt_half[0][:, keys], vt_half[1][:, keys]], axis=1)
        ot = jnp.dot(vt_stack, probs, preferred_element_type=F32)
        o_ref[u * blk:(u + 1) * blk, p * LANES:(p + 1) * LANES] = (ot * inv).T

    units = [(u, p) for u in range(nqb) for p in range(A_GROUP)]
    s_q, p_q = {}, {}
    lag = SWA_PIPE_LAG
    for i in range(len(units) + 2 * lag):
        if i < len(units):
            s_q[i] = scores(*units[i])
        if 0 <= i - lag < len(units):
            p_q[i - lag] = softmax(*units[i - lag], s_q.pop(i - lag))
        if 0 <= i - 2 * lag < len(units):
            values(*units[i - 2 * lag], *p_q.pop(i - 2 * lag))


def _swa(qat, ka, vat, kmeta, vtmeta, bias_t, sinks, batch):
    blk = WINDOW
    nqb = SWA_QBLOCKS
    t = ka.shape[0]
    nq = t // (batch * blk * nqb)
    prev = lambda b, j: (b * nq + j) * nqb + jnp.where(j == 0, 0, -1)
    const2 = lambda shape: pl.BlockSpec(shape, lambda b, j: (0, 0))
    return pl.pallas_call(
        _swa_kernel, out_shape=jax.ShapeDtypeStruct((t, A_WIDTH), F32), grid=(batch, nq),
        in_specs=[pl.BlockSpec((A_WIDTH, nqb * blk), lambda b, j: (0, b * nq + j)),
                  pl.BlockSpec((nqb * blk, LANES), lambda b, j: (b * nq + j, 0)),
                  pl.BlockSpec((blk, LANES), lambda b, j: (prev(b, j), 0)),
                  pl.BlockSpec((LANES, nqb * blk), lambda b, j: (0, b * nq + j)),
                  pl.BlockSpec((LANES, blk), lambda b, j: (0, prev(b, j))),
                  const2((blk, LANES)), const2((LANES, blk)),
                  pl.BlockSpec(bias_t.shape, lambda b, j: (0, 0, 0, 0)),
                  const2((8, LANES))],
        out_specs=pl.BlockSpec((nqb * blk, A_WIDTH), lambda b, j: (b * nq + j, 0)),
        compiler_params=_cparams("arbitrary", "arbitrary"), name="swa_prompt",
    )(qat, ka, ka, vat, vat, kmeta, vtmeta, bias_t, sinks)


def _swa_step_kernel(ck_ref, cv_ref, q_ref, k_ref, v_ref, bias_ref, sink_ref, ko_ref, vo_ref, o_ref):
    g = ck_ref.shape[0]
    lane = lax.broadcasted_iota(I32, (A_HEADS, LANES), 1)
    row = lax.broadcasted_iota(I32, (A_HEADS, LANES), 0)
    own_half = (row % 2 == 0) == (lane < A_HD)
    bias = bias_ref[...]
    sk = sink_ref[:, 0:1]
    for j in range(g):
        ko_ref[j, 0:WINDOW - 1, :] = ck_ref[j, 1:WINDOW, :]
        ko_ref[j, WINDOW - 1:WINDOW, :] = k_ref[j:j + 1, :]
        vo_ref[j, 0:WINDOW - 1, :] = cv_ref[j, 1:WINDOW, :]
        vo_ref[j, WINDOW - 1:WINDOW, :] = v_ref[j:j + 1, :]
        kk = ko_ref[j].astype(BF16)
        vv = vo_ref[j].astype(BF16)
        slabs = [q_ref[j:j + 1, p * LANES:(p + 1) * LANES] for p in range(A_GROUP)]
        q8 = jnp.concatenate([slabs[r // 2] for r in range(A_HEADS)], axis=0)
        q8 = jnp.where(own_half, q8, 0.0).astype(BF16)
        s = lax.dot_general(q8, kk, (((1,), (1,)), ((), ())), preferred_element_type=F32)
        s = s + bias
        m = jnp.maximum(jnp.max(s, axis=1, keepdims=True), sk)
        e = jnp.exp(s - m)
        inv = 1.0 / (jnp.sum(e, axis=1, keepdims=True) + jnp.exp(sk - m))
        o8 = jnp.where(own_half, jnp.dot(e.astype(BF16), vv, preferred_element_type=F32) * inv, 0.0)
        for p in range(A_GROUP):
            o_ref[j:j + 1, p * LANES:(p + 1) * LANES] = o8[2 * p:2 * p + 1, :] + o8[2 * p + 1:2 * p + 2, :]


def _swa_step(ck, cv, q, k, v, bias_rows, sinks):
    nb = ck.shape[0]
    g = SAMPLE_GROUP
    cache = pl.BlockSpec((g, WINDOW, LANES), lambda i: (i, 0, 0))
    row = lambda w: pl.BlockSpec((g, w), lambda i: (i, 0))
    const = lambda a: pl.BlockSpec(a.shape, lambda i: (0, 0))
    return pl.pallas_call(
        _swa_step_kernel,
        out_shape=[jax.ShapeDtypeStruct(ck.shape, F32), jax.ShapeDtypeStruct(cv.shape, F32),
                   jax.ShapeDtypeStruct((nb, A_WIDTH), F32)],
        grid=(nb // g,),
        in_specs=[cache, cache, row(A_WIDTH), row(LANES), row(LANES), const(bias_rows), const(sinks)],
        out_specs=[cache, cache, row(A_WIDTH)],
        compiler_params=_cparams("arbitrary"), name="swa_step",
    )(ck, cv, q, k, v, bias_rows, sinks)


def _layer_norm(z, g, b):
    mu = jnp.mean(z, axis=1, keepdims=True)
    zc = z - mu
    var = jnp.mean(zc * zc, axis=1, keepdims=True)
    return zc * lax.rsqrt(var + LN_EPS) * g + b


def _pack_halves(x):
    w = x.shape[1] // 2
    lo = pltpu.bitcast(x[:, :w].astype(BF16).astype(F32), U32)
    hi = pltpu.bitcast(x[:, w:].astype(BF16).astype(F32), U32)
    return (lo >> 16) | (hi & jnp.uint32(0xFFFF0000))


def _unpack_halves(words):
    lo = pltpu.bitcast(words << 16, F32).astype(BF16)
    hi = pltpu.bitcast(words & jnp.uint32(0xFFFF0000), F32).astype(BF16)
    return lo, hi


def _to_token_tiles(ref, x):
    for q in range(x.shape[1] // LANES):
        ref[:, q, :] = x[:, q * LANES:(q + 1) * LANES]


def _merge_kernel(h_ref, om_ref, att_ref, x_ref, gm_ref, ga_ref, wo_ref, g1_ref, b1_ref, wr_ref, br_ref,
                  x1_ref, xp_ref, tk_ref, cnt_ref):
    @pl.when(pl.program_id(0) == 0)
    def _():
        cnt_ref[...] = jnp.zeros_like(cnt_ref)

    hm = h_ref[...] * _sigmoid(om_ref[...])
    ym = hm * lax.rsqrt(jnp.mean(hm * hm, axis=1, keepdims=True) + LN_EPS) * gm_ref[...]
    att = att_ref[...]
    ya = att * lax.rsqrt(jnp.mean(att * att, axis=1, keepdims=True) + LN_EPS) * ga_ref[...]
    mix = (jnp.dot(ym.astype(BF16), wo_ref[0:M_WIDTH, :], preferred_element_type=F32)
           + jnp.dot(ya.astype(BF16), wo_ref[M_WIDTH:, :], preferred_element_type=F32))
    x1 = _layer_norm(DN_ALPHA * x_ref[...] + mix, g1_ref[...], b1_ref[...])
    x1_ref[...] = x1
    _to_token_tiles(xp_ref, _pack_halves(x1))
    logits = jnp.dot(x1.astype(BF16), wr_ref[...], preferred_element_type=F32) + br_ref[...]
    lane = lax.broadcasted_iota(I32, logits.shape, 1).astype(F32)
    vals, idxs = [], []
    for _ in range(TOP_K):
        mx = jnp.max(logits, axis=1, keepdims=True)
        idx = jnp.min(jnp.where(logits == mx, lane, float(LANES)), axis=1, keepdims=True)
        vals.append(mx)
        idxs.append(idx)
        logits = jnp.where(lane == idx, 2.0 * NEG, logits)
    es = [jnp.exp(vk - vals[0]) for vk in vals]
    tot = es[0] + es[1] + es[2] + es[3]
    tk = jnp.zeros(logits.shape, F32)
    picked = jnp.zeros(logits.shape, F32)
    for k in range(TOP_K):
        tk = jnp.where(lane == float(k), es[k] / tot, tk)
        tk = jnp.where(lane == float(TOP_K + k), idxs[k], tk)
        picked = jnp.where(lane == idxs[k], 1.0, picked)
    tk_ref[...] = tk
    cnt_ref[...] = cnt_ref[...] + jnp.sum(picked, axis=0, keepdims=True)


def _merge(h, om, att, x, gm, ga, wo, g1, b1, wr, br, tile, name):
    t = x.shape[0]
    rows = lambda w: pl.BlockSpec((tile, w), lambda i: (i, 0))
    const = lambda a: pl.BlockSpec(a.shape, lambda i: (0, 0))
    return pl.pallas_call(
        _merge_kernel,
        out_shape=[jax.ShapeDtypeStruct((t, D_MODEL), F32), jax.ShapeDtypeStruct((t, XP_TILE, LANES), U32),
                   jax.ShapeDtypeStruct((t, LANES), F32), jax.ShapeDtypeStruct((8, LANES), F32)],
        grid=(t // tile,),
        in_specs=[rows(M_WIDTH), rows(M_WIDTH), rows(A_WIDTH), rows(D_MODEL), const(gm), const(ga), const(wo),
                  const(g1), const(b1), const(wr), const(br)],
        out_specs=[rows(D_MODEL), pl.BlockSpec((tile, XP_TILE, LANES), lambda i: (i, 0, 0)), rows(LANES),
                   pl.BlockSpec((8, LANES), lambda i: (0, 0))],
        compiler_params=_cparams("arbitrary"), name=name,
    )(h, om, att, x, gm, ga, wo, g1, b1, wr, br)


def _route_kernel(tk_ref, first_ref, strict_ref, dest_ref, next_scr):
    @pl.when(pl.program_id(0) == 0)
    def _():
        next_scr[...] = first_ref[...]

    tk = tk_ref[...]
    lane = lax.broadcasted_iota(I32, tk.shape, 1).astype(F32)
    onehots = [jnp.where(lane == tk[:, TOP_K + k:TOP_K + k + 1], 1.0, 0.0) for k in range(TOP_K)]
    tot = onehots[0] + onehots[1] + onehots[2] + onehots[3]
    row = jnp.dot(strict_ref[...], tot.astype(BF16), preferred_element_type=F32) + next_scr[0:1, :]
    out = jnp.zeros(tk.shape, F32)
    for k in range(TOP_K):
        out = jnp.where(lane == float(k), jnp.sum(onehots[k] * row, axis=1, keepdims=True), out)
    dest_ref[...] = out.astype(I32)
    next_scr[...] = next_scr[...] + jnp.sum(tot, axis=0, keepdims=True)


def _route(tk, first):
    t = tk.shape[0]
    tile = min(RANK_TILE, t)
    strict = jnp.asarray(np.tril(np.ones((tile, tile), np.float32), -1), BF16)
    return pl.pallas_call(
        _route_kernel, out_shape=jax.ShapeDtypeStruct((t, LANES), I32), grid=(t // tile,),
        in_specs=[pl.BlockSpec((tile, LANES), lambda i: (i, 0)), pl.BlockSpec((8, LANES), lambda i: (0, 0)),
                  pl.BlockSpec((tile, tile), lambda i: (0, 0))],
        out_specs=pl.BlockSpec((tile, LANES), lambda i: (i, 0)),
        scratch_shapes=[pltpu.VMEM((8, LANES), F32)],
        compiler_params=_cparams("arbitrary"), name="moe_route",
    )(tk, first, strict)


def _offsets_kernel(cnt_ref, off_ref, be_ref, nu_ref, pad_ref, *, tile):
    cnt = cnt_ref[...]
    nblk = jnp.floor((cnt + float(tile - 1)) * (1.0 / tile))
    r_i = lax.broadcasted_iota(I32, (LANES, LANES), 0)
    c_i = lax.broadcasted_iota(I32, (LANES, LANES), 1)
    incl = jnp.where(r_i <= c_i, 1.0, 0.0).astype(BF16)
    cum = jnp.dot(nblk.astype(BF16), incl, preferred_element_type=F32)
    off = (cum - nblk) * float(tile)
    off_ref[...] = off
    which = lax.broadcasted_iota(I32, cnt.shape, 0)
    pad_ref[...] = jnp.where(which == 0, off + cnt, jnp.where(which == 1, nblk * float(tile) - cnt, 0.0)).astype(I32)
    rows = be_ref.shape[0]
    jb = (lax.broadcasted_iota(I32, (rows, LANES), 0) * LANES + lax.broadcasted_iota(I32, (rows, LANES), 1)).astype(F32)
    acc = jnp.zeros((rows, LANES), F32)
    for e in range(N_EXPERTS):
        acc = acc + jnp.where(jb >= cum[0:1, e:e + 1], 1.0, 0.0)
    be_ref[...] = jnp.minimum(acc, float(N_EXPERTS - 1)).astype(I32)
    nu_ref[...] = jnp.broadcast_to(cum[0:1, N_EXPERTS - 1:N_EXPERTS], nu_ref.shape).astype(I32)


def _offsets(cnt, n_blocks, tile):
    rows = -(-n_blocks // LANES)
    rows = -(-rows // 8) * 8
    return pl.pallas_call(
        functools.partial(_offsets_kernel, tile=tile),
        out_shape=[jax.ShapeDtypeStruct((8, LANES), F32), jax.ShapeDtypeStruct((rows, LANES), I32),
                   jax.ShapeDtypeStruct((8, LANES), I32), jax.ShapeDtypeStruct((8, LANES), I32)],
        name="moe_offsets",
    )(cnt)


def _scatter_rows(dest_ref, xp_ref, xs_ref, sem):
    t = xp_ref.shape[0]

    def row_copy(tok, dst):
        return pltpu.make_async_copy(xp_ref.at[pl.ds(tok, 1)], xs_ref.at[pl.ds(dst, 1)], sem)

    def issue(grp, carry):
        base = pl.multiple_of(grp * ISSUE_GROUP, ISSUE_GROUP)
        for u in range(ISSUE_GROUP):
            for k in range(TOP_K):
                row_copy(base + u, dest_ref[(base + u) * TOP_K + k]).start(priority=k % 2)
        return carry

    lax.fori_loop(0, t // ISSUE_GROUP, issue, 0)
    for k in range(TOP_K):
        pltpu.make_async_copy(xp_ref, xs_ref.at[pl.ds(0, t)], sem).wait()


def _dispatch_kernel(dest_ref, pads_ref, dest2_ref, xp_ref, xp2_ref, xs_ref, sem, zsem, zbuf, *, block_rows):
    zr = zbuf.shape[0]
    n_blocks = xs_ref.shape[0] // block_rows

    @pl.when(pl.program_id(0) == 0)
    def _():
        zbuf[...] = jnp.zeros_like(zbuf)
        used = pads_ref[2 * N_EXPERTS]

        def pieces(e, act):
            start, n = pads_ref[e], pads_ref[N_EXPERTS + e]
            for sh in range(zr.bit_length() - 1, -1, -1):
                b = 1 << sh
                before = lax.shift_left(lax.shift_right_logical(n, sh + 1), sh + 1)

                @pl.when((n & b) != 0)
                def _():
                    act(pltpu.make_async_copy(zbuf.at[pl.ds(0, b)], xs_ref.at[pl.ds(start + before, b)], zsem))

        def tail(jb, act):
            for h in range(block_rows // zr):
                act(pltpu.make_async_copy(zbuf, xs_ref.at[pl.ds(jb * block_rows + h * zr, zr)], zsem))

        for act in (lambda cp: cp.start(), lambda cp: cp.wait()):
            lax.fori_loop(0, N_EXPERTS, lambda e, c: (pieces(e, act), c)[1], 0)
            lax.fori_loop(used, n_blocks, lambda jb, c: (tail(jb, act), c)[1], 0)

    _scatter_rows(dest_ref, xp_ref, xs_ref, sem)

    @pl.when(pl.program_id(0) == pl.num_programs(0) - 1)
    def _():
        _scatter_rows(dest2_ref, xp2_ref, xs_ref, sem)


def _dispatch(dest, pads, dest2, xp, xp2, n_rows, block_rows):
    t = xp.shape[0]
    tile = min(ROW_TILE, t)
    return pl.pallas_call(
        functools.partial(_dispatch_kernel, block_rows=block_rows),
        out_shape=jax.ShapeDtypeStruct((n_rows,) + xp.shape[1:], xp.dtype), grid=(t // tile,),
        in_specs=[pl.BlockSpec((tile * TOP_K,), lambda i: (i,), memory_space=pltpu.SMEM),
                  pl.BlockSpec(memory_space=pltpu.SMEM),
                  pl.BlockSpec(memory_space=pltpu.SMEM),
                  pl.BlockSpec((tile,) + xp.shape[1:], lambda i: (i, 0, 0)),
                  pl.BlockSpec(xp2.shape, lambda i: (0, 0, 0))],
        out_specs=pl.BlockSpec(memory_space=pl.ANY),
        scratch_shapes=[pltpu.SemaphoreType.DMA(()), pltpu.SemaphoreType.DMA(()),
                        pltpu.VMEM((EXPERT_TILE // 2,) + xp.shape[1:], xp.dtype)],
        compiler_params=_cparams("arbitrary"), name="moe_dispatch",
    )(dest, pads, dest2, xp, xp2)


def _expert_kernel(be_ref, nu_ref, xs_ref, w1_ref, b1g_ref, b1l_ref, w2_ref, b2_ref, perm_ref, ys_ref,
                   w1g_scr, w1l_scr, w2_scr, xq_scr, y_scr, sem, osem):
    j = pl.program_id(0)
    active = j < nu_ref[0]
    changed = jnp.logical_or(j == 0, be_ref[j] != be_ref[jnp.maximum(j - 1, 0)])
    tm = y_scr.shape[0]
    slot = lax.rem(j, 2)

    def fetch(blk, slot):
        row0 = pl.multiple_of(blk * tm, tm)
        return [pltpu.make_async_copy(xs_ref.at[pl.ds(row0, tm), q, :], xq_scr.at[slot, q], sem.at[slot])
                for q in range(XP_TILE)]

    def put(blk):
        row0 = pl.multiple_of(blk * tm, tm)
        return [pltpu.make_async_copy(y_scr.at[:, q * LANES:(q + 1) * LANES], ys_ref.at[pl.ds(row0, tm), q, :], osem)
                for q in range(YS_TILE)]

    def emit(y):
        @pl.when(j > 0)
        def _():
            for cp in put(j - 1):
                cp.wait()

        y_scr[...] = _pack_halves(y)
        for cp in put(j):
            cp.start()

    @pl.when(j == 0)
    def _():
        for cp in fetch(0, 0):
            cp.start()

    @pl.when(j + 1 < nu_ref[0])
    def _():
        for cp in fetch(j + 1, 1 - slot):
            cp.start()

    @pl.when(jnp.logical_and(active, changed))
    def _():
        for c in range(2 * D_FF // 256):
            wc = w1_ref[0, :, c * 256:(c + 1) * 256].astype(BF16)
            d = jnp.dot(wc, perm_ref[...], preferred_element_type=F32).astype(BF16)
            w1g_scr[:, c * 128:(c + 1) * 128] = d[:, :128]
            w1l_scr[:, c * 128:(c + 1) * 128] = d[:, 128:]
        for c in range(D_FF // 256):
            w2_scr[c * 256:(c + 1) * 256, :] = w2_ref[0, c * 256:(c + 1) * 256, :].astype(BF16)

    @pl.when(active)
    def _():
        for cp in fetch(j, slot):
            cp.wait()
        lo, hi = _unpack_halves(jnp.concatenate([xq_scr[slot, q] for q in range(XP_TILE)], axis=1))
        xb = jnp.concatenate([lo, hi], axis=1)

        def first(c):
            cs = slice(c * EXPERT_CHUNK, (c + 1) * EXPERT_CHUNK)
            return (jnp.dot(xb, w1g_scr[:, cs], preferred_element_type=F32) + b1g_ref[0, :, cs],
                    jnp.dot(xb, w1l_scr[:, cs], preferred_element_type=F32) + b1l_ref[0, :, cs])

        def act(hg, hl):
            x_glu = jnp.minimum(hg, SWIGLU_LIMIT)
            x_lin = jnp.clip(hl, -SWIGLU_LIMIT, SWIGLU_LIMIT)
            return (x_glu * _sigmoid(SWIGLU_ALPHA * x_glu) * (x_lin + 1.0)).astype(BF16)

        def second(c, a):
            return jnp.dot(a, w2_scr[c * EXPERT_CHUNK:(c + 1) * EXPERT_CHUNK, :], preferred_element_type=F32)

        n_chunks, lag = D_FF // EXPERT_CHUNK, EXPERT_PIPE_LAG
        h_q, a_q = {}, {}
        y = b2_ref[0]
        for i in range(n_chunks + 2 * lag):
            if i < n_chunks:
                h_q[i] = first(i)
            if 0 <= i - lag < n_chunks:
                a_q[i - lag] = act(*h_q.pop(i - lag))
            if 0 <= i - 2 * lag < n_chunks:
                y = y + second(i - 2 * lag, a_q.pop(i - 2 * lag))
        emit(y)

    @pl.when(jnp.logical_not(active))
    def _():
        emit(jnp.zeros((tm, D_MODEL), F32))

    @pl.when(j == pl.num_programs(0) - 1)
    def _():
        for cp in put(j):
            cp.wait()


def _experts(be, nu, xs, w1, b1g, b1l, w2, b2, perm, tile):
    n_blocks = xs.shape[0] // tile
    grid_spec = pltpu.PrefetchScalarGridSpec(
        num_scalar_prefetch=2, grid=(n_blocks,),
        in_specs=[pl.BlockSpec(memory_space=pl.ANY),
                  pl.BlockSpec((1, D_MODEL, 2 * D_FF), lambda j, be, nu: (be[j], 0, 0)),
                  pl.BlockSpec((1, 1, D_FF), lambda j, be, nu: (be[j], 0, 0)),
                  pl.BlockSpec((1, 1, D_FF), lambda j, be, nu: (be[j], 0, 0)),
                  pl.BlockSpec((1, D_FF, D_MODEL), lambda j, be, nu: (be[j], 0, 0)),
                  pl.BlockSpec((1, 1, D_MODEL), lambda j, be, nu: (be[j], 0, 0)),
                  pl.BlockSpec((256, 256), lambda j, be, nu: (0, 0))],
        out_specs=pl.BlockSpec(memory_space=pl.ANY),
        scratch_shapes=[pltpu.VMEM((D_MODEL, D_FF), BF16), pltpu.VMEM((D_MODEL, D_FF), BF16),
                        pltpu.VMEM((D_FF, D_MODEL), BF16), pltpu.VMEM((2, XP_TILE, tile, LANES), U32),
                        pltpu.VMEM((tile, YS_TILE * LANES), U32), pltpu.SemaphoreType.DMA((2,)),
                        pltpu.SemaphoreType.DMA(())])
    return pl.pallas_call(
        _expert_kernel, out_shape=jax.ShapeDtypeStruct((xs.shape[0], YS_TILE, LANES), U32), grid_spec=grid_spec,
        compiler_params=_cparams("arbitrary"), name="moe_experts",
    )(be, nu, xs, w1, b1g, b1l, w2, b2, perm)


def _combine_kernel(dest_ref, next_ref, ys_ref, tk_ref, x1_ref, g2_ref, b2_ref, out_ref, buf, sem):
    i = pl.program_id(0)
    t = x1_ref.shape[0]
    slot = lax.rem(i, 2)

    def gather(idx_ref, s):
        def issue(grp, carry):
            base = pl.multiple_of(grp * 8, 8)
            for u in range(8):
                for k in range(TOP_K):
                    pltpu.make_async_copy(ys_ref.at[idx_ref[(base + u) * TOP_K + k]],
                                          buf.at[s, k, grp, :, u, :], sem.at[s]).start(priority=k % 2)
            return carry

        lax.fori_loop(0, t // 8, issue, 0)

    @pl.when(i == 0)
    def _():
        gather(dest_ref, 0)

    @pl.when(i + 1 < pl.num_programs(0))
    def _():
        gather(next_ref, 1 - slot)

    for k in range(TOP_K):
        for u in range(8):
            pltpu.make_async_copy(ys_ref.at[pl.ds(0, t // 8)], buf.at[slot, k, :, :, u, :], sem.at[slot]).wait()
    tk = tk_ref[...]
    los, his = [], []
    for q in range(YS_TILE):
        lo = hi = None
        for k in range(TOP_K):
            words = buf[slot, k, :, q].reshape(t, LANES)
            g = tk[:, k:k + 1]
            lo_k = g * pltpu.bitcast(words << 16, F32)
            hi_k = g * pltpu.bitcast(words & jnp.uint32(0xFFFF0000), F32)
            lo = lo_k if lo is None else lo + lo_k
            hi = hi_k if hi is None else hi + hi_k
        los.append(lo)
        his.append(hi)
    ff = jnp.concatenate(los + his, axis=1)
    out_ref[...] = _layer_norm(DN_ALPHA * x1_ref[...] + ff, g2_ref[...], b2_ref[...])


def _combine(dest_flat, ys, tk, x1, g2, b2):
    t = x1.shape[0]
    tile = min(ROW_TILE, t)
    n = t // tile
    return pl.pallas_call(
        _combine_kernel, out_shape=jax.ShapeDtypeStruct((t, D_MODEL), F32), grid=(n,),
        in_specs=[pl.BlockSpec((tile * TOP_K,), lambda i: (i,), memory_space=pltpu.SMEM),
                  pl.BlockSpec((tile * TOP_K,), lambda i: (jnp.minimum(i + 1, n - 1),), memory_space=pltpu.SMEM),
                  pl.BlockSpec(memory_space=pl.ANY),
                  pl.BlockSpec((tile, LANES), lambda i: (i, 0)),
                  pl.BlockSpec((tile, D_MODEL), lambda i: (i, 0)),
                  pl.BlockSpec((1, D_MODEL), lambda i: (0, 0)),
                  pl.BlockSpec((1, D_MODEL), lambda i: (0, 0))],
        out_specs=pl.BlockSpec((tile, D_MODEL), lambda i: (i, 0)),
        scratch_shapes=[pltpu.VMEM((2, TOP_K, tile // 8, YS_TILE, 8, LANES), U32), pltpu.SemaphoreType.DMA((2,))],
        compiler_params=_cparams("arbitrary"), name="moe_combine",
    )(dest_flat, dest_flat, ys, tk, x1, g2, b2)


def _rel_bucket(dist):
    exact = REL_BUCKETS // 2
    d = np.maximum(dist, 0)
    log_b = exact + (np.log(np.maximum(d, 1).astype(np.float32) / np.float32(exact))
                     / np.float32(math.log(REL_MAX_DIST / exact)) * np.float32(REL_BUCKETS - exact)).astype(np.int32)
    return np.where(d < exact, d, np.minimum(log_b, REL_BUCKETS - 1)).astype(np.int32)


def _bias_lookup(table, bucket, valid):
    bucket = jnp.asarray(bucket)[None]
    acc = jnp.zeros((table.shape[1],) + bucket.shape[1:], F32)
    for b in range(REL_BUCKETS):
        acc = jnp.where(bucket == b, table[b].reshape((-1,) + (1,) * (bucket.ndim - 1)), acc)
    return jnp.where(jnp.asarray(valid)[None], acc, NEG)


def _bias_tables(rel_bias):
    table = rel_bias.astype(F32)
    r = np.arange(WINDOW)[:, None]
    c = np.arange(2 * WINDOW)[None, :]
    dist = r + WINDOW - c
    valid = (dist >= 0) & (dist < WINDOW)
    dist0 = np.where(c < N_META, N_META + r - c, dist)
    valid0 = np.where(c < N_META, dist0 < WINDOW, (c >= WINDOW) & valid)
    both = jnp.stack([_bias_lookup(table, _rel_bucket(dist0), valid0), _bias_lookup(table, _rel_bucket(dist), valid)])
    dist_s = WINDOW - 1 - np.arange(WINDOW)
    rows = _bias_lookup(table[:, np.asarray(HEAD_ORDER)], _rel_bucket(dist_s), np.ones_like(dist_s, bool))
    return both, rows


def _perm_heads(a, axis):
    assert HEAD_ORDER == tuple(kv * A_GROUP + g for g in range(A_GROUP) for kv in range(A_KV_HEADS))
    shape = a.shape
    a = a.reshape(shape[:axis] + (A_KV_HEADS, A_GROUP, A_HD) + shape[axis + 1:])
    return jnp.swapaxes(a, axis, axis + 1).reshape(shape)


def _rep_rows(vec, rows=8):
    out = jnp.zeros((rows, LANES), F32)
    return out.at[:vec.shape[0], :].set(jnp.broadcast_to(vec.astype(F32)[:, None], (vec.shape[0], LANES)))


def kernel(x_prompt, x_sample, cache_swa_k, cache_swa_v, state_mlstm_C, state_mlstm_n, state_mlstm_m, meta_tokens, rel_bias, w_in, b_igate, b_fgate, attn_sinks, g_mlstm_out, g_attn_out, w_out, ln1_g, ln1_b, w_router, b_router, w_moe1, b_moe1, w_moe2, b_moe2, ln2_g, ln2_b):
    B, S, _ = x_prompt.shape
    NB = x_sample.shape[0]
    assert x_sample.shape[1] == 1 and w_in.shape[0] == 1
    assert S % PROJ_TILE == 0 and S % M_CHUNK == 0 and S % WINDOW == 0 and NB % SAMPLE_GROUP == 0
    l = 0

    assert IN_WIDTHS == (512, 512, 512, 512, 4, 4, 512, 128, 128)
    bf = lambda a: a.astype(BF16)
    w = w_in[l]
    n_main, n_gate = 4 * M_WIDTH, 2 * M_HEADS
    w_gate = w[:, n_main:n_main + n_gate]
    w_att = w[:, n_main + n_gate:]
    assert math.frexp(A_HD ** -0.5)[0] == 0.5
    w_qa = _perm_heads(w_att[:, :A_WIDTH], 1) * (A_HD ** -0.5)
    wr = bf(jnp.concatenate([w[:, :n_main], w_qa, w_att[:, A_WIDTH:],
                             jnp.pad(w_gate, ((0, 0), (0, LANES - n_gate)))], axis=1))
    wt = bf(jnp.concatenate([w[:, :M_WIDTH], w[:, 2 * M_WIDTH:3 * M_WIDTH], w_qa, w_att[:, A_WIDTH + LANES:],
                             w_gate], axis=1).T)
    b_gate = jnp.concatenate([b_igate[l], b_fgate[l]]).astype(F32)
    brow = jnp.pad(b_gate, (0, LANES - n_gate))[None, :]
    bcol = b_gate[:, None]
    plan_p = ((512, 512, "plain", BF16), (1536, 512, "plain", F32), (2560, 128, "plain", BF16),
              (2816, 128, "gate", F32))
    tplan_p = ((0, 512, "plain", BF16), (512, 512, "plain", BF16), (1024, 512, "plain", BF16),
               (1536, 128, "plain", BF16), (1664, 8, "gate", F32))
    plan_s = ((0, 512, "plain", F32), (512, 512, "plain", F32), (1024, 512, "plain", F32), (1536, 512, "plain", F32),
              (2048, 512, "plain", F32), (2560, 128, "plain", F32), (2688, 128, "plain", F32), (2816, 128, "gate", F32))

    bias_tab, bias_rows = _bias_tables(rel_bias)
    sinks = _rep_rows(attn_sinks[l])
    sinks_step = _rep_rows(attn_sinks[l][np.asarray(HEAD_ORDER)])
    g_m = g_mlstm_out[l].astype(F32)[None, :]
    g_a = _perm_heads(g_attn_out[l].astype(F32), 0)[None, :]
    wo = bf(jnp.concatenate([w_out[l][:M_WIDTH], _perm_heads(w_out[l][M_WIDTH:], 0)], axis=0))
    g1, b1 = ln1_g[l].astype(F32)[None, :], ln1_b[l].astype(F32)[None, :]
    g2, b2 = ln2_g[l].astype(F32)[None, :], ln2_b[l].astype(F32)[None, :]
    w_r = bf(jnp.pad(w_router[l], ((0, 0), (0, LANES - N_EXPERTS))))
    b_r = jnp.pad(b_router[l].astype(F32), (0, LANES - N_EXPERTS), constant_values=NEG)[None, :]
    b1g = b_moe1[l][:, 0::2].astype(F32)[:, None, :]
    b1l = b_moe1[l][:, 1::2].astype(F32)[:, None, :]
    b2e = b_moe2[l].astype(F32)[:, None, :]
    pj = np.zeros((256, 256), np.float32)
    pj[2 * np.arange(128), np.arange(128)] = 1.0
    pj[2 * np.arange(128) + 1, 128 + np.arange(128)] = 1.0
    perm = jnp.asarray(pj, BF16)

    xp2 = x_prompt.reshape(B * S, D_MODEL)
    km, om, ka, gc, qt, vt, qat, vat, gr, kv_tail = _proj(
        xp2, wr, wt, brow, bcol, plan_p, tplan_p, (2560, 256), PROJ_TILE, S, "proj_prompt")
    x_meta = jnp.pad(meta_tokens.astype(F32), ((0, M_CHUNK - N_META), (0, 0)))
    km0, _, ka0, gc0, qt0, vt0, _, vat0, gr0 = _proj(
        x_meta, wr, wt, brow, bcol, plan_p, tplan_p, None, M_CHUNK, M_CHUNK, "proj_meta")
    xs2 = x_sample.reshape(NB, D_MODEL)
    qm_s, km_s, vm_s, om_s, qa_s, ka_s, va_s, gc_s = _proj(
        xs2, wr, wt, brow, bcol, plan_s, (), None, NB, NB, "proj_sample")

    zero_c = jnp.zeros((M_HEADS, M_DV + 8, M_DK), F32)
    zero_m = jnp.zeros((8, LANES), F32)
    _, c_meta, m_meta = _mlstm(qt0, km0, vt0, gc0, gr0, zero_c, zero_m, 1, N_META, "mlstm_meta")
    h_p, c_p, m_p = _mlstm(qt, km, vt, gc, gr, c_meta[0], m_meta[0], B, M_CHUNK, "mlstm_prompt")
    C_p = c_p[:, :, :M_DV, :]
    n_p = c_p[:, :, M_DV, :]
    m_prompt = m_p[:, :M_HEADS, 0]
    m_pad = jnp.pad(state_mlstm_m[l].astype(F32), ((0, 0), (0, LANES - M_HEADS)))
    C_s, n_s, m_s, h_s = _mlstm_step(state_mlstm_C[l].astype(F32), state_mlstm_n[l].astype(F32), m_pad,
                                     gc_s, qm_s, km_s, vm_s)

    att_p = _swa(qat, ka, vat, ka0, vat0, jnp.swapaxes(bias_tab, 2, 3), sinks, B)
    ck = cache_swa_k[l].reshape(NB, WINDOW, LANES)
    cv = cache_swa_v[l].reshape(NB, WINDOW, LANES)
    k_new, v_new, att_s = _swa_step(ck, cv, qa_s, ka_s, va_s, bias_rows, sinks_step)

    x1_p, xpk_p, tk_p, cnt_p = _merge(h_p, om, att_p, xp2, g_m, g_a, wo, g1, b1, w_r, b_r, MERGE_TILE, "merge_prompt")
    x1_s, xpk_s, tk_s, cnt_s = _merge(h_s, om_s, att_s, xs2, g_m, g_a, wo, g1, b1, w_r, b_r, NB, "merge_sample")

    T_p = B * S
    assert T_p % RANK_TILE == 0 and T_p % ROW_TILE == 0
    n_blocks = -(-((T_p + NB) * TOP_K) // EXPERT_TILE) + N_EXPERTS
    off, be2, nu2, pad = _offsets(cnt_p + cnt_s, n_blocks, EXPERT_TILE)
    pads = jnp.concatenate([pad[0, :N_EXPERTS], pad[1, :N_EXPERTS], nu2[0, :1]])
    dest_p = _route(tk_p, off)[:, :TOP_K].reshape(-1)
    dest_s = _route(tk_s, off + cnt_p)[:, :TOP_K].reshape(-1)
    be = be2.reshape(-1)[:n_blocks]
    nu = nu2[0, :1]
    xs = _dispatch(dest_p, pads, dest_s, xpk_p, xpk_s, n_blocks * EXPERT_TILE, EXPERT_TILE)
    ys = _experts(be, nu, xs, w_moe1[l], b1g, b1l, w_moe2[l], b2e, perm, EXPERT_TILE)
    y_p = _combine(dest_p, ys, tk_p, x1_p, g2, b2)
    y_s = _combine(dest_s, ys, tk_s, x1_s, g2, b2)

    kv_tail = kv_tail.reshape(B, WINDOW, 2, A_KV_HEADS, A_HD)
    dt_k, dt_v = cache_swa_k.dtype, cache_swa_v.dtype
    return (y_p.reshape(B, S, D_MODEL).astype(x_prompt.dtype), y_s.reshape(NB, 1, D_MODEL).astype(x_sample.dtype),
            kv_tail[:, :, 0][None].astype(dt_k), kv_tail[:, :, 1][None].astype(dt_v),
            C_p[None].astype(state_mlstm_C.dtype), n_p[None].astype(state_mlstm_n.dtype),
            m_prompt[None].astype(state_mlstm_m.dtype),
            k_new.reshape(1, NB, WINDOW, A_KV_HEADS, A_HD).astype(dt_k),
            v_new.reshape(1, NB, WINDOW, A_KV_HEADS, A_HD).astype(dt_v),
            C_s[None].astype(state_mlstm_C.dtype), n_s[None].astype(state_mlstm_n.dtype),
            m_s[:, :M_HEADS][None].astype(state_mlstm_m.dtype))
```

```python
import functools
import math

import numpy as np
import jax
import jax.numpy as jnp
from jax import lax
from jax.experimental import pallas as pl
from jax.experimental.pallas import tpu as pltpu

F32 = jnp.float32
BF16 = jnp.bfloat16
I32 = jnp.int32
U32 = jnp.uint32

D_MODEL = 1024
N_META = 16
M_HEADS = 4
M_DK = 128
M_DV = 128
M_WIDTH = M_HEADS * M_DV
A_HD = 64
A_HEADS = 8
A_KV_HEADS = 2
A_GROUP = A_HEADS // A_KV_HEADS
A_WIDTH = A_HEADS * A_HD
WINDOW = 128
REL_BUCKETS = 32
REL_MAX_DIST = 128
N_EXPERTS = 32
TOP_K = 4
D_FF = D_MODEL
SWIGLU_LIMIT = 7.0
SWIGLU_ALPHA = 1.702
DEPTH = 1
DN_ALPHA = (2.0 * DEPTH) ** 0.25
LN_EPS = 1e-5
IN_WIDTHS = (M_WIDTH, M_WIDTH, M_WIDTH, M_WIDTH, M_HEADS, M_HEADS, A_WIDTH, A_KV_HEADS * A_HD, A_KV_HEADS * A_HD)

LANES = 128
NEG = -1e30
VMEM_LIMIT = 56 * 1024 * 1024

M_CHUNK = 256
PROJ_TILE = 1024
MERGE_TILE = 1024
RANK_TILE = 1024
ROW_TILE = 512
EXPERT_TILE = 512
SAMPLE_GROUP = 8
ISSUE_GROUP = 8
SWA_QBLOCKS = 8
STEP_PIPE_LAG = 3
SWA_PIPE_LAG = 2
XP_TILE = D_MODEL // 2 // LANES
YS_TILE = D_MODEL // 2 // LANES
HEAD_ORDER = (0, 4, 1, 5, 2, 6, 3, 7)


def _cparams(*sem):
    return pltpu.CompilerParams(dimension_semantics=sem, vmem_limit_bytes=VMEM_LIMIT)


def _log_sigmoid(x):
    return jnp.minimum(x, 0.0) - jnp.log1p(jnp.exp(-jnp.abs(x)))


def _sigmoid(x):
    return 1.0 / (1.0 + jnp.exp(-x))


def _proj_kernel(x_ref, wr_ref, wt_ref, brow_ref, bcol_ref, *outs, row_plan, t_plan, tail_cols):
    xb = x_ref[...].astype(BF16)
    tm = xb.shape[0]
    o = 0
    for (c0, width, kind, _) in row_plan:
        r = jnp.dot(xb, wr_ref[:, c0:c0 + width], preferred_element_type=F32)
        if kind == "gate":
            r = r + brow_ref[...]
            lane = lax.broadcasted_iota(I32, r.shape, 1)
            r = jnp.where(lane < M_HEADS, r, _log_sigmoid(r))
        outs[o][...] = r.astype(outs[o].dtype)
        o += 1
    for (r0, nrows, kind, _) in t_plan:
        r = lax.dot_general(wt_ref[r0:r0 + nrows, :], xb, (((1,), (1,)), ((), ())), preferred_element_type=F32)
        if kind == "gate":
            r = r + bcol_ref[...]
            row = lax.broadcasted_iota(I32, r.shape, 0)
            r = jnp.where(row < M_HEADS, r, _log_sigmoid(r))
        outs[o][...] = r.astype(outs[o].dtype)
        o += 1
    if tail_cols is not None:
        c0, width = tail_cols
        outs[o][...] = jnp.dot(xb[tm - WINDOW:, :], wr_ref[:, c0:c0 + width], preferred_element_type=F32)


def _proj(x, wr, wt, brow, bcol, row_plan, t_plan, tail_cols, tile, rows_per_group, name):
    t = x.shape[0]
    nt = t // tile
    out_shape, out_specs = [], []
    for (_, width, _, dt) in row_plan:
        out_shape.append(jax.ShapeDtypeStruct((t, width), dt))
        out_specs.append(pl.BlockSpec((tile, width), lambda i: (i, 0)))
    for (_, nrows, _, dt) in t_plan:
        out_shape.append(jax.ShapeDtypeStruct((nrows, t), dt))
        out_specs.append(pl.BlockSpec((nrows, tile), lambda i: (0, i)))
    if tail_cols is not None:
        tiles_per_group = rows_per_group // tile
        out_shape.append(jax.ShapeDtypeStruct((t // rows_per_group * WINDOW, tail_cols[1]), F32))
        out_specs.append(pl.BlockSpec((WINDOW, tail_cols[1]), lambda i: (i // tiles_per_group, 0)))
    kern = functools.partial(_proj_kernel, row_plan=row_plan, t_plan=t_plan, tail_cols=tail_cols)
    return pl.pallas_call(
        kern, out_shape=out_shape, grid=(nt,),
        in_specs=[pl.BlockSpec((tile, D_MODEL), lambda i: (i, 0)),
                  pl.BlockSpec(wr.shape, lambda i: (0, 0)),
                  pl.BlockSpec(wt.shape, lambda i: (0, 0)),
                  pl.BlockSpec(brow.shape, lambda i: (0, 0)),
                  pl.BlockSpec(bcol.shape, lambda i: (0, 0))],
        out_specs=out_specs, compiler_params=_cparams("arbitrary"), name=name,
    )(x, wr, wt, brow, bcol)


def _split3(a):
    hi = a.astype(BF16)
    r1 = a - hi.astype(F32)
    mid = r1.astype(BF16)
    lo = (r1 - mid.astype(F32)).astype(BF16)
    return hi, mid, lo


def _mlstm_kernel(qt_ref, k_ref, vt_ref, gc_ref, gr_ref, c0_ref, m0_ref, h_ref, c_out_ref, m_out_ref,
                  c_scr, m_scr, *, n_valid):
    c = pl.program_id(1)
    nc = pl.num_programs(1)
    L = k_ref.shape[0]

    @pl.when(c == 0)
    def _():
        c_scr[...] = c0_ref[...]
        m_scr[...] = m0_ref[...]

    gc = gc_ref[...]
    gr = gr_ref[...]
    if n_valid < L:
        rowc = lax.broadcasted_iota(I32, gc.shape, 0)
        lanec = lax.broadcasted_iota(I32, gc.shape, 1)
        gc = jnp.where(rowc < n_valid, gc, jnp.where(lanec < M_HEADS, NEG, 0.0))
        rowr = lax.broadcasted_iota(I32, gr.shape, 0)
        colr = lax.broadcasted_iota(I32, gr.shape, 1)
        gr = jnp.where(colr < n_valid, gr, jnp.where(rowr < M_HEADS, NEG, 0.0))
    r_i = lax.broadcasted_iota(I32, (L, L), 0)
    c_i = lax.broadcasted_iota(I32, (L, L), 1)
    upper = r_i <= c_i
    tril = jnp.where(c_i <= r_i, 1.0, 0.0).astype(BF16)
    triu = jnp.where(upper, 1.0, 0.0).astype(BF16)
    b_cols = sum(jnp.dot(tril, part, preferred_element_type=F32) for part in _split3(gc))
    b_rows = sum(jnp.dot(part, triu, preferred_element_type=F32) for part in _split3(gr))
    scale = M_DK ** -0.5
    ones_rows = jnp.where(lax.broadcasted_iota(I32, (8, L), 0) == 0, 1.0, 0.0).astype(BF16)

    m_all = m_scr[...]
    c_all = [c_scr[h] for h in range(M_HEADS)]
    h_new, c_new, m_new_all = {}, {}, {}

    def operands(h):
        sl = slice(h * M_DK, (h + 1) * M_DK)
        return qt_ref[sl, :], k_ref[:, sl], jnp.concatenate([vt_ref[sl, :], ones_rows], axis=0)

    def stage_a(h):
        qt, k, vt_aug = operands(h)
        ig_r = gr[h:h + 1, :]
        b_r = b_rows[M_HEADS + h:M_HEADS + h + 1, :]
        m_prev = m_all[h:h + 1, 0:1]
        cs = c_all[h]
        qk = jnp.dot(k, qt, preferred_element_type=F32)
        inter = jnp.dot(cs.astype(BF16), qt, preferred_element_type=F32)
        b_last = b_r[:, L - 1:L]
        g = ig_r + b_last - b_r
        m_new = jnp.maximum(b_last + m_prev, jnp.max(g, axis=1, keepdims=True))
        a = jnp.exp(b_last + m_prev - m_new)
        wv = (vt_aug.astype(F32) * jnp.exp(g - m_new)).astype(BF16)
        c_new[h] = a * cs + jnp.dot(wv, k, preferred_element_type=F32) * scale
        m_new_all[h] = jnp.broadcast_to(m_new, (1, LANES))
        return qk, inter

    def stage_b(h, qk, inter):
        b_r = b_rows[M_HEADS + h:M_HEADS + h + 1, :]
        m_prev = m_all[h:h + 1, 0:1]
        r_c = gc[:, h:h + 1] - b_cols[:, M_HEADS + h:M_HEADS + h + 1]
        dt = jnp.where(upper, b_r + r_c, NEG)
        m_t = jnp.maximum(b_r + m_prev, jnp.max(dt, axis=0, keepdims=True))
        st = (qk * (scale * jnp.exp(dt - m_t))).astype(BF16)
        return st, jnp.exp(b_r + m_prev - m_t) * inter, jnp.exp(-m_t)

    def stage_c(h, st, inter_w, floor):
        _, _, vt_aug = operands(h)
        nd = inter_w + jnp.dot(vt_aug, st, preferred_element_type=F32)
        den = nd[M_DV:M_DV + 1, :]
        h_new[h] = (nd[:M_DV, :] / jnp.maximum(jnp.abs(den), floor)).T

    a_q, b_q = {}, {}
    for i in range(M_HEADS + 2):
        if i < M_HEADS:
            a_q[i] = stage_a(i)
        if 0 <= i - 1 < M_HEADS:
            b_q[i - 1] = stage_b(i - 1, *a_q.pop(i - 1))
        if 0 <= i - 2 < M_HEADS:
            stage_c(i - 2, *b_q.pop(i - 2))

    h_ref[...] = jnp.concatenate([h_new[h] for h in range(M_HEADS)], axis=1)
    for h in range(M_HEADS):
        c_scr[h] = c_new[h]
    m_scr[0:M_HEADS, :] = jnp.concatenate([m_new_all[h] for h in range(M_HEADS)], axis=0)

    @pl.when(c == nc - 1)
    def _():
        c_out_ref[0] = c_scr[...]
        m_out_ref[0] = m_scr[...]


def _mlstm(qt, km, vt, gc, gr, c0, m0, batch, n_valid, name):
    L = M_CHUNK
    nc = km.shape[0] // (batch * L)
    kern = functools.partial(_mlstm_kernel, n_valid=n_valid)
    rows = pl.BlockSpec((L, M_WIDTH), lambda b, c: (b * nc + c, 0))
    cols = pl.BlockSpec((M_WIDTH, L), lambda b, c: (0, b * nc + c))
    return pl.pallas_call(
        kern,
        out_shape=[jax.ShapeDtypeStruct((batch * nc * L, M_WIDTH), F32),
                   jax.ShapeDtypeStruct((batch, M_HEADS, M_DV + 8, M_DK), F32),
                   jax.ShapeDtypeStruct((batch, 8, LANES), F32)],
        grid=(batch, nc),
        in_specs=[cols, rows, cols,
                  pl.BlockSpec((L, LANES), lambda b, c: (b * nc + c, 0)),
                  pl.BlockSpec((8, L), lambda b, c: (0, b * nc + c)),
                  pl.BlockSpec((M_HEADS, M_DV + 8, M_DK), lambda b, c: (0, 0, 0)),
                  pl.BlockSpec((8, LANES), lambda b, c: (0, 0))],
        out_specs=[rows,
                   pl.BlockSpec((1, M_HEADS, M_DV + 8, M_DK), lambda b, c: (b, 0, 0, 0)),
                   pl.BlockSpec((1, 8, LANES), lambda b, c: (b, 0, 0))],
        scratch_shapes=[pltpu.VMEM((M_HEADS, M_DV + 8, M_DK), F32), pltpu.VMEM((8, LANES), F32)],
        compiler_params=_cparams("arbitrary", "arbitrary"), name=name,
    )(qt, km, vt, gc, gr, c0, m0)


def _outer_f32(a, b):
    ah, am, al = (t.astype(F32) for t in _split3(a))
    bh, bm, bl = (t.astype(F32) for t in _split3(b))
    z = jnp.zeros_like(ah)
    lhs = jnp.concatenate([ah, ah, ah, am, am, al, z, z], axis=0).astype(BF16)
    rhs = jnp.concatenate([bh, bm, bl, bh, bm, bh, z, z], axis=0).astype(BF16)
    return lax.dot_general(lhs, rhs, (((0,), (0,)), ((), ())), preferred_element_type=F32)


def _mlstm_step_kernel(c_ref, n_ref, m_ref, gc_ref, q_ref, k_ref, v_ref,
                       c_out_ref, n_out_ref, m_out_ref, h_ref):
    g = c_ref.shape[0]
    assert g == 8
    scale = M_DK ** -0.5
    ig = gc_ref[:, 0:M_HEADS]
    lf = gc_ref[:, M_HEADS:2 * M_HEADS]
    m = m_ref[:, 0:M_HEADS]
    m_t = jnp.maximum(lf + m, ig)
    w = jnp.exp(lf + m - m_t)
    wg = jnp.exp(ig - m_t)
    floor = jnp.exp(-m_t)
    m_out_ref[...] = jnp.zeros_like(m_out_ref)
    m_out_ref[:, 0:M_HEADS] = m_t
    row8 = lax.broadcasted_iota(I32, (g, M_DV), 0)

    per_head = []
    for h in range(M_HEADS):
        sl = slice(h * M_DK, (h + 1) * M_DK)
        q = q_ref[:, sl]
        k = k_ref[:, sl] * scale
        v = v_ref[:, sl]
        n = n_ref[:, h, :]
        w_h, wg_h = w[:, h:h + 1], wg[:, h:h + 1]
        s = jnp.sum(q * k, axis=1, keepdims=True) * wg_h
        den = w_h * jnp.sum(n * q, axis=1, keepdims=True) + s
        n_out_ref[:, h, :] = w_h * n + wg_h * k
        per_head.append((q.astype(BF16), k, wg_h * v, w_h, s * v, 1.0 / jnp.maximum(jnp.abs(den), floor[:, h:h + 1])))

    def stage_a(h, j):
        qb = per_head[h][0]
        r = lax.dot_general(qb, c_ref[j, h].astype(BF16), (((1,), (1,)), ((), ())), preferred_element_type=F32)
        return jnp.where(row8 == j, r, 0.0)

    def stage_b(h, j):
        _, k, wv, w_h, _, _ = per_head[h]
        c_out_ref[j, h] = w_h[j:j + 1, :] * c_ref[j, h] + _outer_f32(wv[j:j + 1, :], k[j:j + 1, :])

    units = [(h, j) for h in range(M_HEADS) for j in range(g)]
    lag = STEP_PIPE_LAG
    cq = [jnp.zeros((g, M_DV), F32) for _ in range(M_HEADS)]
    for i in range(len(units) + lag):
        if i < len(units):
            cq[units[i][0]] = cq[units[i][0]] + stage_a(*units[i])
        if 0 <= i - lag < len(units):
            stage_b(*units[i - lag])
    for h in range(M_HEADS):
        _, _, _, w_h, sv, inv = per_head[h]
        h_ref[:, h * M_DK:(h + 1) * M_DK] = (w_h * cq[h] + sv) * inv


def _mlstm_step(c, n, m_pad, gc, q, k, v):
    nb = c.shape[0]
    g = SAMPLE_GROUP
    row = lambda w: pl.BlockSpec((g, w), lambda i: (i, 0))
    return pl.pallas_call(
        _mlstm_step_kernel,
        out_shape=[jax.ShapeDtypeStruct(c.shape, F32), jax.ShapeDtypeStruct(n.shape, F32),
                   jax.ShapeDtypeStruct((nb, LANES), F32), jax.ShapeDtypeStruct((nb, M_WIDTH), F32)],
        grid=(nb // g,),
        in_specs=[pl.BlockSpec((g, M_HEADS, M_DV, M_DK), lambda i: (i, 0, 0, 0)),
                  pl.BlockSpec((g, M_HEADS, M_DK), lambda i: (i, 0, 0)),
                  row(LANES), row(LANES), row(M_WIDTH), row(M_WIDTH), row(M_WIDTH)],
        out_specs=[pl.BlockSpec((g, M_HEADS, M_DV, M_DK), lambda i: (i, 0, 0, 0)),
                   pl.BlockSpec((g, M_HEADS, M_DK), lambda i: (i, 0, 0)),
                   row(LANES), row(M_WIDTH)],
        compiler_params=_cparams("arbitrary"), name="mlstm_step",
    )(c, n, m_pad, gc, q, k, v)


def _swa_kernel(qt_ref, kc_ref, kp_ref, vtc_ref, vtp_ref, km_ref, vtm_ref, bias_ref, sink_ref, o_ref):
    j = pl.program_id(1)
    first = j == 0
    blk = WINDOW
    nqb = qt_ref.shape[1] // blk
    kp = jnp.where(first, km_ref[...], kp_ref[...])
    vtp = jnp.where(first, vtm_ref[...], vtp_ref[...])
    k = jnp.concatenate([kp, kc_ref[...]], axis=0)
    vt = jnp.concatenate([vtp, vtc_ref[...]], axis=1)
    row_v = lax.broadcasted_iota(I32, vt.shape, 0)
    zero_v = jnp.zeros_like(vt)
    vt_half = (jnp.where(row_v < A_HD, vt, zero_v), jnp.where(row_v >= A_HD, vt, zero_v))
    row_q = lax.broadcasted_iota(I32, (LANES, blk), 0)
    lo_rows = row_q < A_HD
    def scores(u, p):
        cols = slice(u * blk, (u + 1) * blk)
        keys = slice(u * blk, (u + 2) * blk)
        qs = qt_ref[p * LANES:(p + 1) * LANES, cols]
        zero_q = jnp.zeros_like(qs)
        q_own = (jnp.where(lo_rows, qs, zero_q), jnp.where(lo_rows, zero_q, qs))
        return [jnp.dot(k[keys], q_own[half], preferred_element_type=F32) for half in range(2)]

    def softmax(u, p, s2):
        table = jnp.where(first, 0, 1) if u == 0 else 1
        probs, inv = [], []
        for half in range(2):
            hd = HEAD_ORDER[2 * p + half]
            s = s2[half] + bias_ref[table, hd]
            sk = sink_ref[hd:hd + 1, 0:1]
            m = jnp.maximum(jnp.max(s, axis=0, keepdims=True), sk)
            e = jnp.exp(s - m)
            probs.append(e.astype(BF16))
            inv.append(1.0 / (jnp.sum(e, axis=0, keepdims=True) + jnp.exp(sk - m)))
        return jnp.concatenate(probs, axis=0), jnp.where(lo_rows, inv[0], inv[1])

    def values(u, p, probs, inv):
        keys = slice(u * blk, (u + 2) * blk)
        vt_stack = jnp.concatenate([vt_half[0][:, keys], vt_half[1][:, keys]], axis=1)
        ot = jnp.dot(vt_stack, probs, preferred_element_type=F32)
        o_ref[u * blk:(u + 1) * blk, p * LANES:(p + 1) * LANES] = (ot * inv).T

    units = [(u, p) for u in range(nqb) for p in range(A_GROUP)]
    s_q, p_q = {}, {}
    lag = SWA_PIPE_LAG
    for i in range(len(units) + 2 * lag):
        if i < len(units):
            s_q[i] = scores(*units[i])
        if 0 <= i - lag < len(units):
            p_q[i - lag] = softmax(*units[i - lag], s_q.pop(i - lag))
        if 0 <= i - 2 * lag < len(units):
            values(*units[i - 2 * lag], *p_q.pop(i - 2 * lag))


def _swa(qat, ka, vat, kmeta, vtmeta, bias_t, sinks, batch):
    blk = WINDOW
    nqb = SWA_QBLOCKS
    t = ka.shape[0]
    nq = t // (batch * blk * nqb)
    prev = lambda b, j: (b * nq + j) * nqb + jnp.where(j == 0, 0, -1)
    const2 = lambda shape: pl.BlockSpec(shape, lambda b, j: (0, 0))
    return pl.pallas_call(
        _swa_kernel, out_shape=jax.ShapeDtypeStruct((t, A_WIDTH), F32), grid=(batch, nq),
        in_specs=[pl.BlockSpec((A_WIDTH, nqb * blk), lambda b, j: (0, b * nq + j)),
                  pl.BlockSpec((nqb * blk, LANES), lambda b, j: (b * nq + j, 0)),
                  pl.BlockSpec((blk, LANES), lambda b, j: (prev(b, j), 0)),
                  pl.BlockSpec((LANES, nqb * blk), lambda b, j: (0, b * nq + j)),
                  pl.BlockSpec((LANES, blk), lambda b, j: (0, prev(b, j))),
                  const2((blk, LANES)), const2((LANES, blk)),
                  pl.BlockSpec(bias_t.shape, lambda b, j: (0, 0, 0, 0)),
                  const2((8, LANES))],
        out_specs=pl.BlockSpec((nqb * blk, A_WIDTH), lambda b, j: (b * nq + j, 0)),
        compiler_params=_cparams("arbitrary", "arbitrary"), name="swa_prompt",
    )(qat, ka, ka, vat, vat, kmeta, vtmeta, bias_t, sinks)


def _swa_step_kernel(ck_ref, cv_ref, q_ref, k_ref, v_ref, bias_ref, sink_ref, ko_ref, vo_ref, o_ref):
    g = ck_ref.shape[0]
    lane = lax.broadcasted_iota(I32, (A_HEADS, LANES), 1)
    row = lax.broadcasted_iota(I32, (A_HEADS, LANES), 0)
    own_half = (row % 2 == 0) == (lane < A_HD)
    bias = bias_ref[...]
    sk = sink_ref[:, 0:1]
    def stage_a(j):
        ko_ref[j, 0:WINDOW - 1, :] = ck_ref[j, 1:WINDOW, :]
        ko_ref[j, WINDOW - 1:WINDOW, :] = k_ref[j:j + 1, :]
        vo_ref[j, 0:WINDOW - 1, :] = cv_ref[j, 1:WINDOW, :]
        vo_ref[j, WINDOW - 1:WINDOW, :] = v_ref[j:j + 1, :]
        kk = ko_ref[j].astype(BF16)
        slabs = [q_ref[j:j + 1, p * LANES:(p + 1) * LANES] for p in range(A_GROUP)]
        q8 = jnp.concatenate([slabs[r // 2] for r in range(A_HEADS)], axis=0)
        q8 = jnp.where(own_half, q8, 0.0).astype(BF16)
        s = lax.dot_general(q8, kk, (((1,), (1,)), ((), ())), preferred_element_type=F32)
        s = s + bias
        m = jnp.maximum(jnp.max(s, axis=1, keepdims=True), sk)
        e = jnp.exp(s - m)
        return e.astype(BF16), 1.0 / (jnp.sum(e, axis=1, keepdims=True) + jnp.exp(sk - m))

    def stage_b(j, p8, inv):
        vv = vo_ref[j].astype(BF16)
        o8 = jnp.where(own_half, jnp.dot(p8, vv, preferred_element_type=F32) * inv, 0.0)
        for p in range(A_GROUP):
            o_ref[j:j + 1, p * LANES:(p + 1) * LANES] = o8[2 * p:2 * p + 1, :] + o8[2 * p + 1:2 * p + 2, :]

    lag, pending = STEP_PIPE_LAG, {}
    for i in range(g + lag):
        if i < g:
            pending[i] = stage_a(i)
        if 0 <= i - lag < g:
            stage_b(i - lag, *pending.pop(i - lag))


def _swa_step(ck, cv, q, k, v, bias_rows, sinks):
    nb = ck.shape[0]
    g = SAMPLE_GROUP
    cache = pl.BlockSpec((g, WINDOW, LANES), lambda i: (i, 0, 0))
    row = lambda w: pl.BlockSpec((g, w), lambda i: (i, 0))
    const = lambda a: pl.BlockSpec(a.shape, lambda i: (0, 0))
    return pl.pallas_call(
        _swa_step_kernel,
        out_shape=[jax.ShapeDtypeStruct(ck.shape, F32), jax.ShapeDtypeStruct(cv.shape, F32),
                   jax.ShapeDtypeStruct((nb, A_WIDTH), F32)],
        grid=(nb // g,),
        in_specs=[cache, cache, row(A_WIDTH), row(LANES), row(LANES), const(bias_rows), const(sinks)],
        out_specs=[cache, cache, row(A_WIDTH)],
        compiler_params=_cparams("arbitrary"), name="swa_step",
    )(ck, cv, q, k, v, bias_rows, sinks)


def _layer_norm(z, g, b):
    mu = jnp.mean(z, axis=1, keepdims=True)
    zc = z - mu
    var = jnp.mean(zc * zc, axis=1, keepdims=True)
    return zc * lax.rsqrt(var + LN_EPS) * g + b


def _pack_halves(x):
    w = x.shape[1] // 2
    lo = pltpu.bitcast(x[:, :w].astype(BF16).astype(F32), U32)
    hi = pltpu.bitcast(x[:, w:].astype(BF16).astype(F32), U32)
    return (lo >> 16) | (hi & jnp.uint32(0xFFFF0000))


def _unpack_halves(words):
    lo = pltpu.bitcast(words << 16, F32).astype(BF16)
    hi = pltpu.bitcast(words & jnp.uint32(0xFFFF0000), F32).astype(BF16)
    return lo, hi


def _to_token_tiles(ref, x):
    for q in range(x.shape[1] // LANES):
        ref[:, q, :] = x[:, q * LANES:(q + 1) * LANES]


def _merge_kernel(h_ref, om_ref, att_ref, x_ref, gm_ref, ga_ref, wo_ref, g1_ref, b1_ref, wr_ref, br_ref,
                  x1_ref, xp_ref, tk_ref, cnt_ref):
    @pl.when(pl.program_id(0) == 0)
    def _():
        cnt_ref[...] = jnp.zeros_like(cnt_ref)

    hm = h_ref[...] * _sigmoid(om_ref[...])
    ym = hm * lax.rsqrt(jnp.mean(hm * hm, axis=1, keepdims=True) + LN_EPS) * gm_ref[...]
    att = att_ref[...]
    ya = att * lax.rsqrt(jnp.mean(att * att, axis=1, keepdims=True) + LN_EPS) * ga_ref[...]
    mix = (jnp.dot(ym.astype(BF16), wo_ref[0:M_WIDTH, :], preferred_element_type=F32)
           + jnp.dot(ya.astype(BF16), wo_ref[M_WIDTH:, :], preferred_element_type=F32))
    x1 = _layer_norm(DN_ALPHA * x_ref[...] + mix, g1_ref[...], b1_ref[...])
    x1_ref[...] = x1
    _to_token_tiles(xp_ref, _pack_halves(x1))
    logits = jnp.dot(x1.astype(BF16), wr_ref[...], preferred_element_type=F32) + br_ref[...]
    lane = lax.broadcasted_iota(I32, logits.shape, 1).astype(F32)
    vals, idxs = [], []
    for _ in range(TOP_K):
        mx = jnp.max(logits, axis=1, keepdims=True)
        idx = jnp.min(jnp.where(logits == mx, lane, float(LANES)), axis=1, keepdims=True)
        vals.append(mx)
        idxs.append(idx)
        logits = jnp.where(lane == idx, 2.0 * NEG, logits)
    es = [jnp.exp(vk - vals[0]) for vk in vals]
    tot = es[0] + es[1] + es[2] + es[3]
    tk = jnp.zeros(logits.shape, F32)
    picked = jnp.zeros(logits.shape, F32)
    for k in range(TOP_K):
        tk = jnp.where(lane == float(k), es[k] / tot, tk)
        tk = jnp.where(lane == float(TOP_K + k), idxs[k], tk)
        picked = jnp.where(lane == idxs[k], 1.0, picked)
    tk_ref[...] = tk
    cnt_ref[...] = cnt_ref[...] + jnp.sum(picked, axis=0, keepdims=True)


def _merge(h, om, att, x, gm, ga, wo, g1, b1, wr, br, tile, name):
    t = x.shape[0]
    rows = lambda w: pl.BlockSpec((tile, w), lambda i: (i, 0))
    const = lambda a: pl.BlockSpec(a.shape, lambda i: (0, 0))
    return pl.pallas_call(
        _merge_kernel,
        out_shape=[jax.ShapeDtypeStruct((t, D_MODEL), F32), jax.ShapeDtypeStruct((t, XP_TILE, LANES), U32),
                   jax.ShapeDtypeStruct((t, LANES), F32), jax.ShapeDtypeStruct((8, LANES), F32)],
        grid=(t // tile,),
        in_specs=[rows(M_WIDTH), rows(M_WIDTH), rows(A_WIDTH), rows(D_MODEL), const(gm), const(ga), const(wo),
                  const(g1), const(b1), const(wr), const(br)],
        out_specs=[rows(D_MODEL), pl.BlockSpec((tile, XP_TILE, LANES), lambda i: (i, 0, 0)), rows(LANES),
                   pl.BlockSpec((8, LANES), lambda i: (0, 0))],
        compiler_params=_cparams("arbitrary"), name=name,
    )(h, om, att, x, gm, ga, wo, g1, b1, wr, br)


def _route_kernel(tk_ref, first_ref, strict_ref, dest_ref, next_scr):
    @pl.when(pl.program_id(0) == 0)
    def _():
        next_scr[...] = first_ref[...]

    tk = tk_ref[...]
    lane = lax.broadcasted_iota(I32, tk.shape, 1).astype(F32)
    onehots = [jnp.where(lane == tk[:, TOP_K + k:TOP_K + k + 1], 1.0, 0.0) for k in range(TOP_K)]
    tot = onehots[0] + onehots[1] + onehots[2] + onehots[3]
    row = jnp.dot(strict_ref[...], tot.astype(BF16), preferred_element_type=F32) + next_scr[0:1, :]
    out = jnp.zeros(tk.shape, F32)
    for k in range(TOP_K):
        out = jnp.where(lane == float(k), jnp.sum(onehots[k] * row, axis=1, keepdims=True), out)
    dest_ref[...] = out.astype(I32)
    next_scr[...] = next_scr[...] + jnp.sum(tot, axis=0, keepdims=True)


def _route(tk, first):
    t = tk.shape[0]
    tile = min(RANK_TILE, t)
    strict = jnp.asarray(np.tril(np.ones((tile, tile), np.float32), -1), BF16)
    return pl.pallas_call(
        _route_kernel, out_shape=jax.ShapeDtypeStruct((t, LANES), I32), grid=(t // tile,),
        in_specs=[pl.BlockSpec((tile, LANES), lambda i: (i, 0)), pl.BlockSpec((8, LANES), lambda i: (0, 0)),
                  pl.BlockSpec((tile, tile), lambda i: (0, 0))],
        out_specs=pl.BlockSpec((tile, LANES), lambda i: (i, 0)),
        scratch_shapes=[pltpu.VMEM((8, LANES), F32)],
        compiler_params=_cparams("arbitrary"), name="moe_route",
    )(tk, first, strict)


def _offsets_kernel(cnt_ref, off_ref, be_ref, nu_ref, pad_ref, *, tile):
    cnt = cnt_ref[...]
    nblk = jnp.floor((cnt + float(tile - 1)) * (1.0 / tile))
    r_i = lax.broadcasted_iota(I32, (LANES, LANES), 0)
    c_i = lax.broadcasted_iota(I32, (LANES, LANES), 1)
    incl = jnp.where(r_i <= c_i, 1.0, 0.0).astype(BF16)
    cum = jnp.dot(nblk.astype(BF16), incl, preferred_element_type=F32)
    off = (cum - nblk) * float(tile)
    off_ref[...] = off
    which = lax.broadcasted_iota(I32, cnt.shape, 0)
    pad_ref[...] = jnp.where(which == 0, off + cnt, jnp.where(which == 1, nblk * float(tile) - cnt, 0.0)).astype(I32)
    rows = be_ref.shape[0]
    jb = (lax.broadcasted_iota(I32, (rows, LANES), 0) * LANES + lax.broadcasted_iota(I32, (rows, LANES), 1)).astype(F32)
    acc = jnp.zeros((rows, LANES), F32)
    for e in range(N_EXPERTS):
        acc = acc + jnp.where(jb >= cum[0:1, e:e + 1], 1.0, 0.0)
    be_ref[...] = jnp.minimum(acc, float(N_EXPERTS - 1)).astype(I32)
    nu_ref[...] = jnp.broadcast_to(cum[0:1, N_EXPERTS - 1:N_EXPERTS], nu_ref.shape).astype(I32)


def _offsets(cnt, n_blocks, tile):
    rows = -(-n_blocks // LANES)
    rows = -(-rows // 8) * 8
    return pl.pallas_call(
        functools.partial(_offsets_kernel, tile=tile),
        out_shape=[jax.ShapeDtypeStruct((8, LANES), F32), jax.ShapeDtypeStruct((rows, LANES), I32),
                   jax.ShapeDtypeStruct((8, LANES), I32), jax.ShapeDtypeStruct((8, LANES), I32)],
        name="moe_offsets",
    )(cnt)


def _scatter_rows(dest_ref, xp_ref, xs_ref, sem):
    t = xp_ref.shape[0]

    def row_copy(tok, dst):
        return pltpu.make_async_copy(xp_ref.at[pl.ds(tok, 1)], xs_ref.at[pl.ds(dst, 1)], sem)

    def issue(grp, carry):
        base = pl.multiple_of(grp * ISSUE_GROUP, ISSUE_GROUP)
        for u in range(ISSUE_GROUP):
            for k in range(TOP_K):
                row_copy(base + u, dest_ref[(base + u) * TOP_K + k]).start(priority=k % 2)
        return carry

    lax.fori_loop(0, t // ISSUE_GROUP, issue, 0)
    for k in range(TOP_K):
        pltpu.make_async_copy(xp_ref, xs_ref.at[pl.ds(0, t)], sem).wait()


def _dispatch_kernel(dest_ref, pads_ref, dest2_ref, xp_ref, xp2_ref, xs_ref, sem, zsem, zbuf, *, block_rows):
    zr = zbuf.shape[0]
    n_blocks = xs_ref.shape[0] // block_rows

    @pl.when(pl.program_id(0) == 0)
    def _():
        zbuf[...] = jnp.zeros_like(zbuf)
        used = pads_ref[2 * N_EXPERTS]

        def pieces(e, act):
            start, n = pads_ref[e], pads_ref[N_EXPERTS + e]
            for sh in range(zr.bit_length() - 1, -1, -1):
                b = 1 << sh
                before = lax.shift_left(lax.shift_right_logical(n, sh + 1), sh + 1)

                @pl.when((n & b) != 0)
                def _():
                    act(pltpu.make_async_copy(zbuf.at[pl.ds(0, b)], xs_ref.at[pl.ds(start + before, b)], zsem))

        def tail(jb, act):
            for h in range(block_rows // zr):
                act(pltpu.make_async_copy(zbuf, xs_ref.at[pl.ds(jb * block_rows + h * zr, zr)], zsem))

        for act in (lambda cp: cp.start(), lambda cp: cp.wait()):
            lax.fori_loop(0, N_EXPERTS, lambda e, c: (pieces(e, act), c)[1], 0)
            lax.fori_loop(used, n_blocks, lambda jb, c: (tail(jb, act), c)[1], 0)

    _scatter_rows(dest_ref, xp_ref, xs_ref, sem)

    @pl.when(pl.program_id(0) == pl.num_programs(0) - 1)
    def _():
        _scatter_rows(dest2_ref, xp2_ref, xs_ref, sem)


def _dispatch(dest, pads, dest2, xp, xp2, n_rows, block_rows):
    t = xp.shape[0]
    tile = min(ROW_TILE, t)
    return pl.pallas_call(
        functools.partial(_dispatch_kernel, block_rows=block_rows),
        out_shape=jax.ShapeDtypeStruct((n_rows,) + xp.shape[1:], xp.dtype), grid=(t // tile,),
        in_specs=[pl.BlockSpec((tile * TOP_K,), lambda i: (i,), memory_space=pltpu.SMEM),
                  pl.BlockSpec(memory_space=pltpu.SMEM),
                  pl.BlockSpec(memory_space=pltpu.SMEM),
                  pl.BlockSpec((tile,) + xp.shape[1:], lambda i: (i, 0, 0)),
                  pl.BlockSpec(xp2.shape, lambda i: (0, 0, 0))],
        out_specs=pl.BlockSpec(memory_space=pl.ANY),
        scratch_shapes=[pltpu.SemaphoreType.DMA(()), pltpu.SemaphoreType.DMA(()),
                        pltpu.VMEM((EXPERT_TILE // 2,) + xp.shape[1:], xp.dtype)],
        compiler_params=_cparams("arbitrary"), name="moe_dispatch",
    )(dest, pads, dest2, xp, xp2)


def _expert_kernel(be_ref, nu_ref, xs_ref, w1_ref, b1g_ref, b1l_ref, w2_ref, b2_ref, perm_ref, ys_ref,
                   w1g_scr, w1l_scr, w2_scr, xq_scr, y_scr, sem, osem):
    j = pl.program_id(0)
    active = j < nu_ref[0]
    changed = jnp.logical_or(j == 0, be_ref[j] != be_ref[jnp.maximum(j - 1, 0)])
    tm = y_scr.shape[0]
    slot = lax.rem(j, 2)

    def fetch(blk, slot):
        row0 = pl.multiple_of(blk * tm, tm)
        return [pltpu.make_async_copy(xs_ref.at[pl.ds(row0, tm), q, :], xq_scr.at[slot, q], sem.at[slot])
                for q in range(XP_TILE)]

    def put(blk):
        row0 = pl.multiple_of(blk * tm, tm)
        return [pltpu.make_async_copy(y_scr.at[:, q * LANES:(q + 1) * LANES], ys_ref.at[pl.ds(row0, tm), q, :], osem)
                for q in range(YS_TILE)]

    def emit(y):
        @pl.when(j > 0)
        def _():
            for cp in put(j - 1):
                cp.wait()

        y_scr[...] = _pack_halves(y)
        for cp in put(j):
            cp.start()

    @pl.when(j == 0)
    def _():
        for cp in fetch(0, 0):
            cp.start()

    @pl.when(j + 1 < nu_ref[0])
    def _():
        for cp in fetch(j + 1, 1 - slot):
            cp.start()

    @pl.when(jnp.logical_and(active, changed))
    def _():
        for c in range(2 * D_FF // 256):
            wc = w1_ref[0, :, c * 256:(c + 1) * 256].astype(BF16)
            d = jnp.dot(wc, perm_ref[...], preferred_element_type=F32).astype(BF16)
            w1g_scr[:, c * 128:(c + 1) * 128] = d[:, :128]
            w1l_scr[:, c * 128:(c + 1) * 128] = d[:, 128:]
        for c in range(D_FF // 256):
            w2_scr[c * 256:(c + 1) * 256, :] = w2_ref[0, c * 256:(c + 1) * 256, :].astype(BF16)

    @pl.when(active)
    def _():
        for cp in fetch(j, slot):
            cp.wait()
        lo, hi = _unpack_halves(jnp.concatenate([xq_scr[slot, q] for q in range(XP_TILE)], axis=1))
        xb = jnp.concatenate([lo, hi], axis=1)
        hg = jnp.dot(xb, w1g_scr[...], preferred_element_type=F32) + b1g_ref[0]
        hl = jnp.dot(xb, w1l_scr[...], preferred_element_type=F32) + b1l_ref[0]
        x_glu = jnp.minimum(hg, SWIGLU_LIMIT)
        x_lin = jnp.clip(hl, -SWIGLU_LIMIT, SWIGLU_LIMIT)
        a = x_glu * _sigmoid(SWIGLU_ALPHA * x_glu) * (x_lin + 1.0)
        emit(jnp.dot(a.astype(BF16), w2_scr[...], preferred_element_type=F32) + b2_ref[0])

    @pl.when(jnp.logical_not(active))
    def _():
        emit(jnp.zeros((tm, D_MODEL), F32))

    @pl.when(j == pl.num_programs(0) - 1)
    def _():
        for cp in put(j):
            cp.wait()


def _experts(be, nu, xs, w1, b1g, b1l, w2, b2, perm, tile):
    n_blocks = xs.shape[0] // tile
    grid_spec = pltpu.PrefetchScalarGridSpec(
        num_scalar_prefetch=2, grid=(n_blocks,),
        in_specs=[pl.BlockSpec(memory_space=pl.ANY),
                  pl.BlockSpec((1, D_MODEL, 2 * D_FF), lambda j, be, nu: (be[j], 0, 0)),
                  pl.BlockSpec((1, 1, D_FF), lambda j, be, nu: (be[j], 0, 0)),
                  pl.BlockSpec((1, 1, D_FF), lambda j, be, nu: (be[j], 0, 0)),
                  pl.BlockSpec((1, D_FF, D_MODEL), lambda j, be, nu: (be[j], 0, 0)),
                  pl.BlockSpec((1, 1, D_MODEL), lambda j, be, nu: (be[j], 0, 0)),
                  pl.BlockSpec((256, 256), lambda j, be, nu: (0, 0))],
        out_specs=pl.BlockSpec(memory_space=pl.ANY),
        scratch_shapes=[pltpu.VMEM((D_MODEL, D_FF), BF16), pltpu.VMEM((D_MODEL, D_FF), BF16),
                        pltpu.VMEM((D_FF, D_MODEL), BF16), pltpu.VMEM((2, XP_TILE, tile, LANES), U32),
                        pltpu.VMEM((tile, YS_TILE * LANES), U32), pltpu.SemaphoreType.DMA((2,)),
                        pltpu.SemaphoreType.DMA(())])
    return pl.pallas_call(
        _expert_kernel, out_shape=jax.ShapeDtypeStruct((xs.shape[0], YS_TILE, LANES), U32), grid_spec=grid_spec,
        compiler_params=_cparams("arbitrary"), name="moe_experts",
    )(be, nu, xs, w1, b1g, b1l, w2, b2, perm)


def _combine_kernel(dest_ref, next_ref, ys_ref, tk_ref, x1_ref, g2_ref, b2_ref, out_ref, buf, sem):
    i = pl.program_id(0)
    t = x1_ref.shape[0]
    slot = lax.rem(i, 2)

    def gather(idx_ref, s):
        def issue(grp, carry):
            base = pl.multiple_of(grp * 8, 8)
            for u in range(8):
                for k in range(TOP_K):
                    pltpu.make_async_copy(ys_ref.at[idx_ref[(base + u) * TOP_K + k]],
                                          buf.at[s, k, grp, :, u, :], sem.at[s]).start(priority=k % 2)
            return carry

        lax.fori_loop(0, t // 8, issue, 0)

    @pl.when(i == 0)
    def _():
        gather(dest_ref, 0)

    @pl.when(i + 1 < pl.num_programs(0))
    def _():
        gather(next_ref, 1 - slot)

    for k in range(TOP_K):
        for u in range(8):
            pltpu.make_async_copy(ys_ref.at[pl.ds(0, t // 8)], buf.at[slot, k, :, :, u, :], sem.at[slot]).wait()
    tk = tk_ref[...]
    los, his = [], []
    for q in range(YS_TILE):
        lo = hi = None
        for k in range(TOP_K):
            words = buf[slot, k, :, q].reshape(t, LANES)
            g = tk[:, k:k + 1]
            lo_k = g * pltpu.bitcast(words << 16, F32)
            hi_k = g * pltpu.bitcast(words & jnp.uint32(0xFFFF0000), F32)
            lo = lo_k if lo is None else lo + lo_k
            hi = hi_k if hi is None else hi + hi_k
        los.append(lo)
        his.append(hi)
    ff = jnp.concatenate(los + his, axis=1)
    out_ref[...] = _layer_norm(DN_ALPHA * x1_ref[...] + ff, g2_ref[...], b2_ref[...])


def _combine(dest_flat, ys, tk, x1, g2, b2):
    t = x1.shape[0]
    tile = min(ROW_TILE, t)
    n = t // tile
    return pl.pallas_call(
        _combine_kernel, out_shape=jax.ShapeDtypeStruct((t, D_MODEL), F32), grid=(n,),
        in_specs=[pl.BlockSpec((tile * TOP_K,), lambda i: (i,), memory_space=pltpu.SMEM),
                  pl.BlockSpec((tile * TOP_K,), lambda i: (jnp.minimum(i + 1, n - 1),), memory_space=pltpu.SMEM),
                  pl.BlockSpec(memory_space=pl.ANY),
                  pl.BlockSpec((tile, LANES), lambda i: (i, 0)),
                  pl.BlockSpec((tile, D_MODEL), lambda i: (i, 0)),
                  pl.BlockSpec((1, D_MODEL), lambda i: (0, 0)),
                  pl.BlockSpec((1, D_MODEL), lambda i: (0, 0))],
        out_specs=pl.BlockSpec((tile, D_MODEL), lambda i: (i, 0)),
        scratch_shapes=[pltpu.VMEM((2, TOP_K, tile // 8, YS_TILE, 8, LANES), U32), pltpu.SemaphoreType.DMA((2,))],
        compiler_params=_cparams("arbitrary"), name="moe_combine",
    )(dest_flat, dest_flat, ys, tk, x1, g2, b2)


def _rel_bucket(dist):
    exact = REL_BUCKETS // 2
    d = np.maximum(dist, 0)
    log_b = exact + (np.log(np.maximum(d, 1).astype(np.float32) / np.float32(exact))
                     / np.float32(math.log(REL_MAX_DIST / exact)) * np.float32(REL_BUCKETS - exact)).astype(np.int32)
    return np.where(d < exact, d, np.minimum(log_b, REL_BUCKETS - 1)).astype(np.int32)


def _bias_lookup(table, bucket, valid):
    bucket = jnp.asarray(bucket)[None]
    acc = jnp.zeros((table.shape[1],) + bucket.shape[1:], F32)
    for b in range(REL_BUCKETS):
        acc = jnp.where(bucket == b, table[b].reshape((-1,) + (1,) * (bucket.ndim - 1)), acc)
    return jnp.where(jnp.asarray(valid)[None], acc, NEG)


def _bias_tables(rel_bias):
    table = rel_bias.astype(F32)
    r = np.arange(WINDOW)[:, None]
    c = np.arange(2 * WINDOW)[None, :]
    dist = r + WINDOW - c
    valid = (dist >= 0) & (dist < WINDOW)
    dist0 = np.where(c < N_META, N_META + r - c, dist)
    valid0 = np.where(c < N_META, dist0 < WINDOW, (c >= WINDOW) & valid)
    both = jnp.stack([_bias_lookup(table, _rel_bucket(dist0), valid0), _bias_lookup(table, _rel_bucket(dist), valid)])
    dist_s = WINDOW - 1 - np.arange(WINDOW)
    rows = _bias_lookup(table[:, np.asarray(HEAD_ORDER)], _rel_bucket(dist_s), np.ones_like(dist_s, bool))
    return both, rows


def _perm_heads(a, axis):
    assert HEAD_ORDER == tuple(kv * A_GROUP + g for g in range(A_GROUP) for kv in range(A_KV_HEADS))
    shape = a.shape
    a = a.reshape(shape[:axis] + (A_KV_HEADS, A_GROUP, A_HD) + shape[axis + 1:])
    return jnp.swapaxes(a, axis, axis + 1).reshape(shape)


def _rep_rows(vec, rows=8):
    out = jnp.zeros((rows, LANES), F32)
    return out.at[:vec.shape[0], :].set(jnp.broadcast_to(vec.astype(F32)[:, None], (vec.shape[0], LANES)))


def kernel(x_prompt, x_sample, cache_swa_k, cache_swa_v, state_mlstm_C, state_mlstm_n, state_mlstm_m, meta_tokens, rel_bias, w_in, b_igate, b_fgate, attn_sinks, g_mlstm_out, g_attn_out, w_out, ln1_g, ln1_b, w_router, b_router, w_moe1, b_moe1, w_moe2, b_moe2, ln2_g, ln2_b):
    B, S, _ = x_prompt.shape
    NB = x_sample.shape[0]
    assert x_sample.shape[1] == 1 and w_in.shape[0] == 1
    assert S % PROJ_TILE == 0 and S % M_CHUNK == 0 and S % WINDOW == 0 and NB % SAMPLE_GROUP == 0
    l = 0

    assert IN_WIDTHS == (512, 512, 512, 512, 4, 4, 512, 128, 128)
    bf = lambda a: a.astype(BF16)
    w = w_in[l]
    n_main, n_gate = 4 * M_WIDTH, 2 * M_HEADS
    w_gate = w[:, n_main:n_main + n_gate]
    w_att = w[:, n_main + n_gate:]
    assert math.frexp(A_HD ** -0.5)[0] == 0.5
    w_qa = _perm_heads(w_att[:, :A_WIDTH], 1) * (A_HD ** -0.5)
    wr = bf(jnp.concatenate([w[:, :n_main], w_qa, w_att[:, A_WIDTH:],
                             jnp.pad(w_gate, ((0, 0), (0, LANES - n_gate)))], axis=1))
    wt = bf(jnp.concatenate([w[:, :M_WIDTH], w[:, 2 * M_WIDTH:3 * M_WIDTH], w_qa, w_att[:, A_WIDTH + LANES:],
                             w_gate], axis=1).T)
    b_gate = jnp.concatenate([b_igate[l], b_fgate[l]]).astype(F32)
    brow = jnp.pad(b_gate, (0, LANES - n_gate))[None, :]
    bcol = b_gate[:, None]
    plan_p = ((512, 512, "plain", BF16), (1536, 512, "plain", F32), (2560, 128, "plain", BF16),
              (2816, 128, "gate", F32))
    tplan_p = ((0, 512, "plain", BF16), (512, 512, "plain", BF16), (1024, 512, "plain", BF16),
               (1536, 128, "plain", BF16), (1664, 8, "gate", F32))
    plan_s = ((0, 512, "plain", F32), (512, 512, "plain", F32), (1024, 512, "plain", F32), (1536, 512, "plain", F32),
              (2048, 512, "plain", F32), (2560, 128, "plain", F32), (2688, 128, "plain", F32), (2816, 128, "gate", F32))

    bias_tab, bias_rows = _bias_tables(rel_bias)
    sinks = _rep_rows(attn_sinks[l])
    sinks_step = _rep_rows(attn_sinks[l][np.asarray(HEAD_ORDER)])
    g_m = g_mlstm_out[l].astype(F32)[None, :]
    g_a = _perm_heads(g_attn_out[l].astype(F32), 0)[None, :]
    wo = bf(jnp.concatenate([w_out[l][:M_WIDTH], _perm_heads(w_out[l][M_WIDTH:], 0)], axis=0))
    g1, b1 = ln1_g[l].astype(F32)[None, :], ln1_b[l].astype(F32)[None, :]
    g2, b2 = ln2_g[l].astype(F32)[None, :], ln2_b[l].astype(F32)[None, :]
    w_r = bf(jnp.pad(w_router[l], ((0, 0), (0, LANES - N_EXPERTS))))
    b_r = jnp.pad(b_router[l].astype(F32), (0, LANES - N_EXPERTS), constant_values=NEG)[None, :]
    b1g = b_moe1[l][:, 0::2].astype(F32)[:, None, :]
    b1l = b_moe1[l][:, 1::2].astype(F32)[:, None, :]
    b2e = b_moe2[l].astype(F32)[:, None, :]
    pj = np.zeros((256, 256), np.float32)
    pj[2 * np.arange(128), np.arange(128)] = 1.0
    pj[2 * np.arange(128) + 1, 128 + np.arange(128)] = 1.0
    perm = jnp.asarray(pj, BF16)

    xp2 = x_prompt.reshape(B * S, D_MODEL)
    km, om, ka, gc, qt, vt, qat, vat, gr, kv_tail = _proj(
        xp2, wr, wt, brow, bcol, plan_p, tplan_p, (2560, 256), PROJ_TILE, S, "proj_prompt")
    x_meta = jnp.pad(meta_tokens.astype(F32), ((0, M_CHUNK - N_META), (0, 0)))
    km0, _, ka0, gc0, qt0, vt0, _, vat0, gr0 = _proj(
        x_meta, wr, wt, brow, bcol, plan_p, tplan_p, None, M_CHUNK, M_CHUNK, "proj_meta")
    xs2 = x_sample.reshape(NB, D_MODEL)
    qm_s, km_s, vm_s, om_s, qa_s, ka_s, va_s, gc_s = _proj(
        xs2, wr, wt, brow, bcol, plan_s, (), None, NB, NB, "proj_sample")

    zero_c = jnp.zeros((M_HEADS, M_DV + 8, M_DK), F32)
    zero_m = jnp.zeros((8, LANES), F32)
    _, c_meta, m_meta = _mlstm(qt0, km0, vt0, gc0, gr0, zero_c, zero_m, 1, N_META, "mlstm_meta")
    h_p, c_p, m_p = _mlstm(qt, km, vt, gc, gr, c_meta[0], m_meta[0], B, M_CHUNK, "mlstm_prompt")
    C_p = c_p[:, :, :M_DV, :]
    n_p = c_p[:, :, M_DV, :]
    m_prompt = m_p[:, :M_HEADS, 0]
    m_pad = jnp.pad(state_mlstm_m[l].astype(F32), ((0, 0), (0, LANES - M_HEADS)))
    C_s, n_s, m_s, h_s = _mlstm_step(state_mlstm_C[l].astype(F32), state_mlstm_n[l].astype(F32), m_pad,
                                     gc_s, qm_s, km_s, vm_s)

    att_p = _swa(qat, ka, vat, ka0, vat0, jnp.swapaxes(bias_tab, 2, 3), sinks, B)
    ck = cache_swa_k[l].reshape(NB, WINDOW, LANES)
    cv = cache_swa_v[l].reshape(NB, WINDOW, LANES)
    k_new, v_new, att_s = _swa_step(ck, cv, qa_s, ka_s, va_s, bias_rows, sinks_step)

    x1_p, xpk_p, tk_p, cnt_p = _merge(h_p, om, att_p, xp2, g_m, g_a, wo, g1, b1, w_r, b_r, MERGE_TILE, "merge_prompt")
    x1_s, xpk_s, tk_s, cnt_s = _merge(h_s, om_s, att_s, xs2, g_m, g_a, wo, g1, b1, w_r, b_r, NB, "merge_sample")

    T_p = B * S
    assert T_p % RANK_TILE == 0 and T_p % ROW_TILE == 0
    n_blocks = -(-((T_p + NB) * TOP_K) // EXPERT_TILE) + N_EXPERTS
    off, be2, nu2, pad = _offsets(cnt_p + cnt_s, n_blocks, EXPERT_TILE)
    pads = jnp.concatenate([pad[0, :N_EXPERTS], pad[1, :N_EXPERTS], nu2[0, :1]])
    dest_p = _route(tk_p, off)[:, :TOP_K].reshape(-1)
    dest_s = _route(tk_s, off + cnt_p)[:, :TOP_K].reshape(-1)
    be = be2.reshape(-1)[:n_blocks]
    nu = nu2[0, :1]
    xs = _dispatch(dest_p, pads, dest_s, xpk_p, xpk_s, n_blocks * EXPERT_TILE, EXPERT_TILE)
    ys = _experts(be, nu, xs, w_moe1[l], b1g, b1l, w_moe2[l], b2e, perm, EXPERT_TILE)
    y_p = _combine(dest_p, ys, tk_p, x1_p, g2, b2)
    y_s = _combine(dest_s, ys, tk_s, x1_s, g2, b2)

    kv_tail = kv_tail.reshape(B, WINDOW, 2, A_KV_HEADS, A_HD)
    dt_k, dt_v = cache_swa_k.dtype, cache_swa_v.dtype
    return (y_p.reshape(B, S, D_MODEL).astype(x_prompt.dtype), y_s.reshape(NB, 1, D_MODEL).astype(x_sample.dtype),
            kv_tail[:, :, 0][None].astype(dt_k), kv_tail[:, :, 1][None].astype(dt_v),
            C_p[None].astype(state_mlstm_C.dtype), n_p[None].astype(state_mlstm_n.dtype),
            m_prompt[None].astype(state_mlstm_m.dtype),
            k_new.reshape(1, NB, WINDOW, A_KV_HEADS, A_HD).astype(dt_k),
            v_new.reshape(1, NB, WINDOW, A_KV_HEADS, A_HD).astype(dt_v),
            C_s[None].astype(state_mlstm_C.dtype), n_s[None].astype(state_mlstm_n.dtype),
            m_s[:, :M_HEADS][None].astype(state_mlstm_m.dtype))
```

```python
import functools
import math

import numpy as np
import jax
import jax.numpy as jnp
from jax import lax
from jax.experimental import pallas as pl
from jax.experimental.pallas import tpu as pltpu

F32 = jnp.float32
BF16 = jnp.bfloat16
I32 = jnp.int32
U32 = jnp.uint32

D_MODEL = 1024
N_META = 16
M_HEADS = 4
M_DK = 128
M_DV = 128
M_WIDTH = M_HEADS * M_DV
A_HD = 64
A_HEADS = 8
A_KV_HEADS = 2
A_GROUP = A_HEADS // A_KV_HEADS
A_WIDTH = A_HEADS * A_HD
WINDOW = 128
REL_BUCKETS = 32
REL_MAX_DIST = 128
N_EXPERTS = 32
TOP_K = 4
D_FF = D_MODEL
SWIGLU_LIMIT = 7.0
SWIGLU_ALPHA = 1.702
DEPTH = 1
DN_ALPHA = (2.0 * DEPTH) ** 0.25
LN_EPS = 1e-5
IN_WIDTHS = (M_WIDTH, M_WIDTH, M_WIDTH, M_WIDTH, M_HEADS, M_HEADS, A_WIDTH, A_KV_HEADS * A_HD, A_KV_HEADS * A_HD)

LANES = 128
NEG = -1e30
VMEM_LIMIT = 56 * 1024 * 1024

M_CHUNK = 256
PROJ_TILE = 1024
MERGE_TILE = 1024
RANK_TILE = 1024
ROW_TILE = 512
EXPERT_TILE = 512
SAMPLE_GROUP = 8
ISSUE_GROUP = 8
SWA_QBLOCKS = 8
MLSTM_SEQS = 4
STEP_PIPE_LAG = 3
SWA_PIPE_LAG = 2
XP_TILE = D_MODEL // 2 // LANES
YS_TILE = D_MODEL // 2 // LANES
HEAD_ORDER = (0, 4, 1, 5, 2, 6, 3, 7)


def _cparams(*sem):
    return pltpu.CompilerParams(dimension_semantics=sem, vmem_limit_bytes=VMEM_LIMIT)


def _log_sigmoid(x):
    return jnp.minimum(x, 0.0) - jnp.log1p(jnp.exp(-jnp.abs(x)))


def _sigmoid(x):
    return 1.0 / (1.0 + jnp.exp(-x))


def _proj_kernel(x_ref, wr_ref, wt_ref, brow_ref, bcol_ref, *outs, row_plan, t_plan, tail_cols):
    xb = x_ref[...].astype(BF16)
    tm = xb.shape[0]
    o = 0
    for (c0, width, kind, _) in row_plan:
        r = jnp.dot(xb, wr_ref[:, c0:c0 + width], preferred_element_type=F32)
        if kind == "gate":
            r = r + brow_ref[...]
            lane = lax.broadcasted_iota(I32, r.shape, 1)
            r = jnp.where(lane < M_HEADS, r, _log_sigmoid(r))
        outs[o][...] = r.astype(outs[o].dtype)
        o += 1
    for (r0, nrows, kind, _) in t_plan:
        r = lax.dot_general(wt_ref[r0:r0 + nrows, :], xb, (((1,), (1,)), ((), ())), preferred_element_type=F32)
        if kind == "gate":
            r = r + bcol_ref[...]
            row = lax.broadcasted_iota(I32, r.shape, 0)
            r = jnp.where(row < M_HEADS, r, _log_sigmoid(r))
        outs[o][...] = r.astype(outs[o].dtype)
        o += 1
    if tail_cols is not None:
        c0, width = tail_cols
        outs[o][...] = jnp.dot(xb[tm - WINDOW:, :], wr_ref[:, c0:c0 + width], preferred_element_type=F32)


def _proj(x, wr, wt, brow, bcol, row_plan, t_plan, tail_cols, tile, rows_per_group, name):
    t = x.shape[0]
    nt = t // tile
    out_shape, out_specs = [], []
    for (_, width, _, dt) in row_plan:
        out_shape.append(jax.ShapeDtypeStruct((t, width), dt))
        out_specs.append(pl.BlockSpec((tile, width), lambda i: (i, 0)))
    for (_, nrows, _, dt) in t_plan:
        out_shape.append(jax.ShapeDtypeStruct((nrows, t), dt))
        out_specs.append(pl.BlockSpec((nrows, tile), lambda i: (0, i)))
    if tail_cols is not None:
        tiles_per_group = rows_per_group // tile
        out_shape.append(jax.ShapeDtypeStruct((t // rows_per_group * WINDOW, tail_cols[1]), F32))
        out_specs.append(pl.BlockSpec((WINDOW, tail_cols[1]), lambda i: (i // tiles_per_group, 0)))
    kern = functools.partial(_proj_kernel, row_plan=row_plan, t_plan=t_plan, tail_cols=tail_cols)
    return pl.pallas_call(
        kern, out_shape=out_shape, grid=(nt,),
        in_specs=[pl.BlockSpec((tile, D_MODEL), lambda i: (i, 0)),
                  pl.BlockSpec(wr.shape, lambda i: (0, 0)),
                  pl.BlockSpec(wt.shape, lambda i: (0, 0)),
                  pl.BlockSpec(brow.shape, lambda i: (0, 0)),
                  pl.BlockSpec(bcol.shape, lambda i: (0, 0))],
        out_specs=out_specs, compiler_params=_cparams("arbitrary"), name=name,
    )(x, wr, wt, brow, bcol)


def _split3(a):
    hi = a.astype(BF16)
    r1 = a - hi.astype(F32)
    mid = r1.astype(BF16)
    lo = (r1 - mid.astype(F32)).astype(BF16)
    return hi, mid, lo


def _mlstm_kernel(*refs, n_valid, nseq):
    seq_in = [refs[5 * i:5 * i + 5] for i in range(nseq)]
    c0_ref, m0_ref = refs[5 * nseq:5 * nseq + 2]
    h_ref, c_out_ref, m_out_ref, c_scr, m_scr = refs[5 * nseq + 2:]
    c = pl.program_id(1)
    nc = pl.num_programs(1)
    L = seq_in[0][1].shape[0]

    @pl.when(c == 0)
    def _():
        for i in range(nseq):
            c_scr[i] = c0_ref[...]
            m_scr[i] = m0_ref[...]

    r_i = lax.broadcasted_iota(I32, (L, L), 0)
    c_i = lax.broadcasted_iota(I32, (L, L), 1)
    upper = r_i <= c_i
    tril = jnp.where(c_i <= r_i, 1.0, 0.0).astype(BF16)
    triu = jnp.where(upper, 1.0, 0.0).astype(BF16)
    scale = M_DK ** -0.5
    ones_rows = jnp.where(lax.broadcasted_iota(I32, (8, L), 0) == 0, 1.0, 0.0).astype(BF16)

    gates = []
    for (_, _, _, gc_ref, gr_ref) in seq_in:
        gc = gc_ref[...]
        gr = gr_ref[...]
        if n_valid < L:
            rowc = lax.broadcasted_iota(I32, gc.shape, 0)
            lanec = lax.broadcasted_iota(I32, gc.shape, 1)
            gc = jnp.where(rowc < n_valid, gc, jnp.where(lanec < M_HEADS, NEG, 0.0))
            rowr = lax.broadcasted_iota(I32, gr.shape, 0)
            colr = lax.broadcasted_iota(I32, gr.shape, 1)
            gr = jnp.where(colr < n_valid, gr, jnp.where(rowr < M_HEADS, NEG, 0.0))
        b_cols = sum(jnp.dot(tril, part, preferred_element_type=F32) for part in _split3(gc))
        b_rows = sum(jnp.dot(part, triu, preferred_element_type=F32) for part in _split3(gr))
        gates.append((gc, gr, b_cols, b_rows))

    m_alls = [m_scr[i] for i in range(nseq)]
    c_alls = [[c_scr[i, h] for h in range(M_HEADS)] for i in range(nseq)]
    h_new, c_new, m_new_all = {}, {}, {}

    def operands(u):
        i, h = u
        qt_ref, k_ref, vt_ref = seq_in[i][:3]
        sl = slice(h * M_DK, (h + 1) * M_DK)
        return qt_ref[sl, :], k_ref[:, sl], jnp.concatenate([vt_ref[sl, :], ones_rows], axis=0)

    def stage_a(u):
        i, h = u
        gc, gr, b_cols, b_rows = gates[i]
        qt, k, vt_aug = operands(u)
        ig_r = gr[h:h + 1, :]
        b_r = b_rows[M_HEADS + h:M_HEADS + h + 1, :]
        m_prev = m_alls[i][h:h + 1, 0:1]
        cs = c_alls[i][h]
        qk = jnp.dot(k, qt, preferred_element_type=F32)
        inter = jnp.dot(cs.astype(BF16), qt, preferred_element_type=F32)
        b_last = b_r[:, L - 1:L]
        g = ig_r + b_last - b_r
        m_new = jnp.maximum(b_last + m_prev, jnp.max(g, axis=1, keepdims=True))
        a = jnp.exp(b_last + m_prev - m_new)
        wv = (vt_aug.astype(F32) * jnp.exp(g - m_new)).astype(BF16)
        c_new[u] = a * cs + jnp.dot(wv, k, preferred_element_type=F32) * scale
        m_new_all[u] = jnp.broadcast_to(m_new, (1, LANES))
        return qk, inter

    def stage_b(u, qk, inter):
        i, h = u
        gc, gr, b_cols, b_rows = gates[i]
        b_r = b_rows[M_HEADS + h:M_HEADS + h + 1, :]
        m_prev = m_alls[i][h:h + 1, 0:1]
        r_c = gc[:, h:h + 1] - b_cols[:, M_HEADS + h:M_HEADS + h + 1]
        dt = jnp.where(upper, b_r + r_c, NEG)
        m_t = jnp.maximum(b_r + m_prev, jnp.max(dt, axis=0, keepdims=True))
        st = (qk * (scale * jnp.exp(dt - m_t))).astype(BF16)
        return st, jnp.exp(b_r + m_prev - m_t) * inter, jnp.exp(-m_t)

    def stage_c(u, st, inter_w, floor):
        _, _, vt_aug = operands(u)
        nd = inter_w + jnp.dot(vt_aug, st, preferred_element_type=F32)
        den = nd[M_DV:M_DV + 1, :]
        h_new[u] = (nd[:M_DV, :] / jnp.maximum(jnp.abs(den), floor)).T

    units = [(i, h) for h in range(M_HEADS) for i in range(nseq)]
    a_q, b_q = {}, {}
    for n in range(len(units) + 2):
        if n < len(units):
            a_q[n] = stage_a(units[n])
        if 0 <= n - 1 < len(units):
            b_q[n - 1] = stage_b(units[n - 1], *a_q.pop(n - 1))
        if 0 <= n - 2 < len(units):
            stage_c(units[n - 2], *b_q.pop(n - 2))

    for i in range(nseq):
        h_ref[0, i] = jnp.concatenate([h_new[(i, h)] for h in range(M_HEADS)], axis=1)
        for h in range(M_HEADS):
            c_scr[i, h] = c_new[(i, h)]
        m_scr[i, 0:M_HEADS, :] = jnp.concatenate([m_new_all[(i, h)] for h in range(M_HEADS)], axis=0)

    @pl.when(c == nc - 1)
    def _():
        c_out_ref[...] = c_scr[...]
        m_out_ref[...] = m_scr[...]


def _mlstm(qt, km, vt, gc, gr, c0, m0, batch, n_valid, name):
    L = M_CHUNK
    nc = km.shape[0] // (batch * L)
    nseq = MLSTM_SEQS if batch % MLSTM_SEQS == 0 else 1
    kern = functools.partial(_mlstm_kernel, n_valid=n_valid, nseq=nseq)
    in_specs, operands = [], []
    for i in range(nseq):
        blk = functools.partial(lambda b, c, i: (b * nseq + i) * nc + c, i=i)
        rows = pl.BlockSpec((L, M_WIDTH), functools.partial(lambda b, c, blk: (blk(b, c), 0), blk=blk))
        cols = pl.BlockSpec((M_WIDTH, L), functools.partial(lambda b, c, blk: (0, blk(b, c)), blk=blk))
        in_specs += [cols, rows, cols,
                     pl.BlockSpec((L, LANES), functools.partial(lambda b, c, blk: (blk(b, c), 0), blk=blk)),
                     pl.BlockSpec((8, L), functools.partial(lambda b, c, blk: (0, blk(b, c)), blk=blk))]
        operands += [qt, km, vt, gc, gr]
    in_specs += [pl.BlockSpec((M_HEADS, M_DV + 8, M_DK), lambda b, c: (0, 0, 0)),
                 pl.BlockSpec((8, LANES), lambda b, c: (0, 0))]
    h4, c_fin, m_fin = pl.pallas_call(
        kern,
        out_shape=[jax.ShapeDtypeStruct((batch // nseq, nseq, nc * L, M_WIDTH), F32),
                   jax.ShapeDtypeStruct((batch, M_HEADS, M_DV + 8, M_DK), F32),
                   jax.ShapeDtypeStruct((batch, 8, LANES), F32)],
        grid=(batch // nseq, nc),
        in_specs=in_specs,
        out_specs=[pl.BlockSpec((1, nseq, L, M_WIDTH), lambda b, c: (b, 0, c, 0)),
                   pl.BlockSpec((nseq, M_HEADS, M_DV + 8, M_DK), lambda b, c: (b, 0, 0, 0)),
                   pl.BlockSpec((nseq, 8, LANES), lambda b, c: (b, 0, 0))],
        scratch_shapes=[pltpu.VMEM((nseq, M_HEADS, M_DV + 8, M_DK), F32), pltpu.VMEM((nseq, 8, LANES), F32)],
        compiler_params=_cparams("arbitrary", "arbitrary"), name=name,
    )(*operands, c0, m0)
    return h4.reshape(batch * nc * L, M_WIDTH), c_fin, m_fin


def _outer_f32(a, b):
    ah, am, al = (t.astype(F32) for t in _split3(a))
    bh, bm, bl = (t.astype(F32) for t in _split3(b))
    z = jnp.zeros_like(ah)
    lhs = jnp.concatenate([ah, ah, ah, am, am, al, z, z], axis=0).astype(BF16)
    rhs = jnp.concatenate([bh, bm, bl, bh, bm, bh, z, z], axis=0).astype(BF16)
    return lax.dot_general(lhs, rhs, (((0,), (0,)), ((), ())), preferred_element_type=F32)


def _mlstm_step_kernel(c_ref, n_ref, m_ref, gc_ref, q_ref, k_ref, v_ref,
                       c_out_ref, n_out_ref, m_out_ref, h_ref):
    g = c_ref.shape[0]
    assert g == 8
    scale = M_DK ** -0.5
    ig = gc_ref[:, 0:M_HEADS]
    lf = gc_ref[:, M_HEADS:2 * M_HEADS]
    m = m_ref[:, 0:M_HEADS]
    m_t = jnp.maximum(lf + m, ig)
    w = jnp.exp(lf + m - m_t)
    wg = jnp.exp(ig - m_t)
    floor = jnp.exp(-m_t)
    m_out_ref[...] = jnp.zeros_like(m_out_ref)
    m_out_ref[:, 0:M_HEADS] = m_t
    row8 = lax.broadcasted_iota(I32, (g, M_DV), 0)

    per_head = []
    for h in range(M_HEADS):
        sl = slice(h * M_DK, (h + 1) * M_DK)
        q = q_ref[:, sl]
        k = k_ref[:, sl] * scale
        v = v_ref[:, sl]
        n = n_ref[:, h, :]
        w_h, wg_h = w[:, h:h + 1], wg[:, h:h + 1]
        s = jnp.sum(q * k, axis=1, keepdims=True) * wg_h
        den = w_h * jnp.sum(n * q, axis=1, keepdims=True) + s
        n_out_ref[:, h, :] = w_h * n + wg_h * k
        per_head.append((q.astype(BF16), k, wg_h * v, w_h, s * v, 1.0 / jnp.maximum(jnp.abs(den), floor[:, h:h + 1])))

    def stage_a(h, j):
        qb = per_head[h][0]
        r = lax.dot_general(qb, c_ref[j, h].astype(BF16), (((1,), (1,)), ((), ())), preferred_element_type=F32)
        return jnp.where(row8 == j, r, 0.0)

    def stage_b(h, j):
        _, k, wv, w_h, _, _ = per_head[h]
        c_out_ref[j, h] = w_h[j:j + 1, :] * c_ref[j, h] + _outer_f32(wv[j:j + 1, :], k[j:j + 1, :])

    units = [(h, j) for h in range(M_HEADS) for j in range(g)]
    lag = STEP_PIPE_LAG
    cq = [jnp.zeros((g, M_DV), F32) for _ in range(M_HEADS)]
    for i in range(len(units) + lag):
        if i < len(units):
            cq[units[i][0]] = cq[units[i][0]] + stage_a(*units[i])
        if 0 <= i - lag < len(units):
            stage_b(*units[i - lag])
    for h in range(M_HEADS):
        _, _, _, w_h, sv, inv = per_head[h]
        h_ref[:, h * M_DK:(h + 1) * M_DK] = (w_h * cq[h] + sv) * inv


def _mlstm_step(c, n, m_pad, gc, q, k, v):
    nb = c.shape[0]
    g = SAMPLE_GROUP
    row = lambda w: pl.BlockSpec((g, w), lambda i: (i, 0))
    return pl.pallas_call(
        _mlstm_step_kernel,
        out_shape=[jax.ShapeDtypeStruct(c.shape, F32), jax.ShapeDtypeStruct(n.shape, F32),
                   jax.ShapeDtypeStruct((nb, LANES), F32), jax.ShapeDtypeStruct((nb, M_WIDTH), F32)],
        grid=(nb // g,),
        in_specs=[pl.BlockSpec((g, M_HEADS, M_DV, M_DK), lambda i: (i, 0, 0, 0)),
                  pl.BlockSpec((g, M_HEADS, M_DK), lambda i: (i, 0, 0)),
                  row(LANES), row(LANES), row(M_WIDTH), row(M_WIDTH), row(M_WIDTH)],
        out_specs=[pl.BlockSpec((g, M_HEADS, M_DV, M_DK), lambda i: (i, 0, 0, 0)),
                   pl.BlockSpec((g, M_HEADS, M_DK), lambda i: (i, 0, 0)),
                   row(LANES), row(M_WIDTH)],
        compiler_params=_cparams("arbitrary"), name="mlstm_step",
    )(c, n, m_pad, gc, q, k, v)


def _swa_kernel(qt_ref, kc_ref, kp_ref, vtc_ref, vtp_ref, km_ref, vtm_ref, bias_ref, sink_ref, o_ref):
    j = pl.program_id(1)
    first = j == 0
    blk = WINDOW
    nqb = qt_ref.shape[1] // blk
    kp = jnp.where(first, km_ref[...], kp_ref[...])
    vtp = jnp.where(first, vtm_ref[...], vtp_ref[...])
    k = jnp.concatenate([kp, kc_ref[...]], axis=0)
    vt = jnp.concatenate([vtp, vtc_ref[...]], axis=1)
    row_v = lax.broadcasted_iota(I32, vt.shape, 0)
    zero_v = jnp.zeros_like(vt)
    vt_half = (jnp.where(row_v < A_HD, vt, zero_v), jnp.where(row_v >= A_HD, vt, zero_v))
    row_q = lax.broadcasted_iota(I32, (LANES, blk), 0)
    lo_rows = row_q < A_HD
    def scores(u, p):
        cols = slice(u * blk, (u + 1) * blk)
        keys = slice(u * blk, (u + 2) * blk)
        qs = qt_ref[p * LANES:(p + 1) * LANES, cols]
        zero_q = jnp.zeros_like(qs)
        q_own = (jnp.where(lo_rows, qs, zero_q), jnp.where(lo_rows, zero_q, qs))
        return [jnp.dot(k[keys], q_own[half], preferred_element_type=F32) for half in range(2)]

    def softmax(u, p, s2):
        table = jnp.where(first, 0, 1) if u == 0 else 1
        probs, inv = [], []
        for half in range(2):
            hd = HEAD_ORDER[2 * p + half]
            s = s2[half] + bias_ref[table, hd]
            sk = sink_ref[hd:hd + 1, 0:1]
            m = jnp.maximum(jnp.max(s, axis=0, keepdims=True), sk)
            e = jnp.exp(s - m)
            probs.append(e.astype(BF16))
            inv.append(1.0 / (jnp.sum(e, axis=0, keepdims=True) + jnp.exp(sk - m)))
        return jnp.concatenate(probs, axis=0), jnp.where(lo_rows, inv[0], inv[1])

    def values(u, p, probs, inv):
        keys = slice(u * blk, (u + 2) * blk)
        vt_stack = jnp.concatenate([vt_half[0][:, keys], vt_half[1][:, keys]], axis=1)
        ot = jnp.dot(vt_stack, probs, preferred_element_type=F32)
        o_ref[u * blk:(u + 1) * blk, p * LANES:(p + 1) * LANES] = (ot * inv).T

    units = [(u, p) for u in range(nqb) for p in range(A_GROUP)]
    s_q, p_q = {}, {}
    lag = SWA_PIPE_LAG
    for i in range(len(units) + 2 * lag):
        if i < len(units):
            s_q[i] = scores(*units[i])
        if 0 <= i - lag < len(units):
            p_q[i - lag] = softmax(*units[i - lag], s_q.pop(i - lag))
        if 0 <= i - 2 * lag < len(units):
            values(*units[i - 2 * lag], *p_q.pop(i - 2 * lag))


def _swa(qat, ka, vat, kmeta, vtmeta, bias_t, sinks, batch):
    blk = WINDOW
    nqb = SWA_QBLOCKS
    t = ka.shape[0]
    nq = t // (batch * blk * nqb)
    prev = lambda b, j: (b * nq + j) * nqb + jnp.where(j == 0, 0, -1)
    const2 = lambda shape: pl.BlockSpec(shape, lambda b, j: (0, 0))
    return pl.pallas_call(
        _swa_kernel, out_shape=jax.ShapeDtypeStruct((t, A_WIDTH), F32), grid=(batch, nq),
        in_specs=[pl.BlockSpec((A_WIDTH, nqb * blk), lambda b, j: (0, b * nq + j)),
                  pl.BlockSpec((nqb * blk, LANES), lambda b, j: (b * nq + j, 0)),
                  pl.BlockSpec((blk, LANES), lambda b, j: (prev(b, j), 0)),
                  pl.BlockSpec((LANES, nqb * blk), lambda b, j: (0, b * nq + j)),
                  pl.BlockSpec((LANES, blk), lambda b, j: (0, prev(b, j))),
                  const2((blk, LANES)), const2((LANES, blk)),
                  pl.BlockSpec(bias_t.shape, lambda b, j: (0, 0, 0, 0)),
                  const2((8, LANES))],
        out_specs=pl.BlockSpec((nqb * blk, A_WIDTH), lambda b, j: (b * nq + j, 0)),
        compiler_params=_cparams("arbitrary", "arbitrary"), name="swa_prompt",
    )(qat, ka, ka, vat, vat, kmeta, vtmeta, bias_t, sinks)


def _swa_step_kernel(ck_ref, cv_ref, q_ref, k_ref, v_ref, bias_ref, sink_ref, ko_ref, vo_ref, o_ref):
    g = ck_ref.shape[0]
    lane = lax.broadcasted_iota(I32, (A_HEADS, LANES), 1)
    row = lax.broadcasted_iota(I32, (A_HEADS, LANES), 0)
    own_half = (row % 2 == 0) == (lane < A_HD)
    bias = bias_ref[...]
    sk = sink_ref[:, 0:1]
    def stage_a(j):
        ko_ref[j, 0:WINDOW - 1, :] = ck_ref[j, 1:WINDOW, :]
        ko_ref[j, WINDOW - 1:WINDOW, :] = k_ref[j:j + 1, :]
        vo_ref[j, 0:WINDOW - 1, :] = cv_ref[j, 1:WINDOW, :]
        vo_ref[j, WINDOW - 1:WINDOW, :] = v_ref[j:j + 1, :]
        kk = ko_ref[j].astype(BF16)
        slabs = [q_ref[j:j + 1, p * LANES:(p + 1) * LANES] for p in range(A_GROUP)]
        q8 = jnp.concatenate([slabs[r // 2] for r in range(A_HEADS)], axis=0)
        q8 = jnp.where(own_half, q8, 0.0).astype(BF16)
        s = lax.dot_general(q8, kk, (((1,), (1,)), ((), ())), preferred_element_type=F32)
        s = s + bias
        m = jnp.maximum(jnp.max(s, axis=1, keepdims=True), sk)
        e = jnp.exp(s - m)
        return e.astype(BF16), 1.0 / (jnp.sum(e, axis=1, keepdims=True) + jnp.exp(sk - m))

    def stage_b(j, p8, inv):
        vv = vo_ref[j].astype(BF16)
        o8 = jnp.where(own_half, jnp.dot(p8, vv, preferred_element_type=F32) * inv, 0.0)
        for p in range(A_GROUP):
            o_ref[j:j + 1, p * LANES:(p + 1) * LANES] = o8[2 * p:2 * p + 1, :] + o8[2 * p + 1:2 * p + 2, :]

    lag, pending = STEP_PIPE_LAG, {}
    for i in range(g + lag):
        if i < g:
            pending[i] = stage_a(i)
        if 0 <= i - lag < g:
            stage_b(i - lag, *pending.pop(i - lag))


def _swa_step(ck, cv, q, k, v, bias_rows, sinks):
    nb = ck.shape[0]
    g = SAMPLE_GROUP
    cache = pl.BlockSpec((g, WINDOW, LANES), lambda i: (i, 0, 0))
    row = lambda w: pl.BlockSpec((g, w), lambda i: (i, 0))
    const = lambda a: pl.BlockSpec(a.shape, lambda i: (0, 0))
    return pl.pallas_call(
        _swa_step_kernel,
        out_shape=[jax.ShapeDtypeStruct(ck.shape, F32), jax.ShapeDtypeStruct(cv.shape, F32),
                   jax.ShapeDtypeStruct((nb, A_WIDTH), F32)],
        grid=(nb // g,),
        in_specs=[cache, cache, row(A_WIDTH), row(LANES), row(LANES), const(bias_rows), const(sinks)],
        out_specs=[cache, cache, row(A_WIDTH)],
        compiler_params=_cparams("arbitrary"), name="swa_step",
    )(ck, cv, q, k, v, bias_rows, sinks)


def _layer_norm(z, g, b):
    mu = jnp.mean(z, axis=1, keepdims=True)
    zc = z - mu
    var = jnp.mean(zc * zc, axis=1, keepdims=True)
    return zc * lax.rsqrt(var + LN_EPS) * g + b


def _pack_halves(x):
    w = x.shape[1] // 2
    lo = pltpu.bitcast(x[:, :w].astype(BF16).astype(F32), U32)
    hi = pltpu.bitcast(x[:, w:].astype(BF16).astype(F32), U32)
    return (lo >> 16) | (hi & jnp.uint32(0xFFFF0000))


def _unpack_halves(words):
    lo = pltpu.bitcast(words << 16, F32).astype(BF16)
    hi = pltpu.bitcast(words & jnp.uint32(0xFFFF0000), F32).astype(BF16)
    return lo, hi


def _to_token_tiles(ref, x):
    for q in range(x.shape[1] // LANES):
        ref[:, q, :] = x[:, q * LANES:(q + 1) * LANES]


def _merge_kernel(h_ref, om_ref, att_ref, x_ref, gm_ref, ga_ref, wo_ref, g1_ref, b1_ref, wr_ref, br_ref,
                  x1_ref, xp_ref, tk_ref, cnt_ref):
    @pl.when(pl.program_id(0) == 0)
    def _():
        cnt_ref[...] = jnp.zeros_like(cnt_ref)

    hm = h_ref[...] * _sigmoid(om_ref[...])
    ym = hm * lax.rsqrt(jnp.mean(hm * hm, axis=1, keepdims=True) + LN_EPS) * gm_ref[...]
    att = att_ref[...]
    ya = att * lax.rsqrt(jnp.mean(att * att, axis=1, keepdims=True) + LN_EPS) * ga_ref[...]
    mix = (jnp.dot(ym.astype(BF16), wo_ref[0:M_WIDTH, :], preferred_element_type=F32)
           + jnp.dot(ya.astype(BF16), wo_ref[M_WIDTH:, :], preferred_element_type=F32))
    x1 = _layer_norm(DN_ALPHA * x_ref[...] + mix, g1_ref[...], b1_ref[...])
    x1_ref[...] = x1
    _to_token_tiles(xp_ref, _pack_halves(x1))
    logits = jnp.dot(x1.astype(BF16), wr_ref[...], preferred_element_type=F32) + br_ref[...]
    lane = lax.broadcasted_iota(I32, logits.shape, 1).astype(F32)
    vals, idxs = [], []
    for _ in range(TOP_K):
        mx = jnp.max(logits, axis=1, keepdims=True)
        idx = jnp.min(jnp.where(logits == mx, lane, float(LANES)), axis=1, keepdims=True)
        vals.append(mx)
        idxs.append(idx)
        logits = jnp.where(lane == idx, 2.0 * NEG, logits)
    es = [jnp.exp(vk - vals[0]) for vk in vals]
    tot = es[0] + es[1] + es[2] + es[3]
    tk = jnp.zeros(logits.shape, F32)
    picked = jnp.zeros(logits.shape, F32)
    for k in range(TOP_K):
        tk = jnp.where(lane == float(k), es[k] / tot, tk)
        tk = jnp.where(lane == float(TOP_K + k), idxs[k], tk)
        picked = jnp.where(lane == idxs[k], 1.0, picked)
    tk_ref[...] = tk
    cnt_ref[...] = cnt_ref[...] + jnp.sum(picked, axis=0, keepdims=True)


def _merge(h, om, att, x, gm, ga, wo, g1, b1, wr, br, tile, name):
    t = x.shape[0]
    rows = lambda w: pl.BlockSpec((tile, w), lambda i: (i, 0))
    const = lambda a: pl.BlockSpec(a.shape, lambda i: (0, 0))
    return pl.pallas_call(
        _merge_kernel,
        out_shape=[jax.ShapeDtypeStruct((t, D_MODEL), F32), jax.ShapeDtypeStruct((t, XP_TILE, LANES), U32),
                   jax.ShapeDtypeStruct((t, LANES), F32), jax.ShapeDtypeStruct((8, LANES), F32)],
        grid=(t // tile,),
        in_specs=[rows(M_WIDTH), rows(M_WIDTH), rows(A_WIDTH), rows(D_MODEL), const(gm), const(ga), const(wo),
                  const(g1), const(b1), const(wr), const(br)],
        out_specs=[rows(D_MODEL), pl.BlockSpec((tile, XP_TILE, LANES), lambda i: (i, 0, 0)), rows(LANES),
                   pl.BlockSpec((8, LANES), lambda i: (0, 0))],
        compiler_params=_cparams("arbitrary"), name=name,
    )(h, om, att, x, gm, ga, wo, g1, b1, wr, br)


def _route_kernel(tk_ref, first_ref, strict_ref, dest_ref, next_scr):
    @pl.when(pl.program_id(0) == 0)
    def _():
        next_scr[...] = first_ref[...]

    tk = tk_ref[...]
    lane = lax.broadcasted_iota(I32, tk.shape, 1).astype(F32)
    onehots = [jnp.where(lane == tk[:, TOP_K + k:TOP_K + k + 1], 1.0, 0.0) for k in range(TOP_K)]
    tot = onehots[0] + onehots[1] + onehots[2] + onehots[3]
    row = jnp.dot(strict_ref[...], tot.astype(BF16), preferred_element_type=F32) + next_scr[0:1, :]
    out = jnp.zeros(tk.shape, F32)
    for k in range(TOP_K):
        out = jnp.where(lane == float(k), jnp.sum(onehots[k] * row, axis=1, keepdims=True), out)
    dest_ref[...] = out.astype(I32)
    next_scr[...] = next_scr[...] + jnp.sum(tot, axis=0, keepdims=True)


def _route(tk, first):
    t = tk.shape[0]
    tile = min(RANK_TILE, t)
    strict = jnp.asarray(np.tril(np.ones((tile, tile), np.float32), -1), BF16)
    return pl.pallas_call(
        _route_kernel, out_shape=jax.ShapeDtypeStruct((t, LANES), I32), grid=(t // tile,),
        in_specs=[pl.BlockSpec((tile, LANES), lambda i: (i, 0)), pl.BlockSpec((8, LANES), lambda i: (0, 0)),
                  pl.BlockSpec((tile, tile), lambda i: (0, 0))],
        out_specs=pl.BlockSpec((tile, LANES), lambda i: (i, 0)),
        scratch_shapes=[pltpu.VMEM((8, LANES), F32)],
        compiler_params=_cparams("arbitrary"), name="moe_route",
    )(tk, first, strict)


def _offsets_kernel(cnt_ref, off_ref, be_ref, nu_ref, pad_ref, *, tile):
    cnt = cnt_ref[...]
    nblk = jnp.floor((cnt + float(tile - 1)) * (1.0 / tile))
    r_i = lax.broadcasted_iota(I32, (LANES, LANES), 0)
    c_i = lax.broadcasted_iota(I32, (LANES, LANES), 1)
    incl = jnp.where(r_i <= c_i, 1.0, 0.0).astype(BF16)
    cum = jnp.dot(nblk.astype(BF16), incl, preferred_element_type=F32)
    off = (cum - nblk) * float(tile)
    off_ref[...] = off
    which = lax.broadcasted_iota(I32, cnt.shape, 0)
    pad_ref[...] = jnp.where(which == 0, off + cnt, jnp.where(which == 1, nblk * float(tile) - cnt, 0.0)).astype(I32)
    rows = be_ref.shape[0]
    jb = (lax.broadcasted_iota(I32, (rows, LANES), 0) * LANES + lax.broadcasted_iota(I32, (rows, LANES), 1)).astype(F32)
    acc = jnp.zeros((rows, LANES), F32)
    for e in range(N_EXPERTS):
        acc = acc + jnp.where(jb >= cum[0:1, e:e + 1], 1.0, 0.0)
    be_ref[...] = jnp.minimum(acc, float(N_EXPERTS - 1)).astype(I32)
    nu_ref[...] = jnp.broadcast_to(cum[0:1, N_EXPERTS - 1:N_EXPERTS], nu_ref.shape).astype(I32)


def _offsets(cnt, n_blocks, tile):
    rows = -(-n_blocks // LANES)
    rows = -(-rows // 8) * 8
    return pl.pallas_call(
        functools.partial(_offsets_kernel, tile=tile),
        out_shape=[jax.ShapeDtypeStruct((8, LANES), F32), jax.ShapeDtypeStruct((rows, LANES), I32),
                   jax.ShapeDtypeStruct((8, LANES), I32), jax.ShapeDtypeStruct((8, LANES), I32)],
        name="moe_offsets",
    )(cnt)


def _scatter_rows(dest_ref, xp_ref, xs_ref, sem):
    t = xp_ref.shape[0]

    def row_copy(tok, dst):
        return pltpu.make_async_copy(xp_ref.at[pl.ds(tok, 1)], xs_ref.at[pl.ds(dst, 1)], sem)

    def issue(grp, carry):
        base = pl.multiple_of(grp * ISSUE_GROUP, ISSUE_GROUP)
        for u in range(ISSUE_GROUP):
            for k in range(TOP_K):
                row_copy(base + u, dest_ref[(base + u) * TOP_K + k]).start(priority=k % 2)
        return carry

    lax.fori_loop(0, t // ISSUE_GROUP, issue, 0)
    for k in range(TOP_K):
        pltpu.make_async_copy(xp_ref, xs_ref.at[pl.ds(0, t)], sem).wait()


def _dispatch_kernel(dest_ref, pads_ref, dest2_ref, xp_ref, xp2_ref, xs_ref, sem, zsem, zbuf, *, block_rows):
    zr = zbuf.shape[0]
    n_blocks = xs_ref.shape[0] // block_rows

    @pl.when(pl.program_id(0) == 0)
    def _():
        zbuf[...] = jnp.zeros_like(zbuf)
        used = pads_ref[2 * N_EXPERTS]

        def pieces(e, act):
            start, n = pads_ref[e], pads_ref[N_EXPERTS + e]
            for sh in range(zr.bit_length() - 1, -1, -1):
                b = 1 << sh
                before = lax.shift_left(lax.shift_right_logical(n, sh + 1), sh + 1)

                @pl.when((n & b) != 0)
                def _():
                    act(pltpu.make_async_copy(zbuf.at[pl.ds(0, b)], xs_ref.at[pl.ds(start + before, b)], zsem))

        def tail(jb, act):
            for h in range(block_rows // zr):
                act(pltpu.make_async_copy(zbuf, xs_ref.at[pl.ds(jb * block_rows + h * zr, zr)], zsem))

        for act in (lambda cp: cp.start(), lambda cp: cp.wait()):
            lax.fori_loop(0, N_EXPERTS, lambda e, c: (pieces(e, act), c)[1], 0)
            lax.fori_loop(used, n_blocks, lambda jb, c: (tail(jb, act), c)[1], 0)

    _scatter_rows(dest_ref, xp_ref, xs_ref, sem)

    @pl.when(pl.program_id(0) == pl.num_programs(0) - 1)
    def _():
        _scatter_rows(dest2_ref, xp2_ref, xs_ref, sem)


def _dispatch(dest, pads, dest2, xp, xp2, n_rows, block_rows):
    t = xp.shape[0]
    tile = min(ROW_TILE, t)
    return pl.pallas_call(
        functools.partial(_dispatch_kernel, block_rows=block_rows),
        out_shape=jax.ShapeDtypeStruct((n_rows,) + xp.shape[1:], xp.dtype), grid=(t // tile,),
        in_specs=[pl.BlockSpec((tile * TOP_K,), lambda i: (i,), memory_space=pltpu.SMEM),
                  pl.BlockSpec(memory_space=pltpu.SMEM),
                  pl.BlockSpec(memory_space=pltpu.SMEM),
                  pl.BlockSpec((tile,) + xp.shape[1:], lambda i: (i, 0, 0)),
                  pl.BlockSpec(xp2.shape, lambda i: (0, 0, 0))],
        out_specs=pl.BlockSpec(memory_space=pl.ANY),
        scratch_shapes=[pltpu.SemaphoreType.DMA(()), pltpu.SemaphoreType.DMA(()),
                        pltpu.VMEM((EXPERT_TILE // 2,) + xp.shape[1:], xp.dtype)],
        compiler_params=_cparams("arbitrary"), name="moe_dispatch",
    )(dest, pads, dest2, xp, xp2)


def _expert_kernel(be_ref, nu_ref, xs_ref, w1_ref, b1g_ref, b1l_ref, w2_ref, b2_ref, perm_ref, ys_ref,
                   w1g_scr, w1l_scr, w2_scr, xq_scr, y_scr, sem, osem):
    j = pl.program_id(0)
    active = j < nu_ref[0]
    changed = jnp.logical_or(j == 0, be_ref[j] != be_ref[jnp.maximum(j - 1, 0)])
    tm = y_scr.shape[0]
    slot = lax.rem(j, 2)

    def fetch(blk, slot):
        row0 = pl.multiple_of(blk * tm, tm)
        return [pltpu.make_async_copy(xs_ref.at[pl.ds(row0, tm), q, :], xq_scr.at[slot, q], sem.at[slot])
                for q in range(XP_TILE)]

    def put(blk):
        row0 = pl.multiple_of(blk * tm, tm)
        return [pltpu.make_async_copy(y_scr.at[:, q * LANES:(q + 1) * LANES], ys_ref.at[pl.ds(row0, tm), q, :], osem)
                for q in range(YS_TILE)]

    def emit(y):
        @pl.when(j > 0)
        def _():
            for cp in put(j - 1):
                cp.wait()

        y_scr[...] = _pack_halves(y)
        for cp in put(j):
            cp.start()

    @pl.when(j == 0)
    def _():
        for cp in fetch(0, 0):
            cp.start()

    @pl.when(j + 1 < nu_ref[0])
    def _():
        for cp in fetch(j + 1, 1 - slot):
            cp.start()

    @pl.when(jnp.logical_and(active, changed))
    def _():
        for c in range(2 * D_FF // 256):
            wc = w1_ref[0, :, c * 256:(c + 1) * 256].astype(BF16)
            d = jnp.dot(wc, perm_ref[...], preferred_element_type=F32).astype(BF16)
            w1g_scr[:, c * 128:(c + 1) * 128] = d[:, :128]
            w1l_scr[:, c * 128:(c + 1) * 128] = d[:, 128:]
        for c in range(D_FF // 256):
            w2_scr[c * 256:(c + 1) * 256, :] = w2_ref[0, c * 256:(c + 1) * 256, :].astype(BF16)

    @pl.when(active)
    def _():
        for cp in fetch(j, slot):
            cp.wait()
        lo, hi = _unpack_halves(jnp.concatenate([xq_scr[slot, q] for q in range(XP_TILE)], axis=1))
        xb = jnp.concatenate([lo, hi], axis=1)
        hg = jnp.dot(xb, w1g_scr[...], preferred_element_type=F32) + b1g_ref[0]
        hl = jnp.dot(xb, w1l_scr[...], preferred_element_type=F32) + b1l_ref[0]
        x_glu = jnp.minimum(hg, SWIGLU_LIMIT)
        x_lin = jnp.clip(hl, -SWIGLU_LIMIT, SWIGLU_LIMIT)
        a = x_glu * _sigmoid(SWIGLU_ALPHA * x_glu) * (x_lin + 1.0)
        emit(jnp.dot(a.astype(BF16), w2_scr[...], preferred_element_type=F32) + b2_ref[0])

    @pl.when(jnp.logical_not(active))
    def _():
        emit(jnp.zeros((tm, D_MODEL), F32))

    @pl.when(j == pl.num_programs(0) - 1)
    def _():
        for cp in put(j):
            cp.wait()


def _experts(be, nu, xs, w1, b1g, b1l, w2, b2, perm, tile):
    n_blocks = xs.shape[0] // tile
    grid_spec = pltpu.PrefetchScalarGridSpec(
        num_scalar_prefetch=2, grid=(n_blocks,),
        in_specs=[pl.BlockSpec(memory_space=pl.ANY),
                  pl.BlockSpec((1, D_MODEL, 2 * D_FF), lambda j, be, nu: (be[j], 0, 0)),
                  pl.BlockSpec((1, 1, D_FF), lambda j, be, nu: (be[j], 0, 0)),
                  pl.BlockSpec((1, 1, D_FF), lambda j, be, nu: (be[j], 0, 0)),
                  pl.BlockSpec((1, D_FF, D_MODEL), lambda j, be, nu: (be[j], 0, 0)),
                  pl.BlockSpec((1, 1, D_MODEL), lambda j, be, nu: (be[j], 0, 0)),
                  pl.BlockSpec((256, 256), lambda j, be, nu: (0, 0))],
        out_specs=pl.BlockSpec(memory_space=pl.ANY),
        scratch_shapes=[pltpu.VMEM((D_MODEL, D_FF), BF16), pltpu.VMEM((D_MODEL, D_FF), BF16),
                        pltpu.VMEM((D_FF, D_MODEL), BF16), pltpu.VMEM((2, XP_TILE, tile, LANES), U32),
                        pltpu.VMEM((tile, YS_TILE * LANES), U32), pltpu.SemaphoreType.DMA((2,)),
                        pltpu.SemaphoreType.DMA(())])
    return pl.pallas_call(
        _expert_kernel, out_shape=jax.ShapeDtypeStruct((xs.shape[0], YS_TILE, LANES), U32), grid_spec=grid_spec,
        compiler_params=_cparams("arbitrary"), name="moe_experts",
    )(be, nu, xs, w1, b1g, b1l, w2, b2, perm)


def _combine_kernel(dest_ref, next_ref, ys_ref, tk_ref, x1_ref, g2_ref, b2_ref, out_ref, buf, sem):
    i = pl.program_id(0)
    t = x1_ref.shape[0]
    slot = lax.rem(i, 2)

    def gather(idx_ref, s):
        def issue(grp, carry):
            base = pl.multiple_of(grp * 8, 8)
            for u in range(8):
                for k in range(TOP_K):
                    pltpu.make_async_copy(ys_ref.at[idx_ref[(base + u) * TOP_K + k]],
                                          buf.at[s, k, grp, :, u, :], sem.at[s]).start(priority=k % 2)
            return carry

        lax.fori_loop(0, t // 8, issue, 0)

    @pl.when(i == 0)
    def _():
        gather(dest_ref, 0)

    @pl.when(i + 1 < pl.num_programs(0))
    def _():
        gather(next_ref, 1 - slot)

    for k in range(TOP_K):
        for u in range(8):
            pltpu.make_async_copy(ys_ref.at[pl.ds(0, t // 8)], buf.at[slot, k, :, :, u, :], sem.at[slot]).wait()
    tk = tk_ref[...]
    los, his = [], []
    for q in range(YS_TILE):
        lo = hi = None
        for k in range(TOP_K):
            words = buf[slot, k, :, q].reshape(t, LANES)
            g = tk[:, k:k + 1]
            lo_k = g * pltpu.bitcast(words << 16, F32)
            hi_k = g * pltpu.bitcast(words & jnp.uint32(0xFFFF0000), F32)
            lo = lo_k if lo is None else lo + lo_k
            hi = hi_k if hi is None else hi + hi_k
        los.append(lo)
        his.append(hi)
    ff = jnp.concatenate(los + his, axis=1)
    out_ref[...] = _layer_norm(DN_ALPHA * x1_ref[...] + ff, g2_ref[...], b2_ref[...])


def _combine(dest_flat, ys, tk, x1, g2, b2):
    t = x1.shape[0]
    tile = min(ROW_TILE, t)
    n = t // tile
    return pl.pallas_call(
        _combine_kernel, out_shape=jax.ShapeDtypeStruct((t, D_MODEL), F32), grid=(n,),
        in_specs=[pl.BlockSpec((tile * TOP_K,), lambda i: (i,), memory_space=pltpu.SMEM),
                  pl.BlockSpec((tile * TOP_K,), lambda i: (jnp.minimum(i + 1, n - 1),), memory_space=pltpu.SMEM),
                  pl.BlockSpec(memory_space=pl.ANY),
                  pl.BlockSpec((tile, LANES), lambda i: (i, 0)),
                  pl.BlockSpec((tile, D_MODEL), lambda i: (i, 0)),
                  pl.BlockSpec((1, D_MODEL), lambda i: (0, 0)),
                  pl.BlockSpec((1, D_MODEL), lambda i: (0, 0))],
        out_specs=pl.BlockSpec((tile, D_MODEL), lambda i: (i, 0)),
        scratch_shapes=[pltpu.VMEM((2, TOP_K, tile // 8, YS_TILE, 8, LANES), U32), pltpu.SemaphoreType.DMA((2,))],
        compiler_params=_cparams("arbitrary"), name="moe_combine",
    )(dest_flat, dest_flat, ys, tk, x1, g2, b2)


def _rel_bucket(dist):
    exact = REL_BUCKETS // 2
    d = np.maximum(dist, 0)
    log_b = exact + (np.log(np.maximum(d, 1).astype(np.float32) / np.float32(exact))
                     / np.float32(math.log(REL_MAX_DIST / exact)) * np.float32(REL_BUCKETS - exact)).astype(np.int32)
    return np.where(d < exact, d, np.minimum(log_b, REL_BUCKETS - 1)).astype(np.int32)


def _bias_lookup(table, bucket, valid):
    bucket = jnp.asarray(bucket)[None]
    acc = jnp.zeros((table.shape[1],) + bucket.shape[1:], F32)
    for b in range(REL_BUCKETS):
        acc = jnp.where(bucket == b, table[b].reshape((-1,) + (1,) * (bucket.ndim - 1)), acc)
    return jnp.where(jnp.asarray(valid)[None], acc, NEG)


def _bias_tables(rel_bias):
    table = rel_bias.astype(F32)
    r = np.arange(WINDOW)[:, None]
    c = np.arange(2 * WINDOW)[None, :]
    dist = r + WINDOW - c
    valid = (dist >= 0) & (dist < WINDOW)
    dist0 = np.where(c < N_META, N_META + r - c, dist)
    valid0 = np.where(c < N_META, dist0 < WINDOW, (c >= WINDOW) & valid)
    both = jnp.stack([_bias_lookup(table, _rel_bucket(dist0), valid0), _bias_lookup(table, _rel_bucket(dist), valid)])
    dist_s = WINDOW - 1 - np.arange(WINDOW)
    rows = _bias_lookup(table[:, np.asarray(HEAD_ORDER)], _rel_bucket(dist_s), np.ones_like(dist_s, bool))
    return both, rows


def _perm_heads(a, axis):
    assert HEAD_ORDER == tuple(kv * A_GROUP + g for g in range(A_GROUP) for kv in range(A_KV_HEADS))
    shape = a.shape
    a = a.reshape(shape[:axis] + (A_KV_HEADS, A_GROUP, A_HD) + shape[axis + 1:])
    return jnp.swapaxes(a, axis, axis + 1).reshape(shape)


def _rep_rows(vec, rows=8):
    out = jnp.zeros((rows, LANES), F32)
    return out.at[:vec.shape[0], :].set(jnp.broadcast_to(vec.astype(F32)[:, None], (vec.shape[0], LANES)))


def kernel(x_prompt, x_sample, cache_swa_k, cache_swa_v, state_mlstm_C, state_mlstm_n, state_mlstm_m, meta_tokens, rel_bias, w_in, b_igate, b_fgate, attn_sinks, g_mlstm_out, g_attn_out, w_out, ln1_g, ln1_b, w_router, b_router, w_moe1, b_moe1, w_moe2, b_moe2, ln2_g, ln2_b):
    B, S, _ = x_prompt.shape
    NB = x_sample.shape[0]
    assert x_sample.shape[1] == 1 and w_in.shape[0] == 1
    assert S % PROJ_TILE == 0 and S % M_CHUNK == 0 and S % WINDOW == 0 and NB % SAMPLE_GROUP == 0
    l = 0

    assert IN_WIDTHS == (512, 512, 512, 512, 4, 4, 512, 128, 128)
    bf = lambda a: a.astype(BF16)
    w = w_in[l]
    n_main, n_gate = 4 * M_WIDTH, 2 * M_HEADS
    w_gate = w[:, n_main:n_main + n_gate]
    w_att = w[:, n_main + n_gate:]
    assert math.frexp(A_HD ** -0.5)[0] == 0.5
    w_qa = _perm_heads(w_att[:, :A_WIDTH], 1) * (A_HD ** -0.5)
    wr = bf(jnp.concatenate([w[:, :n_main], w_qa, w_att[:, A_WIDTH:],
                             jnp.pad(w_gate, ((0, 0), (0, LANES - n_gate)))], axis=1))
    wt = bf(jnp.concatenate([w[:, :M_WIDTH], w[:, 2 * M_WIDTH:3 * M_WIDTH], w_qa, w_att[:, A_WIDTH + LANES:],
                             w_gate], axis=1).T)
    b_gate = jnp.concatenate([b_igate[l], b_fgate[l]]).astype(F32)
    brow = jnp.pad(b_gate, (0, LANES - n_gate))[None, :]
    bcol = b_gate[:, None]
    plan_p = ((512, 512, "plain", BF16), (1536, 512, "plain", F32), (2560, 128, "plain", BF16),
              (2816, 128, "gate", F32))
    tplan_p = ((0, 512, "plain", BF16), (512, 512, "plain", BF16), (1024, 512, "plain", BF16),
               (1536, 128, "plain", BF16), (1664, 8, "gate", F32))
    plan_s = ((0, 512, "plain", F32), (512, 512, "plain", F32), (1024, 512, "plain", F32), (1536, 512, "plain", F32),
              (2048, 512, "plain", F32), (2560, 128, "plain", F32), (2688, 128, "plain", F32), (2816, 128, "gate", F32))

    bias_tab, bias_rows = _bias_tables(rel_bias)
    sinks = _rep_rows(attn_sinks[l])
    sinks_step = _rep_rows(attn_sinks[l][np.asarray(HEAD_ORDER)])
    g_m = g_mlstm_out[l].astype(F32)[None, :]
    g_a = _perm_heads(g_attn_out[l].astype(F32), 0)[None, :]
    wo = bf(jnp.concatenate([w_out[l][:M_WIDTH], _perm_heads(w_out[l][M_WIDTH:], 0)], axis=0))
    g1, b1 = ln1_g[l].astype(F32)[None, :], ln1_b[l].astype(F32)[None, :]
    g2, b2 = ln2_g[l].astype(F32)[None, :], ln2_b[l].astype(F32)[None, :]
    w_r = bf(jnp.pad(w_router[l], ((0, 0), (0, LANES - N_EXPERTS))))
    b_r = jnp.pad(b_router[l].astype(F32), (0, LANES - N_EXPERTS), constant_values=NEG)[None, :]
    b1g = b_moe1[l][:, 0::2].astype(F32)[:, None, :]
    b1l = b_moe1[l][:, 1::2].astype(F32)[:, None, :]
    b2e = b_moe2[l].astype(F32)[:, None, :]
    pj = np.zeros((256, 256), np.float32)
    pj[2 * np.arange(128), np.arange(128)] = 1.0
    pj[2 * np.arange(128) + 1, 128 + np.arange(128)] = 1.0
    perm = jnp.asarray(pj, BF16)

    xp2 = x_prompt.reshape(B * S, D_MODEL)
    km, om, ka, gc, qt, vt, qat, vat, gr, kv_tail = _proj(
        xp2, wr, wt, brow, bcol, plan_p, tplan_p, (2560, 256), PROJ_TILE, S, "proj_prompt")
    x_meta = jnp.pad(meta_tokens.astype(F32), ((0, M_CHUNK - N_META), (0, 0)))
    km0, _, ka0, gc0, qt0, vt0, _, vat0, gr0 = _proj(
        x_meta, wr, wt, brow, bcol, plan_p, tplan_p, None, M_CHUNK, M_CHUNK, "proj_meta")
    xs2 = x_sample.reshape(NB, D_MODEL)
    qm_s, km_s, vm_s, om_s, qa_s, ka_s, va_s, gc_s = _proj(
        xs2, wr, wt, brow, bcol, plan_s, (), None, NB, NB, "proj_sample")

    zero_c = jnp.zeros((M_HEADS, M_DV + 8, M_DK), F32)
    zero_m = jnp.zeros((8, LANES), F32)
    _, c_meta, m_meta = _mlstm(qt0, km0, vt0, gc0, gr0, zero_c, zero_m, 1, N_META, "mlstm_meta")
    h_p, c_p, m_p = _mlstm(qt, km, vt, gc, gr, c_meta[0], m_meta[0], B, M_CHUNK, "mlstm_prompt")
    C_p = c_p[:, :, :M_DV, :]
    n_p = c_p[:, :, M_DV, :]
    m_prompt = m_p[:, :M_HEADS, 0]
    m_pad = jnp.pad(state_mlstm_m[l].astype(F32), ((0, 0), (0, LANES - M_HEADS)))
    C_s, n_s, m_s, h_s = _mlstm_step(state_mlstm_C[l].astype(F32), state_mlstm_n[l].astype(F32), m_pad,
                                     gc_s, qm_s, km_s, vm_s)

    att_p = _swa(qat, ka, vat, ka0, vat0, jnp.swapaxes(bias_tab, 2, 3), sinks, B)
    ck = cache_swa_k[l].reshape(NB, WINDOW, LANES)
    cv = cache_swa_v[l].reshape(NB, WINDOW, LANES)
    k_new, v_new, att_s = _swa_step(ck, cv, qa_s, ka_s, va_s, bias_rows, sinks_step)

    x1_p, xpk_p, tk_p, cnt_p = _merge(h_p, om, att_p, xp2, g_m, g_a, wo, g1, b1, w_r, b_r, MERGE_TILE, "merge_prompt")
    x1_s, xpk_s, tk_s, cnt_s = _merge(h_s, om_s, att_s, xs2, g_m, g_a, wo, g1, b1, w_r, b_r, NB, "merge_sample")

    T_p = B * S
    assert T_p % RANK_TILE == 0 and T_p % ROW_TILE == 0
    n_blocks = -(-((T_p + NB) * TOP_K) // EXPERT_TILE) + N_EXPERTS
    off, be2, nu2, pad = _offsets(cnt_p + cnt_s, n_blocks, EXPERT_TILE)
    pads = jnp.concatenate([pad[0, :N_EXPERTS], pad[1, :N_EXPERTS], nu2[0, :1]])
    dest_p = _route(tk_p, off)[:, :TOP_K].reshape(-1)
    dest_s = _route(tk_s, off + cnt_p)[:, :TOP_K].reshape(-1)
    be = be2.reshape(-1)[:n_blocks]
    nu = nu2[0, :1]
    xs = _dispatch(dest_p, pads, dest_s, xpk_p, xpk_s, n_blocks * EXPERT_TILE, EXPERT_TILE)
    ys = _experts(be, nu, xs, w_moe1[l], b1g, b1l, w_moe2[l], b2e, perm, EXPERT_TILE)
    y_p = _combine(dest_p, ys, tk_p, x1_p, g2, b2)
    y_s = _combine(dest_s, ys, tk_s, x1_s, g2, b2)

    kv_tail = kv_tail.reshape(B, WINDOW, 2, A_KV_HEADS, A_HD)
    dt_k, dt_v = cache_swa_k.dtype, cache_swa_v.dtype
    return (y_p.reshape(B, S, D_MODEL).astype(x_prompt.dtype), y_s.reshape(NB, 1, D_MODEL).astype(x_sample.dtype),
            kv_tail[:, :, 0][None].astype(dt_k), kv_tail[:, :, 1][None].astype(dt_v),
            C_p[None].astype(state_mlstm_C.dtype), n_p[None].astype(state_mlstm_n.dtype),
            m_prompt[None].astype(state_mlstm_m.dtype),
            k_new.reshape(1, NB, WINDOW, A_KV_HEADS, A_HD).astype(dt_k),
            v_new.reshape(1, NB, WINDOW, A_KV_HEADS, A_HD).astype(dt_v),
            C_s[None].astype(state_mlstm_C.dtype), n_s[None].astype(state_mlstm_n.dtype),
            m_s[:, :M_HEADS][None].astype(state_mlstm_m.dtype))
```

```python
import functools
import math

import numpy as np
import jax
import jax.numpy as jnp
from jax import lax
from jax.experimental import pallas as pl
from jax.experimental.pallas import tpu as pltpu
from jax.experimental.pallas import tpu_sc as plsc

F32 = jnp.float32
BF16 = jnp.bfloat16
I32 = jnp.int32
U32 = jnp.uint32

D_MODEL = 1024
N_META = 16
M_HEADS = 4
M_DK = 128
M_DV = 128
M_WIDTH = M_HEADS * M_DV
A_HD = 64
A_HEADS = 8
A_KV_HEADS = 2
A_GROUP = A_HEADS // A_KV_HEADS
A_WIDTH = A_HEADS * A_HD
WINDOW = 128
REL_BUCKETS = 32
REL_MAX_DIST = 128
N_EXPERTS = 32
TOP_K = 4
D_FF = D_MODEL
SWIGLU_LIMIT = 7.0
SWIGLU_ALPHA = 1.702
DEPTH = 1
DN_ALPHA = (2.0 * DEPTH) ** 0.25
LN_EPS = 1e-5
IN_WIDTHS = (M_WIDTH, M_WIDTH, M_WIDTH, M_WIDTH, M_HEADS, M_HEADS, A_WIDTH, A_KV_HEADS * A_HD, A_KV_HEADS * A_HD)

LANES = 128
NEG = -1e30
VMEM_LIMIT = 56 * 1024 * 1024

M_CHUNK = 256
PROJ_TILE = 1024
MERGE_TILE = 1024
RANK_TILE = 1024
ROW_TILE = 512
EXPERT_TILE = 512
SAMPLE_GROUP = 8
ISSUE_GROUP = 8
SWA_QBLOCKS = 8
SC_CORES, SC_SUBCORES = 2, 16
SC_CHUNK = 128
MLSTM_SEQS = 4
STEP_PIPE_LAG = 3
SWA_PIPE_LAG = 2
XP_TILE = D_MODEL // 2 // LANES
YS_TILE = D_MODEL // 2 // LANES
HEAD_ORDER = (0, 4, 1, 5, 2, 6, 3, 7)


def _cparams(*sem):
    return pltpu.CompilerParams(dimension_semantics=sem, vmem_limit_bytes=VMEM_LIMIT)


def _log_sigmoid(x):
    return jnp.minimum(x, 0.0) - jnp.log1p(jnp.exp(-jnp.abs(x)))


def _sigmoid(x):
    return 1.0 / (1.0 + jnp.exp(-x))


def _proj_kernel(x_ref, wr_ref, wt_ref, brow_ref, bcol_ref, *outs, row_plan, t_plan, tail_cols):
    xb = x_ref[...].astype(BF16)
    tm = xb.shape[0]
    o = 0
    for (c0, width, kind, _) in row_plan:
        r = jnp.dot(xb, wr_ref[:, c0:c0 + width], preferred_element_type=F32)
        if kind == "gate":
            r = r + brow_ref[...]
            lane = lax.broadcasted_iota(I32, r.shape, 1)
            r = jnp.where(lane < M_HEADS, r, _log_sigmoid(r))
        outs[o][...] = r.astype(outs[o].dtype)
        o += 1
    for (r0, nrows, kind, _) in t_plan:
        r = lax.dot_general(wt_ref[r0:r0 + nrows, :], xb, (((1,), (1,)), ((), ())), preferred_element_type=F32)
        if kind == "gate":
            r = r + bcol_ref[...]
            row = lax.broadcasted_iota(I32, r.shape, 0)
            r = jnp.where(row < M_HEADS, r, _log_sigmoid(r))
        outs[o][...] = r.astype(outs[o].dtype)
        o += 1
    if tail_cols is not None:
        c0, width = tail_cols
        outs[o][...] = jnp.dot(xb[tm - WINDOW:, :], wr_ref[:, c0:c0 + width], preferred_element_type=F32)


def _proj(x, wr, wt, brow, bcol, row_plan, t_plan, tail_cols, tile, rows_per_group, name):
    t = x.shape[0]
    nt = t // tile
    out_shape, out_specs = [], []
    for (_, width, _, dt) in row_plan:
        out_shape.append(jax.ShapeDtypeStruct((t, width), dt))
        out_specs.append(pl.BlockSpec((tile, width), lambda i: (i, 0)))
    for (_, nrows, _, dt) in t_plan:
        out_shape.append(jax.ShapeDtypeStruct((nrows, t), dt))
        out_specs.append(pl.BlockSpec((nrows, tile), lambda i: (0, i)))
    if tail_cols is not None:
        tiles_per_group = rows_per_group // tile
        out_shape.append(jax.ShapeDtypeStruct((t // rows_per_group * WINDOW, tail_cols[1]), F32))
        out_specs.append(pl.BlockSpec((WINDOW, tail_cols[1]), lambda i: (i // tiles_per_group, 0)))
    kern = functools.partial(_proj_kernel, row_plan=row_plan, t_plan=t_plan, tail_cols=tail_cols)
    return pl.pallas_call(
        kern, out_shape=out_shape, grid=(nt,),
        in_specs=[pl.BlockSpec((tile, D_MODEL), lambda i: (i, 0)),
                  pl.BlockSpec(wr.shape, lambda i: (0, 0)),
                  pl.BlockSpec(wt.shape, lambda i: (0, 0)),
                  pl.BlockSpec(brow.shape, lambda i: (0, 0)),
                  pl.BlockSpec(bcol.shape, lambda i: (0, 0))],
        out_specs=out_specs, compiler_params=_cparams("arbitrary"), name=name,
    )(x, wr, wt, brow, bcol)


def _split3(a):
    hi = a.astype(BF16)
    r1 = a - hi.astype(F32)
    mid = r1.astype(BF16)
    lo = (r1 - mid.astype(F32)).astype(BF16)
    return hi, mid, lo


def _mlstm_kernel(*refs, n_valid, nseq):
    seq_in = [refs[5 * i:5 * i + 5] for i in range(nseq)]
    c0_ref, m0_ref = refs[5 * nseq:5 * nseq + 2]
    h_ref, c_out_ref, m_out_ref, c_scr, m_scr = refs[5 * nseq + 2:]
    c = pl.program_id(1)
    nc = pl.num_programs(1)
    L = seq_in[0][1].shape[0]

    @pl.when(c == 0)
    def _():
        for i in range(nseq):
            c_scr[i] = c0_ref[...]
            m_scr[i] = m0_ref[...]

    r_i = lax.broadcasted_iota(I32, (L, L), 0)
    c_i = lax.broadcasted_iota(I32, (L, L), 1)
    upper = r_i <= c_i
    tril = jnp.where(c_i <= r_i, 1.0, 0.0).astype(BF16)
    triu = jnp.where(upper, 1.0, 0.0).astype(BF16)
    scale = M_DK ** -0.5
    ones_rows = jnp.where(lax.broadcasted_iota(I32, (8, L), 0) == 0, 1.0, 0.0).astype(BF16)

    gates = []
    for (_, _, _, gc_ref, gr_ref) in seq_in:
        gc = gc_ref[...]
        gr = gr_ref[...]
        if n_valid < L:
            rowc = lax.broadcasted_iota(I32, gc.shape, 0)
            lanec = lax.broadcasted_iota(I32, gc.shape, 1)
            gc = jnp.where(rowc < n_valid, gc, jnp.where(lanec < M_HEADS, NEG, 0.0))
            rowr = lax.broadcasted_iota(I32, gr.shape, 0)
            colr = lax.broadcasted_iota(I32, gr.shape, 1)
            gr = jnp.where(colr < n_valid, gr, jnp.where(rowr < M_HEADS, NEG, 0.0))
        b_cols = sum(jnp.dot(tril, part, preferred_element_type=F32) for part in _split3(gc))
        b_rows = sum(jnp.dot(part, triu, preferred_element_type=F32) for part in _split3(gr))
        gates.append((gc, gr, b_cols, b_rows))

    m_alls = [m_scr[i] for i in range(nseq)]
    c_alls = [[c_scr[i, h] for h in range(M_HEADS)] for i in range(nseq)]
    h_new, c_new, m_new_all = {}, {}, {}

    def operands(u):
        i, h = u
        qt_ref, k_ref, vt_ref = seq_in[i][:3]
        sl = slice(h * M_DK, (h + 1) * M_DK)
        return qt_ref[sl, :], k_ref[:, sl], jnp.concatenate([vt_ref[sl, :], ones_rows], axis=0)

    def stage_a(u):
        i, h = u
        gc, gr, b_cols, b_rows = gates[i]
        qt, k, vt_aug = operands(u)
        ig_r = gr[h:h + 1, :]
        b_r = b_rows[M_HEADS + h:M_HEADS + h + 1, :]
        m_prev = m_alls[i][h:h + 1, 0:1]
        cs = c_alls[i][h]
        qk = jnp.dot(k, qt, preferred_element_type=F32)
        inter = jnp.dot(cs.astype(BF16), qt, preferred_element_type=F32)
        b_last = b_r[:, L - 1:L]
        g = ig_r + b_last - b_r
        m_new = jnp.maximum(b_last + m_prev, jnp.max(g, axis=1, keepdims=True))
        a = jnp.exp(b_last + m_prev - m_new)
        wv = (vt_aug.astype(F32) * jnp.exp(g - m_new)).astype(BF16)
        c_new[u] = a * cs + jnp.dot(wv, k, preferred_element_type=F32) * scale
        m_new_all[u] = jnp.broadcast_to(m_new, (1, LANES))
        return qk, inter

    def stage_b(u, qk, inter):
        i, h = u
        gc, gr, b_cols, b_rows = gates[i]
        b_r = b_rows[M_HEADS + h:M_HEADS + h + 1, :]
        m_prev = m_alls[i][h:h + 1, 0:1]
        r_c = gc[:, h:h + 1] - b_cols[:, M_HEADS + h:M_HEADS + h + 1]
        dt = jnp.where(upper, b_r + r_c, NEG)
        m_t = jnp.maximum(b_r + m_prev, jnp.max(dt, axis=0, keepdims=True))
        st = (qk * (scale * jnp.exp(dt - m_t))).astype(BF16)
        return st, jnp.exp(b_r + m_prev - m_t) * inter, jnp.exp(-m_t)

    def stage_c(u, st, inter_w, floor):
        _, _, vt_aug = operands(u)
        nd = inter_w + jnp.dot(vt_aug, st, preferred_element_type=F32)
        den = nd[M_DV:M_DV + 1, :]
        h_new[u] = (nd[:M_DV, :] / jnp.maximum(jnp.abs(den), floor)).T

    units = [(i, h) for h in range(M_HEADS) for i in range(nseq)]
    a_q, b_q = {}, {}
    for n in range(len(units) + 2):
        if n < len(units):
            a_q[n] = stage_a(units[n])
        if 0 <= n - 1 < len(units):
            b_q[n - 1] = stage_b(units[n - 1], *a_q.pop(n - 1))
        if 0 <= n - 2 < len(units):
            stage_c(units[n - 2], *b_q.pop(n - 2))

    for i in range(nseq):
        h_ref[0, i] = jnp.concatenate([h_new[(i, h)] for h in range(M_HEADS)], axis=1)
        for h in range(M_HEADS):
            c_scr[i, h] = c_new[(i, h)]
        m_scr[i, 0:M_HEADS, :] = jnp.concatenate([m_new_all[(i, h)] for h in range(M_HEADS)], axis=0)

    @pl.when(c == nc - 1)
    def _():
        c_out_ref[...] = c_scr[...]
        m_out_ref[...] = m_scr[...]


def _mlstm(qt, km, vt, gc, gr, c0, m0, batch, n_valid, name):
    L = M_CHUNK
    nc = km.shape[0] // (batch * L)
    nseq = MLSTM_SEQS if batch % MLSTM_SEQS == 0 else 1
    kern = functools.partial(_mlstm_kernel, n_valid=n_valid, nseq=nseq)
    in_specs, operands = [], []
    for i in range(nseq):
        blk = functools.partial(lambda b, c, i: (b * nseq + i) * nc + c, i=i)
        rows = pl.BlockSpec((L, M_WIDTH), functools.partial(lambda b, c, blk: (blk(b, c), 0), blk=blk))
        cols = pl.BlockSpec((M_WIDTH, L), functools.partial(lambda b, c, blk: (0, blk(b, c)), blk=blk))
        in_specs += [cols, rows, cols,
                     pl.BlockSpec((L, LANES), functools.partial(lambda b, c, blk: (blk(b, c), 0), blk=blk)),
                     pl.BlockSpec((8, L), functools.partial(lambda b, c, blk: (0, blk(b, c)), blk=blk))]
        operands += [qt, km, vt, gc, gr]
    in_specs += [pl.BlockSpec((M_HEADS, M_DV + 8, M_DK), lambda b, c: (0, 0, 0)),
                 pl.BlockSpec((8, LANES), lambda b, c: (0, 0))]
    h4, c_fin, m_fin = pl.pallas_call(
        kern,
        out_shape=[jax.ShapeDtypeStruct((batch // nseq, nseq, nc * L, M_WIDTH), F32),
                   jax.ShapeDtypeStruct((batch, M_HEADS, M_DV + 8, M_DK), F32),
                   jax.ShapeDtypeStruct((batch, 8, LANES), F32)],
        grid=(batch // nseq, nc),
        in_specs=in_specs,
        out_specs=[pl.BlockSpec((1, nseq, L, M_WIDTH), lambda b, c: (b, 0, c, 0)),
                   pl.BlockSpec((nseq, M_HEADS, M_DV + 8, M_DK), lambda b, c: (b, 0, 0, 0)),
                   pl.BlockSpec((nseq, 8, LANES), lambda b, c: (b, 0, 0))],
        scratch_shapes=[pltpu.VMEM((nseq, M_HEADS, M_DV + 8, M_DK), F32), pltpu.VMEM((nseq, 8, LANES), F32)],
        compiler_params=_cparams("arbitrary", "arbitrary"), name=name,
    )(*operands, c0, m0)
    return h4.reshape(batch * nc * L, M_WIDTH), c_fin, m_fin


def _outer_f32(a, b):
    ah, am, al = (t.astype(F32) for t in _split3(a))
    bh, bm, bl = (t.astype(F32) for t in _split3(b))
    z = jnp.zeros_like(ah)
    lhs = jnp.concatenate([ah, ah, ah, am, am, al, z, z], axis=0).astype(BF16)
    rhs = jnp.concatenate([bh, bm, bl, bh, bm, bh, z, z], axis=0).astype(BF16)
    return lax.dot_general(lhs, rhs, (((0,), (0,)), ((), ())), preferred_element_type=F32)


def _mlstm_step_kernel(c_ref, n_ref, m_ref, gc_ref, q_ref, k_ref, v_ref,
                       c_out_ref, n_out_ref, m_out_ref, h_ref):
    g = c_ref.shape[0]
    assert g == 8
    scale = M_DK ** -0.5
    ig = gc_ref[:, 0:M_HEADS]
    lf = gc_ref[:, M_HEADS:2 * M_HEADS]
    m = m_ref[:, 0:M_HEADS]
    m_t = jnp.maximum(lf + m, ig)
    w = jnp.exp(lf + m - m_t)
    wg = jnp.exp(ig - m_t)
    floor = jnp.exp(-m_t)
    m_out_ref[...] = jnp.zeros_like(m_out_ref)
    m_out_ref[:, 0:M_HEADS] = m_t
    row8 = lax.broadcasted_iota(I32, (g, M_DV), 0)

    per_head = []
    for h in range(M_HEADS):
        sl = slice(h * M_DK, (h + 1) * M_DK)
        q = q_ref[:, sl]
        k = k_ref[:, sl] * scale
        v = v_ref[:, sl]
        n = n_ref[:, h, :]
        w_h, wg_h = w[:, h:h + 1], wg[:, h:h + 1]
        s = jnp.sum(q * k, axis=1, keepdims=True) * wg_h
        den = w_h * jnp.sum(n * q, axis=1, keepdims=True) + s
        n_out_ref[:, h, :] = w_h * n + wg_h * k
        per_head.append((q.astype(BF16), k, wg_h * v, w_h, s * v, 1.0 / jnp.maximum(jnp.abs(den), floor[:, h:h + 1])))

    def stage_a(h, j):
        qb = per_head[h][0]
        r = lax.dot_general(qb, c_ref[j, h].astype(BF16), (((1,), (1,)), ((), ())), preferred_element_type=F32)
        return jnp.where(row8 == j, r, 0.0)

    def stage_b(h, j):
        _, k, wv, w_h, _, _ = per_head[h]
        c_out_ref[j, h] = w_h[j:j + 1, :] * c_ref[j, h] + _outer_f32(wv[j:j + 1, :], k[j:j + 1, :])

    units = [(h, j) for h in range(M_HEADS) for j in range(g)]
    lag = STEP_PIPE_LAG
    cq = [jnp.zeros((g, M_DV), F32) for _ in range(M_HEADS)]
    for i in range(len(units) + lag):
        if i < len(units):
            cq[units[i][0]] = cq[units[i][0]] + stage_a(*units[i])
        if 0 <= i - lag < len(units):
            stage_b(*units[i - lag])
    for h in range(M_HEADS):
        _, _, _, w_h, sv, inv = per_head[h]
        h_ref[:, h * M_DK:(h + 1) * M_DK] = (w_h * cq[h] + sv) * inv


def _mlstm_step(c, n, m_pad, gc, q, k, v):
    nb = c.shape[0]
    g = SAMPLE_GROUP
    row = lambda w: pl.BlockSpec((g, w), lambda i: (i, 0))
    return pl.pallas_call(
        _mlstm_step_kernel,
        out_shape=[jax.ShapeDtypeStruct(c.shape, F32), jax.ShapeDtypeStruct(n.shape, F32),
                   jax.ShapeDtypeStruct((nb, LANES), F32), jax.ShapeDtypeStruct((nb, M_WIDTH), F32)],
        grid=(nb // g,),
        in_specs=[pl.BlockSpec((g, M_HEADS, M_DV, M_DK), lambda i: (i, 0, 0, 0)),
                  pl.BlockSpec((g, M_HEADS, M_DK), lambda i: (i, 0, 0)),
                  row(LANES), row(LANES), row(M_WIDTH), row(M_WIDTH), row(M_WIDTH)],
        out_specs=[pl.BlockSpec((g, M_HEADS, M_DV, M_DK), lambda i: (i, 0, 0, 0)),
                   pl.BlockSpec((g, M_HEADS, M_DK), lambda i: (i, 0, 0)),
                   row(LANES), row(M_WIDTH)],
        compiler_params=_cparams("arbitrary"), name="mlstm_step",
    )(c, n, m_pad, gc, q, k, v)


def _swa_kernel(qt_ref, kc_ref, kp_ref, vtc_ref, vtp_ref, km_ref, vtm_ref, bias_ref, sink_ref, o_ref):
    j = pl.program_id(1)
    first = j == 0
    blk = WINDOW
    nqb = qt_ref.shape[1] // blk
    kp = jnp.where(first, km_ref[...], kp_ref[...])
    vtp = jnp.where(first, vtm_ref[...], vtp_ref[...])
    k = jnp.concatenate([kp, kc_ref[...]], axis=0)
    vt = jnp.concatenate([vtp, vtc_ref[...]], axis=1)
    row_v = lax.broadcasted_iota(I32, vt.shape, 0)
    zero_v = jnp.zeros_like(vt)
    vt_half = (jnp.where(row_v < A_HD, vt, zero_v), jnp.where(row_v >= A_HD, vt, zero_v))
    row_q = lax.broadcasted_iota(I32, (LANES, blk), 0)
    lo_rows = row_q < A_HD
    def scores(u, p):
        cols = slice(u * blk, (u + 1) * blk)
        keys = slice(u * blk, (u + 2) * blk)
        qs = qt_ref[p * LANES:(p + 1) * LANES, cols]
        zero_q = jnp.zeros_like(qs)
        q_own = (jnp.where(lo_rows, qs, zero_q), jnp.where(lo_rows, zero_q, qs))
        return [jnp.dot(k[keys], q_own[half], preferred_element_type=F32) for half in range(2)]

    def softmax(u, p, s2):
        table = jnp.where(first, 0, 1) if u == 0 else 1
        probs, inv = [], []
        for half in range(2):
            hd = HEAD_ORDER[2 * p + half]
            s = s2[half] + bias_ref[table, hd]
            sk = sink_ref[hd:hd + 1, 0:1]
            m = jnp.maximum(jnp.max(s, axis=0, keepdims=True), sk)
            e = jnp.exp(s - m)
            probs.append(e.astype(BF16))
            inv.append(1.0 / (jnp.sum(e, axis=0, keepdims=True) + jnp.exp(sk - m)))
        return jnp.concatenate(probs, axis=0), jnp.where(lo_rows, inv[0], inv[1])

    def values(u, p, probs, inv):
        keys = slice(u * blk, (u + 2) * blk)
        vt_stack = jnp.concatenate([vt_half[0][:, keys], vt_half[1][:, keys]], axis=1)
        ot = jnp.dot(vt_stack, probs, preferred_element_type=F32)
        o_ref[u * blk:(u + 1) * blk, p * LANES:(p + 1) * LANES] = (ot * inv).T

    units = [(u, p) for u in range(nqb) for p in range(A_GROUP)]
    s_q, p_q = {}, {}
    lag = SWA_PIPE_LAG
    for i in range(len(units) + 2 * lag):
        if i < len(units):
            s_q[i] = scores(*units[i])
        if 0 <= i - lag < len(units):
            p_q[i - lag] = softmax(*units[i - lag], s_q.pop(i - lag))
        if 0 <= i - 2 * lag < len(units):
            values(*units[i - 2 * lag], *p_q.pop(i - 2 * lag))


def _swa(qat, ka, vat, kmeta, vtmeta, bias_t, sinks, batch):
    blk = WINDOW
    nqb = SWA_QBLOCKS
    t = ka.shape[0]
    nq = t // (batch * blk * nqb)
    prev = lambda b, j: (b * nq + j) * nqb + jnp.where(j == 0, 0, -1)
    const2 = lambda shape: pl.BlockSpec(shape, lambda b, j: (0, 0))
    return pl.pallas_call(
        _swa_kernel, out_shape=jax.ShapeDtypeStruct((t, A_WIDTH), F32), grid=(batch, nq),
        in_specs=[pl.BlockSpec((A_WIDTH, nqb * blk), lambda b, j: (0, b * nq + j)),
                  pl.BlockSpec((nqb * blk, LANES), lambda b, j: (b * nq + j, 0)),
                  pl.BlockSpec((blk, LANES), lambda b, j: (prev(b, j), 0)),
                  pl.BlockSpec((LANES, nqb * blk), lambda b, j: (0, b * nq + j)),
                  pl.BlockSpec((LANES, blk), lambda b, j: (0, prev(b, j))),
                  const2((blk, LANES)), const2((LANES, blk)),
                  pl.BlockSpec(bias_t.shape, lambda b, j: (0, 0, 0, 0)),
                  const2((8, LANES))],
        out_specs=pl.BlockSpec((nqb * blk, A_WIDTH), lambda b, j: (b * nq + j, 0)),
        compiler_params=_cparams("arbitrary", "arbitrary"), name="swa_prompt",
    )(qat, ka, ka, vat, vat, kmeta, vtmeta, bias_t, sinks)


def _swa_step_kernel(ck_ref, cv_ref, q_ref, k_ref, v_ref, bias_ref, sink_ref, ko_ref, vo_ref, o_ref):
    g = ck_ref.shape[0]
    lane = lax.broadcasted_iota(I32, (A_HEADS, LANES), 1)
    row = lax.broadcasted_iota(I32, (A_HEADS, LANES), 0)
    own_half = (row % 2 == 0) == (lane < A_HD)
    bias = bias_ref[...]
    sk = sink_ref[:, 0:1]
    def stage_a(j):
        ko_ref[j, 0:WINDOW - 1, :] = ck_ref[j, 1:WINDOW, :]
        ko_ref[j, WINDOW - 1:WINDOW, :] = k_ref[j:j + 1, :]
        vo_ref[j, 0:WINDOW - 1, :] = cv_ref[j, 1:WINDOW, :]
        vo_ref[j, WINDOW - 1:WINDOW, :] = v_ref[j:j + 1, :]
        kk = ko_ref[j].astype(BF16)
        slabs = [q_ref[j:j + 1, p * LANES:(p + 1) * LANES] for p in range(A_GROUP)]
        q8 = jnp.concatenate([slabs[r // 2] for r in range(A_HEADS)], axis=0)
        q8 = jnp.where(own_half, q8, 0.0).astype(BF16)
        s = lax.dot_general(q8, kk, (((1,), (1,)), ((), ())), preferred_element_type=F32)
        s = s + bias
        m = jnp.maximum(jnp.max(s, axis=1, keepdims=True), sk)
        e = jnp.exp(s - m)
        return e.astype(BF16), 1.0 / (jnp.sum(e, axis=1, keepdims=True) + jnp.exp(sk - m))

    def stage_b(j, p8, inv):
        vv = vo_ref[j].astype(BF16)
        o8 = jnp.where(own_half, jnp.dot(p8, vv, preferred_element_type=F32) * inv, 0.0)
        for p in range(A_GROUP):
            o_ref[j:j + 1, p * LANES:(p + 1) * LANES] = o8[2 * p:2 * p + 1, :] + o8[2 * p + 1:2 * p + 2, :]

    lag, pending = STEP_PIPE_LAG, {}
    for i in range(g + lag):
        if i < g:
            pending[i] = stage_a(i)
        if 0 <= i - lag < g:
            stage_b(i - lag, *pending.pop(i - lag))


def _swa_step(ck, cv, q, k, v, bias_rows, sinks):
    nb = ck.shape[0]
    g = SAMPLE_GROUP
    cache = pl.BlockSpec((g, WINDOW, LANES), lambda i: (i, 0, 0))
    row = lambda w: pl.BlockSpec((g, w), lambda i: (i, 0))
    const = lambda a: pl.BlockSpec(a.shape, lambda i: (0, 0))
    return pl.pallas_call(
        _swa_step_kernel,
        out_shape=[jax.ShapeDtypeStruct(ck.shape, F32), jax.ShapeDtypeStruct(cv.shape, F32),
                   jax.ShapeDtypeStruct((nb, A_WIDTH), F32)],
        grid=(nb // g,),
        in_specs=[cache, cache, row(A_WIDTH), row(LANES), row(LANES), const(bias_rows), const(sinks)],
        out_specs=[cache, cache, row(A_WIDTH)],
        compiler_params=_cparams("arbitrary"), name="swa_step",
    )(ck, cv, q, k, v, bias_rows, sinks)


def _layer_norm(z, g, b):
    mu = jnp.mean(z, axis=1, keepdims=True)
    zc = z - mu
    var = jnp.mean(zc * zc, axis=1, keepdims=True)
    return zc * lax.rsqrt(var + LN_EPS) * g + b


def _pack_halves(x):
    w = x.shape[1] // 2
    lo = pltpu.bitcast(x[:, :w].astype(BF16).astype(F32), U32)
    hi = pltpu.bitcast(x[:, w:].astype(BF16).astype(F32), U32)
    return (lo >> 16) | (hi & jnp.uint32(0xFFFF0000))


def _unpack_halves(words):
    lo = pltpu.bitcast(words << 16, F32).astype(BF16)
    hi = pltpu.bitcast(words & jnp.uint32(0xFFFF0000), F32).astype(BF16)
    return lo, hi


def _to_token_tiles(ref, x):
    for q in range(x.shape[1] // LANES):
        ref[:, q, :] = x[:, q * LANES:(q + 1) * LANES]


def _merge_kernel(h_ref, om_ref, att_ref, x_ref, gm_ref, ga_ref, wo_ref, g1_ref, b1_ref, wr_ref, br_ref,
                  x1_ref, xp_ref, tk_ref, cnt_ref):
    @pl.when(pl.program_id(0) == 0)
    def _():
        cnt_ref[...] = jnp.zeros_like(cnt_ref)

    hm = h_ref[...] * _sigmoid(om_ref[...])
    ym = hm * lax.rsqrt(jnp.mean(hm * hm, axis=1, keepdims=True) + LN_EPS) * gm_ref[...]
    att = att_ref[...]
    ya = att * lax.rsqrt(jnp.mean(att * att, axis=1, keepdims=True) + LN_EPS) * ga_ref[...]
    mix = (jnp.dot(ym.astype(BF16), wo_ref[0:M_WIDTH, :], preferred_element_type=F32)
           + jnp.dot(ya.astype(BF16), wo_ref[M_WIDTH:, :], preferred_element_type=F32))
    x1 = _layer_norm(DN_ALPHA * x_ref[...] + mix, g1_ref[...], b1_ref[...])
    x1_ref[...] = x1
    _to_token_tiles(xp_ref, _pack_halves(x1))
    logits = jnp.dot(x1.astype(BF16), wr_ref[...], preferred_element_type=F32) + br_ref[...]
    lane = lax.broadcasted_iota(I32, logits.shape, 1).astype(F32)
    vals, idxs = [], []
    for _ in range(TOP_K):
        mx = jnp.max(logits, axis=1, keepdims=True)
        idx = jnp.min(jnp.where(logits == mx, lane, float(LANES)), axis=1, keepdims=True)
        vals.append(mx)
        idxs.append(idx)
        logits = jnp.where(lane == idx, 2.0 * NEG, logits)
    es = [jnp.exp(vk - vals[0]) for vk in vals]
    tot = es[0] + es[1] + es[2] + es[3]
    tk = jnp.zeros(logits.shape, F32)
    picked = jnp.zeros(logits.shape, F32)
    for k in range(TOP_K):
        tk = jnp.where(lane == float(k), es[k] / tot, tk)
        tk = jnp.where(lane == float(TOP_K + k), idxs[k], tk)
        picked = jnp.where(lane == idxs[k], 1.0, picked)
    tk_ref[...] = tk
    cnt_ref[...] = cnt_ref[...] + jnp.sum(picked, axis=0, keepdims=True)


def _merge(h, om, att, x, gm, ga, wo, g1, b1, wr, br, tile, name):
    t = x.shape[0]
    rows = lambda w: pl.BlockSpec((tile, w), lambda i: (i, 0))
    const = lambda a: pl.BlockSpec(a.shape, lambda i: (0, 0))
    return pl.pallas_call(
        _merge_kernel,
        out_shape=[jax.ShapeDtypeStruct((t, D_MODEL), F32), jax.ShapeDtypeStruct((t, XP_TILE, LANES), U32),
                   jax.ShapeDtypeStruct((t, LANES), F32), jax.ShapeDtypeStruct((8, LANES), F32)],
        grid=(t // tile,),
        in_specs=[rows(M_WIDTH), rows(M_WIDTH), rows(A_WIDTH), rows(D_MODEL), const(gm), const(ga), const(wo),
                  const(g1), const(b1), const(wr), const(br)],
        out_specs=[rows(D_MODEL), pl.BlockSpec((tile, XP_TILE, LANES), lambda i: (i, 0, 0)), rows(LANES),
                   pl.BlockSpec((8, LANES), lambda i: (0, 0))],
        compiler_params=_cparams("arbitrary"), name=name,
    )(h, om, att, x, gm, ga, wo, g1, b1, wr, br)


def _route_kernel(tk_ref, first_ref, strict_ref, dest_ref, next_scr):
    @pl.when(pl.program_id(0) == 0)
    def _():
        next_scr[...] = first_ref[...]

    tk = tk_ref[...]
    lane = lax.broadcasted_iota(I32, tk.shape, 1).astype(F32)
    onehots = [jnp.where(lane == tk[:, TOP_K + k:TOP_K + k + 1], 1.0, 0.0) for k in range(TOP_K)]
    tot = onehots[0] + onehots[1] + onehots[2] + onehots[3]
    row = jnp.dot(strict_ref[...], tot.astype(BF16), preferred_element_type=F32) + next_scr[0:1, :]
    out = jnp.zeros(tk.shape, F32)
    for k in range(TOP_K):
        out = jnp.where(lane == float(k), jnp.sum(onehots[k] * row, axis=1, keepdims=True), out)
    dest_ref[...] = out.astype(I32)
    next_scr[...] = next_scr[...] + jnp.sum(tot, axis=0, keepdims=True)


def _route(tk, first):
    t = tk.shape[0]
    tile = min(RANK_TILE, t)
    strict = jnp.asarray(np.tril(np.ones((tile, tile), np.float32), -1), BF16)
    return pl.pallas_call(
        _route_kernel, out_shape=jax.ShapeDtypeStruct((t, LANES), I32), grid=(t // tile,),
        in_specs=[pl.BlockSpec((tile, LANES), lambda i: (i, 0)), pl.BlockSpec((8, LANES), lambda i: (0, 0)),
                  pl.BlockSpec((tile, tile), lambda i: (0, 0))],
        out_specs=pl.BlockSpec((tile, LANES), lambda i: (i, 0)),
        scratch_shapes=[pltpu.VMEM((8, LANES), F32)],
        compiler_params=_cparams("arbitrary"), name="moe_route",
    )(tk, first, strict)


def _offsets_kernel(cnt_ref, off_ref, be_ref, nu_ref, pad_ref, *, tile):
    cnt = cnt_ref[...]
    nblk = jnp.floor((cnt + float(tile - 1)) * (1.0 / tile))
    r_i = lax.broadcasted_iota(I32, (LANES, LANES), 0)
    c_i = lax.broadcasted_iota(I32, (LANES, LANES), 1)
    incl = jnp.where(r_i <= c_i, 1.0, 0.0).astype(BF16)
    cum = jnp.dot(nblk.astype(BF16), incl, preferred_element_type=F32)
    off = (cum - nblk) * float(tile)
    off_ref[...] = off
    which = lax.broadcasted_iota(I32, cnt.shape, 0)
    pad_ref[...] = jnp.where(which == 0, off + cnt, jnp.where(which == 1, nblk * float(tile) - cnt, 0.0)).astype(I32)
    rows = be_ref.shape[0]
    jb = (lax.broadcasted_iota(I32, (rows, LANES), 0) * LANES + lax.broadcasted_iota(I32, (rows, LANES), 1)).astype(F32)
    acc = jnp.zeros((rows, LANES), F32)
    for e in range(N_EXPERTS):
        acc = acc + jnp.where(jb >= cum[0:1, e:e + 1], 1.0, 0.0)
    be_ref[...] = jnp.minimum(acc, float(N_EXPERTS - 1)).astype(I32)
    nu_ref[...] = jnp.broadcast_to(cum[0:1, N_EXPERTS - 1:N_EXPERTS], nu_ref.shape).astype(I32)


def _offsets(cnt, n_blocks, tile):
    rows = -(-n_blocks // LANES)
    rows = -(-rows // 8) * 8
    return pl.pallas_call(
        functools.partial(_offsets_kernel, tile=tile),
        out_shape=[jax.ShapeDtypeStruct((8, LANES), F32), jax.ShapeDtypeStruct((rows, LANES), I32),
                   jax.ShapeDtypeStruct((8, LANES), I32), jax.ShapeDtypeStruct((8, LANES), I32)],
        name="moe_offsets",
    )(cnt)


def _scatter_rows(dest_ref, xp_ref, xs_ref, sem):
    t = xp_ref.shape[0]

    def row_copy(tok, dst):
        return pltpu.make_async_copy(xp_ref.at[pl.ds(tok, 1)], xs_ref.at[pl.ds(dst, 1)], sem)

    def issue(grp, carry):
        base = pl.multiple_of(grp * ISSUE_GROUP, ISSUE_GROUP)
        for u in range(ISSUE_GROUP):
            for k in range(TOP_K):
                row_copy(base + u, dest_ref[(base + u) * TOP_K + k]).start(priority=k % 2)
        return carry

    lax.fori_loop(0, t // ISSUE_GROUP, issue, 0)
    for k in range(TOP_K):
        pltpu.make_async_copy(xp_ref, xs_ref.at[pl.ds(0, t)], sem).wait()


def _dispatch_kernel(dest_ref, pads_ref, dest2_ref, xp_ref, xp2_ref, xs_ref, sem, zsem, zbuf, *, block_rows):
    zr = zbuf.shape[0]
    n_blocks = xs_ref.shape[0] // block_rows

    @pl.when(pl.program_id(0) == 0)
    def _():
        zbuf[...] = jnp.zeros_like(zbuf)
        used = pads_ref[2 * N_EXPERTS]

        def pieces(e, act):
            start, n = pads_ref[e], pads_ref[N_EXPERTS + e]
            for sh in range(zr.bit_length() - 1, -1, -1):
                b = 1 << sh
                before = lax.shift_left(lax.shift_right_logical(n, sh + 1), sh + 1)

                @pl.when((n & b) != 0)
                def _():
                    act(pltpu.make_async_copy(zbuf.at[pl.ds(0, b)], xs_ref.at[pl.ds(start + before, b)], zsem))

        def tail(jb, act):
            for h in range(block_rows // zr):
                act(pltpu.make_async_copy(zbuf, xs_ref.at[pl.ds(jb * block_rows + h * zr, zr)], zsem))

        for act in (lambda cp: cp.start(), lambda cp: cp.wait()):
            lax.fori_loop(0, N_EXPERTS, lambda e, c: (pieces(e, act), c)[1], 0)
            lax.fori_loop(used, n_blocks, lambda jb, c: (tail(jb, act), c)[1], 0)

    _scatter_rows(dest_ref, xp_ref, xs_ref, sem)

    @pl.when(pl.program_id(0) == pl.num_programs(0) - 1)
    def _():
        _scatter_rows(dest2_ref, xp2_ref, xs_ref, sem)


def _dispatch(dest, pads, dest2, xp, xp2, n_rows, block_rows):
    t = xp.shape[0]
    tile = min(ROW_TILE, t)
    return pl.pallas_call(
        functools.partial(_dispatch_kernel, block_rows=block_rows),
        out_shape=jax.ShapeDtypeStruct((n_rows,) + xp.shape[1:], xp.dtype), grid=(t // tile,),
        in_specs=[pl.BlockSpec((tile * TOP_K,), lambda i: (i,), memory_space=pltpu.SMEM),
                  pl.BlockSpec(memory_space=pltpu.SMEM),
                  pl.BlockSpec(memory_space=pltpu.SMEM),
                  pl.BlockSpec((tile,) + xp.shape[1:], lambda i: (i, 0, 0)),
                  pl.BlockSpec(xp2.shape, lambda i: (0, 0, 0))],
        out_specs=pl.BlockSpec(memory_space=pl.ANY),
        scratch_shapes=[pltpu.SemaphoreType.DMA(()), pltpu.SemaphoreType.DMA(()),
                        pltpu.VMEM((EXPERT_TILE // 2,) + xp.shape[1:], xp.dtype)],
        compiler_params=_cparams("arbitrary"), name="moe_dispatch",
    )(dest, pads, dest2, xp, xp2)


def _expert_kernel(be_ref, nu_ref, xs_ref, w1_ref, b1g_ref, b1l_ref, w2_ref, b2_ref, perm_ref, ys_ref,
                   w1g_scr, w1l_scr, w2_scr, xq_scr, y_scr, sem, osem):
    j = pl.program_id(0)
    active = j < nu_ref[0]
    changed = jnp.logical_or(j == 0, be_ref[j] != be_ref[jnp.maximum(j - 1, 0)])
    tm = y_scr.shape[0]
    slot = lax.rem(j, 2)

    def fetch(blk, slot):
        row0 = pl.multiple_of(blk * tm, tm)
        return [pltpu.make_async_copy(xs_ref.at[pl.ds(row0, tm), q, :], xq_scr.at[slot, q], sem.at[slot])
                for q in range(XP_TILE)]

    def put(blk):
        row0 = pl.multiple_of(blk * tm, tm)
        return [pltpu.make_async_copy(y_scr.at[:, q * LANES:(q + 1) * LANES], ys_ref.at[pl.ds(row0, tm), q, :], osem)
                for q in range(YS_TILE)]

    def emit(y):
        @pl.when(j > 0)
        def _():
            for cp in put(j - 1):
                cp.wait()

        y_scr[...] = _pack_halves(y)
        for cp in put(j):
            cp.start()

    @pl.when(j == 0)
    def _():
        for cp in fetch(0, 0):
            cp.start()

    @pl.when(j + 1 < nu_ref[0])
    def _():
        for cp in fetch(j + 1, 1 - slot):
            cp.start()

    @pl.when(jnp.logical_and(active, changed))
    def _():
        for c in range(2 * D_FF // 256):
            wc = w1_ref[0, :, c * 256:(c + 1) * 256].astype(BF16)
            d = jnp.dot(wc, perm_ref[...], preferred_element_type=F32).astype(BF16)
            w1g_scr[:, c * 128:(c + 1) * 128] = d[:, :128]
            w1l_scr[:, c * 128:(c + 1) * 128] = d[:, 128:]
        for c in range(D_FF // 256):
            w2_scr[c * 256:(c + 1) * 256, :] = w2_ref[0, c * 256:(c + 1) * 256, :].astype(BF16)

    @pl.when(active)
    def _():
        for cp in fetch(j, slot):
            cp.wait()
        lo, hi = _unpack_halves(jnp.concatenate([xq_scr[slot, q] for q in range(XP_TILE)], axis=1))
        xb = jnp.concatenate([lo, hi], axis=1)
        hg = jnp.dot(xb, w1g_scr[...], preferred_element_type=F32) + b1g_ref[0]
        hl = jnp.dot(xb, w1l_scr[...], preferred_element_type=F32) + b1l_ref[0]
        x_glu = jnp.minimum(hg, SWIGLU_LIMIT)
        x_lin = jnp.clip(hl, -SWIGLU_LIMIT, SWIGLU_LIMIT)
        a = x_glu * _sigmoid(SWIGLU_ALPHA * x_glu) * (x_lin + 1.0)
        emit(jnp.dot(a.astype(BF16), w2_scr[...], preferred_element_type=F32) + b2_ref[0])

    @pl.when(jnp.logical_not(active))
    def _():
        emit(jnp.zeros((tm, D_MODEL), F32))

    @pl.when(j == pl.num_programs(0) - 1)
    def _():
        for cp in put(j):
            cp.wait()


def _experts(be, nu, xs, w1, b1g, b1l, w2, b2, perm, tile):
    n_blocks = xs.shape[0] // tile
    grid_spec = pltpu.PrefetchScalarGridSpec(
        num_scalar_prefetch=2, grid=(n_blocks,),
        in_specs=[pl.BlockSpec(memory_space=pl.ANY),
                  pl.BlockSpec((1, D_MODEL, 2 * D_FF), lambda j, be, nu: (be[j], 0, 0)),
                  pl.BlockSpec((1, 1, D_FF), lambda j, be, nu: (be[j], 0, 0)),
                  pl.BlockSpec((1, 1, D_FF), lambda j, be, nu: (be[j], 0, 0)),
                  pl.BlockSpec((1, D_FF, D_MODEL), lambda j, be, nu: (be[j], 0, 0)),
                  pl.BlockSpec((1, 1, D_MODEL), lambda j, be, nu: (be[j], 0, 0)),
                  pl.BlockSpec((256, 256), lambda j, be, nu: (0, 0))],
        out_specs=pl.BlockSpec(memory_space=pl.ANY),
        scratch_shapes=[pltpu.VMEM((D_MODEL, D_FF), BF16), pltpu.VMEM((D_MODEL, D_FF), BF16),
                        pltpu.VMEM((D_FF, D_MODEL), BF16), pltpu.VMEM((2, XP_TILE, tile, LANES), U32),
                        pltpu.VMEM((tile, YS_TILE * LANES), U32), pltpu.SemaphoreType.DMA((2,)),
                        pltpu.SemaphoreType.DMA(())])
    return pl.pallas_call(
        _expert_kernel, out_shape=jax.ShapeDtypeStruct((xs.shape[0], YS_TILE, LANES), U32), grid_spec=grid_spec,
        compiler_params=_cparams("arbitrary"), name="moe_experts",
    )(be, nu, xs, w1, b1g, b1l, w2, b2, perm)


def _combine_kernel(dest_ref, next_ref, ys_ref, tk_ref, x1_ref, g2_ref, b2_ref, out_ref, buf, sem):
    i = pl.program_id(0)
    t = x1_ref.shape[0]
    slot = lax.rem(i, 2)

    def gather(idx_ref, s):
        def issue(grp, carry):
            base = pl.multiple_of(grp * 8, 8)
            for u in range(8):
                for k in range(TOP_K):
                    pltpu.make_async_copy(ys_ref.at[idx_ref[(base + u) * TOP_K + k]],
                                          buf.at[s, k, grp, :, u, :], sem.at[s]).start(priority=k % 2)
            return carry

        lax.fori_loop(0, t // 8, issue, 0)

    @pl.when(i == 0)
    def _():
        gather(dest_ref, 0)

    @pl.when(i + 1 < pl.num_programs(0))
    def _():
        gather(next_ref, 1 - slot)

    for k in range(TOP_K):
        for u in range(8):
            pltpu.make_async_copy(ys_ref.at[pl.ds(0, t // 8)], buf.at[slot, k, :, :, u, :], sem.at[slot]).wait()
    tk = tk_ref[...]
    los, his = [], []
    for q in range(YS_TILE):
        lo = hi = None
        for k in range(TOP_K):
            words = buf[slot, k, :, q].reshape(t, LANES)
            g = tk[:, k:k + 1]
            lo_k = g * pltpu.bitcast(words << 16, F32)
            hi_k = g * pltpu.bitcast(words & jnp.uint32(0xFFFF0000), F32)
            lo = lo_k if lo is None else lo + lo_k
            hi = hi_k if hi is None else hi + hi_k
        los.append(lo)
        his.append(hi)
    ff = jnp.concatenate(los + his, axis=1)
    out_ref[...] = _layer_norm(DN_ALPHA * x1_ref[...] + ff, g2_ref[...], b2_ref[...])


def _sc_gather(rows, idx):
    b = idx.shape[0]
    n_workers = SC_CORES * SC_SUBCORES
    assert b % (n_workers * SC_CHUNK) == 0
    per_worker = b // n_workers
    mesh = plsc.VectorSubcoreMesh(core_axis_name="c", subcore_axis_name="s", num_cores=SC_CORES,
                                  num_subcores=SC_SUBCORES)

    @functools.partial(
        pl.kernel, mesh=mesh, out_type=jax.ShapeDtypeStruct((b,) + rows.shape[1:], rows.dtype),
        scratch_types=[pltpu.VMEM((SC_CHUNK,), I32), pltpu.VMEM((SC_CHUNK,) + rows.shape[1:], rows.dtype),
                       pltpu.SemaphoreType.DMA])
    def gather(rows_hbm, idx_hbm, out_hbm, idx_v, rows_v, sem):
        wid = lax.axis_index("s") * SC_CORES + lax.axis_index("c")
        base = wid * per_worker

        @pl.loop(0, per_worker // SC_CHUNK)
        def _(j):
            off = pl.multiple_of(base + j * SC_CHUNK, SC_CHUNK)
            pltpu.sync_copy(idx_hbm.at[pl.ds(off, SC_CHUNK)], idx_v)
            pltpu.async_copy(rows_hbm.at[idx_v], rows_v, sem).wait()
            pltpu.sync_copy(rows_v, out_hbm.at[pl.ds(off, SC_CHUNK)])

    return gather(rows, idx)


def _combine_stream_kernel(g_ref, tk_ref, x1_ref, g2_ref, b2_ref, out_ref, buf, sem):
    i = pl.program_id(0)
    t = x1_ref.shape[0]
    slot = lax.rem(i, 2)

    def fetch(blk, s):
        row0 = pl.multiple_of(blk * t, t)
        return [pltpu.make_async_copy(g_ref.at[k, pl.ds(row0, t), q, :], buf.at[s, k, q], sem.at[s])
                for k in range(TOP_K) for q in range(YS_TILE)]

    @pl.when(i == 0)
    def _():
        for cp in fetch(0, 0):
            cp.start()

    @pl.when(i + 1 < pl.num_programs(0))
    def _():
        for cp in fetch(i + 1, 1 - slot):
            cp.start()

    for cp in fetch(i, slot):
        cp.wait()
    tk = tk_ref[...]
    los, his = [], []
    for q in range(YS_TILE):
        lo = hi = None
        for k in range(TOP_K):
            words = buf[slot, k, q]
            g = tk[:, k:k + 1]
            lo_k = g * pltpu.bitcast(words << 16, F32)
            hi_k = g * pltpu.bitcast(words & jnp.uint32(0xFFFF0000), F32)
            lo = lo_k if lo is None else lo + lo_k
            hi = hi_k if hi is None else hi + hi_k
        los.append(lo)
        his.append(hi)
    ff = jnp.concatenate(los + his, axis=1)
    out_ref[...] = _layer_norm(DN_ALPHA * x1_ref[...] + ff, g2_ref[...], b2_ref[...])


def _combine_stream(g4, tk, x1, g2, b2):
    t = x1.shape[0]
    tile = min(MERGE_TILE, t)
    return pl.pallas_call(
        _combine_stream_kernel, out_shape=jax.ShapeDtypeStruct((t, D_MODEL), F32), grid=(t // tile,),
        in_specs=[pl.BlockSpec(memory_space=pl.ANY),
                  pl.BlockSpec((tile, LANES), lambda i: (i, 0)),
                  pl.BlockSpec((tile, D_MODEL), lambda i: (i, 0)),
                  pl.BlockSpec((1, D_MODEL), lambda i: (0, 0)),
                  pl.BlockSpec((1, D_MODEL), lambda i: (0, 0))],
        out_specs=pl.BlockSpec((tile, D_MODEL), lambda i: (i, 0)),
        scratch_shapes=[pltpu.VMEM((2, TOP_K, YS_TILE, tile, LANES), U32), pltpu.SemaphoreType.DMA((2,))],
        compiler_params=_cparams("arbitrary"), name="moe_combine_stream",
    )(g4, tk, x1, g2, b2)


def _combine(dest_flat, ys, tk, x1, g2, b2):
    t = x1.shape[0]
    tile = min(ROW_TILE, t)
    n = t // tile
    return pl.pallas_call(
        _combine_kernel, out_shape=jax.ShapeDtypeStruct((t, D_MODEL), F32), grid=(n,),
        in_specs=[pl.BlockSpec((tile * TOP_K,), lambda i: (i,), memory_space=pltpu.SMEM),
                  pl.BlockSpec((tile * TOP_K,), lambda i: (jnp.minimum(i + 1, n - 1),), memory_space=pltpu.SMEM),
                  pl.BlockSpec(memory_space=pl.ANY),
                  pl.BlockSpec((tile, LANES), lambda i: (i, 0)),
                  pl.BlockSpec((tile, D_MODEL), lambda i: (i, 0)),
                  pl.BlockSpec((1, D_MODEL), lambda i: (0, 0)),
                  pl.BlockSpec((1, D_MODEL), lambda i: (0, 0))],
        out_specs=pl.BlockSpec((tile, D_MODEL), lambda i: (i, 0)),
        scratch_shapes=[pltpu.VMEM((2, TOP_K, tile // 8, YS_TILE, 8, LANES), U32), pltpu.SemaphoreType.DMA((2,))],
        compiler_params=_cparams("arbitrary"), name="moe_combine",
    )(dest_flat, dest_flat, ys, tk, x1, g2, b2)


def _rel_bucket(dist):
    exact = REL_BUCKETS // 2
    d = np.maximum(dist, 0)
    log_b = exact + (np.log(np.maximum(d, 1).astype(np.float32) / np.float32(exact))
                     / np.float32(math.log(REL_MAX_DIST / exact)) * np.float32(REL_BUCKETS - exact)).astype(np.int32)
    return np.where(d < exact, d, np.minimum(log_b, REL_BUCKETS - 1)).astype(np.int32)


def _bias_lookup(table, bucket, valid):
    bucket = jnp.asarray(bucket)[None]
    acc = jnp.zeros((table.shape[1],) + bucket.shape[1:], F32)
    for b in range(REL_BUCKETS):
        acc = jnp.where(bucket == b, table[b].reshape((-1,) + (1,) * (bucket.ndim - 1)), acc)
    return jnp.where(jnp.asarray(valid)[None], acc, NEG)


def _bias_tables(rel_bias):
    table = rel_bias.astype(F32)
    r = np.arange(WINDOW)[:, None]
    c = np.arange(2 * WINDOW)[None, :]
    dist = r + WINDOW - c
    valid = (dist >= 0) & (dist < WINDOW)
    dist0 = np.where(c < N_META, N_META + r - c, dist)
    valid0 = np.where(c < N_META, dist0 < WINDOW, (c >= WINDOW) & valid)
    both = jnp.stack([_bias_lookup(table, _rel_bucket(dist0), valid0), _bias_lookup(table, _rel_bucket(dist), valid)])
    dist_s = WINDOW - 1 - np.arange(WINDOW)
    rows = _bias_lookup(table[:, np.asarray(HEAD_ORDER)], _rel_bucket(dist_s), np.ones_like(dist_s, bool))
    return both, rows


def _perm_heads(a, axis):
    assert HEAD_ORDER == tuple(kv * A_GROUP + g for g in range(A_GROUP) for kv in range(A_KV_HEADS))
    shape = a.shape
    a = a.reshape(shape[:axis] + (A_KV_HEADS, A_GROUP, A_HD) + shape[axis + 1:])
    return jnp.swapaxes(a, axis, axis + 1).reshape(shape)


def _rep_rows(vec, rows=8):
    out = jnp.zeros((rows, LANES), F32)
    return out.at[:vec.shape[0], :].set(jnp.broadcast_to(vec.astype(F32)[:, None], (vec.shape[0], LANES)))


def kernel(x_prompt, x_sample, cache_swa_k, cache_swa_v, state_mlstm_C, state_mlstm_n, state_mlstm_m, meta_tokens, rel_bias, w_in, b_igate, b_fgate, attn_sinks, g_mlstm_out, g_attn_out, w_out, ln1_g, ln1_b, w_router, b_router, w_moe1, b_moe1, w_moe2, b_moe2, ln2_g, ln2_b):
    B, S, _ = x_prompt.shape
    NB = x_sample.shape[0]
    assert x_sample.shape[1] == 1 and w_in.shape[0] == 1
    assert S % PROJ_TILE == 0 and S % M_CHUNK == 0 and S % WINDOW == 0 and NB % SAMPLE_GROUP == 0
    l = 0

    assert IN_WIDTHS == (512, 512, 512, 512, 4, 4, 512, 128, 128)
    bf = lambda a: a.astype(BF16)
    w = w_in[l]
    n_main, n_gate = 4 * M_WIDTH, 2 * M_HEADS
    w_gate = w[:, n_main:n_main + n_gate]
    w_att = w[:, n_main + n_gate:]
    assert math.frexp(A_HD ** -0.5)[0] == 0.5
    w_qa = _perm_heads(w_att[:, :A_WIDTH], 1) * (A_HD ** -0.5)
    wr = bf(jnp.concatenate([w[:, :n_main], w_qa, w_att[:, A_WIDTH:],
                             jnp.pad(w_gate, ((0, 0), (0, LANES - n_gate)))], axis=1))
    wt = bf(jnp.concatenate([w[:, :M_WIDTH], w[:, 2 * M_WIDTH:3 * M_WIDTH], w_qa, w_att[:, A_WIDTH + LANES:],
                             w_gate], axis=1).T)
    b_gate = jnp.concatenate([b_igate[l], b_fgate[l]]).astype(F32)
    brow = jnp.pad(b_gate, (0, LANES - n_gate))[None, :]
    bcol = b_gate[:, None]
    plan_p = ((512, 512, "plain", BF16), (1536, 512, "plain", F32), (2560, 128, "plain", BF16),
              (2816, 128, "gate", F32))
    tplan_p = ((0, 512, "plain", BF16), (512, 512, "plain", BF16), (1024, 512, "plain", BF16),
               (1536, 128, "plain", BF16), (1664, 8, "gate", F32))
    plan_s = ((0, 512, "plain", F32), (512, 512, "plain", F32), (1024, 512, "plain", F32), (1536, 512, "plain", F32),
              (2048, 512, "plain", F32), (2560, 128, "plain", F32), (2688, 128, "plain", F32), (2816, 128, "gate", F32))

    bias_tab, bias_rows = _bias_tables(rel_bias)
    sinks = _rep_rows(attn_sinks[l])
    sinks_step = _rep_rows(attn_sinks[l][np.asarray(HEAD_ORDER)])
    g_m = g_mlstm_out[l].astype(F32)[None, :]
    g_a = _perm_heads(g_attn_out[l].astype(F32), 0)[None, :]
    wo = bf(jnp.concatenate([w_out[l][:M_WIDTH], _perm_heads(w_out[l][M_WIDTH:], 0)], axis=0))
    g1, b1 = ln1_g[l].astype(F32)[None, :], ln1_b[l].astype(F32)[None, :]
    g2, b2 = ln2_g[l].astype(F32)[None, :], ln2_b[l].astype(F32)[None, :]
    w_r = bf(jnp.pad(w_router[l], ((0, 0), (0, LANES - N_EXPERTS))))
    b_r = jnp.pad(b_router[l].astype(F32), (0, LANES - N_EXPERTS), constant_values=NEG)[None, :]
    b1g = b_moe1[l][:, 0::2].astype(F32)[:, None, :]
    b1l = b_moe1[l][:, 1::2].astype(F32)[:, None, :]
    b2e = b_moe2[l].astype(F32)[:, None, :]
    pj = np.zeros((256, 256), np.float32)
    pj[2 * np.arange(128), np.arange(128)] = 1.0
    pj[2 * np.arange(128) + 1, 128 + np.arange(128)] = 1.0
    perm = jnp.asarray(pj, BF16)

    xp2 = x_prompt.reshape(B * S, D_MODEL)
    km, om, ka, gc, qt, vt, qat, vat, gr, kv_tail = _proj(
        xp2, wr, wt, brow, bcol, plan_p, tplan_p, (2560, 256), PROJ_TILE, S, "proj_prompt")
    x_meta = jnp.pad(meta_tokens.astype(F32), ((0, M_CHUNK - N_META), (0, 0)))
    km0, _, ka0, gc0, qt0, vt0, _, vat0, gr0 = _proj(
        x_meta, wr, wt, brow, bcol, plan_p, tplan_p, None, M_CHUNK, M_CHUNK, "proj_meta")
    xs2 = x_sample.reshape(NB, D_MODEL)
    qm_s, km_s, vm_s, om_s, qa_s, ka_s, va_s, gc_s = _proj(
        xs2, wr, wt, brow, bcol, plan_s, (), None, NB, NB, "proj_sample")

    zero_c = jnp.zeros((M_HEADS, M_DV + 8, M_DK), F32)
    zero_m = jnp.zeros((8, LANES), F32)
    _, c_meta, m_meta = _mlstm(qt0, km0, vt0, gc0, gr0, zero_c, zero_m, 1, N_META, "mlstm_meta")
    h_p, c_p, m_p = _mlstm(qt, km, vt, gc, gr, c_meta[0], m_meta[0], B, M_CHUNK, "mlstm_prompt")
    C_p = c_p[:, :, :M_DV, :]
    n_p = c_p[:, :, M_DV, :]
    m_prompt = m_p[:, :M_HEADS, 0]
    m_pad = jnp.pad(state_mlstm_m[l].astype(F32), ((0, 0), (0, LANES - M_HEADS)))
    C_s, n_s, m_s, h_s = _mlstm_step(state_mlstm_C[l].astype(F32), state_mlstm_n[l].astype(F32), m_pad,
                                     gc_s, qm_s, km_s, vm_s)

    att_p = _swa(qat, ka, vat, ka0, vat0, jnp.swapaxes(bias_tab, 2, 3), sinks, B)
    ck = cache_swa_k[l].reshape(NB, WINDOW, LANES)
    cv = cache_swa_v[l].reshape(NB, WINDOW, LANES)
    k_new, v_new, att_s = _swa_step(ck, cv, qa_s, ka_s, va_s, bias_rows, sinks_step)

    x1_p, xpk_p, tk_p, cnt_p = _merge(h_p, om, att_p, xp2, g_m, g_a, wo, g1, b1, w_r, b_r, MERGE_TILE, "merge_prompt")
    x1_s, xpk_s, tk_s, cnt_s = _merge(h_s, om_s, att_s, xs2, g_m, g_a, wo, g1, b1, w_r, b_r, NB, "merge_sample")

    T_p = B * S
    assert T_p % RANK_TILE == 0 and T_p % ROW_TILE == 0
    n_blocks = -(-((T_p + NB) * TOP_K) // EXPERT_TILE) + N_EXPERTS
    off, be2, nu2, pad = _offsets(cnt_p + cnt_s, n_blocks, EXPERT_TILE)
    pads = jnp.concatenate([pad[0, :N_EXPERTS], pad[1, :N_EXPERTS], nu2[0, :1]])
    dest_p = _route(tk_p, off)[:, :TOP_K].reshape(-1)
    dest_s = _route(tk_s, off + cnt_p)[:, :TOP_K].reshape(-1)
    be = be2.reshape(-1)[:n_blocks]
    nu = nu2[0, :1]
    xs = _dispatch(dest_p, pads, dest_s, xpk_p, xpk_s, n_blocks * EXPERT_TILE, EXPERT_TILE)
    ys = _experts(be, nu, xs, w_moe1[l], b1g, b1l, w_moe2[l], b2e, perm, EXPERT_TILE)
    idx_p = dest_p.reshape(T_p, TOP_K).T.reshape(-1)
    g4 = _sc_gather(ys, idx_p).reshape(TOP_K, T_p, YS_TILE, LANES)
    y_p = _combine_stream(g4, tk_p, x1_p, g2, b2)
    y_s = _combine(dest_s, ys, tk_s, x1_s, g2, b2)

    kv_tail = kv_tail.reshape(B, WINDOW, 2, A_KV_HEADS, A_HD)
    dt_k, dt_v = cache_swa_k.dtype, cache_swa_v.dtype
    return (y_p.reshape(B, S, D_MODEL).astype(x_prompt.dtype), y_s.reshape(NB, 1, D_MODEL).astype(x_sample.dtype),
            kv_tail[:, :, 0][None].astype(dt_k), kv_tail[:, :, 1][None].astype(dt_v),
            C_p[None].astype(state_mlstm_C.dtype), n_p[None].astype(state_mlstm_n.dtype),
            m_prompt[None].astype(state_mlstm_m.dtype),
            k_new.reshape(1, NB, WINDOW, A_KV_HEADS, A_HD).astype(dt_k),
            v_new.reshape(1, NB, WINDOW, A_KV_HEADS, A_HD).astype(dt_v),
            C_s[None].astype(state_mlstm_C.dtype), n_s[None].astype(state_mlstm_n.dtype),
            m_s[:, :M_HEADS][None].astype(state_mlstm_m.dtype))
```

```python
import functools
import math

import numpy as np
import jax
import jax.numpy as jnp
from jax import lax
from jax.experimental import pallas as pl
from jax.experimental.pallas import tpu as pltpu
from jax.experimental.pallas import tpu_sc as plsc

F32 = jnp.float32
BF16 = jnp.bfloat16
I32 = jnp.int32
U32 = jnp.uint32

D_MODEL = 1024
N_META = 16
M_HEADS = 4
M_DK = 128
M_DV = 128
M_WIDTH = M_HEADS * M_DV
A_HD = 64
A_HEADS = 8
A_KV_HEADS = 2
A_GROUP = A_HEADS // A_KV_HEADS
A_WIDTH = A_HEADS * A_HD
WINDOW = 128
REL_BUCKETS = 32
REL_MAX_DIST = 128
N_EXPERTS = 32
TOP_K = 4
D_FF = D_MODEL
SWIGLU_LIMIT = 7.0
SWIGLU_ALPHA = 1.702
DEPTH = 1
DN_ALPHA = (2.0 * DEPTH) ** 0.25
LN_EPS = 1e-5
IN_WIDTHS = (M_WIDTH, M_WIDTH, M_WIDTH, M_WIDTH, M_HEADS, M_HEADS, A_WIDTH, A_KV_HEADS * A_HD, A_KV_HEADS * A_HD)

LANES = 128
NEG = -1e30
VMEM_LIMIT = 56 * 1024 * 1024

M_CHUNK = 256
PROJ_TILE = 1024
MERGE_TILE = 1024
RANK_TILE = 1024
ROW_TILE = 512
EXPERT_TILE = 512
SAMPLE_GROUP = 8
ISSUE_GROUP = 8
SWA_QBLOCKS = 8
SC_CORES, SC_SUBCORES = 2, 16
SC_CHUNK = 128
MLSTM_SEQS = 4
STEP_PIPE_LAG = 3
SWA_PIPE_LAG = 2
XP_TILE = D_MODEL // 2 // LANES
YS_TILE = D_MODEL // 2 // LANES
HEAD_ORDER = (0, 4, 1, 5, 2, 6, 3, 7)


def _cparams(*sem):
    return pltpu.CompilerParams(dimension_semantics=sem, vmem_limit_bytes=VMEM_LIMIT)


def _log_sigmoid(x):
    return jnp.minimum(x, 0.0) - jnp.log1p(jnp.exp(-jnp.abs(x)))


def _sigmoid(x):
    return 1.0 / (1.0 + jnp.exp(-x))


def _proj_kernel(x_ref, wr_ref, wt_ref, brow_ref, bcol_ref, *outs, row_plan, t_plan, tail_cols):
    xb = x_ref[...].astype(BF16)
    tm = xb.shape[0]
    o = 0
    for (c0, width, kind, _) in row_plan:
        r = jnp.dot(xb, wr_ref[:, c0:c0 + width], preferred_element_type=F32)
        if kind == "gate":
            r = r + brow_ref[...]
            lane = lax.broadcasted_iota(I32, r.shape, 1)
            r = jnp.where(lane < M_HEADS, r, _log_sigmoid(r))
        outs[o][...] = r.astype(outs[o].dtype)
        o += 1
    for (r0, nrows, kind, _) in t_plan:
        r = lax.dot_general(wt_ref[r0:r0 + nrows, :], xb, (((1,), (1,)), ((), ())), preferred_element_type=F32)
        if kind == "gate":
            r = r + bcol_ref[...]
            row = lax.broadcasted_iota(I32, r.shape, 0)
            r = jnp.where(row < M_HEADS, r, _log_sigmoid(r))
        outs[o][...] = r.astype(outs[o].dtype)
        o += 1
    if tail_cols is not None:
        c0, width = tail_cols
        outs[o][...] = jnp.dot(xb[tm - WINDOW:, :], wr_ref[:, c0:c0 + width], preferred_element_type=F32)


def _proj(x, wr, wt, brow, bcol, row_plan, t_plan, tail_cols, tile, rows_per_group, name):
    t = x.shape[0]
    nt = t // tile
    out_shape, out_specs = [], []
    for (_, width, _, dt) in row_plan:
        out_shape.append(jax.ShapeDtypeStruct((t, width), dt))
        out_specs.append(pl.BlockSpec((tile, width), lambda i: (i, 0)))
    for (_, nrows, _, dt) in t_plan:
        out_shape.append(jax.ShapeDtypeStruct((nrows, t), dt))
        out_specs.append(pl.BlockSpec((nrows, tile), lambda i: (0, i)))
    if tail_cols is not None:
        tiles_per_group = rows_per_group // tile
        out_shape.append(jax.ShapeDtypeStruct((t // rows_per_group * WINDOW, tail_cols[1]), F32))
        out_specs.append(pl.BlockSpec((WINDOW, tail_cols[1]), lambda i: (i // tiles_per_group, 0)))
    kern = functools.partial(_proj_kernel, row_plan=row_plan, t_plan=t_plan, tail_cols=tail_cols)
    return pl.pallas_call(
        kern, out_shape=out_shape, grid=(nt,),
        in_specs=[pl.BlockSpec((tile, D_MODEL), lambda i: (i, 0)),
                  pl.BlockSpec(wr.shape, lambda i: (0, 0)),
                  pl.BlockSpec(wt.shape, lambda i: (0, 0)),
                  pl.BlockSpec(brow.shape, lambda i: (0, 0)),
                  pl.BlockSpec(bcol.shape, lambda i: (0, 0))],
        out_specs=out_specs, compiler_params=_cparams("arbitrary"), name=name,
    )(x, wr, wt, brow, bcol)


def _split3(a):
    hi = a.astype(BF16)
    r1 = a - hi.astype(F32)
    mid = r1.astype(BF16)
    lo = (r1 - mid.astype(F32)).astype(BF16)
    return hi, mid, lo


def _mlstm_kernel(*refs, n_valid, nseq):
    seq_in = [refs[5 * i:5 * i + 5] for i in range(nseq)]
    c0_ref, m0_ref = refs[5 * nseq:5 * nseq + 2]
    h_ref, c_out_ref, m_out_ref, c_scr, m_scr = refs[5 * nseq + 2:]
    c = pl.program_id(1)
    nc = pl.num_programs(1)
    L = seq_in[0][1].shape[0]

    @pl.when(c == 0)
    def _():
        for i in range(nseq):
            c_scr[i] = c0_ref[...]
            m_scr[i] = m0_ref[...]

    r_i = lax.broadcasted_iota(I32, (L, L), 0)
    c_i = lax.broadcasted_iota(I32, (L, L), 1)
    upper = r_i <= c_i
    tril = jnp.where(c_i <= r_i, 1.0, 0.0).astype(BF16)
    triu = jnp.where(upper, 1.0, 0.0).astype(BF16)
    scale = M_DK ** -0.5
    ones_rows = jnp.where(lax.broadcasted_iota(I32, (8, L), 0) == 0, 1.0, 0.0).astype(BF16)

    gates = []
    for (_, _, _, gc_ref, gr_ref) in seq_in:
        gc = gc_ref[...]
        gr = gr_ref[...]
        if n_valid < L:
            rowc = lax.broadcasted_iota(I32, gc.shape, 0)
            lanec = lax.broadcasted_iota(I32, gc.shape, 1)
            gc = jnp.where(rowc < n_valid, gc, jnp.where(lanec < M_HEADS, NEG, 0.0))
            rowr = lax.broadcasted_iota(I32, gr.shape, 0)
            colr = lax.broadcasted_iota(I32, gr.shape, 1)
            gr = jnp.where(colr < n_valid, gr, jnp.where(rowr < M_HEADS, NEG, 0.0))
        b_cols = sum(jnp.dot(tril, part, preferred_element_type=F32) for part in _split3(gc))
        b_rows = sum(jnp.dot(part, triu, preferred_element_type=F32) for part in _split3(gr))
        gates.append((gc, gr, b_cols, b_rows))

    m_alls = [m_scr[i] for i in range(nseq)]
    c_alls = [[c_scr[i, h] for h in range(M_HEADS)] for i in range(nseq)]
    h_new, c_new, m_new_all = {}, {}, {}

    def operands(u):
        i, h = u
        qt_ref, k_ref, vt_ref = seq_in[i][:3]
        sl = slice(h * M_DK, (h + 1) * M_DK)
        return qt_ref[sl, :], k_ref[:, sl], jnp.concatenate([vt_ref[sl, :], ones_rows], axis=0)

    def stage_a(u):
        i, h = u
        gc, gr, b_cols, b_rows = gates[i]
        qt, k, vt_aug = operands(u)
        ig_r = gr[h:h + 1, :]
        b_r = b_rows[M_HEADS + h:M_HEADS + h + 1, :]
        m_prev = m_alls[i][h:h + 1, 0:1]
        cs = c_alls[i][h]
        qk = jnp.dot(k, qt, preferred_element_type=F32)
        inter = jnp.dot(cs.astype(BF16), qt, preferred_element_type=F32)
        b_last = b_r[:, L - 1:L]
        g = ig_r + b_last - b_r
        m_new = jnp.maximum(b_last + m_prev, jnp.max(g, axis=1, keepdims=True))
        a = jnp.exp(b_last + m_prev - m_new)
        wv = (vt_aug.astype(F32) * jnp.exp(g - m_new)).astype(BF16)
        c_new[u] = a * cs + jnp.dot(wv, k, preferred_element_type=F32) * scale
        m_new_all[u] = jnp.broadcast_to(m_new, (1, LANES))
        return qk, inter

    def stage_b(u, qk, inter):
        i, h = u
        gc, gr, b_cols, b_rows = gates[i]
        b_r = b_rows[M_HEADS + h:M_HEADS + h + 1, :]
        m_prev = m_alls[i][h:h + 1, 0:1]
        r_c = gc[:, h:h + 1] - b_cols[:, M_HEADS + h:M_HEADS + h + 1]
        dt = jnp.where(upper, b_r + r_c, NEG)
        m_t = jnp.maximum(b_r + m_prev, jnp.max(dt, axis=0, keepdims=True))
        st = (qk * (scale * jnp.exp(dt - m_t))).astype(BF16)
        return st, jnp.exp(b_r + m_prev - m_t) * inter, jnp.exp(-m_t)

    def stage_c(u, st, inter_w, floor):
        _, _, vt_aug = operands(u)
        nd = inter_w + jnp.dot(vt_aug, st, preferred_element_type=F32)
        den = nd[M_DV:M_DV + 1, :]
        h_new[u] = (nd[:M_DV, :] / jnp.maximum(jnp.abs(den), floor)).T

    units = [(i, h) for h in range(M_HEADS) for i in range(nseq)]
    a_q, b_q = {}, {}
    for n in range(len(units) + 2):
        if n < len(units):
            a_q[n] = stage_a(units[n])
        if 0 <= n - 1 < len(units):
            b_q[n - 1] = stage_b(units[n - 1], *a_q.pop(n - 1))
        if 0 <= n - 2 < len(units):
            stage_c(units[n - 2], *b_q.pop(n - 2))

    for i in range(nseq):
        h_ref[0, i] = jnp.concatenate([h_new[(i, h)] for h in range(M_HEADS)], axis=1)
        for h in range(M_HEADS):
            c_scr[i, h] = c_new[(i, h)]
        m_scr[i, 0:M_HEADS, :] = jnp.concatenate([m_new_all[(i, h)] for h in range(M_HEADS)], axis=0)

    @pl.when(c == nc - 1)
    def _():
        c_out_ref[...] = c_scr[...]
        m_out_ref[...] = m_scr[...]


def _mlstm(qt, km, vt, gc, gr, c0, m0, batch, n_valid, name):
    L = M_CHUNK
    nc = km.shape[0] // (batch * L)
    nseq = MLSTM_SEQS if batch % MLSTM_SEQS == 0 else 1
    kern = functools.partial(_mlstm_kernel, n_valid=n_valid, nseq=nseq)
    in_specs, operands = [], []
    for i in range(nseq):
        blk = functools.partial(lambda b, c, i: (b * nseq + i) * nc + c, i=i)
        rows = pl.BlockSpec((L, M_WIDTH), functools.partial(lambda b, c, blk: (blk(b, c), 0), blk=blk))
        cols = pl.BlockSpec((M_WIDTH, L), functools.partial(lambda b, c, blk: (0, blk(b, c)), blk=blk))
        in_specs += [cols, rows, cols,
                     pl.BlockSpec((L, LANES), functools.partial(lambda b, c, blk: (blk(b, c), 0), blk=blk)),
                     pl.BlockSpec((8, L), functools.partial(lambda b, c, blk: (0, blk(b, c)), blk=blk))]
        operands += [qt, km, vt, gc, gr]
    in_specs += [pl.BlockSpec((M_HEADS, M_DV + 8, M_DK), lambda b, c: (0, 0, 0)),
                 pl.BlockSpec((8, LANES), lambda b, c: (0, 0))]
    h4, c_fin, m_fin = pl.pallas_call(
        kern,
        out_shape=[jax.ShapeDtypeStruct((batch // nseq, nseq, nc * L, M_WIDTH), F32),
                   jax.ShapeDtypeStruct((batch, M_HEADS, M_DV + 8, M_DK), F32),
                   jax.ShapeDtypeStruct((batch, 8, LANES), F32)],
        grid=(batch // nseq, nc),
        in_specs=in_specs,
        out_specs=[pl.BlockSpec((1, nseq, L, M_WIDTH), lambda b, c: (b, 0, c, 0)),
                   pl.BlockSpec((nseq, M_HEADS, M_DV + 8, M_DK), lambda b, c: (b, 0, 0, 0)),
                   pl.BlockSpec((nseq, 8, LANES), lambda b, c: (b, 0, 0))],
        scratch_shapes=[pltpu.VMEM((nseq, M_HEADS, M_DV + 8, M_DK), F32), pltpu.VMEM((nseq, 8, LANES), F32)],
        compiler_params=_cparams("arbitrary", "arbitrary"), name=name,
    )(*operands, c0, m0)
    return h4.reshape(batch * nc * L, M_WIDTH), c_fin, m_fin


def _outer_f32(a, b):
    ah, am, al = (t.astype(F32) for t in _split3(a))
    bh, bm, bl = (t.astype(F32) for t in _split3(b))
    z = jnp.zeros_like(ah)
    lhs = jnp.concatenate([ah, ah, ah, am, am, al, z, z], axis=0).astype(BF16)
    rhs = jnp.concatenate([bh, bm, bl, bh, bm, bh, z, z], axis=0).astype(BF16)
    return lax.dot_general(lhs, rhs, (((0,), (0,)), ((), ())), preferred_element_type=F32)


def _mlstm_step_kernel(c_ref, n_ref, m_ref, gc_ref, q_ref, k_ref, v_ref,
                       c_out_ref, n_out_ref, m_out_ref, h_ref):
    g = c_ref.shape[0]
    assert g == 8
    scale = M_DK ** -0.5
    ig = gc_ref[:, 0:M_HEADS]
    lf = gc_ref[:, M_HEADS:2 * M_HEADS]
    m = m_ref[:, 0:M_HEADS]
    m_t = jnp.maximum(lf + m, ig)
    w = jnp.exp(lf + m - m_t)
    wg = jnp.exp(ig - m_t)
    floor = jnp.exp(-m_t)
    m_out_ref[...] = jnp.zeros_like(m_out_ref)
    m_out_ref[:, 0:M_HEADS] = m_t
    row8 = lax.broadcasted_iota(I32, (g, M_DV), 0)

    per_head = []
    for h in range(M_HEADS):
        sl = slice(h * M_DK, (h + 1) * M_DK)
        q = q_ref[:, sl]
        k = k_ref[:, sl] * scale
        v = v_ref[:, sl]
        n = n_ref[:, h, :]
        w_h, wg_h = w[:, h:h + 1], wg[:, h:h + 1]
        s = jnp.sum(q * k, axis=1, keepdims=True) * wg_h
        den = w_h * jnp.sum(n * q, axis=1, keepdims=True) + s
        n_out_ref[:, h, :] = w_h * n + wg_h * k
        per_head.append((q.astype(BF16), k, wg_h * v, w_h, s * v, 1.0 / jnp.maximum(jnp.abs(den), floor[:, h:h + 1])))

    def stage_a(h, j):
        qb = per_head[h][0]
        r = lax.dot_general(qb, c_ref[j, h].astype(BF16), (((1,), (1,)), ((), ())), preferred_element_type=F32)
        return jnp.where(row8 == j, r, 0.0)

    def stage_b(h, j):
        _, k, wv, w_h, _, _ = per_head[h]
        c_out_ref[j, h] = w_h[j:j + 1, :] * c_ref[j, h] + _outer_f32(wv[j:j + 1, :], k[j:j + 1, :])

    units = [(h, j) for h in range(M_HEADS) for j in range(g)]
    lag = STEP_PIPE_LAG
    cq = [jnp.zeros((g, M_DV), F32) for _ in range(M_HEADS)]
    for i in range(len(units) + lag):
        if i < len(units):
            cq[units[i][0]] = cq[units[i][0]] + stage_a(*units[i])
        if 0 <= i - lag < len(units):
            stage_b(*units[i - lag])
    for h in range(M_HEADS):
        _, _, _, w_h, sv, inv = per_head[h]
        h_ref[:, h * M_DK:(h + 1) * M_DK] = (w_h * cq[h] + sv) * inv


def _mlstm_step(c, n, m_pad, gc, q, k, v):
    nb = c.shape[0]
    g = SAMPLE_GROUP
    row = lambda w: pl.BlockSpec((g, w), lambda i: (i, 0))
    return pl.pallas_call(
        _mlstm_step_kernel,
        out_shape=[jax.ShapeDtypeStruct(c.shape, F32), jax.ShapeDtypeStruct(n.shape, F32),
                   jax.ShapeDtypeStruct((nb, LANES), F32), jax.ShapeDtypeStruct((nb, M_WIDTH), F32)],
        grid=(nb // g,),
        in_specs=[pl.BlockSpec((g, M_HEADS, M_DV, M_DK), lambda i: (i, 0, 0, 0)),
                  pl.BlockSpec((g, M_HEADS, M_DK), lambda i: (i, 0, 0)),
                  row(LANES), row(LANES), row(M_WIDTH), row(M_WIDTH), row(M_WIDTH)],
        out_specs=[pl.BlockSpec((g, M_HEADS, M_DV, M_DK), lambda i: (i, 0, 0, 0)),
                   pl.BlockSpec((g, M_HEADS, M_DK), lambda i: (i, 0, 0)),
                   row(LANES), row(M_WIDTH)],
        compiler_params=_cparams("arbitrary"), name="mlstm_step",
    )(c, n, m_pad, gc, q, k, v)


def _swa_kernel(qt_ref, kc_ref, kp_ref, vtc_ref, vtp_ref, km_ref, vtm_ref, bias_ref, sink_ref, o_ref):
    j = pl.program_id(1)
    first = j == 0
    blk = WINDOW
    nqb = qt_ref.shape[1] // blk
    kp = jnp.where(first, km_ref[...], kp_ref[...])
    vtp = jnp.where(first, vtm_ref[...], vtp_ref[...])
    k = jnp.concatenate([kp, kc_ref[...]], axis=0)
    vt = jnp.concatenate([vtp, vtc_ref[...]], axis=1)
    row_v = lax.broadcasted_iota(I32, vt.shape, 0)
    zero_v = jnp.zeros_like(vt)
    vt_half = (jnp.where(row_v < A_HD, vt, zero_v), jnp.where(row_v >= A_HD, vt, zero_v))
    row_q = lax.broadcasted_iota(I32, (LANES, blk), 0)
    lo_rows = row_q < A_HD
    def scores(u, p):
        cols = slice(u * blk, (u + 1) * blk)
        keys = slice(u * blk, (u + 2) * blk)
        qs = qt_ref[p * LANES:(p + 1) * LANES, cols]
        zero_q = jnp.zeros_like(qs)
        q_own = (jnp.where(lo_rows, qs, zero_q), jnp.where(lo_rows, zero_q, qs))
        return [jnp.dot(k[keys], q_own[half], preferred_element_type=F32) for half in range(2)]

    def softmax(u, p, s2):
        table = jnp.where(first, 0, 1) if u == 0 else 1
        probs, inv = [], []
        for half in range(2):
            hd = HEAD_ORDER[2 * p + half]
            s = s2[half] + bias_ref[table, hd]
            sk = sink_ref[hd:hd + 1, 0:1]
            m = jnp.maximum(jnp.max(s, axis=0, keepdims=True), sk)
            e = jnp.exp(s - m)
            probs.append(e.astype(BF16))
            inv.append(1.0 / (jnp.sum(e, axis=0, keepdims=True) + jnp.exp(sk - m)))
        return jnp.concatenate(probs, axis=0), jnp.where(lo_rows, inv[0], inv[1])

    def values(u, p, probs, inv):
        keys = slice(u * blk, (u + 2) * blk)
        vt_stack = jnp.concatenate([vt_half[0][:, keys], vt_half[1][:, keys]], axis=1)
        ot = jnp.dot(vt_stack, probs, preferred_element_type=F32)
        o_ref[u * blk:(u + 1) * blk, p * LANES:(p + 1) * LANES] = (ot * inv).T

    units = [(u, p) for u in range(nqb) for p in range(A_GROUP)]
    s_q, p_q = {}, {}
    lag = SWA_PIPE_LAG
    for i in range(len(units) + 2 * lag):
        if i < len(units):
            s_q[i] = scores(*units[i])
        if 0 <= i - lag < len(units):
            p_q[i - lag] = softmax(*units[i - lag], s_q.pop(i - lag))
        if 0 <= i - 2 * lag < len(units):
            values(*units[i - 2 * lag], *p_q.pop(i - 2 * lag))


def _swa(qat, ka, vat, kmeta, vtmeta, bias_t, sinks, batch):
    blk = WINDOW
    nqb = SWA_QBLOCKS
    t = ka.shape[0]
    nq = t // (batch * blk * nqb)
    prev = lambda b, j: (b * nq + j) * nqb + jnp.where(j == 0, 0, -1)
    const2 = lambda shape: pl.BlockSpec(shape, lambda b, j: (0, 0))
    return pl.pallas_call(
        _swa_kernel, out_shape=jax.ShapeDtypeStruct((t, A_WIDTH), F32), grid=(batch, nq),
        in_specs=[pl.BlockSpec((A_WIDTH, nqb * blk), lambda b, j: (0, b * nq + j)),
                  pl.BlockSpec((nqb * blk, LANES), lambda b, j: (b * nq + j, 0)),
                  pl.BlockSpec((blk, LANES), lambda b, j: (prev(b, j), 0)),
                  pl.BlockSpec((LANES, nqb * blk), lambda b, j: (0, b * nq + j)),
                  pl.BlockSpec((LANES, blk), lambda b, j: (0, prev(b, j))),
                  const2((blk, LANES)), const2((LANES, blk)),
                  pl.BlockSpec(bias_t.shape, lambda b, j: (0, 0, 0, 0)),
                  const2((8, LANES))],
        out_specs=pl.BlockSpec((nqb * blk, A_WIDTH), lambda b, j: (b * nq + j, 0)),
        compiler_params=_cparams("arbitrary", "arbitrary"), name="swa_prompt",
    )(qat, ka, ka, vat, vat, kmeta, vtmeta, bias_t, sinks)


def _swa_step_kernel(ck_ref, cv_ref, q_ref, k_ref, v_ref, bias_ref, sink_ref, ko_ref, vo_ref, o_ref):
    g = ck_ref.shape[0]
    lane = lax.broadcasted_iota(I32, (A_HEADS, LANES), 1)
    row = lax.broadcasted_iota(I32, (A_HEADS, LANES), 0)
    own_half = (row % 2 == 0) == (lane < A_HD)
    bias = bias_ref[...]
    sk = sink_ref[:, 0:1]
    def stage_a(j):
        ko_ref[j, 0:WINDOW - 1, :] = ck_ref[j, 1:WINDOW, :]
        ko_ref[j, WINDOW - 1:WINDOW, :] = k_ref[j:j + 1, :]
        vo_ref[j, 0:WINDOW - 1, :] = cv_ref[j, 1:WINDOW, :]
        vo_ref[j, WINDOW - 1:WINDOW, :] = v_ref[j:j + 1, :]
        kk = ko_ref[j].astype(BF16)
        slabs = [q_ref[j:j + 1, p * LANES:(p + 1) * LANES] for p in range(A_GROUP)]
        q8 = jnp.concatenate([slabs[r // 2] for r in range(A_HEADS)], axis=0)
        q8 = jnp.where(own_half, q8, 0.0).astype(BF16)
        s = lax.dot_general(q8, kk, (((1,), (1,)), ((), ())), preferred_element_type=F32)
        s = s + bias
        m = jnp.maximum(jnp.max(s, axis=1, keepdims=True), sk)
        e = jnp.exp(s - m)
        return e.astype(BF16), 1.0 / (jnp.sum(e, axis=1, keepdims=True) + jnp.exp(sk - m))

    def stage_b(j, p8, inv):
        vv = vo_ref[j].astype(BF16)
        o8 = jnp.where(own_half, jnp.dot(p8, vv, preferred_element_type=F32) * inv, 0.0)
        for p in range(A_GROUP):
            o_ref[j:j + 1, p * LANES:(p + 1) * LANES] = o8[2 * p:2 * p + 1, :] + o8[2 * p + 1:2 * p + 2, :]

    lag, pending = STEP_PIPE_LAG, {}
    for i in range(g + lag):
        if i < g:
            pending[i] = stage_a(i)
        if 0 <= i - lag < g:
            stage_b(i - lag, *pending.pop(i - lag))


def _swa_step(ck, cv, q, k, v, bias_rows, sinks):
    nb = ck.shape[0]
    g = SAMPLE_GROUP
    cache = pl.BlockSpec((g, WINDOW, LANES), lambda i: (i, 0, 0))
    row = lambda w: pl.BlockSpec((g, w), lambda i: (i, 0))
    const = lambda a: pl.BlockSpec(a.shape, lambda i: (0, 0))
    return pl.pallas_call(
        _swa_step_kernel,
        out_shape=[jax.ShapeDtypeStruct(ck.shape, F32), jax.ShapeDtypeStruct(cv.shape, F32),
                   jax.ShapeDtypeStruct((nb, A_WIDTH), F32)],
        grid=(nb // g,),
        in_specs=[cache, cache, row(A_WIDTH), row(LANES), row(LANES), const(bias_rows), const(sinks)],
        out_specs=[cache, cache, row(A_WIDTH)],
        compiler_params=_cparams("arbitrary"), name="swa_step",
    )(ck, cv, q, k, v, bias_rows, sinks)


def _layer_norm(z, g, b):
    mu = jnp.mean(z, axis=1, keepdims=True)
    zc = z - mu
    var = jnp.mean(zc * zc, axis=1, keepdims=True)
    return zc * lax.rsqrt(var + LN_EPS) * g + b


def _pack_halves(x):
    w = x.shape[1] // 2
    lo = pltpu.bitcast(x[:, :w].astype(BF16).astype(F32), U32)
    hi = pltpu.bitcast(x[:, w:].astype(BF16).astype(F32), U32)
    return (lo >> 16) | (hi & jnp.uint32(0xFFFF0000))


def _unpack_halves(words):
    lo = pltpu.bitcast(words << 16, F32).astype(BF16)
    hi = pltpu.bitcast(words & jnp.uint32(0xFFFF0000), F32).astype(BF16)
    return lo, hi


def _to_token_tiles(ref, x):
    for q in range(x.shape[1] // LANES):
        ref[:, q, :] = x[:, q * LANES:(q + 1) * LANES]


def _merge_kernel(h_ref, om_ref, att_ref, x_ref, gm_ref, ga_ref, wo_ref, g1_ref, b1_ref, wr_ref, br_ref,
                  x1_ref, xp_ref, tk_ref, cnt_ref):
    @pl.when(pl.program_id(0) == 0)
    def _():
        cnt_ref[...] = jnp.zeros_like(cnt_ref)

    hm = h_ref[...] * _sigmoid(om_ref[...])
    ym = hm * lax.rsqrt(jnp.mean(hm * hm, axis=1, keepdims=True) + LN_EPS) * gm_ref[...]
    att = att_ref[...]
    ya = att * lax.rsqrt(jnp.mean(att * att, axis=1, keepdims=True) + LN_EPS) * ga_ref[...]
    mix = (jnp.dot(ym.astype(BF16), wo_ref[0:M_WIDTH, :], preferred_element_type=F32)
           + jnp.dot(ya.astype(BF16), wo_ref[M_WIDTH:, :], preferred_element_type=F32))
    x1 = _layer_norm(DN_ALPHA * x_ref[...] + mix, g1_ref[...], b1_ref[...])
    x1_ref[...] = x1
    _to_token_tiles(xp_ref, _pack_halves(x1))
    logits = jnp.dot(x1.astype(BF16), wr_ref[...], preferred_element_type=F32) + br_ref[...]
    lane = lax.broadcasted_iota(I32, logits.shape, 1).astype(F32)
    vals, idxs = [], []
    for _ in range(TOP_K):
        mx = jnp.max(logits, axis=1, keepdims=True)
        idx = jnp.min(jnp.where(logits == mx, lane, float(LANES)), axis=1, keepdims=True)
        vals.append(mx)
        idxs.append(idx)
        logits = jnp.where(lane == idx, 2.0 * NEG, logits)
    es = [jnp.exp(vk - vals[0]) for vk in vals]
    tot = es[0] + es[1] + es[2] + es[3]
    tk = jnp.zeros(logits.shape, F32)
    picked = jnp.zeros(logits.shape, F32)
    for k in range(TOP_K):
        tk = jnp.where(lane == float(k), es[k] / tot, tk)
        tk = jnp.where(lane == float(TOP_K + k), idxs[k], tk)
        picked = jnp.where(lane == idxs[k], 1.0, picked)
    tk_ref[...] = tk
    cnt_ref[...] = cnt_ref[...] + jnp.sum(picked, axis=0, keepdims=True)


def _merge(h, om, att, x, gm, ga, wo, g1, b1, wr, br, tile, name):
    t = x.shape[0]
    rows = lambda w: pl.BlockSpec((tile, w), lambda i: (i, 0))
    const = lambda a: pl.BlockSpec(a.shape, lambda i: (0, 0))
    return pl.pallas_call(
        _merge_kernel,
        out_shape=[jax.ShapeDtypeStruct((t, D_MODEL), F32), jax.ShapeDtypeStruct((t, XP_TILE, LANES), U32),
                   jax.ShapeDtypeStruct((t, LANES), F32), jax.ShapeDtypeStruct((8, LANES), F32)],
        grid=(t // tile,),
        in_specs=[rows(M_WIDTH), rows(M_WIDTH), rows(A_WIDTH), rows(D_MODEL), const(gm), const(ga), const(wo),
                  const(g1), const(b1), const(wr), const(br)],
        out_specs=[rows(D_MODEL), pl.BlockSpec((tile, XP_TILE, LANES), lambda i: (i, 0, 0)), rows(LANES),
                   pl.BlockSpec((8, LANES), lambda i: (0, 0))],
        compiler_params=_cparams("arbitrary"), name=name,
    )(h, om, att, x, gm, ga, wo, g1, b1, wr, br)


def _route_kernel(tk_ref, first_ref, strict_ref, dest_ref, next_scr):
    @pl.when(pl.program_id(0) == 0)
    def _():
        next_scr[...] = first_ref[...]

    tk = tk_ref[...]
    lane = lax.broadcasted_iota(I32, tk.shape, 1).astype(F32)
    onehots = [jnp.where(lane == tk[:, TOP_K + k:TOP_K + k + 1], 1.0, 0.0) for k in range(TOP_K)]
    tot = onehots[0] + onehots[1] + onehots[2] + onehots[3]
    row = jnp.dot(strict_ref[...], tot.astype(BF16), preferred_element_type=F32) + next_scr[0:1, :]
    out = jnp.zeros(tk.shape, F32)
    for k in range(TOP_K):
        out = jnp.where(lane == float(k), jnp.sum(onehots[k] * row, axis=1, keepdims=True), out)
    dest_ref[...] = out.astype(I32)
    next_scr[...] = next_scr[...] + jnp.sum(tot, axis=0, keepdims=True)


def _route(tk, first):
    t = tk.shape[0]
    tile = min(RANK_TILE, t)
    strict = jnp.asarray(np.tril(np.ones((tile, tile), np.float32), -1), BF16)
    return pl.pallas_call(
        _route_kernel, out_shape=jax.ShapeDtypeStruct((t, LANES), I32), grid=(t // tile,),
        in_specs=[pl.BlockSpec((tile, LANES), lambda i: (i, 0)), pl.BlockSpec((8, LANES), lambda i: (0, 0)),
                  pl.BlockSpec((tile, tile), lambda i: (0, 0))],
        out_specs=pl.BlockSpec((tile, LANES), lambda i: (i, 0)),
        scratch_shapes=[pltpu.VMEM((8, LANES), F32)],
        compiler_params=_cparams("arbitrary"), name="moe_route",
    )(tk, first, strict)


def _offsets_kernel(cnt_ref, off_ref, be_ref, nu_ref, pad_ref, *, tile):
    cnt = cnt_ref[...]
    nblk = jnp.floor((cnt + float(tile - 1)) * (1.0 / tile))
    r_i = lax.broadcasted_iota(I32, (LANES, LANES), 0)
    c_i = lax.broadcasted_iota(I32, (LANES, LANES), 1)
    incl = jnp.where(r_i <= c_i, 1.0, 0.0).astype(BF16)
    cum = jnp.dot(nblk.astype(BF16), incl, preferred_element_type=F32)
    off = (cum - nblk) * float(tile)
    off_ref[...] = off
    which = lax.broadcasted_iota(I32, cnt.shape, 0)
    pad_ref[...] = jnp.where(which == 0, off + cnt, jnp.where(which == 1, nblk * float(tile) - cnt, 0.0)).astype(I32)
    rows = be_ref.shape[0]
    jb = (lax.broadcasted_iota(I32, (rows, LANES), 0) * LANES + lax.broadcasted_iota(I32, (rows, LANES), 1)).astype(F32)
    acc = jnp.zeros((rows, LANES), F32)
    for e in range(N_EXPERTS):
        acc = acc + jnp.where(jb >= cum[0:1, e:e + 1], 1.0, 0.0)
    be_ref[...] = jnp.minimum(acc, float(N_EXPERTS - 1)).astype(I32)
    nu_ref[...] = jnp.broadcast_to(cum[0:1, N_EXPERTS - 1:N_EXPERTS], nu_ref.shape).astype(I32)


def _offsets(cnt, n_blocks, tile):
    rows = -(-n_blocks // LANES)
    rows = -(-rows // 8) * 8
    return pl.pallas_call(
        functools.partial(_offsets_kernel, tile=tile),
        out_shape=[jax.ShapeDtypeStruct((8, LANES), F32), jax.ShapeDtypeStruct((rows, LANES), I32),
                   jax.ShapeDtypeStruct((8, LANES), I32), jax.ShapeDtypeStruct((8, LANES), I32)],
        name="moe_offsets",
    )(cnt)


def _scatter_rows(dest_ref, xp_ref, xs_ref, sem):
    t = xp_ref.shape[0]

    def row_copy(tok, dst):
        return pltpu.make_async_copy(xp_ref.at[pl.ds(tok, 1)], xs_ref.at[pl.ds(dst, 1)], sem)

    def issue(grp, carry):
        base = pl.multiple_of(grp * ISSUE_GROUP, ISSUE_GROUP)
        for u in range(ISSUE_GROUP):
            for k in range(TOP_K):
                row_copy(base + u, dest_ref[(base + u) * TOP_K + k]).start(priority=k % 2)
        return carry

    lax.fori_loop(0, t // ISSUE_GROUP, issue, 0)
    for k in range(TOP_K):
        pltpu.make_async_copy(xp_ref, xs_ref.at[pl.ds(0, t)], sem).wait()


def _sc_scatter(rows, idx, n_out):
    t = rows.shape[0]
    n_workers = SC_CORES * SC_SUBCORES
    assert idx.shape[0] == TOP_K * t and t % (n_workers * SC_CHUNK) == 0
    per_worker = t // n_workers
    mesh = plsc.VectorSubcoreMesh(core_axis_name="c", subcore_axis_name="s", num_cores=SC_CORES,
                                  num_subcores=SC_SUBCORES)

    @functools.partial(
        pl.kernel, mesh=mesh, out_type=jax.ShapeDtypeStruct((n_out,) + rows.shape[1:], rows.dtype),
        scratch_types=[pltpu.VMEM((SC_CHUNK,), I32) for _ in range(TOP_K)]
                      + [pltpu.VMEM((SC_CHUNK,) + rows.shape[1:], rows.dtype), pltpu.SemaphoreType.DMA])
    def scatter(rows_hbm, idx_hbm, out_hbm, *scratch):
        idx_v, rows_v, sem = scratch[:TOP_K], scratch[TOP_K], scratch[TOP_K + 1]
        wid = lax.axis_index("s") * SC_CORES + lax.axis_index("c")
        base = wid * per_worker

        @pl.loop(0, per_worker // SC_CHUNK)
        def _(j):
            off = pl.multiple_of(base + j * SC_CHUNK, SC_CHUNK)
            pltpu.sync_copy(rows_hbm.at[pl.ds(off, SC_CHUNK)], rows_v)
            for k in range(TOP_K):
                pltpu.sync_copy(idx_hbm.at[pl.ds(pl.multiple_of(k * t + off, SC_CHUNK), SC_CHUNK)], idx_v[k])
            copies = [pltpu.async_copy(rows_v, out_hbm.at[idx_v[k]], sem) for k in range(TOP_K)]
            for cp in copies:
                cp.wait()

    return scatter(rows, idx)


def _dispatch_rest_kernel(pads_ref, dest2_ref, xp2_ref, xs_in_ref, xs_ref, sem, zsem, zbuf, *, block_rows):
    del xs_in_ref
    zr = zbuf.shape[0]
    n_blocks = xs_ref.shape[0] // block_rows
    zbuf[...] = jnp.zeros_like(zbuf)
    used = pads_ref[2 * N_EXPERTS]

    def pieces(e, act):
        start, n = pads_ref[e], pads_ref[N_EXPERTS + e]
        for sh in range(zr.bit_length() - 1, -1, -1):
            b = 1 << sh
            before = lax.shift_left(lax.shift_right_logical(n, sh + 1), sh + 1)

            @pl.when((n & b) != 0)
            def _():
                act(pltpu.make_async_copy(zbuf.at[pl.ds(0, b)], xs_ref.at[pl.ds(start + before, b)], zsem))

    def tail(jb, act):
        for h in range(block_rows // zr):
            act(pltpu.make_async_copy(zbuf, xs_ref.at[pl.ds(jb * block_rows + h * zr, zr)], zsem))

    for act in (lambda cp: cp.start(), lambda cp: cp.wait()):
        lax.fori_loop(0, N_EXPERTS, lambda e, c: (pieces(e, act), c)[1], 0)
        lax.fori_loop(used, n_blocks, lambda jb, c: (tail(jb, act), c)[1], 0)

    _scatter_rows(dest2_ref, xp2_ref, xs_ref, sem)


def _dispatch_rest(pads, dest2, xp2, xs, block_rows):
    return pl.pallas_call(
        functools.partial(_dispatch_rest_kernel, block_rows=block_rows),
        out_shape=jax.ShapeDtypeStruct(xs.shape, xs.dtype),
        in_specs=[pl.BlockSpec(memory_space=pltpu.SMEM), pl.BlockSpec(memory_space=pltpu.SMEM),
                  pl.BlockSpec(memory_space=pltpu.VMEM), pl.BlockSpec(memory_space=pl.ANY)],
        out_specs=pl.BlockSpec(memory_space=pl.ANY),
        scratch_shapes=[pltpu.SemaphoreType.DMA(()), pltpu.SemaphoreType.DMA(()),
                        pltpu.VMEM((EXPERT_TILE // 2,) + xp2.shape[1:], xp2.dtype)],
        input_output_aliases={3: 0},
        compiler_params=_cparams(), name="moe_dispatch_rest",
    )(pads, dest2, xp2, xs)


def _expert_kernel(be_ref, nu_ref, xs_ref, w1_ref, b1g_ref, b1l_ref, w2_ref, b2_ref, perm_ref, ys_ref,
                   w1g_scr, w1l_scr, w2_scr, xq_scr, y_scr, sem, osem):
    j = pl.program_id(0)
    active = j < nu_ref[0]
    changed = jnp.logical_or(j == 0, be_ref[j] != be_ref[jnp.maximum(j - 1, 0)])
    tm = y_scr.shape[0]
    slot = lax.rem(j, 2)

    def fetch(blk, slot):
        row0 = pl.multiple_of(blk * tm, tm)
        return [pltpu.make_async_copy(xs_ref.at[pl.ds(row0, tm), q, :], xq_scr.at[slot, q], sem.at[slot])
                for q in range(XP_TILE)]

    def put(blk):
        row0 = pl.multiple_of(blk * tm, tm)
        return [pltpu.make_async_copy(y_scr.at[:, q * LANES:(q + 1) * LANES], ys_ref.at[pl.ds(row0, tm), q, :], osem)
                for q in range(YS_TILE)]

    def emit(y):
        @pl.when(j > 0)
        def _():
            for cp in put(j - 1):
                cp.wait()

        y_scr[...] = _pack_halves(y)
        for cp in put(j):
            cp.start()

    @pl.when(j == 0)
    def _():
        for cp in fetch(0, 0):
            cp.start()

    @pl.when(j + 1 < nu_ref[0])
    def _():
        for cp in fetch(j + 1, 1 - slot):
            cp.start()

    @pl.when(jnp.logical_and(active, changed))
    def _():
        for c in range(2 * D_FF // 256):
            wc = w1_ref[0, :, c * 256:(c + 1) * 256].astype(BF16)
            d = jnp.dot(wc, perm_ref[...], preferred_element_type=F32).astype(BF16)
            w1g_scr[:, c * 128:(c + 1) * 128] = d[:, :128]
            w1l_scr[:, c * 128:(c + 1) * 128] = d[:, 128:]
        for c in range(D_FF // 256):
            w2_scr[c * 256:(c + 1) * 256, :] = w2_ref[0, c * 256:(c + 1) * 256, :].astype(BF16)

    @pl.when(active)
    def _():
        for cp in fetch(j, slot):
            cp.wait()
        lo, hi = _unpack_halves(jnp.concatenate([xq_scr[slot, q] for q in range(XP_TILE)], axis=1))
        xb = jnp.concatenate([lo, hi], axis=1)
        hg = jnp.dot(xb, w1g_scr[...], preferred_element_type=F32) + b1g_ref[0]
        hl = jnp.dot(xb, w1l_scr[...], preferred_element_type=F32) + b1l_ref[0]
        x_glu = jnp.minimum(hg, SWIGLU_LIMIT)
        x_lin = jnp.clip(hl, -SWIGLU_LIMIT, SWIGLU_LIMIT)
        a = x_glu * _sigmoid(SWIGLU_ALPHA * x_glu) * (x_lin + 1.0)
        emit(jnp.dot(a.astype(BF16), w2_scr[...], preferred_element_type=F32) + b2_ref[0])

    @pl.when(jnp.logical_not(active))
    def _():
        emit(jnp.zeros((tm, D_MODEL), F32))

    @pl.when(j == pl.num_programs(0) - 1)
    def _():
        for cp in put(j):
            cp.wait()


def _experts(be, nu, xs, w1, b1g, b1l, w2, b2, perm, tile):
    n_blocks = xs.shape[0] // tile
    grid_spec = pltpu.PrefetchScalarGridSpec(
        num_scalar_prefetch=2, grid=(n_blocks,),
        in_specs=[pl.BlockSpec(memory_space=pl.ANY),
                  pl.BlockSpec((1, D_MODEL, 2 * D_FF), lambda j, be, nu: (be[j], 0, 0)),
                  pl.BlockSpec((1, 1, D_FF), lambda j, be, nu: (be[j], 0, 0)),
                  pl.BlockSpec((1, 1, D_FF), lambda j, be, nu: (be[j], 0, 0)),
                  pl.BlockSpec((1, D_FF, D_MODEL), lambda j, be, nu: (be[j], 0, 0)),
                  pl.BlockSpec((1, 1, D_MODEL), lambda j, be, nu: (be[j], 0, 0)),
                  pl.BlockSpec((256, 256), lambda j, be, nu: (0, 0))],
        out_specs=pl.BlockSpec(memory_space=pl.ANY),
        scratch_shapes=[pltpu.VMEM((D_MODEL, D_FF), BF16), pltpu.VMEM((D_MODEL, D_FF), BF16),
                        pltpu.VMEM((D_FF, D_MODEL), BF16), pltpu.VMEM((2, XP_TILE, tile, LANES), U32),
                        pltpu.VMEM((tile, YS_TILE * LANES), U32), pltpu.SemaphoreType.DMA((2,)),
                        pltpu.SemaphoreType.DMA(())])
    return pl.pallas_call(
        _expert_kernel, out_shape=jax.ShapeDtypeStruct((xs.shape[0], YS_TILE, LANES), U32), grid_spec=grid_spec,
        compiler_params=_cparams("arbitrary"), name="moe_experts",
    )(be, nu, xs, w1, b1g, b1l, w2, b2, perm)


def _combine_kernel(dest_ref, next_ref, ys_ref, tk_ref, x1_ref, g2_ref, b2_ref, out_ref, buf, sem):
    i = pl.program_id(0)
    t = x1_ref.shape[0]
    slot = lax.rem(i, 2)

    def gather(idx_ref, s):
        def issue(grp, carry):
            base = pl.multiple_of(grp * 8, 8)
            for u in range(8):
                for k in range(TOP_K):
                    pltpu.make_async_copy(ys_ref.at[idx_ref[(base + u) * TOP_K + k]],
                                          buf.at[s, k, grp, :, u, :], sem.at[s]).start(priority=k % 2)
            return carry

        lax.fori_loop(0, t // 8, issue, 0)

    @pl.when(i == 0)
    def _():
        gather(dest_ref, 0)

    @pl.when(i + 1 < pl.num_programs(0))
    def _():
        gather(next_ref, 1 - slot)

    for k in range(TOP_K):
        for u in range(8):
            pltpu.make_async_copy(ys_ref.at[pl.ds(0, t // 8)], buf.at[slot, k, :, :, u, :], sem.at[slot]).wait()
    tk = tk_ref[...]
    los, his = [], []
    for q in range(YS_TILE):
        lo = hi = None
        for k in range(TOP_K):
            words = buf[slot, k, :, q].reshape(t, LANES)
            g = tk[:, k:k + 1]
            lo_k = g * pltpu.bitcast(words << 16, F32)
            hi_k = g * pltpu.bitcast(words & jnp.uint32(0xFFFF0000), F32)
            lo = lo_k if lo is None else lo + lo_k
            hi = hi_k if hi is None else hi + hi_k
        los.append(lo)
        his.append(hi)
    ff = jnp.concatenate(los + his, axis=1)
    out_ref[...] = _layer_norm(DN_ALPHA * x1_ref[...] + ff, g2_ref[...], b2_ref[...])


def _sc_gather(rows, idx):
    b = idx.shape[0]
    n_workers = SC_CORES * SC_SUBCORES
    assert b % (n_workers * SC_CHUNK) == 0
    per_worker = b // n_workers
    mesh = plsc.VectorSubcoreMesh(core_axis_name="c", subcore_axis_name="s", num_cores=SC_CORES,
                                  num_subcores=SC_SUBCORES)

    @functools.partial(
        pl.kernel, mesh=mesh, out_type=jax.ShapeDtypeStruct((b,) + rows.shape[1:], rows.dtype),
        scratch_types=[pltpu.VMEM((SC_CHUNK,), I32), pltpu.VMEM((SC_CHUNK,) + rows.shape[1:], rows.dtype),
                       pltpu.SemaphoreType.DMA])
    def gather(rows_hbm, idx_hbm, out_hbm, idx_v, rows_v, sem):
        wid = lax.axis_index("s") * SC_CORES + lax.axis_index("c")
        base = wid * per_worker

        @pl.loop(0, per_worker // SC_CHUNK)
        def _(j):
            off = pl.multiple_of(base + j * SC_CHUNK, SC_CHUNK)
            pltpu.sync_copy(idx_hbm.at[pl.ds(off, SC_CHUNK)], idx_v)
            pltpu.async_copy(rows_hbm.at[idx_v], rows_v, sem).wait()
            pltpu.sync_copy(rows_v, out_hbm.at[pl.ds(off, SC_CHUNK)])

    return gather(rows, idx)


def _combine_stream_kernel(g_ref, tk_ref, x1_ref, g2_ref, b2_ref, out_ref, buf, sem):
    i = pl.program_id(0)
    t = x1_ref.shape[0]
    slot = lax.rem(i, 2)

    def fetch(blk, s):
        row0 = pl.multiple_of(blk * t, t)
        return [pltpu.make_async_copy(g_ref.at[k, pl.ds(row0, t), q, :], buf.at[s, k, q], sem.at[s])
                for k in range(TOP_K) for q in range(YS_TILE)]

    @pl.when(i == 0)
    def _():
        for cp in fetch(0, 0):
            cp.start()

    @pl.when(i + 1 < pl.num_programs(0))
    def _():
        for cp in fetch(i + 1, 1 - slot):
            cp.start()

    for cp in fetch(i, slot):
        cp.wait()
    tk = tk_ref[...]
    los, his = [], []
    for q in range(YS_TILE):
        lo = hi = None
        for k in range(TOP_K):
            words = buf[slot, k, q]
            g = tk[:, k:k + 1]
            lo_k = g * pltpu.bitcast(words << 16, F32)
            hi_k = g * pltpu.bitcast(words & jnp.uint32(0xFFFF0000), F32)
            lo = lo_k if lo is None else lo + lo_k
            hi = hi_k if hi is None else hi + hi_k
        los.append(lo)
        his.append(hi)
    ff = jnp.concatenate(los + his, axis=1)
    out_ref[...] = _layer_norm(DN_ALPHA * x1_ref[...] + ff, g2_ref[...], b2_ref[...])


def _combine_stream(g4, tk, x1, g2, b2):
    t = x1.shape[0]
    tile = min(MERGE_TILE, t)
    return pl.pallas_call(
        _combine_stream_kernel, out_shape=jax.ShapeDtypeStruct((t, D_MODEL), F32), grid=(t // tile,),
        in_specs=[pl.BlockSpec(memory_space=pl.ANY),
                  pl.BlockSpec((tile, LANES), lambda i: (i, 0)),
                  pl.BlockSpec((tile, D_MODEL), lambda i: (i, 0)),
                  pl.BlockSpec((1, D_MODEL), lambda i: (0, 0)),
                  pl.BlockSpec((1, D_MODEL), lambda i: (0, 0))],
        out_specs=pl.BlockSpec((tile, D_MODEL), lambda i: (i, 0)),
        scratch_shapes=[pltpu.VMEM((2, TOP_K, YS_TILE, tile, LANES), U32), pltpu.SemaphoreType.DMA((2,))],
        compiler_params=_cparams("arbitrary"), name="moe_combine_stream",
    )(g4, tk, x1, g2, b2)


def _combine(dest_flat, ys, tk, x1, g2, b2):
    t = x1.shape[0]
    tile = min(ROW_TILE, t)
    n = t // tile
    return pl.pallas_call(
        _combine_kernel, out_shape=jax.ShapeDtypeStruct((t, D_MODEL), F32), grid=(n,),
        in_specs=[pl.BlockSpec((tile * TOP_K,), lambda i: (i,), memory_space=pltpu.SMEM),
                  pl.BlockSpec((tile * TOP_K,), lambda i: (jnp.minimum(i + 1, n - 1),), memory_space=pltpu.SMEM),
                  pl.BlockSpec(memory_space=pl.ANY),
                  pl.BlockSpec((tile, LANES), lambda i: (i, 0)),
                  pl.BlockSpec((tile, D_MODEL), lambda i: (i, 0)),
                  pl.BlockSpec((1, D_MODEL), lambda i: (0, 0)),
                  pl.BlockSpec((1, D_MODEL), lambda i: (0, 0))],
        out_specs=pl.BlockSpec((tile, D_MODEL), lambda i: (i, 0)),
        scratch_shapes=[pltpu.VMEM((2, TOP_K, tile // 8, YS_TILE, 8, LANES), U32), pltpu.SemaphoreType.DMA((2,))],
        compiler_params=_cparams("arbitrary"), name="moe_combine",
    )(dest_flat, dest_flat, ys, tk, x1, g2, b2)


def _rel_bucket(dist):
    exact = REL_BUCKETS // 2
    d = np.maximum(dist, 0)
    log_b = exact + (np.log(np.maximum(d, 1).astype(np.float32) / np.float32(exact))
                     / np.float32(math.log(REL_MAX_DIST / exact)) * np.float32(REL_BUCKETS - exact)).astype(np.int32)
    return np.where(d < exact, d, np.minimum(log_b, REL_BUCKETS - 1)).astype(np.int32)


def _bias_lookup(table, bucket, valid):
    bucket = jnp.asarray(bucket)[None]
    acc = jnp.zeros((table.shape[1],) + bucket.shape[1:], F32)
    for b in range(REL_BUCKETS):
        acc = jnp.where(bucket == b, table[b].reshape((-1,) + (1,) * (bucket.ndim - 1)), acc)
    return jnp.where(jnp.asarray(valid)[None], acc, NEG)


def _bias_tables(rel_bias):
    table = rel_bias.astype(F32)
    r = np.arange(WINDOW)[:, None]
    c = np.arange(2 * WINDOW)[None, :]
    dist = r + WINDOW - c
    valid = (dist >= 0) & (dist < WINDOW)
    dist0 = np.where(c < N_META, N_META + r - c, dist)
    valid0 = np.where(c < N_META, dist0 < WINDOW, (c >= WINDOW) & valid)
    both = jnp.stack([_bias_lookup(table, _rel_bucket(dist0), valid0), _bias_lookup(table, _rel_bucket(dist), valid)])
    dist_s = WINDOW - 1 - np.arange(WINDOW)
    rows = _bias_lookup(table[:, np.asarray(HEAD_ORDER)], _rel_bucket(dist_s), np.ones_like(dist_s, bool))
    return both, rows


def _perm_heads(a, axis):
    assert HEAD_ORDER == tuple(kv * A_GROUP + g for g in range(A_GROUP) for kv in range(A_KV_HEADS))
    shape = a.shape
    a = a.reshape(shape[:axis] + (A_KV_HEADS, A_GROUP, A_HD) + shape[axis + 1:])
    return jnp.swapaxes(a, axis, axis + 1).reshape(shape)


def _rep_rows(vec, rows=8):
    out = jnp.zeros((rows, LANES), F32)
    return out.at[:vec.shape[0], :].set(jnp.broadcast_to(vec.astype(F32)[:, None], (vec.shape[0], LANES)))


def kernel(x_prompt, x_sample, cache_swa_k, cache_swa_v, state_mlstm_C, state_mlstm_n, state_mlstm_m, meta_tokens, rel_bias, w_in, b_igate, b_fgate, attn_sinks, g_mlstm_out, g_attn_out, w_out, ln1_g, ln1_b, w_router, b_router, w_moe1, b_moe1, w_moe2, b_moe2, ln2_g, ln2_b):
    B, S, _ = x_prompt.shape
    NB = x_sample.shape[0]
    assert x_sample.shape[1] == 1 and w_in.shape[0] == 1
    assert S % PROJ_TILE == 0 and S % M_CHUNK == 0 and S % WINDOW == 0 and NB % SAMPLE_GROUP == 0
    l = 0

    assert IN_WIDTHS == (512, 512, 512, 512, 4, 4, 512, 128, 128)
    bf = lambda a: a.astype(BF16)
    w = w_in[l]
    n_main, n_gate = 4 * M_WIDTH, 2 * M_HEADS
    w_gate = w[:, n_main:n_main + n_gate]
    w_att = w[:, n_main + n_gate:]
    assert math.frexp(A_HD ** -0.5)[0] == 0.5
    w_qa = _perm_heads(w_att[:, :A_WIDTH], 1) * (A_HD ** -0.5)
    wr = bf(jnp.concatenate([w[:, :n_main], w_qa, w_att[:, A_WIDTH:],
                             jnp.pad(w_gate, ((0, 0), (0, LANES - n_gate)))], axis=1))
    wt = bf(jnp.concatenate([w[:, :M_WIDTH], w[:, 2 * M_WIDTH:3 * M_WIDTH], w_qa, w_att[:, A_WIDTH + LANES:],
                             w_gate], axis=1).T)
    b_gate = jnp.concatenate([b_igate[l], b_fgate[l]]).astype(F32)
    brow = jnp.pad(b_gate, (0, LANES - n_gate))[None, :]
    bcol = b_gate[:, None]
    plan_p = ((512, 512, "plain", BF16), (1536, 512, "plain", F32), (2560, 128, "plain", BF16),
              (2816, 128, "gate", F32))
    tplan_p = ((0, 512, "plain", BF16), (512, 512, "plain", BF16), (1024, 512, "plain", BF16),
               (1536, 128, "plain", BF16), (1664, 8, "gate", F32))
    plan_s = ((0, 512, "plain", F32), (512, 512, "plain", F32), (1024, 512, "plain", F32), (1536, 512, "plain", F32),
              (2048, 512, "plain", F32), (2560, 128, "plain", F32), (2688, 128, "plain", F32), (2816, 128, "gate", F32))

    bias_tab, bias_rows = _bias_tables(rel_bias)
    sinks = _rep_rows(attn_sinks[l])
    sinks_step = _rep_rows(attn_sinks[l][np.asarray(HEAD_ORDER)])
    g_m = g_mlstm_out[l].astype(F32)[None, :]
    g_a = _perm_heads(g_attn_out[l].astype(F32), 0)[None, :]
    wo = bf(jnp.concatenate([w_out[l][:M_WIDTH], _perm_heads(w_out[l][M_WIDTH:], 0)], axis=0))
    g1, b1 = ln1_g[l].astype(F32)[None, :], ln1_b[l].astype(F32)[None, :]
    g2, b2 = ln2_g[l].astype(F32)[None, :], ln2_b[l].astype(F32)[None, :]
    w_r = bf(jnp.pad(w_router[l], ((0, 0), (0, LANES - N_EXPERTS))))
    b_r = jnp.pad(b_router[l].astype(F32), (0, LANES - N_EXPERTS), constant_values=NEG)[None, :]
    b1g = b_moe1[l][:, 0::2].astype(F32)[:, None, :]
    b1l = b_moe1[l][:, 1::2].astype(F32)[:, None, :]
    b2e = b_moe2[l].astype(F32)[:, None, :]
    pj = np.zeros((256, 256), np.float32)
    pj[2 * np.arange(128), np.arange(128)] = 1.0
    pj[2 * np.arange(128) + 1, 128 + np.arange(128)] = 1.0
    perm = jnp.asarray(pj, BF16)

    xp2 = x_prompt.reshape(B * S, D_MODEL)
    km, om, ka, gc, qt, vt, qat, vat, gr, kv_tail = _proj(
        xp2, wr, wt, brow, bcol, plan_p, tplan_p, (2560, 256), PROJ_TILE, S, "proj_prompt")
    x_meta = jnp.pad(meta_tokens.astype(F32), ((0, M_CHUNK - N_META), (0, 0)))
    km0, _, ka0, gc0, qt0, vt0, _, vat0, gr0 = _proj(
        x_meta, wr, wt, brow, bcol, plan_p, tplan_p, None, M_CHUNK, M_CHUNK, "proj_meta")
    xs2 = x_sample.reshape(NB, D_MODEL)
    qm_s, km_s, vm_s, om_s, qa_s, ka_s, va_s, gc_s = _proj(
        xs2, wr, wt, brow, bcol, plan_s, (), None, NB, NB, "proj_sample")

    zero_c = jnp.zeros((M_HEADS, M_DV + 8, M_DK), F32)
    zero_m = jnp.zeros((8, LANES), F32)
    _, c_meta, m_meta = _mlstm(qt0, km0, vt0, gc0, gr0, zero_c, zero_m, 1, N_META, "mlstm_meta")
    h_p, c_p, m_p = _mlstm(qt, km, vt, gc, gr, c_meta[0], m_meta[0], B, M_CHUNK, "mlstm_prompt")
    C_p = c_p[:, :, :M_DV, :]
    n_p = c_p[:, :, M_DV, :]
    m_prompt = m_p[:, :M_HEADS, 0]
    m_pad = jnp.pad(state_mlstm_m[l].astype(F32), ((0, 0), (0, LANES - M_HEADS)))
    C_s, n_s, m_s, h_s = _mlstm_step(state_mlstm_C[l].astype(F32), state_mlstm_n[l].astype(F32), m_pad,
                                     gc_s, qm_s, km_s, vm_s)

    att_p = _swa(qat, ka, vat, ka0, vat0, jnp.swapaxes(bias_tab, 2, 3), sinks, B)
    ck = cache_swa_k[l].reshape(NB, WINDOW, LANES)
    cv = cache_swa_v[l].reshape(NB, WINDOW, LANES)
    k_new, v_new, att_s = _swa_step(ck, cv, qa_s, ka_s, va_s, bias_rows, sinks_step)

    x1_p, xpk_p, tk_p, cnt_p = _merge(h_p, om, att_p, xp2, g_m, g_a, wo, g1, b1, w_r, b_r, MERGE_TILE, "merge_prompt")
    x1_s, xpk_s, tk_s, cnt_s = _merge(h_s, om_s, att_s, xs2, g_m, g_a, wo, g1, b1, w_r, b_r, NB, "merge_sample")

    T_p = B * S
    assert T_p % RANK_TILE == 0 and T_p % ROW_TILE == 0
    n_blocks = -(-((T_p + NB) * TOP_K) // EXPERT_TILE) + N_EXPERTS
    off, be2, nu2, pad = _offsets(cnt_p + cnt_s, n_blocks, EXPERT_TILE)
    pads = jnp.concatenate([pad[0, :N_EXPERTS], pad[1, :N_EXPERTS], nu2[0, :1]])
    dest_p = _route(tk_p, off)[:, :TOP_K].reshape(-1)
    dest_s = _route(tk_s, off + cnt_p)[:, :TOP_K].reshape(-1)
    be = be2.reshape(-1)[:n_blocks]
    nu = nu2[0, :1]
    idx_p = dest_p.reshape(T_p, TOP_K).T.reshape(-1)
    xs = _sc_scatter(xpk_p, idx_p, n_blocks * EXPERT_TILE)
    xs = _dispatch_rest(pads, dest_s, xpk_s, xs, EXPERT_TILE)
    ys = _experts(be, nu, xs, w_moe1[l], b1g, b1l, w_moe2[l], b2e, perm, EXPERT_TILE)
    g4 = _sc_gather(ys, idx_p).reshape(TOP_K, T_p, YS_TILE, LANES)
    y_p = _combine_stream(g4, tk_p, x1_p, g2, b2)
    y_s = _combine(dest_s, ys, tk_s, x1_s, g2, b2)

    kv_tail = kv_tail.reshape(B, WINDOW, 2, A_KV_HEADS, A_HD)
    dt_k, dt_v = cache_swa_k.dtype, cache_swa_v.dtype
    return (y_p.reshape(B, S, D_MODEL).astype(x_prompt.dtype), y_s.reshape(NB, 1, D_MODEL).astype(x_sample.dtype),
            kv_tail[:, :, 0][None].astype(dt_k), kv_tail[:, :, 1][None].astype(dt_v),
            C_p[None].astype(state_mlstm_C.dtype), n_p[None].astype(state_mlstm_n.dtype),
            m_prompt[None].astype(state_mlstm_m.dtype),
            k_new.reshape(1, NB, WINDOW, A_KV_HEADS, A_HD).astype(dt_k),
            v_new.reshape(1, NB, WINDOW, A_KV_HEADS, A_HD).astype(dt_v),
            C_s[None].astype(state_mlstm_C.dtype), n_s[None].astype(state_mlstm_n.dtype),
            m_s[:, :M_HEADS][None].astype(state_mlstm_m.dtype))
```

```python
import functools
import math

import numpy as np
import jax
import jax.numpy as jnp
from jax import lax
from jax.experimental import pallas as pl
from jax.experimental.pallas import tpu as pltpu
from jax.experimental.pallas import tpu_sc as plsc

F32 = jnp.float32
BF16 = jnp.bfloat16
I32 = jnp.int32
U32 = jnp.uint32

D_MODEL = 1024
N_META = 16
M_HEADS = 4
M_DK = 128
M_DV = 128
M_WIDTH = M_HEADS * M_DV
A_HD = 64
A_HEADS = 8
A_KV_HEADS = 2
A_GROUP = A_HEADS // A_KV_HEADS
A_WIDTH = A_HEADS * A_HD
WINDOW = 128
REL_BUCKETS = 32
REL_MAX_DIST = 128
N_EXPERTS = 32
TOP_K = 4
D_FF = D_MODEL
SWIGLU_LIMIT = 7.0
SWIGLU_ALPHA = 1.702
DEPTH = 1
DN_ALPHA = (2.0 * DEPTH) ** 0.25
LN_EPS = 1e-5
IN_WIDTHS = (M_WIDTH, M_WIDTH, M_WIDTH, M_WIDTH, M_HEADS, M_HEADS, A_WIDTH, A_KV_HEADS * A_HD, A_KV_HEADS * A_HD)

LANES = 128
NEG = -1e30
VMEM_LIMIT = 56 * 1024 * 1024

M_CHUNK = 256
PROJ_TILE = 1024
MERGE_TILE = 1024
RANK_TILE = 1024
ROW_TILE = 512
EXPERT_TILE = 512
SAMPLE_GROUP = 8
ISSUE_GROUP = 8
SWA_QBLOCKS = 8
SC_CORES, SC_SUBCORES = 2, 16
SC_CHUNK = 128
SC_GATHER_BUFS = 4
SC_SCATTER_BUFS = 2
MLSTM_SEQS = 4
STEP_PIPE_LAG = 3
SWA_PIPE_LAG = 2
XP_TILE = D_MODEL // 2 // LANES
YS_TILE = D_MODEL // 2 // LANES
HEAD_ORDER = (0, 4, 1, 5, 2, 6, 3, 7)


def _cparams(*sem):
    return pltpu.CompilerParams(dimension_semantics=sem, vmem_limit_bytes=VMEM_LIMIT)


def _log_sigmoid(x):
    return jnp.minimum(x, 0.0) - jnp.log1p(jnp.exp(-jnp.abs(x)))


def _sigmoid(x):
    return 1.0 / (1.0 + jnp.exp(-x))


def _proj_kernel(x_ref, wr_ref, wt_ref, brow_ref, bcol_ref, *outs, row_plan, t_plan, tail_cols):
    xb = x_ref[...].astype(BF16)
    tm = xb.shape[0]
    o = 0
    for (c0, width, kind, _) in row_plan:
        r = jnp.dot(xb, wr_ref[:, c0:c0 + width], preferred_element_type=F32)
        if kind == "gate":
            r = r + brow_ref[...]
            lane = lax.broadcasted_iota(I32, r.shape, 1)
            r = jnp.where(lane < M_HEADS, r, _log_sigmoid(r))
        outs[o][...] = r.astype(outs[o].dtype)
        o += 1
    for (r0, nrows, kind, _) in t_plan:
        r = lax.dot_general(wt_ref[r0:r0 + nrows, :], xb, (((1,), (1,)), ((), ())), preferred_element_type=F32)
        if kind == "gate":
            r = r + bcol_ref[...]
            row = lax.broadcasted_iota(I32, r.shape, 0)
            r = jnp.where(row < M_HEADS, r, _log_sigmoid(r))
        outs[o][...] = r.astype(outs[o].dtype)
        o += 1
    if tail_cols is not None:
        c0, width = tail_cols
        outs[o][...] = jnp.dot(xb[tm - WINDOW:, :], wr_ref[:, c0:c0 + width], preferred_element_type=F32)


def _proj(x, wr, wt, brow, bcol, row_plan, t_plan, tail_cols, tile, rows_per_group, name):
    t = x.shape[0]
    nt = t // tile
    out_shape, out_specs = [], []
    for (_, width, _, dt) in row_plan:
        out_shape.append(jax.ShapeDtypeStruct((t, width), dt))
        out_specs.append(pl.BlockSpec((tile, width), lambda i: (i, 0)))
    for (_, nrows, _, dt) in t_plan:
        out_shape.append(jax.ShapeDtypeStruct((nrows, t), dt))
        out_specs.append(pl.BlockSpec((nrows, tile), lambda i: (0, i)))
    if tail_cols is not None:
        tiles_per_group = rows_per_group // tile
        out_shape.append(jax.ShapeDtypeStruct((t // rows_per_group * WINDOW, tail_cols[1]), F32))
        out_specs.append(pl.BlockSpec((WINDOW, tail_cols[1]), lambda i: (i // tiles_per_group, 0)))
    kern = functools.partial(_proj_kernel, row_plan=row_plan, t_plan=t_plan, tail_cols=tail_cols)
    return pl.pallas_call(
        kern, out_shape=out_shape, grid=(nt,),
        in_specs=[pl.BlockSpec((tile, D_MODEL), lambda i: (i, 0)),
                  pl.BlockSpec(wr.shape, lambda i: (0, 0)),
                  pl.BlockSpec(wt.shape, lambda i: (0, 0)),
                  pl.BlockSpec(brow.shape, lambda i: (0, 0)),
                  pl.BlockSpec(bcol.shape, lambda i: (0, 0))],
        out_specs=out_specs, compiler_params=_cparams("arbitrary"), name=name,
    )(x, wr, wt, brow, bcol)


def _split3(a):
    hi = a.astype(BF16)
    r1 = a - hi.astype(F32)
    mid = r1.astype(BF16)
    lo = (r1 - mid.astype(F32)).astype(BF16)
    return hi, mid, lo


def _mlstm_kernel(*refs, n_valid, nseq):
    seq_in = [refs[5 * i:5 * i + 5] for i in range(nseq)]
    c0_ref, m0_ref = refs[5 * nseq:5 * nseq + 2]
    h_ref, c_out_ref, m_out_ref, c_scr, m_scr = refs[5 * nseq + 2:]
    c = pl.program_id(1)
    nc = pl.num_programs(1)
    L = seq_in[0][1].shape[0]

    @pl.when(c == 0)
    def _():
        for i in range(nseq):
            c_scr[i] = c0_ref[...]
            m_scr[i] = m0_ref[...]

    r_i = lax.broadcasted_iota(I32, (L, L), 0)
    c_i = lax.broadcasted_iota(I32, (L, L), 1)
    upper = r_i <= c_i
    tril = jnp.where(c_i <= r_i, 1.0, 0.0).astype(BF16)
    triu = jnp.where(upper, 1.0, 0.0).astype(BF16)
    scale = M_DK ** -0.5
    ones_rows = jnp.where(lax.broadcasted_iota(I32, (8, L), 0) == 0, 1.0, 0.0).astype(BF16)

    gates = []
    for (_, _, _, gc_ref, gr_ref) in seq_in:
        gc = gc_ref[...]
        gr = gr_ref[...]
        if n_valid < L:
            rowc = lax.broadcasted_iota(I32, gc.shape, 0)
            lanec = lax.broadcasted_iota(I32, gc.shape, 1)
            gc = jnp.where(rowc < n_valid, gc, jnp.where(lanec < M_HEADS, NEG, 0.0))
            rowr = lax.broadcasted_iota(I32, gr.shape, 0)
            colr = lax.broadcasted_iota(I32, gr.shape, 1)
            gr = jnp.where(colr < n_valid, gr, jnp.where(rowr < M_HEADS, NEG, 0.0))
        b_cols = sum(jnp.dot(tril, part, preferred_element_type=F32) for part in _split3(gc))
        b_rows = sum(jnp.dot(part, triu, preferred_element_type=F32) for part in _split3(gr))
        gates.append((gc, gr, b_cols, b_rows))

    m_alls = [m_scr[i] for i in range(nseq)]
    c_alls = [[c_scr[i, h] for h in range(M_HEADS)] for i in range(nseq)]
    h_new, c_new, m_new_all = {}, {}, {}

    def operands(u):
        i, h = u
        qt_ref, k_ref, vt_ref = seq_in[i][:3]
        sl = slice(h * M_DK, (h + 1) * M_DK)
        return qt_ref[sl, :], k_ref[:, sl], jnp.concatenate([vt_ref[sl, :], ones_rows], axis=0)

    def stage_a(u):
        i, h = u
        gc, gr, b_cols, b_rows = gates[i]
        qt, k, vt_aug = operands(u)
        ig_r = gr[h:h + 1, :]
        b_r = b_rows[M_HEADS + h:M_HEADS + h + 1, :]
        m_prev = m_alls[i][h:h + 1, 0:1]
        cs = c_alls[i][h]
        qk = jnp.dot(k, qt, preferred_element_type=F32)
        inter = jnp.dot(cs.astype(BF16), qt, preferred_element_type=F32)
        b_last = b_r[:, L - 1:L]
        g = ig_r + b_last - b_r
        m_new = jnp.maximum(b_last + m_prev, jnp.max(g, axis=1, keepdims=True))
        a = jnp.exp(b_last + m_prev - m_new)
        wv = (vt_aug.astype(F32) * jnp.exp(g - m_new)).astype(BF16)
        c_new[u] = a * cs + jnp.dot(wv, k, preferred_element_type=F32) * scale
        m_new_all[u] = jnp.broadcast_to(m_new, (1, LANES))
        return qk, inter

    def stage_b(u, qk, inter):
        i, h = u
        gc, gr, b_cols, b_rows = gates[i]
        b_r = b_rows[M_HEADS + h:M_HEADS + h + 1, :]
        m_prev = m_alls[i][h:h + 1, 0:1]
        r_c = gc[:, h:h + 1] - b_cols[:, M_HEADS + h:M_HEADS + h + 1]
        dt = jnp.where(upper, b_r + r_c, NEG)
        m_t = jnp.maximum(b_r + m_prev, jnp.max(dt, axis=0, keepdims=True))
        st = (qk * (scale * jnp.exp(dt - m_t))).astype(BF16)
        return st, jnp.exp(b_r + m_prev - m_t) * inter, jnp.exp(-m_t)

    def stage_c(u, st, inter_w, floor):
        _, _, vt_aug = operands(u)
        nd = inter_w + jnp.dot(vt_aug, st, preferred_element_type=F32)
        den = nd[M_DV:M_DV + 1, :]
        h_new[u] = (nd[:M_DV, :] / jnp.maximum(jnp.abs(den), floor)).T

    units = [(i, h) for h in range(M_HEADS) for i in range(nseq)]
    a_q, b_q = {}, {}
    for n in range(len(units) + 2):
        if n < len(units):
            a_q[n] = stage_a(units[n])
        if 0 <= n - 1 < len(units):
            b_q[n - 1] = stage_b(units[n - 1], *a_q.pop(n - 1))
        if 0 <= n - 2 < len(units):
            stage_c(units[n - 2], *b_q.pop(n - 2))

    for i in range(nseq):
        h_ref[0, i] = jnp.concatenate([h_new[(i, h)] for h in range(M_HEADS)], axis=1)
        for h in range(M_HEADS):
            c_scr[i, h] = c_new[(i, h)]
        m_scr[i, 0:M_HEADS, :] = jnp.concatenate([m_new_all[(i, h)] for h in range(M_HEADS)], axis=0)

    @pl.when(c == nc - 1)
    def _():
        c_out_ref[...] = c_scr[...]
        m_out_ref[...] = m_scr[...]


def _mlstm(qt, km, vt, gc, gr, c0, m0, batch, n_valid, name):
    L = M_CHUNK
    nc = km.shape[0] // (batch * L)
    nseq = MLSTM_SEQS if batch % MLSTM_SEQS == 0 else 1
    kern = functools.partial(_mlstm_kernel, n_valid=n_valid, nseq=nseq)
    in_specs, operands = [], []
    for i in range(nseq):
        blk = functools.partial(lambda b, c, i: (b * nseq + i) * nc + c, i=i)
        rows = pl.BlockSpec((L, M_WIDTH), functools.partial(lambda b, c, blk: (blk(b, c), 0), blk=blk))
        cols = pl.BlockSpec((M_WIDTH, L), functools.partial(lambda b, c, blk: (0, blk(b, c)), blk=blk))
        in_specs += [cols, rows, cols,
                     pl.BlockSpec((L, LANES), functools.partial(lambda b, c, blk: (blk(b, c), 0), blk=blk)),
                     pl.BlockSpec((8, L), functools.partial(lambda b, c, blk: (0, blk(b, c)), blk=blk))]
        operands += [qt, km, vt, gc, gr]
    in_specs += [pl.BlockSpec((M_HEADS, M_DV + 8, M_DK), lambda b, c: (0, 0, 0)),
                 pl.BlockSpec((8, LANES), lambda b, c: (0, 0))]
    h4, c_fin, m_fin = pl.pallas_call(
        kern,
        out_shape=[jax.ShapeDtypeStruct((batch // nseq, nseq, nc * L, M_WIDTH), F32),
                   jax.ShapeDtypeStruct((batch, M_HEADS, M_DV + 8, M_DK), F32),
                   jax.ShapeDtypeStruct((batch, 8, LANES), F32)],
        grid=(batch // nseq, nc),
        in_specs=in_specs,
        out_specs=[pl.BlockSpec((1, nseq, L, M_WIDTH), lambda b, c: (b, 0, c, 0)),
                   pl.BlockSpec((nseq, M_HEADS, M_DV + 8, M_DK), lambda b, c: (b, 0, 0, 0)),
                   pl.BlockSpec((nseq, 8, LANES), lambda b, c: (b, 0, 0))],
        scratch_shapes=[pltpu.VMEM((nseq, M_HEADS, M_DV + 8, M_DK), F32), pltpu.VMEM((nseq, 8, LANES), F32)],
        compiler_params=_cparams("arbitrary", "arbitrary"), name=name,
    )(*operands, c0, m0)
    return h4.reshape(batch * nc * L, M_WIDTH), c_fin, m_fin


def _outer_f32(a, b):
    ah, am, al = (t.astype(F32) for t in _split3(a))
    bh, bm, bl = (t.astype(F32) for t in _split3(b))
    z = jnp.zeros_like(ah)
    lhs = jnp.concatenate([ah, ah, ah, am, am, al, z, z], axis=0).astype(BF16)
    rhs = jnp.concatenate([bh, bm, bl, bh, bm, bh, z, z], axis=0).astype(BF16)
    return lax.dot_general(lhs, rhs, (((0,), (0,)), ((), ())), preferred_element_type=F32)


def _mlstm_step_kernel(c_ref, n_ref, m_ref, gc_ref, q_ref, k_ref, v_ref,
                       c_out_ref, n_out_ref, m_out_ref, h_ref):
    g = c_ref.shape[0]
    assert g == 8
    scale = M_DK ** -0.5
    ig = gc_ref[:, 0:M_HEADS]
    lf = gc_ref[:, M_HEADS:2 * M_HEADS]
    m = m_ref[:, 0:M_HEADS]
    m_t = jnp.maximum(lf + m, ig)
    w = jnp.exp(lf + m - m_t)
    wg = jnp.exp(ig - m_t)
    floor = jnp.exp(-m_t)
    m_out_ref[...] = jnp.zeros_like(m_out_ref)
    m_out_ref[:, 0:M_HEADS] = m_t
    row8 = lax.broadcasted_iota(I32, (g, M_DV), 0)

    per_head = []
    for h in range(M_HEADS):
        sl = slice(h * M_DK, (h + 1) * M_DK)
        q = q_ref[:, sl]
        k = k_ref[:, sl] * scale
        v = v_ref[:, sl]
        n = n_ref[:, h, :]
        w_h, wg_h = w[:, h:h + 1], wg[:, h:h + 1]
        s = jnp.sum(q * k, axis=1, keepdims=True) * wg_h
        den = w_h * jnp.sum(n * q, axis=1, keepdims=True) + s
        n_out_ref[:, h, :] = w_h * n + wg_h * k
        per_head.append((q.astype(BF16), k, wg_h * v, w_h, s * v, 1.0 / jnp.maximum(jnp.abs(den), floor[:, h:h + 1])))

    def stage_a(h, j):
        qb = per_head[h][0]
        r = lax.dot_general(qb, c_ref[j, h].astype(BF16), (((1,), (1,)), ((), ())), preferred_element_type=F32)
        return jnp.where(row8 == j, r, 0.0)

    def stage_b(h, j):
        _, k, wv, w_h, _, _ = per_head[h]
        c_out_ref[j, h] = w_h[j:j + 1, :] * c_ref[j, h] + _outer_f32(wv[j:j + 1, :], k[j:j + 1, :])

    units = [(h, j) for h in range(M_HEADS) for j in range(g)]
    lag = STEP_PIPE_LAG
    cq = [jnp.zeros((g, M_DV), F32) for _ in range(M_HEADS)]
    for i in range(len(units) + lag):
        if i < len(units):
            cq[units[i][0]] = cq[units[i][0]] + stage_a(*units[i])
        if 0 <= i - lag < len(units):
            stage_b(*units[i - lag])
    for h in range(M_HEADS):
        _, _, _, w_h, sv, inv = per_head[h]
        h_ref[:, h * M_DK:(h + 1) * M_DK] = (w_h * cq[h] + sv) * inv


def _mlstm_step(c, n, m_pad, gc, q, k, v):
    nb = c.shape[0]
    g = SAMPLE_GROUP
    row = lambda w: pl.BlockSpec((g, w), lambda i: (i, 0))
    return pl.pallas_call(
        _mlstm_step_kernel,
        out_shape=[jax.ShapeDtypeStruct(c.shape, F32), jax.ShapeDtypeStruct(n.shape, F32),
                   jax.ShapeDtypeStruct((nb, LANES), F32), jax.ShapeDtypeStruct((nb, M_WIDTH), F32)],
        grid=(nb // g,),
        in_specs=[pl.BlockSpec((g, M_HEADS, M_DV, M_DK), lambda i: (i, 0, 0, 0)),
                  pl.BlockSpec((g, M_HEADS, M_DK), lambda i: (i, 0, 0)),
                  row(LANES), row(LANES), row(M_WIDTH), row(M_WIDTH), row(M_WIDTH)],
        out_specs=[pl.BlockSpec((g, M_HEADS, M_DV, M_DK), lambda i: (i, 0, 0, 0)),
                   pl.BlockSpec((g, M_HEADS, M_DK), lambda i: (i, 0, 0)),
                   row(LANES), row(M_WIDTH)],
        compiler_params=_cparams("arbitrary"), name="mlstm_step",
    )(c, n, m_pad, gc, q, k, v)


def _swa_kernel(qt_ref, kc_ref, kp_ref, vtc_ref, vtp_ref, km_ref, vtm_ref, bias_ref, sink_ref, o_ref):
    j = pl.program_id(1)
    first = j == 0
    blk = WINDOW
    nqb = qt_ref.shape[1] // blk
    kp = jnp.where(first, km_ref[...], kp_ref[...])
    vtp = jnp.where(first, vtm_ref[...], vtp_ref[...])
    k = jnp.concatenate([kp, kc_ref[...]], axis=0)
    vt = jnp.concatenate([vtp, vtc_ref[...]], axis=1)
    row_v = lax.broadcasted_iota(I32, vt.shape, 0)
    zero_v = jnp.zeros_like(vt)
    vt_half = (jnp.where(row_v < A_HD, vt, zero_v), jnp.where(row_v >= A_HD, vt, zero_v))
    row_q = lax.broadcasted_iota(I32, (LANES, blk), 0)
    lo_rows = row_q < A_HD
    def scores(u, p):
        cols = slice(u * blk, (u + 1) * blk)
        keys = slice(u * blk, (u + 2) * blk)
        qs = qt_ref[p * LANES:(p + 1) * LANES, cols]
        zero_q = jnp.zeros_like(qs)
        q_own = (jnp.where(lo_rows, qs, zero_q), jnp.where(lo_rows, zero_q, qs))
        return [jnp.dot(k[keys], q_own[half], preferred_element_type=F32) for half in range(2)]

    def softmax(u, p, s2):
        table = jnp.where(first, 0, 1) if u == 0 else 1
        probs, inv = [], []
        for half in range(2):
            hd = HEAD_ORDER[2 * p + half]
            s = s2[half] + bias_ref[table, hd]
            sk = sink_ref[hd:hd + 1, 0:1]
            m = jnp.maximum(jnp.max(s, axis=0, keepdims=True), sk)
            e = jnp.exp(s - m)
            probs.append(e.astype(BF16))
            inv.append(1.0 / (jnp.sum(e, axis=0, keepdims=True) + jnp.exp(sk - m)))
        return jnp.concatenate(probs, axis=0), jnp.where(lo_rows, inv[0], inv[1])

    def values(u, p, probs, inv):
        keys = slice(u * blk, (u + 2) * blk)
        vt_stack = jnp.concatenate([vt_half[0][:, keys], vt_half[1][:, keys]], axis=1)
        ot = jnp.dot(vt_stack, probs, preferred_element_type=F32)
        o_ref[u * blk:(u + 1) * blk, p * LANES:(p + 1) * LANES] = (ot * inv).T

    units = [(u, p) for u in range(nqb) for p in range(A_GROUP)]
    s_q, p_q = {}, {}
    lag = SWA_PIPE_LAG
    for i in range(len(units) + 2 * lag):
        if i < len(units):
            s_q[i] = scores(*units[i])
        if 0 <= i - lag < len(units):
            p_q[i - lag] = softmax(*units[i - lag], s_q.pop(i - lag))
        if 0 <= i - 2 * lag < len(units):
            values(*units[i - 2 * lag], *p_q.pop(i - 2 * lag))


def _swa(qat, ka, vat, kmeta, vtmeta, bias_t, sinks, batch):
    blk = WINDOW
    nqb = SWA_QBLOCKS
    t = ka.shape[0]
    nq = t // (batch * blk * nqb)
    prev = lambda b, j: (b * nq + j) * nqb + jnp.where(j == 0, 0, -1)
    const2 = lambda shape: pl.BlockSpec(shape, lambda b, j: (0, 0))
    return pl.pallas_call(
        _swa_kernel, out_shape=jax.ShapeDtypeStruct((t, A_WIDTH), F32), grid=(batch, nq),
        in_specs=[pl.BlockSpec((A_WIDTH, nqb * blk), lambda b, j: (0, b * nq + j)),
                  pl.BlockSpec((nqb * blk, LANES), lambda b, j: (b * nq + j, 0)),
                  pl.BlockSpec((blk, LANES), lambda b, j: (prev(b, j), 0)),
                  pl.BlockSpec((LANES, nqb * blk), lambda b, j: (0, b * nq + j)),
                  pl.BlockSpec((LANES, blk), lambda b, j: (0, prev(b, j))),
                  const2((blk, LANES)), const2((LANES, blk)),
                  pl.BlockSpec(bias_t.shape, lambda b, j: (0, 0, 0, 0)),
                  const2((8, LANES))],
        out_specs=pl.BlockSpec((nqb * blk, A_WIDTH), lambda b, j: (b * nq + j, 0)),
        compiler_params=_cparams("arbitrary", "arbitrary"), name="swa_prompt",
    )(qat, ka, ka, vat, vat, kmeta, vtmeta, bias_t, sinks)


def _swa_step_kernel(ck_ref, cv_ref, q_ref, k_ref, v_ref, bias_ref, sink_ref, ko_ref, vo_ref, o_ref):
    g = ck_ref.shape[0]
    lane = lax.broadcasted_iota(I32, (A_HEADS, LANES), 1)
    row = lax.broadcasted_iota(I32, (A_HEADS, LANES), 0)
    own_half = (row % 2 == 0) == (lane < A_HD)
    bias = bias_ref[...]
    sk = sink_ref[:, 0:1]
    def stage_a(j):
        ko_ref[j, 0:WINDOW - 1, :] = ck_ref[j, 1:WINDOW, :]
        ko_ref[j, WINDOW - 1:WINDOW, :] = k_ref[j:j + 1, :]
        vo_ref[j, 0:WINDOW - 1, :] = cv_ref[j, 1:WINDOW, :]
        vo_ref[j, WINDOW - 1:WINDOW, :] = v_ref[j:j + 1, :]
        kk = ko_ref[j].astype(BF16)
        slabs = [q_ref[j:j + 1, p * LANES:(p + 1) * LANES] for p in range(A_GROUP)]
        q8 = jnp.concatenate([slabs[r // 2] for r in range(A_HEADS)], axis=0)
        q8 = jnp.where(own_half, q8, 0.0).astype(BF16)
        s = lax.dot_general(q8, kk, (((1,), (1,)), ((), ())), preferred_element_type=F32)
        s = s + bias
        m = jnp.maximum(jnp.max(s, axis=1, keepdims=True), sk)
        e = jnp.exp(s - m)
        return e.astype(BF16), 1.0 / (jnp.sum(e, axis=1, keepdims=True) + jnp.exp(sk - m))

    def stage_b(j, p8, inv):
        vv = vo_ref[j].astype(BF16)
        o8 = jnp.where(own_half, jnp.dot(p8, vv, preferred_element_type=F32) * inv, 0.0)
        for p in range(A_GROUP):
            o_ref[j:j + 1, p * LANES:(p + 1) * LANES] = o8[2 * p:2 * p + 1, :] + o8[2 * p + 1:2 * p + 2, :]

    lag, pending = STEP_PIPE_LAG, {}
    for i in range(g + lag):
        if i < g:
            pending[i] = stage_a(i)
        if 0 <= i - lag < g:
            stage_b(i - lag, *pending.pop(i - lag))


def _swa_step(ck, cv, q, k, v, bias_rows, sinks):
    nb = ck.shape[0]
    g = SAMPLE_GROUP
    cache = pl.BlockSpec((g, WINDOW, LANES), lambda i: (i, 0, 0))
    row = lambda w: pl.BlockSpec((g, w), lambda i: (i, 0))
    const = lambda a: pl.BlockSpec(a.shape, lambda i: (0, 0))
    return pl.pallas_call(
        _swa_step_kernel,
        out_shape=[jax.ShapeDtypeStruct(ck.shape, F32), jax.ShapeDtypeStruct(cv.shape, F32),
                   jax.ShapeDtypeStruct((nb, A_WIDTH), F32)],
        grid=(nb // g,),
        in_specs=[cache, cache, row(A_WIDTH), row(LANES), row(LANES), const(bias_rows), const(sinks)],
        out_specs=[cache, cache, row(A_WIDTH)],
        compiler_params=_cparams("arbitrary"), name="swa_step",
    )(ck, cv, q, k, v, bias_rows, sinks)


def _layer_norm(z, g, b):
    mu = jnp.mean(z, axis=1, keepdims=True)
    zc = z - mu
    var = jnp.mean(zc * zc, axis=1, keepdims=True)
    return zc * lax.rsqrt(var + LN_EPS) * g + b


def _pack_halves(x):
    w = x.shape[1] // 2
    lo = pltpu.bitcast(x[:, :w].astype(BF16).astype(F32), U32)
    hi = pltpu.bitcast(x[:, w:].astype(BF16).astype(F32), U32)
    return (lo >> 16) | (hi & jnp.uint32(0xFFFF0000))


def _unpack_halves(words):
    lo = pltpu.bitcast(words << 16, F32).astype(BF16)
    hi = pltpu.bitcast(words & jnp.uint32(0xFFFF0000), F32).astype(BF16)
    return lo, hi


def _to_token_tiles(ref, x):
    for q in range(x.shape[1] // LANES):
        ref[:, q, :] = x[:, q * LANES:(q + 1) * LANES]


def _merge_kernel(h_ref, om_ref, att_ref, x_ref, gm_ref, ga_ref, wo_ref, g1_ref, b1_ref, wr_ref, br_ref,
                  x1_ref, xp_ref, tk_ref, cnt_ref):
    @pl.when(pl.program_id(0) == 0)
    def _():
        cnt_ref[...] = jnp.zeros_like(cnt_ref)

    hm = h_ref[...] * _sigmoid(om_ref[...])
    ym = hm * lax.rsqrt(jnp.mean(hm * hm, axis=1, keepdims=True) + LN_EPS) * gm_ref[...]
    att = att_ref[...]
    ya = att * lax.rsqrt(jnp.mean(att * att, axis=1, keepdims=True) + LN_EPS) * ga_ref[...]
    mix = (jnp.dot(ym.astype(BF16), wo_ref[0:M_WIDTH, :], preferred_element_type=F32)
           + jnp.dot(ya.astype(BF16), wo_ref[M_WIDTH:, :], preferred_element_type=F32))
    x1 = _layer_norm(DN_ALPHA * x_ref[...] + mix, g1_ref[...], b1_ref[...])
    x1_ref[...] = x1
    _to_token_tiles(xp_ref, _pack_halves(x1))
    logits = jnp.dot(x1.astype(BF16), wr_ref[...], preferred_element_type=F32) + br_ref[...]
    lane = lax.broadcasted_iota(I32, logits.shape, 1).astype(F32)
    vals, idxs = [], []
    for _ in range(TOP_K):
        mx = jnp.max(logits, axis=1, keepdims=True)
        idx = jnp.min(jnp.where(logits == mx, lane, float(LANES)), axis=1, keepdims=True)
        vals.append(mx)
        idxs.append(idx)
        logits = jnp.where(lane == idx, 2.0 * NEG, logits)
    es = [jnp.exp(vk - vals[0]) for vk in vals]
    tot = es[0] + es[1] + es[2] + es[3]
    tk = jnp.zeros(logits.shape, F32)
    picked = jnp.zeros(logits.shape, F32)
    for k in range(TOP_K):
        tk = jnp.where(lane == float(k), es[k] / tot, tk)
        tk = jnp.where(lane == float(TOP_K + k), idxs[k], tk)
        picked = jnp.where(lane == idxs[k], 1.0, picked)
    tk_ref[...] = tk
    cnt_ref[...] = cnt_ref[...] + jnp.sum(picked, axis=0, keepdims=True)


def _merge(h, om, att, x, gm, ga, wo, g1, b1, wr, br, tile, name):
    t = x.shape[0]
    rows = lambda w: pl.BlockSpec((tile, w), lambda i: (i, 0))
    const = lambda a: pl.BlockSpec(a.shape, lambda i: (0, 0))
    return pl.pallas_call(
        _merge_kernel,
        out_shape=[jax.ShapeDtypeStruct((t, D_MODEL), F32), jax.ShapeDtypeStruct((t, XP_TILE, LANES), U32),
                   jax.ShapeDtypeStruct((t, LANES), F32), jax.ShapeDtypeStruct((8, LANES), F32)],
        grid=(t // tile,),
        in_specs=[rows(M_WIDTH), rows(M_WIDTH), rows(A_WIDTH), rows(D_MODEL), const(gm), const(ga), const(wo),
                  const(g1), const(b1), const(wr), const(br)],
        out_specs=[rows(D_MODEL), pl.BlockSpec((tile, XP_TILE, LANES), lambda i: (i, 0, 0)), rows(LANES),
                   pl.BlockSpec((8, LANES), lambda i: (0, 0))],
        compiler_params=_cparams("arbitrary"), name=name,
    )(h, om, att, x, gm, ga, wo, g1, b1, wr, br)


def _route_kernel(tk_ref, first_ref, strict_ref, dest_ref, next_scr):
    @pl.when(pl.program_id(0) == 0)
    def _():
        next_scr[...] = first_ref[...]

    tk = tk_ref[...]
    lane = lax.broadcasted_iota(I32, tk.shape, 1).astype(F32)
    onehots = [jnp.where(lane == tk[:, TOP_K + k:TOP_K + k + 1], 1.0, 0.0) for k in range(TOP_K)]
    tot = onehots[0] + onehots[1] + onehots[2] + onehots[3]
    row = jnp.dot(strict_ref[...], tot.astype(BF16), preferred_element_type=F32) + next_scr[0:1, :]
    out = jnp.zeros(tk.shape, F32)
    for k in range(TOP_K):
        out = jnp.where(lane == float(k), jnp.sum(onehots[k] * row, axis=1, keepdims=True), out)
    dest_ref[...] = out.astype(I32)
    next_scr[...] = next_scr[...] + jnp.sum(tot, axis=0, keepdims=True)


def _route(tk, first):
    t = tk.shape[0]
    tile = min(RANK_TILE, t)
    strict = jnp.asarray(np.tril(np.ones((tile, tile), np.float32), -1), BF16)
    return pl.pallas_call(
        _route_kernel, out_shape=jax.ShapeDtypeStruct((t, LANES), I32), grid=(t // tile,),
        in_specs=[pl.BlockSpec((tile, LANES), lambda i: (i, 0)), pl.BlockSpec((8, LANES), lambda i: (0, 0)),
                  pl.BlockSpec((tile, tile), lambda i: (0, 0))],
        out_specs=pl.BlockSpec((tile, LANES), lambda i: (i, 0)),
        scratch_shapes=[pltpu.VMEM((8, LANES), F32)],
        compiler_params=_cparams("arbitrary"), name="moe_route",
    )(tk, first, strict)


def _offsets_kernel(cnt_ref, off_ref, be_ref, nu_ref, pad_ref, *, tile):
    cnt = cnt_ref[...]
    nblk = jnp.floor((cnt + float(tile - 1)) * (1.0 / tile))
    r_i = lax.broadcasted_iota(I32, (LANES, LANES), 0)
    c_i = lax.broadcasted_iota(I32, (LANES, LANES), 1)
    incl = jnp.where(r_i <= c_i, 1.0, 0.0).astype(BF16)
    cum = jnp.dot(nblk.astype(BF16), incl, preferred_element_type=F32)
    off = (cum - nblk) * float(tile)
    off_ref[...] = off
    which = lax.broadcasted_iota(I32, cnt.shape, 0)
    pad_ref[...] = jnp.where(which == 0, off + cnt, jnp.where(which == 1, nblk * float(tile) - cnt, 0.0)).astype(I32)
    rows = be_ref.shape[0]
    jb = (lax.broadcasted_iota(I32, (rows, LANES), 0) * LANES + lax.broadcasted_iota(I32, (rows, LANES), 1)).astype(F32)
    acc = jnp.zeros((rows, LANES), F32)
    for e in range(N_EXPERTS):
        acc = acc + jnp.where(jb >= cum[0:1, e:e + 1], 1.0, 0.0)
    be_ref[...] = jnp.minimum(acc, float(N_EXPERTS - 1)).astype(I32)
    nu_ref[...] = jnp.broadcast_to(cum[0:1, N_EXPERTS - 1:N_EXPERTS], nu_ref.shape).astype(I32)


def _offsets(cnt, n_blocks, tile):
    rows = -(-n_blocks // LANES)
    rows = -(-rows // 8) * 8
    return pl.pallas_call(
        functools.partial(_offsets_kernel, tile=tile),
        out_shape=[jax.ShapeDtypeStruct((8, LANES), F32), jax.ShapeDtypeStruct((rows, LANES), I32),
                   jax.ShapeDtypeStruct((8, LANES), I32), jax.ShapeDtypeStruct((8, LANES), I32)],
        name="moe_offsets",
    )(cnt)


def _scatter_rows(dest_ref, xp_ref, xs_ref, sem):
    t = xp_ref.shape[0]

    def row_copy(tok, dst):
        return pltpu.make_async_copy(xp_ref.at[pl.ds(tok, 1)], xs_ref.at[pl.ds(dst, 1)], sem)

    def issue(grp, carry):
        base = pl.multiple_of(grp * ISSUE_GROUP, ISSUE_GROUP)
        for u in range(ISSUE_GROUP):
            for k in range(TOP_K):
                row_copy(base + u, dest_ref[(base + u) * TOP_K + k]).start(priority=k % 2)
        return carry

    lax.fori_loop(0, t // ISSUE_GROUP, issue, 0)
    for k in range(TOP_K):
        pltpu.make_async_copy(xp_ref, xs_ref.at[pl.ds(0, t)], sem).wait()


def _sc_scatter(rows, idx, n_out):
    t = rows.shape[0]
    n_workers = SC_CORES * SC_SUBCORES
    assert idx.shape[0] == TOP_K * t and t % (n_workers * SC_CHUNK) == 0
    per_worker = t // n_workers
    mesh = plsc.VectorSubcoreMesh(core_axis_name="c", subcore_axis_name="s", num_cores=SC_CORES,
                                  num_subcores=SC_SUBCORES)

    nbuf, chunk = SC_SCATTER_BUFS, SC_CHUNK // SC_SCATTER_BUFS
    assert per_worker % (nbuf * chunk) == 0

    @functools.partial(
        pl.kernel, mesh=mesh, out_type=jax.ShapeDtypeStruct((n_out,) + rows.shape[1:], rows.dtype),
        scratch_types=[pltpu.VMEM((chunk,), I32) for _ in range(nbuf * TOP_K)]
                      + [pltpu.VMEM((chunk,) + rows.shape[1:], rows.dtype) for _ in range(nbuf)]
                      + [pltpu.SemaphoreType.DMA for _ in range(nbuf + 1)])
    def scatter(rows_hbm, idx_hbm, out_hbm, *scratch):
        idx_v = [scratch[u * TOP_K:(u + 1) * TOP_K] for u in range(nbuf)]
        rows_v = scratch[nbuf * TOP_K:nbuf * TOP_K + nbuf]
        lsems, ssem = scratch[nbuf * TOP_K + nbuf:-1], scratch[-1]
        wid = lax.axis_index("s") * SC_CORES + lax.axis_index("c")
        base = wid * per_worker

        @pl.loop(0, per_worker // (nbuf * chunk))
        def _(j):
            loads = []
            for u in range(nbuf):
                off = pl.multiple_of(base + (j * nbuf + u) * chunk, chunk)
                loads.append([pltpu.async_copy(rows_hbm.at[pl.ds(off, chunk)], rows_v[u], lsems[u])]
                             + [pltpu.async_copy(idx_hbm.at[pl.ds(pl.multiple_of(k * t + off, chunk), chunk)],
                                                 idx_v[u][k], lsems[u]) for k in range(TOP_K)])
            scatters = []
            for u in range(nbuf):
                for cp in loads[u]:
                    cp.wait()
                scatters += [pltpu.async_copy(rows_v[u], out_hbm.at[idx_v[u][k]], ssem) for k in range(TOP_K)]
            for cp in scatters:
                cp.wait()

    return scatter(rows, idx)


def _dispatch_rest_kernel(pads_ref, dest2_ref, xp2_ref, xs_in_ref, xs_ref, sem, zsem, zbuf, *, block_rows):
    del xs_in_ref
    zr = zbuf.shape[0]
    n_blocks = xs_ref.shape[0] // block_rows
    zbuf[...] = jnp.zeros_like(zbuf)
    used = pads_ref[2 * N_EXPERTS]

    def pieces(e, act):
        start, n = pads_ref[e], pads_ref[N_EXPERTS + e]
        for sh in range(zr.bit_length() - 1, -1, -1):
            b = 1 << sh
            before = lax.shift_left(lax.shift_right_logical(n, sh + 1), sh + 1)

            @pl.when((n & b) != 0)
            def _():
                act(pltpu.make_async_copy(zbuf.at[pl.ds(0, b)], xs_ref.at[pl.ds(start + before, b)], zsem))

    def tail(jb, act):
        for h in range(block_rows // zr):
            act(pltpu.make_async_copy(zbuf, xs_ref.at[pl.ds(jb * block_rows + h * zr, zr)], zsem))

    for act in (lambda cp: cp.start(), lambda cp: cp.wait()):
        lax.fori_loop(0, N_EXPERTS, lambda e, c: (pieces(e, act), c)[1], 0)
        lax.fori_loop(used, n_blocks, lambda jb, c: (tail(jb, act), c)[1], 0)

    _scatter_rows(dest2_ref, xp2_ref, xs_ref, sem)


def _dispatch_rest(pads, dest2, xp2, xs, block_rows):
    return pl.pallas_call(
        functools.partial(_dispatch_rest_kernel, block_rows=block_rows),
        out_shape=jax.ShapeDtypeStruct(xs.shape, xs.dtype),
        in_specs=[pl.BlockSpec(memory_space=pltpu.SMEM), pl.BlockSpec(memory_space=pltpu.SMEM),
                  pl.BlockSpec(memory_space=pltpu.VMEM), pl.BlockSpec(memory_space=pl.ANY)],
        out_specs=pl.BlockSpec(memory_space=pl.ANY),
        scratch_shapes=[pltpu.SemaphoreType.DMA(()), pltpu.SemaphoreType.DMA(()),
                        pltpu.VMEM((EXPERT_TILE // 2,) + xp2.shape[1:], xp2.dtype)],
        input_output_aliases={3: 0},
        compiler_params=_cparams(), name="moe_dispatch_rest",
    )(pads, dest2, xp2, xs)


def _expert_kernel(be_ref, nu_ref, xs_ref, w1_ref, b1g_ref, b1l_ref, w2_ref, b2_ref, perm_ref, ys_ref,
                   w1g_scr, w1l_scr, w2_scr, xq_scr, y_scr, sem, osem):
    j = pl.program_id(0)
    active = j < nu_ref[0]
    changed = jnp.logical_or(j == 0, be_ref[j] != be_ref[jnp.maximum(j - 1, 0)])
    tm = y_scr.shape[0]
    slot = lax.rem(j, 2)

    def fetch(blk, slot):
        row0 = pl.multiple_of(blk * tm, tm)
        return [pltpu.make_async_copy(xs_ref.at[pl.ds(row0, tm), q, :], xq_scr.at[slot, q], sem.at[slot])
                for q in range(XP_TILE)]

    def put(blk):
        row0 = pl.multiple_of(blk * tm, tm)
        return [pltpu.make_async_copy(y_scr.at[:, q * LANES:(q + 1) * LANES], ys_ref.at[pl.ds(row0, tm), q, :], osem)
                for q in range(YS_TILE)]

    def emit(y):
        @pl.when(j > 0)
        def _():
            for cp in put(j - 1):
                cp.wait()

        y_scr[...] = _pack_halves(y)
        for cp in put(j):
            cp.start()

    @pl.when(j == 0)
    def _():
        for cp in fetch(0, 0):
            cp.start()

    @pl.when(j + 1 < nu_ref[0])
    def _():
        for cp in fetch(j + 1, 1 - slot):
            cp.start()

    @pl.when(jnp.logical_and(active, changed))
    def _():
        for c in range(2 * D_FF // 256):
            wc = w1_ref[0, :, c * 256:(c + 1) * 256].astype(BF16)
            d = jnp.dot(wc, perm_ref[...], preferred_element_type=F32).astype(BF16)
            w1g_scr[:, c * 128:(c + 1) * 128] = d[:, :128]
            w1l_scr[:, c * 128:(c + 1) * 128] = d[:, 128:]
        for c in range(D_FF // 256):
            w2_scr[c * 256:(c + 1) * 256, :] = w2_ref[0, c * 256:(c + 1) * 256, :].astype(BF16)

    @pl.when(active)
    def _():
        for cp in fetch(j, slot):
            cp.wait()
        lo, hi = _unpack_halves(jnp.concatenate([xq_scr[slot, q] for q in range(XP_TILE)], axis=1))
        xb = jnp.concatenate([lo, hi], axis=1)
        hg = jnp.dot(xb, w1g_scr[...], preferred_element_type=F32) + b1g_ref[0]
        hl = jnp.dot(xb, w1l_scr[...], preferred_element_type=F32) + b1l_ref[0]
        x_glu = jnp.minimum(hg, SWIGLU_LIMIT)
        x_lin = jnp.clip(hl, -SWIGLU_LIMIT, SWIGLU_LIMIT)
        a = x_glu * _sigmoid(SWIGLU_ALPHA * x_glu) * (x_lin + 1.0)
        emit(jnp.dot(a.astype(BF16), w2_scr[...], preferred_element_type=F32) + b2_ref[0])

    @pl.when(jnp.logical_not(active))
    def _():
        emit(jnp.zeros((tm, D_MODEL), F32))

    @pl.when(j == pl.num_programs(0) - 1)
    def _():
        for cp in put(j):
            cp.wait()


def _experts(be, nu, xs, w1, b1g, b1l, w2, b2, perm, tile):
    n_blocks = xs.shape[0] // tile
    grid_spec = pltpu.PrefetchScalarGridSpec(
        num_scalar_prefetch=2, grid=(n_blocks,),
        in_specs=[pl.BlockSpec(memory_space=pl.ANY),
                  pl.BlockSpec((1, D_MODEL, 2 * D_FF), lambda j, be, nu: (be[j], 0, 0)),
                  pl.BlockSpec((1, 1, D_FF), lambda j, be, nu: (be[j], 0, 0)),
                  pl.BlockSpec((1, 1, D_FF), lambda j, be, nu: (be[j], 0, 0)),
                  pl.BlockSpec((1, D_FF, D_MODEL), lambda j, be, nu: (be[j], 0, 0)),
                  pl.BlockSpec((1, 1, D_MODEL), lambda j, be, nu: (be[j], 0, 0)),
                  pl.BlockSpec((256, 256), lambda j, be, nu: (0, 0))],
        out_specs=pl.BlockSpec(memory_space=pl.ANY),
        scratch_shapes=[pltpu.VMEM((D_MODEL, D_FF), BF16), pltpu.VMEM((D_MODEL, D_FF), BF16),
                        pltpu.VMEM((D_FF, D_MODEL), BF16), pltpu.VMEM((2, XP_TILE, tile, LANES), U32),
                        pltpu.VMEM((tile, YS_TILE * LANES), U32), pltpu.SemaphoreType.DMA((2,)),
                        pltpu.SemaphoreType.DMA(())])
    return pl.pallas_call(
        _expert_kernel, out_shape=jax.ShapeDtypeStruct((xs.shape[0], YS_TILE, LANES), U32), grid_spec=grid_spec,
        compiler_params=_cparams("arbitrary"), name="moe_experts",
    )(be, nu, xs, w1, b1g, b1l, w2, b2, perm)


def _combine_kernel(dest_ref, next_ref, ys_ref, tk_ref, x1_ref, g2_ref, b2_ref, out_ref, buf, sem):
    i = pl.program_id(0)
    t = x1_ref.shape[0]
    slot = lax.rem(i, 2)

    def gather(idx_ref, s):
        def issue(grp, carry):
            base = pl.multiple_of(grp * 8, 8)
            for u in range(8):
                for k in range(TOP_K):
                    pltpu.make_async_copy(ys_ref.at[idx_ref[(base + u) * TOP_K + k]],
                                          buf.at[s, k, grp, :, u, :], sem.at[s]).start(priority=k % 2)
            return carry

        lax.fori_loop(0, t // 8, issue, 0)

    @pl.when(i == 0)
    def _():
        gather(dest_ref, 0)

    @pl.when(i + 1 < pl.num_programs(0))
    def _():
        gather(next_ref, 1 - slot)

    for k in range(TOP_K):
        for u in range(8):
            pltpu.make_async_copy(ys_ref.at[pl.ds(0, t // 8)], buf.at[slot, k, :, :, u, :], sem.at[slot]).wait()
    tk = tk_ref[...]
    los, his = [], []
    for q in range(YS_TILE):
        lo = hi = None
        for k in range(TOP_K):
            words = buf[slot, k, :, q].reshape(t, LANES)
            g = tk[:, k:k + 1]
            lo_k = g * pltpu.bitcast(words << 16, F32)
            hi_k = g * pltpu.bitcast(words & jnp.uint32(0xFFFF0000), F32)
            lo = lo_k if lo is None else lo + lo_k
            hi = hi_k if hi is None else hi + hi_k
        los.append(lo)
        his.append(hi)
    ff = jnp.concatenate(los + his, axis=1)
    out_ref[...] = _layer_norm(DN_ALPHA * x1_ref[...] + ff, g2_ref[...], b2_ref[...])


def _sc_gather(rows, idx):
    b = idx.shape[0]
    n_workers = SC_CORES * SC_SUBCORES
    assert b % (n_workers * SC_CHUNK) == 0
    per_worker = b // n_workers
    mesh = plsc.VectorSubcoreMesh(core_axis_name="c", subcore_axis_name="s", num_cores=SC_CORES,
                                  num_subcores=SC_SUBCORES)

    nbuf, chunk = SC_GATHER_BUFS, SC_CHUNK // SC_GATHER_BUFS
    assert per_worker % (nbuf * chunk) == 0

    @functools.partial(
        pl.kernel, mesh=mesh, out_type=jax.ShapeDtypeStruct((b,) + rows.shape[1:], rows.dtype),
        scratch_types=[pltpu.VMEM((chunk,), I32) for _ in range(nbuf)]
                      + [pltpu.VMEM((chunk,) + rows.shape[1:], rows.dtype) for _ in range(nbuf)]
                      + [pltpu.SemaphoreType.DMA for _ in range(nbuf + 1)])
    def gather(rows_hbm, idx_hbm, out_hbm, *scratch):
        idx_v, rows_v = scratch[:nbuf], scratch[nbuf:2 * nbuf]
        gsems, wsem = scratch[2 * nbuf:3 * nbuf], scratch[3 * nbuf]
        wid = lax.axis_index("s") * SC_CORES + lax.axis_index("c")
        base = wid * per_worker

        @pl.loop(0, per_worker // (nbuf * chunk))
        def _(j):
            offs = [pl.multiple_of(base + (j * nbuf + u) * chunk, chunk) for u in range(nbuf)]
            for u in range(nbuf):
                pltpu.sync_copy(idx_hbm.at[pl.ds(offs[u], chunk)], idx_v[u])
            gathers = [pltpu.async_copy(rows_hbm.at[idx_v[u]], rows_v[u], gsems[u]) for u in range(nbuf)]
            writes = []
            for u in range(nbuf):
                gathers[u].wait()
                writes.append(pltpu.async_copy(rows_v[u], out_hbm.at[pl.ds(offs[u], chunk)], wsem))
            for w in writes:
                w.wait()

    return gather(rows, idx)


def _combine_stream_kernel(g_ref, tk_ref, x1_ref, g2_ref, b2_ref, out_ref, buf, sem):
    i = pl.program_id(0)
    t = x1_ref.shape[0]
    slot = lax.rem(i, 2)

    def fetch(blk, s):
        row0 = pl.multiple_of(blk * t, t)
        return [pltpu.make_async_copy(g_ref.at[k, pl.ds(row0, t), q, :], buf.at[s, k, q], sem.at[s])
                for k in range(TOP_K) for q in range(YS_TILE)]

    @pl.when(i == 0)
    def _():
        for cp in fetch(0, 0):
            cp.start()

    @pl.when(i + 1 < pl.num_programs(0))
    def _():
        for cp in fetch(i + 1, 1 - slot):
            cp.start()

    for cp in fetch(i, slot):
        cp.wait()
    tk = tk_ref[...]
    los, his = [], []
    for q in range(YS_TILE):
        lo = hi = None
        for k in range(TOP_K):
            words = buf[slot, k, q]
            g = tk[:, k:k + 1]
            lo_k = g * pltpu.bitcast(words << 16, F32)
            hi_k = g * pltpu.bitcast(words & jnp.uint32(0xFFFF0000), F32)
            lo = lo_k if lo is None else lo + lo_k
            hi = hi_k if hi is None else hi + hi_k
        los.append(lo)
        his.append(hi)
    ff = jnp.concatenate(los + his, axis=1)
    out_ref[...] = _layer_norm(DN_ALPHA * x1_ref[...] + ff, g2_ref[...], b2_ref[...])


def _combine_stream(g4, tk, x1, g2, b2):
    t = x1.shape[0]
    tile = min(MERGE_TILE, t)
    return pl.pallas_call(
        _combine_stream_kernel, out_shape=jax.ShapeDtypeStruct((t, D_MODEL), F32), grid=(t // tile,),
        in_specs=[pl.BlockSpec(memory_space=pl.ANY),
                  pl.BlockSpec((tile, LANES), lambda i: (i, 0)),
                  pl.BlockSpec((tile, D_MODEL), lambda i: (i, 0)),
                  pl.BlockSpec((1, D_MODEL), lambda i: (0, 0)),
                  pl.BlockSpec((1, D_MODEL), lambda i: (0, 0))],
        out_specs=pl.BlockSpec((tile, D_MODEL), lambda i: (i, 0)),
        scratch_shapes=[pltpu.VMEM((2, TOP_K, YS_TILE, tile, LANES), U32), pltpu.SemaphoreType.DMA((2,))],
        compiler_params=_cparams("arbitrary"), name="moe_combine_stream",
    )(g4, tk, x1, g2, b2)


def _combine(dest_flat, ys, tk, x1, g2, b2):
    t = x1.shape[0]
    tile = min(ROW_TILE, t)
    n = t // tile
    return pl.pallas_call(
        _combine_kernel, out_shape=jax.ShapeDtypeStruct((t, D_MODEL), F32), grid=(n,),
        in_specs=[pl.BlockSpec((tile * TOP_K,), lambda i: (i,), memory_space=pltpu.SMEM),
                  pl.BlockSpec((tile * TOP_K,), lambda i: (jnp.minimum(i + 1, n - 1),), memory_space=pltpu.SMEM),
                  pl.BlockSpec(memory_space=pl.ANY),
                  pl.BlockSpec((tile, LANES), lambda i: (i, 0)),
                  pl.BlockSpec((tile, D_MODEL), lambda i: (i, 0)),
                  pl.BlockSpec((1, D_MODEL), lambda i: (0, 0)),
                  pl.BlockSpec((1, D_MODEL), lambda i: (0, 0))],
        out_specs=pl.BlockSpec((tile, D_MODEL), lambda i: (i, 0)),
        scratch_shapes=[pltpu.VMEM((2, TOP_K, tile // 8, YS_TILE, 8, LANES), U32), pltpu.SemaphoreType.DMA((2,))],
        compiler_params=_cparams("arbitrary"), name="moe_combine",
    )(dest_flat, dest_flat, ys, tk, x1, g2, b2)


def _rel_bucket(dist):
    exact = REL_BUCKETS // 2
    d = np.maximum(dist, 0)
    log_b = exact + (np.log(np.maximum(d, 1).astype(np.float32) / np.float32(exact))
                     / np.float32(math.log(REL_MAX_DIST / exact)) * np.float32(REL_BUCKETS - exact)).astype(np.int32)
    return np.where(d < exact, d, np.minimum(log_b, REL_BUCKETS - 1)).astype(np.int32)


def _bias_lookup(table, bucket, valid):
    bucket = jnp.asarray(bucket)[None]
    acc = jnp.zeros((table.shape[1],) + bucket.shape[1:], F32)
    for b in range(REL_BUCKETS):
        acc = jnp.where(bucket == b, table[b].reshape((-1,) + (1,) * (bucket.ndim - 1)), acc)
    return jnp.where(jnp.asarray(valid)[None], acc, NEG)


def _bias_tables(rel_bias):
    table = rel_bias.astype(F32)
    r = np.arange(WINDOW)[:, None]
    c = np.arange(2 * WINDOW)[None, :]
    dist = r + WINDOW - c
    valid = (dist >= 0) & (dist < WINDOW)
    dist0 = np.where(c < N_META, N_META + r - c, dist)
    valid0 = np.where(c < N_META, dist0 < WINDOW, (c >= WINDOW) & valid)
    both = jnp.stack([_bias_lookup(table, _rel_bucket(dist0), valid0), _bias_lookup(table, _rel_bucket(dist), valid)])
    dist_s = WINDOW - 1 - np.arange(WINDOW)
    rows = _bias_lookup(table[:, np.asarray(HEAD_ORDER)], _rel_bucket(dist_s), np.ones_like(dist_s, bool))
    return both, rows


def _perm_heads(a, axis):
    assert HEAD_ORDER == tuple(kv * A_GROUP + g for g in range(A_GROUP) for kv in range(A_KV_HEADS))
    shape = a.shape
    a = a.reshape(shape[:axis] + (A_KV_HEADS, A_GROUP, A_HD) + shape[axis + 1:])
    return jnp.swapaxes(a, axis, axis + 1).reshape(shape)


def _rep_rows(vec, rows=8):
    out = jnp.zeros((rows, LANES), F32)
    return out.at[:vec.shape[0], :].set(jnp.broadcast_to(vec.astype(F32)[:, None], (vec.shape[0], LANES)))


def kernel(x_prompt, x_sample, cache_swa_k, cache_swa_v, state_mlstm_C, state_mlstm_n, state_mlstm_m, meta_tokens, rel_bias, w_in, b_igate, b_fgate, attn_sinks, g_mlstm_out, g_attn_out, w_out, ln1_g, ln1_b, w_router, b_router, w_moe1, b_moe1, w_moe2, b_moe2, ln2_g, ln2_b):
    B, S, _ = x_prompt.shape
    NB = x_sample.shape[0]
    assert x_sample.shape[1] == 1 and w_in.shape[0] == 1
    assert S % PROJ_TILE == 0 and S % M_CHUNK == 0 and S % WINDOW == 0 and NB % SAMPLE_GROUP == 0
    l = 0

    assert IN_WIDTHS == (512, 512, 512, 512, 4, 4, 512, 128, 128)
    bf = lambda a: a.astype(BF16)
    w = w_in[l]
    n_main, n_gate = 4 * M_WIDTH, 2 * M_HEADS
    w_gate = w[:, n_main:n_main + n_gate]
    w_att = w[:, n_main + n_gate:]
    assert math.frexp(A_HD ** -0.5)[0] == 0.5
    w_qa = _perm_heads(w_att[:, :A_WIDTH], 1) * (A_HD ** -0.5)
    wr = bf(jnp.concatenate([w[:, :n_main], w_qa, w_att[:, A_WIDTH:],
                             jnp.pad(w_gate, ((0, 0), (0, LANES - n_gate)))], axis=1))
    wt = bf(jnp.concatenate([w[:, :M_WIDTH], w[:, 2 * M_WIDTH:3 * M_WIDTH], w_qa, w_att[:, A_WIDTH + LANES:],
                             w_gate], axis=1).T)
    b_gate = jnp.concatenate([b_igate[l], b_fgate[l]]).astype(F32)
    brow = jnp.pad(b_gate, (0, LANES - n_gate))[None, :]
    bcol = b_gate[:, None]
    plan_p = ((512, 512, "plain", BF16), (1536, 512, "plain", F32), (2560, 128, "plain", BF16),
              (2816, 128, "gate", F32))
    tplan_p = ((0, 512, "plain", BF16), (512, 512, "plain", BF16), (1024, 512, "plain", BF16),
               (1536, 128, "plain", BF16), (1664, 8, "gate", F32))
    plan_s = ((0, 512, "plain", F32), (512, 512, "plain", F32), (1024, 512, "plain", F32), (1536, 512, "plain", F32),
              (2048, 512, "plain", F32), (2560, 128, "plain", F32), (2688, 128, "plain", F32), (2816, 128, "gate", F32))

    bias_tab, bias_rows = _bias_tables(rel_bias)
    sinks = _rep_rows(attn_sinks[l])
    sinks_step = _rep_rows(attn_sinks[l][np.asarray(HEAD_ORDER)])
    g_m = g_mlstm_out[l].astype(F32)[None, :]
    g_a = _perm_heads(g_attn_out[l].astype(F32), 0)[None, :]
    wo = bf(jnp.concatenate([w_out[l][:M_WIDTH], _perm_heads(w_out[l][M_WIDTH:], 0)], axis=0))
    g1, b1 = ln1_g[l].astype(F32)[None, :], ln1_b[l].astype(F32)[None, :]
    g2, b2 = ln2_g[l].astype(F32)[None, :], ln2_b[l].astype(F32)[None, :]
    w_r = bf(jnp.pad(w_router[l], ((0, 0), (0, LANES - N_EXPERTS))))
    b_r = jnp.pad(b_router[l].astype(F32), (0, LANES - N_EXPERTS), constant_values=NEG)[None, :]
    b1g = b_moe1[l][:, 0::2].astype(F32)[:, None, :]
    b1l = b_moe1[l][:, 1::2].astype(F32)[:, None, :]
    b2e = b_moe2[l].astype(F32)[:, None, :]
    pj = np.zeros((256, 256), np.float32)
    pj[2 * np.arange(128), np.arange(128)] = 1.0
    pj[2 * np.arange(128) + 1, 128 + np.arange(128)] = 1.0
    perm = jnp.asarray(pj, BF16)

    xp2 = x_prompt.reshape(B * S, D_MODEL)
    km, om, ka, gc, qt, vt, qat, vat, gr, kv_tail = _proj(
        xp2, wr, wt, brow, bcol, plan_p, tplan_p, (2560, 256), PROJ_TILE, S, "proj_prompt")
    x_meta = jnp.pad(meta_tokens.astype(F32), ((0, M_CHUNK - N_META), (0, 0)))
    km0, _, ka0, gc0, qt0, vt0, _, vat0, gr0 = _proj(
        x_meta, wr, wt, brow, bcol, plan_p, tplan_p, None, M_CHUNK, M_CHUNK, "proj_meta")
    xs2 = x_sample.reshape(NB, D_MODEL)
    qm_s, km_s, vm_s, om_s, qa_s, ka_s, va_s, gc_s = _proj(
        xs2, wr, wt, brow, bcol, plan_s, (), None, NB, NB, "proj_sample")

    zero_c = jnp.zeros((M_HEADS, M_DV + 8, M_DK), F32)
    zero_m = jnp.zeros((8, LANES), F32)
    _, c_meta, m_meta = _mlstm(qt0, km0, vt0, gc0, gr0, zero_c, zero_m, 1, N_META, "mlstm_meta")
    h_p, c_p, m_p = _mlstm(qt, km, vt, gc, gr, c_meta[0], m_meta[0], B, M_CHUNK, "mlstm_prompt")
    C_p = c_p[:, :, :M_DV, :]
    n_p = c_p[:, :, M_DV, :]
    m_prompt = m_p[:, :M_HEADS, 0]
    m_pad = jnp.pad(state_mlstm_m[l].astype(F32), ((0, 0), (0, LANES - M_HEADS)))
    C_s, n_s, m_s, h_s = _mlstm_step(state_mlstm_C[l].astype(F32), state_mlstm_n[l].astype(F32), m_pad,
                                     gc_s, qm_s, km_s, vm_s)

    att_p = _swa(qat, ka, vat, ka0, vat0, jnp.swapaxes(bias_tab, 2, 3), sinks, B)
    ck = cache_swa_k[l].reshape(NB, WINDOW, LANES)
    cv = cache_swa_v[l].reshape(NB, WINDOW, LANES)
    k_new, v_new, att_s = _swa_step(ck, cv, qa_s, ka_s, va_s, bias_rows, sinks_step)

    x1_p, xpk_p, tk_p, cnt_p = _merge(h_p, om, att_p, xp2, g_m, g_a, wo, g1, b1, w_r, b_r, MERGE_TILE, "merge_prompt")
    x1_s, xpk_s, tk_s, cnt_s = _merge(h_s, om_s, att_s, xs2, g_m, g_a, wo, g1, b1, w_r, b_r, NB, "merge_sample")

    T_p = B * S
    assert T_p % RANK_TILE == 0 and T_p % ROW_TILE == 0
    n_blocks = -(-((T_p + NB) * TOP_K) // EXPERT_TILE) + N_EXPERTS
    off, be2, nu2, pad = _offsets(cnt_p + cnt_s, n_blocks, EXPERT_TILE)
    pads = jnp.concatenate([pad[0, :N_EXPERTS], pad[1, :N_EXPERTS], nu2[0, :1]])
    dest_p = _route(tk_p, off)[:, :TOP_K].reshape(-1)
    dest_s = _route(tk_s, off + cnt_p)[:, :TOP_K].reshape(-1)
    be = be2.reshape(-1)[:n_blocks]
    nu = nu2[0, :1]
    idx_p = dest_p.reshape(T_p, TOP_K).T.reshape(-1)
    xs = _sc_scatter(xpk_p, idx_p, n_blocks * EXPERT_TILE)
    xs = _dispatch_rest(pads, dest_s, xpk_s, xs, EXPERT_TILE)
    ys = _experts(be, nu, xs, w_moe1[l], b1g, b1l, w_moe2[l], b2e, perm, EXPERT_TILE)
    g4 = _sc_gather(ys, idx_p).reshape(TOP_K, T_p, YS_TILE, LANES)
    y_p = _combine_stream(g4, tk_p, x1_p, g2, b2)
    y_s = _combine(dest_s, ys, tk_s, x1_s, g2, b2)

    kv_tail = kv_tail.reshape(B, WINDOW, 2, A_KV_HEADS, A_HD)
    dt_k, dt_v = cache_swa_k.dtype, cache_swa_v.dtype
    return (y_p.reshape(B, S, D_MODEL).astype(x_prompt.dtype), y_s.reshape(NB, 1, D_MODEL).astype(x_sample.dtype),
            kv_tail[:, :, 0][None].astype(dt_k), kv_tail[:, :, 1][None].astype(dt_v),
            C_p[None].astype(state_mlstm_C.dtype), n_p[None].astype(state_mlstm_n.dtype),
            m_prompt[None].astype(state_mlstm_m.dtype),
            k_new.reshape(1, NB, WINDOW, A_KV_HEADS, A_HD).astype(dt_k),
            v_new.reshape(1, NB, WINDOW, A_KV_HEADS, A_HD).astype(dt_v),
            C_s[None].astype(state_mlstm_C.dtype), n_s[None].astype(state_mlstm_n.dtype),
            m_s[:, :M_HEADS][None].astype(state_mlstm_m.dtype))
```

```python
import functools
import math

import numpy as np
import jax
import jax.numpy as jnp
from jax import lax
from jax.experimental import pallas as pl
from jax.experimental.pallas import tpu as pltpu
from jax.experimental.pallas import tpu_sc as plsc

F32 = jnp.float32
BF16 = jnp.bfloat16
I32 = jnp.int32
U32 = jnp.uint32

D_MODEL = 1024
N_META = 16
M_HEADS = 4
M_DK = 128
M_DV = 128
M_WIDTH = M_HEADS * M_DV
A_HD = 64
A_HEADS = 8
A_KV_HEADS = 2
A_GROUP = A_HEADS // A_KV_HEADS
A_WIDTH = A_HEADS * A_HD
WINDOW = 128
REL_BUCKETS = 32
REL_MAX_DIST = 128
N_EXPERTS = 32
TOP_K = 4
D_FF = D_MODEL
SWIGLU_LIMIT = 7.0
SWIGLU_ALPHA = 1.702
DEPTH = 1
DN_ALPHA = (2.0 * DEPTH) ** 0.25
LN_EPS = 1e-5
IN_WIDTHS = (M_WIDTH, M_WIDTH, M_WIDTH, M_WIDTH, M_HEADS, M_HEADS, A_WIDTH, A_KV_HEADS * A_HD, A_KV_HEADS * A_HD)

LANES = 128
NEG = -1e30
VMEM_LIMIT = 56 * 1024 * 1024

M_CHUNK = 256
PROJ_TILE = 1024
MERGE_TILE = 1024
RANK_TILE = 1024
ROW_TILE = 512
EXPERT_TILE = 512
SAMPLE_GROUP = 8
ISSUE_GROUP = 8
SWA_QBLOCKS = 8
SC_CORES, SC_SUBCORES = 2, 16
SC_CHUNK = 128
SC_GATHER_BUFS = 1
SC_SCATTER_BUFS = 2
MLSTM_SEQS = 4
STEP_PIPE_LAG = 3
SWA_PIPE_LAG = 2
XP_TILE = D_MODEL // 2 // LANES
YS_TILE = D_MODEL // 2 // LANES
HEAD_ORDER = (0, 4, 1, 5, 2, 6, 3, 7)


def _cparams(*sem):
    return pltpu.CompilerParams(dimension_semantics=sem, vmem_limit_bytes=VMEM_LIMIT)


def _log_sigmoid(x):
    return jnp.minimum(x, 0.0) - jnp.log1p(jnp.exp(-jnp.abs(x)))


def _sigmoid(x):
    return 1.0 / (1.0 + jnp.exp(-x))


def _proj_kernel(x_ref, wr_ref, wt_ref, brow_ref, bcol_ref, *outs, row_plan, t_plan, tail_cols):
    xb = x_ref[...].astype(BF16)
    tm = xb.shape[0]
    o = 0
    for (c0, width, kind, _) in row_plan:
        r = jnp.dot(xb, wr_ref[:, c0:c0 + width], preferred_element_type=F32)
        if kind == "gate":
            r = r + brow_ref[...]
            lane = lax.broadcasted_iota(I32, r.shape, 1)
            r = jnp.where(lane < M_HEADS, r, _log_sigmoid(r))
        outs[o][...] = r.astype(outs[o].dtype)
        o += 1
    for (r0, nrows, kind, _) in t_plan:
        r = lax.dot_general(wt_ref[r0:r0 + nrows, :], xb, (((1,), (1,)), ((), ())), preferred_element_type=F32)
        if kind == "gate":
            r = r + bcol_ref[...]
            row = lax.broadcasted_iota(I32, r.shape, 0)
            r = jnp.where(row < M_HEADS, r, _log_sigmoid(r))
        outs[o][...] = r.astype(outs[o].dtype)
        o += 1
    if tail_cols is not None:
        c0, width = tail_cols
        outs[o][...] = jnp.dot(xb[tm - WINDOW:, :], wr_ref[:, c0:c0 + width], preferred_element_type=F32)


def _proj(x, wr, wt, brow, bcol, row_plan, t_plan, tail_cols, tile, rows_per_group, name):
    t = x.shape[0]
    nt = t // tile
    out_shape, out_specs = [], []
    for (_, width, _, dt) in row_plan:
        out_shape.append(jax.ShapeDtypeStruct((t, width), dt))
        out_specs.append(pl.BlockSpec((tile, width), lambda i: (i, 0)))
    for (_, nrows, _, dt) in t_plan:
        out_shape.append(jax.ShapeDtypeStruct((nrows, t), dt))
        out_specs.append(pl.BlockSpec((nrows, tile), lambda i: (0, i)))
    if tail_cols is not None:
        tiles_per_group = rows_per_group // tile
        out_shape.append(jax.ShapeDtypeStruct((t // rows_per_group * WINDOW, tail_cols[1]), F32))
        out_specs.append(pl.BlockSpec((WINDOW, tail_cols[1]), lambda i: (i // tiles_per_group, 0)))
    kern = functools.partial(_proj_kernel, row_plan=row_plan, t_plan=t_plan, tail_cols=tail_cols)
    return pl.pallas_call(
        kern, out_shape=out_shape, grid=(nt,),
        in_specs=[pl.BlockSpec((tile, D_MODEL), lambda i: (i, 0)),
                  pl.BlockSpec(wr.shape, lambda i: (0, 0)),
                  pl.BlockSpec(wt.shape, lambda i: (0, 0)),
                  pl.BlockSpec(brow.shape, lambda i: (0, 0)),
                  pl.BlockSpec(bcol.shape, lambda i: (0, 0))],
        out_specs=out_specs, compiler_params=_cparams("arbitrary"), name=name,
    )(x, wr, wt, brow, bcol)


def _split3(a):
    hi = a.astype(BF16)
    r1 = a - hi.astype(F32)
    mid = r1.astype(BF16)
    lo = (r1 - mid.astype(F32)).astype(BF16)
    return hi, mid, lo


def _mlstm_kernel(*refs, n_valid, nseq):
    seq_in = [refs[5 * i:5 * i + 5] for i in range(nseq)]
    c0_ref, m0_ref = refs[5 * nseq:5 * nseq + 2]
    h_ref, c_out_ref, m_out_ref, c_scr, m_scr = refs[5 * nseq + 2:]
    c = pl.program_id(1)
    nc = pl.num_programs(1)
    L = seq_in[0][1].shape[0]

    @pl.when(c == 0)
    def _():
        for i in range(nseq):
            c_scr[i] = c0_ref[...]
            m_scr[i] = m0_ref[...]

    r_i = lax.broadcasted_iota(I32, (L, L), 0)
    c_i = lax.broadcasted_iota(I32, (L, L), 1)
    upper = r_i <= c_i
    tril = jnp.where(c_i <= r_i, 1.0, 0.0).astype(BF16)
    triu = jnp.where(upper, 1.0, 0.0).astype(BF16)
    scale = M_DK ** -0.5
    ones_rows = jnp.where(lax.broadcasted_iota(I32, (8, L), 0) == 0, 1.0, 0.0).astype(BF16)

    gates = []
    for (_, _, _, gc_ref, gr_ref) in seq_in:
        gc = gc_ref[...]
        gr = gr_ref[...]
        if n_valid < L:
            rowc = lax.broadcasted_iota(I32, gc.shape, 0)
            lanec = lax.broadcasted_iota(I32, gc.shape, 1)
            gc = jnp.where(rowc < n_valid, gc, jnp.where(lanec < M_HEADS, NEG, 0.0))
            rowr = lax.broadcasted_iota(I32, gr.shape, 0)
            colr = lax.broadcasted_iota(I32, gr.shape, 1)
            gr = jnp.where(colr < n_valid, gr, jnp.where(rowr < M_HEADS, NEG, 0.0))
        b_cols = sum(jnp.dot(tril, part, preferred_element_type=F32) for part in _split3(gc))
        b_rows = sum(jnp.dot(part, triu, preferred_element_type=F32) for part in _split3(gr))
        gates.append((gc, gr, b_cols, b_rows))

    m_alls = [m_scr[i] for i in range(nseq)]
    c_alls = [[c_scr[i, h] for h in range(M_HEADS)] for i in range(nseq)]
    h_new, c_new, m_new_all = {}, {}, {}

    def operands(u):
        i, h = u
        qt_ref, k_ref, vt_ref = seq_in[i][:3]
        sl = slice(h * M_DK, (h + 1) * M_DK)
        return qt_ref[sl, :], k_ref[:, sl], jnp.concatenate([vt_ref[sl, :], ones_rows], axis=0)

    def stage_a(u):
        i, h = u
        gc, gr, b_cols, b_rows = gates[i]
        qt, k, vt_aug = operands(u)
        ig_r = gr[h:h + 1, :]
        b_r = b_rows[M_HEADS + h:M_HEADS + h + 1, :]
        m_prev = m_alls[i][h:h + 1, 0:1]
        cs = c_alls[i][h]
        qk = jnp.dot(k, qt, preferred_element_type=F32)
        inter = jnp.dot(cs.astype(BF16), qt, preferred_element_type=F32)
        b_last = b_r[:, L - 1:L]
        g = ig_r + b_last - b_r
        m_new = jnp.maximum(b_last + m_prev, jnp.max(g, axis=1, keepdims=True))
        a = jnp.exp(b_last + m_prev - m_new)
        wv = (vt_aug.astype(F32) * jnp.exp(g - m_new)).astype(BF16)
        c_new[u] = a * cs + jnp.dot(wv, k, preferred_element_type=F32) * scale
        m_new_all[u] = jnp.broadcast_to(m_new, (1, LANES))
        return qk, inter

    def stage_b(u, qk, inter):
        i, h = u
        gc, gr, b_cols, b_rows = gates[i]
        b_r = b_rows[M_HEADS + h:M_HEADS + h + 1, :]
        m_prev = m_alls[i][h:h + 1, 0:1]
        r_c = gc[:, h:h + 1] - b_cols[:, M_HEADS + h:M_HEADS + h + 1]
        dt = jnp.where(upper, b_r + r_c, NEG)
        m_t = jnp.maximum(b_r + m_prev, jnp.max(dt, axis=0, keepdims=True))
        st = (qk * (scale * jnp.exp(dt - m_t))).astype(BF16)
        return st, jnp.exp(b_r + m_prev - m_t) * inter, jnp.exp(-m_t)

    def stage_c(u, st, inter_w, floor):
        _, _, vt_aug = operands(u)
        nd = inter_w + jnp.dot(vt_aug, st, preferred_element_type=F32)
        den = nd[M_DV:M_DV + 1, :]
        h_new[u] = (nd[:M_DV, :] / jnp.maximum(jnp.abs(den), floor)).T

    units = [(i, h) for h in range(M_HEADS) for i in range(nseq)]
    a_q, b_q = {}, {}
    for n in range(len(units) + 2):
        if n < len(units):
            a_q[n] = stage_a(units[n])
        if 0 <= n - 1 < len(units):
            b_q[n - 1] = stage_b(units[n - 1], *a_q.pop(n - 1))
        if 0 <= n - 2 < len(units):
            stage_c(units[n - 2], *b_q.pop(n - 2))

    for i in range(nseq):
        h_ref[0, i] = jnp.concatenate([h_new[(i, h)] for h in range(M_HEADS)], axis=1)
        for h in range(M_HEADS):
            c_scr[i, h] = c_new[(i, h)]
        m_scr[i, 0:M_HEADS, :] = jnp.concatenate([m_new_all[(i, h)] for h in range(M_HEADS)], axis=0)

    @pl.when(c == nc - 1)
    def _():
        c_out_ref[...] = c_scr[...]
        m_out_ref[...] = m_scr[...]


def _mlstm(qt, km, vt, gc, gr, c0, m0, batch, n_valid, name):
    L = M_CHUNK
    nc = km.shape[0] // (batch * L)
    nseq = MLSTM_SEQS if batch % MLSTM_SEQS == 0 else 1
    kern = functools.partial(_mlstm_kernel, n_valid=n_valid, nseq=nseq)
    in_specs, operands = [], []
    for i in range(nseq):
        blk = functools.partial(lambda b, c, i: (b * nseq + i) * nc + c, i=i)
        rows = pl.BlockSpec((L, M_WIDTH), functools.partial(lambda b, c, blk: (blk(b, c), 0), blk=blk))
        cols = pl.BlockSpec((M_WIDTH, L), functools.partial(lambda b, c, blk: (0, blk(b, c)), blk=blk))
        in_specs += [cols, rows, cols,
                     pl.BlockSpec((L, LANES), functools.partial(lambda b, c, blk: (blk(b, c), 0), blk=blk)),
                     pl.BlockSpec((8, L), functools.partial(lambda b, c, blk: (0, blk(b, c)), blk=blk))]
        operands += [qt, km, vt, gc, gr]
    in_specs += [pl.BlockSpec((M_HEADS, M_DV + 8, M_DK), lambda b, c: (0, 0, 0)),
                 pl.BlockSpec((8, LANES), lambda b, c: (0, 0))]
    h4, c_fin, m_fin = pl.pallas_call(
        kern,
        out_shape=[jax.ShapeDtypeStruct((batch // nseq, nseq, nc * L, M_WIDTH), F32),
                   jax.ShapeDtypeStruct((batch, M_HEADS, M_DV + 8, M_DK), F32),
                   jax.ShapeDtypeStruct((batch, 8, LANES), F32)],
        grid=(batch // nseq, nc),
        in_specs=in_specs,
        out_specs=[pl.BlockSpec((1, nseq, L, M_WIDTH), lambda b, c: (b, 0, c, 0)),
                   pl.BlockSpec((nseq, M_HEADS, M_DV + 8, M_DK), lambda b, c: (b, 0, 0, 0)),
                   pl.BlockSpec((nseq, 8, LANES), lambda b, c: (b, 0, 0))],
        scratch_shapes=[pltpu.VMEM((nseq, M_HEADS, M_DV + 8, M_DK), F32), pltpu.VMEM((nseq, 8, LANES), F32)],
        compiler_params=_cparams("arbitrary", "arbitrary"), name=name,
    )(*operands, c0, m0)
    return h4.reshape(batch * nc * L, M_WIDTH), c_fin, m_fin


def _outer_f32(a, b):
    ah, am, al = (t.astype(F32) for t in _split3(a))
    bh, bm, bl = (t.astype(F32) for t in _split3(b))
    z = jnp.zeros_like(ah)
    lhs = jnp.concatenate([ah, ah, ah, am, am, al, z, z], axis=0).astype(BF16)
    rhs = jnp.concatenate([bh, bm, bl, bh, bm, bh, z, z], axis=0).astype(BF16)
    return lax.dot_general(lhs, rhs, (((0,), (0,)), ((), ())), preferred_element_type=F32)


def _mlstm_step_kernel(c_ref, n_ref, m_ref, gc_ref, q_ref, k_ref, v_ref,
                       c_out_ref, n_out_ref, m_out_ref, h_ref):
    g = c_ref.shape[0]
    assert g == 8
    scale = M_DK ** -0.5
    ig = gc_ref[:, 0:M_HEADS]
    lf = gc_ref[:, M_HEADS:2 * M_HEADS]
    m = m_ref[:, 0:M_HEADS]
    m_t = jnp.maximum(lf + m, ig)
    w = jnp.exp(lf + m - m_t)
    wg = jnp.exp(ig - m_t)
    floor = jnp.exp(-m_t)
    m_out_ref[...] = jnp.zeros_like(m_out_ref)
    m_out_ref[:, 0:M_HEADS] = m_t
    row8 = lax.broadcasted_iota(I32, (g, M_DV), 0)

    per_head = []
    for h in range(M_HEADS):
        sl = slice(h * M_DK, (h + 1) * M_DK)
        q = q_ref[:, sl]
        k = k_ref[:, sl] * scale
        v = v_ref[:, sl]
        n = n_ref[:, h, :]
        w_h, wg_h = w[:, h:h + 1], wg[:, h:h + 1]
        s = jnp.sum(q * k, axis=1, keepdims=True) * wg_h
        den = w_h * jnp.sum(n * q, axis=1, keepdims=True) + s
        n_out_ref[:, h, :] = w_h * n + wg_h * k
        per_head.append((q.astype(BF16), k, wg_h * v, w_h, s * v, 1.0 / jnp.maximum(jnp.abs(den), floor[:, h:h + 1])))

    def stage_a(h, j):
        qb = per_head[h][0]
        r = lax.dot_general(qb, c_ref[j, h].astype(BF16), (((1,), (1,)), ((), ())), preferred_element_type=F32)
        return jnp.where(row8 == j, r, 0.0)

    def stage_b(h, j):
        _, k, wv, w_h, _, _ = per_head[h]
        c_out_ref[j, h] = w_h[j:j + 1, :] * c_ref[j, h] + _outer_f32(wv[j:j + 1, :], k[j:j + 1, :])

    units = [(h, j) for h in range(M_HEADS) for j in range(g)]
    lag = STEP_PIPE_LAG
    cq = [jnp.zeros((g, M_DV), F32) for _ in range(M_HEADS)]
    for i in range(len(units) + lag):
        if i < len(units):
            cq[units[i][0]] = cq[units[i][0]] + stage_a(*units[i])
        if 0 <= i - lag < len(units):
            stage_b(*units[i - lag])
    for h in range(M_HEADS):
        _, _, _, w_h, sv, inv = per_head[h]
        h_ref[:, h * M_DK:(h + 1) * M_DK] = (w_h * cq[h] + sv) * inv


def _mlstm_step(c, n, m_pad, gc, q, k, v):
    nb = c.shape[0]
    g = SAMPLE_GROUP
    row = lambda w: pl.BlockSpec((g, w), lambda i: (i, 0))
    return pl.pallas_call(
        _mlstm_step_kernel,
        out_shape=[jax.ShapeDtypeStruct(c.shape, F32), jax.ShapeDtypeStruct(n.shape, F32),
                   jax.ShapeDtypeStruct((nb, LANES), F32), jax.ShapeDtypeStruct((nb, M_WIDTH), F32)],
        grid=(nb // g,),
        in_specs=[pl.BlockSpec((g, M_HEADS, M_DV, M_DK), lambda i: (i, 0, 0, 0)),
                  pl.BlockSpec((g, M_HEADS, M_DK), lambda i: (i, 0, 0)),
                  row(LANES), row(LANES), row(M_WIDTH), row(M_WIDTH), row(M_WIDTH)],
        out_specs=[pl.BlockSpec((g, M_HEADS, M_DV, M_DK), lambda i: (i, 0, 0, 0)),
                   pl.BlockSpec((g, M_HEADS, M_DK), lambda i: (i, 0, 0)),
                   row(LANES), row(M_WIDTH)],
        compiler_params=_cparams("arbitrary"), name="mlstm_step",
    )(c, n, m_pad, gc, q, k, v)


def _swa_kernel(qt_ref, kc_ref, kp_ref, vtc_ref, vtp_ref, km_ref, vtm_ref, bias_ref, sink_ref, o_ref):
    j = pl.program_id(1)
    first = j == 0
    blk = WINDOW
    nqb = qt_ref.shape[1] // blk
    kp = jnp.where(first, km_ref[...], kp_ref[...])
    vtp = jnp.where(first, vtm_ref[...], vtp_ref[...])
    k = jnp.concatenate([kp, kc_ref[...]], axis=0)
    vt = jnp.concatenate([vtp, vtc_ref[...]], axis=1)
    row_v = lax.broadcasted_iota(I32, vt.shape, 0)
    zero_v = jnp.zeros_like(vt)
    vt_half = (jnp.where(row_v < A_HD, vt, zero_v), jnp.where(row_v >= A_HD, vt, zero_v))
    row_q = lax.broadcasted_iota(I32, (LANES, blk), 0)
    lo_rows = row_q < A_HD
    def scores(u, p):
        cols = slice(u * blk, (u + 1) * blk)
        keys = slice(u * blk, (u + 2) * blk)
        qs = qt_ref[p * LANES:(p + 1) * LANES, cols]
        zero_q = jnp.zeros_like(qs)
        q_own = (jnp.where(lo_rows, qs, zero_q), jnp.where(lo_rows, zero_q, qs))
        return [jnp.dot(k[keys], q_own[half], preferred_element_type=F32) for half in range(2)]

    def softmax(u, p, s2):
        table = jnp.where(first, 0, 1) if u == 0 else 1
        probs, inv = [], []
        for half in range(2):
            hd = HEAD_ORDER[2 * p + half]
            s = s2[half] + bias_ref[table, hd]
            sk = sink_ref[hd:hd + 1, 0:1]
            m = jnp.maximum(jnp.max(s, axis=0, keepdims=True), sk)
            e = jnp.exp(s - m)
            probs.append(e.astype(BF16))
            inv.append(1.0 / (jnp.sum(e, axis=0, keepdims=True) + jnp.exp(sk - m)))
        return jnp.concatenate(probs, axis=0), jnp.where(lo_rows, inv[0], inv[1])

    def values(u, p, probs, inv):
        keys = slice(u * blk, (u + 2) * blk)
        vt_stack = jnp.concatenate([vt_half[0][:, keys], vt_half[1][:, keys]], axis=1)
        ot = jnp.dot(vt_stack, probs, preferred_element_type=F32)
        o_ref[u * blk:(u + 1) * blk, p * LANES:(p + 1) * LANES] = (ot * inv).T

    units = [(u, p) for u in range(nqb) for p in range(A_GROUP)]
    s_q, p_q = {}, {}
    lag = SWA_PIPE_LAG
    for i in range(len(units) + 2 * lag):
        if i < len(units):
            s_q[i] = scores(*units[i])
        if 0 <= i - lag < len(units):
            p_q[i - lag] = softmax(*units[i - lag], s_q.pop(i - lag))
        if 0 <= i - 2 * lag < len(units):
            values(*units[i - 2 * lag], *p_q.pop(i - 2 * lag))


def _swa(qat, ka, vat, kmeta, vtmeta, bias_t, sinks, batch):
    blk = WINDOW
    nqb = SWA_QBLOCKS
    t = ka.shape[0]
    nq = t // (batch * blk * nqb)
    prev = lambda b, j: (b * nq + j) * nqb + jnp.where(j == 0, 0, -1)
    const2 = lambda shape: pl.BlockSpec(shape, lambda b, j: (0, 0))
    return pl.pallas_call(
        _swa_kernel, out_shape=jax.ShapeDtypeStruct((t, A_WIDTH), F32), grid=(batch, nq),
        in_specs=[pl.BlockSpec((A_WIDTH, nqb * blk), lambda b, j: (0, b * nq + j)),
                  pl.BlockSpec((nqb * blk, LANES), lambda b, j: (b * nq + j, 0)),
                  pl.BlockSpec((blk, LANES), lambda b, j: (prev(b, j), 0)),
                  pl.BlockSpec((LANES, nqb * blk), lambda b, j: (0, b * nq + j)),
                  pl.BlockSpec((LANES, blk), lambda b, j: (0, prev(b, j))),
                  const2((blk, LANES)), const2((LANES, blk)),
                  pl.BlockSpec(bias_t.shape, lambda b, j: (0, 0, 0, 0)),
                  const2((8, LANES))],
        out_specs=pl.BlockSpec((nqb * blk, A_WIDTH), lambda b, j: (b * nq + j, 0)),
        compiler_params=_cparams("arbitrary", "arbitrary"), name="swa_prompt",
    )(qat, ka, ka, vat, vat, kmeta, vtmeta, bias_t, sinks)


def _swa_step_kernel(ck_ref, cv_ref, q_ref, k_ref, v_ref, bias_ref, sink_ref, ko_ref, vo_ref, o_ref):
    g = ck_ref.shape[0]
    lane = lax.broadcasted_iota(I32, (A_HEADS, LANES), 1)
    row = lax.broadcasted_iota(I32, (A_HEADS, LANES), 0)
    own_half = (row % 2 == 0) == (lane < A_HD)
    bias = bias_ref[...]
    sk = sink_ref[:, 0:1]
    def stage_a(j):
        ko_ref[j, 0:WINDOW - 1, :] = ck_ref[j, 1:WINDOW, :]
        ko_ref[j, WINDOW - 1:WINDOW, :] = k_ref[j:j + 1, :]
        vo_ref[j, 0:WINDOW - 1, :] = cv_ref[j, 1:WINDOW, :]
        vo_ref[j, WINDOW - 1:WINDOW, :] = v_ref[j:j + 1, :]
        kk = ko_ref[j].astype(BF16)
        slabs = [q_ref[j:j + 1, p * LANES:(p + 1) * LANES] for p in range(A_GROUP)]
        q8 = jnp.concatenate([slabs[r // 2] for r in range(A_HEADS)], axis=0)
        q8 = jnp.where(own_half, q8, 0.0).astype(BF16)
        s = lax.dot_general(q8, kk, (((1,), (1,)), ((), ())), preferred_element_type=F32)
        s = s + bias
        m = jnp.maximum(jnp.max(s, axis=1, keepdims=True), sk)
        e = jnp.exp(s - m)
        return e.astype(BF16), 1.0 / (jnp.sum(e, axis=1, keepdims=True) + jnp.exp(sk - m))

    def stage_b(j, p8, inv):
        vv = vo_ref[j].astype(BF16)
        o8 = jnp.where(own_half, jnp.dot(p8, vv, preferred_element_type=F32) * inv, 0.0)
        for p in range(A_GROUP):
            o_ref[j:j + 1, p * LANES:(p + 1) * LANES] = o8[2 * p:2 * p + 1, :] + o8[2 * p + 1:2 * p + 2, :]

    lag, pending = STEP_PIPE_LAG, {}
    for i in range(g + lag):
        if i < g:
            pending[i] = stage_a(i)
        if 0 <= i - lag < g:
            stage_b(i - lag, *pending.pop(i - lag))


def _swa_step(ck, cv, q, k, v, bias_rows, sinks):
    nb = ck.shape[0]
    g = SAMPLE_GROUP
    cache = pl.BlockSpec((g, WINDOW, LANES), lambda i: (i, 0, 0))
    row = lambda w: pl.BlockSpec((g, w), lambda i: (i, 0))
    const = lambda a: pl.BlockSpec(a.shape, lambda i: (0, 0))
    return pl.pallas_call(
        _swa_step_kernel,
        out_shape=[jax.ShapeDtypeStruct(ck.shape, F32), jax.ShapeDtypeStruct(cv.shape, F32),
                   jax.ShapeDtypeStruct((nb, A_WIDTH), F32)],
        grid=(nb // g,),
        in_specs=[cache, cache, row(A_WIDTH), row(LANES), row(LANES), const(bias_rows), const(sinks)],
        out_specs=[cache, cache, row(A_WIDTH)],
        compiler_params=_cparams("arbitrary"), name="swa_step",
    )(ck, cv, q, k, v, bias_rows, sinks)


def _layer_norm(z, g, b):
    mu = jnp.mean(z, axis=1, keepdims=True)
    zc = z - mu
    var = jnp.mean(zc * zc, axis=1, keepdims=True)
    return zc * lax.rsqrt(var + LN_EPS) * g + b


def _pack_halves(x):
    w = x.shape[1] // 2
    lo = pltpu.bitcast(x[:, :w].astype(BF16).astype(F32), U32)
    hi = pltpu.bitcast(x[:, w:].astype(BF16).astype(F32), U32)
    return (lo >> 16) | (hi & jnp.uint32(0xFFFF0000))


def _unpack_halves(words):
    lo = pltpu.bitcast(words << 16, F32).astype(BF16)
    hi = pltpu.bitcast(words & jnp.uint32(0xFFFF0000), F32).astype(BF16)
    return lo, hi


def _to_token_tiles(ref, x):
    for q in range(x.shape[1] // LANES):
        ref[:, q, :] = x[:, q * LANES:(q + 1) * LANES]


def _merge_kernel(h_ref, om_ref, att_ref, x_ref, gm_ref, ga_ref, wo_ref, g1_ref, b1_ref, wr_ref, br_ref,
                  x1_ref, xp_ref, tk_ref, cnt_ref):
    @pl.when(pl.program_id(0) == 0)
    def _():
        cnt_ref[...] = jnp.zeros_like(cnt_ref)

    hm = h_ref[...] * _sigmoid(om_ref[...])
    ym = hm * lax.rsqrt(jnp.mean(hm * hm, axis=1, keepdims=True) + LN_EPS) * gm_ref[...]
    att = att_ref[...]
    ya = att * lax.rsqrt(jnp.mean(att * att, axis=1, keepdims=True) + LN_EPS) * ga_ref[...]
    mix = (jnp.dot(ym.astype(BF16), wo_ref[0:M_WIDTH, :], preferred_element_type=F32)
           + jnp.dot(ya.astype(BF16), wo_ref[M_WIDTH:, :], preferred_element_type=F32))
    x1 = _layer_norm(DN_ALPHA * x_ref[...] + mix, g1_ref[...], b1_ref[...])
    x1_ref[...] = x1
    _to_token_tiles(xp_ref, _pack_halves(x1))
    logits = jnp.dot(x1.astype(BF16), wr_ref[...], preferred_element_type=F32) + br_ref[...]
    lane = lax.broadcasted_iota(I32, logits.shape, 1).astype(F32)
    vals, idxs = [], []
    for _ in range(TOP_K):
        mx = jnp.max(logits, axis=1, keepdims=True)
        idx = jnp.min(jnp.where(logits == mx, lane, float(LANES)), axis=1, keepdims=True)
        vals.append(mx)
        idxs.append(idx)
        logits = jnp.where(lane == idx, 2.0 * NEG, logits)
    es = [jnp.exp(vk - vals[0]) for vk in vals]
    tot = es[0] + es[1] + es[2] + es[3]
    tk = jnp.zeros(logits.shape, F32)
    picked = jnp.zeros(logits.shape, F32)
    for k in range(TOP_K):
        tk = jnp.where(lane == float(k), es[k] / tot, tk)
        tk = jnp.where(lane == float(TOP_K + k), idxs[k], tk)
        picked = jnp.where(lane == idxs[k], 1.0, picked)
    tk_ref[...] = tk
    cnt_ref[...] = cnt_ref[...] + jnp.sum(picked, axis=0, keepdims=True)


def _merge(h, om, att, x, gm, ga, wo, g1, b1, wr, br, tile, name):
    t = x.shape[0]
    rows = lambda w: pl.BlockSpec((tile, w), lambda i: (i, 0))
    const = lambda a: pl.BlockSpec(a.shape, lambda i: (0, 0))
    return pl.pallas_call(
        _merge_kernel,
        out_shape=[jax.ShapeDtypeStruct((t, D_MODEL), F32), jax.ShapeDtypeStruct((t, XP_TILE, LANES), U32),
                   jax.ShapeDtypeStruct((t, LANES), F32), jax.ShapeDtypeStruct((8, LANES), F32)],
        grid=(t // tile,),
        in_specs=[rows(M_WIDTH), rows(M_WIDTH), rows(A_WIDTH), rows(D_MODEL), const(gm), const(ga), const(wo),
                  const(g1), const(b1), const(wr), const(br)],
        out_specs=[rows(D_MODEL), pl.BlockSpec((tile, XP_TILE, LANES), lambda i: (i, 0, 0)), rows(LANES),
                   pl.BlockSpec((8, LANES), lambda i: (0, 0))],
        compiler_params=_cparams("arbitrary"), name=name,
    )(h, om, att, x, gm, ga, wo, g1, b1, wr, br)


def _route_kernel(tk_ref, first_ref, strict_ref, dest_ref, next_scr):
    @pl.when(pl.program_id(0) == 0)
    def _():
        next_scr[...] = first_ref[...]

    tk = tk_ref[...]
    lane = lax.broadcasted_iota(I32, tk.shape, 1).astype(F32)
    onehots = [jnp.where(lane == tk[:, TOP_K + k:TOP_K + k + 1], 1.0, 0.0) for k in range(TOP_K)]
    tot = onehots[0] + onehots[1] + onehots[2] + onehots[3]
    row = jnp.dot(strict_ref[...], tot.astype(BF16), preferred_element_type=F32) + next_scr[0:1, :]
    out = jnp.zeros(tk.shape, F32)
    for k in range(TOP_K):
        out = jnp.where(lane == float(k), jnp.sum(onehots[k] * row, axis=1, keepdims=True), out)
    dest_ref[...] = out.astype(I32)
    next_scr[...] = next_scr[...] + jnp.sum(tot, axis=0, keepdims=True)


def _route(tk, first):
    t = tk.shape[0]
    tile = min(RANK_TILE, t)
    strict = jnp.asarray(np.tril(np.ones((tile, tile), np.float32), -1), BF16)
    return pl.pallas_call(
        _route_kernel, out_shape=jax.ShapeDtypeStruct((t, LANES), I32), grid=(t // tile,),
        in_specs=[pl.BlockSpec((tile, LANES), lambda i: (i, 0)), pl.BlockSpec((8, LANES), lambda i: (0, 0)),
                  pl.BlockSpec((tile, tile), lambda i: (0, 0))],
        out_specs=pl.BlockSpec((tile, LANES), lambda i: (i, 0)),
        scratch_shapes=[pltpu.VMEM((8, LANES), F32)],
        compiler_params=_cparams("arbitrary"), name="moe_route",
    )(tk, first, strict)


def _offsets_kernel(cnt_ref, off_ref, be_ref, nu_ref, pad_ref, *, tile):
    cnt = cnt_ref[...]
    nblk = jnp.floor((cnt + float(tile - 1)) * (1.0 / tile))
    r_i = lax.broadcasted_iota(I32, (LANES, LANES), 0)
    c_i = lax.broadcasted_iota(I32, (LANES, LANES), 1)
    incl = jnp.where(r_i <= c_i, 1.0, 0.0).astype(BF16)
    cum = jnp.dot(nblk.astype(BF16), incl, preferred_element_type=F32)
    off = (cum - nblk) * float(tile)
    off_ref[...] = off
    which = lax.broadcasted_iota(I32, cnt.shape, 0)
    pad_ref[...] = jnp.where(which == 0, off + cnt, jnp.where(which == 1, nblk * float(tile) - cnt, 0.0)).astype(I32)
    rows = be_ref.shape[0]
    jb = (lax.broadcasted_iota(I32, (rows, LANES), 0) * LANES + lax.broadcasted_iota(I32, (rows, LANES), 1)).astype(F32)
    acc = jnp.zeros((rows, LANES), F32)
    for e in range(N_EXPERTS):
        acc = acc + jnp.where(jb >= cum[0:1, e:e + 1], 1.0, 0.0)
    be_ref[...] = jnp.minimum(acc, float(N_EXPERTS - 1)).astype(I32)
    nu_ref[...] = jnp.broadcast_to(cum[0:1, N_EXPERTS - 1:N_EXPERTS], nu_ref.shape).astype(I32)


def _offsets(cnt, n_blocks, tile):
    rows = -(-n_blocks // LANES)
    rows = -(-rows // 8) * 8
    return pl.pallas_call(
        functools.partial(_offsets_kernel, tile=tile),
        out_shape=[jax.ShapeDtypeStruct((8, LANES), F32), jax.ShapeDtypeStruct((rows, LANES), I32),
                   jax.ShapeDtypeStruct((8, LANES), I32), jax.ShapeDtypeStruct((8, LANES), I32)],
        name="moe_offsets",
    )(cnt)


def _scatter_rows(dest_ref, xp_ref, xs_ref, sem):
    t = xp_ref.shape[0]

    def row_copy(tok, dst):
        return pltpu.make_async_copy(xp_ref.at[pl.ds(tok, 1)], xs_ref.at[pl.ds(dst, 1)], sem)

    def issue(grp, carry):
        base = pl.multiple_of(grp * ISSUE_GROUP, ISSUE_GROUP)
        for u in range(ISSUE_GROUP):
            for k in range(TOP_K):
                row_copy(base + u, dest_ref[(base + u) * TOP_K + k]).start(priority=k % 2)
        return carry

    lax.fori_loop(0, t // ISSUE_GROUP, issue, 0)
    for k in range(TOP_K):
        pltpu.make_async_copy(xp_ref, xs_ref.at[pl.ds(0, t)], sem).wait()


def _sc_scatter(rows, idx, n_out):
    t = rows.shape[0]
    n_workers = SC_CORES * SC_SUBCORES
    assert idx.shape[0] == TOP_K * t and t % (n_workers * SC_CHUNK) == 0
    per_worker = t // n_workers
    mesh = plsc.VectorSubcoreMesh(core_axis_name="c", subcore_axis_name="s", num_cores=SC_CORES,
                                  num_subcores=SC_SUBCORES)

    nbuf, chunk = SC_SCATTER_BUFS, SC_CHUNK // SC_SCATTER_BUFS
    assert per_worker % (nbuf * chunk) == 0

    @functools.partial(
        pl.kernel, mesh=mesh, out_type=jax.ShapeDtypeStruct((n_out,) + rows.shape[1:], rows.dtype),
        scratch_types=[pltpu.VMEM((chunk,), I32) for _ in range(nbuf * TOP_K)]
                      + [pltpu.VMEM((chunk,) + rows.shape[1:], rows.dtype) for _ in range(nbuf)]
                      + [pltpu.SemaphoreType.DMA for _ in range(nbuf + 1)])
    def scatter(rows_hbm, idx_hbm, out_hbm, *scratch):
        idx_v = [scratch[u * TOP_K:(u + 1) * TOP_K] for u in range(nbuf)]
        rows_v = scratch[nbuf * TOP_K:nbuf * TOP_K + nbuf]
        lsems, ssem = scratch[nbuf * TOP_K + nbuf:-1], scratch[-1]
        wid = lax.axis_index("s") * SC_CORES + lax.axis_index("c")
        base = wid * per_worker

        @pl.loop(0, per_worker // (nbuf * chunk))
        def _(j):
            loads = []
            for u in range(nbuf):
                off = pl.multiple_of(base + (j * nbuf + u) * chunk, chunk)
                loads.append([pltpu.async_copy(rows_hbm.at[pl.ds(off, chunk)], rows_v[u], lsems[u])]
                             + [pltpu.async_copy(idx_hbm.at[pl.ds(pl.multiple_of(k * t + off, chunk), chunk)],
                                                 idx_v[u][k], lsems[u]) for k in range(TOP_K)])
            scatters = []
            for u in range(nbuf):
                for cp in loads[u]:
                    cp.wait()
                scatters += [pltpu.async_copy(rows_v[u], out_hbm.at[idx_v[u][k]], ssem) for k in range(TOP_K)]
            for cp in scatters:
                cp.wait()

    return scatter(rows, idx)


def _dispatch_rest_kernel(pads_ref, dest2_ref, xp2_ref, xs_in_ref, xs_ref, sem, zsem, zbuf, *, block_rows):
    del xs_in_ref
    zr = zbuf.shape[0]
    n_blocks = xs_ref.shape[0] // block_rows
    zbuf[...] = jnp.zeros_like(zbuf)
    used = pads_ref[2 * N_EXPERTS]

    def pieces(e, act):
        start, n = pads_ref[e], pads_ref[N_EXPERTS + e]
        for sh in range(zr.bit_length() - 1, -1, -1):
            b = 1 << sh
            before = lax.shift_left(lax.shift_right_logical(n, sh + 1), sh + 1)

            @pl.when((n & b) != 0)
            def _():
                act(pltpu.make_async_copy(zbuf.at[pl.ds(0, b)], xs_ref.at[pl.ds(start + before, b)], zsem))

    def tail(jb, act):
        for h in range(block_rows // zr):
            act(pltpu.make_async_copy(zbuf, xs_ref.at[pl.ds(jb * block_rows + h * zr, zr)], zsem))

    for act in (lambda cp: cp.start(), lambda cp: cp.wait()):
        lax.fori_loop(0, N_EXPERTS, lambda e, c: (pieces(e, act), c)[1], 0)
        lax.fori_loop(used, n_blocks, lambda jb, c: (tail(jb, act), c)[1], 0)

    _scatter_rows(dest2_ref, xp2_ref, xs_ref, sem)


def _dispatch_rest(pads, dest2, xp2, xs, block_rows):
    return pl.pallas_call(
        functools.partial(_dispatch_rest_kernel, block_rows=block_rows),
        out_shape=jax.ShapeDtypeStruct(xs.shape, xs.dtype),
        in_specs=[pl.BlockSpec(memory_space=pltpu.SMEM), pl.BlockSpec(memory_space=pltpu.SMEM),
                  pl.BlockSpec(memory_space=pltpu.VMEM), pl.BlockSpec(memory_space=pl.ANY)],
        out_specs=pl.BlockSpec(memory_space=pl.ANY),
        scratch_shapes=[pltpu.SemaphoreType.DMA(()), pltpu.SemaphoreType.DMA(()),
                        pltpu.VMEM((EXPERT_TILE // 2,) + xp2.shape[1:], xp2.dtype)],
        input_output_aliases={3: 0},
        compiler_params=_cparams(), name="moe_dispatch_rest",
    )(pads, dest2, xp2, xs)


def _expert_kernel(be_ref, nu_ref, xs_ref, w1_ref, b1g_ref, b1l_ref, w2_ref, b2_ref, perm_ref, ys_ref,
                   w1g_scr, w1l_scr, w2_scr, xq_scr, y_scr, sem, osem):
    j = pl.program_id(0)
    active = j < nu_ref[0]
    changed = jnp.logical_or(j == 0, be_ref[j] != be_ref[jnp.maximum(j - 1, 0)])
    tm = y_scr.shape[0]
    slot = lax.rem(j, 2)

    def fetch(blk, slot):
        row0 = pl.multiple_of(blk * tm, tm)
        return [pltpu.make_async_copy(xs_ref.at[pl.ds(row0, tm), q, :], xq_scr.at[slot, q], sem.at[slot])
                for q in range(XP_TILE)]

    def put(blk):
        row0 = pl.multiple_of(blk * tm, tm)
        return [pltpu.make_async_copy(y_scr.at[:, q * LANES:(q + 1) * LANES], ys_ref.at[pl.ds(row0, tm), q, :], osem)
                for q in range(YS_TILE)]

    def emit(y):
        @pl.when(j > 0)
        def _():
            for cp in put(j - 1):
                cp.wait()

        y_scr[...] = _pack_halves(y)
        for cp in put(j):
            cp.start()

    @pl.when(j == 0)
    def _():
        for cp in fetch(0, 0):
            cp.start()

    @pl.when(j + 1 < nu_ref[0])
    def _():
        for cp in fetch(j + 1, 1 - slot):
            cp.start()

    @pl.when(jnp.logical_and(active, changed))
    def _():
        for c in range(2 * D_FF // 256):
            wc = w1_ref[0, :, c * 256:(c + 1) * 256].astype(BF16)
            d = jnp.dot(wc, perm_ref[...], preferred_element_type=F32).astype(BF16)
            w1g_scr[:, c * 128:(c + 1) * 128] = d[:, :128]
            w1l_scr[:, c * 128:(c + 1) * 128] = d[:, 128:]
        for c in range(D_FF // 256):
            w2_scr[c * 256:(c + 1) * 256, :] = w2_ref[0, c * 256:(c + 1) * 256, :].astype(BF16)

    @pl.when(active)
    def _():
        for cp in fetch(j, slot):
            cp.wait()
        lo, hi = _unpack_halves(jnp.concatenate([xq_scr[slot, q] for q in range(XP_TILE)], axis=1))
        xb = jnp.concatenate([lo, hi], axis=1)
        hg = jnp.dot(xb, w1g_scr[...], preferred_element_type=F32) + b1g_ref[0]
        hl = jnp.dot(xb, w1l_scr[...], preferred_element_type=F32) + b1l_ref[0]
        x_glu = jnp.minimum(hg, SWIGLU_LIMIT)
        x_lin = jnp.clip(hl, -SWIGLU_LIMIT, SWIGLU_LIMIT)
        a = x_glu * _sigmoid(SWIGLU_ALPHA * x_glu) * (x_lin + 1.0)
        emit(jnp.dot(a.astype(BF16), w2_scr[...], preferred_element_type=F32) + b2_ref[0])

    @pl.when(jnp.logical_not(active))
    def _():
        emit(jnp.zeros((tm, D_MODEL), F32))

    @pl.when(j == pl.num_programs(0) - 1)
    def _():
        for cp in put(j):
            cp.wait()


def _experts(be, nu, xs, w1, b1g, b1l, w2, b2, perm, tile):
    n_blocks = xs.shape[0] // tile
    grid_spec = pltpu.PrefetchScalarGridSpec(
        num_scalar_prefetch=2, grid=(n_blocks,),
        in_specs=[pl.BlockSpec(memory_space=pl.ANY),
                  pl.BlockSpec((1, D_MODEL, 2 * D_FF), lambda j, be, nu: (be[j], 0, 0)),
                  pl.BlockSpec((1, 1, D_FF), lambda j, be, nu: (be[j], 0, 0)),
                  pl.BlockSpec((1, 1, D_FF), lambda j, be, nu: (be[j], 0, 0)),
                  pl.BlockSpec((1, D_FF, D_MODEL), lambda j, be, nu: (be[j], 0, 0)),
                  pl.BlockSpec((1, 1, D_MODEL), lambda j, be, nu: (be[j], 0, 0)),
                  pl.BlockSpec((256, 256), lambda j, be, nu: (0, 0))],
        out_specs=pl.BlockSpec(memory_space=pl.ANY),
        scratch_shapes=[pltpu.VMEM((D_MODEL, D_FF), BF16), pltpu.VMEM((D_MODEL, D_FF), BF16),
                        pltpu.VMEM((D_FF, D_MODEL), BF16), pltpu.VMEM((2, XP_TILE, tile, LANES), U32),
                        pltpu.VMEM((tile, YS_TILE * LANES), U32), pltpu.SemaphoreType.DMA((2,)),
                        pltpu.SemaphoreType.DMA(())])
    return pl.pallas_call(
        _expert_kernel, out_shape=jax.ShapeDtypeStruct((xs.shape[0], YS_TILE, LANES), U32), grid_spec=grid_spec,
        compiler_params=_cparams("arbitrary"), name="moe_experts",
    )(be, nu, xs, w1, b1g, b1l, w2, b2, perm)


def _combine_kernel(dest_ref, next_ref, ys_ref, tk_ref, x1_ref, g2_ref, b2_ref, out_ref, buf, sem):
    i = pl.program_id(0)
    t = x1_ref.shape[0]
    slot = lax.rem(i, 2)

    def gather(idx_ref, s):
        def issue(grp, carry):
            base = pl.multiple_of(grp * 8, 8)
            for u in range(8):
                for k in range(TOP_K):
                    pltpu.make_async_copy(ys_ref.at[idx_ref[(base + u) * TOP_K + k]],
                                          buf.at[s, k, grp, :, u, :], sem.at[s]).start(priority=k % 2)
            return carry

        lax.fori_loop(0, t // 8, issue, 0)

    @pl.when(i == 0)
    def _():
        gather(dest_ref, 0)

    @pl.when(i + 1 < pl.num_programs(0))
    def _():
        gather(next_ref, 1 - slot)

    for k in range(TOP_K):
        for u in range(8):
            pltpu.make_async_copy(ys_ref.at[pl.ds(0, t // 8)], buf.at[slot, k, :, :, u, :], sem.at[slot]).wait()
    tk = tk_ref[...]
    los, his = [], []
    for q in range(YS_TILE):
        lo = hi = None
        for k in range(TOP_K):
            words = buf[slot, k, :, q].reshape(t, LANES)
            g = tk[:, k:k + 1]
            lo_k = g * pltpu.bitcast(words << 16, F32)
            hi_k = g * pltpu.bitcast(words & jnp.uint32(0xFFFF0000), F32)
            lo = lo_k if lo is None else lo + lo_k
            hi = hi_k if hi is None else hi + hi_k
        los.append(lo)
        his.append(hi)
    ff = jnp.concatenate(los + his, axis=1)
    out_ref[...] = _layer_norm(DN_ALPHA * x1_ref[...] + ff, g2_ref[...], b2_ref[...])


def _sc_gather(rows, idx):
    b = idx.shape[0]
    n_workers = SC_CORES * SC_SUBCORES
    assert b % (n_workers * SC_CHUNK) == 0
    per_worker = b // n_workers
    mesh = plsc.VectorSubcoreMesh(core_axis_name="c", subcore_axis_name="s", num_cores=SC_CORES,
                                  num_subcores=SC_SUBCORES)

    nbuf, chunk = SC_GATHER_BUFS, SC_CHUNK // SC_GATHER_BUFS
    assert per_worker % (nbuf * chunk) == 0

    @functools.partial(
        pl.kernel, mesh=mesh, out_type=jax.ShapeDtypeStruct((b,) + rows.shape[1:], rows.dtype),
        scratch_types=[pltpu.VMEM((chunk,), I32) for _ in range(nbuf)]
                      + [pltpu.VMEM((chunk,) + rows.shape[1:], rows.dtype) for _ in range(nbuf)]
                      + [pltpu.SemaphoreType.DMA for _ in range(nbuf + 1)])
    def gather(rows_hbm, idx_hbm, out_hbm, *scratch):
        idx_v, rows_v = scratch[:nbuf], scratch[nbuf:2 * nbuf]
        gsems, wsem = scratch[2 * nbuf:3 * nbuf], scratch[3 * nbuf]
        wid = lax.axis_index("s") * SC_CORES + lax.axis_index("c")
        base = wid * per_worker

        @pl.loop(0, per_worker // (nbuf * chunk))
        def _(j):
            offs = [pl.multiple_of(base + (j * nbuf + u) * chunk, chunk) for u in range(nbuf)]
            for u in range(nbuf):
                pltpu.sync_copy(idx_hbm.at[pl.ds(offs[u], chunk)], idx_v[u])
            gathers = [pltpu.async_copy(rows_hbm.at[idx_v[u]], rows_v[u], gsems[u]) for u in range(nbuf)]
            writes = []
            for u in range(nbuf):
                gathers[u].wait()
                writes.append(pltpu.async_copy(rows_v[u], out_hbm.at[pl.ds(offs[u], chunk)], wsem))
            for w in writes:
                w.wait()

    return gather(rows, idx)


def _combine_stream_kernel(g_ref, tk_ref, x1_ref, g2_ref, b2_ref, out_ref, buf, sem):
    i = pl.program_id(0)
    t = x1_ref.shape[0]
    slot = lax.rem(i, 2)

    def fetch(blk, s):
        row0 = pl.multiple_of(blk * t, t)
        return [pltpu.make_async_copy(g_ref.at[k, pl.ds(row0, t), q, :], buf.at[s, k, q], sem.at[s])
                for k in range(TOP_K) for q in range(YS_TILE)]

    @pl.when(i == 0)
    def _():
        for cp in fetch(0, 0):
            cp.start()

    @pl.when(i + 1 < pl.num_programs(0))
    def _():
        for cp in fetch(i + 1, 1 - slot):
            cp.start()

    for cp in fetch(i, slot):
        cp.wait()
    tk = tk_ref[...]
    los, his = [], []
    for q in range(YS_TILE):
        lo = hi = None
        for k in range(TOP_K):
            words = buf[slot, k, q]
            g = tk[:, k:k + 1]
            lo_k = g * pltpu.bitcast(words << 16, F32)
            hi_k = g * pltpu.bitcast(words & jnp.uint32(0xFFFF0000), F32)
            lo = lo_k if lo is None else lo + lo_k
            hi = hi_k if hi is None else hi + hi_k
        los.append(lo)
        his.append(hi)
    ff = jnp.concatenate(los + his, axis=1)
    out_ref[...] = _layer_norm(DN_ALPHA * x1_ref[...] + ff, g2_ref[...], b2_ref[...])


def _combine_stream(g4, tk, x1, g2, b2):
    t = x1.shape[0]
    tile = min(MERGE_TILE, t)
    rows = lambda w: pl.BlockSpec((tile, w), lambda i: (i, 0))
    return pl.pallas_call(
        _combine_stream_kernel, out_shape=jax.ShapeDtypeStruct((t, D_MODEL), F32), grid=(t // tile,),
        in_specs=[pl.BlockSpec(memory_space=pl.ANY), rows(LANES), rows(D_MODEL),
                  pl.BlockSpec((1, D_MODEL), lambda i: (0, 0)),
                  pl.BlockSpec((1, D_MODEL), lambda i: (0, 0))],
        out_specs=rows(D_MODEL),
        scratch_shapes=[pltpu.VMEM((2, TOP_K, YS_TILE, tile, LANES), U32), pltpu.SemaphoreType.DMA((2,))],
        compiler_params=_cparams("arbitrary"), name="moe_combine_stream",
    )(g4, tk, x1, g2, b2)


def _combine(dest_flat, ys, tk, x1, g2, b2):
    t = x1.shape[0]
    tile = min(ROW_TILE, t)
    n = t // tile
    return pl.pallas_call(
        _combine_kernel, out_shape=jax.ShapeDtypeStruct((t, D_MODEL), F32), grid=(n,),
        in_specs=[pl.BlockSpec((tile * TOP_K,), lambda i: (i,), memory_space=pltpu.SMEM),
                  pl.BlockSpec((tile * TOP_K,), lambda i: (jnp.minimum(i + 1, n - 1),), memory_space=pltpu.SMEM),
                  pl.BlockSpec(memory_space=pl.ANY),
                  pl.BlockSpec((tile, LANES), lambda i: (i, 0)),
                  pl.BlockSpec((tile, D_MODEL), lambda i: (i, 0)),
                  pl.BlockSpec((1, D_MODEL), lambda i: (0, 0)),
                  pl.BlockSpec((1, D_MODEL), lambda i: (0, 0))],
        out_specs=pl.BlockSpec((tile, D_MODEL), lambda i: (i, 0)),
        scratch_shapes=[pltpu.VMEM((2, TOP_K, tile // 8, YS_TILE, 8, LANES), U32), pltpu.SemaphoreType.DMA((2,))],
        compiler_params=_cparams("arbitrary"), name="moe_combine",
    )(dest_flat, dest_flat, ys, tk, x1, g2, b2)


def _rel_bucket(dist):
    exact = REL_BUCKETS // 2
    d = np.maximum(dist, 0)
    log_b = exact + (np.log(np.maximum(d, 1).astype(np.float32) / np.float32(exact))
                     / np.float32(math.log(REL_MAX_DIST / exact)) * np.float32(REL_BUCKETS - exact)).astype(np.int32)
    return np.where(d < exact, d, np.minimum(log_b, REL_BUCKETS - 1)).astype(np.int32)


def _bias_lookup(table, bucket, valid):
    bucket = jnp.asarray(bucket)[None]
    acc = jnp.zeros((table.shape[1],) + bucket.shape[1:], F32)
    for b in range(REL_BUCKETS):
        acc = jnp.where(bucket == b, table[b].reshape((-1,) + (1,) * (bucket.ndim - 1)), acc)
    return jnp.where(jnp.asarray(valid)[None], acc, NEG)


def _bias_tables(rel_bias):
    table = rel_bias.astype(F32)
    r = np.arange(WINDOW)[:, None]
    c = np.arange(2 * WINDOW)[None, :]
    dist = r + WINDOW - c
    valid = (dist >= 0) & (dist < WINDOW)
    dist0 = np.where(c < N_META, N_META + r - c, dist)
    valid0 = np.where(c < N_META, dist0 < WINDOW, (c >= WINDOW) & valid)
    both = jnp.stack([_bias_lookup(table, _rel_bucket(dist0), valid0), _bias_lookup(table, _rel_bucket(dist), valid)])
    dist_s = WINDOW - 1 - np.arange(WINDOW)
    rows = _bias_lookup(table[:, np.asarray(HEAD_ORDER)], _rel_bucket(dist_s), np.ones_like(dist_s, bool))
    return both, rows


def _perm_heads(a, axis):
    assert HEAD_ORDER == tuple(kv * A_GROUP + g for g in range(A_GROUP) for kv in range(A_KV_HEADS))
    shape = a.shape
    a = a.reshape(shape[:axis] + (A_KV_HEADS, A_GROUP, A_HD) + shape[axis + 1:])
    return jnp.swapaxes(a, axis, axis + 1).reshape(shape)


def _rep_rows(vec, rows=8):
    out = jnp.zeros((rows, LANES), F32)
    return out.at[:vec.shape[0], :].set(jnp.broadcast_to(vec.astype(F32)[:, None], (vec.shape[0], LANES)))


def kernel(x_prompt, x_sample, cache_swa_k, cache_swa_v, state_mlstm_C, state_mlstm_n, state_mlstm_m, meta_tokens, rel_bias, w_in, b_igate, b_fgate, attn_sinks, g_mlstm_out, g_attn_out, w_out, ln1_g, ln1_b, w_router, b_router, w_moe1, b_moe1, w_moe2, b_moe2, ln2_g, ln2_b):
    B, S, _ = x_prompt.shape
    NB = x_sample.shape[0]
    assert x_sample.shape[1] == 1 and w_in.shape[0] == 1
    assert S % PROJ_TILE == 0 and S % M_CHUNK == 0 and S % WINDOW == 0 and NB % SAMPLE_GROUP == 0
    l = 0

    assert IN_WIDTHS == (512, 512, 512, 512, 4, 4, 512, 128, 128)
    bf = lambda a: a.astype(BF16)
    w = w_in[l]
    n_main, n_gate = 4 * M_WIDTH, 2 * M_HEADS
    w_gate = w[:, n_main:n_main + n_gate]
    w_att = w[:, n_main + n_gate:]
    assert math.frexp(A_HD ** -0.5)[0] == 0.5
    w_qa = _perm_heads(w_att[:, :A_WIDTH], 1) * (A_HD ** -0.5)
    wr = bf(jnp.concatenate([w[:, :n_main], w_qa, w_att[:, A_WIDTH:],
                             jnp.pad(w_gate, ((0, 0), (0, LANES - n_gate)))], axis=1))
    wt = bf(jnp.concatenate([w[:, :M_WIDTH], w[:, 2 * M_WIDTH:3 * M_WIDTH], w_qa, w_att[:, A_WIDTH + LANES:],
                             w_gate], axis=1).T)
    b_gate = jnp.concatenate([b_igate[l], b_fgate[l]]).astype(F32)
    brow = jnp.pad(b_gate, (0, LANES - n_gate))[None, :]
    bcol = b_gate[:, None]
    plan_p = ((512, 512, "plain", BF16), (1536, 512, "plain", F32), (2560, 128, "plain", BF16),
              (2816, 128, "gate", F32))
    tplan_p = ((0, 512, "plain", BF16), (512, 512, "plain", BF16), (1024, 512, "plain", BF16),
               (1536, 128, "plain", BF16), (1664, 8, "gate", F32))
    plan_s = ((0, 512, "plain", F32), (512, 512, "plain", F32), (1024, 512, "plain", F32), (1536, 512, "plain", F32),
              (2048, 512, "plain", F32), (2560, 128, "plain", F32), (2688, 128, "plain", F32), (2816, 128, "gate", F32))

    bias_tab, bias_rows = _bias_tables(rel_bias)
    sinks = _rep_rows(attn_sinks[l])
    sinks_step = _rep_rows(attn_sinks[l][np.asarray(HEAD_ORDER)])
    g_m = g_mlstm_out[l].astype(F32)[None, :]
    g_a = _perm_heads(g_attn_out[l].astype(F32), 0)[None, :]
    wo = bf(jnp.concatenate([w_out[l][:M_WIDTH], _perm_heads(w_out[l][M_WIDTH:], 0)], axis=0))
    g1, b1 = ln1_g[l].astype(F32)[None, :], ln1_b[l].astype(F32)[None, :]
    g2, b2 = ln2_g[l].astype(F32)[None, :], ln2_b[l].astype(F32)[None, :]
    w_r = bf(jnp.pad(w_router[l], ((0, 0), (0, LANES - N_EXPERTS))))
    b_r = jnp.pad(b_router[l].astype(F32), (0, LANES - N_EXPERTS), constant_values=NEG)[None, :]
    b1g = b_moe1[l][:, 0::2].astype(F32)[:, None, :]
    b1l = b_moe1[l][:, 1::2].astype(F32)[:, None, :]
    b2e = b_moe2[l].astype(F32)[:, None, :]
    pj = np.zeros((256, 256), np.float32)
    pj[2 * np.arange(128), np.arange(128)] = 1.0
    pj[2 * np.arange(128) + 1, 128 + np.arange(128)] = 1.0
    perm = jnp.asarray(pj, BF16)

    xp2 = x_prompt.reshape(B * S, D_MODEL)
    km, om, ka, gc, qt, vt, qat, vat, gr, kv_tail = _proj(
        xp2, wr, wt, brow, bcol, plan_p, tplan_p, (2560, 256), PROJ_TILE, S, "proj_prompt")
    x_meta = jnp.pad(meta_tokens.astype(F32), ((0, M_CHUNK - N_META), (0, 0)))
    km0, _, ka0, gc0, qt0, vt0, _, vat0, gr0 = _proj(
        x_meta, wr, wt, brow, bcol, plan_p, tplan_p, None, M_CHUNK, M_CHUNK, "proj_meta")
    xs2 = x_sample.reshape(NB, D_MODEL)
    qm_s, km_s, vm_s, om_s, qa_s, ka_s, va_s, gc_s = _proj(
        xs2, wr, wt, brow, bcol, plan_s, (), None, NB, NB, "proj_sample")

    zero_c = jnp.zeros((M_HEADS, M_DV + 8, M_DK), F32)
    zero_m = jnp.zeros((8, LANES), F32)
    _, c_meta, m_meta = _mlstm(qt0, km0, vt0, gc0, gr0, zero_c, zero_m, 1, N_META, "mlstm_meta")
    h_p, c_p, m_p = _mlstm(qt, km, vt, gc, gr, c_meta[0], m_meta[0], B, M_CHUNK, "mlstm_prompt")
    C_p = c_p[:, :, :M_DV, :]
    n_p = c_p[:, :, M_DV, :]
    m_prompt = m_p[:, :M_HEADS, 0]
    m_pad = jnp.pad(state_mlstm_m[l].astype(F32), ((0, 0), (0, LANES - M_HEADS)))
    C_s, n_s, m_s, h_s = _mlstm_step(state_mlstm_C[l].astype(F32), state_mlstm_n[l].astype(F32), m_pad,
                                     gc_s, qm_s, km_s, vm_s)

    att_p = _swa(qat, ka, vat, ka0, vat0, jnp.swapaxes(bias_tab, 2, 3), sinks, B)
    ck = cache_swa_k[l].reshape(NB, WINDOW, LANES)
    cv = cache_swa_v[l].reshape(NB, WINDOW, LANES)
    k_new, v_new, att_s = _swa_step(ck, cv, qa_s, ka_s, va_s, bias_rows, sinks_step)

    x1_p, xpk_p, tk_p, cnt_p = _merge(h_p, om, att_p, xp2, g_m, g_a, wo, g1, b1, w_r, b_r, MERGE_TILE, "merge_prompt")
    x1_s, xpk_s, tk_s, cnt_s = _merge(h_s, om_s, att_s, xs2, g_m, g_a, wo, g1, b1, w_r, b_r, NB, "merge_sample")

    T_p = B * S
    assert T_p % RANK_TILE == 0 and T_p % ROW_TILE == 0
    n_blocks = -(-((T_p + NB) * TOP_K) // EXPERT_TILE) + N_EXPERTS
    off, be2, nu2, pad = _offsets(cnt_p + cnt_s, n_blocks, EXPERT_TILE)
    pads = jnp.concatenate([pad[0, :N_EXPERTS], pad[1, :N_EXPERTS], nu2[0, :1]])
    dest_p = _route(tk_p, off)[:, :TOP_K].reshape(-1)
    dest_s = _route(tk_s, off + cnt_p)[:, :TOP_K].reshape(-1)
    be = be2.reshape(-1)[:n_blocks]
    nu = nu2[0, :1]
    idx_p = dest_p.reshape(T_p, TOP_K).T.reshape(-1)
    xs = _sc_scatter(xpk_p, idx_p, n_blocks * EXPERT_TILE)
    xs = _dispatch_rest(pads, dest_s, xpk_s, xs, EXPERT_TILE)
    ys = _experts(be, nu, xs, w_moe1[l], b1g, b1l, w_moe2[l], b2e, perm, EXPERT_TILE)
    g4 = _sc_gather(ys, idx_p).reshape(TOP_K, T_p, YS_TILE, LANES)
    y_p = _combine_stream(g4, tk_p, x1_p, g2, b2)
    y_s = _combine(dest_s, ys, tk_s, x1_s, g2, b2)

    kv_tail = kv_tail.reshape(B, WINDOW, 2, A_KV_HEADS, A_HD)
    dt_k, dt_v = cache_swa_k.dtype, cache_swa_v.dtype
    return (y_p.reshape(B, S, D_MODEL).astype(x_prompt.dtype), y_s.reshape(NB, 1, D_MODEL).astype(x_sample.dtype),
            kv_tail[:, :, 0][None].astype(dt_k), kv_tail[:, :, 1][None].astype(dt_v),
            C_p[None].astype(state_mlstm_C.dtype), n_p[None].astype(state_mlstm_n.dtype),
            m_prompt[None].astype(state_mlstm_m.dtype),
            k_new.reshape(1, NB, WINDOW, A_KV_HEADS, A_HD).astype(dt_k),
            v_new.reshape(1, NB, WINDOW, A_KV_HEADS, A_HD).astype(dt_v),
            C_s[None].astype(state_mlstm_C.dtype), n_s[None].astype(state_mlstm_n.dtype),
            m_s[:, :M_HEADS][None].astype(state_mlstm_m.dtype))
```

```python
import functools
import math

import numpy as np
import jax
import jax.numpy as jnp
from jax import lax
from jax.experimental import pallas as pl
from jax.experimental.pallas import tpu as pltpu
from jax.experimental.pallas import tpu_sc as plsc

F32 = jnp.float32
BF16 = jnp.bfloat16
I32 = jnp.int32
U32 = jnp.uint32

D_MODEL = 1024
N_META = 16
M_HEADS = 4
M_DK = 128
M_DV = 128
M_WIDTH = M_HEADS * M_DV
A_HD = 64
A_HEADS = 8
A_KV_HEADS = 2
A_GROUP = A_HEADS // A_KV_HEADS
A_WIDTH = A_HEADS * A_HD
WINDOW = 128
REL_BUCKETS = 32
REL_MAX_DIST = 128
N_EXPERTS = 32
TOP_K = 4
D_FF = D_MODEL
SWIGLU_LIMIT = 7.0
SWIGLU_ALPHA = 1.702
DEPTH = 1
DN_ALPHA = (2.0 * DEPTH) ** 0.25
LN_EPS = 1e-5
IN_WIDTHS = (M_WIDTH, M_WIDTH, M_WIDTH, M_WIDTH, M_HEADS, M_HEADS, A_WIDTH, A_KV_HEADS * A_HD, A_KV_HEADS * A_HD)

LANES = 128
NEG = -1e30
VMEM_LIMIT = 56 * 1024 * 1024

M_CHUNK = 256
PROJ_TILE = 1024
MERGE_TILE = 1024
RANK_TILE = 1024
ROW_TILE = 512
EXPERT_TILE = 512
SAMPLE_GROUP = 8
ISSUE_GROUP = 8
SWA_QBLOCKS = 8
SC_CORES, SC_SUBCORES = 2, 16
SC_CHUNK = 128
SC_SCATTER_BUFS = 2
MLSTM_SEQS = 4
STEP_PIPE_LAG = 3
SWA_PIPE_LAG = 2
XP_TILE = D_MODEL // 2 // LANES
YS_TILE = D_MODEL // 2 // LANES
HEAD_ORDER = (0, 4, 1, 5, 2, 6, 3, 7)


def _cparams(*sem):
    return pltpu.CompilerParams(dimension_semantics=sem, vmem_limit_bytes=VMEM_LIMIT)


def _log_sigmoid(x):
    return jnp.minimum(x, 0.0) - jnp.log1p(jnp.exp(-jnp.abs(x)))


def _sigmoid(x):
    return 1.0 / (1.0 + jnp.exp(-x))


def _proj_kernel(x_ref, wr_ref, wt_ref, brow_ref, bcol_ref, *outs, row_plan, t_plan, tail_cols):
    xb = x_ref[...].astype(BF16)
    tm = xb.shape[0]
    o = 0
    for (c0, width, kind, _) in row_plan:
        r = jnp.dot(xb, wr_ref[:, c0:c0 + width], preferred_element_type=F32)
        if kind == "gate":
            r = r + brow_ref[...]
            lane = lax.broadcasted_iota(I32, r.shape, 1)
            r = jnp.where(lane < M_HEADS, r, _log_sigmoid(r))
        outs[o][...] = r.astype(outs[o].dtype)
        o += 1
    for (r0, nrows, kind, _) in t_plan:
        r = lax.dot_general(wt_ref[r0:r0 + nrows, :], xb, (((1,), (1,)), ((), ())), preferred_element_type=F32)
        if kind == "gate":
            r = r + bcol_ref[...]
            row = lax.broadcasted_iota(I32, r.shape, 0)
            r = jnp.where(row < M_HEADS, r, _log_sigmoid(r))
        outs[o][...] = r.astype(outs[o].dtype)
        o += 1
    if tail_cols is not None:
        c0, width = tail_cols
        outs[o][...] = jnp.dot(xb[tm - WINDOW:, :], wr_ref[:, c0:c0 + width], preferred_element_type=F32)


def _proj(x, wr, wt, brow, bcol, row_plan, t_plan, tail_cols, tile, rows_per_group, name):
    t = x.shape[0]
    nt = t // tile
    out_shape, out_specs = [], []
    for (_, width, _, dt) in row_plan:
        out_shape.append(jax.ShapeDtypeStruct((t, width), dt))
        out_specs.append(pl.BlockSpec((tile, width), lambda i: (i, 0)))
    for (_, nrows, _, dt) in t_plan:
        out_shape.append(jax.ShapeDtypeStruct((nrows, t), dt))
        out_specs.append(pl.BlockSpec((nrows, tile), lambda i: (0, i)))
    if tail_cols is not None:
        tiles_per_group = rows_per_group // tile
        out_shape.append(jax.ShapeDtypeStruct((t // rows_per_group * WINDOW, tail_cols[1]), F32))
        out_specs.append(pl.BlockSpec((WINDOW, tail_cols[1]), lambda i: (i // tiles_per_group, 0)))
    kern = functools.partial(_proj_kernel, row_plan=row_plan, t_plan=t_plan, tail_cols=tail_cols)
    return pl.pallas_call(
        kern, out_shape=out_shape, grid=(nt,),
        in_specs=[pl.BlockSpec((tile, D_MODEL), lambda i: (i, 0)),
                  pl.BlockSpec(wr.shape, lambda i: (0, 0)),
                  pl.BlockSpec(wt.shape, lambda i: (0, 0)),
                  pl.BlockSpec(brow.shape, lambda i: (0, 0)),
                  pl.BlockSpec(bcol.shape, lambda i: (0, 0))],
        out_specs=out_specs, compiler_params=_cparams("arbitrary"), name=name,
    )(x, wr, wt, brow, bcol)


def _split3(a):
    hi = a.astype(BF16)
    r1 = a - hi.astype(F32)
    mid = r1.astype(BF16)
    lo = (r1 - mid.astype(F32)).astype(BF16)
    return hi, mid, lo


def _mlstm_kernel(*refs, n_valid, nseq):
    seq_in = [refs[5 * i:5 * i + 5] for i in range(nseq)]
    c0_ref, m0_ref = refs[5 * nseq:5 * nseq + 2]
    h_ref, c_out_ref, m_out_ref, c_scr, m_scr = refs[5 * nseq + 2:]
    c = pl.program_id(1)
    nc = pl.num_programs(1)
    L = seq_in[0][1].shape[0]

    @pl.when(c == 0)
    def _():
        for i in range(nseq):
            c_scr[i] = c0_ref[...]
            m_scr[i] = m0_ref[...]

    r_i = lax.broadcasted_iota(I32, (L, L), 0)
    c_i = lax.broadcasted_iota(I32, (L, L), 1)
    upper = r_i <= c_i
    tril = jnp.where(c_i <= r_i, 1.0, 0.0).astype(BF16)
    triu = jnp.where(upper, 1.0, 0.0).astype(BF16)
    scale = M_DK ** -0.5
    ones_rows = jnp.where(lax.broadcasted_iota(I32, (8, L), 0) == 0, 1.0, 0.0).astype(BF16)

    gates = []
    for (_, _, _, gc_ref, gr_ref) in seq_in:
        gc = gc_ref[...]
        gr = gr_ref[...]
        if n_valid < L:
            rowc = lax.broadcasted_iota(I32, gc.shape, 0)
            lanec = lax.broadcasted_iota(I32, gc.shape, 1)
            gc = jnp.where(rowc < n_valid, gc, jnp.where(lanec < M_HEADS, NEG, 0.0))
            rowr = lax.broadcasted_iota(I32, gr.shape, 0)
            colr = lax.broadcasted_iota(I32, gr.shape, 1)
            gr = jnp.where(colr < n_valid, gr, jnp.where(rowr < M_HEADS, NEG, 0.0))
        b_cols = sum(jnp.dot(tril, part, preferred_element_type=F32) for part in _split3(gc))
        b_rows = sum(jnp.dot(part, triu, preferred_element_type=F32) for part in _split3(gr))
        gates.append((gc, gr, b_cols, b_rows))

    m_alls = [m_scr[i] for i in range(nseq)]
    c_alls = [[c_scr[i, h] for h in range(M_HEADS)] for i in range(nseq)]
    h_new, c_new, m_new_all = {}, {}, {}

    def operands(u):
        i, h = u
        qt_ref, k_ref, vt_ref = seq_in[i][:3]
        sl = slice(h * M_DK, (h + 1) * M_DK)
        return qt_ref[sl, :], k_ref[:, sl], jnp.concatenate([vt_ref[sl, :], ones_rows], axis=0)

    def stage_a(u):
        i, h = u
        gc, gr, b_cols, b_rows = gates[i]
        qt, k, vt_aug = operands(u)
        ig_r = gr[h:h + 1, :]
        b_r = b_rows[M_HEADS + h:M_HEADS + h + 1, :]
        m_prev = m_alls[i][h:h + 1, 0:1]
        cs = c_alls[i][h]
        qk = jnp.dot(k, qt, preferred_element_type=F32)
        inter = jnp.dot(cs.astype(BF16), qt, preferred_element_type=F32)
        b_last = b_r[:, L - 1:L]
        g = ig_r + b_last - b_r
        m_new = jnp.maximum(b_last + m_prev, jnp.max(g, axis=1, keepdims=True))
        a = jnp.exp(b_last + m_prev - m_new)
        wv = (vt_aug.astype(F32) * jnp.exp(g - m_new)).astype(BF16)
        c_new[u] = a * cs + jnp.dot(wv, k, preferred_element_type=F32) * scale
        m_new_all[u] = jnp.broadcast_to(m_new, (1, LANES))
        return qk, inter

    def stage_b(u, qk, inter):
        i, h = u
        gc, gr, b_cols, b_rows = gates[i]
        b_r = b_rows[M_HEADS + h:M_HEADS + h + 1, :]
        m_prev = m_alls[i][h:h + 1, 0:1]
        r_c = gc[:, h:h + 1] - b_cols[:, M_HEADS + h:M_HEADS + h + 1]
        dt = jnp.where(upper, b_r + r_c, NEG)
        m_t = jnp.maximum(b_r + m_prev, jnp.max(dt, axis=0, keepdims=True))
        st = (qk * (scale * jnp.exp(dt - m_t))).astype(BF16)
        return st, jnp.exp(b_r + m_prev - m_t) * inter, jnp.exp(-m_t)

    def stage_c(u, st, inter_w, floor):
        _, _, vt_aug = operands(u)
        nd = inter_w + jnp.dot(vt_aug, st, preferred_element_type=F32)
        den = nd[M_DV:M_DV + 1, :]
        h_new[u] = (nd[:M_DV, :] / jnp.maximum(jnp.abs(den), floor)).T

    units = [(i, h) for h in range(M_HEADS) for i in range(nseq)]
    a_q, b_q = {}, {}
    for n in range(len(units) + 2):
        if n < len(units):
            a_q[n] = stage_a(units[n])
        if 0 <= n - 1 < len(units):
            b_q[n - 1] = stage_b(units[n - 1], *a_q.pop(n - 1))
        if 0 <= n - 2 < len(units):
            stage_c(units[n - 2], *b_q.pop(n - 2))

    for i in range(nseq):
        h_ref[0, i] = jnp.concatenate([h_new[(i, h)] for h in range(M_HEADS)], axis=1)
        for h in range(M_HEADS):
            c_scr[i, h] = c_new[(i, h)]
        m_scr[i, 0:M_HEADS, :] = jnp.concatenate([m_new_all[(i, h)] for h in range(M_HEADS)], axis=0)

    @pl.when(c == nc - 1)
    def _():
        c_out_ref[...] = c_scr[...]
        m_out_ref[...] = m_scr[...]


def _mlstm(qt, km, vt, gc, gr, c0, m0, batch, n_valid, name):
    L = M_CHUNK
    nc = km.shape[0] // (batch * L)
    nseq = MLSTM_SEQS if batch % MLSTM_SEQS == 0 else 1
    kern = functools.partial(_mlstm_kernel, n_valid=n_valid, nseq=nseq)
    in_specs, operands = [], []
    for i in range(nseq):
        blk = functools.partial(lambda b, c, i: (b * nseq + i) * nc + c, i=i)
        rows = pl.BlockSpec((L, M_WIDTH), functools.partial(lambda b, c, blk: (blk(b, c), 0), blk=blk))
        cols = pl.BlockSpec((M_WIDTH, L), functools.partial(lambda b, c, blk: (0, blk(b, c)), blk=blk))
        in_specs += [cols, rows, cols,
                     pl.BlockSpec((L, LANES), functools.partial(lambda b, c, blk: (blk(b, c), 0), blk=blk)),
                     pl.BlockSpec((8, L), functools.partial(lambda b, c, blk: (0, blk(b, c)), blk=blk))]
        operands += [qt, km, vt, gc, gr]
    in_specs += [pl.BlockSpec((M_HEADS, M_DV + 8, M_DK), lambda b, c: (0, 0, 0)),
                 pl.BlockSpec((8, LANES), lambda b, c: (0, 0))]
    h4, c_fin, m_fin = pl.pallas_call(
        kern,
        out_shape=[jax.ShapeDtypeStruct((batch // nseq, nseq, nc * L, M_WIDTH), F32),
                   jax.ShapeDtypeStruct((batch, M_HEADS, M_DV + 8, M_DK), F32),
                   jax.ShapeDtypeStruct((batch, 8, LANES), F32)],
        grid=(batch // nseq, nc),
        in_specs=in_specs,
        out_specs=[pl.BlockSpec((1, nseq, L, M_WIDTH), lambda b, c: (b, 0, c, 0)),
                   pl.BlockSpec((nseq, M_HEADS, M_DV + 8, M_DK), lambda b, c: (b, 0, 0, 0)),
                   pl.BlockSpec((nseq, 8, LANES), lambda b, c: (b, 0, 0))],
        scratch_shapes=[pltpu.VMEM((nseq, M_HEADS, M_DV + 8, M_DK), F32), pltpu.VMEM((nseq, 8, LANES), F32)],
        compiler_params=_cparams("arbitrary", "arbitrary"), name=name,
    )(*operands, c0, m0)
    return h4.reshape(batch * nc * L, M_WIDTH), c_fin, m_fin


def _outer_f32(a, b):
    ah, am, al = (t.astype(F32) for t in _split3(a))
    bh, bm, bl = (t.astype(F32) for t in _split3(b))
    z = jnp.zeros_like(ah)
    lhs = jnp.concatenate([ah, ah, ah, am, am, al, z, z], axis=0).astype(BF16)
    rhs = jnp.concatenate([bh, bm, bl, bh, bm, bh, z, z], axis=0).astype(BF16)
    return lax.dot_general(lhs, rhs, (((0,), (0,)), ((), ())), preferred_element_type=F32)


def _mlstm_step_kernel(c_ref, n_ref, m_ref, gc_ref, q_ref, k_ref, v_ref,
                       c_out_ref, n_out_ref, m_out_ref, h_ref):
    g = c_ref.shape[0]
    assert g == 8
    scale = M_DK ** -0.5
    ig = gc_ref[:, 0:M_HEADS]
    lf = gc_ref[:, M_HEADS:2 * M_HEADS]
    m = m_ref[:, 0:M_HEADS]
    m_t = jnp.maximum(lf + m, ig)
    w = jnp.exp(lf + m - m_t)
    wg = jnp.exp(ig - m_t)
    floor = jnp.exp(-m_t)
    m_out_ref[...] = jnp.zeros_like(m_out_ref)
    m_out_ref[:, 0:M_HEADS] = m_t
    row8 = lax.broadcasted_iota(I32, (g, M_DV), 0)

    per_head = []
    for h in range(M_HEADS):
        sl = slice(h * M_DK, (h + 1) * M_DK)
        q = q_ref[:, sl]
        k = k_ref[:, sl] * scale
        v = v_ref[:, sl]
        n = n_ref[:, h, :]
        w_h, wg_h = w[:, h:h + 1], wg[:, h:h + 1]
        s = jnp.sum(q * k, axis=1, keepdims=True) * wg_h
        den = w_h * jnp.sum(n * q, axis=1, keepdims=True) + s
        n_out_ref[:, h, :] = w_h * n + wg_h * k
        per_head.append((q.astype(BF16), k, wg_h * v, w_h, s * v, 1.0 / jnp.maximum(jnp.abs(den), floor[:, h:h + 1])))

    def stage_a(h, j):
        qb = per_head[h][0]
        r = lax.dot_general(qb, c_ref[j, h].astype(BF16), (((1,), (1,)), ((), ())), preferred_element_type=F32)
        return jnp.where(row8 == j, r, 0.0)

    def stage_b(h, j):
        _, k, wv, w_h, _, _ = per_head[h]
        c_out_ref[j, h] = w_h[j:j + 1, :] * c_ref[j, h] + _outer_f32(wv[j:j + 1, :], k[j:j + 1, :])

    units = [(h, j) for h in range(M_HEADS) for j in range(g)]
    lag = STEP_PIPE_LAG
    cq = [jnp.zeros((g, M_DV), F32) for _ in range(M_HEADS)]
    for i in range(len(units) + lag):
        if i < len(units):
            cq[units[i][0]] = cq[units[i][0]] + stage_a(*units[i])
        if 0 <= i - lag < len(units):
            stage_b(*units[i - lag])
    for h in range(M_HEADS):
        _, _, _, w_h, sv, inv = per_head[h]
        h_ref[:, h * M_DK:(h + 1) * M_DK] = (w_h * cq[h] + sv) * inv


def _mlstm_step(c, n, m_pad, gc, q, k, v):
    nb = c.shape[0]
    g = SAMPLE_GROUP
    row = lambda w: pl.BlockSpec((g, w), lambda i: (i, 0))
    return pl.pallas_call(
        _mlstm_step_kernel,
        out_shape=[jax.ShapeDtypeStruct(c.shape, F32), jax.ShapeDtypeStruct(n.shape, F32),
                   jax.ShapeDtypeStruct((nb, LANES), F32), jax.ShapeDtypeStruct((nb, M_WIDTH), F32)],
        grid=(nb // g,),
        in_specs=[pl.BlockSpec((g, M_HEADS, M_DV, M_DK), lambda i: (i, 0, 0, 0)),
                  pl.BlockSpec((g, M_HEADS, M_DK), lambda i: (i, 0, 0)),
                  row(LANES), row(LANES), row(M_WIDTH), row(M_WIDTH), row(M_WIDTH)],
        out_specs=[pl.BlockSpec((g, M_HEADS, M_DV, M_DK), lambda i: (i, 0, 0, 0)),
                   pl.BlockSpec((g, M_HEADS, M_DK), lambda i: (i, 0, 0)),
                   row(LANES), row(M_WIDTH)],
        compiler_params=_cparams("arbitrary"), name="mlstm_step",
    )(c, n, m_pad, gc, q, k, v)


def _swa_kernel(qt_ref, kc_ref, kp_ref, vtc_ref, vtp_ref, km_ref, vtm_ref, bias_ref, sink_ref, o_ref):
    j = pl.program_id(1)
    first = j == 0
    blk = WINDOW
    nqb = qt_ref.shape[1] // blk
    kp = jnp.where(first, km_ref[...], kp_ref[...])
    vtp = jnp.where(first, vtm_ref[...], vtp_ref[...])
    k = jnp.concatenate([kp, kc_ref[...]], axis=0)
    vt = jnp.concatenate([vtp, vtc_ref[...]], axis=1)
    row_v = lax.broadcasted_iota(I32, vt.shape, 0)
    zero_v = jnp.zeros_like(vt)
    vt_half = (jnp.where(row_v < A_HD, vt, zero_v), jnp.where(row_v >= A_HD, vt, zero_v))
    row_q = lax.broadcasted_iota(I32, (LANES, blk), 0)
    lo_rows = row_q < A_HD
    def scores(u, p):
        cols = slice(u * blk, (u + 1) * blk)
        keys = slice(u * blk, (u + 2) * blk)
        qs = qt_ref[p * LANES:(p + 1) * LANES, cols]
        zero_q = jnp.zeros_like(qs)
        q_own = (jnp.where(lo_rows, qs, zero_q), jnp.where(lo_rows, zero_q, qs))
        return [jnp.dot(k[keys], q_own[half], preferred_element_type=F32) for half in range(2)]

    def softmax(u, p, s2):
        table = jnp.where(first, 0, 1) if u == 0 else 1
        probs, inv = [], []
        for half in range(2):
            hd = HEAD_ORDER[2 * p + half]
            s = s2[half] + bias_ref[table, hd]
            sk = sink_ref[hd:hd + 1, 0:1]
            m = jnp.maximum(jnp.max(s, axis=0, keepdims=True), sk)
            e = jnp.exp(s - m)
            probs.append(e.astype(BF16))
            inv.append(1.0 / (jnp.sum(e, axis=0, keepdims=True) + jnp.exp(sk - m)))
        return jnp.concatenate(probs, axis=0), jnp.where(lo_rows, inv[0], inv[1])

    def values(u, p, probs, inv):
        keys = slice(u * blk, (u + 2) * blk)
        vt_stack = jnp.concatenate([vt_half[0][:, keys], vt_half[1][:, keys]], axis=1)
        ot = jnp.dot(vt_stack, probs, preferred_element_type=F32)
        o_ref[u * blk:(u + 1) * blk, p * LANES:(p + 1) * LANES] = (ot * inv).T

    units = [(u, p) for u in range(nqb) for p in range(A_GROUP)]
    s_q, p_q = {}, {}
    lag = SWA_PIPE_LAG
    for i in range(len(units) + 2 * lag):
        if i < len(units):
            s_q[i] = scores(*units[i])
        if 0 <= i - lag < len(units):
            p_q[i - lag] = softmax(*units[i - lag], s_q.pop(i - lag))
        if 0 <= i - 2 * lag < len(units):
            values(*units[i - 2 * lag], *p_q.pop(i - 2 * lag))


def _swa(qat, ka, vat, kmeta, vtmeta, bias_t, sinks, batch):
    blk = WINDOW
    nqb = SWA_QBLOCKS
    t = ka.shape[0]
    nq = t // (batch * blk * nqb)
    prev = lambda b, j: (b * nq + j) * nqb + jnp.where(j == 0, 0, -1)
    const2 = lambda shape: pl.BlockSpec(shape, lambda b, j: (0, 0))
    return pl.pallas_call(
        _swa_kernel, out_shape=jax.ShapeDtypeStruct((t, A_WIDTH), F32), grid=(batch, nq),
        in_specs=[pl.BlockSpec((A_WIDTH, nqb * blk), lambda b, j: (0, b * nq + j)),
                  pl.BlockSpec((nqb * blk, LANES), lambda b, j: (b * nq + j, 0)),
                  pl.BlockSpec((blk, LANES), lambda b, j: (prev(b, j), 0)),
                  pl.BlockSpec((LANES, nqb * blk), lambda b, j: (0, b * nq + j)),
                  pl.BlockSpec((LANES, blk), lambda b, j: (0, prev(b, j))),
                  const2((blk, LANES)), const2((LANES, blk)),
                  pl.BlockSpec(bias_t.shape, lambda b, j: (0, 0, 0, 0)),
                  const2((8, LANES))],
        out_specs=pl.BlockSpec((nqb * blk, A_WIDTH), lambda b, j: (b * nq + j, 0)),
        compiler_params=_cparams("arbitrary", "arbitrary"), name="swa_prompt",
    )(qat, ka, ka, vat, vat, kmeta, vtmeta, bias_t, sinks)


def _swa_step_kernel(ck_ref, cv_ref, q_ref, k_ref, v_ref, bias_ref, sink_ref, ko_ref, vo_ref, o_ref):
    g = ck_ref.shape[0]
    lane = lax.broadcasted_iota(I32, (A_HEADS, LANES), 1)
    row = lax.broadcasted_iota(I32, (A_HEADS, LANES), 0)
    own_half = (row % 2 == 0) == (lane < A_HD)
    bias = bias_ref[...]
    sk = sink_ref[:, 0:1]
    def stage_a(j):
        ko_ref[j, 0:WINDOW - 1, :] = ck_ref[j, 1:WINDOW, :]
        ko_ref[j, WINDOW - 1:WINDOW, :] = k_ref[j:j + 1, :]
        vo_ref[j, 0:WINDOW - 1, :] = cv_ref[j, 1:WINDOW, :]
        vo_ref[j, WINDOW - 1:WINDOW, :] = v_ref[j:j + 1, :]
        kk = ko_ref[j].astype(BF16)
        slabs = [q_ref[j:j + 1, p * LANES:(p + 1) * LANES] for p in range(A_GROUP)]
        q8 = jnp.concatenate([slabs[r // 2] for r in range(A_HEADS)], axis=0)
        q8 = jnp.where(own_half, q8, 0.0).astype(BF16)
        s = lax.dot_general(q8, kk, (((1,), (1,)), ((), ())), preferred_element_type=F32)
        s = s + bias
        m = jnp.maximum(jnp.max(s, axis=1, keepdims=True), sk)
        e = jnp.exp(s - m)
        return e.astype(BF16), 1.0 / (jnp.sum(e, axis=1, keepdims=True) + jnp.exp(sk - m))

    def stage_b(j, p8, inv):
        vv = vo_ref[j].astype(BF16)
        o8 = jnp.where(own_half, jnp.dot(p8, vv, preferred_element_type=F32) * inv, 0.0)
        for p in range(A_GROUP):
            o_ref[j:j + 1, p * LANES:(p + 1) * LANES] = o8[2 * p:2 * p + 1, :] + o8[2 * p + 1:2 * p + 2, :]

    lag, pending = STEP_PIPE_LAG, {}
    for i in range(g + lag):
        if i < g:
            pending[i] = stage_a(i)
        if 0 <= i - lag < g:
            stage_b(i - lag, *pending.pop(i - lag))


def _swa_step(ck, cv, q, k, v, bias_rows, sinks):
    nb = ck.shape[0]
    g = SAMPLE_GROUP
    cache = pl.BlockSpec((g, WINDOW, LANES), lambda i: (i, 0, 0))
    row = lambda w: pl.BlockSpec((g, w), lambda i: (i, 0))
    const = lambda a: pl.BlockSpec(a.shape, lambda i: (0, 0))
    return pl.pallas_call(
        _swa_step_kernel,
        out_shape=[jax.ShapeDtypeStruct(ck.shape, F32), jax.ShapeDtypeStruct(cv.shape, F32),
                   jax.ShapeDtypeStruct((nb, A_WIDTH), F32)],
        grid=(nb // g,),
        in_specs=[cache, cache, row(A_WIDTH), row(LANES), row(LANES), const(bias_rows), const(sinks)],
        out_specs=[cache, cache, row(A_WIDTH)],
        compiler_params=_cparams("arbitrary"), name="swa_step",
    )(ck, cv, q, k, v, bias_rows, sinks)


def _layer_norm(z, g, b):
    mu = jnp.mean(z, axis=1, keepdims=True)
    zc = z - mu
    var = jnp.mean(zc * zc, axis=1, keepdims=True)
    return zc * lax.rsqrt(var + LN_EPS) * g + b


def _pack_halves(x):
    w = x.shape[1] // 2
    lo = pltpu.bitcast(x[:, :w].astype(BF16).astype(F32), U32)
    hi = pltpu.bitcast(x[:, w:].astype(BF16).astype(F32), U32)
    return (lo >> 16) | (hi & jnp.uint32(0xFFFF0000))


def _unpack_halves(words):
    lo = pltpu.bitcast(words << 16, F32).astype(BF16)
    hi = pltpu.bitcast(words & jnp.uint32(0xFFFF0000), F32).astype(BF16)
    return lo, hi


def _to_token_tiles(ref, x):
    for q in range(x.shape[1] // LANES):
        ref[:, q, :] = x[:, q * LANES:(q + 1) * LANES]


def _merge_kernel(h_ref, om_ref, att_ref, x_ref, gm_ref, ga_ref, wo_ref, g1_ref, b1_ref, wr_ref, br_ref,
                  x1_ref, xp_ref, tk_ref, cnt_ref):
    @pl.when(pl.program_id(0) == 0)
    def _():
        cnt_ref[...] = jnp.zeros_like(cnt_ref)

    hm = h_ref[...] * _sigmoid(om_ref[...])
    ym = hm * lax.rsqrt(jnp.mean(hm * hm, axis=1, keepdims=True) + LN_EPS) * gm_ref[...]
    att = att_ref[...]
    ya = att * lax.rsqrt(jnp.mean(att * att, axis=1, keepdims=True) + LN_EPS) * ga_ref[...]
    mix = (jnp.dot(ym.astype(BF16), wo_ref[0:M_WIDTH, :], preferred_element_type=F32)
           + jnp.dot(ya.astype(BF16), wo_ref[M_WIDTH:, :], preferred_element_type=F32))
    x1 = _layer_norm(DN_ALPHA * x_ref[...] + mix, g1_ref[...], b1_ref[...])
    x1_ref[...] = x1
    _to_token_tiles(xp_ref, _pack_halves(x1))
    logits = jnp.dot(x1.astype(BF16), wr_ref[...], preferred_element_type=F32) + br_ref[...]
    lane = lax.broadcasted_iota(I32, logits.shape, 1).astype(F32)
    vals, idxs = [], []
    for _ in range(TOP_K):
        mx = jnp.max(logits, axis=1, keepdims=True)
        idx = jnp.min(jnp.where(logits == mx, lane, float(LANES)), axis=1, keepdims=True)
        vals.append(mx)
        idxs.append(idx)
        logits = jnp.where(lane == idx, 2.0 * NEG, logits)
    es = [jnp.exp(vk - vals[0]) for vk in vals]
    tot = es[0] + es[1] + es[2] + es[3]
    tk = jnp.zeros(logits.shape, F32)
    picked = jnp.zeros(logits.shape, F32)
    for k in range(TOP_K):
        tk = jnp.where(lane == float(k), es[k] / tot, tk)
        tk = jnp.where(lane == float(TOP_K + k), idxs[k], tk)
        picked = jnp.where(lane == idxs[k], 1.0, picked)
    tk_ref[...] = tk
    cnt_ref[...] = cnt_ref[...] + jnp.sum(picked, axis=0, keepdims=True)


def _merge(h, om, att, x, gm, ga, wo, g1, b1, wr, br, tile, name):
    t = x.shape[0]
    rows = lambda w: pl.BlockSpec((tile, w), lambda i: (i, 0))
    const = lambda a: pl.BlockSpec(a.shape, lambda i: (0, 0))
    return pl.pallas_call(
        _merge_kernel,
        out_shape=[jax.ShapeDtypeStruct((t, D_MODEL), F32), jax.ShapeDtypeStruct((t, XP_TILE, LANES), U32),
                   jax.ShapeDtypeStruct((t, LANES), F32), jax.ShapeDtypeStruct((8, LANES), F32)],
        grid=(t // tile,),
        in_specs=[rows(M_WIDTH), rows(M_WIDTH), rows(A_WIDTH), rows(D_MODEL), const(gm), const(ga), const(wo),
                  const(g1), const(b1), const(wr), const(br)],
        out_specs=[rows(D_MODEL), pl.BlockSpec((tile, XP_TILE, LANES), lambda i: (i, 0, 0)), rows(LANES),
                   pl.BlockSpec((8, LANES), lambda i: (0, 0))],
        compiler_params=_cparams("arbitrary"), name=name,
    )(h, om, att, x, gm, ga, wo, g1, b1, wr, br)


def _route_kernel(tk_ref, first_ref, strict_ref, dest_ref, next_scr):
    @pl.when(pl.program_id(0) == 0)
    def _():
        next_scr[...] = first_ref[...]

    tk = tk_ref[...]
    lane = lax.broadcasted_iota(I32, tk.shape, 1).astype(F32)
    onehots = [jnp.where(lane == tk[:, TOP_K + k:TOP_K + k + 1], 1.0, 0.0) for k in range(TOP_K)]
    tot = onehots[0] + onehots[1] + onehots[2] + onehots[3]
    row = jnp.dot(strict_ref[...], tot.astype(BF16), preferred_element_type=F32) + next_scr[0:1, :]
    out = jnp.zeros(tk.shape, F32)
    for k in range(TOP_K):
        out = jnp.where(lane == float(k), jnp.sum(onehots[k] * row, axis=1, keepdims=True), out)
    dest_ref[...] = out.astype(I32)
    next_scr[...] = next_scr[...] + jnp.sum(tot, axis=0, keepdims=True)


def _route(tk, first):
    t = tk.shape[0]
    tile = min(RANK_TILE, t)
    strict = jnp.asarray(np.tril(np.ones((tile, tile), np.float32), -1), BF16)
    return pl.pallas_call(
        _route_kernel, out_shape=jax.ShapeDtypeStruct((t, LANES), I32), grid=(t // tile,),
        in_specs=[pl.BlockSpec((tile, LANES), lambda i: (i, 0)), pl.BlockSpec((8, LANES), lambda i: (0, 0)),
                  pl.BlockSpec((tile, tile), lambda i: (0, 0))],
        out_specs=pl.BlockSpec((tile, LANES), lambda i: (i, 0)),
        scratch_shapes=[pltpu.VMEM((8, LANES), F32)],
        compiler_params=_cparams("arbitrary"), name="moe_route",
    )(tk, first, strict)


def _offsets_kernel(cnt_ref, off_ref, be_ref, nu_ref, pad_ref, *, tile):
    cnt = cnt_ref[...]
    nblk = jnp.floor((cnt + float(tile - 1)) * (1.0 / tile))
    r_i = lax.broadcasted_iota(I32, (LANES, LANES), 0)
    c_i = lax.broadcasted_iota(I32, (LANES, LANES), 1)
    incl = jnp.where(r_i <= c_i, 1.0, 0.0).astype(BF16)
    cum = jnp.dot(nblk.astype(BF16), incl, preferred_element_type=F32)
    off = (cum - nblk) * float(tile)
    off_ref[...] = off
    which = lax.broadcasted_iota(I32, cnt.shape, 0)
    pad_ref[...] = jnp.where(which == 0, off + cnt, jnp.where(which == 1, nblk * float(tile) - cnt, 0.0)).astype(I32)
    rows = be_ref.shape[0]
    jb = (lax.broadcasted_iota(I32, (rows, LANES), 0) * LANES + lax.broadcasted_iota(I32, (rows, LANES), 1)).astype(F32)
    acc = jnp.zeros((rows, LANES), F32)
    for e in range(N_EXPERTS):
        acc = acc + jnp.where(jb >= cum[0:1, e:e + 1], 1.0, 0.0)
    be_ref[...] = jnp.minimum(acc, float(N_EXPERTS - 1)).astype(I32)
    nu_ref[...] = jnp.broadcast_to(cum[0:1, N_EXPERTS - 1:N_EXPERTS], nu_ref.shape).astype(I32)


def _offsets(cnt, n_blocks, tile):
    rows = -(-n_blocks // LANES)
    rows = -(-rows // 8) * 8
    return pl.pallas_call(
        functools.partial(_offsets_kernel, tile=tile),
        out_shape=[jax.ShapeDtypeStruct((8, LANES), F32), jax.ShapeDtypeStruct((rows, LANES), I32),
                   jax.ShapeDtypeStruct((8, LANES), I32), jax.ShapeDtypeStruct((8, LANES), I32)],
        name="moe_offsets",
    )(cnt)


def _scatter_rows(dest_ref, xp_ref, xs_ref, sem):
    t = xp_ref.shape[0]

    def row_copy(tok, dst):
        return pltpu.make_async_copy(xp_ref.at[pl.ds(tok, 1)], xs_ref.at[pl.ds(dst, 1)], sem)

    def issue(grp, carry):
        base = pl.multiple_of(grp * ISSUE_GROUP, ISSUE_GROUP)
        for u in range(ISSUE_GROUP):
            for k in range(TOP_K):
                row_copy(base + u, dest_ref[(base + u) * TOP_K + k]).start(priority=k % 2)
        return carry

    lax.fori_loop(0, t // ISSUE_GROUP, issue, 0)
    for k in range(TOP_K):
        pltpu.make_async_copy(xp_ref, xs_ref.at[pl.ds(0, t)], sem).wait()


def _sc_scatter(rows, idx, n_out):
    t = rows.shape[0]
    n_workers = SC_CORES * SC_SUBCORES
    assert idx.shape[0] == TOP_K * t and t % (n_workers * SC_CHUNK) == 0
    per_worker = t // n_workers
    mesh = plsc.VectorSubcoreMesh(core_axis_name="c", subcore_axis_name="s", num_cores=SC_CORES,
                                  num_subcores=SC_SUBCORES)

    nbuf, chunk = SC_SCATTER_BUFS, SC_CHUNK // SC_SCATTER_BUFS
    assert per_worker % (nbuf * chunk) == 0

    @functools.partial(
        pl.kernel, mesh=mesh, out_type=jax.ShapeDtypeStruct((n_out,) + rows.shape[1:], rows.dtype),
        scratch_types=[pltpu.VMEM((chunk,), I32) for _ in range(nbuf * TOP_K)]
                      + [pltpu.VMEM((chunk,) + rows.shape[1:], rows.dtype) for _ in range(nbuf)]
                      + [pltpu.SemaphoreType.DMA for _ in range(nbuf + 1)])
    def scatter(rows_hbm, idx_hbm, out_hbm, *scratch):
        idx_v = [scratch[u * TOP_K:(u + 1) * TOP_K] for u in range(nbuf)]
        rows_v = scratch[nbuf * TOP_K:nbuf * TOP_K + nbuf]
        lsems, ssem = scratch[nbuf * TOP_K + nbuf:-1], scratch[-1]
        wid = lax.axis_index("s") * SC_CORES + lax.axis_index("c")
        base = wid * per_worker

        @pl.loop(0, per_worker // (nbuf * chunk))
        def _(j):
            loads = []
            for u in range(nbuf):
                off = pl.multiple_of(base + (j * nbuf + u) * chunk, chunk)
                loads.append([pltpu.async_copy(rows_hbm.at[pl.ds(off, chunk)], rows_v[u], lsems[u])]
                             + [pltpu.async_copy(idx_hbm.at[pl.ds(pl.multiple_of(k * t + off, chunk), chunk)],
                                                 idx_v[u][k], lsems[u]) for k in range(TOP_K)])
            scatters = []
            for u in range(nbuf):
                for cp in loads[u]:
                    cp.wait()
                scatters += [pltpu.async_copy(rows_v[u], out_hbm.at[idx_v[u][k]], ssem) for k in range(TOP_K)]
            for cp in scatters:
                cp.wait()

    return scatter(rows, idx)


def _dispatch_rest_kernel(pads_ref, dest2_ref, xp2_ref, xs_in_ref, xs_ref, sem, zsem, zbuf, *, block_rows):
    del xs_in_ref
    zr = zbuf.shape[0]
    n_blocks = xs_ref.shape[0] // block_rows
    zbuf[...] = jnp.zeros_like(zbuf)
    used = pads_ref[2 * N_EXPERTS]

    def pieces(e, act):
        start, n = pads_ref[e], pads_ref[N_EXPERTS + e]
        for sh in range(zr.bit_length() - 1, -1, -1):
            b = 1 << sh
            before = lax.shift_left(lax.shift_right_logical(n, sh + 1), sh + 1)

            @pl.when((n & b) != 0)
            def _():
                act(pltpu.make_async_copy(zbuf.at[pl.ds(0, b)], xs_ref.at[pl.ds(start + before, b)], zsem))

    def tail(jb, act):
        for h in range(block_rows // zr):
            act(pltpu.make_async_copy(zbuf, xs_ref.at[pl.ds(jb * block_rows + h * zr, zr)], zsem))

    for act in (lambda cp: cp.start(), lambda cp: cp.wait()):
        lax.fori_loop(0, N_EXPERTS, lambda e, c: (pieces(e, act), c)[1], 0)
        lax.fori_loop(used, n_blocks, lambda jb, c: (tail(jb, act), c)[1], 0)

    _scatter_rows(dest2_ref, xp2_ref, xs_ref, sem)


def _dispatch_rest(pads, dest2, xp2, xs, block_rows):
    return pl.pallas_call(
        functools.partial(_dispatch_rest_kernel, block_rows=block_rows),
        out_shape=jax.ShapeDtypeStruct(xs.shape, xs.dtype),
        in_specs=[pl.BlockSpec(memory_space=pltpu.SMEM), pl.BlockSpec(memory_space=pltpu.SMEM),
                  pl.BlockSpec(memory_space=pltpu.VMEM), pl.BlockSpec(memory_space=pl.ANY)],
        out_specs=pl.BlockSpec(memory_space=pl.ANY),
        scratch_shapes=[pltpu.SemaphoreType.DMA(()), pltpu.SemaphoreType.DMA(()),
                        pltpu.VMEM((EXPERT_TILE // 2,) + xp2.shape[1:], xp2.dtype)],
        input_output_aliases={3: 0},
        compiler_params=_cparams(), name="moe_dispatch_rest",
    )(pads, dest2, xp2, xs)


def _expert_kernel(be_ref, nu_ref, xs_ref, w1_ref, b1g_ref, b1l_ref, w2_ref, b2_ref, perm_ref, ys_ref,
                   w1g_scr, w1l_scr, w2_scr, xq_scr, y_scr, sem, osem):
    j = pl.program_id(0)
    active = j < nu_ref[0]
    changed = jnp.logical_or(j == 0, be_ref[j] != be_ref[jnp.maximum(j - 1, 0)])
    tm = y_scr.shape[0]
    slot = lax.rem(j, 2)

    def fetch(blk, slot):
        row0 = pl.multiple_of(blk * tm, tm)
        return [pltpu.make_async_copy(xs_ref.at[pl.ds(row0, tm), q, :], xq_scr.at[slot, q], sem.at[slot])
                for q in range(XP_TILE)]

    def put(blk):
        row0 = pl.multiple_of(blk * tm, tm)
        return [pltpu.make_async_copy(y_scr.at[:, q * LANES:(q + 1) * LANES], ys_ref.at[pl.ds(row0, tm), q, :], osem)
                for q in range(YS_TILE)]

    def emit(y):
        @pl.when(j > 0)
        def _():
            for cp in put(j - 1):
                cp.wait()

        y_scr[...] = _pack_halves(y)
        for cp in put(j):
            cp.start()

    @pl.when(j == 0)
    def _():
        for cp in fetch(0, 0):
            cp.start()

    @pl.when(j + 1 < nu_ref[0])
    def _():
        for cp in fetch(j + 1, 1 - slot):
            cp.start()

    @pl.when(jnp.logical_and(active, changed))
    def _():
        for c in range(2 * D_FF // 256):
            wc = w1_ref[0, :, c * 256:(c + 1) * 256].astype(BF16)
            d = jnp.dot(wc, perm_ref[...], preferred_element_type=F32).astype(BF16)
            w1g_scr[:, c * 128:(c + 1) * 128] = d[:, :128]
            w1l_scr[:, c * 128:(c + 1) * 128] = d[:, 128:]
        for c in range(D_FF // 256):
            w2_scr[c * 256:(c + 1) * 256, :] = w2_ref[0, c * 256:(c + 1) * 256, :].astype(BF16)

    @pl.when(active)
    def _():
        for cp in fetch(j, slot):
            cp.wait()
        lo, hi = _unpack_halves(jnp.concatenate([xq_scr[slot, q] for q in range(XP_TILE)], axis=1))
        xb = jnp.concatenate([lo, hi], axis=1)
        hg = jnp.dot(xb, w1g_scr[...], preferred_element_type=F32) + b1g_ref[0]
        hl = jnp.dot(xb, w1l_scr[...], preferred_element_type=F32) + b1l_ref[0]
        x_glu = jnp.minimum(hg, SWIGLU_LIMIT)
        x_lin = jnp.clip(hl, -SWIGLU_LIMIT, SWIGLU_LIMIT)
        a = x_glu * _sigmoid(SWIGLU_ALPHA * x_glu) * (x_lin + 1.0)
        emit(jnp.dot(a.astype(BF16), w2_scr[...], preferred_element_type=F32) + b2_ref[0])

    @pl.when(jnp.logical_not(active))
    def _():
        emit(jnp.zeros((tm, D_MODEL), F32))

    @pl.when(j == pl.num_programs(0) - 1)
    def _():
        for cp in put(j):
            cp.wait()


def _experts(be, nu, xs, w1, b1g, b1l, w2, b2, perm, tile):
    n_blocks = xs.shape[0] // tile
    grid_spec = pltpu.PrefetchScalarGridSpec(
        num_scalar_prefetch=2, grid=(n_blocks,),
        in_specs=[pl.BlockSpec(memory_space=pl.ANY),
                  pl.BlockSpec((1, D_MODEL, 2 * D_FF), lambda j, be, nu: (be[j], 0, 0)),
                  pl.BlockSpec((1, 1, D_FF), lambda j, be, nu: (be[j], 0, 0)),
                  pl.BlockSpec((1, 1, D_FF), lambda j, be, nu: (be[j], 0, 0)),
                  pl.BlockSpec((1, D_FF, D_MODEL), lambda j, be, nu: (be[j], 0, 0)),
                  pl.BlockSpec((1, 1, D_MODEL), lambda j, be, nu: (be[j], 0, 0)),
                  pl.BlockSpec((256, 256), lambda j, be, nu: (0, 0))],
        out_specs=pl.BlockSpec(memory_space=pl.ANY),
        scratch_shapes=[pltpu.VMEM((D_MODEL, D_FF), BF16), pltpu.VMEM((D_MODEL, D_FF), BF16),
                        pltpu.VMEM((D_FF, D_MODEL), BF16), pltpu.VMEM((2, XP_TILE, tile, LANES), U32),
                        pltpu.VMEM((tile, YS_TILE * LANES), U32), pltpu.SemaphoreType.DMA((2,)),
                        pltpu.SemaphoreType.DMA(())])
    return pl.pallas_call(
        _expert_kernel, out_shape=jax.ShapeDtypeStruct((xs.shape[0], YS_TILE, LANES), U32), grid_spec=grid_spec,
        compiler_params=_cparams("arbitrary"), name="moe_experts",
    )(be, nu, xs, w1, b1g, b1l, w2, b2, perm)


def _combine_kernel(dest_ref, next_ref, ys_ref, tk_ref, x1_ref, g2_ref, b2_ref, out_ref, buf, sem):
    i = pl.program_id(0)
    t = x1_ref.shape[0]
    slot = lax.rem(i, 2)

    def gather(idx_ref, s):
        def issue(grp, carry):
            base = pl.multiple_of(grp * 8, 8)
            for u in range(8):
                for k in range(TOP_K):
                    pltpu.make_async_copy(ys_ref.at[idx_ref[(base + u) * TOP_K + k]],
                                          buf.at[s, k, grp, :, u, :], sem.at[s]).start(priority=k % 2)
            return carry

        lax.fori_loop(0, t // 8, issue, 0)

    @pl.when(i == 0)
    def _():
        gather(dest_ref, 0)

    @pl.when(i + 1 < pl.num_programs(0))
    def _():
        gather(next_ref, 1 - slot)

    for k in range(TOP_K):
        for u in range(8):
            pltpu.make_async_copy(ys_ref.at[pl.ds(0, t // 8)], buf.at[slot, k, :, :, u, :], sem.at[slot]).wait()
    tk = tk_ref[...]
    los, his = [], []
    for q in range(YS_TILE):
        lo = hi = None
        for k in range(TOP_K):
            words = buf[slot, k, :, q].reshape(t, LANES)
            g = tk[:, k:k + 1]
            lo_k = g * pltpu.bitcast(words << 16, F32)
            hi_k = g * pltpu.bitcast(words & jnp.uint32(0xFFFF0000), F32)
            lo = lo_k if lo is None else lo + lo_k
            hi = hi_k if hi is None else hi + hi_k
        los.append(lo)
        his.append(hi)
    ff = jnp.concatenate(los + his, axis=1)
    out_ref[...] = _layer_norm(DN_ALPHA * x1_ref[...] + ff, g2_ref[...], b2_ref[...])


def _sc_gather(rows, idx):
    b = idx.shape[0]
    n_workers = SC_CORES * SC_SUBCORES
    assert b % (n_workers * SC_CHUNK) == 0
    per_worker = b // n_workers
    mesh = plsc.VectorSubcoreMesh(core_axis_name="c", subcore_axis_name="s", num_cores=SC_CORES,
                                  num_subcores=SC_SUBCORES)

    n_chunks = per_worker // SC_CHUNK

    @functools.partial(
        pl.kernel, mesh=mesh, out_type=jax.ShapeDtypeStruct((b,) + rows.shape[1:], rows.dtype),
        scratch_types=[pltpu.VMEM((n_chunks, SC_CHUNK), I32), pltpu.VMEM((SC_CHUNK,) + rows.shape[1:], rows.dtype),
                       pltpu.SemaphoreType.DMA])
    def gather(rows_hbm, idx_hbm, out_hbm, idx_v, rows_v, sem):
        wid = lax.axis_index("s") * SC_CORES + lax.axis_index("c")
        pltpu.sync_copy(idx_hbm.at[pl.ds(wid * n_chunks, n_chunks)], idx_v)

        @pl.loop(0, n_chunks)
        def _(j):
            off = pl.multiple_of((wid * n_chunks + j) * SC_CHUNK, SC_CHUNK)
            pltpu.async_copy(rows_hbm.at[idx_v.at[j]], rows_v, sem).wait()
            pltpu.sync_copy(rows_v, out_hbm.at[pl.ds(off, SC_CHUNK)])

    return gather(rows, idx.reshape(b // SC_CHUNK, SC_CHUNK))


def _combine_stream_kernel(g_ref, tk_ref, x1_ref, g2_ref, b2_ref, out_ref, buf, sem):
    i = pl.program_id(0)
    t = x1_ref.shape[0]
    slot = lax.rem(i, 2)

    def fetch(blk, s):
        row0 = pl.multiple_of(blk * t, t)
        return [pltpu.make_async_copy(g_ref.at[k, pl.ds(row0, t), q, :], buf.at[s, k, q], sem.at[s])
                for k in range(TOP_K) for q in range(YS_TILE)]

    @pl.when(i == 0)
    def _():
        for cp in fetch(0, 0):
            cp.start()

    @pl.when(i + 1 < pl.num_programs(0))
    def _():
        for cp in fetch(i + 1, 1 - slot):
            cp.start()

    for cp in fetch(i, slot):
        cp.wait()
    tk = tk_ref[...]
    los, his = [], []
    for q in range(YS_TILE):
        lo = hi = None
        for k in range(TOP_K):
            words = buf[slot, k, q]
            g = tk[:, k:k + 1]
            lo_k = g * pltpu.bitcast(words << 16, F32)
            hi_k = g * pltpu.bitcast(words & jnp.uint32(0xFFFF0000), F32)
            lo = lo_k if lo is None else lo + lo_k
            hi = hi_k if hi is None else hi + hi_k
        los.append(lo)
        his.append(hi)
    ff = jnp.concatenate(los + his, axis=1)
    out_ref[...] = _layer_norm(DN_ALPHA * x1_ref[...] + ff, g2_ref[...], b2_ref[...])


def _combine_stream(g4, tk, x1, g2, b2):
    t = x1.shape[0]
    tile = min(MERGE_TILE, t)
    rows = lambda w: pl.BlockSpec((tile, w), lambda i: (i, 0))
    return pl.pallas_call(
        _combine_stream_kernel, out_shape=jax.ShapeDtypeStruct((t, D_MODEL), F32), grid=(t // tile,),
        in_specs=[pl.BlockSpec(memory_space=pl.ANY), rows(LANES), rows(D_MODEL),
                  pl.BlockSpec((1, D_MODEL), lambda i: (0, 0)),
                  pl.BlockSpec((1, D_MODEL), lambda i: (0, 0))],
        out_specs=rows(D_MODEL),
        scratch_shapes=[pltpu.VMEM((2, TOP_K, YS_TILE, tile, LANES), U32), pltpu.SemaphoreType.DMA((2,))],
        compiler_params=_cparams("arbitrary"), name="moe_combine_stream",
    )(g4, tk, x1, g2, b2)


def _combine(dest_flat, ys, tk, x1, g2, b2):
    t = x1.shape[0]
    tile = min(ROW_TILE, t)
    n = t // tile
    return pl.pallas_call(
        _combine_kernel, out_shape=jax.ShapeDtypeStruct((t, D_MODEL), F32), grid=(n,),
        in_specs=[pl.BlockSpec((tile * TOP_K,), lambda i: (i,), memory_space=pltpu.SMEM),
                  pl.BlockSpec((tile * TOP_K,), lambda i: (jnp.minimum(i + 1, n - 1),), memory_space=pltpu.SMEM),
                  pl.BlockSpec(memory_space=pl.ANY),
                  pl.BlockSpec((tile, LANES), lambda i: (i, 0)),
                  pl.BlockSpec((tile, D_MODEL), lambda i: (i, 0)),
                  pl.BlockSpec((1, D_MODEL), lambda i: (0, 0)),
                  pl.BlockSpec((1, D_MODEL), lambda i: (0, 0))],
        out_specs=pl.BlockSpec((tile, D_MODEL), lambda i: (i, 0)),
        scratch_shapes=[pltpu.VMEM((2, TOP_K, tile // 8, YS_TILE, 8, LANES), U32), pltpu.SemaphoreType.DMA((2,))],
        compiler_params=_cparams("arbitrary"), name="moe_combine",
    )(dest_flat, dest_flat, ys, tk, x1, g2, b2)


def _rel_bucket(dist):
    exact = REL_BUCKETS // 2
    d = np.maximum(dist, 0)
    log_b = exact + (np.log(np.maximum(d, 1).astype(np.float32) / np.float32(exact))
                     / np.float32(math.log(REL_MAX_DIST / exact)) * np.float32(REL_BUCKETS - exact)).astype(np.int32)
    return np.where(d < exact, d, np.minimum(log_b, REL_BUCKETS - 1)).astype(np.int32)


def _bias_lookup(table, bucket, valid):
    bucket = jnp.asarray(bucket)[None]
    acc = jnp.zeros((table.shape[1],) + bucket.shape[1:], F32)
    for b in range(REL_BUCKETS):
        acc = jnp.where(bucket == b, table[b].reshape((-1,) + (1,) * (bucket.ndim - 1)), acc)
    return jnp.where(jnp.asarray(valid)[None], acc, NEG)


def _bias_tables(rel_bias):
    table = rel_bias.astype(F32)
    r = np.arange(WINDOW)[:, None]
    c = np.arange(2 * WINDOW)[None, :]
    dist = r + WINDOW - c
    valid = (dist >= 0) & (dist < WINDOW)
    dist0 = np.where(c < N_META, N_META + r - c, dist)
    valid0 = np.where(c < N_META, dist0 < WINDOW, (c >= WINDOW) & valid)
    both = jnp.stack([_bias_lookup(table, _rel_bucket(dist0), valid0), _bias_lookup(table, _rel_bucket(dist), valid)])
    dist_s = WINDOW - 1 - np.arange(WINDOW)
    rows = _bias_lookup(table[:, np.asarray(HEAD_ORDER)], _rel_bucket(dist_s), np.ones_like(dist_s, bool))
    return both, rows


def _perm_heads(a, axis):
    assert HEAD_ORDER == tuple(kv * A_GROUP + g for g in range(A_GROUP) for kv in range(A_KV_HEADS))
    shape = a.shape
    a = a.reshape(shape[:axis] + (A_KV_HEADS, A_GROUP, A_HD) + shape[axis + 1:])
    return jnp.swapaxes(a, axis, axis + 1).reshape(shape)


def _rep_rows(vec, rows=8):
    out = jnp.zeros((rows, LANES), F32)
    return out.at[:vec.shape[0], :].set(jnp.broadcast_to(vec.astype(F32)[:, None], (vec.shape[0], LANES)))


def kernel(x_prompt, x_sample, cache_swa_k, cache_swa_v, state_mlstm_C, state_mlstm_n, state_mlstm_m, meta_tokens, rel_bias, w_in, b_igate, b_fgate, attn_sinks, g_mlstm_out, g_attn_out, w_out, ln1_g, ln1_b, w_router, b_router, w_moe1, b_moe1, w_moe2, b_moe2, ln2_g, ln2_b):
    B, S, _ = x_prompt.shape
    NB = x_sample.shape[0]
    assert x_sample.shape[1] == 1 and w_in.shape[0] == 1
    assert S % PROJ_TILE == 0 and S % M_CHUNK == 0 and S % WINDOW == 0 and NB % SAMPLE_GROUP == 0
    l = 0

    assert IN_WIDTHS == (512, 512, 512, 512, 4, 4, 512, 128, 128)
    bf = lambda a: a.astype(BF16)
    w = w_in[l]
    n_main, n_gate = 4 * M_WIDTH, 2 * M_HEADS
    w_gate = w[:, n_main:n_main + n_gate]
    w_att = w[:, n_main + n_gate:]
    assert math.frexp(A_HD ** -0.5)[0] == 0.5
    w_qa = _perm_heads(w_att[:, :A_WIDTH], 1) * (A_HD ** -0.5)
    wr = bf(jnp.concatenate([w[:, :n_main], w_qa, w_att[:, A_WIDTH:],
                             jnp.pad(w_gate, ((0, 0), (0, LANES - n_gate)))], axis=1))
    wt = bf(jnp.concatenate([w[:, :M_WIDTH], w[:, 2 * M_WIDTH:3 * M_WIDTH], w_qa, w_att[:, A_WIDTH + LANES:],
                             w_gate], axis=1).T)
    b_gate = jnp.concatenate([b_igate[l], b_fgate[l]]).astype(F32)
    brow = jnp.pad(b_gate, (0, LANES - n_gate))[None, :]
    bcol = b_gate[:, None]
    plan_p = ((512, 512, "plain", BF16), (1536, 512, "plain", F32), (2560, 128, "plain", BF16),
              (2816, 128, "gate", F32))
    tplan_p = ((0, 512, "plain", BF16), (512, 512, "plain", BF16), (1024, 512, "plain", BF16),
               (1536, 128, "plain", BF16), (1664, 8, "gate", F32))
    plan_s = ((0, 512, "plain", F32), (512, 512, "plain", F32), (1024, 512, "plain", F32), (1536, 512, "plain", F32),
              (2048, 512, "plain", F32), (2560, 128, "plain", F32), (2688, 128, "plain", F32), (2816, 128, "gate", F32))

    bias_tab, bias_rows = _bias_tables(rel_bias)
    sinks = _rep_rows(attn_sinks[l])
    sinks_step = _rep_rows(attn_sinks[l][np.asarray(HEAD_ORDER)])
    g_m = g_mlstm_out[l].astype(F32)[None, :]
    g_a = _perm_heads(g_attn_out[l].astype(F32), 0)[None, :]
    wo = bf(jnp.concatenate([w_out[l][:M_WIDTH], _perm_heads(w_out[l][M_WIDTH:], 0)], axis=0))
    g1, b1 = ln1_g[l].astype(F32)[None, :], ln1_b[l].astype(F32)[None, :]
    g2, b2 = ln2_g[l].astype(F32)[None, :], ln2_b[l].astype(F32)[None, :]
    w_r = bf(jnp.pad(w_router[l], ((0, 0), (0, LANES - N_EXPERTS))))
    b_r = jnp.pad(b_router[l].astype(F32), (0, LANES - N_EXPERTS), constant_values=NEG)[None, :]
    b1g = b_moe1[l][:, 0::2].astype(F32)[:, None, :]
    b1l = b_moe1[l][:, 1::2].astype(F32)[:, None, :]
    b2e = b_moe2[l].astype(F32)[:, None, :]
    pj = np.zeros((256, 256), np.float32)
    pj[2 * np.arange(128), np.arange(128)] = 1.0
    pj[2 * np.arange(128) + 1, 128 + np.arange(128)] = 1.0
    perm = jnp.asarray(pj, BF16)

    xp2 = x_prompt.reshape(B * S, D_MODEL)
    km, om, ka, gc, qt, vt, qat, vat, gr, kv_tail = _proj(
        xp2, wr, wt, brow, bcol, plan_p, tplan_p, (2560, 256), PROJ_TILE, S, "proj_prompt")
    x_meta = jnp.pad(meta_tokens.astype(F32), ((0, M_CHUNK - N_META), (0, 0)))
    km0, _, ka0, gc0, qt0, vt0, _, vat0, gr0 = _proj(
        x_meta, wr, wt, brow, bcol, plan_p, tplan_p, None, M_CHUNK, M_CHUNK, "proj_meta")
    xs2 = x_sample.reshape(NB, D_MODEL)
    qm_s, km_s, vm_s, om_s, qa_s, ka_s, va_s, gc_s = _proj(
        xs2, wr, wt, brow, bcol, plan_s, (), None, NB, NB, "proj_sample")

    zero_c = jnp.zeros((M_HEADS, M_DV + 8, M_DK), F32)
    zero_m = jnp.zeros((8, LANES), F32)
    _, c_meta, m_meta = _mlstm(qt0, km0, vt0, gc0, gr0, zero_c, zero_m, 1, N_META, "mlstm_meta")
    h_p, c_p, m_p = _mlstm(qt, km, vt, gc, gr, c_meta[0], m_meta[0], B, M_CHUNK, "mlstm_prompt")
    C_p = c_p[:, :, :M_DV, :]
    n_p = c_p[:, :, M_DV, :]
    m_prompt = m_p[:, :M_HEADS, 0]
    m_pad = jnp.pad(state_mlstm_m[l].astype(F32), ((0, 0), (0, LANES - M_HEADS)))
    C_s, n_s, m_s, h_s = _mlstm_step(state_mlstm_C[l].astype(F32), state_mlstm_n[l].astype(F32), m_pad,
                                     gc_s, qm_s, km_s, vm_s)

    att_p = _swa(qat, ka, vat, ka0, vat0, jnp.swapaxes(bias_tab, 2, 3), sinks, B)
    ck = cache_swa_k[l].reshape(NB, WINDOW, LANES)
    cv = cache_swa_v[l].reshape(NB, WINDOW, LANES)
    k_new, v_new, att_s = _swa_step(ck, cv, qa_s, ka_s, va_s, bias_rows, sinks_step)

    x1_p, xpk_p, tk_p, cnt_p = _merge(h_p, om, att_p, xp2, g_m, g_a, wo, g1, b1, w_r, b_r, MERGE_TILE, "merge_prompt")
    x1_s, xpk_s, tk_s, cnt_s = _merge(h_s, om_s, att_s, xs2, g_m, g_a, wo, g1, b1, w_r, b_r, NB, "merge_sample")

    T_p = B * S
    assert T_p % RANK_TILE == 0 and T_p % ROW_TILE == 0
    n_blocks = -(-((T_p + NB) * TOP_K) // EXPERT_TILE) + N_EXPERTS
    off, be2, nu2, pad = _offsets(cnt_p + cnt_s, n_blocks, EXPERT_TILE)
    pads = jnp.concatenate([pad[0, :N_EXPERTS], pad[1, :N_EXPERTS], nu2[0, :1]])
    dest_p = _route(tk_p, off)[:, :TOP_K].reshape(-1)
    dest_s = _route(tk_s, off + cnt_p)[:, :TOP_K].reshape(-1)
    be = be2.reshape(-1)[:n_blocks]
    nu = nu2[0, :1]
    idx_p = dest_p.reshape(T_p, TOP_K).T.reshape(-1)
    xs = _sc_scatter(xpk_p, idx_p, n_blocks * EXPERT_TILE)
    xs = _dispatch_rest(pads, dest_s, xpk_s, xs, EXPERT_TILE)
    ys = _experts(be, nu, xs, w_moe1[l], b1g, b1l, w_moe2[l], b2e, perm, EXPERT_TILE)
    g4 = _sc_gather(ys, idx_p).reshape(TOP_K, T_p, YS_TILE, LANES)
    y_p = _combine_stream(g4, tk_p, x1_p, g2, b2)
    y_s = _combine(dest_s, ys, tk_s, x1_s, g2, b2)

    kv_tail = kv_tail.reshape(B, WINDOW, 2, A_KV_HEADS, A_HD)
    dt_k, dt_v = cache_swa_k.dtype, cache_swa_v.dtype
    return (y_p.reshape(B, S, D_MODEL).astype(x_prompt.dtype), y_s.reshape(NB, 1, D_MODEL).astype(x_sample.dtype),
            kv_tail[:, :, 0][None].astype(dt_k), kv_tail[:, :, 1][None].astype(dt_v),
            C_p[None].astype(state_mlstm_C.dtype), n_p[None].astype(state_mlstm_n.dtype),
            m_prompt[None].astype(state_mlstm_m.dtype),
            k_new.reshape(1, NB, WINDOW, A_KV_HEADS, A_HD).astype(dt_k),
            v_new.reshape(1, NB, WINDOW, A_KV_HEADS, A_HD).astype(dt_v),
            C_s[None].astype(state_mlstm_C.dtype), n_s[None].astype(state_mlstm_n.dtype),
            m_s[:, :M_HEADS][None].astype(state_mlstm_m.dtype))
```

```python
import functools
import math

import numpy as np
import jax
import jax.numpy as jnp
from jax import lax
from jax.experimental import pallas as pl
from jax.experimental.pallas import tpu as pltpu
from jax.experimental.pallas import tpu_sc as plsc

F32 = jnp.float32
BF16 = jnp.bfloat16
I32 = jnp.int32
U32 = jnp.uint32

D_MODEL = 1024
N_META = 16
M_HEADS = 4
M_DK = 128
M_DV = 128
M_WIDTH = M_HEADS * M_DV
A_HD = 64
A_HEADS = 8
A_KV_HEADS = 2
A_GROUP = A_HEADS // A_KV_HEADS
A_WIDTH = A_HEADS * A_HD
WINDOW = 128
REL_BUCKETS = 32
REL_MAX_DIST = 128
N_EXPERTS = 32
TOP_K = 4
D_FF = D_MODEL
SWIGLU_LIMIT = 7.0
SWIGLU_ALPHA = 1.702
DEPTH = 1
DN_ALPHA = (2.0 * DEPTH) ** 0.25
LN_EPS = 1e-5
IN_WIDTHS = (M_WIDTH, M_WIDTH, M_WIDTH, M_WIDTH, M_HEADS, M_HEADS, A_WIDTH, A_KV_HEADS * A_HD, A_KV_HEADS * A_HD)

LANES = 128
NEG = -1e30
VMEM_LIMIT = 56 * 1024 * 1024

M_CHUNK = 256
PROJ_TILE = 1024
MERGE_TILE = 1024
MERGE_SUB = 256
RANK_TILE = 1024
ROW_TILE = 512
EXPERT_TILE = 512
SAMPLE_GROUP = 8
ISSUE_GROUP = 8
SWA_QBLOCKS = 8
SC_CORES, SC_SUBCORES = 2, 16
SC_CHUNK = 128
SC_SCATTER_BUFS = 2
MLSTM_SEQS = 4
STEP_PIPE_LAG = 3
SWA_PIPE_LAG = 2
XP_TILE = D_MODEL // 2 // LANES
YS_TILE = D_MODEL // 2 // LANES
HEAD_ORDER = (0, 4, 1, 5, 2, 6, 3, 7)


def _cparams(*sem):
    return pltpu.CompilerParams(dimension_semantics=sem, vmem_limit_bytes=VMEM_LIMIT)


def _log_sigmoid(x):
    return jnp.minimum(x, 0.0) - jnp.log1p(jnp.exp(-jnp.abs(x)))


def _sigmoid(x):
    return 1.0 / (1.0 + jnp.exp(-x))


def _proj_kernel(x_ref, wr_ref, wt_ref, brow_ref, bcol_ref, *outs, row_plan, t_plan, tail_cols):
    xb = x_ref[...].astype(BF16)
    tm = xb.shape[0]
    o = 0
    for (c0, width, kind, _) in row_plan:
        r = jnp.dot(xb, wr_ref[:, c0:c0 + width], preferred_element_type=F32)
        if kind == "gate":
            r = r + brow_ref[...]
            lane = lax.broadcasted_iota(I32, r.shape, 1)
            r = jnp.where(lane < M_HEADS, r, _log_sigmoid(r))
        outs[o][...] = r.astype(outs[o].dtype)
        o += 1
    for (r0, nrows, kind, _) in t_plan:
        r = lax.dot_general(wt_ref[r0:r0 + nrows, :], xb, (((1,), (1,)), ((), ())), preferred_element_type=F32)
        if kind == "gate":
            r = r + bcol_ref[...]
            row = lax.broadcasted_iota(I32, r.shape, 0)
            r = jnp.where(row < M_HEADS, r, _log_sigmoid(r))
        outs[o][...] = r.astype(outs[o].dtype)
        o += 1
    if tail_cols is not None:
        c0, width = tail_cols
        outs[o][...] = jnp.dot(xb[tm - WINDOW:, :], wr_ref[:, c0:c0 + width], preferred_element_type=F32)


def _proj(x, wr, wt, brow, bcol, row_plan, t_plan, tail_cols, tile, rows_per_group, name):
    t = x.shape[0]
    nt = t // tile
    out_shape, out_specs = [], []
    for (_, width, _, dt) in row_plan:
        out_shape.append(jax.ShapeDtypeStruct((t, width), dt))
        out_specs.append(pl.BlockSpec((tile, width), lambda i: (i, 0)))
    for (_, nrows, _, dt) in t_plan:
        out_shape.append(jax.ShapeDtypeStruct((nrows, t), dt))
        out_specs.append(pl.BlockSpec((nrows, tile), lambda i: (0, i)))
    if tail_cols is not None:
        tiles_per_group = rows_per_group // tile
        out_shape.append(jax.ShapeDtypeStruct((t // rows_per_group * WINDOW, tail_cols[1]), F32))
        out_specs.append(pl.BlockSpec((WINDOW, tail_cols[1]), lambda i: (i // tiles_per_group, 0)))
    kern = functools.partial(_proj_kernel, row_plan=row_plan, t_plan=t_plan, tail_cols=tail_cols)
    return pl.pallas_call(
        kern, out_shape=out_shape, grid=(nt,),
        in_specs=[pl.BlockSpec((tile, D_MODEL), lambda i: (i, 0)),
                  pl.BlockSpec(wr.shape, lambda i: (0, 0)),
                  pl.BlockSpec(wt.shape, lambda i: (0, 0)),
                  pl.BlockSpec(brow.shape, lambda i: (0, 0)),
                  pl.BlockSpec(bcol.shape, lambda i: (0, 0))],
        out_specs=out_specs, compiler_params=_cparams("arbitrary"), name=name,
    )(x, wr, wt, brow, bcol)


def _split3(a):
    hi = a.astype(BF16)
    r1 = a - hi.astype(F32)
    mid = r1.astype(BF16)
    lo = (r1 - mid.astype(F32)).astype(BF16)
    return hi, mid, lo


def _mlstm_kernel(*refs, n_valid, nseq):
    seq_in = [refs[5 * i:5 * i + 5] for i in range(nseq)]
    c0_ref, m0_ref = refs[5 * nseq:5 * nseq + 2]
    h_ref, c_out_ref, m_out_ref, c_scr, m_scr = refs[5 * nseq + 2:]
    c = pl.program_id(1)
    nc = pl.num_programs(1)
    L = seq_in[0][1].shape[0]

    @pl.when(c == 0)
    def _():
        for i in range(nseq):
            c_scr[i] = c0_ref[...]
            m_scr[i] = m0_ref[...]

    r_i = lax.broadcasted_iota(I32, (L, L), 0)
    c_i = lax.broadcasted_iota(I32, (L, L), 1)
    upper = r_i <= c_i
    tril = jnp.where(c_i <= r_i, 1.0, 0.0).astype(BF16)
    triu = jnp.where(upper, 1.0, 0.0).astype(BF16)
    scale = M_DK ** -0.5
    ones_rows = jnp.where(lax.broadcasted_iota(I32, (8, L), 0) == 0, 1.0, 0.0).astype(BF16)

    gates = []
    for (_, _, _, gc_ref, gr_ref) in seq_in:
        gc = gc_ref[...]
        gr = gr_ref[...]
        if n_valid < L:
            rowc = lax.broadcasted_iota(I32, gc.shape, 0)
            lanec = lax.broadcasted_iota(I32, gc.shape, 1)
            gc = jnp.where(rowc < n_valid, gc, jnp.where(lanec < M_HEADS, NEG, 0.0))
            rowr = lax.broadcasted_iota(I32, gr.shape, 0)
            colr = lax.broadcasted_iota(I32, gr.shape, 1)
            gr = jnp.where(colr < n_valid, gr, jnp.where(rowr < M_HEADS, NEG, 0.0))
        b_cols = sum(jnp.dot(tril, part, preferred_element_type=F32) for part in _split3(gc))
        b_rows = sum(jnp.dot(part, triu, preferred_element_type=F32) for part in _split3(gr))
        gates.append((gc, gr, b_cols, b_rows))

    m_alls = [m_scr[i] for i in range(nseq)]
    c_alls = [[c_scr[i, h] for h in range(M_HEADS)] for i in range(nseq)]
    h_new, c_new, m_new_all = {}, {}, {}

    def operands(u):
        i, h = u
        qt_ref, k_ref, vt_ref = seq_in[i][:3]
        sl = slice(h * M_DK, (h + 1) * M_DK)
        return qt_ref[sl, :], k_ref[:, sl], jnp.concatenate([vt_ref[sl, :], ones_rows], axis=0)

    def stage_a(u):
        i, h = u
        gc, gr, b_cols, b_rows = gates[i]
        qt, k, vt_aug = operands(u)
        ig_r = gr[h:h + 1, :]
        b_r = b_rows[M_HEADS + h:M_HEADS + h + 1, :]
        m_prev = m_alls[i][h:h + 1, 0:1]
        cs = c_alls[i][h]
        qk = jnp.dot(k, qt, preferred_element_type=F32)
        inter = jnp.dot(cs.astype(BF16), qt, preferred_element_type=F32)
        b_last = b_r[:, L - 1:L]
        g = ig_r + b_last - b_r
        m_new = jnp.maximum(b_last + m_prev, jnp.max(g, axis=1, keepdims=True))
        a = jnp.exp(b_last + m_prev - m_new)
        wv = (vt_aug.astype(F32) * jnp.exp(g - m_new)).astype(BF16)
        c_new[u] = a * cs + jnp.dot(wv, k, preferred_element_type=F32) * scale
        m_new_all[u] = jnp.broadcast_to(m_new, (1, LANES))
        return qk, inter

    def stage_b(u, qk, inter):
        i, h = u
        gc, gr, b_cols, b_rows = gates[i]
        b_r = b_rows[M_HEADS + h:M_HEADS + h + 1, :]
        m_prev = m_alls[i][h:h + 1, 0:1]
        r_c = gc[:, h:h + 1] - b_cols[:, M_HEADS + h:M_HEADS + h + 1]
        dt = jnp.where(upper, b_r + r_c, NEG)
        m_t = jnp.maximum(b_r + m_prev, jnp.max(dt, axis=0, keepdims=True))
        st = (qk * (scale * jnp.exp(dt - m_t))).astype(BF16)
        return st, jnp.exp(b_r + m_prev - m_t) * inter, jnp.exp(-m_t)

    def stage_c(u, st, inter_w, floor):
        _, _, vt_aug = operands(u)
        nd = inter_w + jnp.dot(vt_aug, st, preferred_element_type=F32)
        den = nd[M_DV:M_DV + 1, :]
        h_new[u] = (nd[:M_DV, :] / jnp.maximum(jnp.abs(den), floor)).T

    units = [(i, h) for h in range(M_HEADS) for i in range(nseq)]
    a_q, b_q = {}, {}
    for n in range(len(units) + 2):
        if n < len(units):
            a_q[n] = stage_a(units[n])
        if 0 <= n - 1 < len(units):
            b_q[n - 1] = stage_b(units[n - 1], *a_q.pop(n - 1))
        if 0 <= n - 2 < len(units):
            stage_c(units[n - 2], *b_q.pop(n - 2))

    for i in range(nseq):
        h_ref[0, i] = jnp.concatenate([h_new[(i, h)] for h in range(M_HEADS)], axis=1)
        for h in range(M_HEADS):
            c_scr[i, h] = c_new[(i, h)]
        m_scr[i, 0:M_HEADS, :] = jnp.concatenate([m_new_all[(i, h)] for h in range(M_HEADS)], axis=0)

    @pl.when(c == nc - 1)
    def _():
        c_out_ref[...] = c_scr[...]
        m_out_ref[...] = m_scr[...]


def _mlstm(qt, km, vt, gc, gr, c0, m0, batch, n_valid, name):
    L = M_CHUNK
    nc = km.shape[0] // (batch * L)
    nseq = MLSTM_SEQS if batch % MLSTM_SEQS == 0 else 1
    kern = functools.partial(_mlstm_kernel, n_valid=n_valid, nseq=nseq)
    in_specs, operands = [], []
    for i in range(nseq):
        blk = functools.partial(lambda b, c, i: (b * nseq + i) * nc + c, i=i)
        rows = pl.BlockSpec((L, M_WIDTH), functools.partial(lambda b, c, blk: (blk(b, c), 0), blk=blk))
        cols = pl.BlockSpec((M_WIDTH, L), functools.partial(lambda b, c, blk: (0, blk(b, c)), blk=blk))
        in_specs += [cols, rows, cols,
                     pl.BlockSpec((L, LANES), functools.partial(lambda b, c, blk: (blk(b, c), 0), blk=blk)),
                     pl.BlockSpec((8, L), functools.partial(lambda b, c, blk: (0, blk(b, c)), blk=blk))]
        operands += [qt, km, vt, gc, gr]
    in_specs += [pl.BlockSpec((M_HEADS, M_DV + 8, M_DK), lambda b, c: (0, 0, 0)),
                 pl.BlockSpec((8, LANES), lambda b, c: (0, 0))]
    h4, c_fin, m_fin = pl.pallas_call(
        kern,
        out_shape=[jax.ShapeDtypeStruct((batch // nseq, nseq, nc * L, M_WIDTH), F32),
                   jax.ShapeDtypeStruct((batch, M_HEADS, M_DV + 8, M_DK), F32),
                   jax.ShapeDtypeStruct((batch, 8, LANES), F32)],
        grid=(batch // nseq, nc),
        in_specs=in_specs,
        out_specs=[pl.BlockSpec((1, nseq, L, M_WIDTH), lambda b, c: (b, 0, c, 0)),
                   pl.BlockSpec((nseq, M_HEADS, M_DV + 8, M_DK), lambda b, c: (b, 0, 0, 0)),
                   pl.BlockSpec((nseq, 8, LANES), lambda b, c: (b, 0, 0))],
        scratch_shapes=[pltpu.VMEM((nseq, M_HEADS, M_DV + 8, M_DK), F32), pltpu.VMEM((nseq, 8, LANES), F32)],
        compiler_params=_cparams("arbitrary", "arbitrary"), name=name,
    )(*operands, c0, m0)
    return h4.reshape(batch * nc * L, M_WIDTH), c_fin, m_fin


def _outer_f32(a, b):
    ah, am, al = (t.astype(F32) for t in _split3(a))
    bh, bm, bl = (t.astype(F32) for t in _split3(b))
    z = jnp.zeros_like(ah)
    lhs = jnp.concatenate([ah, ah, ah, am, am, al, z, z], axis=0).astype(BF16)
    rhs = jnp.concatenate([bh, bm, bl, bh, bm, bh, z, z], axis=0).astype(BF16)
    return lax.dot_general(lhs, rhs, (((0,), (0,)), ((), ())), preferred_element_type=F32)


def _mlstm_step_kernel(c_ref, n_ref, m_ref, gc_ref, q_ref, k_ref, v_ref,
                       c_out_ref, n_out_ref, m_out_ref, h_ref):
    g = c_ref.shape[0]
    assert g == 8
    scale = M_DK ** -0.5
    ig = gc_ref[:, 0:M_HEADS]
    lf = gc_ref[:, M_HEADS:2 * M_HEADS]
    m = m_ref[:, 0:M_HEADS]
    m_t = jnp.maximum(lf + m, ig)
    w = jnp.exp(lf + m - m_t)
    wg = jnp.exp(ig - m_t)
    floor = jnp.exp(-m_t)
    m_out_ref[...] = jnp.zeros_like(m_out_ref)
    m_out_ref[:, 0:M_HEADS] = m_t
    row8 = lax.broadcasted_iota(I32, (g, M_DV), 0)

    per_head = []
    for h in range(M_HEADS):
        sl = slice(h * M_DK, (h + 1) * M_DK)
        q = q_ref[:, sl]
        k = k_ref[:, sl] * scale
        v = v_ref[:, sl]
        n = n_ref[:, h, :]
        w_h, wg_h = w[:, h:h + 1], wg[:, h:h + 1]
        s = jnp.sum(q * k, axis=1, keepdims=True) * wg_h
        den = w_h * jnp.sum(n * q, axis=1, keepdims=True) + s
        n_out_ref[:, h, :] = w_h * n + wg_h * k
        per_head.append((q.astype(BF16), k, wg_h * v, w_h, s * v, 1.0 / jnp.maximum(jnp.abs(den), floor[:, h:h + 1])))

    def stage_a(h, j):
        qb = per_head[h][0]
        r = lax.dot_general(qb, c_ref[j, h].astype(BF16), (((1,), (1,)), ((), ())), preferred_element_type=F32)
        return jnp.where(row8 == j, r, 0.0)

    def stage_b(h, j):
        _, k, wv, w_h, _, _ = per_head[h]
        c_out_ref[j, h] = w_h[j:j + 1, :] * c_ref[j, h] + _outer_f32(wv[j:j + 1, :], k[j:j + 1, :])

    units = [(h, j) for h in range(M_HEADS) for j in range(g)]
    lag = STEP_PIPE_LAG
    cq = [jnp.zeros((g, M_DV), F32) for _ in range(M_HEADS)]
    for i in range(len(units) + lag):
        if i < len(units):
            cq[units[i][0]] = cq[units[i][0]] + stage_a(*units[i])
        if 0 <= i - lag < len(units):
            stage_b(*units[i - lag])
    for h in range(M_HEADS):
        _, _, _, w_h, sv, inv = per_head[h]
        h_ref[:, h * M_DK:(h + 1) * M_DK] = (w_h * cq[h] + sv) * inv


def _mlstm_step(c, n, m_pad, gc, q, k, v):
    nb = c.shape[0]
    g = SAMPLE_GROUP
    row = lambda w: pl.BlockSpec((g, w), lambda i: (i, 0))
    return pl.pallas_call(
        _mlstm_step_kernel,
        out_shape=[jax.ShapeDtypeStruct(c.shape, F32), jax.ShapeDtypeStruct(n.shape, F32),
                   jax.ShapeDtypeStruct((nb, LANES), F32), jax.ShapeDtypeStruct((nb, M_WIDTH), F32)],
        grid=(nb // g,),
        in_specs=[pl.BlockSpec((g, M_HEADS, M_DV, M_DK), lambda i: (i, 0, 0, 0)),
                  pl.BlockSpec((g, M_HEADS, M_DK), lambda i: (i, 0, 0)),
                  row(LANES), row(LANES), row(M_WIDTH), row(M_WIDTH), row(M_WIDTH)],
        out_specs=[pl.BlockSpec((g, M_HEADS, M_DV, M_DK), lambda i: (i, 0, 0, 0)),
                   pl.BlockSpec((g, M_HEADS, M_DK), lambda i: (i, 0, 0)),
                   row(LANES), row(M_WIDTH)],
        compiler_params=_cparams("arbitrary"), name="mlstm_step",
    )(c, n, m_pad, gc, q, k, v)


def _swa_kernel(qt_ref, kc_ref, kp_ref, vtc_ref, vtp_ref, km_ref, vtm_ref, bias_ref, sink_ref, o_ref):
    j = pl.program_id(1)
    first = j == 0
    blk = WINDOW
    nqb = qt_ref.shape[1] // blk
    kp = jnp.where(first, km_ref[...], kp_ref[...])
    vtp = jnp.where(first, vtm_ref[...], vtp_ref[...])
    k = jnp.concatenate([kp, kc_ref[...]], axis=0)
    vt = jnp.concatenate([vtp, vtc_ref[...]], axis=1)
    row_v = lax.broadcasted_iota(I32, vt.shape, 0)
    zero_v = jnp.zeros_like(vt)
    vt_half = (jnp.where(row_v < A_HD, vt, zero_v), jnp.where(row_v >= A_HD, vt, zero_v))
    row_q = lax.broadcasted_iota(I32, (LANES, blk), 0)
    lo_rows = row_q < A_HD
    def scores(u, p):
        cols = slice(u * blk, (u + 1) * blk)
        keys = slice(u * blk, (u + 2) * blk)
        qs = qt_ref[p * LANES:(p + 1) * LANES, cols]
        zero_q = jnp.zeros_like(qs)
        q_own = (jnp.where(lo_rows, qs, zero_q), jnp.where(lo_rows, zero_q, qs))
        return [jnp.dot(k[keys], q_own[half], preferred_element_type=F32) for half in range(2)]

    def softmax(u, p, s2):
        table = jnp.where(first, 0, 1) if u == 0 else 1
        probs, inv = [], []
        for half in range(2):
            hd = HEAD_ORDER[2 * p + half]
            s = s2[half] + bias_ref[table, hd]
            sk = sink_ref[hd:hd + 1, 0:1]
            m = jnp.maximum(jnp.max(s, axis=0, keepdims=True), sk)
            e = jnp.exp(s - m)
            probs.append(e.astype(BF16))
            inv.append(1.0 / (jnp.sum(e, axis=0, keepdims=True) + jnp.exp(sk - m)))
        return jnp.concatenate(probs, axis=0), jnp.where(lo_rows, inv[0], inv[1])

    def values(u, p, probs, inv):
        keys = slice(u * blk, (u + 2) * blk)
        vt_stack = jnp.concatenate([vt_half[0][:, keys], vt_half[1][:, keys]], axis=1)
        ot = jnp.dot(vt_stack, probs, preferred_element_type=F32)
        o_ref[u * blk:(u + 1) * blk, p * LANES:(p + 1) * LANES] = (ot * inv).T

    units = [(u, p) for u in range(nqb) for p in range(A_GROUP)]
    s_q, p_q = {}, {}
    lag = SWA_PIPE_LAG
    for i in range(len(units) + 2 * lag):
        if i < len(units):
            s_q[i] = scores(*units[i])
        if 0 <= i - lag < len(units):
            p_q[i - lag] = softmax(*units[i - lag], s_q.pop(i - lag))
        if 0 <= i - 2 * lag < len(units):
            values(*units[i - 2 * lag], *p_q.pop(i - 2 * lag))


def _swa(qat, ka, vat, kmeta, vtmeta, bias_t, sinks, batch):
    blk = WINDOW
    nqb = SWA_QBLOCKS
    t = ka.shape[0]
    nq = t // (batch * blk * nqb)
    prev = lambda b, j: (b * nq + j) * nqb + jnp.where(j == 0, 0, -1)
    const2 = lambda shape: pl.BlockSpec(shape, lambda b, j: (0, 0))
    return pl.pallas_call(
        _swa_kernel, out_shape=jax.ShapeDtypeStruct((t, A_WIDTH), F32), grid=(batch, nq),
        in_specs=[pl.BlockSpec((A_WIDTH, nqb * blk), lambda b, j: (0, b * nq + j)),
                  pl.BlockSpec((nqb * blk, LANES), lambda b, j: (b * nq + j, 0)),
                  pl.BlockSpec((blk, LANES), lambda b, j: (prev(b, j), 0)),
                  pl.BlockSpec((LANES, nqb * blk), lambda b, j: (0, b * nq + j)),
                  pl.BlockSpec((LANES, blk), lambda b, j: (0, prev(b, j))),
                  const2((blk, LANES)), const2((LANES, blk)),
                  pl.BlockSpec(bias_t.shape, lambda b, j: (0, 0, 0, 0)),
                  const2((8, LANES))],
        out_specs=pl.BlockSpec((nqb * blk, A_WIDTH), lambda b, j: (b * nq + j, 0)),
        compiler_params=_cparams("arbitrary", "arbitrary"), name="swa_prompt",
    )(qat, ka, ka, vat, vat, kmeta, vtmeta, bias_t, sinks)


def _swa_step_kernel(ck_ref, cv_ref, q_ref, k_ref, v_ref, bias_ref, sink_ref, ko_ref, vo_ref, o_ref):
    g = ck_ref.shape[0]
    lane = lax.broadcasted_iota(I32, (A_HEADS, LANES), 1)
    row = lax.broadcasted_iota(I32, (A_HEADS, LANES), 0)
    own_half = (row % 2 == 0) == (lane < A_HD)
    bias = bias_ref[...]
    sk = sink_ref[:, 0:1]
    def stage_a(j):
        ko_ref[j, 0:WINDOW - 1, :] = ck_ref[j, 1:WINDOW, :]
        ko_ref[j, WINDOW - 1:WINDOW, :] = k_ref[j:j + 1, :]
        vo_ref[j, 0:WINDOW - 1, :] = cv_ref[j, 1:WINDOW, :]
        vo_ref[j, WINDOW - 1:WINDOW, :] = v_ref[j:j + 1, :]
        kk = ko_ref[j].astype(BF16)
        slabs = [q_ref[j:j + 1, p * LANES:(p + 1) * LANES] for p in range(A_GROUP)]
        q8 = jnp.concatenate([slabs[r // 2] for r in range(A_HEADS)], axis=0)
        q8 = jnp.where(own_half, q8, 0.0).astype(BF16)
        s = lax.dot_general(q8, kk, (((1,), (1,)), ((), ())), preferred_element_type=F32)
        s = s + bias
        m = jnp.maximum(jnp.max(s, axis=1, keepdims=True), sk)
        e = jnp.exp(s - m)
        return e.astype(BF16), 1.0 / (jnp.sum(e, axis=1, keepdims=True) + jnp.exp(sk - m))

    def stage_b(j, p8, inv):
        vv = vo_ref[j].astype(BF16)
        o8 = jnp.where(own_half, jnp.dot(p8, vv, preferred_element_type=F32) * inv, 0.0)
        for p in range(A_GROUP):
            o_ref[j:j + 1, p * LANES:(p + 1) * LANES] = o8[2 * p:2 * p + 1, :] + o8[2 * p + 1:2 * p + 2, :]

    lag, pending = STEP_PIPE_LAG, {}
    for i in range(g + lag):
        if i < g:
            pending[i] = stage_a(i)
        if 0 <= i - lag < g:
            stage_b(i - lag, *pending.pop(i - lag))


def _swa_step(ck, cv, q, k, v, bias_rows, sinks):
    nb = ck.shape[0]
    g = SAMPLE_GROUP
    cache = pl.BlockSpec((g, WINDOW, LANES), lambda i: (i, 0, 0))
    row = lambda w: pl.BlockSpec((g, w), lambda i: (i, 0))
    const = lambda a: pl.BlockSpec(a.shape, lambda i: (0, 0))
    return pl.pallas_call(
        _swa_step_kernel,
        out_shape=[jax.ShapeDtypeStruct(ck.shape, F32), jax.ShapeDtypeStruct(cv.shape, F32),
                   jax.ShapeDtypeStruct((nb, A_WIDTH), F32)],
        grid=(nb // g,),
        in_specs=[cache, cache, row(A_WIDTH), row(LANES), row(LANES), const(bias_rows), const(sinks)],
        out_specs=[cache, cache, row(A_WIDTH)],
        compiler_params=_cparams("arbitrary"), name="swa_step",
    )(ck, cv, q, k, v, bias_rows, sinks)


def _layer_norm(z, g, b):
    mu = jnp.mean(z, axis=1, keepdims=True)
    zc = z - mu
    var = jnp.mean(zc * zc, axis=1, keepdims=True)
    return zc * lax.rsqrt(var + LN_EPS) * g + b


def _pack_halves(x):
    w = x.shape[1] // 2
    lo = pltpu.bitcast(x[:, :w].astype(BF16).astype(F32), U32)
    hi = pltpu.bitcast(x[:, w:].astype(BF16).astype(F32), U32)
    return (lo >> 16) | (hi & jnp.uint32(0xFFFF0000))


def _unpack_halves(words):
    lo = pltpu.bitcast(words << 16, F32).astype(BF16)
    hi = pltpu.bitcast(words & jnp.uint32(0xFFFF0000), F32).astype(BF16)
    return lo, hi


def _to_token_tiles(ref, x):
    for q in range(x.shape[1] // LANES):
        ref[:, q, :] = x[:, q * LANES:(q + 1) * LANES]


def _merge_kernel(h_ref, om_ref, att_ref, x_ref, gm_ref, ga_ref, wo_ref, g1_ref, b1_ref, wr_ref, br_ref,
                  x1_ref, xp_ref, tk_ref, cnt_ref):
    @pl.when(pl.program_id(0) == 0)
    def _():
        cnt_ref[...] = jnp.zeros_like(cnt_ref)

    sub = min(MERGE_SUB, x_ref.shape[0])
    n_sub = x_ref.shape[0] // sub

    def mix_stage(r):
        rs = pl.ds(r * sub, sub)
        hm = h_ref[rs, :] * _sigmoid(om_ref[rs, :])
        ym = hm * lax.rsqrt(jnp.mean(hm * hm, axis=1, keepdims=True) + LN_EPS) * gm_ref[...]
        att = att_ref[rs, :]
        ya = att * lax.rsqrt(jnp.mean(att * att, axis=1, keepdims=True) + LN_EPS) * ga_ref[...]
        return (jnp.dot(ym.astype(BF16), wo_ref[0:M_WIDTH, :], preferred_element_type=F32)
                + jnp.dot(ya.astype(BF16), wo_ref[M_WIDTH:, :], preferred_element_type=F32))

    def norm_stage(r, mix):
        rs = pl.ds(r * sub, sub)
        x1 = _layer_norm(DN_ALPHA * x_ref[rs, :] + mix, g1_ref[...], b1_ref[...])
        x1_ref[rs, :] = x1
        _to_token_tiles(xp_ref.at[rs], _pack_halves(x1))
        return jnp.dot(x1.astype(BF16), wr_ref[...], preferred_element_type=F32) + br_ref[...]

    def route_stage(r, logits):
        lane = lax.broadcasted_iota(I32, logits.shape, 1).astype(F32)
        vals, idxs = [], []
        for _ in range(TOP_K):
            mx = jnp.max(logits, axis=1, keepdims=True)
            idx = jnp.min(jnp.where(logits == mx, lane, float(LANES)), axis=1, keepdims=True)
            vals.append(mx)
            idxs.append(idx)
            logits = jnp.where(lane == idx, 2.0 * NEG, logits)
        es = [jnp.exp(vk - vals[0]) for vk in vals]
        tot = es[0] + es[1] + es[2] + es[3]
        tk = jnp.zeros(logits.shape, F32)
        picked = jnp.zeros(logits.shape, F32)
        for k in range(TOP_K):
            tk = jnp.where(lane == float(k), es[k] / tot, tk)
            tk = jnp.where(lane == float(TOP_K + k), idxs[k], tk)
            picked = jnp.where(lane == idxs[k], 1.0, picked)
        tk_ref[pl.ds(r * sub, sub), :] = tk
        return jnp.sum(picked, axis=0, keepdims=True)

    mixes, logit_q = {}, {}
    counts = jnp.zeros((1, LANES), F32)
    for i in range(n_sub + 2):
        if i < n_sub:
            mixes[i] = mix_stage(i)
        if 0 <= i - 1 < n_sub:
            logit_q[i - 1] = norm_stage(i - 1, mixes.pop(i - 1))
        if 0 <= i - 2 < n_sub:
            counts = counts + route_stage(i - 2, logit_q.pop(i - 2))
    cnt_ref[...] = cnt_ref[...] + counts


def _merge(h, om, att, x, gm, ga, wo, g1, b1, wr, br, tile, name):
    t = x.shape[0]
    rows = lambda w: pl.BlockSpec((tile, w), lambda i: (i, 0))
    const = lambda a: pl.BlockSpec(a.shape, lambda i: (0, 0))
    return pl.pallas_call(
        _merge_kernel,
        out_shape=[jax.ShapeDtypeStruct((t, D_MODEL), F32), jax.ShapeDtypeStruct((t, XP_TILE, LANES), U32),
                   jax.ShapeDtypeStruct((t, LANES), F32), jax.ShapeDtypeStruct((8, LANES), F32)],
        grid=(t // tile,),
        in_specs=[rows(M_WIDTH), rows(M_WIDTH), rows(A_WIDTH), rows(D_MODEL), const(gm), const(ga), const(wo),
                  const(g1), const(b1), const(wr), const(br)],
        out_specs=[rows(D_MODEL), pl.BlockSpec((tile, XP_TILE, LANES), lambda i: (i, 0, 0)), rows(LANES),
                   pl.BlockSpec((8, LANES), lambda i: (0, 0))],
        compiler_params=_cparams("arbitrary"), name=name,
    )(h, om, att, x, gm, ga, wo, g1, b1, wr, br)


def _route_kernel(tk_ref, first_ref, strict_ref, dest_ref, next_scr):
    @pl.when(pl.program_id(0) == 0)
    def _():
        next_scr[...] = first_ref[...]

    tk = tk_ref[...]
    lane = lax.broadcasted_iota(I32, tk.shape, 1).astype(F32)
    onehots = [jnp.where(lane == tk[:, TOP_K + k:TOP_K + k + 1], 1.0, 0.0) for k in range(TOP_K)]
    tot = onehots[0] + onehots[1] + onehots[2] + onehots[3]
    row = jnp.dot(strict_ref[...], tot.astype(BF16), preferred_element_type=F32) + next_scr[0:1, :]
    out = jnp.zeros(tk.shape, F32)
    for k in range(TOP_K):
        out = jnp.where(lane == float(k), jnp.sum(onehots[k] * row, axis=1, keepdims=True), out)
    dest_ref[...] = out.astype(I32)
    next_scr[...] = next_scr[...] + jnp.sum(tot, axis=0, keepdims=True)


def _route(tk, first):
    t = tk.shape[0]
    tile = min(RANK_TILE, t)
    strict = jnp.asarray(np.tril(np.ones((tile, tile), np.float32), -1), BF16)
    return pl.pallas_call(
        _route_kernel, out_shape=jax.ShapeDtypeStruct((t, LANES), I32), grid=(t // tile,),
        in_specs=[pl.BlockSpec((tile, LANES), lambda i: (i, 0)), pl.BlockSpec((8, LANES), lambda i: (0, 0)),
                  pl.BlockSpec((tile, tile), lambda i: (0, 0))],
        out_specs=pl.BlockSpec((tile, LANES), lambda i: (i, 0)),
        scratch_shapes=[pltpu.VMEM((8, LANES), F32)],
        compiler_params=_cparams("arbitrary"), name="moe_route",
    )(tk, first, strict)


def _offsets_kernel(cnt_ref, off_ref, be_ref, nu_ref, pad_ref, *, tile):
    cnt = cnt_ref[...]
    nblk = jnp.floor((cnt + float(tile - 1)) * (1.0 / tile))
    r_i = lax.broadcasted_iota(I32, (LANES, LANES), 0)
    c_i = lax.broadcasted_iota(I32, (LANES, LANES), 1)
    incl = jnp.where(r_i <= c_i, 1.0, 0.0).astype(BF16)
    cum = jnp.dot(nblk.astype(BF16), incl, preferred_element_type=F32)
    off = (cum - nblk) * float(tile)
    off_ref[...] = off
    which = lax.broadcasted_iota(I32, cnt.shape, 0)
    pad_ref[...] = jnp.where(which == 0, off + cnt, jnp.where(which == 1, nblk * float(tile) - cnt, 0.0)).astype(I32)
    rows = be_ref.shape[0]
    jb = (lax.broadcasted_iota(I32, (rows, LANES), 0) * LANES + lax.broadcasted_iota(I32, (rows, LANES), 1)).astype(F32)
    acc = jnp.zeros((rows, LANES), F32)
    for e in range(N_EXPERTS):
        acc = acc + jnp.where(jb >= cum[0:1, e:e + 1], 1.0, 0.0)
    be_ref[...] = jnp.minimum(acc, float(N_EXPERTS - 1)).astype(I32)
    nu_ref[...] = jnp.broadcast_to(cum[0:1, N_EXPERTS - 1:N_EXPERTS], nu_ref.shape).astype(I32)


def _offsets(cnt, n_blocks, tile):
    rows = -(-n_blocks // LANES)
    rows = -(-rows // 8) * 8
    return pl.pallas_call(
        functools.partial(_offsets_kernel, tile=tile),
        out_shape=[jax.ShapeDtypeStruct((8, LANES), F32), jax.ShapeDtypeStruct((rows, LANES), I32),
                   jax.ShapeDtypeStruct((8, LANES), I32), jax.ShapeDtypeStruct((8, LANES), I32)],
        name="moe_offsets",
    )(cnt)


def _scatter_rows(dest_ref, xp_ref, xs_ref, sem):
    t = xp_ref.shape[0]

    def row_copy(tok, dst):
        return pltpu.make_async_copy(xp_ref.at[pl.ds(tok, 1)], xs_ref.at[pl.ds(dst, 1)], sem)

    def issue(grp, carry):
        base = pl.multiple_of(grp * ISSUE_GROUP, ISSUE_GROUP)
        for u in range(ISSUE_GROUP):
            for k in range(TOP_K):
                row_copy(base + u, dest_ref[(base + u) * TOP_K + k]).start(priority=k % 2)
        return carry

    lax.fori_loop(0, t // ISSUE_GROUP, issue, 0)
    for k in range(TOP_K):
        pltpu.make_async_copy(xp_ref, xs_ref.at[pl.ds(0, t)], sem).wait()


def _sc_scatter(rows, idx, n_out):
    t = rows.shape[0]
    n_workers = SC_CORES * SC_SUBCORES
    assert idx.shape[0] == TOP_K * t and t % (n_workers * SC_CHUNK) == 0
    per_worker = t // n_workers
    mesh = plsc.VectorSubcoreMesh(core_axis_name="c", subcore_axis_name="s", num_cores=SC_CORES,
                                  num_subcores=SC_SUBCORES)

    nbuf, chunk = SC_SCATTER_BUFS, SC_CHUNK // SC_SCATTER_BUFS
    assert per_worker % (nbuf * chunk) == 0

    @functools.partial(
        pl.kernel, mesh=mesh, out_type=jax.ShapeDtypeStruct((n_out,) + rows.shape[1:], rows.dtype),
        scratch_types=[pltpu.VMEM((chunk,), I32) for _ in range(nbuf * TOP_K)]
                      + [pltpu.VMEM((chunk,) + rows.shape[1:], rows.dtype) for _ in range(nbuf)]
                      + [pltpu.SemaphoreType.DMA for _ in range(nbuf + 1)])
    def scatter(rows_hbm, idx_hbm, out_hbm, *scratch):
        idx_v = [scratch[u * TOP_K:(u + 1) * TOP_K] for u in range(nbuf)]
        rows_v = scratch[nbuf * TOP_K:nbuf * TOP_K + nbuf]
        lsems, ssem = scratch[nbuf * TOP_K + nbuf:-1], scratch[-1]
        wid = lax.axis_index("s") * SC_CORES + lax.axis_index("c")
        base = wid * per_worker

        @pl.loop(0, per_worker // (nbuf * chunk))
        def _(j):
            loads = []
            for u in range(nbuf):
                off = pl.multiple_of(base + (j * nbuf + u) * chunk, chunk)
                loads.append([pltpu.async_copy(rows_hbm.at[pl.ds(off, chunk)], rows_v[u], lsems[u])]
                             + [pltpu.async_copy(idx_hbm.at[pl.ds(pl.multiple_of(k * t + off, chunk), chunk)],
                                                 idx_v[u][k], lsems[u]) for k in range(TOP_K)])
            scatters = []
            for u in range(nbuf):
                for cp in loads[u]:
                    cp.wait()
                scatters += [pltpu.async_copy(rows_v[u], out_hbm.at[idx_v[u][k]], ssem) for k in range(TOP_K)]
            for cp in scatters:
                cp.wait()

    return scatter(rows, idx)


def _dispatch_rest_kernel(pads_ref, dest2_ref, xp2_ref, xs_in_ref, xs_ref, sem, zsem, zbuf, *, block_rows):
    del xs_in_ref
    zr = zbuf.shape[0]
    n_blocks = xs_ref.shape[0] // block_rows
    zbuf[...] = jnp.zeros_like(zbuf)
    used = pads_ref[2 * N_EXPERTS]

    def pieces(e, act):
        start, n = pads_ref[e], pads_ref[N_EXPERTS + e]
        for sh in range(zr.bit_length() - 1, -1, -1):
            b = 1 << sh
            before = lax.shift_left(lax.shift_right_logical(n, sh + 1), sh + 1)

            @pl.when((n & b) != 0)
            def _():
                act(pltpu.make_async_copy(zbuf.at[pl.ds(0, b)], xs_ref.at[pl.ds(start + before, b)], zsem))

    def tail(jb, act):
        for h in range(block_rows // zr):
            act(pltpu.make_async_copy(zbuf, xs_ref.at[pl.ds(jb * block_rows + h * zr, zr)], zsem))

    for act in (lambda cp: cp.start(), lambda cp: cp.wait()):
        lax.fori_loop(0, N_EXPERTS, lambda e, c: (pieces(e, act), c)[1], 0)
        lax.fori_loop(used, n_blocks, lambda jb, c: (tail(jb, act), c)[1], 0)

    _scatter_rows(dest2_ref, xp2_ref, xs_ref, sem)


def _dispatch_rest(pads, dest2, xp2, xs, block_rows):
    return pl.pallas_call(
        functools.partial(_dispatch_rest_kernel, block_rows=block_rows),
        out_shape=jax.ShapeDtypeStruct(xs.shape, xs.dtype),
        in_specs=[pl.BlockSpec(memory_space=pltpu.SMEM), pl.BlockSpec(memory_space=pltpu.SMEM),
                  pl.BlockSpec(memory_space=pltpu.VMEM), pl.BlockSpec(memory_space=pl.ANY)],
        out_specs=pl.BlockSpec(memory_space=pl.ANY),
        scratch_shapes=[pltpu.SemaphoreType.DMA(()), pltpu.SemaphoreType.DMA(()),
                        pltpu.VMEM((EXPERT_TILE // 2,) + xp2.shape[1:], xp2.dtype)],
        input_output_aliases={3: 0},
        compiler_params=_cparams(), name="moe_dispatch_rest",
    )(pads, dest2, xp2, xs)


def _expert_kernel(be_ref, nu_ref, xs_ref, w1_ref, b1g_ref, b1l_ref, w2_ref, b2_ref, perm_ref, ys_ref,
                   w1g_scr, w1l_scr, w2_scr, xq_scr, y_scr, sem, osem):
    j = pl.program_id(0)
    active = j < nu_ref[0]
    changed = jnp.logical_or(j == 0, be_ref[j] != be_ref[jnp.maximum(j - 1, 0)])
    tm = y_scr.shape[0]
    slot = lax.rem(j, 2)

    def fetch(blk, slot):
        row0 = pl.multiple_of(blk * tm, tm)
        return [pltpu.make_async_copy(xs_ref.at[pl.ds(row0, tm), q, :], xq_scr.at[slot, q], sem.at[slot])
                for q in range(XP_TILE)]

    def put(blk):
        row0 = pl.multiple_of(blk * tm, tm)
        return [pltpu.make_async_copy(y_scr.at[:, q * LANES:(q + 1) * LANES], ys_ref.at[pl.ds(row0, tm), q, :], osem)
                for q in range(YS_TILE)]

    def emit(y):
        @pl.when(j > 0)
        def _():
            for cp in put(j - 1):
                cp.wait()

        y_scr[...] = _pack_halves(y)
        for cp in put(j):
            cp.start()

    @pl.when(j == 0)
    def _():
        for cp in fetch(0, 0):
            cp.start()

    @pl.when(j + 1 < nu_ref[0])
    def _():
        for cp in fetch(j + 1, 1 - slot):
            cp.start()

    @pl.when(jnp.logical_and(active, changed))
    def _():
        for c in range(2 * D_FF // 256):
            wc = w1_ref[0, :, c * 256:(c + 1) * 256].astype(BF16)
            d = jnp.dot(wc, perm_ref[...], preferred_element_type=F32).astype(BF16)
            w1g_scr[:, c * 128:(c + 1) * 128] = d[:, :128]
            w1l_scr[:, c * 128:(c + 1) * 128] = d[:, 128:]
        for c in range(D_FF // 256):
            w2_scr[c * 256:(c + 1) * 256, :] = w2_ref[0, c * 256:(c + 1) * 256, :].astype(BF16)

    @pl.when(active)
    def _():
        for cp in fetch(j, slot):
            cp.wait()
        lo, hi = _unpack_halves(jnp.concatenate([xq_scr[slot, q] for q in range(XP_TILE)], axis=1))
        xb = jnp.concatenate([lo, hi], axis=1)
        hg = jnp.dot(xb, w1g_scr[...], preferred_element_type=F32) + b1g_ref[0]
        hl = jnp.dot(xb, w1l_scr[...], preferred_element_type=F32) + b1l_ref[0]
        x_glu = jnp.minimum(hg, SWIGLU_LIMIT)
        x_lin = jnp.clip(hl, -SWIGLU_LIMIT, SWIGLU_LIMIT)
        a = x_glu * _sigmoid(SWIGLU_ALPHA * x_glu) * (x_lin + 1.0)
        emit(jnp.dot(a.astype(BF16), w2_scr[...], preferred_element_type=F32) + b2_ref[0])

    @pl.when(jnp.logical_not(active))
    def _():
        emit(jnp.zeros((tm, D_MODEL), F32))

    @pl.when(j == pl.num_programs(0) - 1)
    def _():
        for cp in put(j):
            cp.wait()


def _experts(be, nu, xs, w1, b1g, b1l, w2, b2, perm, tile):
    n_blocks = xs.shape[0] // tile
    grid_spec = pltpu.PrefetchScalarGridSpec(
        num_scalar_prefetch=2, grid=(n_blocks,),
        in_specs=[pl.BlockSpec(memory_space=pl.ANY),
                  pl.BlockSpec((1, D_MODEL, 2 * D_FF), lambda j, be, nu: (be[j], 0, 0)),
                  pl.BlockSpec((1, 1, D_FF), lambda j, be, nu: (be[j], 0, 0)),
                  pl.BlockSpec((1, 1, D_FF), lambda j, be, nu: (be[j], 0, 0)),
                  pl.BlockSpec((1, D_FF, D_MODEL), lambda j, be, nu: (be[j], 0, 0)),
                  pl.BlockSpec((1, 1, D_MODEL), lambda j, be, nu: (be[j], 0, 0)),
                  pl.BlockSpec((256, 256), lambda j, be, nu: (0, 0))],
        out_specs=pl.BlockSpec(memory_space=pl.ANY),
        scratch_shapes=[pltpu.VMEM((D_MODEL, D_FF), BF16), pltpu.VMEM((D_MODEL, D_FF), BF16),
                        pltpu.VMEM((D_FF, D_MODEL), BF16), pltpu.VMEM((2, XP_TILE, tile, LANES), U32),
                        pltpu.VMEM((tile, YS_TILE * LANES), U32), pltpu.SemaphoreType.DMA((2,)),
                        pltpu.SemaphoreType.DMA(())])
    return pl.pallas_call(
        _expert_kernel, out_shape=jax.ShapeDtypeStruct((xs.shape[0], YS_TILE, LANES), U32), grid_spec=grid_spec,
        compiler_params=_cparams("arbitrary"), name="moe_experts",
    )(be, nu, xs, w1, b1g, b1l, w2, b2, perm)


def _combine_kernel(dest_ref, next_ref, ys_ref, tk_ref, x1_ref, g2_ref, b2_ref, out_ref, buf, sem):
    i = pl.program_id(0)
    t = x1_ref.shape[0]
    slot = lax.rem(i, 2)

    def gather(idx_ref, s):
        def issue(grp, carry):
            base = pl.multiple_of(grp * 8, 8)
            for u in range(8):
                for k in range(TOP_K):
                    pltpu.make_async_copy(ys_ref.at[idx_ref[(base + u) * TOP_K + k]],
                                          buf.at[s, k, grp, :, u, :], sem.at[s]).start(priority=k % 2)
            return carry

        lax.fori_loop(0, t // 8, issue, 0)

    @pl.when(i == 0)
    def _():
        gather(dest_ref, 0)

    @pl.when(i + 1 < pl.num_programs(0))
    def _():
        gather(next_ref, 1 - slot)

    for k in range(TOP_K):
        for u in range(8):
            pltpu.make_async_copy(ys_ref.at[pl.ds(0, t // 8)], buf.at[slot, k, :, :, u, :], sem.at[slot]).wait()
    tk = tk_ref[...]
    los, his = [], []
    for q in range(YS_TILE):
        lo = hi = None
        for k in range(TOP_K):
            words = buf[slot, k, :, q].reshape(t, LANES)
            g = tk[:, k:k + 1]
            lo_k = g * pltpu.bitcast(words << 16, F32)
            hi_k = g * pltpu.bitcast(words & jnp.uint32(0xFFFF0000), F32)
            lo = lo_k if lo is None else lo + lo_k
            hi = hi_k if hi is None else hi + hi_k
        los.append(lo)
        his.append(hi)
    ff = jnp.concatenate(los + his, axis=1)
    out_ref[...] = _layer_norm(DN_ALPHA * x1_ref[...] + ff, g2_ref[...], b2_ref[...])


def _sc_gather(rows, idx):
    b = idx.shape[0]
    n_workers = SC_CORES * SC_SUBCORES
    assert b % (n_workers * SC_CHUNK) == 0
    per_worker = b // n_workers
    mesh = plsc.VectorSubcoreMesh(core_axis_name="c", subcore_axis_name="s", num_cores=SC_CORES,
                                  num_subcores=SC_SUBCORES)

    n_chunks = per_worker // SC_CHUNK

    @functools.partial(
        pl.kernel, mesh=mesh, out_type=jax.ShapeDtypeStruct((b,) + rows.shape[1:], rows.dtype),
        scratch_types=[pltpu.VMEM((n_chunks, SC_CHUNK), I32), pltpu.VMEM((SC_CHUNK,) + rows.shape[1:], rows.dtype),
                       pltpu.SemaphoreType.DMA])
    def gather(rows_hbm, idx_hbm, out_hbm, idx_v, rows_v, sem):
        wid = lax.axis_index("s") * SC_CORES + lax.axis_index("c")
        pltpu.sync_copy(idx_hbm.at[pl.ds(wid * n_chunks, n_chunks)], idx_v)

        @pl.loop(0, n_chunks)
        def _(j):
            off = pl.multiple_of((wid * n_chunks + j) * SC_CHUNK, SC_CHUNK)
            pltpu.async_copy(rows_hbm.at[idx_v.at[j]], rows_v, sem).wait()
            pltpu.sync_copy(rows_v, out_hbm.at[pl.ds(off, SC_CHUNK)])

    return gather(rows, idx.reshape(b // SC_CHUNK, SC_CHUNK))


def _combine_stream_kernel(g_ref, tk_ref, x1_ref, g2_ref, b2_ref, out_ref, buf, sem):
    i = pl.program_id(0)
    t = x1_ref.shape[0]
    slot = lax.rem(i, 2)

    def fetch(blk, s):
        row0 = pl.multiple_of(blk * t, t)
        return [pltpu.make_async_copy(g_ref.at[k, pl.ds(row0, t), q, :], buf.at[s, k, q], sem.at[s])
                for k in range(TOP_K) for q in range(YS_TILE)]

    @pl.when(i == 0)
    def _():
        for cp in fetch(0, 0):
            cp.start()

    @pl.when(i + 1 < pl.num_programs(0))
    def _():
        for cp in fetch(i + 1, 1 - slot):
            cp.start()

    for cp in fetch(i, slot):
        cp.wait()
    tk = tk_ref[...]
    los, his = [], []
    for q in range(YS_TILE):
        lo = hi = None
        for k in range(TOP_K):
            words = buf[slot, k, q]
            g = tk[:, k:k + 1]
            lo_k = g * pltpu.bitcast(words << 16, F32)
            hi_k = g * pltpu.bitcast(words & jnp.uint32(0xFFFF0000), F32)
            lo = lo_k if lo is None else lo + lo_k
            hi = hi_k if hi is None else hi + hi_k
        los.append(lo)
        his.append(hi)
    ff = jnp.concatenate(los + his, axis=1)
    out_ref[...] = _layer_norm(DN_ALPHA * x1_ref[...] + ff, g2_ref[...], b2_ref[...])


def _combine_stream(g4, tk, x1, g2, b2):
    t = x1.shape[0]
    tile = min(MERGE_TILE, t)
    rows = lambda w: pl.BlockSpec((tile, w), lambda i: (i, 0))
    return pl.pallas_call(
        _combine_stream_kernel, out_shape=jax.ShapeDtypeStruct((t, D_MODEL), F32), grid=(t // tile,),
        in_specs=[pl.BlockSpec(memory_space=pl.ANY), rows(LANES), rows(D_MODEL),
                  pl.BlockSpec((1, D_MODEL), lambda i: (0, 0)),
                  pl.BlockSpec((1, D_MODEL), lambda i: (0, 0))],
        out_specs=rows(D_MODEL),
        scratch_shapes=[pltpu.VMEM((2, TOP_K, YS_TILE, tile, LANES), U32), pltpu.SemaphoreType.DMA((2,))],
        compiler_params=_cparams("arbitrary"), name="moe_combine_stream",
    )(g4, tk, x1, g2, b2)


def _combine(dest_flat, ys, tk, x1, g2, b2):
    t = x1.shape[0]
    tile = min(ROW_TILE, t)
    n = t // tile
    return pl.pallas_call(
        _combine_kernel, out_shape=jax.ShapeDtypeStruct((t, D_MODEL), F32), grid=(n,),
        in_specs=[pl.BlockSpec((tile * TOP_K,), lambda i: (i,), memory_space=pltpu.SMEM),
                  pl.BlockSpec((tile * TOP_K,), lambda i: (jnp.minimum(i + 1, n - 1),), memory_space=pltpu.SMEM),
                  pl.BlockSpec(memory_space=pl.ANY),
                  pl.BlockSpec((tile, LANES), lambda i: (i, 0)),
                  pl.BlockSpec((tile, D_MODEL), lambda i: (i, 0)),
                  pl.BlockSpec((1, D_MODEL), lambda i: (0, 0)),
                  pl.BlockSpec((1, D_MODEL), lambda i: (0, 0))],
        out_specs=pl.BlockSpec((tile, D_MODEL), lambda i: (i, 0)),
        scratch_shapes=[pltpu.VMEM((2, TOP_K, tile // 8, YS_TILE, 8, LANES), U32), pltpu.SemaphoreType.DMA((2,))],
        compiler_params=_cparams("arbitrary"), name="moe_combine",
    )(dest_flat, dest_flat, ys, tk, x1, g2, b2)


def _rel_bucket(dist):
    exact = REL_BUCKETS // 2
    d = np.maximum(dist, 0)
    log_b = exact + (np.log(np.maximum(d, 1).astype(np.float32) / np.float32(exact))
                     / np.float32(math.log(REL_MAX_DIST / exact)) * np.float32(REL_BUCKETS - exact)).astype(np.int32)
    return np.where(d < exact, d, np.minimum(log_b, REL_BUCKETS - 1)).astype(np.int32)


def _bias_lookup(table, bucket, valid):
    bucket = jnp.asarray(bucket)[None]
    acc = jnp.zeros((table.shape[1],) + bucket.shape[1:], F32)
    for b in range(REL_BUCKETS):
        acc = jnp.where(bucket == b, table[b].reshape((-1,) + (1,) * (bucket.ndim - 1)), acc)
    return jnp.where(jnp.asarray(valid)[None], acc, NEG)


def _bias_tables(rel_bias):
    table = rel_bias.astype(F32)
    r = np.arange(WINDOW)[:, None]
    c = np.arange(2 * WINDOW)[None, :]
    dist = r + WINDOW - c
    valid = (dist >= 0) & (dist < WINDOW)
    dist0 = np.where(c < N_META, N_META + r - c, dist)
    valid0 = np.where(c < N_META, dist0 < WINDOW, (c >= WINDOW) & valid)
    both = jnp.stack([_bias_lookup(table, _rel_bucket(dist0), valid0), _bias_lookup(table, _rel_bucket(dist), valid)])
    dist_s = WINDOW - 1 - np.arange(WINDOW)
    rows = _bias_lookup(table[:, np.asarray(HEAD_ORDER)], _rel_bucket(dist_s), np.ones_like(dist_s, bool))
    return both, rows


def _perm_heads(a, axis):
    assert HEAD_ORDER == tuple(kv * A_GROUP + g for g in range(A_GROUP) for kv in range(A_KV_HEADS))
    shape = a.shape
    a = a.reshape(shape[:axis] + (A_KV_HEADS, A_GROUP, A_HD) + shape[axis + 1:])
    return jnp.swapaxes(a, axis, axis + 1).reshape(shape)


def _rep_rows(vec, rows=8):
    out = jnp.zeros((rows, LANES), F32)
    return out.at[:vec.shape[0], :].set(jnp.broadcast_to(vec.astype(F32)[:, None], (vec.shape[0], LANES)))


def kernel(x_prompt, x_sample, cache_swa_k, cache_swa_v, state_mlstm_C, state_mlstm_n, state_mlstm_m, meta_tokens, rel_bias, w_in, b_igate, b_fgate, attn_sinks, g_mlstm_out, g_attn_out, w_out, ln1_g, ln1_b, w_router, b_router, w_moe1, b_moe1, w_moe2, b_moe2, ln2_g, ln2_b):
    B, S, _ = x_prompt.shape
    NB = x_sample.shape[0]
    assert x_sample.shape[1] == 1 and w_in.shape[0] == 1
    assert S % PROJ_TILE == 0 and S % M_CHUNK == 0 and S % WINDOW == 0 and NB % SAMPLE_GROUP == 0
    l = 0

    assert IN_WIDTHS == (512, 512, 512, 512, 4, 4, 512, 128, 128)
    bf = lambda a: a.astype(BF16)
    w = w_in[l]
    n_main, n_gate = 4 * M_WIDTH, 2 * M_HEADS
    w_gate = w[:, n_main:n_main + n_gate]
    w_att = w[:, n_main + n_gate:]
    assert math.frexp(A_HD ** -0.5)[0] == 0.5
    w_qa = _perm_heads(w_att[:, :A_WIDTH], 1) * (A_HD ** -0.5)
    wr = bf(jnp.concatenate([w[:, :n_main], w_qa, w_att[:, A_WIDTH:],
                             jnp.pad(w_gate, ((0, 0), (0, LANES - n_gate)))], axis=1))
    wt = bf(jnp.concatenate([w[:, :M_WIDTH], w[:, 2 * M_WIDTH:3 * M_WIDTH], w_qa, w_att[:, A_WIDTH + LANES:],
                             w_gate], axis=1).T)
    b_gate = jnp.concatenate([b_igate[l], b_fgate[l]]).astype(F32)
    brow = jnp.pad(b_gate, (0, LANES - n_gate))[None, :]
    bcol = b_gate[:, None]
    plan_p = ((512, 512, "plain", BF16), (1536, 512, "plain", F32), (2560, 128, "plain", BF16),
              (2816, 128, "gate", F32))
    tplan_p = ((0, 512, "plain", BF16), (512, 512, "plain", BF16), (1024, 512, "plain", BF16),
               (1536, 128, "plain", BF16), (1664, 8, "gate", F32))
    plan_s = ((0, 512, "plain", F32), (512, 512, "plain", F32), (1024, 512, "plain", F32), (1536, 512, "plain", F32),
              (2048, 512, "plain", F32), (2560, 128, "plain", F32), (2688, 128, "plain", F32), (2816, 128, "gate", F32))

    bias_tab, bias_rows = _bias_tables(rel_bias)
    sinks = _rep_rows(attn_sinks[l])
    sinks_step = _rep_rows(attn_sinks[l][np.asarray(HEAD_ORDER)])
    g_m = g_mlstm_out[l].astype(F32)[None, :]
    g_a = _perm_heads(g_attn_out[l].astype(F32), 0)[None, :]
    wo = bf(jnp.concatenate([w_out[l][:M_WIDTH], _perm_heads(w_out[l][M_WIDTH:], 0)], axis=0))
    g1, b1 = ln1_g[l].astype(F32)[None, :], ln1_b[l].astype(F32)[None, :]
    g2, b2 = ln2_g[l].astype(F32)[None, :], ln2_b[l].astype(F32)[None, :]
    w_r = bf(jnp.pad(w_router[l], ((0, 0), (0, LANES - N_EXPERTS))))
    b_r = jnp.pad(b_router[l].astype(F32), (0, LANES - N_EXPERTS), constant_values=NEG)[None, :]
    b1g = b_moe1[l][:, 0::2].astype(F32)[:, None, :]
    b1l = b_moe1[l][:, 1::2].astype(F32)[:, None, :]
    b2e = b_moe2[l].astype(F32)[:, None, :]
    pj = np.zeros((256, 256), np.float32)
    pj[2 * np.arange(128), np.arange(128)] = 1.0
    pj[2 * np.arange(128) + 1, 128 + np.arange(128)] = 1.0
    perm = jnp.asarray(pj, BF16)

    xp2 = x_prompt.reshape(B * S, D_MODEL)
    km, om, ka, gc, qt, vt, qat, vat, gr, kv_tail = _proj(
        xp2, wr, wt, brow, bcol, plan_p, tplan_p, (2560, 256), PROJ_TILE, S, "proj_prompt")
    x_meta = jnp.pad(meta_tokens.astype(F32), ((0, M_CHUNK - N_META), (0, 0)))
    km0, _, ka0, gc0, qt0, vt0, _, vat0, gr0 = _proj(
        x_meta, wr, wt, brow, bcol, plan_p, tplan_p, None, M_CHUNK, M_CHUNK, "proj_meta")
    xs2 = x_sample.reshape(NB, D_MODEL)
    qm_s, km_s, vm_s, om_s, qa_s, ka_s, va_s, gc_s = _proj(
        xs2, wr, wt, brow, bcol, plan_s, (), None, NB, NB, "proj_sample")

    zero_c = jnp.zeros((M_HEADS, M_DV + 8, M_DK), F32)
    zero_m = jnp.zeros((8, LANES), F32)
    _, c_meta, m_meta = _mlstm(qt0, km0, vt0, gc0, gr0, zero_c, zero_m, 1, N_META, "mlstm_meta")
    h_p, c_p, m_p = _mlstm(qt, km, vt, gc, gr, c_meta[0], m_meta[0], B, M_CHUNK, "mlstm_prompt")
    C_p = c_p[:, :, :M_DV, :]
    n_p = c_p[:, :, M_DV, :]
    m_prompt = m_p[:, :M_HEADS, 0]
    m_pad = jnp.pad(state_mlstm_m[l].astype(F32), ((0, 0), (0, LANES - M_HEADS)))
    C_s, n_s, m_s, h_s = _mlstm_step(state_mlstm_C[l].astype(F32), state_mlstm_n[l].astype(F32), m_pad,
                                     gc_s, qm_s, km_s, vm_s)

    att_p = _swa(qat, ka, vat, ka0, vat0, jnp.swapaxes(bias_tab, 2, 3), sinks, B)
    ck = cache_swa_k[l].reshape(NB, WINDOW, LANES)
    cv = cache_swa_v[l].reshape(NB, WINDOW, LANES)
    k_new, v_new, att_s = _swa_step(ck, cv, qa_s, ka_s, va_s, bias_rows, sinks_step)

    x1_p, xpk_p, tk_p, cnt_p = _merge(h_p, om, att_p, xp2, g_m, g_a, wo, g1, b1, w_r, b_r, MERGE_TILE, "merge_prompt")
    x1_s, xpk_s, tk_s, cnt_s = _merge(h_s, om_s, att_s, xs2, g_m, g_a, wo, g1, b1, w_r, b_r, NB, "merge_sample")

    T_p = B * S
    assert T_p % RANK_TILE == 0 and T_p % ROW_TILE == 0
    n_blocks = -(-((T_p + NB) * TOP_K) // EXPERT_TILE) + N_EXPERTS
    off, be2, nu2, pad = _offsets(cnt_p + cnt_s, n_blocks, EXPERT_TILE)
    pads = jnp.concatenate([pad[0, :N_EXPERTS], pad[1, :N_EXPERTS], nu2[0, :1]])
    dest_p = _route(tk_p, off)[:, :TOP_K].reshape(-1)
    dest_s = _route(tk_s, off + cnt_p)[:, :TOP_K].reshape(-1)
    be = be2.reshape(-1)[:n_blocks]
    nu = nu2[0, :1]
    idx_p = dest_p.reshape(T_p, TOP_K).T.reshape(-1)
    xs = _sc_scatter(xpk_p, idx_p, n_blocks * EXPERT_TILE)
    xs = _dispatch_rest(pads, dest_s, xpk_s, xs, EXPERT_TILE)
    ys = _experts(be, nu, xs, w_moe1[l], b1g, b1l, w_moe2[l], b2e, perm, EXPERT_TILE)
    g4 = _sc_gather(ys, idx_p).reshape(TOP_K, T_p, YS_TILE, LANES)
    y_p = _combine_stream(g4, tk_p, x1_p, g2, b2)
    y_s = _combine(dest_s, ys, tk_s, x1_s, g2, b2)

    kv_tail = kv_tail.reshape(B, WINDOW, 2, A_KV_HEADS, A_HD)
    dt_k, dt_v = cache_swa_k.dtype, cache_swa_v.dtype
    return (y_p.reshape(B, S, D_MODEL).astype(x_prompt.dtype), y_s.reshape(NB, 1, D_MODEL).astype(x_sample.dtype),
            kv_tail[:, :, 0][None].astype(dt_k), kv_tail[:, :, 1][None].astype(dt_v),
            C_p[None].astype(state_mlstm_C.dtype), n_p[None].astype(state_mlstm_n.dtype),
            m_prompt[None].astype(state_mlstm_m.dtype),
            k_new.reshape(1, NB, WINDOW, A_KV_HEADS, A_HD).astype(dt_k),
            v_new.reshape(1, NB, WINDOW, A_KV_HEADS, A_HD).astype(dt_v),
            C_s[None].astype(state_mlstm_C.dtype), n_s[None].astype(state_mlstm_n.dtype),
            m_s[:, :M_HEADS][None].astype(state_mlstm_m.dtype))
```
